```python
import math
import jax
import jax.numpy as jnp
from jax import lax
import numpy as np

D_MODEL = 1024
BATCH = 4
SEQ = 8192
DEPTH = 2

A_WIDTH = 512
A_CONV = 3
B_WIDTH = 512
B_CONV = 31
C_HEADS = 8
C_HEAD_DIM = 64
IDX_HEADS = 8
IDX_DIM = 32
TOPK_MAX = 256
DIFF_HEADS = 4
DIFF_HEAD_DIM = 64
NUM_BUCKETS = 32
MAX_DISTANCE = 128
N_ATTN_HEADS = C_HEADS + DIFF_HEADS
N_GROUPS = 4
EXPERTS_PER_GROUP = 8
N_EXPERTS = N_GROUPS * EXPERTS_PER_GROUP
TOP_E = 2
D_EXPERT = 512
Q_BLOCK = 128
MOE_BLOCK = 128
EPS = 1e-6
N_EVEN = (DEPTH + 1) // 2
N_ODD = DEPTH // 2

AB_SIZES = (A_WIDTH, A_WIDTH, A_WIDTH, B_WIDTH, B_WIDTH)
AB_IN = 3 * A_WIDTH + 2 * B_WIDTH
C_WIDTH = C_HEADS * C_HEAD_DIM
DIFF_QK = DIFF_HEADS * 2 * DIFF_HEAD_DIM
DIFF_V = DIFF_HEADS * 2 * DIFF_HEAD_DIM
CD_SIZES = (C_WIDTH, C_WIDTH, C_WIDTH, IDX_HEADS * IDX_DIM, IDX_DIM, IDX_HEADS, DIFF_QK, DIFF_QK, DIFF_V)
CD_IN = 3 * C_WIDTH + IDX_HEADS * IDX_DIM + IDX_DIM + IDX_HEADS + 2 * DIFF_QK + DIFF_V

kernel_name = 'hybrid_conv_sparse_diff_hmoe_trunk'


def rmsnorm(x, g):
    xf = x.astype(jnp.float32)
    y = xf * lax.rsqrt(jnp.mean(xf * xf, axis=-1, keepdims=True) + EPS)
    return (y * g.astype(jnp.float32)).astype(x.dtype)


def layernorm(x, g, b):
    xf = x.astype(jnp.float32)
    mu = jnp.mean(xf, axis=-1, keepdims=True)
    var = jnp.mean(jnp.square(xf - mu), axis=-1, keepdims=True)
    y = (xf - mu) * lax.rsqrt(var + EPS) * g.astype(jnp.float32) + b.astype(jnp.float32)
    return y.astype(x.dtype)


def modulate(h, shift, scale):
    return h * (1 + scale[:, None, :]) + shift[:, None, :]


def split_cols(z, sizes):
    cuts = [int(v) for v in np.cumsum(sizes)[:-1]]
    return jnp.split(z, cuts, axis=-1)


def causal_depthwise_conv(u, w):
    width, ch = w.shape
    return lax.conv_general_dilated(u, w[:, None, :].astype(u.dtype), window_strides=(1,),
                                    padding=[(width - 1, 0)],
                                    dimension_numbers=('NWC', 'WIO', 'NWC'),
                                    feature_group_count=ch)


def rel_bucket(dist):
    max_exact = NUM_BUCKETS // 2
    n = jnp.maximum(dist, 0)
    log_ratio = jnp.log(jnp.maximum(n, 1).astype(jnp.float32) / max_exact) / math.log(MAX_DISTANCE / max_exact)
    large = max_exact + (log_ratio * (NUM_BUCKETS - max_exact)).astype(jnp.int32)
    return jnp.where(n < max_exact, n, jnp.minimum(large, NUM_BUCKETS - 1))


def conv_mixers(h, w_in, conv_a, conv_b, conv_b_bias, ln_g, ln_b, w_out):
    gate_b, gate_c, x_a, val_b, glu_gate = split_cols(h @ w_in, AB_SIZES)
    y_a = gate_b * causal_depthwise_conv(gate_c * x_a, conv_a)
    u = causal_depthwise_conv(val_b * jax.nn.sigmoid(glu_gate), conv_b) + conv_b_bias
    y_b = jax.nn.silu(layernorm(u, ln_g, ln_b))
    return jnp.concatenate([y_a, y_b], axis=-1) @ w_out


def attn_mixers(h, positions, w_in, rel_bias, diff_lam, diff_norm_g, w_out, lambda_init):
    bsz, seq, _ = h.shape
    n_blocks = seq // Q_BLOCK
    topk = min(TOPK_MAX, seq // 4)
    q_c, k_c, v_c, q_idx, k_idx, w_idx, q_d, k_d, v_d = split_cols(h @ w_in, CD_SIZES)
    q_c = q_c.reshape(bsz, seq, C_HEADS, C_HEAD_DIM)
    k_c = k_c.reshape(bsz, seq, C_HEADS, C_HEAD_DIM)
    v_c = v_c.reshape(bsz, seq, C_HEADS, C_HEAD_DIM)
    q_idx = q_idx.reshape(bsz, seq, IDX_HEADS, IDX_DIM)
    q_d = q_d.reshape(bsz, seq, DIFF_HEADS, 2, DIFF_HEAD_DIM)
    k_d = k_d.reshape(bsz, seq, DIFF_HEADS, 2, DIFF_HEAD_DIM)
    q1, q2 = q_d[..., 0, :], q_d[..., 1, :]
    k1, k2 = k_d[..., 0, :], k_d[..., 1, :]
    v_d = v_d.reshape(bsz, seq, DIFF_HEADS, 2 * DIFF_HEAD_DIM)
    lam_f = diff_lam.astype(jnp.float32)
    lam = jnp.exp(jnp.sum(lam_f[0] * lam_f[1])) - jnp.exp(jnp.sum(lam_f[2] * lam_f[3])) + lambda_init
    bias_c = rel_bias[:, :C_HEADS]
    bias_d = rel_bias[:, C_HEADS:]
    scale_c = C_HEAD_DIM ** -0.5
    scale_d = DIFF_HEAD_DIM ** -0.5
    scale_idx = (IDX_DIM * IDX_HEADS) ** -0.5
    gather_rows = jax.vmap(lambda a, i: a[i])

    def block(j):
        s0 = j * Q_BLOCK
        take = lambda a: lax.dynamic_slice_in_dim(a, s0, Q_BLOCK, axis=1)
        pos_q = take(positions)
        causal = positions[:, None, :] <= pos_q[:, :, None]
        rel = jnp.einsum('bqhd,bsd->bqhs', take(q_idx), k_idx)
        score = jnp.einsum('bqhs,bqh->bqs', jax.nn.relu(rel), take(w_idx)).astype(jnp.float32) * scale_idx
        score = jnp.where(causal, score, -jnp.inf)
        _, sel = lax.top_k(score, topk)
        k_sel = gather_rows(k_c, sel)
        v_sel = gather_rows(v_c, sel)
        pos_sel = gather_rows(positions, sel)
        dist_c = pos_q[:, :, None] - pos_sel
        logit_c = (jnp.einsum('bqhd,bqkhd->bhqk', take(q_c), k_sel).astype(jnp.float32) * scale_c
                   + jnp.moveaxis(bias_c[rel_bucket(dist_c)], -1, 1))
        logit_c = jnp.where((dist_c >= 0)[:, None], logit_c, -jnp.inf)
        p_c = jax.nn.softmax(logit_c, axis=-1).astype(v_sel.dtype)
        out_c = jnp.einsum('bhqk,bqkhd->bqhd', p_c, v_sel)
        dist_d = pos_q[:, :, None] - positions[:, None, :]
        b_d = jnp.moveaxis(bias_d[rel_bucket(dist_d)], -1, 1)
        mask_d = causal[:, None]

        def softmax_map(qb, kk):
            lg = jnp.einsum('bqhd,bshd->bhqs', qb, kk).astype(jnp.float32) * scale_d + b_d
            return jax.nn.softmax(jnp.where(mask_d, lg, -jnp.inf), axis=-1)

        attn = softmax_map(take(q1), k1) - lam * softmax_map(take(q2), k2)
        out_d = jnp.einsum('bhqs,bshd->bqhd', attn.astype(v_d.dtype), v_d)
        return out_c, out_d

    out_c, out_d = lax.map(block, jnp.arange(n_blocks))
    out_c = jnp.moveaxis(out_c, 0, 1).reshape(bsz, seq, C_WIDTH)
    out_d = jnp.moveaxis(out_d, 0, 1).reshape(bsz, seq, DIFF_HEADS, 2 * DIFF_HEAD_DIM)
    out_d = (rmsnorm(out_d, diff_norm_g) * (1 - lambda_init)).reshape(bsz, seq, DIFF_V)
    return jnp.concatenate([out_c, out_d], axis=-1) @ w_out


def hier_moe(h, wr_g, br_g, wr_e, br_e, w_gate, w_up, w_down):
    bsz, seq, d = h.shape
    n_tok = bsz * seq
    xf = h.reshape(n_tok, d)
    p_group = jax.nn.softmax((xf @ wr_g + br_g).astype(jnp.float32), axis=-1)
    p_top, g_sel = lax.top_k(p_group, 1)
    fine = (xf @ wr_e + br_e).astype(jnp.float32).reshape(n_tok, N_GROUPS, EXPERTS_PER_GROUP)
    fine = jnp.take_along_axis(fine, g_sel[:, :, None], axis=1)[:, 0]
    e_val, e_sel = lax.top_k(fine, TOP_E)
    gate = (p_top * jax.nn.softmax(e_val, axis=-1)).reshape(-1)
    expert_id = (g_sel * EXPERTS_PER_GROUP + e_sel).reshape(-1)
    token_id = jnp.repeat(jnp.arange(n_tok, dtype=jnp.int32), TOP_E)
    n_assign = n_tok * TOP_E
    order = jnp.argsort(expert_id)
    e_sorted = expert_id[order]
    counts = jnp.bincount(expert_id, length=N_EXPERTS)
    starts = jnp.cumsum(counts) - counts
    padded = (counts + MOE_BLOCK - 1) // MOE_BLOCK * MOE_BLOCK
    pad_end = jnp.cumsum(padded)
    pad_start = pad_end - padded
    dest = pad_start[e_sorted] + jnp.arange(n_assign, dtype=jnp.int32) - starts[e_sorted]
    n_blocks = -(-n_assign // MOE_BLOCK) + N_EXPERTS
    n_rows = n_blocks * MOE_BLOCK
    row_token = jnp.full((n_rows,), n_tok, jnp.int32).at[dest].set(token_id[order])
    row_gate = jnp.zeros((n_rows,), gate.dtype).at[dest].set(gate[order])
    block_expert = jnp.minimum(jnp.searchsorted(pad_end, jnp.arange(n_blocks, dtype=jnp.int32) * MOE_BLOCK, side='right'),
                               N_EXPERTS - 1)
    x_rows = jnp.concatenate([xf, jnp.zeros((1, d), xf.dtype)], axis=0)[row_token].reshape(n_blocks, MOE_BLOCK, d)

    def expert_block(args):
        xb, e = args
        return (jax.nn.silu(xb @ w_gate[e]) * (xb @ w_up[e])) @ w_down[e]

    y_rows = lax.map(expert_block, (x_rows, block_expert)).reshape(n_rows, d)
    y_rows = y_rows * row_gate[:, None].astype(y_rows.dtype)
    out = jnp.zeros((n_tok + 1, d), y_rows.dtype).at[row_token].add(y_rows)[:n_tok]
    return out.reshape(bsz, seq, d)


def setup_inputs(seed: int = 0) -> dict:
    key = jax.random.key(seed)
    ks = jax.random.split(key, 32)
    f32 = jnp.float32

    def nrm(k, shape, scale):
        return jax.random.normal(k, shape, f32) * scale

    return {
        'x': nrm(ks[0], (BATCH, SEQ, D_MODEL), 1.0),
        'c': nrm(ks[1], (BATCH, D_MODEL), 1.0),
        'positions': jnp.broadcast_to(jnp.arange(SEQ, dtype=jnp.int32), (BATCH, SEQ)),
        'rel_bias': nrm(ks[2], (NUM_BUCKETS, N_ATTN_HEADS), 0.5),
        'norm_g': 1.0 + nrm(ks[3], (DEPTH, 2, D_MODEL), 0.1),
        'final_norm_g': 1.0 + nrm(ks[4], (D_MODEL,), 0.1),
        'ada_w': nrm(ks[5], (DEPTH, D_MODEL, 6 * D_MODEL), 0.5 * D_MODEL ** -0.5),
        'ada_b': nrm(ks[6], (DEPTH, 6 * D_MODEL), 0.02),
        'ab_w_in': nrm(ks[7], (N_EVEN, D_MODEL, AB_IN), D_MODEL ** -0.5),
        'ab_conv_a': nrm(ks[8], (N_EVEN, A_CONV, A_WIDTH), A_CONV ** -0.5),
        'ab_conv_b': nrm(ks[9], (N_EVEN, B_CONV, B_WIDTH), B_CONV ** -0.5),
        'ab_conv_b_bias': nrm(ks[10], (N_EVEN, B_WIDTH), 0.02),
        'ab_ln_g': 1.0 + nrm(ks[11], (N_EVEN, B_WIDTH), 0.1),
        'ab_ln_b': nrm(ks[12], (N_EVEN, B_WIDTH), 0.02),
        'ab_w_out': nrm(ks[13], (N_EVEN, A_WIDTH + B_WIDTH, D_MODEL), (A_WIDTH + B_WIDTH) ** -0.5),
        'cd_w_in': nrm(ks[14], (N_ODD, D_MODEL, CD_IN), D_MODEL ** -0.5),
        'diff_lam': nrm(ks[15], (N_ODD, 4, DIFF_HEAD_DIM), 0.1),
        'diff_norm_g': 1.0 + nrm(ks[16], (N_ODD, 2 * DIFF_HEAD_DIM), 0.1),
        'cd_w_out': nrm(ks[17], (N_ODD, C_WIDTH + DIFF_V, D_MODEL), (C_WIDTH + DIFF_V) ** -0.5),
        'moe_wr_g': nrm(ks[18], (DEPTH, D_MODEL, N_GROUPS), D_MODEL ** -0.5),
        'moe_br_g': nrm(ks[19], (DEPTH, N_GROUPS), 0.01),
        'moe_wr_e': nrm(ks[20], (DEPTH, D_MODEL, N_EXPERTS), D_MODEL ** -0.5),
        'moe_br_e': nrm(ks[21], (DEPTH, N_EXPERTS), 0.01),
        'moe_w_gate': nrm(ks[22], (DEPTH, N_EXPERTS, D_MODEL, D_EXPERT), D_MODEL ** -0.5),
        'moe_w_up': nrm(ks[23], (DEPTH, N_EXPERTS, D_MODEL, D_EXPERT), D_MODEL ** -0.5),
        'moe_w_down': nrm(ks[24], (DEPTH, N_EXPERTS, D_EXPERT, D_MODEL), D_EXPERT ** -0.5),
    }


def reference(x, c, positions, rel_bias, norm_g, final_norm_g, ada_w, ada_b,
              ab_w_in, ab_conv_a, ab_conv_b, ab_conv_b_bias, ab_ln_g, ab_ln_b, ab_w_out,
              cd_w_in, diff_lam, diff_norm_g, cd_w_out,
              moe_wr_g, moe_br_g, moe_wr_e, moe_br_e, moe_w_gate, moe_w_up, moe_w_down):
    cond = jax.nn.silu(c)
    for i in range(DEPTH):
        mod = cond @ ada_w[i] + ada_b[i]
        sh1, sc1, g1, sh2, sc2, g2 = jnp.split(mod, 6, axis=-1)
        h = modulate(rmsnorm(x, norm_g[i, 0]), sh1, sc1)
        j = i // 2
        if i % 2 == 0:
            y = conv_mixers(h, ab_w_in[j], ab_conv_a[j], ab_conv_b[j], ab_conv_b_bias[j],
                            ab_ln_g[j], ab_ln_b[j], ab_w_out[j])
        else:
            lambda_init = 0.8 - 0.6 * math.exp(-0.3 * i)
            y = attn_mixers(h, positions, cd_w_in[j], rel_bias, diff_lam[j], diff_norm_g[j],
                            cd_w_out[j], lambda_init)
        x = x + g1[:, None, :] * y
        h = modulate(rmsnorm(x, norm_g[i, 1]), sh2, sc2)
        x = x + g2[:, None, :] * hier_moe(h, moe_wr_g[i], moe_br_g[i], moe_wr_e[i], moe_br_e[i],
                                          moe_w_gate[i], moe_w_up[i], moe_w_down[i])
    return rmsnorm(x, final_norm_g)
```

```python
import functools
import math

import numpy as np
import jax
import jax.numpy as jnp
from jax import lax
from jax.experimental import pallas as pl
from jax.experimental.pallas import tpu as pltpu

F32 = jnp.float32
BF16 = jnp.bfloat16
I32 = jnp.int32
HIGHEST = lax.Precision.HIGHEST

EPS = 1e-6
A_WIDTH = 512
A_CONV = 3
B_WIDTH = 512
B_CONV = 31
C_HEADS = 8
C_HEAD_DIM = 64
IDX_HEADS = 8
IDX_DIM = 32
TOPK_MAX = 256
DIFF_HEADS = 4
DIFF_HEAD_DIM = 64
NUM_BUCKETS = 32
MAX_DISTANCE = 128
N_GROUPS = 4
EXPERTS_PER_GROUP = 8
N_EXPERTS = N_GROUPS * EXPERTS_PER_GROUP
C_WIDTH = C_HEADS * C_HEAD_DIM
DIFF_W = DIFF_HEADS * 2 * DIFF_HEAD_DIM
IDX_COLS = 384
IDX_K_OFF = IDX_HEADS * IDX_DIM
IDX_W_OFF = IDX_K_OFF + IDX_DIM
LANES = 128
INT_MIN = -(2 ** 31)
VMEM_LIMIT = 56 * 1024 * 1024


def _bucket_starts():
    n = np.arange(0, 2 * MAX_DISTANCE)
    me = NUM_BUCKETS // 2
    lr = np.log(np.maximum(n, 1) / me) / math.log(MAX_DISTANCE / me)
    large = me + (lr * (NUM_BUCKETS - me)).astype(np.int64)
    b = np.where(n < me, n, np.minimum(large, NUM_BUCKETS - 1))
    return [int(n[b >= k].min()) for k in range(NUM_BUCKETS)]


BUCKET_START = _bucket_starts()


def _cparams(sem):
    return pltpu.CompilerParams(dimension_semantics=sem, vmem_limit_bytes=VMEM_LIMIT)


def _rms(x):
    return x * lax.rsqrt(jnp.mean(x * x, axis=-1, keepdims=True) + EPS)


def _sigmoid(x):
    return 1.0 / (1.0 + jnp.exp(-x))


def _ada_body(c_ref, w_ref, b_ref, o_ref):
    c = c_ref[...]
    cond = c * _sigmoid(c)
    o_ref[0] = jnp.dot(cond, w_ref[0], precision=HIGHEST, preferred_element_type=F32) + b_ref[0]


def _ada_mod(c, ada_w, ada_b):
    depth, d, n6 = ada_w.shape
    bsz = c.shape[0]
    rows = 8
    c_pad = jnp.zeros((rows, d), F32).at[:bsz].set(c)
    tn = 1536
    out = pl.pallas_call(
        _ada_body,
        grid=(depth, n6 // tn),
        in_specs=[pl.BlockSpec((rows, d), lambda i, j: (0, 0)),
                  pl.BlockSpec((1, d, tn), lambda i, j: (i, 0, j)),
                  pl.BlockSpec((1, 1, tn), lambda i, j: (i, 0, j))],
        out_specs=pl.BlockSpec((1, rows, tn), lambda i, j: (i, 0, j)),
        out_shape=jax.ShapeDtypeStruct((depth, rows, n6), F32),
        compiler_params=_cparams(("arbitrary", "arbitrary")),
        name="ada_mod",
    )(c_pad, ada_w, ada_b.reshape(depth, 1, n6))
    return out[:, :bsz]


def _norm_proj_body(x_ref, g_ref, sh_ref, sc_ref, w_ref, o_ref):
    y = _rms(x_ref[0]) * g_ref[...]
    h = y * (1.0 + sc_ref[0]) + sh_ref[0]
    o_ref[0] = jnp.dot(h.astype(BF16), w_ref[...], preferred_element_type=F32).astype(o_ref.dtype)


def _norm_proj(x, g, sh, sc, w_bf16, tm):
    bsz, seq, d = x.shape
    n = w_bf16.shape[1]
    return pl.pallas_call(
        _norm_proj_body,
        grid=(bsz, seq // tm),
        in_specs=[pl.BlockSpec((1, tm, d), lambda b, i: (b, i, 0)),
                  pl.BlockSpec((1, d), lambda b, i: (0, 0)),
                  pl.BlockSpec((1, 1, d), lambda b, i: (b, 0, 0)),
                  pl.BlockSpec((1, 1, d), lambda b, i: (b, 0, 0)),
                  pl.BlockSpec((d, n), lambda b, i: (0, 0))],
        out_specs=pl.BlockSpec((1, tm, n), lambda b, i: (b, i, 0)),
        out_shape=jax.ShapeDtypeStruct((bsz, seq, n), BF16),
        compiler_params=_cparams(("arbitrary", "arbitrary")),
        name="norm_proj",
    )(x, g.reshape(1, d), sh.reshape(bsz, 1, d), sc.reshape(bsz, 1, d), w_bf16)


CONV_HALO = 32
CONV_ROWS = 64


def _conv_body(z_ref, x_ref, g1_ref, ca_ref, cb_ref, cbb_ref, lng_ref, lnb_ref, wo_ref, o_ref,
               ua_scr, ub_scr, y_scr, *, tl):
    l = pl.program_id(1)

    @pl.when(l == 0)
    def _():
        ua_scr[0:CONV_HALO, :] = jnp.zeros((CONV_HALO, A_WIDTH), F32)
        ub_scr[0:CONV_HALO, :] = jnp.zeros((CONV_HALO, B_WIDTH), F32)

    a = A_WIDTH
    gate_c = z_ref[0, :, a:2 * a].astype(F32)
    x_a = z_ref[0, :, 2 * a:3 * a].astype(F32)
    ua_scr[CONV_HALO:CONV_HALO + tl, :] = gate_c * x_a
    val_b = z_ref[0, :, 3 * a:3 * a + B_WIDTH].astype(F32)
    glu = z_ref[0, :, 3 * a + B_WIDTH:3 * a + 2 * B_WIDTH].astype(F32)
    ub_scr[CONV_HALO:CONV_HALO + tl, :] = val_b * _sigmoid(glu)

    for r in range(0, tl, CONV_ROWS):
        acc_a = None
        for k in range(A_CONV):
            tap = ua_scr[CONV_HALO + r - (A_CONV - 1) + k:CONV_HALO + r - (A_CONV - 1) + k + CONV_ROWS, :]
            term = tap * ca_ref[k:k + 1, :]
            acc_a = term if acc_a is None else acc_a + term
        gate_b = z_ref[0, r:r + CONV_ROWS, 0:a].astype(F32)
        y_scr[r:r + CONV_ROWS, 0:a] = (gate_b * acc_a).astype(BF16)

        acc_b = None
        for k in range(B_CONV):
            tap = ub_scr[CONV_HALO + r - (B_CONV - 1) + k:CONV_HALO + r - (B_CONV - 1) + k + CONV_ROWS, :]
            term = tap * cb_ref[k:k + 1, :]
            acc_b = term if acc_b is None else acc_b + term
        u = acc_b + cbb_ref[...]
        mu = jnp.mean(u, axis=-1, keepdims=True)
        uc = u - mu
        var = jnp.mean(uc * uc, axis=-1, keepdims=True)
        v = uc * lax.rsqrt(var + EPS) * lng_ref[...] + lnb_ref[...]
        y_scr[r:r + CONV_ROWS, a:a + B_WIDTH] = (v * _sigmoid(v)).astype(BF16)

    ua_scr[0:CONV_HALO, :] = ua_scr[tl:tl + CONV_HALO, :]
    ub_scr[0:CONV_HALO, :] = ub_scr[tl:tl + CONV_HALO, :]
    y = jnp.dot(y_scr[...], wo_ref[...], preferred_element_type=F32)
    o_ref[0] = x_ref[0] + g1_ref[0] * y


def _conv_mix(z, x, g1, conv_a, conv_b, conv_b_bias, ln_g, ln_b, w_out_bf16, tl):
    bsz, seq, d = x.shape
    nz = z.shape[-1]
    wide = A_WIDTH + B_WIDTH
    full = lambda shape: pl.BlockSpec(shape, lambda b, l: (0,) * len(shape))
    return pl.pallas_call(
        functools.partial(_conv_body, tl=tl),
        grid=(bsz, seq // tl),
        in_specs=[pl.BlockSpec((1, tl, nz), lambda b, l: (b, l, 0)),
                  pl.BlockSpec((1, tl, d), lambda b, l: (b, l, 0)),
                  pl.BlockSpec((1, 1, d), lambda b, l: (b, 0, 0)),
                  full((A_CONV, A_WIDTH)), full((B_CONV, B_WIDTH)), full((1, B_WIDTH)),
                  full((1, B_WIDTH)), full((1, B_WIDTH)), full((wide, d))],
        out_specs=pl.BlockSpec((1, tl, d), lambda b, l: (b, l, 0)),
        out_shape=jax.ShapeDtypeStruct((bsz, seq, d), F32),
        scratch_shapes=[pltpu.VMEM((CONV_HALO + tl, A_WIDTH), F32),
                        pltpu.VMEM((CONV_HALO + tl, B_WIDTH), F32),
                        pltpu.VMEM((tl, wide), BF16)],
        compiler_params=_cparams(("arbitrary", "arbitrary")),
        name="conv_mix",
    )(z, x, g1.reshape(bsz, 1, d), conv_a, conv_b, conv_b_bias.reshape(1, -1), ln_g.reshape(1, -1),
      ln_b.reshape(1, -1), w_out_bf16)


MOE_ROWS = 256
ROUTER_ROWS = 512
COMBINE_ROWS = 256
META_GATE0, META_GATE1, META_E0, META_E1, META_RANK0, META_RANK1 = range(6)
GROUP_LANE0 = N_EXPERTS


def _router_body(x_ref, g_ref, sh_ref, sc_ref, wr_ref, br_ref, tri_ref, h_ref, meta_ref, cnt_ref, base_scr):
    @pl.when(pl.program_id(0) == 0)
    def _():
        base_scr[...] = jnp.zeros_like(base_scr)

    h = _rms(x_ref[...]) * g_ref[...]
    h = h * (1.0 + sc_ref[0]) + sh_ref[0]
    h_ref[...] = h
    logits = jnp.dot(h, wr_ref[...], precision=HIGHEST, preferred_element_type=F32) + br_ref[...]
    tr = logits.shape[0]
    lane = lax.broadcasted_iota(I32, (tr, LANES), 1)
    lane_f = lane.astype(F32)
    neg = jnp.float32(-jnp.inf)
    big = jnp.float32(1e9)

    is_group = (lane >= GROUP_LANE0) & (lane < GROUP_LANE0 + N_GROUPS)
    glog = jnp.where(is_group, logits, neg)
    gmax = jnp.max(glog, axis=1, keepdims=True)
    p_top = 1.0 / jnp.sum(jnp.exp(glog - gmax), axis=1, keepdims=True)
    g_sel = jnp.min(jnp.where(glog == gmax, lane_f, big), axis=1, keepdims=True) - GROUP_LANE0
    lo = g_sel * EXPERTS_PER_GROUP
    in_group = (lane_f >= lo) & (lane_f < lo + EXPERTS_PER_GROUP)
    f1 = jnp.where(in_group, logits, neg)
    v1 = jnp.max(f1, axis=1, keepdims=True)
    i1 = jnp.min(jnp.where(f1 == v1, lane_f, big), axis=1, keepdims=True)
    f2 = jnp.where(lane_f == i1, neg, f1)
    v2 = jnp.max(f2, axis=1, keepdims=True)
    i2 = jnp.min(jnp.where(f2 == v2, lane_f, big), axis=1, keepdims=True)
    a = jnp.exp(v2 - v1)
    w1 = 1.0 / (1.0 + a)
    gate0 = p_top * w1
    gate1 = p_top * (a * w1)

    oh0 = lane_f == i1
    oh1 = lane_f == i2
    ind0 = jnp.where(oh0, 1.0, 0.0)
    ind1 = jnp.where(oh1, 1.0, 0.0)
    pre0 = jnp.dot(tri_ref[...], ind0.astype(BF16), preferred_element_type=F32)
    pre1 = jnp.dot(tri_ref[...], ind1.astype(BF16), preferred_element_type=F32)
    tot0 = jnp.sum(ind0, axis=0, keepdims=True)
    tot1 = jnp.sum(ind1, axis=0, keepdims=True)
    base = base_scr[...]
    rank0 = jnp.sum(jnp.where(oh0, base + pre0, 0.0), axis=1, keepdims=True)
    rank1 = jnp.sum(jnp.where(oh1, base + tot0 + pre1, 0.0), axis=1, keepdims=True)
    new_base = base + tot0 + tot1
    base_scr[...] = new_base
    cnt_ref[...] = new_base

    meta = jnp.zeros((tr, LANES), F32)
    for col, val in ((META_GATE0, gate0), (META_GATE1, gate1), (META_E0, i1), (META_E1, i2),
                     (META_RANK0, rank0), (META_RANK1, rank1)):
        meta = jnp.where(lane == col, val, meta)
    meta_ref[...] = meta


def _moe_router(x2, g, sh, sc, wr_g, br_g, wr_e, br_e, seq):
    n_tok, d = x2.shape
    bsz = n_tok // seq
    tr = min(ROUTER_ROWS, seq)
    steps_per_batch = seq // tr
    wr = jnp.zeros((d, LANES), F32).at[:, :N_EXPERTS].set(wr_e).at[:, GROUP_LANE0:GROUP_LANE0 + N_GROUPS].set(wr_g)
    br = jnp.zeros((1, LANES), F32).at[0, :N_EXPERTS].set(br_e).at[0, GROUP_LANE0:GROUP_LANE0 + N_GROUPS].set(br_g)
    tri = jnp.tril(jnp.ones((tr, tr), BF16), -1)
    full = lambda shape: pl.BlockSpec(shape, lambda i: (0,) * len(shape))
    return pl.pallas_call(
        _router_body,
        grid=(n_tok // tr,),
        in_specs=[pl.BlockSpec((tr, d), lambda i: (i, 0)),
                  full((1, d)),
                  pl.BlockSpec((1, 1, d), lambda i: (i // steps_per_batch, 0, 0)),
                  pl.BlockSpec((1, 1, d), lambda i: (i // steps_per_batch, 0, 0)),
                  full((d, LANES)), full((1, LANES)), full((tr, tr))],
        out_specs=[pl.BlockSpec((tr, d), lambda i: (i, 0)),
                   pl.BlockSpec((tr, LANES), lambda i: (i, 0)),
                   full((1, LANES))],
        out_shape=[jax.ShapeDtypeStruct((n_tok, d), F32),
                   jax.ShapeDtypeStruct((n_tok, LANES), F32),
                   jax.ShapeDtypeStruct((1, LANES), F32)],
        scratch_shapes=[pltpu.VMEM((1, LANES), F32)],
        compiler_params=_cparams(("arbitrary",)),
        name="moe_router",
    )(x2, g.reshape(1, d), sh.reshape(bsz, 1, d), sc.reshape(bsz, 1, d), wr, br, tri)


def _dispatch_body(dest_ref, h_ref, xr_ref, sem, *, tr):
    def row_copy(t, k):
        return pltpu.make_async_copy(h_ref.at[pl.ds(t, 1)], xr_ref.at[pl.ds(dest_ref[0, 0, k * tr + t], 1)], sem)

    def start(t, carry):
        row_copy(t, 0).start()
        row_copy(t, 1).start()
        return carry

    lax.fori_loop(0, tr, start, 0)

    def wait(t, carry):
        row_copy(t, 0).wait()
        row_copy(t, 1).wait()
        return carry

    lax.fori_loop(0, tr, wait, 0)


def _moe_dispatch(h2, dest_blocks, tr):
    n_tok, d = h2.shape
    return pl.pallas_call(
        functools.partial(_dispatch_body, tr=tr),
        grid=(n_tok // tr,),
        in_specs=[pl.BlockSpec((1, 1, 2 * tr), lambda i: (i, 0, 0), memory_space=pltpu.SMEM),
                  pl.BlockSpec((tr, d), lambda i: (i, 0))],
        out_specs=pl.BlockSpec(memory_space=pl.ANY),
        out_shape=jax.ShapeDtypeStruct((2 * n_tok, d), F32),
        scratch_shapes=[pltpu.SemaphoreType.DMA(())],
        compiler_params=_cparams(("arbitrary",)),
        name="moe_dispatch",
    )(dest_blocks, h2)


def _expert_body(pb_ref, pe_ref, plo_ref, phi_ref, x_ref, wg_ref, wu_ref, wd_ref, o_ref, wg_s, wu_s, wd_s):
    p = pl.program_id(0)
    prev = jnp.maximum(p - 1, 0)
    new_expert = (p == 0) | (pe_ref[p] != pe_ref[prev])
    first = (p == 0) | (pb_ref[p] != pb_ref[prev])

    @pl.when(new_expert)
    def _():
        wg_s[...] = wg_ref[0].astype(BF16)
        wu_s[...] = wu_ref[0].astype(BF16)
        wd_s[...] = wd_ref[0].astype(BF16)

    x = x_ref[...].astype(BF16)
    gt = jnp.dot(x, wg_s[...], preferred_element_type=F32)
    up = jnp.dot(x, wu_s[...], preferred_element_type=F32)
    act = (gt * _sigmoid(gt)) * up
    y = jnp.dot(act.astype(BF16), wd_s[...], preferred_element_type=F32)
    rows = lax.broadcasted_iota(I32, (y.shape[0], 1), 0)
    y = jnp.where((rows >= plo_ref[p]) & (rows < phi_ref[p]), y, 0.0)

    @pl.when(first)
    def _():
        o_ref[...] = y

    @pl.when(jnp.logical_not(first))
    def _():
        o_ref[...] += y


def _moe_experts(x_rows, pairs, w_gate, w_up, w_down):
    n_rows, d = x_rows.shape
    de = w_gate.shape[-1]
    n_pairs = pairs[0].shape[0]
    grid_spec = pltpu.PrefetchScalarGridSpec(
        num_scalar_prefetch=4,
        grid=(n_pairs,),
        in_specs=[pl.BlockSpec((MOE_ROWS, d), lambda p, pb, pe, lo, hi: (pb[p], 0)),
                  pl.BlockSpec((1, d, de), lambda p, pb, pe, lo, hi: (pe[p], 0, 0)),
                  pl.BlockSpec((1, d, de), lambda p, pb, pe, lo, hi: (pe[p], 0, 0)),
                  pl.BlockSpec((1, de, d), lambda p, pb, pe, lo, hi: (pe[p], 0, 0))],
        out_specs=pl.BlockSpec((MOE_ROWS, d), lambda p, pb, pe, lo, hi: (pb[p], 0)),
        scratch_shapes=[pltpu.VMEM((d, de), BF16), pltpu.VMEM((d, de), BF16), pltpu.VMEM((de, d), BF16)],
    )
    return pl.pallas_call(
        _expert_body,
        grid_spec=grid_spec,
        out_shape=jax.ShapeDtypeStruct((n_rows, d), F32),
        compiler_params=_cparams(("arbitrary",)),
        name="moe_experts",
    )(*pairs, x_rows, w_gate, w_up, w_down)


def _combine_body(dest_ref, y_ref, x_ref, meta_ref, g2_ref, fg_ref, o_ref, r0, r1, sem, *, tc, final_norm):
    def row_copy(t, k, buf):
        return pltpu.make_async_copy(y_ref.at[pl.ds(dest_ref[0, 0, k * tc + t], 1)], buf.at[pl.ds(t, 1)], sem)

    def start(t, carry):
        row_copy(t, 0, r0).start()
        row_copy(t, 1, r1).start()
        return carry

    lax.fori_loop(0, tc, start, 0)

    def wait(t, carry):
        row_copy(t, 0, r0).wait()
        row_copy(t, 1, r1).wait()
        return carry

    lax.fori_loop(0, tc, wait, 0)
    meta = meta_ref[...]
    moe = meta[:, META_GATE0:META_GATE0 + 1] * r0[...] + meta[:, META_GATE1:META_GATE1 + 1] * r1[...]
    xn = x_ref[...] + g2_ref[0] * moe
    if final_norm:
        xn = _rms(xn) * fg_ref[...]
    o_ref[...] = xn


def _moe_combine(y_rows, dest_blocks, x2, meta, g2, final_g, seq, tc, final_norm):
    n_tok, d = x2.shape
    bsz = n_tok // seq
    steps_per_batch = seq // tc
    return pl.pallas_call(
        functools.partial(_combine_body, tc=tc, final_norm=final_norm),
        grid=(n_tok // tc,),
        in_specs=[pl.BlockSpec((1, 1, 2 * tc), lambda i: (i, 0, 0), memory_space=pltpu.SMEM),
                  pl.BlockSpec(memory_space=pl.ANY),
                  pl.BlockSpec((tc, d), lambda i: (i, 0)),
                  pl.BlockSpec((tc, LANES), lambda i: (i, 0)),
                  pl.BlockSpec((1, 1, d), lambda i: (i // steps_per_batch, 0, 0)),
                  pl.BlockSpec((1, d), lambda i: (0, 0))],
        out_specs=pl.BlockSpec((tc, d), lambda i: (i, 0)),
        out_shape=jax.ShapeDtypeStruct((n_tok, d), F32),
        scratch_shapes=[pltpu.VMEM((tc, d), F32), pltpu.VMEM((tc, d), F32), pltpu.SemaphoreType.DMA(())],
        compiler_params=_cparams(("arbitrary",)),
        name="moe_combine",
    )(dest_blocks, y_rows, x2, meta, g2.reshape(bsz, 1, d), final_g.reshape(1, d))


def _blocked_dest(dest0, dest1, rows):
    nb = dest0.shape[0] // rows
    return jnp.concatenate([dest0.reshape(nb, rows), dest1.reshape(nb, rows)], axis=1).reshape(nb, 1, 2 * rows)


def _expert_pairs(counts, n_rows):
    n_blocks = n_rows // MOE_ROWS
    n_pairs = n_blocks + N_EXPERTS
    ends = jnp.cumsum(counts)
    starts = ends - counts
    first_blk = starts // MOE_ROWS
    last_blk = (ends - 1) // MOE_ROWS
    npairs = jnp.where(counts > 0, last_blk - first_blk + 1, 0)
    pend = jnp.cumsum(npairs)
    poff = pend - npairs
    total = pend[-1]
    p = jnp.arange(n_pairs, dtype=I32)
    p_eff = jnp.minimum(p, total - 1)
    e = jnp.minimum(jnp.searchsorted(pend, p_eff, side='right'), N_EXPERTS - 1).astype(I32)
    blk = (first_blk[e] + p_eff - poff[e]).astype(I32)
    valid = p < total
    lo = jnp.where(valid, jnp.clip(starts[e] - blk * MOE_ROWS, 0, MOE_ROWS), 0).astype(I32)
    hi = jnp.where(valid, jnp.clip(ends[e] - blk * MOE_ROWS, 0, MOE_ROWS), 0).astype(I32)
    return blk, e, lo, hi


def _hier_moe(x, g, sh, sc, g2, wr_g, br_g, wr_e, br_e, w_gate, w_up, w_down, final_g, final_norm):
    bsz, seq, d = x.shape
    n_tok = bsz * seq
    x2 = x.reshape(n_tok, d)
    h2, meta, cnt = _moe_router(x2, g, sh, sc, wr_g, br_g, wr_e, br_e, seq)
    counts = cnt[0, :N_EXPERTS].astype(I32)
    starts = jnp.cumsum(counts) - counts
    e0 = meta[:, META_E0].astype(I32)
    e1 = meta[:, META_E1].astype(I32)
    dest0 = starts[e0] + meta[:, META_RANK0].astype(I32)
    dest1 = starts[e1] + meta[:, META_RANK1].astype(I32)
    tr = min(ROUTER_ROWS, seq)
    tc = min(COMBINE_ROWS, seq)
    x_rows = _moe_dispatch(h2, _blocked_dest(dest0, dest1, tr), tr)
    y_rows = _moe_experts(x_rows, _expert_pairs(counts, 2 * n_tok), w_gate, w_up, w_down)
    out = _moe_combine(y_rows, _blocked_dest(dest0, dest1, tc), x2, meta, g2, final_g, seq, tc, final_norm)
    return out.reshape(bsz, seq, d)


SEL_ROWS = 256
SEL_CHUNK = 512
ATT_TILE = 512
SEL_THR, SEL_IDXCAP = 0, 1


def _sortable(score):
    bits = lax.bitcast_convert_type(score, I32)
    return bits ^ ((bits >> 31) & jnp.int32(0x7FFFFFFF))


def _idx_heads(zidx):
    zf = zidx.astype(F32)
    qs = [zf[:, h * IDX_DIM:(h + 1) * IDX_DIM].astype(BF16) for h in range(IDX_HEADS)]
    scale = (IDX_DIM * IDX_HEADS) ** -0.5
    ws = [zf[:, IDX_W_OFF + h:IDX_W_OFF + h + 1] * scale for h in range(IDX_HEADS)]
    return qs, ws


def _idx_keys(zidx):
    return zidx.astype(F32)[:, IDX_K_OFF:IDX_K_OFF + IDX_DIM].astype(BF16)


def _idx_score(qs, ws, k):
    acc = None
    for q, w in zip(qs, ws):
        rel = lax.dot_general(q, k, (((1,), (1,)), ((), ())), preferred_element_type=F32)
        term = jnp.maximum(rel, 0.0) * w
        acc = term if acc is None else acc + term
    return acc


def _select_body(zq_ref, zk_ref, o_ref, keys_scr, *, topk):
    i = pl.program_id(1)
    tq = zq_ref.shape[1]
    qs, ws = _idx_heads(zq_ref[0])
    n_chunks = ((i + 1) * tq + SEL_CHUNK - 1) // SEL_CHUNK
    row = i * tq + lax.broadcasted_iota(I32, (tq, SEL_CHUNK), 0)
    col0 = lax.broadcasted_iota(I32, (tq, SEL_CHUNK), 1)
    int_min = jnp.int32(INT_MIN)

    def fill(c, carry):
        off = pl.multiple_of(c * SEL_CHUNK, SEL_CHUNK)
        k = _idx_keys(zk_ref[0, pl.ds(off, SEL_CHUNK), :])
        key = _sortable(_idx_score(qs, ws, k))
        keys_scr[:, pl.ds(off, SEL_CHUNK)] = jnp.where(col0 + off <= row, key, int_min)
        return carry

    lax.fori_loop(0, n_chunks, fill, 0)

    def count(pred):
        def body(c, acc):
            off = pl.multiple_of(c * SEL_CHUNK, SEL_CHUNK)
            ind = jnp.where(pred(keys_scr[:, pl.ds(off, SEL_CHUNK)], col0 + off), 1.0, 0.0)
            for j in range(SEL_CHUNK // LANES):
                acc = acc + ind[:, j * LANES:(j + 1) * LANES]
            return acc
        acc = lax.fori_loop(0, n_chunks, body, jnp.zeros((tq, LANES), F32))
        return jnp.sum(acc, axis=1, keepdims=True)

    k_f = jnp.float32(topk)

    def bit_step(ib, u):
        cand_u = u | jnp.left_shift(jnp.int32(1), 31 - ib)
        cand = cand_u ^ int_min
        cnt = count(lambda keys, col: keys >= cand)
        return jnp.where(cnt >= k_f, cand_u, u)

    u = lax.fori_loop(0, 32, bit_step, jnp.zeros((tq, 1), I32))
    thr = u ^ int_min
    thr = jnp.maximum(thr, int_min + 1)
    lane = lax.broadcasted_iota(I32, (tq, LANES), 1)
    cap_all = jnp.int32(2 ** 30)
    o_ref[0] = jnp.where(lane == SEL_THR, thr, cap_all)

    n_ge = count(lambda keys, col: keys >= thr)

    @pl.when(jnp.max(n_ge) > k_f)
    def _():
        n_gt = count(lambda keys, col: keys > thr)
        need = k_f - n_gt

        def cap_step(ib, v):
            cand = v | jnp.left_shift(jnp.int32(1), 14 - ib)
            cnt = count(lambda keys, col: (keys == thr) & (col < cand))
            return jnp.where(cnt <= need, cand, v)

        cap = lax.fori_loop(0, 15, cap_step, jnp.zeros((tq, 1), I32))
        o_ref[0] = jnp.where(lane == SEL_THR, thr, cap)


def _select(z, idx_blk, topk):
    bsz, seq, _ = z.shape
    tq = min(SEL_ROWS, seq)
    assert seq % SEL_CHUNK == 0 and seq % tq == 0
    return pl.pallas_call(
        functools.partial(_select_body, topk=topk),
        grid=(bsz, seq // tq),
        in_specs=[pl.BlockSpec((1, tq, IDX_COLS), lambda b, i: (b, i, idx_blk)),
                  pl.BlockSpec((1, seq, IDX_COLS), lambda b, i: (b, 0, idx_blk))],
        out_specs=pl.BlockSpec((1, tq, LANES), lambda b, i: (b, i, 0)),
        out_shape=jax.ShapeDtypeStruct((bsz, seq, LANES), I32),
        scratch_shapes=[pltpu.VMEM((tq, seq), I32)],
        compiler_params=_cparams(("arbitrary", "arbitrary")),
        name="dsa_select",
    )(z, z)


BIAS_INIT_ROWS = 8


def _init_bias_tiles(btile, bias_ref, head0, n_heads, ta):
    col = lax.broadcasted_iota(I32, (BIAS_INIT_ROWS, ta), 1)
    row0 = lax.broadcasted_iota(I32, (BIAS_INIT_ROWS, ta), 0)

    def body(r, carry):
        off = pl.multiple_of(r * BIAS_INIT_ROWS, BIAS_INIT_ROWS)
        for kind in range(2):
            dist = row0 + off - col + kind * ta
            for h in range(n_heads):
                val = jnp.full((BIAS_INIT_ROWS, ta), bias_ref[0, head0 + h], F32)
                for b in range(1, NUM_BUCKETS):
                    val = jnp.where(dist >= BUCKET_START[b], bias_ref[b, head0 + h], val)
                btile[h, kind, pl.ds(off, BIAS_INIT_ROWS), :] = val
        return carry

    lax.fori_loop(0, ta // BIAS_INIT_ROWS, body, 0)


def _softmax_step(logits, v, m_ref, l_ref, acc_ref, h):
    m_old = m_ref[h]
    m_new = jnp.maximum(m_old, jnp.max(logits, axis=1, keepdims=True))
    m_safe = jnp.where(m_new == -jnp.inf, 0.0, m_new)
    p = jnp.exp(logits - m_safe)
    alpha = jnp.exp(m_old - m_safe)
    l_ref[h] = alpha * l_ref[h] + jnp.sum(p, axis=1, keepdims=True)
    acc_ref[h] = alpha * acc_ref[h] + jnp.dot(p.astype(BF16), v, preferred_element_type=F32)
    m_ref[h] = m_new


def _cattn_body(qi_ref, ki_ref, bias_ref, zq_ref, zk_ref, zv_ref, zqi_ref, zki_ref, sel_ref, o_ref,
                m_s, l_s, acc_s, btile):
    b = pl.program_id(0)
    p = pl.program_id(1)
    qi = qi_ref[p]
    ki = ki_ref[p]
    ta = zq_ref.shape[1]

    @pl.when((b == 0) & (p == 0))
    def _():
        _init_bias_tiles(btile, bias_ref, 0, C_HEADS, ta)

    @pl.when(ki == 0)
    def _():
        m_s[...] = jnp.full(m_s.shape, -jnp.inf, F32)
        l_s[...] = jnp.zeros(l_s.shape, F32)
        acc_s[...] = jnp.zeros(acc_s.shape, F32)

    qs, ws = _idx_heads(zqi_ref[0])
    key = _sortable(_idx_score(qs, ws, _idx_keys(zki_ref[0])))
    sel = sel_ref[0]
    thr = sel[:, SEL_THR:SEL_THR + 1]
    cap = sel[:, SEL_IDXCAP:SEL_IDXCAP + 1]
    row = qi * ta + lax.broadcasted_iota(I32, (ta, ta), 0)
    col = ki * ta + lax.broadcasted_iota(I32, (ta, ta), 1)
    scale = C_HEAD_DIM ** -0.5

    def heads(keep, bias_of):
        for h in range(C_HEADS):
            lo = h * C_HEAD_DIM
            q = (zq_ref[0, :, lo:lo + C_HEAD_DIM].astype(F32) * scale).astype(BF16)
            k = zk_ref[0, :, lo:lo + C_HEAD_DIM]
            s = lax.dot_general(q, k, (((1,), (1,)), ((), ())), preferred_element_type=F32)
            logits = jnp.where(keep, s + bias_of(h), -jnp.inf)
            _softmax_step(logits, zv_ref[0, :, lo:lo + C_HEAD_DIM], m_s, l_s, acc_s, h)

    near = ki >= qi - 1

    @pl.when(near)
    def _():
        kind = qi - ki
        keym = jnp.where(col <= row, key, jnp.int32(INT_MIN))
        keep = (keym > thr) | ((keym == thr) & (col < cap))
        heads(keep, lambda h: btile[h, kind])

    @pl.when(jnp.logical_not(near))
    def _():
        keep = (key > thr) | ((key == thr) & (col < cap))
        heads(keep, lambda h: bias_ref[NUM_BUCKETS - 1, h])

    @pl.when(ki == qi)
    def _():
        outs = [acc_s[h] / l_s[h] for h in range(C_HEADS)]
        o_ref[0] = jnp.concatenate(outs, axis=1).astype(o_ref.dtype)


def _dattn_body(qi_ref, ki_ref, bias_ref, zq_ref, zk_ref, zv_ref, lam_ref, ng_ref, o_ref,
                m_s, l_s, acc_s, btile, *, lambda_init):
    b = pl.program_id(0)
    p = pl.program_id(1)
    qi = qi_ref[p]
    ki = ki_ref[p]
    ta = zq_ref.shape[1]
    hd = DIFF_HEAD_DIM

    @pl.when((b == 0) & (p == 0))
    def _():
        _init_bias_tiles(btile, bias_ref, C_HEADS, DIFF_HEADS, ta)

    @pl.when(ki == 0)
    def _():
        m_s[...] = jnp.full(m_s.shape, -jnp.inf, F32)
        l_s[...] = jnp.zeros(l_s.shape, F32)
        acc_s[...] = jnp.zeros(acc_s.shape, F32)

    scale = hd ** -0.5

    def heads(mask, bias_of):
        for h in range(DIFF_HEADS):
            v = zv_ref[0, :, 2 * hd * h:2 * hd * (h + 1)]
            bias = bias_of(h)
            for j in range(2):
                lo = (2 * h + j) * hd
                q = (zq_ref[0, :, lo:lo + hd].astype(F32) * scale).astype(BF16)
                k = zk_ref[0, :, lo:lo + hd]
                logits = lax.dot_general(q, k, (((1,), (1,)), ((), ())), preferred_element_type=F32) + bias
                if mask is not None:
                    logits = jnp.where(mask, logits, -jnp.inf)
                _softmax_step(logits, v, m_s, l_s, acc_s, 2 * h + j)

    near = ki >= qi - 1

    @pl.when(near)
    def _():
        kind = qi - ki
        row = qi * ta + lax.broadcasted_iota(I32, (ta, ta), 0)
        col = ki * ta + lax.broadcasted_iota(I32, (ta, ta), 1)
        heads(col <= row, lambda h: btile[h, kind])

    @pl.when(jnp.logical_not(near))
    def _():
        heads(None, lambda h: bias_ref[NUM_BUCKETS - 1, C_HEADS + h])

    @pl.when(ki == qi)
    def _():
        lam_p = lam_ref[...]
        lam = (jnp.exp(jnp.sum(lam_p[0:1] * lam_p[1:2], axis=1, keepdims=True))
               - jnp.exp(jnp.sum(lam_p[2:3] * lam_p[3:4], axis=1, keepdims=True)) + lambda_init)
        outs = []
        for h in range(DIFF_HEADS):
            o = acc_s[2 * h] / l_s[2 * h] - lam * (acc_s[2 * h + 1] / l_s[2 * h + 1])
            outs.append(_rms(o) * ng_ref[...] * (1.0 - lambda_init))
        o_ref[0] = jnp.concatenate(outs, axis=1).astype(o_ref.dtype)


def _causal_pairs(nq):
    qi = [q for q in range(nq) for _ in range(q + 1)]
    ki = [k for q in range(nq) for k in range(q + 1)]
    return jnp.asarray(qi, I32), jnp.asarray(ki, I32)


def _cattn(z, sel, rel_bias, ta):
    bsz, seq, _ = z.shape
    nq = seq // ta
    qi, ki = _causal_pairs(nq)
    wblk = C_WIDTH
    idx_blk = (3 * C_WIDTH + 3 * DIFF_W) // IDX_COLS
    qmap = lambda col: (lambda b, p, qi, ki: (b, qi[p], col))
    kmap = lambda col: (lambda b, p, qi, ki: (b, ki[p], col))
    grid_spec = pltpu.PrefetchScalarGridSpec(
        num_scalar_prefetch=2,
        grid=(bsz, qi.shape[0]),
        in_specs=[pl.BlockSpec(memory_space=pltpu.SMEM),
                  pl.BlockSpec((1, ta, wblk), qmap(0)),
                  pl.BlockSpec((1, ta, wblk), kmap(1)),
                  pl.BlockSpec((1, ta, wblk), kmap(2)),
                  pl.BlockSpec((1, ta, IDX_COLS), qmap(idx_blk)),
                  pl.BlockSpec((1, ta, IDX_COLS), kmap(idx_blk)),
                  pl.BlockSpec((1, ta, LANES), qmap(0))],
        out_specs=pl.BlockSpec((1, ta, wblk), qmap(0)),
        scratch_shapes=[pltpu.VMEM((C_HEADS, ta, 1), F32), pltpu.VMEM((C_HEADS, ta, 1), F32),
                        pltpu.VMEM((C_HEADS, ta, C_HEAD_DIM), F32),
                        pltpu.VMEM((C_HEADS, 2, ta, ta), F32)],
    )
    return pl.pallas_call(
        _cattn_body,
        grid_spec=grid_spec,
        out_shape=jax.ShapeDtypeStruct((bsz, seq, wblk), BF16),
        compiler_params=_cparams(("arbitrary", "arbitrary")),
        name="dsa_attn",
    )(qi, ki, rel_bias, z, z, z, z, z, sel)


def _dattn(z, rel_bias, diff_lam, diff_norm_g, lambda_init, ta):
    bsz, seq, _ = z.shape
    nq = seq // ta
    qi, ki = _causal_pairs(nq)
    wblk = DIFF_W
    n_maps = 2 * DIFF_HEADS
    qmap = lambda col: (lambda b, p, qi, ki: (b, qi[p], col))
    kmap = lambda col: (lambda b, p, qi, ki: (b, ki[p], col))
    grid_spec = pltpu.PrefetchScalarGridSpec(
        num_scalar_prefetch=2,
        grid=(bsz, qi.shape[0]),
        in_specs=[pl.BlockSpec(memory_space=pltpu.SMEM),
                  pl.BlockSpec((1, ta, wblk), qmap(3)),
                  pl.BlockSpec((1, ta, wblk), kmap(4)),
                  pl.BlockSpec((1, ta, wblk), kmap(5)),
                  pl.BlockSpec(diff_lam.shape, lambda b, p, qi, ki: (0, 0)),
                  pl.BlockSpec((1, 2 * DIFF_HEAD_DIM), lambda b, p, qi, ki: (0, 0))],
        out_specs=pl.BlockSpec((1, ta, wblk), qmap(0)),
        scratch_shapes=[pltpu.VMEM((n_maps, ta, 1), F32), pltpu.VMEM((n_maps, ta, 1), F32),
                        pltpu.VMEM((n_maps, ta, 2 * DIFF_HEAD_DIM), F32),
                        pltpu.VMEM((DIFF_HEADS, 2, ta, ta), F32)],
    )
    return pl.pallas_call(
        functools.partial(_dattn_body, lambda_init=lambda_init),
        grid_spec=grid_spec,
        out_shape=jax.ShapeDtypeStruct((bsz, seq, wblk), BF16),
        compiler_params=_cparams(("arbitrary", "arbitrary")),
        name="diff_attn",
    )(qi, ki, rel_bias, z, z, z, diff_lam, diff_norm_g.reshape(1, -1))


def _out_proj_body(oc_ref, od_ref, x_ref, g1_ref, w_ref, o_ref):
    y = (jnp.dot(oc_ref[0], w_ref[0:C_WIDTH, :], preferred_element_type=F32)
         + jnp.dot(od_ref[0], w_ref[C_WIDTH:C_WIDTH + DIFF_W, :], preferred_element_type=F32))
    o_ref[0] = x_ref[0] + g1_ref[0] * y


def _out_proj(out_c, out_d, x, g1, w_out_bf16, tm):
    bsz, seq, d = x.shape
    return pl.pallas_call(
        _out_proj_body,
        grid=(bsz, seq // tm),
        in_specs=[pl.BlockSpec((1, tm, C_WIDTH), lambda b, i: (b, i, 0)),
                  pl.BlockSpec((1, tm, DIFF_W), lambda b, i: (b, i, 0)),
                  pl.BlockSpec((1, tm, d), lambda b, i: (b, i, 0)),
                  pl.BlockSpec((1, 1, d), lambda b, i: (b, 0, 0)),
                  pl.BlockSpec((C_WIDTH + DIFF_W, d), lambda b, i: (0, 0))],
        out_specs=pl.BlockSpec((1, tm, d), lambda b, i: (b, i, 0)),
        out_shape=jax.ShapeDtypeStruct((bsz, seq, d), F32),
        compiler_params=_cparams(("arbitrary", "arbitrary")),
        name="attn_out_proj",
    )(out_c, out_d, x, g1.reshape(bsz, 1, d), w_out_bf16)


def _attn_in_weight(cd_w_in):
    sizes = (C_WIDTH, C_WIDTH, C_WIDTH, IDX_HEADS * IDX_DIM, IDX_DIM, IDX_HEADS, DIFF_W, DIFF_W, DIFF_W)
    cuts = np.cumsum(sizes)[:-1]
    q_c, k_c, v_c, q_i, k_i, w_i, q_d, k_d, v_d = jnp.split(cd_w_in, cuts, axis=1)
    pad = jnp.zeros((cd_w_in.shape[0], IDX_COLS - (IDX_W_OFF + IDX_HEADS)), cd_w_in.dtype)
    return jnp.concatenate([q_c, k_c, v_c, q_d, k_d, v_d, q_i, k_i, w_i, pad], axis=1).astype(BF16)


def kernel(x, c, positions, rel_bias, norm_g, final_norm_g, ada_w, ada_b, ab_w_in, ab_conv_a, ab_conv_b,
           ab_conv_b_bias, ab_ln_g, ab_ln_b, ab_w_out, cd_w_in, diff_lam, diff_norm_g, cd_w_out,
           moe_wr_g, moe_br_g, moe_wr_e, moe_br_e, moe_w_gate, moe_w_up, moe_w_down):
    del positions
    bsz, seq, d = x.shape
    depth = ada_w.shape[0]
    tm = min(512, seq)
    ta = min(ATT_TILE, seq)
    assert ta >= MAX_DISTANCE and seq % ta == 0
    topk = min(TOPK_MAX, seq // 4)
    mods = _ada_mod(c, ada_w, ada_b)
    for i in range(depth):
        sh1, sc1, g1, sh2, sc2, g2 = jnp.split(mods[i], 6, axis=-1)
        j = i // 2
        if i % 2 == 0:
            z = _norm_proj(x, norm_g[i, 0], sh1, sc1, ab_w_in[j].astype(BF16), tm)
            x = _conv_mix(z, x, g1, ab_conv_a[j], ab_conv_b[j], ab_conv_b_bias[j], ab_ln_g[j], ab_ln_b[j],
                          ab_w_out[j].astype(BF16), min(256, seq))
        else:
            lambda_init = 0.8 - 0.6 * math.exp(-0.3 * i)
            z = _norm_proj(x, norm_g[i, 0], sh1, sc1, _attn_in_weight(cd_w_in[j]), tm)
            sel = _select(z, (3 * C_WIDTH + 3 * DIFF_W) // IDX_COLS, topk)
            out_c = _cattn(z, sel, rel_bias, ta)
            out_d = _dattn(z, rel_bias, diff_lam[j], diff_norm_g[j], lambda_init, ta)
            x = _out_proj(out_c, out_d, x, g1, cd_w_out[j].astype(BF16), tm)
        x = _hier_moe(x, norm_g[i, 1], sh2, sc2, g2, moe_wr_g[i], moe_br_g[i], moe_wr_e[i], moe_br_e[i],
                      moe_w_gate[i], moe_w_up[i], moe_w_down[i], final_norm_g, final_norm=(i == depth - 1))
    return x
```

```python
import functools
import math

import numpy as np
import jax
import jax.numpy as jnp
from jax import lax
from jax.experimental import pallas as pl
from jax.experimental.pallas import tpu as pltpu

F32 = jnp.float32
BF16 = jnp.bfloat16
I32 = jnp.int32
HIGHEST = lax.Precision.HIGHEST

EPS = 1e-6
A_WIDTH = 512
A_CONV = 3
B_WIDTH = 512
B_CONV = 31
C_HEADS = 8
C_HEAD_DIM = 64
IDX_HEADS = 8
IDX_DIM = 32
TOPK_MAX = 256
DIFF_HEADS = 4
DIFF_HEAD_DIM = 64
NUM_BUCKETS = 32
MAX_DISTANCE = 128
N_GROUPS = 4
EXPERTS_PER_GROUP = 8
N_EXPERTS = N_GROUPS * EXPERTS_PER_GROUP
C_WIDTH = C_HEADS * C_HEAD_DIM
DIFF_W = DIFF_HEADS * 2 * DIFF_HEAD_DIM
LANES = 128
BF16_ROWS = 16
INT_MIN = -(2 ** 31)
LOG2E = math.log2(math.e)
VMEM_LIMIT = 56 * 1024 * 1024

Z_QC, Z_KC, Z_QD, Z_KD = 0, 1, 2, 3
Z_QIDX_OFF = 4 * C_WIDTH
Z_QIDX_W = IDX_HEADS * IDX_DIM
Z_KIDX_OFF = Z_QIDX_OFF + Z_QIDX_W
Z_COLS = Z_KIDX_OFF + LANES
ZT_VC, ZT_VD = 0, 1
ZT_W_OFF = C_WIDTH + DIFF_W
ZT_ROWS = ZT_W_OFF + BF16_ROWS


def _bucket_starts():
    n = np.arange(0, 2 * MAX_DISTANCE)
    me = NUM_BUCKETS // 2
    lr = np.log(np.maximum(n, 1) / me) / math.log(MAX_DISTANCE / me)
    large = me + (lr * (NUM_BUCKETS - me)).astype(np.int64)
    b = np.where(n < me, n, np.minimum(large, NUM_BUCKETS - 1))
    return [int(n[b >= k].min()) for k in range(NUM_BUCKETS)]


BUCKET_START = _bucket_starts()


def _cparams(sem):
    return pltpu.CompilerParams(dimension_semantics=sem, vmem_limit_bytes=VMEM_LIMIT)


def _rms(x):
    return x * lax.rsqrt(jnp.mean(x * x, axis=-1, keepdims=True) + EPS)


def _sigmoid(x):
    return 1.0 / (1.0 + jnp.exp(-x))


def _dot_nt(a, b):
    return lax.dot_general(a, b, (((1,), (1,)), ((), ())), preferred_element_type=F32)


def _ada_body(c_ref, w_ref, b_ref, o_ref):
    c = c_ref[...]
    cond = c * _sigmoid(c)
    o_ref[0] = jnp.dot(cond, w_ref[0], precision=HIGHEST, preferred_element_type=F32) + b_ref[0]


def _ada_mod(c, ada_w, ada_b):
    depth, d, n6 = ada_w.shape
    bsz = c.shape[0]
    rows = 8
    c_pad = jnp.zeros((rows, d), F32).at[:bsz].set(c)
    tn = 1536
    out = pl.pallas_call(
        _ada_body,
        grid=(depth, n6 // tn),
        in_specs=[pl.BlockSpec((rows, d), lambda i, j: (0, 0)),
                  pl.BlockSpec((1, d, tn), lambda i, j: (i, 0, j)),
                  pl.BlockSpec((1, 1, tn), lambda i, j: (i, 0, j))],
        out_specs=pl.BlockSpec((1, rows, tn), lambda i, j: (i, 0, j)),
        out_shape=jax.ShapeDtypeStruct((depth, rows, n6), F32),
        compiler_params=_cparams(("arbitrary", "arbitrary")),
        name="ada_mod",
    )(c_pad, ada_w, ada_b.reshape(depth, 1, n6))
    return out[:, :bsz]


def _norm_proj_body(x_ref, g_ref, sh_ref, sc_ref, w_ref, *rest):
    y = _rms(x_ref[0]) * g_ref[...]
    h = (y * (1.0 + sc_ref[0]) + sh_ref[0]).astype(BF16)
    if len(rest) == 1:
        (o_ref,) = rest
    else:
        wt_ref, o_ref, ot_ref = rest
        ot_ref[0] = _dot_nt(wt_ref[...], h).astype(ot_ref.dtype)
    o_ref[0] = jnp.dot(h, w_ref[...], preferred_element_type=F32).astype(o_ref.dtype)


def _norm_proj(x, g, sh, sc, w_bf16, tm, wt_bf16=None):
    bsz, seq, d = x.shape
    n = w_bf16.shape[1]
    in_specs = [pl.BlockSpec((1, tm, d), lambda b, i: (b, i, 0)),
                pl.BlockSpec((1, d), lambda b, i: (0, 0)),
                pl.BlockSpec((1, 1, d), lambda b, i: (b, 0, 0)),
                pl.BlockSpec((1, 1, d), lambda b, i: (b, 0, 0)),
                pl.BlockSpec((d, n), lambda b, i: (0, 0))]
    out_specs = pl.BlockSpec((1, tm, n), lambda b, i: (b, i, 0))
    out_shape = jax.ShapeDtypeStruct((bsz, seq, n), BF16)
    args = [x, g.reshape(1, d), sh.reshape(bsz, 1, d), sc.reshape(bsz, 1, d), w_bf16]
    if wt_bf16 is not None:
        nt = wt_bf16.shape[0]
        in_specs.append(pl.BlockSpec((nt, d), lambda b, i: (0, 0)))
        out_specs = [out_specs, pl.BlockSpec((1, nt, tm), lambda b, i: (b, 0, i))]
        out_shape = [out_shape, jax.ShapeDtypeStruct((bsz, nt, seq), BF16)]
        args.append(wt_bf16)
    return pl.pallas_call(
        _norm_proj_body,
        grid=(bsz, seq // tm),
        in_specs=in_specs,
        out_specs=out_specs,
        out_shape=out_shape,
        compiler_params=_cparams(("arbitrary", "arbitrary")),
        name="norm_proj",
    )(*args)


CONV_HALO = 32
CONV_ROWS = 64


def _conv_body(z_ref, x_ref, g1_ref, ca_ref, cb_ref, cbb_ref, lng_ref, lnb_ref, wo_ref, o_ref,
               ua_scr, ub_scr, y_scr, *, tl):
    l = pl.program_id(1)

    @pl.when(l == 0)
    def _():
        ua_scr[0:CONV_HALO, :] = jnp.zeros((CONV_HALO, A_WIDTH), F32)
        ub_scr[0:CONV_HALO, :] = jnp.zeros((CONV_HALO, B_WIDTH), F32)

    a = A_WIDTH
    gate_c = z_ref[0, :, a:2 * a].astype(F32)
    x_a = z_ref[0, :, 2 * a:3 * a].astype(F32)
    ua_scr[CONV_HALO:CONV_HALO + tl, :] = gate_c * x_a
    val_b = z_ref[0, :, 3 * a:3 * a + B_WIDTH].astype(F32)
    glu = z_ref[0, :, 3 * a + B_WIDTH:3 * a + 2 * B_WIDTH].astype(F32)
    ub_scr[CONV_HALO:CONV_HALO + tl, :] = val_b * _sigmoid(glu)

    for r in range(0, tl, CONV_ROWS):
        acc_a = None
        for k in range(A_CONV):
            tap = ua_scr[CONV_HALO + r - (A_CONV - 1) + k:CONV_HALO + r - (A_CONV - 1) + k + CONV_ROWS, :]
            term = tap * ca_ref[k:k + 1, :]
            acc_a = term if acc_a is None else acc_a + term
        gate_b = z_ref[0, r:r + CONV_ROWS, 0:a].astype(F32)
        y_scr[r:r + CONV_ROWS, 0:a] = (gate_b * acc_a).astype(BF16)

        acc_b = None
        for k in range(B_CONV):
            tap = ub_scr[CONV_HALO + r - (B_CONV - 1) + k:CONV_HALO + r - (B_CONV - 1) + k + CONV_ROWS, :]
            term = tap * cb_ref[k:k + 1, :]
            acc_b = term if acc_b is None else acc_b + term
        u = acc_b + cbb_ref[...]
        mu = jnp.mean(u, axis=-1, keepdims=True)
        uc = u - mu
        var = jnp.mean(uc * uc, axis=-1, keepdims=True)
        v = uc * lax.rsqrt(var + EPS) * lng_ref[...] + lnb_ref[...]
        y_scr[r:r + CONV_ROWS, a:a + B_WIDTH] = (v * _sigmoid(v)).astype(BF16)

    ua_scr[0:CONV_HALO, :] = ua_scr[tl:tl + CONV_HALO, :]
    ub_scr[0:CONV_HALO, :] = ub_scr[tl:tl + CONV_HALO, :]
    y = jnp.dot(y_scr[...], wo_ref[...], preferred_element_type=F32)
    o_ref[0] = x_ref[0] + g1_ref[0] * y


def _conv_mix(z, x, g1, conv_a, conv_b, conv_b_bias, ln_g, ln_b, w_out_bf16, tl):
    bsz, seq, d = x.shape
    nz = z.shape[-1]
    wide = A_WIDTH + B_WIDTH
    full = lambda shape: pl.BlockSpec(shape, lambda b, l: (0,) * len(shape))
    return pl.pallas_call(
        functools.partial(_conv_body, tl=tl),
        grid=(bsz, seq // tl),
        in_specs=[pl.BlockSpec((1, tl, nz), lambda b, l: (b, l, 0)),
                  pl.BlockSpec((1, tl, d), lambda b, l: (b, l, 0)),
                  pl.BlockSpec((1, 1, d), lambda b, l: (b, 0, 0)),
                  full((A_CONV, A_WIDTH)), full((B_CONV, B_WIDTH)), full((1, B_WIDTH)),
                  full((1, B_WIDTH)), full((1, B_WIDTH)), full((wide, d))],
        out_specs=pl.BlockSpec((1, tl, d), lambda b, l: (b, l, 0)),
        out_shape=jax.ShapeDtypeStruct((bsz, seq, d), F32),
        scratch_shapes=[pltpu.VMEM((CONV_HALO + tl, A_WIDTH), F32),
                        pltpu.VMEM((CONV_HALO + tl, B_WIDTH), F32),
                        pltpu.VMEM((tl, wide), BF16)],
        compiler_params=_cparams(("arbitrary", "arbitrary")),
        name="conv_mix",
    )(z, x, g1.reshape(bsz, 1, d), conv_a, conv_b, conv_b_bias.reshape(1, -1), ln_g.reshape(1, -1),
      ln_b.reshape(1, -1), w_out_bf16)


MOE_ROWS = 256
ROUTER_ROWS = 512
COMBINE_ROWS = 256
META_GATE0, META_GATE1, META_E0, META_E1, META_RANK0, META_RANK1 = range(6)
GROUP_LANE0 = N_EXPERTS


def _router_body(x_ref, g_ref, sh_ref, sc_ref, wr_ref, br_ref, tri_ref, h_ref, meta_ref, cnt_ref, base_scr):
    @pl.when(pl.program_id(0) == 0)
    def _():
        base_scr[...] = jnp.zeros_like(base_scr)

    h = _rms(x_ref[...]) * g_ref[...]
    h = h * (1.0 + sc_ref[0]) + sh_ref[0]
    h_ref[...] = h
    logits = jnp.dot(h, wr_ref[...], precision=HIGHEST, preferred_element_type=F32) + br_ref[...]
    tr = logits.shape[0]
    lane = lax.broadcasted_iota(I32, (tr, LANES), 1)
    lane_f = lane.astype(F32)
    neg = jnp.float32(-jnp.inf)
    big = jnp.float32(1e9)

    is_group = (lane >= GROUP_LANE0) & (lane < GROUP_LANE0 + N_GROUPS)
    glog = jnp.where(is_group, logits, neg)
    gmax = jnp.max(glog, axis=1, keepdims=True)
    p_top = 1.0 / jnp.sum(jnp.exp(glog - gmax), axis=1, keepdims=True)
    g_sel = jnp.min(jnp.where(glog == gmax, lane_f, big), axis=1, keepdims=True) - GROUP_LANE0
    lo = g_sel * EXPERTS_PER_GROUP
    in_group = (lane_f >= lo) & (lane_f < lo + EXPERTS_PER_GROUP)
    f1 = jnp.where(in_group, logits, neg)
    v1 = jnp.max(f1, axis=1, keepdims=True)
    i1 = jnp.min(jnp.where(f1 == v1, lane_f, big), axis=1, keepdims=True)
    f2 = jnp.where(lane_f == i1, neg, f1)
    v2 = jnp.max(f2, axis=1, keepdims=True)
    i2 = jnp.min(jnp.where(f2 == v2, lane_f, big), axis=1, keepdims=True)
    a = jnp.exp(v2 - v1)
    w1 = 1.0 / (1.0 + a)
    gate0 = p_top * w1
    gate1 = p_top * (a * w1)

    oh0 = lane_f == i1
    oh1 = lane_f == i2
    ind0 = jnp.where(oh0, 1.0, 0.0)
    ind1 = jnp.where(oh1, 1.0, 0.0)
    pre0 = jnp.dot(tri_ref[...], ind0.astype(BF16), preferred_element_type=F32)
    pre1 = jnp.dot(tri_ref[...], ind1.astype(BF16), preferred_element_type=F32)
    tot0 = jnp.sum(ind0, axis=0, keepdims=True)
    tot1 = jnp.sum(ind1, axis=0, keepdims=True)
    base = base_scr[...]
    rank0 = jnp.sum(jnp.where(oh0, base + pre0, 0.0), axis=1, keepdims=True)
    rank1 = jnp.sum(jnp.where(oh1, base + tot0 + pre1, 0.0), axis=1, keepdims=True)
    new_base = base + tot0 + tot1
    base_scr[...] = new_base
    cnt_ref[...] = new_base

    meta = jnp.zeros((tr, LANES), F32)
    for col, val in ((META_GATE0, gate0), (META_GATE1, gate1), (META_E0, i1), (META_E1, i2),
                     (META_RANK0, rank0), (META_RANK1, rank1)):
        meta = jnp.where(lane == col, val, meta)
    meta_ref[...] = meta


def _moe_router(x2, g, sh, sc, wr_g, br_g, wr_e, br_e, seq):
    n_tok, d = x2.shape
    bsz = n_tok // seq
    tr = min(ROUTER_ROWS, seq)
    steps_per_batch = seq // tr
    wr = jnp.zeros((d, LANES), F32).at[:, :N_EXPERTS].set(wr_e).at[:, GROUP_LANE0:GROUP_LANE0 + N_GROUPS].set(wr_g)
    br = jnp.zeros((1, LANES), F32).at[0, :N_EXPERTS].set(br_e).at[0, GROUP_LANE0:GROUP_LANE0 + N_GROUPS].set(br_g)
    tri = jnp.tril(jnp.ones((tr, tr), BF16), -1)
    full = lambda shape: pl.BlockSpec(shape, lambda i: (0,) * len(shape))
    return pl.pallas_call(
        _router_body,
        grid=(n_tok // tr,),
        in_specs=[pl.BlockSpec((tr, d), lambda i: (i, 0)),
                  full((1, d)),
                  pl.BlockSpec((1, 1, d), lambda i: (i // steps_per_batch, 0, 0)),
                  pl.BlockSpec((1, 1, d), lambda i: (i // steps_per_batch, 0, 0)),
                  full((d, LANES)), full((1, LANES)), full((tr, tr))],
        out_specs=[pl.BlockSpec((tr, d), lambda i: (i, 0)),
                   pl.BlockSpec((tr, LANES), lambda i: (i, 0)),
                   full((1, LANES))],
        out_shape=[jax.ShapeDtypeStruct((n_tok, d), F32),
                   jax.ShapeDtypeStruct((n_tok, LANES), F32),
                   jax.ShapeDtypeStruct((1, LANES), F32)],
        scratch_shapes=[pltpu.VMEM((1, LANES), F32)],
        compiler_params=_cparams(("arbitrary",)),
        name="moe_router",
    )(x2, g.reshape(1, d), sh.reshape(bsz, 1, d), sc.reshape(bsz, 1, d), wr, br, tri)


def _dispatch_body(dest_ref, h_ref, xr_ref, sem, *, tr):
    def row_copy(t, k):
        return pltpu.make_async_copy(h_ref.at[pl.ds(t, 1)], xr_ref.at[pl.ds(dest_ref[0, 0, k * tr + t], 1)], sem)

    def start(t, carry):
        row_copy(t, 0).start()
        row_copy(t, 1).start()
        return carry

    lax.fori_loop(0, tr, start, 0)

    def wait(t, carry):
        row_copy(t, 0).wait()
        row_copy(t, 1).wait()
        return carry

    lax.fori_loop(0, tr, wait, 0)


def _moe_dispatch(h2, dest_blocks, tr):
    n_tok, d = h2.shape
    return pl.pallas_call(
        functools.partial(_dispatch_body, tr=tr),
        grid=(n_tok // tr,),
        in_specs=[pl.BlockSpec((1, 1, 2 * tr), lambda i: (i, 0, 0), memory_space=pltpu.SMEM),
                  pl.BlockSpec((tr, d), lambda i: (i, 0))],
        out_specs=pl.BlockSpec(memory_space=pl.ANY),
        out_shape=jax.ShapeDtypeStruct((2 * n_tok, d), F32),
        scratch_shapes=[pltpu.SemaphoreType.DMA(())],
        compiler_params=_cparams(("arbitrary",)),
        name="moe_dispatch",
    )(dest_blocks, h2)


def _expert_body(pb_ref, pe_ref, plo_ref, phi_ref, x_ref, wg_ref, wu_ref, wd_ref, o_ref, wg_s, wu_s, wd_s):
    p = pl.program_id(0)
    prev = jnp.maximum(p - 1, 0)
    new_expert = (p == 0) | (pe_ref[p] != pe_ref[prev])
    first = (p == 0) | (pb_ref[p] != pb_ref[prev])

    @pl.when(new_expert)
    def _():
        wg_s[...] = wg_ref[0].astype(BF16)
        wu_s[...] = wu_ref[0].astype(BF16)
        wd_s[...] = wd_ref[0].astype(BF16)

    x = x_ref[...].astype(BF16)
    gt = jnp.dot(x, wg_s[...], preferred_element_type=F32)
    up = jnp.dot(x, wu_s[...], preferred_element_type=F32)
    act = (gt * _sigmoid(gt)) * up
    y = jnp.dot(act.astype(BF16), wd_s[...], preferred_element_type=F32)
    rows = lax.broadcasted_iota(I32, (y.shape[0], 1), 0)
    y = jnp.where((rows >= plo_ref[p]) & (rows < phi_ref[p]), y, 0.0)

    @pl.when(first)
    def _():
        o_ref[...] = y

    @pl.when(jnp.logical_not(first))
    def _():
        o_ref[...] += y


def _moe_experts(x_rows, pairs, w_gate, w_up, w_down):
    n_rows, d = x_rows.shape
    de = w_gate.shape[-1]
    n_pairs = pairs[0].shape[0]
    grid_spec = pltpu.PrefetchScalarGridSpec(
        num_scalar_prefetch=4,
        grid=(n_pairs,),
        in_specs=[pl.BlockSpec((MOE_ROWS, d), lambda p, pb, pe, lo, hi: (pb[p], 0)),
                  pl.BlockSpec((1, d, de), lambda p, pb, pe, lo, hi: (pe[p], 0, 0)),
                  pl.BlockSpec((1, d, de), lambda p, pb, pe, lo, hi: (pe[p], 0, 0)),
                  pl.BlockSpec((1, de, d), lambda p, pb, pe, lo, hi: (pe[p], 0, 0))],
        out_specs=pl.BlockSpec((MOE_ROWS, d), lambda p, pb, pe, lo, hi: (pb[p], 0)),
        scratch_shapes=[pltpu.VMEM((d, de), BF16), pltpu.VMEM((d, de), BF16), pltpu.VMEM((de, d), BF16)],
    )
    return pl.pallas_call(
        _expert_body,
        grid_spec=grid_spec,
        out_shape=jax.ShapeDtypeStruct((n_rows, d), F32),
        compiler_params=_cparams(("arbitrary",)),
        name="moe_experts",
    )(*pairs, x_rows, w_gate, w_up, w_down)


def _combine_body(dest_ref, y_ref, x_ref, meta_ref, g2_ref, fg_ref, o_ref, r0, r1, sem, *, tc, final_norm):
    def row_copy(t, k, buf):
        return pltpu.make_async_copy(y_ref.at[pl.ds(dest_ref[0, 0, k * tc + t], 1)], buf.at[pl.ds(t, 1)], sem)

    def start(t, carry):
        row_copy(t, 0, r0).start()
        row_copy(t, 1, r1).start()
        return carry

    lax.fori_loop(0, tc, start, 0)

    def wait(t, carry):
        row_copy(t, 0, r0).wait()
        row_copy(t, 1, r1).wait()
        return carry

    lax.fori_loop(0, tc, wait, 0)
    meta = meta_ref[...]
    moe = meta[:, META_GATE0:META_GATE0 + 1] * r0[...] + meta[:, META_GATE1:META_GATE1 + 1] * r1[...]
    xn = x_ref[...] + g2_ref[0] * moe
    if final_norm:
        xn = _rms(xn) * fg_ref[...]
    o_ref[...] = xn


def _moe_combine(y_rows, dest_blocks, x2, meta, g2, final_g, seq, tc, final_norm):
    n_tok, d = x2.shape
    bsz = n_tok // seq
    steps_per_batch = seq // tc
    return pl.pallas_call(
        functools.partial(_combine_body, tc=tc, final_norm=final_norm),
        grid=(n_tok // tc,),
        in_specs=[pl.BlockSpec((1, 1, 2 * tc), lambda i: (i, 0, 0), memory_space=pltpu.SMEM),
                  pl.BlockSpec(memory_space=pl.ANY),
                  pl.BlockSpec((tc, d), lambda i: (i, 0)),
                  pl.BlockSpec((tc, LANES), lambda i: (i, 0)),
                  pl.BlockSpec((1, 1, d), lambda i: (i // steps_per_batch, 0, 0)),
                  pl.BlockSpec((1, d), lambda i: (0, 0))],
        out_specs=pl.BlockSpec((tc, d), lambda i: (i, 0)),
        out_shape=jax.ShapeDtypeStruct((n_tok, d), F32),
        scratch_shapes=[pltpu.VMEM((tc, d), F32), pltpu.VMEM((tc, d), F32), pltpu.SemaphoreType.DMA(())],
        compiler_params=_cparams(("arbitrary",)),
        name="moe_combine",
    )(dest_blocks, y_rows, x2, meta, g2.reshape(bsz, 1, d), final_g.reshape(1, d))


def _blocked_dest(dest0, dest1, rows):
    nb = dest0.shape[0] // rows
    return jnp.concatenate([dest0.reshape(nb, rows), dest1.reshape(nb, rows)], axis=1).reshape(nb, 1, 2 * rows)


def _expert_pairs(counts, n_rows):
    n_blocks = n_rows // MOE_ROWS
    n_pairs = n_blocks + N_EXPERTS
    ends = jnp.cumsum(counts)
    starts = ends - counts
    first_blk = starts // MOE_ROWS
    last_blk = (ends - 1) // MOE_ROWS
    npairs = jnp.where(counts > 0, last_blk - first_blk + 1, 0)
    pend = jnp.cumsum(npairs)
    poff = pend - npairs
    total = pend[-1]
    p = jnp.arange(n_pairs, dtype=I32)
    p_eff = jnp.minimum(p, total - 1)
    e = jnp.minimum(jnp.searchsorted(pend, p_eff, side='right'), N_EXPERTS - 1).astype(I32)
    blk = (first_blk[e] + p_eff - poff[e]).astype(I32)
    valid = p < total
    lo = jnp.where(valid, jnp.clip(starts[e] - blk * MOE_ROWS, 0, MOE_ROWS), 0).astype(I32)
    hi = jnp.where(valid, jnp.clip(ends[e] - blk * MOE_ROWS, 0, MOE_ROWS), 0).astype(I32)
    return blk, e, lo, hi


def _hier_moe(x, g, sh, sc, g2, wr_g, br_g, wr_e, br_e, w_gate, w_up, w_down, final_g, final_norm):
    bsz, seq, d = x.shape
    n_tok = bsz * seq
    x2 = x.reshape(n_tok, d)
    h2, meta, cnt = _moe_router(x2, g, sh, sc, wr_g, br_g, wr_e, br_e, seq)
    counts = cnt[0, :N_EXPERTS].astype(I32)
    starts = jnp.cumsum(counts) - counts
    e0 = meta[:, META_E0].astype(I32)
    e1 = meta[:, META_E1].astype(I32)
    dest0 = starts[e0] + meta[:, META_RANK0].astype(I32)
    dest1 = starts[e1] + meta[:, META_RANK1].astype(I32)
    tr = min(ROUTER_ROWS, seq)
    tc = min(COMBINE_ROWS, seq)
    x_rows = _moe_dispatch(h2, _blocked_dest(dest0, dest1, tr), tr)
    y_rows = _moe_experts(x_rows, _expert_pairs(counts, 2 * n_tok), w_gate, w_up, w_down)
    out = _moe_combine(y_rows, _blocked_dest(dest0, dest1, tc), x2, meta, g2, final_g, seq, tc, final_norm)
    return out.reshape(bsz, seq, d)


SEL_COLS = 256
SEL_CHUNK = 512
SEL_FOLD = 32
ATT_TILE = 512
SEL_THR, SEL_IDXCAP, SEL_OUT_ROWS = 0, 1, 8


def _sortable(score):
    bits = lax.bitcast_convert_type(score, I32)
    return bits ^ ((bits >> 31) & jnp.int32(0x7FFFFFFF))


def _idx_queries(zq):
    zf = zq.astype(F32)
    return [zf[:, h * IDX_DIM:(h + 1) * IDX_DIM].astype(BF16) for h in range(IDX_HEADS)]


def _idx_score_t(k, qs, wt):
    acc = None
    for h, q in enumerate(qs):
        term = jnp.maximum(_dot_nt(k, q), 0.0) * wt[h:h + 1, :]
        acc = term if acc is None else acc + term
    return acc


def _select_body(zq_ref, zk_ref, wt_ref, o_ref, keys_scr, *, topk):
    i = pl.program_id(1)
    tq = zq_ref.shape[1]
    qs = _idx_queries(zq_ref[0])
    wt = wt_ref[0].astype(F32)
    n_chunks = ((i + 1) * tq + SEL_CHUNK - 1) // SEL_CHUNK
    krow0 = lax.broadcasted_iota(I32, (SEL_CHUNK, tq), 0)
    qcol = i * tq + lax.broadcasted_iota(I32, (SEL_CHUNK, tq), 1)
    int_min = jnp.int32(INT_MIN)

    def fill(c, carry):
        off = pl.multiple_of(c * SEL_CHUNK, SEL_CHUNK)
        k = zk_ref[0, pl.ds(off, SEL_CHUNK), :][:, 0:IDX_DIM]
        key = _sortable(_idx_score_t(k, qs, wt))
        keys_scr[pl.ds(off, SEL_CHUNK), :] = jnp.where(krow0 + off <= qcol, key, int_min)
        return carry

    lax.fori_loop(0, n_chunks, fill, 0)

    def count(pred):
        def body(c, acc):
            off = pl.multiple_of(c * SEL_CHUNK, SEL_CHUNK)
            ind = jnp.where(pred(keys_scr[pl.ds(off, SEL_CHUNK), :], krow0 + off), 1, 0)
            return acc + jnp.sum(ind.reshape(SEL_CHUNK // SEL_FOLD, SEL_FOLD, tq), axis=0)
        acc = lax.fori_loop(0, n_chunks, body, jnp.zeros((SEL_FOLD, tq), I32))
        return jnp.sum(acc.astype(F32), axis=0, keepdims=True)

    k_f = jnp.float32(topk)

    def bit_step(ib, u):
        cand_u = u | jnp.left_shift(jnp.int32(1), 31 - ib)
        cand = cand_u ^ int_min
        cnt = count(lambda keys, kidx: keys >= cand)
        return jnp.where(cnt >= k_f, cand_u, u)

    u = lax.fori_loop(0, 32, bit_step, jnp.zeros((1, tq), I32))
    thr = jnp.maximum(u ^ int_min, int_min + 1)
    orow = lax.broadcasted_iota(I32, (SEL_OUT_ROWS, tq), 0)
    cap_all = jnp.int32(2 ** 30)
    o_ref[0] = jnp.where(orow == SEL_THR, thr, cap_all)

    n_ge = count(lambda keys, kidx: keys >= thr)

    @pl.when(jnp.max(n_ge) > k_f)
    def _():
        n_gt = count(lambda keys, kidx: keys > thr)
        need = k_f - n_gt

        def cap_step(ib, v):
            cand = v | jnp.left_shift(jnp.int32(1), 14 - ib)
            cnt = count(lambda keys, kidx: (keys == thr) & (kidx < cand))
            return jnp.where(cnt <= need, cand, v)

        cap = lax.fori_loop(0, 15, cap_step, jnp.zeros((1, tq), I32))
        o_ref[0] = jnp.where(orow == SEL_THR, thr, cap)


def _select(z, zt, topk):
    bsz, seq, _ = z.shape
    tq = min(SEL_COLS, seq)
    assert seq % SEL_CHUNK == 0 and seq % tq == 0
    return pl.pallas_call(
        functools.partial(_select_body, topk=topk),
        grid=(bsz, seq // tq),
        in_specs=[pl.BlockSpec((1, tq, Z_QIDX_W), lambda b, i: (b, i, Z_QIDX_OFF // Z_QIDX_W)),
                  pl.BlockSpec((1, seq, LANES), lambda b, i: (b, 0, Z_KIDX_OFF // LANES)),
                  pl.BlockSpec((1, BF16_ROWS, tq), lambda b, i: (b, ZT_W_OFF // BF16_ROWS, i))],
        out_specs=pl.BlockSpec((1, SEL_OUT_ROWS, tq), lambda b, i: (b, 0, i)),
        out_shape=jax.ShapeDtypeStruct((bsz, SEL_OUT_ROWS, seq), I32),
        scratch_shapes=[pltpu.VMEM((seq, tq), I32)],
        compiler_params=_cparams(("arbitrary", "arbitrary")),
        name="dsa_select",
    )(z, z, zt)


BIAS_INIT_ROWS = 8
ONES_ROWS = BF16_ROWS


def _init_bias_tiles(btile, bias_ref, head0, n_heads, ta):
    col = lax.broadcasted_iota(I32, (BIAS_INIT_ROWS, ta), 1)
    row0 = lax.broadcasted_iota(I32, (BIAS_INIT_ROWS, ta), 0)

    def body(r, carry):
        off = pl.multiple_of(r * BIAS_INIT_ROWS, BIAS_INIT_ROWS)
        for kind in range(2):
            dist = col - (row0 + off) + kind * ta
            for h in range(n_heads):
                val = jnp.full((BIAS_INIT_ROWS, ta), bias_ref[0, head0 + h] * LOG2E, F32)
                for b in range(1, NUM_BUCKETS):
                    val = jnp.where(dist >= BUCKET_START[b], bias_ref[b, head0 + h] * LOG2E, val)
                btile[h, kind, pl.ds(off, BIAS_INIT_ROWS), :] = val
        return carry

    lax.fori_loop(0, ta // BIAS_INIT_ROWS, body, 0)


def _with_ones(vt):
    return jnp.concatenate([vt, jnp.ones((ONES_ROWS, vt.shape[1]), vt.dtype)], axis=0)


def _softmax_step_t(logits, v_aug, m_ref, acc_ref, idx):
    m_old = m_ref[idx]
    m_new = jnp.maximum(m_old, jnp.max(logits, axis=0, keepdims=True))
    m_safe = jnp.where(m_new == -jnp.inf, 0.0, m_new)
    p = jnp.exp2(logits - m_safe)
    alpha = jnp.exp2(m_old - m_safe)
    acc_ref[idx] = alpha * acc_ref[idx] + jnp.dot(v_aug, p.astype(BF16), preferred_element_type=F32)
    m_ref[idx] = m_new


def _reset_softmax(m_s, acc_s):
    m_s[...] = jnp.full(m_s.shape, -jnp.inf, F32)
    acc_s[...] = jnp.zeros(acc_s.shape, F32)


def _cattn_body(qi_ref, ki_ref, bias_ref, zq_ref, zk_ref, vt_ref, zqi_ref, zki_ref, wt_ref, sel_ref, o_ref,
                m_s, acc_s, btile):
    b = pl.program_id(0)
    p = pl.program_id(1)
    qi = qi_ref[p]
    ki = ki_ref[p]
    ta = zq_ref.shape[1]
    hd = C_HEAD_DIM

    @pl.when((b == 0) & (p == 0))
    def _():
        _init_bias_tiles(btile, bias_ref, 0, C_HEADS, ta)

    @pl.when(ki == 0)
    def _():
        _reset_softmax(m_s, acc_s)

    k_idx = zki_ref[0][:, 0:IDX_DIM]
    key = _sortable(_idx_score_t(k_idx, _idx_queries(zqi_ref[0]), wt_ref[0].astype(F32)))
    thr = sel_ref[0, SEL_THR:SEL_THR + 1, :]
    cap = sel_ref[0, SEL_IDXCAP:SEL_IDXCAP + 1, :]
    krow = ki * ta + lax.broadcasted_iota(I32, (ta, ta), 0)
    qcol = qi * ta + lax.broadcasted_iota(I32, (ta, ta), 1)

    def heads(keep, bias_of):
        for h in range(C_HEADS):
            lo = h * hd
            s = _dot_nt(zk_ref[0, :, lo:lo + hd], zq_ref[0, :, lo:lo + hd])
            logits = jnp.where(keep, s + bias_of(h), -jnp.inf)
            _softmax_step_t(logits, _with_ones(vt_ref[0, lo:lo + hd, :]), m_s, acc_s, h)

    near = ki >= qi - 1

    @pl.when(near)
    def _():
        kind = qi - ki
        keym = jnp.where(krow <= qcol, key, jnp.int32(INT_MIN))
        keep = (keym > thr) | ((keym == thr) & (krow < cap))
        heads(keep, lambda h: btile[h, kind])

    @pl.when(jnp.logical_not(near))
    def _():
        keep = (key > thr) | ((key == thr) & (krow < cap))
        heads(keep, lambda h: bias_ref[NUM_BUCKETS - 1, h] * LOG2E)

    @pl.when(ki == qi)
    def _():
        outs = []
        for h in range(C_HEADS):
            a = acc_s[h]
            outs.append((a[0:hd] / a[hd:hd + 1]).T)
        o_ref[0] = jnp.concatenate(outs, axis=1).astype(o_ref.dtype)


def _dattn_body(qi_ref, ki_ref, bias_ref, zq_ref, zk_ref, vt_ref, lam_ref, ng_ref, o_ref,
                m_s, acc_s, btile, *, lambda_init):
    b = pl.program_id(0)
    p = pl.program_id(1)
    qi = qi_ref[p]
    ki = ki_ref[p]
    ta = zq_ref.shape[1]
    hd = DIFF_HEAD_DIM
    dv = 2 * hd

    @pl.when((b == 0) & (p == 0))
    def _():
        _init_bias_tiles(btile, bias_ref, C_HEADS, DIFF_HEADS, ta)

    @pl.when(ki == 0)
    def _():
        _reset_softmax(m_s, acc_s)

    def heads(mask, bias_of):
        for h in range(DIFF_HEADS):
            v_aug = _with_ones(vt_ref[0, dv * h:dv * (h + 1), :])
            bias = bias_of(h)
            for j in range(2):
                lo = (2 * h + j) * hd
                logits = _dot_nt(zk_ref[0, :, lo:lo + hd], zq_ref[0, :, lo:lo + hd]) + bias
                if mask is not None:
                    logits = jnp.where(mask, logits, -jnp.inf)
                _softmax_step_t(logits, v_aug, m_s, acc_s, 2 * h + j)

    near = ki >= qi - 1

    @pl.when(near)
    def _():
        kind = qi - ki
        krow = ki * ta + lax.broadcasted_iota(I32, (ta, ta), 0)
        qcol = qi * ta + lax.broadcasted_iota(I32, (ta, ta), 1)
        heads(krow <= qcol, lambda h: btile[h, kind])

    @pl.when(jnp.logical_not(near))
    def _():
        heads(None, lambda h: bias_ref[NUM_BUCKETS - 1, C_HEADS + h] * LOG2E)

    @pl.when(ki == qi)
    def _():
        lam_p = lam_ref[...]
        lam = (jnp.exp(jnp.sum(lam_p[0:1] * lam_p[1:2], axis=1, keepdims=True))
               - jnp.exp(jnp.sum(lam_p[2:3] * lam_p[3:4], axis=1, keepdims=True)) + lambda_init)
        outs = []
        for h in range(DIFF_HEADS):
            a1 = acc_s[2 * h]
            a2 = acc_s[2 * h + 1]
            o = a1[0:dv] / a1[dv:dv + 1] - lam * (a2[0:dv] / a2[dv:dv + 1])
            o = o * lax.rsqrt(jnp.mean(o * o, axis=0, keepdims=True) + EPS) * ng_ref[...] * (1.0 - lambda_init)
            outs.append(o.T)
        o_ref[0] = jnp.concatenate(outs, axis=1).astype(o_ref.dtype)


def _causal_pairs(nq):
    qi = [q for q in range(nq) for _ in range(q + 1)]
    ki = [k for q in range(nq) for k in range(q + 1)]
    return jnp.asarray(qi, I32), jnp.asarray(ki, I32)


def _cattn(z, zt, sel, rel_bias, ta):
    bsz, seq, _ = z.shape
    qi, ki = _causal_pairs(seq // ta)
    qmap = lambda col: (lambda b, p, qi, ki: (b, qi[p], col))
    kmap = lambda col: (lambda b, p, qi, ki: (b, ki[p], col))
    dv_aug = C_HEAD_DIM + ONES_ROWS
    grid_spec = pltpu.PrefetchScalarGridSpec(
        num_scalar_prefetch=2,
        grid=(bsz, qi.shape[0]),
        in_specs=[pl.BlockSpec(memory_space=pltpu.SMEM),
                  pl.BlockSpec((1, ta, C_WIDTH), qmap(Z_QC)),
                  pl.BlockSpec((1, ta, C_WIDTH), kmap(Z_KC)),
                  pl.BlockSpec((1, C_WIDTH, ta), lambda b, p, qi, ki: (b, ZT_VC, ki[p])),
                  pl.BlockSpec((1, ta, Z_QIDX_W), qmap(Z_QIDX_OFF // Z_QIDX_W)),
                  pl.BlockSpec((1, ta, LANES), kmap(Z_KIDX_OFF // LANES)),
                  pl.BlockSpec((1, BF16_ROWS, ta), lambda b, p, qi, ki: (b, ZT_W_OFF // BF16_ROWS, qi[p])),
                  pl.BlockSpec((1, SEL_OUT_ROWS, ta), lambda b, p, qi, ki: (b, 0, qi[p]))],
        out_specs=pl.BlockSpec((1, ta, C_WIDTH), qmap(0)),
        scratch_shapes=[pltpu.VMEM((C_HEADS, 1, ta), F32),
                        pltpu.VMEM((C_HEADS, dv_aug, ta), F32),
                        pltpu.VMEM((C_HEADS, 2, ta, ta), F32)],
    )
    return pl.pallas_call(
        _cattn_body,
        grid_spec=grid_spec,
        out_shape=jax.ShapeDtypeStruct((bsz, seq, C_WIDTH), BF16),
        compiler_params=_cparams(("arbitrary", "arbitrary")),
        name="dsa_attn",
    )(qi, ki, rel_bias, z, z, zt, z, z, zt, sel)


def _dattn(z, zt, rel_bias, diff_lam, diff_norm_g, lambda_init, ta):
    bsz, seq, _ = z.shape
    qi, ki = _causal_pairs(seq // ta)
    n_maps = 2 * DIFF_HEADS
    dv = 2 * DIFF_HEAD_DIM
    qmap = lambda col: (lambda b, p, qi, ki: (b, qi[p], col))
    kmap = lambda col: (lambda b, p, qi, ki: (b, ki[p], col))
    grid_spec = pltpu.PrefetchScalarGridSpec(
        num_scalar_prefetch=2,
        grid=(bsz, qi.shape[0]),
        in_specs=[pl.BlockSpec(memory_space=pltpu.SMEM),
                  pl.BlockSpec((1, ta, DIFF_W), qmap(Z_QD)),
                  pl.BlockSpec((1, ta, DIFF_W), kmap(Z_KD)),
                  pl.BlockSpec((1, DIFF_W, ta), lambda b, p, qi, ki: (b, ZT_VD, ki[p])),
                  pl.BlockSpec(diff_lam.shape, lambda b, p, qi, ki: (0, 0)),
                  pl.BlockSpec((dv, 1), lambda b, p, qi, ki: (0, 0))],
        out_specs=pl.BlockSpec((1, ta, DIFF_W), qmap(0)),
        scratch_shapes=[pltpu.VMEM((n_maps, 1, ta), F32),
                        pltpu.VMEM((n_maps, dv + ONES_ROWS, ta), F32),
                        pltpu.VMEM((DIFF_HEADS, 2, ta, ta), F32)],
    )
    return pl.pallas_call(
        functools.partial(_dattn_body, lambda_init=lambda_init),
        grid_spec=grid_spec,
        out_shape=jax.ShapeDtypeStruct((bsz, seq, DIFF_W), BF16),
        compiler_params=_cparams(("arbitrary", "arbitrary")),
        name="diff_attn",
    )(qi, ki, rel_bias, z, z, zt, diff_lam, diff_norm_g.reshape(dv, 1))


def _out_proj_body(oc_ref, od_ref, x_ref, g1_ref, w_ref, o_ref):
    y = (jnp.dot(oc_ref[0], w_ref[0:C_WIDTH, :], preferred_element_type=F32)
         + jnp.dot(od_ref[0], w_ref[C_WIDTH:C_WIDTH + DIFF_W, :], preferred_element_type=F32))
    o_ref[0] = x_ref[0] + g1_ref[0] * y


def _out_proj(out_c, out_d, x, g1, w_out_bf16, tm):
    bsz, seq, d = x.shape
    return pl.pallas_call(
        _out_proj_body,
        grid=(bsz, seq // tm),
        in_specs=[pl.BlockSpec((1, tm, C_WIDTH), lambda b, i: (b, i, 0)),
                  pl.BlockSpec((1, tm, DIFF_W), lambda b, i: (b, i, 0)),
                  pl.BlockSpec((1, tm, d), lambda b, i: (b, i, 0)),
                  pl.BlockSpec((1, 1, d), lambda b, i: (b, 0, 0)),
                  pl.BlockSpec((C_WIDTH + DIFF_W, d), lambda b, i: (0, 0))],
        out_specs=pl.BlockSpec((1, tm, d), lambda b, i: (b, i, 0)),
        out_shape=jax.ShapeDtypeStruct((bsz, seq, d), F32),
        compiler_params=_cparams(("arbitrary", "arbitrary")),
        name="attn_out_proj",
    )(out_c, out_d, x, g1.reshape(bsz, 1, d), w_out_bf16)


def _attn_in_weights(cd_w_in):
    sizes = (C_WIDTH, C_WIDTH, C_WIDTH, IDX_HEADS * IDX_DIM, IDX_DIM, IDX_HEADS, DIFF_W, DIFF_W, DIFF_W)
    cuts = np.cumsum(sizes)[:-1]
    q_c, k_c, v_c, q_i, k_i, w_i, q_d, k_d, v_d = jnp.split(cd_w_in, cuts, axis=1)
    d = cd_w_in.shape[0]
    w = jnp.concatenate([q_c * (C_HEAD_DIM ** -0.5 * LOG2E), k_c, q_d * (DIFF_HEAD_DIM ** -0.5 * LOG2E), k_d,
                         q_i, k_i, jnp.zeros((d, LANES - IDX_DIM), cd_w_in.dtype)], axis=1)
    wt = jnp.concatenate([v_c, v_d, w_i * (IDX_DIM * IDX_HEADS) ** -0.5,
                          jnp.zeros((d, BF16_ROWS - IDX_HEADS), cd_w_in.dtype)], axis=1).T
    return w.astype(BF16), wt.astype(BF16)


def kernel(x, c, positions, rel_bias, norm_g, final_norm_g, ada_w, ada_b, ab_w_in, ab_conv_a, ab_conv_b,
           ab_conv_b_bias, ab_ln_g, ab_ln_b, ab_w_out, cd_w_in, diff_lam, diff_norm_g, cd_w_out,
           moe_wr_g, moe_br_g, moe_wr_e, moe_br_e, moe_w_gate, moe_w_up, moe_w_down):
    del positions
    bsz, seq, d = x.shape
    depth = ada_w.shape[0]
    tm = min(512, seq)
    ta = min(ATT_TILE, seq)
    assert ta >= MAX_DISTANCE and seq % ta == 0
    topk = min(TOPK_MAX, seq // 4)
    mods = _ada_mod(c, ada_w, ada_b)
    for i in range(depth):
        sh1, sc1, g1, sh2, sc2, g2 = jnp.split(mods[i], 6, axis=-1)
        j = i // 2
        if i % 2 == 0:
            z = _norm_proj(x, norm_g[i, 0], sh1, sc1, ab_w_in[j].astype(BF16), tm)
            x = _conv_mix(z, x, g1, ab_conv_a[j], ab_conv_b[j], ab_conv_b_bias[j], ab_ln_g[j], ab_ln_b[j],
                          ab_w_out[j].astype(BF16), min(256, seq))
        else:
            lambda_init = 0.8 - 0.6 * math.exp(-0.3 * i)
            w, wt = _attn_in_weights(cd_w_in[j])
            z, zt = _norm_proj(x, norm_g[i, 0], sh1, sc1, w, tm, wt)
            sel = _select(z, zt, topk)
            out_c = _cattn(z, zt, sel, rel_bias, ta)
            out_d = _dattn(z, zt, rel_bias, diff_lam[j], diff_norm_g[j], lambda_init, ta)
            x = _out_proj(out_c, out_d, x, g1, cd_w_out[j].astype(BF16), tm)
        x = _hier_moe(x, norm_g[i, 1], sh2, sc2, g2, moe_wr_g[i], moe_br_g[i], moe_wr_e[i], moe_br_e[i],
                      moe_w_gate[i], moe_w_up[i], moe_w_down[i], final_norm_g, final_norm=(i == depth - 1))
    return x
```

```python
import functools
import math

import numpy as np
import jax
import jax.numpy as jnp
from jax import lax
from jax.experimental import pallas as pl
from jax.experimental.pallas import tpu as pltpu

F32 = jnp.float32
BF16 = jnp.bfloat16
I32 = jnp.int32
HIGHEST = lax.Precision.HIGHEST

EPS = 1e-6
A_WIDTH = 512
A_CONV = 3
B_WIDTH = 512
B_CONV = 31
C_HEADS = 8
C_HEAD_DIM = 64
IDX_HEADS = 8
IDX_DIM = 32
TOPK_MAX = 256
DIFF_HEADS = 4
DIFF_HEAD_DIM = 64
NUM_BUCKETS = 32
MAX_DISTANCE = 128
N_GROUPS = 4
EXPERTS_PER_GROUP = 8
N_EXPERTS = N_GROUPS * EXPERTS_PER_GROUP
C_WIDTH = C_HEADS * C_HEAD_DIM
DIFF_W = DIFF_HEADS * 2 * DIFF_HEAD_DIM
LANES = 128
BF16_ROWS = 16
INT_MIN = -(2 ** 31)
LOG2E = math.log2(math.e)
VMEM_LIMIT = 56 * 1024 * 1024

Z_QC, Z_KC, Z_QD, Z_KD = 0, 1, 2, 3
Z_QIDX_OFF = 4 * C_WIDTH
Z_QIDX_W = IDX_HEADS * IDX_DIM
Z_KIDX_OFF = Z_QIDX_OFF + Z_QIDX_W
Z_COLS = Z_KIDX_OFF + LANES
ZT_VC, ZT_VD = 0, 1
ZT_W_OFF = C_WIDTH + DIFF_W
ZT_ROWS = ZT_W_OFF + BF16_ROWS


def _bucket_starts():
    n = np.arange(0, 2 * MAX_DISTANCE)
    me = NUM_BUCKETS // 2
    lr = np.log(np.maximum(n, 1) / me) / math.log(MAX_DISTANCE / me)
    large = me + (lr * (NUM_BUCKETS - me)).astype(np.int64)
    b = np.where(n < me, n, np.minimum(large, NUM_BUCKETS - 1))
    return [int(n[b >= k].min()) for k in range(NUM_BUCKETS)]


BUCKET_START = _bucket_starts()


def _cparams(sem):
    return pltpu.CompilerParams(dimension_semantics=sem, vmem_limit_bytes=VMEM_LIMIT)


def _rms(x):
    return x * lax.rsqrt(jnp.mean(x * x, axis=-1, keepdims=True) + EPS)


def _sigmoid(x):
    return 1.0 / (1.0 + jnp.exp(-x))


def _dot_nt(a, b):
    return lax.dot_general(a, b, (((1,), (1,)), ((), ())), preferred_element_type=F32)


def _ada_body(c_ref, w_ref, b_ref, o_ref):
    c = c_ref[...]
    cond = c * _sigmoid(c)
    o_ref[0] = jnp.dot(cond, w_ref[0], precision=HIGHEST, preferred_element_type=F32) + b_ref[0]


def _ada_mod(c, ada_w, ada_b):
    depth, d, n6 = ada_w.shape
    bsz = c.shape[0]
    rows = 8
    c_pad = jnp.zeros((rows, d), F32).at[:bsz].set(c)
    tn = 1536
    out = pl.pallas_call(
        _ada_body,
        grid=(depth, n6 // tn),
        in_specs=[pl.BlockSpec((rows, d), lambda i, j: (0, 0)),
                  pl.BlockSpec((1, d, tn), lambda i, j: (i, 0, j)),
                  pl.BlockSpec((1, 1, tn), lambda i, j: (i, 0, j))],
        out_specs=pl.BlockSpec((1, rows, tn), lambda i, j: (i, 0, j)),
        out_shape=jax.ShapeDtypeStruct((depth, rows, n6), F32),
        compiler_params=_cparams(("arbitrary", "arbitrary")),
        name="ada_mod",
    )(c_pad, ada_w, ada_b.reshape(depth, 1, n6))
    return out[:, :bsz]


def _norm_proj_body(x_ref, g_ref, sh_ref, sc_ref, w_ref, *rest):
    y = _rms(x_ref[0]) * g_ref[...]
    h = (y * (1.0 + sc_ref[0]) + sh_ref[0]).astype(BF16)
    if len(rest) == 1:
        (o_ref,) = rest
    else:
        wt_ref, o_ref, ot_ref = rest
        ot_ref[0] = _dot_nt(wt_ref[...], h).astype(ot_ref.dtype)
    o_ref[0] = jnp.dot(h, w_ref[...], preferred_element_type=F32).astype(o_ref.dtype)


def _norm_proj(x, g, sh, sc, w_bf16, tm, wt_bf16=None):
    bsz, seq, d = x.shape
    n = w_bf16.shape[1]
    in_specs = [pl.BlockSpec((1, tm, d), lambda b, i: (b, i, 0)),
                pl.BlockSpec((1, d), lambda b, i: (0, 0)),
                pl.BlockSpec((1, 1, d), lambda b, i: (b, 0, 0)),
                pl.BlockSpec((1, 1, d), lambda b, i: (b, 0, 0)),
                pl.BlockSpec((d, n), lambda b, i: (0, 0))]
    out_specs = pl.BlockSpec((1, tm, n), lambda b, i: (b, i, 0))
    out_shape = jax.ShapeDtypeStruct((bsz, seq, n), BF16)
    args = [x, g.reshape(1, d), sh.reshape(bsz, 1, d), sc.reshape(bsz, 1, d), w_bf16]
    if wt_bf16 is not None:
        nt = wt_bf16.shape[0]
        in_specs.append(pl.BlockSpec((nt, d), lambda b, i: (0, 0)))
        out_specs = [out_specs, pl.BlockSpec((1, nt, tm), lambda b, i: (b, 0, i))]
        out_shape = [out_shape, jax.ShapeDtypeStruct((bsz, nt, seq), BF16)]
        args.append(wt_bf16)
    return pl.pallas_call(
        _norm_proj_body,
        grid=(bsz, seq // tm),
        in_specs=in_specs,
        out_specs=out_specs,
        out_shape=out_shape,
        compiler_params=_cparams(("arbitrary", "arbitrary")),
        name="norm_proj",
    )(*args)


CONV_HALO = 32
CONV_ROWS = 64


def _conv_body(z_ref, x_ref, g1_ref, ca_ref, cb_ref, cbb_ref, lng_ref, lnb_ref, wo_ref, o_ref,
               ua_scr, ub_scr, y_scr, *, tl):
    l = pl.program_id(1)

    @pl.when(l == 0)
    def _():
        ua_scr[0:CONV_HALO, :] = jnp.zeros((CONV_HALO, A_WIDTH), F32)
        ub_scr[0:CONV_HALO, :] = jnp.zeros((CONV_HALO, B_WIDTH), F32)

    a = A_WIDTH
    gate_c = z_ref[0, :, a:2 * a].astype(F32)
    x_a = z_ref[0, :, 2 * a:3 * a].astype(F32)
    ua_scr[CONV_HALO:CONV_HALO + tl, :] = gate_c * x_a
    val_b = z_ref[0, :, 3 * a:3 * a + B_WIDTH].astype(F32)
    glu = z_ref[0, :, 3 * a + B_WIDTH:3 * a + 2 * B_WIDTH].astype(F32)
    ub_scr[CONV_HALO:CONV_HALO + tl, :] = val_b * _sigmoid(glu)

    for r in range(0, tl, CONV_ROWS):
        acc_a = None
        for k in range(A_CONV):
            tap = ua_scr[CONV_HALO + r - (A_CONV - 1) + k:CONV_HALO + r - (A_CONV - 1) + k + CONV_ROWS, :]
            term = tap * ca_ref[k:k + 1, :]
            acc_a = term if acc_a is None else acc_a + term
        gate_b = z_ref[0, r:r + CONV_ROWS, 0:a].astype(F32)
        y_scr[r:r + CONV_ROWS, 0:a] = (gate_b * acc_a).astype(BF16)

        acc_b = None
        for k in range(B_CONV):
            tap = ub_scr[CONV_HALO + r - (B_CONV - 1) + k:CONV_HALO + r - (B_CONV - 1) + k + CONV_ROWS, :]
            term = tap * cb_ref[k:k + 1, :]
            acc_b = term if acc_b is None else acc_b + term
        u = acc_b + cbb_ref[...]
        mu = jnp.mean(u, axis=-1, keepdims=True)
        uc = u - mu
        var = jnp.mean(uc * uc, axis=-1, keepdims=True)
        v = uc * lax.rsqrt(var + EPS) * lng_ref[...] + lnb_ref[...]
        y_scr[r:r + CONV_ROWS, a:a + B_WIDTH] = (v * _sigmoid(v)).astype(BF16)

    ua_scr[0:CONV_HALO, :] = ua_scr[tl:tl + CONV_HALO, :]
    ub_scr[0:CONV_HALO, :] = ub_scr[tl:tl + CONV_HALO, :]
    y = jnp.dot(y_scr[...], wo_ref[...], preferred_element_type=F32)
    o_ref[0] = x_ref[0] + g1_ref[0] * y


def _conv_mix(z, x, g1, conv_a, conv_b, conv_b_bias, ln_g, ln_b, w_out_bf16, tl):
    bsz, seq, d = x.shape
    nz = z.shape[-1]
    wide = A_WIDTH + B_WIDTH
    full = lambda shape: pl.BlockSpec(shape, lambda b, l: (0,) * len(shape))
    return pl.pallas_call(
        functools.partial(_conv_body, tl=tl),
        grid=(bsz, seq // tl),
        in_specs=[pl.BlockSpec((1, tl, nz), lambda b, l: (b, l, 0)),
                  pl.BlockSpec((1, tl, d), lambda b, l: (b, l, 0)),
                  pl.BlockSpec((1, 1, d), lambda b, l: (b, 0, 0)),
                  full((A_CONV, A_WIDTH)), full((B_CONV, B_WIDTH)), full((1, B_WIDTH)),
                  full((1, B_WIDTH)), full((1, B_WIDTH)), full((wide, d))],
        out_specs=pl.BlockSpec((1, tl, d), lambda b, l: (b, l, 0)),
        out_shape=jax.ShapeDtypeStruct((bsz, seq, d), F32),
        scratch_shapes=[pltpu.VMEM((CONV_HALO + tl, A_WIDTH), F32),
                        pltpu.VMEM((CONV_HALO + tl, B_WIDTH), F32),
                        pltpu.VMEM((tl, wide), BF16)],
        compiler_params=_cparams(("arbitrary", "arbitrary")),
        name="conv_mix",
    )(z, x, g1.reshape(bsz, 1, d), conv_a, conv_b, conv_b_bias.reshape(1, -1), ln_g.reshape(1, -1),
      ln_b.reshape(1, -1), w_out_bf16)


MOE_ROWS = 256
ROUTER_ROWS = 512
COMBINE_ROWS = 256
META_GATE0, META_GATE1, META_E0, META_E1, META_RANK0, META_RANK1 = range(6)
GROUP_LANE0 = N_EXPERTS


def _router_body(x_ref, g_ref, sh_ref, sc_ref, wr_ref, br_ref, tri_ref, h_ref, meta_ref, cnt_ref, base_scr):
    @pl.when(pl.program_id(0) == 0)
    def _():
        base_scr[...] = jnp.zeros_like(base_scr)

    h = _rms(x_ref[...]) * g_ref[...]
    h = h * (1.0 + sc_ref[0]) + sh_ref[0]
    h_ref[...] = h
    logits = jnp.dot(h, wr_ref[...], precision=HIGHEST, preferred_element_type=F32) + br_ref[...]
    tr = logits.shape[0]
    lane = lax.broadcasted_iota(I32, (tr, LANES), 1)
    lane_f = lane.astype(F32)
    neg = jnp.float32(-jnp.inf)
    big = jnp.float32(1e9)

    is_group = (lane >= GROUP_LANE0) & (lane < GROUP_LANE0 + N_GROUPS)
    glog = jnp.where(is_group, logits, neg)
    gmax = jnp.max(glog, axis=1, keepdims=True)
    p_top = 1.0 / jnp.sum(jnp.exp(glog - gmax), axis=1, keepdims=True)
    g_sel = jnp.min(jnp.where(glog == gmax, lane_f, big), axis=1, keepdims=True) - GROUP_LANE0
    lo = g_sel * EXPERTS_PER_GROUP
    in_group = (lane_f >= lo) & (lane_f < lo + EXPERTS_PER_GROUP)
    f1 = jnp.where(in_group, logits, neg)
    v1 = jnp.max(f1, axis=1, keepdims=True)
    i1 = jnp.min(jnp.where(f1 == v1, lane_f, big), axis=1, keepdims=True)
    f2 = jnp.where(lane_f == i1, neg, f1)
    v2 = jnp.max(f2, axis=1, keepdims=True)
    i2 = jnp.min(jnp.where(f2 == v2, lane_f, big), axis=1, keepdims=True)
    a = jnp.exp(v2 - v1)
    w1 = 1.0 / (1.0 + a)
    gate0 = p_top * w1
    gate1 = p_top * (a * w1)

    oh0 = lane_f == i1
    oh1 = lane_f == i2
    ind0 = jnp.where(oh0, 1.0, 0.0)
    ind1 = jnp.where(oh1, 1.0, 0.0)
    pre0 = jnp.dot(tri_ref[...], ind0.astype(BF16), preferred_element_type=F32)
    pre1 = jnp.dot(tri_ref[...], ind1.astype(BF16), preferred_element_type=F32)
    tot0 = jnp.sum(ind0, axis=0, keepdims=True)
    tot1 = jnp.sum(ind1, axis=0, keepdims=True)
    base = base_scr[...]
    rank0 = jnp.sum(jnp.where(oh0, base + pre0, 0.0), axis=1, keepdims=True)
    rank1 = jnp.sum(jnp.where(oh1, base + tot0 + pre1, 0.0), axis=1, keepdims=True)
    new_base = base + tot0 + tot1
    base_scr[...] = new_base
    cnt_ref[...] = new_base

    meta = jnp.zeros((tr, LANES), F32)
    for col, val in ((META_GATE0, gate0), (META_GATE1, gate1), (META_E0, i1), (META_E1, i2),
                     (META_RANK0, rank0), (META_RANK1, rank1)):
        meta = jnp.where(lane == col, val, meta)
    meta_ref[...] = meta


def _moe_router(x2, g, sh, sc, wr_g, br_g, wr_e, br_e, seq):
    n_tok, d = x2.shape
    bsz = n_tok // seq
    tr = min(ROUTER_ROWS, seq)
    steps_per_batch = seq // tr
    wr = jnp.zeros((d, LANES), F32).at[:, :N_EXPERTS].set(wr_e).at[:, GROUP_LANE0:GROUP_LANE0 + N_GROUPS].set(wr_g)
    br = jnp.zeros((1, LANES), F32).at[0, :N_EXPERTS].set(br_e).at[0, GROUP_LANE0:GROUP_LANE0 + N_GROUPS].set(br_g)
    tri = jnp.tril(jnp.ones((tr, tr), BF16), -1)
    full = lambda shape: pl.BlockSpec(shape, lambda i: (0,) * len(shape))
    return pl.pallas_call(
        _router_body,
        grid=(n_tok // tr,),
        in_specs=[pl.BlockSpec((tr, d), lambda i: (i, 0)),
                  full((1, d)),
                  pl.BlockSpec((1, 1, d), lambda i: (i // steps_per_batch, 0, 0)),
                  pl.BlockSpec((1, 1, d), lambda i: (i // steps_per_batch, 0, 0)),
                  full((d, LANES)), full((1, LANES)), full((tr, tr))],
        out_specs=[pl.BlockSpec((tr, d), lambda i: (i, 0)),
                   pl.BlockSpec((tr, LANES), lambda i: (i, 0)),
                   full((1, LANES))],
        out_shape=[jax.ShapeDtypeStruct((n_tok, d), F32),
                   jax.ShapeDtypeStruct((n_tok, LANES), F32),
                   jax.ShapeDtypeStruct((1, LANES), F32)],
        scratch_shapes=[pltpu.VMEM((1, LANES), F32)],
        compiler_params=_cparams(("arbitrary",)),
        name="moe_router",
    )(x2, g.reshape(1, d), sh.reshape(bsz, 1, d), sc.reshape(bsz, 1, d), wr, br, tri)


def _dispatch_body(dest_ref, h_ref, xr_ref, sem, *, tr):
    def row_copy(t, k):
        return pltpu.make_async_copy(h_ref.at[pl.ds(t, 1)], xr_ref.at[pl.ds(dest_ref[0, 0, k * tr + t], 1)], sem)

    def start(t, carry):
        row_copy(t, 0).start()
        row_copy(t, 1).start()
        return carry

    lax.fori_loop(0, tr, start, 0)

    def wait(t, carry):
        row_copy(t, 0).wait()
        row_copy(t, 1).wait()
        return carry

    lax.fori_loop(0, tr, wait, 0)


def _moe_dispatch(h2, dest_blocks, tr):
    n_tok, d = h2.shape
    return pl.pallas_call(
        functools.partial(_dispatch_body, tr=tr),
        grid=(n_tok // tr,),
        in_specs=[pl.BlockSpec((1, 1, 2 * tr), lambda i: (i, 0, 0), memory_space=pltpu.SMEM),
                  pl.BlockSpec((tr, d), lambda i: (i, 0))],
        out_specs=pl.BlockSpec(memory_space=pl.ANY),
        out_shape=jax.ShapeDtypeStruct((2 * n_tok, d), F32),
        scratch_shapes=[pltpu.SemaphoreType.DMA(())],
        compiler_params=_cparams(("arbitrary",)),
        name="moe_dispatch",
    )(dest_blocks, h2)


def _expert_body(pb_ref, pe_ref, plo_ref, phi_ref, x_ref, wg_ref, wu_ref, wd_ref, o_ref, wg_s, wu_s, wd_s):
    p = pl.program_id(0)
    prev = jnp.maximum(p - 1, 0)
    new_expert = (p == 0) | (pe_ref[p] != pe_ref[prev])
    first = (p == 0) | (pb_ref[p] != pb_ref[prev])

    @pl.when(new_expert)
    def _():
        wg_s[...] = wg_ref[0].astype(BF16)
        wu_s[...] = wu_ref[0].astype(BF16)
        wd_s[...] = wd_ref[0].astype(BF16)

    x = x_ref[...].astype(BF16)
    gt = jnp.dot(x, wg_s[...], preferred_element_type=F32)
    up = jnp.dot(x, wu_s[...], preferred_element_type=F32)
    act = (gt * _sigmoid(gt)) * up
    y = jnp.dot(act.astype(BF16), wd_s[...], preferred_element_type=F32)
    rows = lax.broadcasted_iota(I32, (y.shape[0], 1), 0)
    y = jnp.where((rows >= plo_ref[p]) & (rows < phi_ref[p]), y, 0.0)

    @pl.when(first)
    def _():
        o_ref[...] = y

    @pl.when(jnp.logical_not(first))
    def _():
        o_ref[...] += y


def _moe_experts(x_rows, pairs, w_gate, w_up, w_down):
    n_rows, d = x_rows.shape
    de = w_gate.shape[-1]
    n_pairs = pairs[0].shape[0]
    grid_spec = pltpu.PrefetchScalarGridSpec(
        num_scalar_prefetch=4,
        grid=(n_pairs,),
        in_specs=[pl.BlockSpec((MOE_ROWS, d), lambda p, pb, pe, lo, hi: (pb[p], 0)),
                  pl.BlockSpec((1, d, de), lambda p, pb, pe, lo, hi: (pe[p], 0, 0)),
                  pl.BlockSpec((1, d, de), lambda p, pb, pe, lo, hi: (pe[p], 0, 0)),
                  pl.BlockSpec((1, de, d), lambda p, pb, pe, lo, hi: (pe[p], 0, 0))],
        out_specs=pl.BlockSpec((MOE_ROWS, d), lambda p, pb, pe, lo, hi: (pb[p], 0)),
        scratch_shapes=[pltpu.VMEM((d, de), BF16), pltpu.VMEM((d, de), BF16), pltpu.VMEM((de, d), BF16)],
    )
    return pl.pallas_call(
        _expert_body,
        grid_spec=grid_spec,
        out_shape=jax.ShapeDtypeStruct((n_rows, d), F32),
        compiler_params=_cparams(("arbitrary",)),
        name="moe_experts",
    )(*pairs, x_rows, w_gate, w_up, w_down)


def _combine_body(dest_ref, y_ref, x_ref, meta_ref, g2_ref, fg_ref, o_ref, r0, r1, sem, *, tc, final_norm):
    def row_copy(t, k, buf):
        return pltpu.make_async_copy(y_ref.at[pl.ds(dest_ref[0, 0, k * tc + t], 1)], buf.at[pl.ds(t, 1)], sem)

    def start(t, carry):
        row_copy(t, 0, r0).start()
        row_copy(t, 1, r1).start()
        return carry

    lax.fori_loop(0, tc, start, 0)

    def wait(t, carry):
        row_copy(t, 0, r0).wait()
        row_copy(t, 1, r1).wait()
        return carry

    lax.fori_loop(0, tc, wait, 0)
    meta = meta_ref[...]
    moe = meta[:, META_GATE0:META_GATE0 + 1] * r0[...] + meta[:, META_GATE1:META_GATE1 + 1] * r1[...]
    xn = x_ref[...] + g2_ref[0] * moe
    if final_norm:
        xn = _rms(xn) * fg_ref[...]
    o_ref[...] = xn


def _moe_combine(y_rows, dest_blocks, x2, meta, g2, final_g, seq, tc, final_norm):
    n_tok, d = x2.shape
    bsz = n_tok // seq
    steps_per_batch = seq // tc
    return pl.pallas_call(
        functools.partial(_combine_body, tc=tc, final_norm=final_norm),
        grid=(n_tok // tc,),
        in_specs=[pl.BlockSpec((1, 1, 2 * tc), lambda i: (i, 0, 0), memory_space=pltpu.SMEM),
                  pl.BlockSpec(memory_space=pl.ANY),
                  pl.BlockSpec((tc, d), lambda i: (i, 0)),
                  pl.BlockSpec((tc, LANES), lambda i: (i, 0)),
                  pl.BlockSpec((1, 1, d), lambda i: (i // steps_per_batch, 0, 0)),
                  pl.BlockSpec((1, d), lambda i: (0, 0))],
        out_specs=pl.BlockSpec((tc, d), lambda i: (i, 0)),
        out_shape=jax.ShapeDtypeStruct((n_tok, d), F32),
        scratch_shapes=[pltpu.VMEM((tc, d), F32), pltpu.VMEM((tc, d), F32), pltpu.SemaphoreType.DMA(())],
        compiler_params=_cparams(("arbitrary",)),
        name="moe_combine",
    )(dest_blocks, y_rows, x2, meta, g2.reshape(bsz, 1, d), final_g.reshape(1, d))


def _blocked_dest(dest0, dest1, rows):
    nb = dest0.shape[0] // rows
    return jnp.concatenate([dest0.reshape(nb, rows), dest1.reshape(nb, rows)], axis=1).reshape(nb, 1, 2 * rows)


def _expert_pairs(counts, n_rows):
    n_blocks = n_rows // MOE_ROWS
    n_pairs = n_blocks + N_EXPERTS
    ends = jnp.cumsum(counts)
    starts = ends - counts
    first_blk = starts // MOE_ROWS
    last_blk = (ends - 1) // MOE_ROWS
    npairs = jnp.where(counts > 0, last_blk - first_blk + 1, 0)
    pend = jnp.cumsum(npairs)
    poff = pend - npairs
    total = pend[-1]
    p = jnp.arange(n_pairs, dtype=I32)
    p_eff = jnp.minimum(p, total - 1)
    e = jnp.minimum(jnp.searchsorted(pend, p_eff, side='right'), N_EXPERTS - 1).astype(I32)
    blk = (first_blk[e] + p_eff - poff[e]).astype(I32)
    valid = p < total
    lo = jnp.where(valid, jnp.clip(starts[e] - blk * MOE_ROWS, 0, MOE_ROWS), 0).astype(I32)
    hi = jnp.where(valid, jnp.clip(ends[e] - blk * MOE_ROWS, 0, MOE_ROWS), 0).astype(I32)
    return blk, e, lo, hi


def _hier_moe(x, g, sh, sc, g2, wr_g, br_g, wr_e, br_e, w_gate, w_up, w_down, final_g, final_norm):
    bsz, seq, d = x.shape
    n_tok = bsz * seq
    x2 = x.reshape(n_tok, d)
    h2, meta, cnt = _moe_router(x2, g, sh, sc, wr_g, br_g, wr_e, br_e, seq)
    counts = cnt[0, :N_EXPERTS].astype(I32)
    starts = jnp.cumsum(counts) - counts
    e0 = meta[:, META_E0].astype(I32)
    e1 = meta[:, META_E1].astype(I32)
    dest0 = starts[e0] + meta[:, META_RANK0].astype(I32)
    dest1 = starts[e1] + meta[:, META_RANK1].astype(I32)
    tr = min(ROUTER_ROWS, seq)
    tc = min(COMBINE_ROWS, seq)
    x_rows = _moe_dispatch(h2, _blocked_dest(dest0, dest1, tr), tr)
    y_rows = _moe_experts(x_rows, _expert_pairs(counts, 2 * n_tok), w_gate, w_up, w_down)
    out = _moe_combine(y_rows, _blocked_dest(dest0, dest1, tc), x2, meta, g2, final_g, seq, tc, final_norm)
    return out.reshape(bsz, seq, d)


SEL_COLS = 256
SEL_CHUNK = 512
SEL_FOLD = 32
ATT_TILE = 512


def _sortable(score):
    bits = lax.bitcast_convert_type(score, I32)
    return bits ^ ((bits >> 31) & jnp.int32(0x7FFFFFFF))


def _idx_queries(zq):
    zf = zq.astype(F32)
    return [zf[:, h * IDX_DIM:(h + 1) * IDX_DIM].astype(BF16) for h in range(IDX_HEADS)]


def _idx_score_t(k, qs, wt):
    acc = None
    for h, q in enumerate(qs):
        term = jnp.maximum(_dot_nt(k, q), 0.0) * wt[h:h + 1, :]
        acc = term if acc is None else acc + term
    return acc


def _select_body(zq_ref, zk_ref, wt_ref, o_ref, keys_scr, cap_scr, *, topk):
    i = pl.program_id(1)
    tq = zq_ref.shape[1]
    qs = _idx_queries(zq_ref[0])
    wt = wt_ref[0].astype(F32)
    n_chunks = ((i + 1) * tq + SEL_CHUNK - 1) // SEL_CHUNK
    krow0 = lax.broadcasted_iota(I32, (SEL_CHUNK, tq), 0)
    qcol = i * tq + lax.broadcasted_iota(I32, (SEL_CHUNK, tq), 1)
    int_min = jnp.int32(INT_MIN)

    def fill(c, carry):
        off = pl.multiple_of(c * SEL_CHUNK, SEL_CHUNK)
        k = zk_ref[0, pl.ds(off, SEL_CHUNK), :][:, 0:IDX_DIM]
        key = _sortable(_idx_score_t(k, qs, wt))
        keys_scr[pl.ds(off, SEL_CHUNK), :] = jnp.where(krow0 + off <= qcol, key, int_min)
        return carry

    lax.fori_loop(0, n_chunks, fill, 0)

    def count(pred):
        def body(c, acc):
            off = pl.multiple_of(c * SEL_CHUNK, SEL_CHUNK)
            ind = jnp.where(pred(keys_scr[pl.ds(off, SEL_CHUNK), :], krow0 + off), 1, 0)
            return acc + jnp.sum(ind.reshape(SEL_CHUNK // SEL_FOLD, SEL_FOLD, tq), axis=0)
        acc = lax.fori_loop(0, n_chunks, body, jnp.zeros((SEL_FOLD, tq), I32))
        return jnp.sum(acc.astype(F32), axis=0, keepdims=True)

    k_f = jnp.float32(topk)

    def bit_step(ib, u):
        cand_u = u | jnp.left_shift(jnp.int32(1), 31 - ib)
        cand = cand_u ^ int_min
        cnt = count(lambda keys, kidx: keys >= cand)
        return jnp.where(cnt >= k_f, cand_u, u)

    u = lax.fori_loop(0, 32, bit_step, jnp.zeros((1, tq), I32))
    thr = jnp.maximum(u ^ int_min, int_min + 1)

    cap_scr[...] = jnp.full((1, tq), 2 ** 30, I32)
    n_ge = count(lambda keys, kidx: keys >= thr)

    @pl.when(jnp.max(n_ge) > k_f)
    def _():
        n_gt = count(lambda keys, kidx: keys > thr)
        need = k_f - n_gt

        def cap_step(ib, v):
            cand = v | jnp.left_shift(jnp.int32(1), 14 - ib)
            cnt = count(lambda keys, kidx: (keys == thr) & (kidx < cand))
            return jnp.where(cnt <= need, cand, v)

        cap_scr[...] = lax.fori_loop(0, 15, cap_step, jnp.zeros((1, tq), I32))

    cap = cap_scr[...]

    def emit(c, carry):
        off = pl.multiple_of(c * SEL_CHUNK, SEL_CHUNK)
        keys = keys_scr[pl.ds(off, SEL_CHUNK), :]
        keep = (keys > thr) | ((keys == thr) & (krow0 + off < cap))
        o_ref[0, pl.ds(off, SEL_CHUNK), :] = jnp.where(keep, 0.0, -jnp.inf).astype(o_ref.dtype)
        return carry

    lax.fori_loop(0, n_chunks, emit, 0)

    def blank(c, carry):
        off = pl.multiple_of(c * SEL_CHUNK, SEL_CHUNK)
        o_ref[0, pl.ds(off, SEL_CHUNK), :] = jnp.full((SEL_CHUNK, tq), -jnp.inf, o_ref.dtype)
        return carry

    lax.fori_loop(n_chunks, o_ref.shape[1] // SEL_CHUNK, blank, 0)


def _select(z, zt, topk):
    bsz, seq, _ = z.shape
    tq = min(SEL_COLS, seq)
    assert seq % SEL_CHUNK == 0 and seq % tq == 0
    return pl.pallas_call(
        functools.partial(_select_body, topk=topk),
        grid=(bsz, seq // tq),
        in_specs=[pl.BlockSpec((1, tq, Z_QIDX_W), lambda b, i: (b, i, Z_QIDX_OFF // Z_QIDX_W)),
                  pl.BlockSpec((1, seq, LANES), lambda b, i: (b, 0, Z_KIDX_OFF // LANES)),
                  pl.BlockSpec((1, BF16_ROWS, tq), lambda b, i: (b, ZT_W_OFF // BF16_ROWS, i))],
        out_specs=pl.BlockSpec((1, seq, tq), lambda b, i: (b, 0, i)),
        out_shape=jax.ShapeDtypeStruct((bsz, seq, seq), BF16),
        scratch_shapes=[pltpu.VMEM((seq, tq), I32), pltpu.VMEM((1, tq), I32)],
        compiler_params=_cparams(("arbitrary", "arbitrary")),
        name="dsa_select",
    )(z, z, zt)


BIAS_INIT_ROWS = 8
ONES_ROWS = BF16_ROWS


def _init_bias_tiles(btile, bias_ref, head0, n_heads, ta):
    col = lax.broadcasted_iota(I32, (BIAS_INIT_ROWS, ta), 1)
    row0 = lax.broadcasted_iota(I32, (BIAS_INIT_ROWS, ta), 0)

    def body(r, carry):
        off = pl.multiple_of(r * BIAS_INIT_ROWS, BIAS_INIT_ROWS)
        for kind in range(2):
            dist = col - (row0 + off) + kind * ta
            for h in range(n_heads):
                far = bias_ref[NUM_BUCKETS - 1, head0 + h]
                val = jnp.full((BIAS_INIT_ROWS, ta), (bias_ref[0, head0 + h] - far) * LOG2E, F32)
                for b in range(1, NUM_BUCKETS - 1):
                    val = jnp.where(dist >= BUCKET_START[b], (bias_ref[b, head0 + h] - far) * LOG2E, val)
                val = jnp.where(dist >= BUCKET_START[NUM_BUCKETS - 1], 0.0, val)
                btile[h, kind, pl.ds(off, BIAS_INIT_ROWS), :] = val
        return carry

    lax.fori_loop(0, ta // BIAS_INIT_ROWS, body, 0)


def _with_ones(vt):
    return jnp.concatenate([vt, jnp.ones((ONES_ROWS, vt.shape[1]), vt.dtype)], axis=0)


def _softmax_step_t(logits, v_aug, m_ref, acc_ref, idx):
    m_old = m_ref[idx]
    m_new = jnp.maximum(m_old, jnp.max(logits, axis=0, keepdims=True))
    m_safe = jnp.where(m_new == -jnp.inf, 0.0, m_new)
    p = jnp.exp2(logits - m_safe)
    alpha = jnp.exp2(m_old - m_safe)
    acc_ref[idx] = alpha * acc_ref[idx] + jnp.dot(v_aug, p.astype(BF16), preferred_element_type=F32)
    m_ref[idx] = m_new


def _reset_softmax(m_s, acc_s):
    m_s[...] = jnp.full(m_s.shape, -jnp.inf, F32)
    acc_s[...] = jnp.zeros(acc_s.shape, F32)


def _cattn_body(qi_ref, ki_ref, bias_ref, zq_ref, zk_ref, vt_ref, mask_ref, o_ref, m_s, acc_s, btile):
    b = pl.program_id(0)
    p = pl.program_id(1)
    qi = qi_ref[p]
    ki = ki_ref[p]
    ta = zq_ref.shape[1]
    hd = C_HEAD_DIM

    @pl.when((b == 0) & (p == 0))
    def _():
        _init_bias_tiles(btile, bias_ref, 0, C_HEADS, ta)

    @pl.when(ki == 0)
    def _():
        _reset_softmax(m_s, acc_s)

    def heads(extra_of_head):
        for h in range(C_HEADS):
            lo = h * hd
            logits = _dot_nt(zk_ref[0, :, lo:lo + hd], zq_ref[0, :, lo:lo + hd]) + extra_of_head(h)
            _softmax_step_t(logits, _with_ones(vt_ref[0, lo:lo + hd, :]), m_s, acc_s, h)

    near = ki >= qi - 1

    @pl.when(near)
    def _():
        heads(lambda h: mask_ref[0].astype(F32) + btile[h, qi - ki])

    @pl.when(jnp.logical_not(near))
    def _():
        heads(lambda h: mask_ref[0].astype(F32))

    @pl.when(ki == qi)
    def _():
        outs = []
        for h in range(C_HEADS):
            a = acc_s[h]
            outs.append((a[0:hd] / a[hd:hd + 1]).T)
        o_ref[0] = jnp.concatenate(outs, axis=1).astype(o_ref.dtype)


def _dattn_body(qi_ref, ki_ref, bias_ref, zq_ref, zk_ref, vt_ref, lam_ref, ng_ref, o_ref,
                m_s, acc_s, btile, *, lambda_init):
    b = pl.program_id(0)
    p = pl.program_id(1)
    qi = qi_ref[p]
    ki = ki_ref[p]
    ta = zq_ref.shape[1]
    hd = DIFF_HEAD_DIM
    dv = 2 * hd

    @pl.when((b == 0) & (p == 0))
    def _():
        _init_bias_tiles(btile, bias_ref, C_HEADS, DIFF_HEADS, ta)

    @pl.when(ki == 0)
    def _():
        _reset_softmax(m_s, acc_s)

    def heads(extra_of_head):
        for h in range(DIFF_HEADS):
            v_aug = _with_ones(vt_ref[0, dv * h:dv * (h + 1), :])
            extra = extra_of_head(h)
            for j in range(2):
                lo = (2 * h + j) * hd
                logits = _dot_nt(zk_ref[0, :, lo:lo + hd], zq_ref[0, :, lo:lo + hd])
                if extra is not None:
                    logits = logits + extra
                _softmax_step_t(logits, v_aug, m_s, acc_s, 2 * h + j)

    near = ki >= qi - 1

    @pl.when(near)
    def _():
        krow = ki * ta + lax.broadcasted_iota(I32, (ta, ta), 0)
        qcol = qi * ta + lax.broadcasted_iota(I32, (ta, ta), 1)
        heads(lambda h: jnp.where(krow <= qcol, btile[h, qi - ki], -jnp.inf))

    @pl.when(jnp.logical_not(near))
    def _():
        heads(lambda h: None)

    @pl.when(ki == qi)
    def _():
        lam_p = lam_ref[...]
        lam = (jnp.exp(jnp.sum(lam_p[0:1] * lam_p[1:2], axis=1, keepdims=True))
               - jnp.exp(jnp.sum(lam_p[2:3] * lam_p[3:4], axis=1, keepdims=True)) + lambda_init)
        outs = []
        for h in range(DIFF_HEADS):
            a1 = acc_s[2 * h]
            a2 = acc_s[2 * h + 1]
            o = a1[0:dv] / a1[dv:dv + 1] - lam * (a2[0:dv] / a2[dv:dv + 1])
            o = o * lax.rsqrt(jnp.mean(o * o, axis=0, keepdims=True) + EPS) * ng_ref[...] * (1.0 - lambda_init)
            outs.append(o.T)
        o_ref[0] = jnp.concatenate(outs, axis=1).astype(o_ref.dtype)


def _causal_pairs(nq):
    qi = [q for q in range(nq) for _ in range(q + 1)]
    ki = [k for q in range(nq) for k in range(q + 1)]
    return jnp.asarray(qi, I32), jnp.asarray(ki, I32)


def _cattn(z, zt, mask, rel_bias, ta):
    bsz, seq, _ = z.shape
    qi, ki = _causal_pairs(seq // ta)
    qmap = lambda col: (lambda b, p, qi, ki: (b, qi[p], col))
    kmap = lambda col: (lambda b, p, qi, ki: (b, ki[p], col))
    dv_aug = C_HEAD_DIM + ONES_ROWS
    grid_spec = pltpu.PrefetchScalarGridSpec(
        num_scalar_prefetch=2,
        grid=(bsz, qi.shape[0]),
        in_specs=[pl.BlockSpec(memory_space=pltpu.SMEM),
                  pl.BlockSpec((1, ta, C_WIDTH), qmap(Z_QC)),
                  pl.BlockSpec((1, ta, C_WIDTH), kmap(Z_KC)),
                  pl.BlockSpec((1, C_WIDTH, ta), lambda b, p, qi, ki: (b, ZT_VC, ki[p])),
                  pl.BlockSpec((1, ta, ta), lambda b, p, qi, ki: (b, ki[p], qi[p]))],
        out_specs=pl.BlockSpec((1, ta, C_WIDTH), qmap(0)),
        scratch_shapes=[pltpu.VMEM((C_HEADS, 1, ta), F32),
                        pltpu.VMEM((C_HEADS, dv_aug, ta), F32),
                        pltpu.VMEM((C_HEADS, 2, ta, ta), F32)],
    )
    return pl.pallas_call(
        _cattn_body,
        grid_spec=grid_spec,
        out_shape=jax.ShapeDtypeStruct((bsz, seq, C_WIDTH), BF16),
        compiler_params=_cparams(("arbitrary", "arbitrary")),
        name="dsa_attn",
    )(qi, ki, rel_bias, z, z, zt, mask)


def _dattn(z, zt, rel_bias, diff_lam, diff_norm_g, lambda_init, ta):
    bsz, seq, _ = z.shape
    qi, ki = _causal_pairs(seq // ta)
    n_maps = 2 * DIFF_HEADS
    dv = 2 * DIFF_HEAD_DIM
    qmap = lambda col: (lambda b, p, qi, ki: (b, qi[p], col))
    kmap = lambda col: (lambda b, p, qi, ki: (b, ki[p], col))
    grid_spec = pltpu.PrefetchScalarGridSpec(
        num_scalar_prefetch=2,
        grid=(bsz, qi.shape[0]),
        in_specs=[pl.BlockSpec(memory_space=pltpu.SMEM),
                  pl.BlockSpec((1, ta, DIFF_W), qmap(Z_QD)),
                  pl.BlockSpec((1, ta, DIFF_W), kmap(Z_KD)),
                  pl.BlockSpec((1, DIFF_W, ta), lambda b, p, qi, ki: (b, ZT_VD, ki[p])),
                  pl.BlockSpec(diff_lam.shape, lambda b, p, qi, ki: (0, 0)),
                  pl.BlockSpec((dv, 1), lambda b, p, qi, ki: (0, 0))],
        out_specs=pl.BlockSpec((1, ta, DIFF_W), qmap(0)),
        scratch_shapes=[pltpu.VMEM((n_maps, 1, ta), F32),
                        pltpu.VMEM((n_maps, dv + ONES_ROWS, ta), F32),
                        pltpu.VMEM((DIFF_HEADS, 2, ta, ta), F32)],
    )
    return pl.pallas_call(
        functools.partial(_dattn_body, lambda_init=lambda_init),
        grid_spec=grid_spec,
        out_shape=jax.ShapeDtypeStruct((bsz, seq, DIFF_W), BF16),
        compiler_params=_cparams(("arbitrary", "arbitrary")),
        name="diff_attn",
    )(qi, ki, rel_bias, z, z, zt, diff_lam, diff_norm_g.reshape(dv, 1))


def _out_proj_body(oc_ref, od_ref, x_ref, g1_ref, w_ref, o_ref):
    y = (jnp.dot(oc_ref[0], w_ref[0:C_WIDTH, :], preferred_element_type=F32)
         + jnp.dot(od_ref[0], w_ref[C_WIDTH:C_WIDTH + DIFF_W, :], preferred_element_type=F32))
    o_ref[0] = x_ref[0] + g1_ref[0] * y


def _out_proj(out_c, out_d, x, g1, w_out_bf16, tm):
    bsz, seq, d = x.shape
    return pl.pallas_call(
        _out_proj_body,
        grid=(bsz, seq // tm),
        in_specs=[pl.BlockSpec((1, tm, C_WIDTH), lambda b, i: (b, i, 0)),
                  pl.BlockSpec((1, tm, DIFF_W), lambda b, i: (b, i, 0)),
                  pl.BlockSpec((1, tm, d), lambda b, i: (b, i, 0)),
                  pl.BlockSpec((1, 1, d), lambda b, i: (b, 0, 0)),
                  pl.BlockSpec((C_WIDTH + DIFF_W, d), lambda b, i: (0, 0))],
        out_specs=pl.BlockSpec((1, tm, d), lambda b, i: (b, i, 0)),
        out_shape=jax.ShapeDtypeStruct((bsz, seq, d), F32),
        compiler_params=_cparams(("arbitrary", "arbitrary")),
        name="attn_out_proj",
    )(out_c, out_d, x, g1.reshape(bsz, 1, d), w_out_bf16)


def _attn_in_weights(cd_w_in):
    sizes = (C_WIDTH, C_WIDTH, C_WIDTH, IDX_HEADS * IDX_DIM, IDX_DIM, IDX_HEADS, DIFF_W, DIFF_W, DIFF_W)
    cuts = np.cumsum(sizes)[:-1]
    q_c, k_c, v_c, q_i, k_i, w_i, q_d, k_d, v_d = jnp.split(cd_w_in, cuts, axis=1)
    d = cd_w_in.shape[0]
    w = jnp.concatenate([q_c * (C_HEAD_DIM ** -0.5 * LOG2E), k_c, q_d * (DIFF_HEAD_DIM ** -0.5 * LOG2E), k_d,
                         q_i, k_i, jnp.zeros((d, LANES - IDX_DIM), cd_w_in.dtype)], axis=1)
    wt = jnp.concatenate([v_c, v_d, w_i * (IDX_DIM * IDX_HEADS) ** -0.5,
                          jnp.zeros((d, BF16_ROWS - IDX_HEADS), cd_w_in.dtype)], axis=1).T
    return w.astype(BF16), wt.astype(BF16)


def kernel(x, c, positions, rel_bias, norm_g, final_norm_g, ada_w, ada_b, ab_w_in, ab_conv_a, ab_conv_b,
           ab_conv_b_bias, ab_ln_g, ab_ln_b, ab_w_out, cd_w_in, diff_lam, diff_norm_g, cd_w_out,
           moe_wr_g, moe_br_g, moe_wr_e, moe_br_e, moe_w_gate, moe_w_up, moe_w_down):
    del positions
    bsz, seq, d = x.shape
    depth = ada_w.shape[0]
    tm = min(512, seq)
    ta = min(ATT_TILE, seq)
    assert ta >= MAX_DISTANCE and seq % ta == 0
    topk = min(TOPK_MAX, seq // 4)
    mods = _ada_mod(c, ada_w, ada_b)
    for i in range(depth):
        sh1, sc1, g1, sh2, sc2, g2 = jnp.split(mods[i], 6, axis=-1)
        j = i // 2
        if i % 2 == 0:
            z = _norm_proj(x, norm_g[i, 0], sh1, sc1, ab_w_in[j].astype(BF16), tm)
            x = _conv_mix(z, x, g1, ab_conv_a[j], ab_conv_b[j], ab_conv_b_bias[j], ab_ln_g[j], ab_ln_b[j],
                          ab_w_out[j].astype(BF16), min(256, seq))
        else:
            lambda_init = 0.8 - 0.6 * math.exp(-0.3 * i)
            w, wt = _attn_in_weights(cd_w_in[j])
            z, zt = _norm_proj(x, norm_g[i, 0], sh1, sc1, w, tm, wt)
            mask = _select(z, zt, topk)
            out_c = _cattn(z, zt, mask, rel_bias, ta)
            out_d = _dattn(z, zt, rel_bias, diff_lam[j], diff_norm_g[j], lambda_init, ta)
            x = _out_proj(out_c, out_d, x, g1, cd_w_out[j].astype(BF16), tm)
        x = _hier_moe(x, norm_g[i, 1], sh2, sc2, g2, moe_wr_g[i], moe_br_g[i], moe_wr_e[i], moe_br_e[i],
                      moe_w_gate[i], moe_w_up[i], moe_w_down[i], final_norm_g, final_norm=(i == depth - 1))
    return x
```

```python
import functools
import math

import numpy as np
import jax
import jax.numpy as jnp
from jax import lax
from jax.experimental import pallas as pl
from jax.experimental.pallas import tpu as pltpu

F32 = jnp.float32
BF16 = jnp.bfloat16
I32 = jnp.int32
HIGHEST = lax.Precision.HIGHEST

EPS = 1e-6
A_WIDTH = 512
A_CONV = 3
B_WIDTH = 512
B_CONV = 31
C_HEADS = 8
C_HEAD_DIM = 64
IDX_HEADS = 8
IDX_DIM = 32
TOPK_MAX = 256
DIFF_HEADS = 4
DIFF_HEAD_DIM = 64
NUM_BUCKETS = 32
MAX_DISTANCE = 128
N_GROUPS = 4
EXPERTS_PER_GROUP = 8
N_EXPERTS = N_GROUPS * EXPERTS_PER_GROUP
C_WIDTH = C_HEADS * C_HEAD_DIM
DIFF_W = DIFF_HEADS * 2 * DIFF_HEAD_DIM
LANES = 128
BF16_ROWS = 16
INT_MIN = -(2 ** 31)
LOG2E = math.log2(math.e)
VMEM_LIMIT = 56 * 1024 * 1024

Z_QC, Z_KC, Z_QD, Z_KD = 0, 1, 2, 3
Z_QIDX_OFF = 4 * C_WIDTH
Z_QIDX_W = IDX_HEADS * IDX_DIM
Z_KIDX_OFF = Z_QIDX_OFF + Z_QIDX_W
Z_COLS = Z_KIDX_OFF + LANES
ZT_VC, ZT_VD = 0, 1
ZT_W_OFF = C_WIDTH + DIFF_W
ZT_ROWS = ZT_W_OFF + BF16_ROWS


def _bucket_starts():
    n = np.arange(0, 2 * MAX_DISTANCE)
    me = NUM_BUCKETS // 2
    lr = np.log(np.maximum(n, 1) / me) / math.log(MAX_DISTANCE / me)
    large = me + (lr * (NUM_BUCKETS - me)).astype(np.int64)
    b = np.where(n < me, n, np.minimum(large, NUM_BUCKETS - 1))
    return [int(n[b >= k].min()) for k in range(NUM_BUCKETS)]


BUCKET_START = _bucket_starts()


def _cparams(sem):
    return pltpu.CompilerParams(dimension_semantics=sem, vmem_limit_bytes=VMEM_LIMIT)


def _rms(x):
    return x * lax.rsqrt(jnp.mean(x * x, axis=-1, keepdims=True) + EPS)


def _sigmoid(x):
    return 1.0 / (1.0 + jnp.exp(-x))


def _dot_nt(a, b):
    return lax.dot_general(a, b, (((1,), (1,)), ((), ())), preferred_element_type=F32)


def _ada_body(c_ref, w_ref, b_ref, o_ref):
    c = c_ref[...]
    cond = c * _sigmoid(c)
    o_ref[0] = jnp.dot(cond, w_ref[0], precision=HIGHEST, preferred_element_type=F32) + b_ref[0]


def _ada_mod(c, ada_w, ada_b):
    depth, d, n6 = ada_w.shape
    bsz = c.shape[0]
    rows = 8
    c_pad = jnp.zeros((rows, d), F32).at[:bsz].set(c)
    tn = 1536
    out = pl.pallas_call(
        _ada_body,
        grid=(depth, n6 // tn),
        in_specs=[pl.BlockSpec((rows, d), lambda i, j: (0, 0)),
                  pl.BlockSpec((1, d, tn), lambda i, j: (i, 0, j)),
                  pl.BlockSpec((1, 1, tn), lambda i, j: (i, 0, j))],
        out_specs=pl.BlockSpec((1, rows, tn), lambda i, j: (i, 0, j)),
        out_shape=jax.ShapeDtypeStruct((depth, rows, n6), F32),
        compiler_params=_cparams(("arbitrary", "arbitrary")),
        name="ada_mod",
    )(c_pad, ada_w, ada_b.reshape(depth, 1, n6))
    return out[:, :bsz]


def _norm_proj_body(x_ref, g_ref, sh_ref, sc_ref, w_ref, *rest):
    y = _rms(x_ref[0]) * g_ref[...]
    h = (y * (1.0 + sc_ref[0]) + sh_ref[0]).astype(BF16)
    if len(rest) == 1:
        (o_ref,) = rest
    else:
        wt_ref, o_ref, ot_ref = rest
        ot_ref[0] = _dot_nt(wt_ref[...], h).astype(ot_ref.dtype)
    o_ref[0] = jnp.dot(h, w_ref[...], preferred_element_type=F32).astype(o_ref.dtype)


def _norm_proj(x, g, sh, sc, w_bf16, tm, wt_bf16=None):
    bsz, seq, d = x.shape
    n = w_bf16.shape[1]
    in_specs = [pl.BlockSpec((1, tm, d), lambda b, i: (b, i, 0)),
                pl.BlockSpec((1, d), lambda b, i: (0, 0)),
                pl.BlockSpec((1, 1, d), lambda b, i: (b, 0, 0)),
                pl.BlockSpec((1, 1, d), lambda b, i: (b, 0, 0)),
                pl.BlockSpec((d, n), lambda b, i: (0, 0))]
    out_specs = pl.BlockSpec((1, tm, n), lambda b, i: (b, i, 0))
    out_shape = jax.ShapeDtypeStruct((bsz, seq, n), BF16)
    args = [x, g.reshape(1, d), sh.reshape(bsz, 1, d), sc.reshape(bsz, 1, d), w_bf16]
    if wt_bf16 is not None:
        nt = wt_bf16.shape[0]
        in_specs.append(pl.BlockSpec((nt, d), lambda b, i: (0, 0)))
        out_specs = [out_specs, pl.BlockSpec((1, nt, tm), lambda b, i: (b, 0, i))]
        out_shape = [out_shape, jax.ShapeDtypeStruct((bsz, nt, seq), BF16)]
        args.append(wt_bf16)
    return pl.pallas_call(
        _norm_proj_body,
        grid=(bsz, seq // tm),
        in_specs=in_specs,
        out_specs=out_specs,
        out_shape=out_shape,
        compiler_params=_cparams(("arbitrary", "arbitrary")),
        name="norm_proj",
    )(*args)


CONV_HALO = 32
CONV_ROWS = 64


def _conv_body(z_ref, x_ref, g1_ref, ca_ref, cb_ref, cbb_ref, lng_ref, lnb_ref, wo_ref, o_ref,
               ua_scr, ub_scr, y_scr, *, tl):
    l = pl.program_id(1)

    @pl.when(l == 0)
    def _():
        ua_scr[0:CONV_HALO, :] = jnp.zeros((CONV_HALO, A_WIDTH), F32)
        ub_scr[0:CONV_HALO, :] = jnp.zeros((CONV_HALO, B_WIDTH), F32)

    a = A_WIDTH
    gate_c = z_ref[0, :, a:2 * a].astype(F32)
    x_a = z_ref[0, :, 2 * a:3 * a].astype(F32)
    ua_scr[CONV_HALO:CONV_HALO + tl, :] = gate_c * x_a
    val_b = z_ref[0, :, 3 * a:3 * a + B_WIDTH].astype(F32)
    glu = z_ref[0, :, 3 * a + B_WIDTH:3 * a + 2 * B_WIDTH].astype(F32)
    ub_scr[CONV_HALO:CONV_HALO + tl, :] = val_b * _sigmoid(glu)

    for r in range(0, tl, CONV_ROWS):
        acc_a = None
        for k in range(A_CONV):
            tap = ua_scr[CONV_HALO + r - (A_CONV - 1) + k:CONV_HALO + r - (A_CONV - 1) + k + CONV_ROWS, :]
            term = tap * ca_ref[k:k + 1, :]
            acc_a = term if acc_a is None else acc_a + term
        gate_b = z_ref[0, r:r + CONV_ROWS, 0:a].astype(F32)
        y_scr[r:r + CONV_ROWS, 0:a] = (gate_b * acc_a).astype(BF16)

        acc_b = None
        for k in range(B_CONV):
            tap = ub_scr[CONV_HALO + r - (B_CONV - 1) + k:CONV_HALO + r - (B_CONV - 1) + k + CONV_ROWS, :]
            term = tap * cb_ref[k:k + 1, :]
            acc_b = term if acc_b is None else acc_b + term
        u = acc_b + cbb_ref[...]
        mu = jnp.mean(u, axis=-1, keepdims=True)
        uc = u - mu
        var = jnp.mean(uc * uc, axis=-1, keepdims=True)
        v = uc * lax.rsqrt(var + EPS) * lng_ref[...] + lnb_ref[...]
        y_scr[r:r + CONV_ROWS, a:a + B_WIDTH] = (v * _sigmoid(v)).astype(BF16)

    ua_scr[0:CONV_HALO, :] = ua_scr[tl:tl + CONV_HALO, :]
    ub_scr[0:CONV_HALO, :] = ub_scr[tl:tl + CONV_HALO, :]
    y = jnp.dot(y_scr[...], wo_ref[...], preferred_element_type=F32)
    o_ref[0] = x_ref[0] + g1_ref[0] * y


def _conv_mix(z, x, g1, conv_a, conv_b, conv_b_bias, ln_g, ln_b, w_out_bf16, tl):
    bsz, seq, d = x.shape
    nz = z.shape[-1]
    wide = A_WIDTH + B_WIDTH
    full = lambda shape: pl.BlockSpec(shape, lambda b, l: (0,) * len(shape))
    return pl.pallas_call(
        functools.partial(_conv_body, tl=tl),
        grid=(bsz, seq // tl),
        in_specs=[pl.BlockSpec((1, tl, nz), lambda b, l: (b, l, 0)),
                  pl.BlockSpec((1, tl, d), lambda b, l: (b, l, 0)),
                  pl.BlockSpec((1, 1, d), lambda b, l: (b, 0, 0)),
                  full((A_CONV, A_WIDTH)), full((B_CONV, B_WIDTH)), full((1, B_WIDTH)),
                  full((1, B_WIDTH)), full((1, B_WIDTH)), full((wide, d))],
        out_specs=pl.BlockSpec((1, tl, d), lambda b, l: (b, l, 0)),
        out_shape=jax.ShapeDtypeStruct((bsz, seq, d), F32),
        scratch_shapes=[pltpu.VMEM((CONV_HALO + tl, A_WIDTH), F32),
                        pltpu.VMEM((CONV_HALO + tl, B_WIDTH), F32),
                        pltpu.VMEM((tl, wide), BF16)],
        compiler_params=_cparams(("arbitrary", "arbitrary")),
        name="conv_mix",
    )(z, x, g1.reshape(bsz, 1, d), conv_a, conv_b, conv_b_bias.reshape(1, -1), ln_g.reshape(1, -1),
      ln_b.reshape(1, -1), w_out_bf16)


MOE_ROWS = 256
ROUTER_ROWS = 512
COMBINE_ROWS = 256
META_GATE0, META_GATE1, META_E0, META_E1, META_RANK0, META_RANK1 = range(6)
GROUP_LANE0 = N_EXPERTS


def _router_body(x_ref, g_ref, sh_ref, sc_ref, wr_ref, br_ref, tri_ref, h_ref, meta_ref, cnt_ref, base_scr):
    @pl.when(pl.program_id(0) == 0)
    def _():
        base_scr[...] = jnp.zeros_like(base_scr)

    h = _rms(x_ref[...]) * g_ref[...]
    h = h * (1.0 + sc_ref[0]) + sh_ref[0]
    h_ref[...] = h
    logits = jnp.dot(h, wr_ref[...], precision=HIGHEST, preferred_element_type=F32) + br_ref[...]
    tr = logits.shape[0]
    lane = lax.broadcasted_iota(I32, (tr, LANES), 1)
    lane_f = lane.astype(F32)
    neg = jnp.float32(-jnp.inf)
    big = jnp.float32(1e9)

    is_group = (lane >= GROUP_LANE0) & (lane < GROUP_LANE0 + N_GROUPS)
    glog = jnp.where(is_group, logits, neg)
    gmax = jnp.max(glog, axis=1, keepdims=True)
    p_top = 1.0 / jnp.sum(jnp.exp(glog - gmax), axis=1, keepdims=True)
    g_sel = jnp.min(jnp.where(glog == gmax, lane_f, big), axis=1, keepdims=True) - GROUP_LANE0
    lo = g_sel * EXPERTS_PER_GROUP
    in_group = (lane_f >= lo) & (lane_f < lo + EXPERTS_PER_GROUP)
    f1 = jnp.where(in_group, logits, neg)
    v1 = jnp.max(f1, axis=1, keepdims=True)
    i1 = jnp.min(jnp.where(f1 == v1, lane_f, big), axis=1, keepdims=True)
    f2 = jnp.where(lane_f == i1, neg, f1)
    v2 = jnp.max(f2, axis=1, keepdims=True)
    i2 = jnp.min(jnp.where(f2 == v2, lane_f, big), axis=1, keepdims=True)
    a = jnp.exp(v2 - v1)
    w1 = 1.0 / (1.0 + a)
    gate0 = p_top * w1
    gate1 = p_top * (a * w1)

    oh0 = lane_f == i1
    oh1 = lane_f == i2
    ind0 = jnp.where(oh0, 1.0, 0.0)
    ind1 = jnp.where(oh1, 1.0, 0.0)
    pre0 = jnp.dot(tri_ref[...], ind0.astype(BF16), preferred_element_type=F32)
    pre1 = jnp.dot(tri_ref[...], ind1.astype(BF16), preferred_element_type=F32)
    tot0 = jnp.sum(ind0, axis=0, keepdims=True)
    tot1 = jnp.sum(ind1, axis=0, keepdims=True)
    base = base_scr[...]
    rank0 = jnp.sum(jnp.where(oh0, base + pre0, 0.0), axis=1, keepdims=True)
    rank1 = jnp.sum(jnp.where(oh1, base + tot0 + pre1, 0.0), axis=1, keepdims=True)
    new_base = base + tot0 + tot1
    base_scr[...] = new_base
    cnt_ref[...] = new_base

    meta = jnp.zeros((tr, LANES), F32)
    for col, val in ((META_GATE0, gate0), (META_GATE1, gate1), (META_E0, i1), (META_E1, i2),
                     (META_RANK0, rank0), (META_RANK1, rank1)):
        meta = jnp.where(lane == col, val, meta)
    meta_ref[...] = meta


def _moe_router(x2, g, sh, sc, wr_g, br_g, wr_e, br_e, seq):
    n_tok, d = x2.shape
    bsz = n_tok // seq
    tr = min(ROUTER_ROWS, seq)
    steps_per_batch = seq // tr
    wr = jnp.zeros((d, LANES), F32).at[:, :N_EXPERTS].set(wr_e).at[:, GROUP_LANE0:GROUP_LANE0 + N_GROUPS].set(wr_g)
    br = jnp.zeros((1, LANES), F32).at[0, :N_EXPERTS].set(br_e).at[0, GROUP_LANE0:GROUP_LANE0 + N_GROUPS].set(br_g)
    tri = jnp.tril(jnp.ones((tr, tr), BF16), -1)
    full = lambda shape: pl.BlockSpec(shape, lambda i: (0,) * len(shape))
    return pl.pallas_call(
        _router_body,
        grid=(n_tok // tr,),
        in_specs=[pl.BlockSpec((tr, d), lambda i: (i, 0)),
                  full((1, d)),
                  pl.BlockSpec((1, 1, d), lambda i: (i // steps_per_batch, 0, 0)),
                  pl.BlockSpec((1, 1, d), lambda i: (i // steps_per_batch, 0, 0)),
                  full((d, LANES)), full((1, LANES)), full((tr, tr))],
        out_specs=[pl.BlockSpec((tr, d), lambda i: (i, 0)),
                   pl.BlockSpec((tr, LANES), lambda i: (i, 0)),
                   full((1, LANES))],
        out_shape=[jax.ShapeDtypeStruct((n_tok, d), F32),
                   jax.ShapeDtypeStruct((n_tok, LANES), F32),
                   jax.ShapeDtypeStruct((1, LANES), F32)],
        scratch_shapes=[pltpu.VMEM((1, LANES), F32)],
        compiler_params=_cparams(("arbitrary",)),
        name="moe_router",
    )(x2, g.reshape(1, d), sh.reshape(bsz, 1, d), sc.reshape(bsz, 1, d), wr, br, tri)


def _dispatch_body(dest_ref, h_ref, xr_ref, sem, *, tr):
    def row_copy(t, k):
        return pltpu.make_async_copy(h_ref.at[pl.ds(t, 1)], xr_ref.at[pl.ds(dest_ref[0, 0, k * tr + t], 1)], sem)

    def start(t, carry):
        row_copy(t, 0).start()
        row_copy(t, 1).start()
        return carry

    lax.fori_loop(0, tr, start, 0)

    def wait(t, carry):
        row_copy(t, 0).wait()
        row_copy(t, 1).wait()
        return carry

    lax.fori_loop(0, tr, wait, 0)


def _moe_dispatch(h2, dest_blocks, tr):
    n_tok, d = h2.shape
    return pl.pallas_call(
        functools.partial(_dispatch_body, tr=tr),
        grid=(n_tok // tr,),
        in_specs=[pl.BlockSpec((1, 1, 2 * tr), lambda i: (i, 0, 0), memory_space=pltpu.SMEM),
                  pl.BlockSpec((tr, d), lambda i: (i, 0))],
        out_specs=pl.BlockSpec(memory_space=pl.ANY),
        out_shape=jax.ShapeDtypeStruct((2 * n_tok, d), F32),
        scratch_shapes=[pltpu.SemaphoreType.DMA(())],
        compiler_params=_cparams(("arbitrary",)),
        name="moe_dispatch",
    )(dest_blocks, h2)


def _expert_body(pb_ref, pe_ref, plo_ref, phi_ref, x_ref, wg_ref, wu_ref, wd_ref, o_ref, wg_s, wu_s, wd_s):
    p = pl.program_id(0)
    prev = jnp.maximum(p - 1, 0)
    new_expert = (p == 0) | (pe_ref[p] != pe_ref[prev])
    first = (p == 0) | (pb_ref[p] != pb_ref[prev])

    @pl.when(new_expert)
    def _():
        wg_s[...] = wg_ref[0].astype(BF16)
        wu_s[...] = wu_ref[0].astype(BF16)
        wd_s[...] = wd_ref[0].astype(BF16)

    x = x_ref[...].astype(BF16)
    gt = jnp.dot(x, wg_s[...], preferred_element_type=F32)
    up = jnp.dot(x, wu_s[...], preferred_element_type=F32)
    act = (gt * _sigmoid(gt)) * up
    y = jnp.dot(act.astype(BF16), wd_s[...], preferred_element_type=F32)
    rows = lax.broadcasted_iota(I32, (y.shape[0], 1), 0)
    y = jnp.where((rows >= plo_ref[p]) & (rows < phi_ref[p]), y, 0.0)

    @pl.when(first)
    def _():
        o_ref[...] = y

    @pl.when(jnp.logical_not(first))
    def _():
        o_ref[...] += y


def _moe_experts(x_rows, pairs, w_gate, w_up, w_down):
    n_rows, d = x_rows.shape
    de = w_gate.shape[-1]
    n_pairs = pairs[0].shape[0]
    grid_spec = pltpu.PrefetchScalarGridSpec(
        num_scalar_prefetch=4,
        grid=(n_pairs,),
        in_specs=[pl.BlockSpec((MOE_ROWS, d), lambda p, pb, pe, lo, hi: (pb[p], 0)),
                  pl.BlockSpec((1, d, de), lambda p, pb, pe, lo, hi: (pe[p], 0, 0)),
                  pl.BlockSpec((1, d, de), lambda p, pb, pe, lo, hi: (pe[p], 0, 0)),
                  pl.BlockSpec((1, de, d), lambda p, pb, pe, lo, hi: (pe[p], 0, 0))],
        out_specs=pl.BlockSpec((MOE_ROWS, d), lambda p, pb, pe, lo, hi: (pb[p], 0)),
        scratch_shapes=[pltpu.VMEM((d, de), BF16), pltpu.VMEM((d, de), BF16), pltpu.VMEM((de, d), BF16)],
    )
    return pl.pallas_call(
        _expert_body,
        grid_spec=grid_spec,
        out_shape=jax.ShapeDtypeStruct((n_rows, d), F32),
        compiler_params=_cparams(("arbitrary",)),
        name="moe_experts",
    )(*pairs, x_rows, w_gate, w_up, w_down)


def _combine_body(dest_ref, y_ref, x_ref, meta_ref, g2_ref, fg_ref, o_ref, r0, r1, sem, *, tc, final_norm):
    def row_copy(t, k, buf):
        return pltpu.make_async_copy(y_ref.at[pl.ds(dest_ref[0, 0, k * tc + t], 1)], buf.at[pl.ds(t, 1)], sem)

    def start(t, carry):
        row_copy(t, 0, r0).start()
        row_copy(t, 1, r1).start()
        return carry

    lax.fori_loop(0, tc, start, 0)

    def wait(t, carry):
        row_copy(t, 0, r0).wait()
        row_copy(t, 1, r1).wait()
        return carry

    lax.fori_loop(0, tc, wait, 0)
    meta = meta_ref[...]
    moe = meta[:, META_GATE0:META_GATE0 + 1] * r0[...] + meta[:, META_GATE1:META_GATE1 + 1] * r1[...]
    xn = x_ref[...] + g2_ref[0] * moe
    if final_norm:
        xn = _rms(xn) * fg_ref[...]
    o_ref[...] = xn


def _moe_combine(y_rows, dest_blocks, x2, meta, g2, final_g, seq, tc, final_norm):
    n_tok, d = x2.shape
    bsz = n_tok // seq
    steps_per_batch = seq // tc
    return pl.pallas_call(
        functools.partial(_combine_body, tc=tc, final_norm=final_norm),
        grid=(n_tok // tc,),
        in_specs=[pl.BlockSpec((1, 1, 2 * tc), lambda i: (i, 0, 0), memory_space=pltpu.SMEM),
                  pl.BlockSpec(memory_space=pl.ANY),
                  pl.BlockSpec((tc, d), lambda i: (i, 0)),
                  pl.BlockSpec((tc, LANES), lambda i: (i, 0)),
                  pl.BlockSpec((1, 1, d), lambda i: (i // steps_per_batch, 0, 0)),
                  pl.BlockSpec((1, d), lambda i: (0, 0))],
        out_specs=pl.BlockSpec((tc, d), lambda i: (i, 0)),
        out_shape=jax.ShapeDtypeStruct((n_tok, d), F32),
        scratch_shapes=[pltpu.VMEM((tc, d), F32), pltpu.VMEM((tc, d), F32), pltpu.SemaphoreType.DMA(())],
        compiler_params=_cparams(("arbitrary",)),
        name="moe_combine",
    )(dest_blocks, y_rows, x2, meta, g2.reshape(bsz, 1, d), final_g.reshape(1, d))


def _blocked_dest(dest0, dest1, rows):
    nb = dest0.shape[0] // rows
    return jnp.concatenate([dest0.reshape(nb, rows), dest1.reshape(nb, rows)], axis=1).reshape(nb, 1, 2 * rows)


def _expert_pairs(counts, n_rows):
    n_blocks = n_rows // MOE_ROWS
    n_pairs = n_blocks + N_EXPERTS
    ends = jnp.cumsum(counts)
    starts = ends - counts
    first_blk = starts // MOE_ROWS
    last_blk = (ends - 1) // MOE_ROWS
    npairs = jnp.where(counts > 0, last_blk - first_blk + 1, 0)
    pend = jnp.cumsum(npairs)
    poff = pend - npairs
    total = pend[-1]
    p = jnp.arange(n_pairs, dtype=I32)
    p_eff = jnp.minimum(p, total - 1)
    e = jnp.minimum(jnp.searchsorted(pend, p_eff, side='right'), N_EXPERTS - 1).astype(I32)
    blk = (first_blk[e] + p_eff - poff[e]).astype(I32)
    valid = p < total
    lo = jnp.where(valid, jnp.clip(starts[e] - blk * MOE_ROWS, 0, MOE_ROWS), 0).astype(I32)
    hi = jnp.where(valid, jnp.clip(ends[e] - blk * MOE_ROWS, 0, MOE_ROWS), 0).astype(I32)
    return blk, e, lo, hi


def _hier_moe(x, g, sh, sc, g2, wr_g, br_g, wr_e, br_e, w_gate, w_up, w_down, final_g, final_norm):
    bsz, seq, d = x.shape
    n_tok = bsz * seq
    x2 = x.reshape(n_tok, d)
    h2, meta, cnt = _moe_router(x2, g, sh, sc, wr_g, br_g, wr_e, br_e, seq)
    counts = cnt[0, :N_EXPERTS].astype(I32)
    starts = jnp.cumsum(counts) - counts
    e0 = meta[:, META_E0].astype(I32)
    e1 = meta[:, META_E1].astype(I32)
    dest0 = starts[e0] + meta[:, META_RANK0].astype(I32)
    dest1 = starts[e1] + meta[:, META_RANK1].astype(I32)
    tr = min(ROUTER_ROWS, seq)
    tc = min(COMBINE_ROWS, seq)
    x_rows = _moe_dispatch(h2, _blocked_dest(dest0, dest1, tr), tr)
    y_rows = _moe_experts(x_rows, _expert_pairs(counts, 2 * n_tok), w_gate, w_up, w_down)
    out = _moe_combine(y_rows, _blocked_dest(dest0, dest1, tc), x2, meta, g2, final_g, seq, tc, final_norm)
    return out.reshape(bsz, seq, d)


SEL_COLS = 256
SEL_CHUNK = 512
SEL_FOLD = 32
HALF_OFFSET = 2 ** 15
ATT_TILE = 512


def _sortable(score):
    bits = lax.bitcast_convert_type(score, I32)
    return bits ^ ((bits >> 31) & jnp.int32(0x7FFFFFFF))


def _idx_queries(zq):
    zf = zq.astype(F32)
    return [zf[:, h * IDX_DIM:(h + 1) * IDX_DIM].astype(BF16) for h in range(IDX_HEADS)]


def _idx_score_t(k, qs, wt):
    acc = None
    for h, q in enumerate(qs):
        term = jnp.maximum(_dot_nt(k, q), 0.0) * wt[h:h + 1, :]
        acc = term if acc is None else acc + term
    return acc


def _select_body(zq_ref, zk_ref, wt_ref, o_ref, hi_scr, lo_scr, *, topk):
    i = pl.program_id(1)
    tq = zq_ref.shape[1]
    qs = _idx_queries(zq_ref[0])
    wt = wt_ref[0].astype(F32)
    n_chunks = ((i + 1) * tq + SEL_CHUNK - 1) // SEL_CHUNK
    krow0 = lax.broadcasted_iota(I32, (SEL_CHUNK, tq), 0)
    qcol = i * tq + lax.broadcasted_iota(I32, (SEL_CHUNK, tq), 1)
    i16 = jnp.int16
    half_min = -HALF_OFFSET

    def fill(c, carry):
        off = pl.multiple_of(c * SEL_CHUNK, SEL_CHUNK)
        k = zk_ref[0, pl.ds(off, SEL_CHUNK), :][:, 0:IDX_DIM]
        key = jnp.where(krow0 + off <= qcol, _sortable(_idx_score_t(k, qs, wt)), jnp.int32(INT_MIN))
        hi_scr[pl.ds(off, SEL_CHUNK), :] = (key >> 16).astype(i16)
        lo_scr[pl.ds(off, SEL_CHUNK), :] = ((key & 0xFFFF) - HALF_OFFSET).astype(i16)
        return carry

    lax.fori_loop(0, n_chunks, fill, 0)

    def count16(pred):
        def body(c, acc):
            off = pl.multiple_of(c * SEL_CHUNK, SEL_CHUNK)
            hit = pred(hi_scr[pl.ds(off, SEL_CHUNK), :], lo_scr[pl.ds(off, SEL_CHUNK), :])
            ind = jnp.where(hit, jnp.ones((), i16), jnp.zeros((), i16))
            parts = [ind[r:r + SEL_FOLD] for r in range(0, SEL_CHUNK, SEL_FOLD)]
            while len(parts) > 1:
                parts = [a + b for a, b in zip(parts[0::2], parts[1::2])]
            return acc + parts[0]
        acc = lax.fori_loop(0, n_chunks, body, jnp.zeros((SEL_FOLD, tq), i16))
        return jnp.sum(acc.astype(F32), axis=0, keepdims=True)

    def radix16(pred_of, need):
        def bit_step(ib, u):
            cand_u = u | jnp.left_shift(jnp.int32(1), 15 - ib)
            cnt = count16(pred_of((cand_u - HALF_OFFSET).astype(i16)))
            return jnp.where(cnt >= need, cand_u, u)
        return lax.fori_loop(0, 16, bit_step, jnp.zeros((1, tq), I32)) - HALF_OFFSET

    k_f = jnp.float32(topk)
    thi32 = radix16(lambda t: (lambda hi, lo: hi >= t), k_f)
    thi = thi32.astype(i16)
    n_above = count16(lambda hi, lo: hi > thi)

    def restrict(c, carry):
        off = pl.multiple_of(c * SEL_CHUNK, SEL_CHUNK)
        lo = lo_scr[pl.ds(off, SEL_CHUNK), :]
        lo_scr[pl.ds(off, SEL_CHUNK), :] = jnp.where(hi_scr[pl.ds(off, SEL_CHUNK), :] == thi, lo,
                                                     jnp.full((), half_min, i16))
        return carry

    lax.fori_loop(0, n_chunks, restrict, 0)
    tlo32 = radix16(lambda t: (lambda hi, lo: lo >= t), k_f - n_above)
    tlo32 = jnp.where(thi32 == half_min, HALF_OFFSET - 1, tlo32)
    tlo = tlo32.astype(i16)

    def kept(hi, lo):
        return (hi > thi) | ((hi == thi) & (lo >= tlo))

    def emit(c, carry):
        off = pl.multiple_of(c * SEL_CHUNK, SEL_CHUNK)
        keep = kept(hi_scr[pl.ds(off, SEL_CHUNK), :], lo_scr[pl.ds(off, SEL_CHUNK), :])
        o_ref[0, pl.ds(off, SEL_CHUNK), :] = jnp.where(keep, jnp.zeros((), BF16), jnp.full((), -jnp.inf, BF16))
        return carry

    lax.fori_loop(0, n_chunks, emit, 0)

    n_ge = count16(kept)

    @pl.when(jnp.max(n_ge) > k_f)
    def _():
        def count32(pred):
            def body(c, acc):
                off = pl.multiple_of(c * SEL_CHUNK, SEL_CHUNK)
                hit = pred(hi_scr[pl.ds(off, SEL_CHUNK), :].astype(I32), lo_scr[pl.ds(off, SEL_CHUNK), :].astype(I32),
                           krow0 + off)
                ind = jnp.where(hit, 1, 0)
                return acc + jnp.sum(ind.reshape(SEL_CHUNK // SEL_FOLD, SEL_FOLD, tq), axis=0)
            acc = lax.fori_loop(0, n_chunks, body, jnp.zeros((SEL_FOLD, tq), I32))
            return jnp.sum(acc.astype(F32), axis=0, keepdims=True)

        def tied(hi, lo):
            return (hi == thi32) & (lo == tlo32)

        n_tied = count32(lambda hi, lo, kidx: tied(hi, lo))
        need = k_f - (n_ge - n_tied)

        def cap_step(ib, v):
            cand = v | jnp.left_shift(jnp.int32(1), 14 - ib)
            cnt = count32(lambda hi, lo, kidx: tied(hi, lo) & (kidx < cand))
            return jnp.where(cnt <= need, cand, v)

        cap = lax.fori_loop(0, 15, cap_step, jnp.zeros((1, tq), I32))
        cap = jnp.where(n_ge > k_f, cap, 2 ** 30)

        def drop(c, carry):
            off = pl.multiple_of(c * SEL_CHUNK, SEL_CHUNK)
            past = (tied(hi_scr[pl.ds(off, SEL_CHUNK), :].astype(I32), lo_scr[pl.ds(off, SEL_CHUNK), :].astype(I32))
                    & (krow0 + off >= cap))
            cur = o_ref[0, pl.ds(off, SEL_CHUNK), :].astype(F32)
            o_ref[0, pl.ds(off, SEL_CHUNK), :] = jnp.where(past, -jnp.inf, cur).astype(o_ref.dtype)
            return carry

        lax.fori_loop(0, n_chunks, drop, 0)

    def blank(c, carry):
        off = pl.multiple_of(c * SEL_CHUNK, SEL_CHUNK)
        o_ref[0, pl.ds(off, SEL_CHUNK), :] = jnp.full((SEL_CHUNK, tq), -jnp.inf, o_ref.dtype)
        return carry

    lax.fori_loop(n_chunks, o_ref.shape[1] // SEL_CHUNK, blank, 0)


def _select(z, zt, topk):
    bsz, seq, _ = z.shape
    tq = min(SEL_COLS, seq)
    assert seq % SEL_CHUNK == 0 and seq % tq == 0
    return pl.pallas_call(
        functools.partial(_select_body, topk=topk),
        grid=(bsz, seq // tq),
        in_specs=[pl.BlockSpec((1, tq, Z_QIDX_W), lambda b, i: (b, i, Z_QIDX_OFF // Z_QIDX_W)),
                  pl.BlockSpec((1, seq, LANES), lambda b, i: (b, 0, Z_KIDX_OFF // LANES)),
                  pl.BlockSpec((1, BF16_ROWS, tq), lambda b, i: (b, ZT_W_OFF // BF16_ROWS, i))],
        out_specs=pl.BlockSpec((1, seq, tq), lambda b, i: (b, 0, i)),
        out_shape=jax.ShapeDtypeStruct((bsz, seq, seq), BF16),
        scratch_shapes=[pltpu.VMEM((seq, tq), jnp.int16), pltpu.VMEM((seq, tq), jnp.int16)],
        compiler_params=_cparams(("arbitrary", "arbitrary")),
        name="dsa_select",
    )(z, z, zt)


BIAS_INIT_ROWS = 8
ONES_ROWS = BF16_ROWS


def _init_bias_tiles(btile, bias_ref, head0, n_heads, ta):
    col = lax.broadcasted_iota(I32, (BIAS_INIT_ROWS, ta), 1)
    row0 = lax.broadcasted_iota(I32, (BIAS_INIT_ROWS, ta), 0)

    def body(r, carry):
        off = pl.multiple_of(r * BIAS_INIT_ROWS, BIAS_INIT_ROWS)
        for kind in range(2):
            dist = col - (row0 + off) + kind * ta
            for h in range(n_heads):
                far = bias_ref[NUM_BUCKETS - 1, head0 + h]
                val = jnp.full((BIAS_INIT_ROWS, ta), (bias_ref[0, head0 + h] - far) * LOG2E, F32)
                for b in range(1, NUM_BUCKETS - 1):
                    val = jnp.where(dist >= BUCKET_START[b], (bias_ref[b, head0 + h] - far) * LOG2E, val)
                val = jnp.where(dist >= BUCKET_START[NUM_BUCKETS - 1], 0.0, val)
                btile[h, kind, pl.ds(off, BIAS_INIT_ROWS), :] = val
        return carry

    lax.fori_loop(0, ta // BIAS_INIT_ROWS, body, 0)


def _with_ones(vt):
    return jnp.concatenate([vt, jnp.ones((ONES_ROWS, vt.shape[1]), vt.dtype)], axis=0)


def _softmax_step_t(logits, v_aug, m_ref, acc_ref, idx):
    m_old = m_ref[idx]
    m_new = jnp.maximum(m_old, jnp.max(logits, axis=0, keepdims=True))
    m_safe = jnp.where(m_new == -jnp.inf, 0.0, m_new)
    p = jnp.exp2(logits - m_safe)
    alpha = jnp.exp2(m_old - m_safe)
    acc_ref[idx] = alpha * acc_ref[idx] + jnp.dot(v_aug, p.astype(BF16), preferred_element_type=F32)
    m_ref[idx] = m_new


def _reset_softmax(m_s, acc_s):
    m_s[...] = jnp.full(m_s.shape, -jnp.inf, F32)
    acc_s[...] = jnp.zeros(acc_s.shape, F32)


def _cattn_body(qi_ref, ki_ref, bias_ref, zq_ref, zk_ref, vt_ref, mask_ref, o_ref, m_s, acc_s, btile):
    b = pl.program_id(0)
    p = pl.program_id(1)
    qi = qi_ref[p]
    ki = ki_ref[p]
    ta = zq_ref.shape[1]
    hd = C_HEAD_DIM

    @pl.when((b == 0) & (p == 0))
    def _():
        _init_bias_tiles(btile, bias_ref, 0, C_HEADS, ta)

    @pl.when(ki == 0)
    def _():
        _reset_softmax(m_s, acc_s)

    def heads(extra_of_head):
        for h in range(C_HEADS):
            lo = h * hd
            logits = _dot_nt(zk_ref[0, :, lo:lo + hd], zq_ref[0, :, lo:lo + hd]) + extra_of_head(h)
            _softmax_step_t(logits, _with_ones(vt_ref[0, lo:lo + hd, :]), m_s, acc_s, h)

    near = ki >= qi - 1

    @pl.when(near)
    def _():
        heads(lambda h: mask_ref[0].astype(F32) + btile[h, qi - ki])

    @pl.when(jnp.logical_not(near))
    def _():
        heads(lambda h: mask_ref[0].astype(F32))

    @pl.when(ki == qi)
    def _():
        outs = []
        for h in range(C_HEADS):
            a = acc_s[h]
            outs.append((a[0:hd] / a[hd:hd + 1]).T)
        o_ref[0] = jnp.concatenate(outs, axis=1).astype(o_ref.dtype)


def _dattn_body(qi_ref, ki_ref, bias_ref, zq_ref, zk_ref, vt_ref, lam_ref, ng_ref, o_ref,
                m_s, acc_s, btile, *, lambda_init):
    b = pl.program_id(0)
    p = pl.program_id(1)
    qi = qi_ref[p]
    ki = ki_ref[p]
    ta = zq_ref.shape[1]
    hd = DIFF_HEAD_DIM
    dv = 2 * hd

    @pl.when((b == 0) & (p == 0))
    def _():
        _init_bias_tiles(btile, bias_ref, C_HEADS, DIFF_HEADS, ta)

    @pl.when(ki == 0)
    def _():
        _reset_softmax(m_s, acc_s)

    def heads(extra_of_head):
        for h in range(DIFF_HEADS):
            v_aug = _with_ones(vt_ref[0, dv * h:dv * (h + 1), :])
            extra = extra_of_head(h)
            for j in range(2):
                lo = (2 * h + j) * hd
                logits = _dot_nt(zk_ref[0, :, lo:lo + hd], zq_ref[0, :, lo:lo + hd])
                if extra is not None:
                    logits = logits + extra
                _softmax_step_t(logits, v_aug, m_s, acc_s, 2 * h + j)

    near = ki >= qi - 1

    @pl.when(near)
    def _():
        krow = ki * ta + lax.broadcasted_iota(I32, (ta, ta), 0)
        qcol = qi * ta + lax.broadcasted_iota(I32, (ta, ta), 1)
        heads(lambda h: jnp.where(krow <= qcol, btile[h, qi - ki], -jnp.inf))

    @pl.when(jnp.logical_not(near))
    def _():
        heads(lambda h: None)

    @pl.when(ki == qi)
    def _():
        lam_p = lam_ref[...]
        lam = (jnp.exp(jnp.sum(lam_p[0:1] * lam_p[1:2], axis=1, keepdims=True))
               - jnp.exp(jnp.sum(lam_p[2:3] * lam_p[3:4], axis=1, keepdims=True)) + lambda_init)
        outs = []
        for h in range(DIFF_HEADS):
            a1 = acc_s[2 * h]
            a2 = acc_s[2 * h + 1]
            o = a1[0:dv] / a1[dv:dv + 1] - lam * (a2[0:dv] / a2[dv:dv + 1])
            o = o * lax.rsqrt(jnp.mean(o * o, axis=0, keepdims=True) + EPS) * ng_ref[...] * (1.0 - lambda_init)
            outs.append(o.T)
        o_ref[0] = jnp.concatenate(outs, axis=1).astype(o_ref.dtype)


def _causal_pairs(nq):
    qi = [q for q in range(nq) for _ in range(q + 1)]
    ki = [k for q in range(nq) for k in range(q + 1)]
    return jnp.asarray(qi, I32), jnp.asarray(ki, I32)


def _cattn(z, zt, mask, rel_bias, ta):
    bsz, seq, _ = z.shape
    qi, ki = _causal_pairs(seq // ta)
    qmap = lambda col: (lambda b, p, qi, ki: (b, qi[p], col))
    kmap = lambda col: (lambda b, p, qi, ki: (b, ki[p], col))
    dv_aug = C_HEAD_DIM + ONES_ROWS
    grid_spec = pltpu.PrefetchScalarGridSpec(
        num_scalar_prefetch=2,
        grid=(bsz, qi.shape[0]),
        in_specs=[pl.BlockSpec(memory_space=pltpu.SMEM),
                  pl.BlockSpec((1, ta, C_WIDTH), qmap(Z_QC)),
                  pl.BlockSpec((1, ta, C_WIDTH), kmap(Z_KC)),
                  pl.BlockSpec((1, C_WIDTH, ta), lambda b, p, qi, ki: (b, ZT_VC, ki[p])),
                  pl.BlockSpec((1, ta, ta), lambda b, p, qi, ki: (b, ki[p], qi[p]))],
        out_specs=pl.BlockSpec((1, ta, C_WIDTH), qmap(0)),
        scratch_shapes=[pltpu.VMEM((C_HEADS, 1, ta), F32),
                        pltpu.VMEM((C_HEADS, dv_aug, ta), F32),
                        pltpu.VMEM((C_HEADS, 2, ta, ta), F32)],
    )
    return pl.pallas_call(
        _cattn_body,
        grid_spec=grid_spec,
        out_shape=jax.ShapeDtypeStruct((bsz, seq, C_WIDTH), BF16),
        compiler_params=_cparams(("arbitrary", "arbitrary")),
        name="dsa_attn",
    )(qi, ki, rel_bias, z, z, zt, mask)


def _dattn(z, zt, rel_bias, diff_lam, diff_norm_g, lambda_init, ta):
    bsz, seq, _ = z.shape
    qi, ki = _causal_pairs(seq // ta)
    n_maps = 2 * DIFF_HEADS
    dv = 2 * DIFF_HEAD_DIM
    qmap = lambda col: (lambda b, p, qi, ki: (b, qi[p], col))
    kmap = lambda col: (lambda b, p, qi, ki: (b, ki[p], col))
    grid_spec = pltpu.PrefetchScalarGridSpec(
        num_scalar_prefetch=2,
        grid=(bsz, qi.shape[0]),
        in_specs=[pl.BlockSpec(memory_space=pltpu.SMEM),
                  pl.BlockSpec((1, ta, DIFF_W), qmap(Z_QD)),
                  pl.BlockSpec((1, ta, DIFF_W), kmap(Z_KD)),
                  pl.BlockSpec((1, DIFF_W, ta), lambda b, p, qi, ki: (b, ZT_VD, ki[p])),
                  pl.BlockSpec(diff_lam.shape, lambda b, p, qi, ki: (0, 0)),
                  pl.BlockSpec((dv, 1), lambda b, p, qi, ki: (0, 0))],
        out_specs=pl.BlockSpec((1, ta, DIFF_W), qmap(0)),
        scratch_shapes=[pltpu.VMEM((n_maps, 1, ta), F32),
                        pltpu.VMEM((n_maps, dv + ONES_ROWS, ta), F32),
                        pltpu.VMEM((DIFF_HEADS, 2, ta, ta), F32)],
    )
    return pl.pallas_call(
        functools.partial(_dattn_body, lambda_init=lambda_init),
        grid_spec=grid_spec,
        out_shape=jax.ShapeDtypeStruct((bsz, seq, DIFF_W), BF16),
        compiler_params=_cparams(("arbitrary", "arbitrary")),
        name="diff_attn",
    )(qi, ki, rel_bias, z, z, zt, diff_lam, diff_norm_g.reshape(dv, 1))


def _out_proj_body(oc_ref, od_ref, x_ref, g1_ref, w_ref, o_ref):
    y = (jnp.dot(oc_ref[0], w_ref[0:C_WIDTH, :], preferred_element_type=F32)
         + jnp.dot(od_ref[0], w_ref[C_WIDTH:C_WIDTH + DIFF_W, :], preferred_element_type=F32))
    o_ref[0] = x_ref[0] + g1_ref[0] * y


def _out_proj(out_c, out_d, x, g1, w_out_bf16, tm):
    bsz, seq, d = x.shape
    return pl.pallas_call(
        _out_proj_body,
        grid=(bsz, seq // tm),
        in_specs=[pl.BlockSpec((1, tm, C_WIDTH), lambda b, i: (b, i, 0)),
                  pl.BlockSpec((1, tm, DIFF_W), lambda b, i: (b, i, 0)),
                  pl.BlockSpec((1, tm, d), lambda b, i: (b, i, 0)),
                  pl.BlockSpec((1, 1, d), lambda b, i: (b, 0, 0)),
                  pl.BlockSpec((C_WIDTH + DIFF_W, d), lambda b, i: (0, 0))],
        out_specs=pl.BlockSpec((1, tm, d), lambda b, i: (b, i, 0)),
        out_shape=jax.ShapeDtypeStruct((bsz, seq, d), F32),
        compiler_params=_cparams(("arbitrary", "arbitrary")),
        name="attn_out_proj",
    )(out_c, out_d, x, g1.reshape(bsz, 1, d), w_out_bf16)


def _attn_in_weights(cd_w_in):
    sizes = (C_WIDTH, C_WIDTH, C_WIDTH, IDX_HEADS * IDX_DIM, IDX_DIM, IDX_HEADS, DIFF_W, DIFF_W, DIFF_W)
    cuts = np.cumsum(sizes)[:-1]
    q_c, k_c, v_c, q_i, k_i, w_i, q_d, k_d, v_d = jnp.split(cd_w_in, cuts, axis=1)
    d = cd_w_in.shape[0]
    w = jnp.concatenate([q_c * (C_HEAD_DIM ** -0.5 * LOG2E), k_c, q_d * (DIFF_HEAD_DIM ** -0.5 * LOG2E), k_d,
                         q_i, k_i, jnp.zeros((d, LANES - IDX_DIM), cd_w_in.dtype)], axis=1)
    wt = jnp.concatenate([v_c, v_d, w_i * (IDX_DIM * IDX_HEADS) ** -0.5,
                          jnp.zeros((d, BF16_ROWS - IDX_HEADS), cd_w_in.dtype)], axis=1).T
    return w.astype(BF16), wt.astype(BF16)


def kernel(x, c, positions, rel_bias, norm_g, final_norm_g, ada_w, ada_b, ab_w_in, ab_conv_a, ab_conv_b,
           ab_conv_b_bias, ab_ln_g, ab_ln_b, ab_w_out, cd_w_in, diff_lam, diff_norm_g, cd_w_out,
           moe_wr_g, moe_br_g, moe_wr_e, moe_br_e, moe_w_gate, moe_w_up, moe_w_down):
    del positions
    bsz, seq, d = x.shape
    depth = ada_w.shape[0]
    tm = min(512, seq)
    ta = min(ATT_TILE, seq)
    assert ta >= MAX_DISTANCE and seq % ta == 0
    topk = min(TOPK_MAX, seq // 4)
    mods = _ada_mod(c, ada_w, ada_b)
    for i in range(depth):
        sh1, sc1, g1, sh2, sc2, g2 = jnp.split(mods[i], 6, axis=-1)
        j = i // 2
        if i % 2 == 0:
            z = _norm_proj(x, norm_g[i, 0], sh1, sc1, ab_w_in[j].astype(BF16), tm)
            x = _conv_mix(z, x, g1, ab_conv_a[j], ab_conv_b[j], ab_conv_b_bias[j], ab_ln_g[j], ab_ln_b[j],
                          ab_w_out[j].astype(BF16), min(256, seq))
        else:
            lambda_init = 0.8 - 0.6 * math.exp(-0.3 * i)
            w, wt = _attn_in_weights(cd_w_in[j])
            z, zt = _norm_proj(x, norm_g[i, 0], sh1, sc1, w, tm, wt)
            mask = _select(z, zt, topk)
            out_c = _cattn(z, zt, mask, rel_bias, ta)
            out_d = _dattn(z, zt, rel_bias, diff_lam[j], diff_norm_g[j], lambda_init, ta)
            x = _out_proj(out_c, out_d, x, g1, cd_w_out[j].astype(BF16), tm)
        x = _hier_moe(x, norm_g[i, 1], sh2, sc2, g2, moe_wr_g[i], moe_br_g[i], moe_wr_e[i], moe_br_e[i],
                      moe_w_gate[i], moe_w_up[i], moe_w_down[i], final_norm_g, final_norm=(i == depth - 1))
    return x
```

```python
import functools
import math

import numpy as np
import jax
import jax.numpy as jnp
from jax import lax
from jax.experimental import pallas as pl
from jax.experimental.pallas import tpu as pltpu

F32 = jnp.float32
BF16 = jnp.bfloat16
I32 = jnp.int32
HIGHEST = lax.Precision.HIGHEST

EPS = 1e-6
A_WIDTH = 512
A_CONV = 3
B_WIDTH = 512
B_CONV = 31
C_HEADS = 8
C_HEAD_DIM = 64
IDX_HEADS = 8
IDX_DIM = 32
TOPK_MAX = 256
DIFF_HEADS = 4
DIFF_HEAD_DIM = 64
NUM_BUCKETS = 32
MAX_DISTANCE = 128
N_GROUPS = 4
EXPERTS_PER_GROUP = 8
N_EXPERTS = N_GROUPS * EXPERTS_PER_GROUP
C_WIDTH = C_HEADS * C_HEAD_DIM
DIFF_W = DIFF_HEADS * 2 * DIFF_HEAD_DIM
LANES = 128
BF16_ROWS = 16
INT_MIN = -(2 ** 31)
LOG2E = math.log2(math.e)
VMEM_LIMIT = 56 * 1024 * 1024

Z_QC, Z_KC, Z_QD, Z_KD = 0, 1, 2, 3
Z_QIDX_OFF = 4 * C_WIDTH
Z_QIDX_W = IDX_HEADS * IDX_DIM
Z_KIDX_OFF = Z_QIDX_OFF + Z_QIDX_W
Z_COLS = Z_KIDX_OFF + LANES
ZT_VC, ZT_VD = 0, 1
ZT_W_OFF = C_WIDTH + DIFF_W
ZT_ROWS = ZT_W_OFF + BF16_ROWS


def _bucket_starts():
    n = np.arange(0, 2 * MAX_DISTANCE)
    me = NUM_BUCKETS // 2
    lr = np.log(np.maximum(n, 1) / me) / math.log(MAX_DISTANCE / me)
    large = me + (lr * (NUM_BUCKETS - me)).astype(np.int64)
    b = np.where(n < me, n, np.minimum(large, NUM_BUCKETS - 1))
    return [int(n[b >= k].min()) for k in range(NUM_BUCKETS)]


BUCKET_START = _bucket_starts()


def _cparams(sem):
    return pltpu.CompilerParams(dimension_semantics=sem, vmem_limit_bytes=VMEM_LIMIT)


def _rms(x):
    return x * lax.rsqrt(jnp.mean(x * x, axis=-1, keepdims=True) + EPS)


def _sigmoid(x):
    return 1.0 / (1.0 + jnp.exp(-x))


def _dot_nt(a, b):
    return lax.dot_general(a, b, (((1,), (1,)), ((), ())), preferred_element_type=F32)


def _ada_body(c_ref, w_ref, b_ref, o_ref):
    c = c_ref[...]
    cond = c * _sigmoid(c)
    o_ref[0] = jnp.dot(cond, w_ref[0], precision=HIGHEST, preferred_element_type=F32) + b_ref[0]


def _ada_mod(c, ada_w, ada_b):
    depth, d, n6 = ada_w.shape
    bsz = c.shape[0]
    rows = 8
    c_pad = jnp.zeros((rows, d), F32).at[:bsz].set(c)
    tn = 1536
    out = pl.pallas_call(
        _ada_body,
        grid=(depth, n6 // tn),
        in_specs=[pl.BlockSpec((rows, d), lambda i, j: (0, 0)),
                  pl.BlockSpec((1, d, tn), lambda i, j: (i, 0, j)),
                  pl.BlockSpec((1, 1, tn), lambda i, j: (i, 0, j))],
        out_specs=pl.BlockSpec((1, rows, tn), lambda i, j: (i, 0, j)),
        out_shape=jax.ShapeDtypeStruct((depth, rows, n6), F32),
        compiler_params=_cparams(("arbitrary", "arbitrary")),
        name="ada_mod",
    )(c_pad, ada_w, ada_b.reshape(depth, 1, n6))
    return out[:, :bsz]


def _norm_proj_body(x_ref, g_ref, sh_ref, sc_ref, w_ref, *rest):
    y = _rms(x_ref[0]) * g_ref[...]
    h = (y * (1.0 + sc_ref[0]) + sh_ref[0]).astype(BF16)
    if len(rest) == 1:
        (o_ref,) = rest
    else:
        wt_ref, o_ref, ot_ref = rest
        ot_ref[0] = _dot_nt(wt_ref[...], h).astype(ot_ref.dtype)
    o_ref[0] = jnp.dot(h, w_ref[...], preferred_element_type=F32).astype(o_ref.dtype)


def _norm_proj(x, g, sh, sc, w_bf16, tm, wt_bf16=None):
    bsz, seq, d = x.shape
    n = w_bf16.shape[1]
    in_specs = [pl.BlockSpec((1, tm, d), lambda b, i: (b, i, 0)),
                pl.BlockSpec((1, d), lambda b, i: (0, 0)),
                pl.BlockSpec((1, 1, d), lambda b, i: (b, 0, 0)),
                pl.BlockSpec((1, 1, d), lambda b, i: (b, 0, 0)),
                pl.BlockSpec((d, n), lambda b, i: (0, 0))]
    out_specs = pl.BlockSpec((1, tm, n), lambda b, i: (b, i, 0))
    out_shape = jax.ShapeDtypeStruct((bsz, seq, n), BF16)
    args = [x, g.reshape(1, d), sh.reshape(bsz, 1, d), sc.reshape(bsz, 1, d), w_bf16]
    if wt_bf16 is not None:
        nt = wt_bf16.shape[0]
        in_specs.append(pl.BlockSpec((nt, d), lambda b, i: (0, 0)))
        out_specs = [out_specs, pl.BlockSpec((1, nt, tm), lambda b, i: (b, 0, i))]
        out_shape = [out_shape, jax.ShapeDtypeStruct((bsz, nt, seq), BF16)]
        args.append(wt_bf16)
    return pl.pallas_call(
        _norm_proj_body,
        grid=(bsz, seq // tm),
        in_specs=in_specs,
        out_specs=out_specs,
        out_shape=out_shape,
        compiler_params=_cparams(("arbitrary", "arbitrary")),
        name="norm_proj",
    )(*args)


CONV_HALO = 32
CONV_ROWS = 64


def _conv_body(z_ref, x_ref, g1_ref, ca_ref, cb_ref, cbb_ref, lng_ref, lnb_ref, wo_ref, o_ref,
               ua_scr, ub_scr, y_scr, *, tl):
    l = pl.program_id(1)

    @pl.when(l == 0)
    def _():
        ua_scr[0:CONV_HALO, :] = jnp.zeros((CONV_HALO, A_WIDTH), F32)
        ub_scr[0:CONV_HALO, :] = jnp.zeros((CONV_HALO, B_WIDTH), F32)

    a = A_WIDTH
    gate_c = z_ref[0, :, a:2 * a].astype(F32)
    x_a = z_ref[0, :, 2 * a:3 * a].astype(F32)
    ua_scr[CONV_HALO:CONV_HALO + tl, :] = gate_c * x_a
    val_b = z_ref[0, :, 3 * a:3 * a + B_WIDTH].astype(F32)
    glu = z_ref[0, :, 3 * a + B_WIDTH:3 * a + 2 * B_WIDTH].astype(F32)
    ub_scr[CONV_HALO:CONV_HALO + tl, :] = val_b * _sigmoid(glu)

    for r in range(0, tl, CONV_ROWS):
        acc_a = None
        for k in range(A_CONV):
            tap = ua_scr[CONV_HALO + r - (A_CONV - 1) + k:CONV_HALO + r - (A_CONV - 1) + k + CONV_ROWS, :]
            term = tap * ca_ref[k:k + 1, :]
            acc_a = term if acc_a is None else acc_a + term
        gate_b = z_ref[0, r:r + CONV_ROWS, 0:a].astype(F32)
        y_scr[r:r + CONV_ROWS, 0:a] = (gate_b * acc_a).astype(BF16)

        acc_b = None
        for k in range(B_CONV):
            tap = ub_scr[CONV_HALO + r - (B_CONV - 1) + k:CONV_HALO + r - (B_CONV - 1) + k + CONV_ROWS, :]
            term = tap * cb_ref[k:k + 1, :]
            acc_b = term if acc_b is None else acc_b + term
        u = acc_b + cbb_ref[...]
        mu = jnp.mean(u, axis=-1, keepdims=True)
        uc = u - mu
        var = jnp.mean(uc * uc, axis=-1, keepdims=True)
        v = uc * lax.rsqrt(var + EPS) * lng_ref[...] + lnb_ref[...]
        y_scr[r:r + CONV_ROWS, a:a + B_WIDTH] = (v * _sigmoid(v)).astype(BF16)

    ua_scr[0:CONV_HALO, :] = ua_scr[tl:tl + CONV_HALO, :]
    ub_scr[0:CONV_HALO, :] = ub_scr[tl:tl + CONV_HALO, :]
    y = jnp.dot(y_scr[...], wo_ref[...], preferred_element_type=F32)
    o_ref[0] = x_ref[0] + g1_ref[0] * y


def _conv_mix(z, x, g1, conv_a, conv_b, conv_b_bias, ln_g, ln_b, w_out_bf16, tl):
    bsz, seq, d = x.shape
    nz = z.shape[-1]
    wide = A_WIDTH + B_WIDTH
    full = lambda shape: pl.BlockSpec(shape, lambda b, l: (0,) * len(shape))
    return pl.pallas_call(
        functools.partial(_conv_body, tl=tl),
        grid=(bsz, seq // tl),
        in_specs=[pl.BlockSpec((1, tl, nz), lambda b, l: (b, l, 0)),
                  pl.BlockSpec((1, tl, d), lambda b, l: (b, l, 0)),
                  pl.BlockSpec((1, 1, d), lambda b, l: (b, 0, 0)),
                  full((A_CONV, A_WIDTH)), full((B_CONV, B_WIDTH)), full((1, B_WIDTH)),
                  full((1, B_WIDTH)), full((1, B_WIDTH)), full((wide, d))],
        out_specs=pl.BlockSpec((1, tl, d), lambda b, l: (b, l, 0)),
        out_shape=jax.ShapeDtypeStruct((bsz, seq, d), F32),
        scratch_shapes=[pltpu.VMEM((CONV_HALO + tl, A_WIDTH), F32),
                        pltpu.VMEM((CONV_HALO + tl, B_WIDTH), F32),
                        pltpu.VMEM((tl, wide), BF16)],
        compiler_params=_cparams(("arbitrary", "arbitrary")),
        name="conv_mix",
    )(z, x, g1.reshape(bsz, 1, d), conv_a, conv_b, conv_b_bias.reshape(1, -1), ln_g.reshape(1, -1),
      ln_b.reshape(1, -1), w_out_bf16)


MOE_ROWS = 256
ROUTER_ROWS = 512
META_GATE0, META_GATE1, META_POS0, META_POS1 = range(4)
SEG_LEN, SEG_BASE, SEG_OFF = range(3)
SEG_ALIGN = 8
SEG_SIZES = tuple(2 ** b for b in range(10, 2, -1))
GROUP_LANE0 = N_EXPERTS


def _router_body(x_ref, g_ref, sh_ref, sc_ref, wr_ref, br_ref, tri_ref, upper_ref, h_ref, meta_ref, post_ref, seg_ref,
                 cnt_ref, base_scr):
    @pl.when(pl.program_id(0) == 0)
    def _():
        base_scr[...] = jnp.zeros_like(base_scr)

    h = _rms(x_ref[...]) * g_ref[...]
    h = h * (1.0 + sc_ref[0]) + sh_ref[0]
    h_ref[...] = h.astype(h_ref.dtype)
    logits = jnp.dot(h, wr_ref[...], precision=HIGHEST, preferred_element_type=F32) + br_ref[...]
    tr = logits.shape[0]
    lane = lax.broadcasted_iota(I32, (tr, LANES), 1)
    lane_f = lane.astype(F32)
    neg = jnp.float32(-jnp.inf)
    big = jnp.float32(1e9)

    is_group = (lane >= GROUP_LANE0) & (lane < GROUP_LANE0 + N_GROUPS)
    glog = jnp.where(is_group, logits, neg)
    gmax = jnp.max(glog, axis=1, keepdims=True)
    p_top = 1.0 / jnp.sum(jnp.exp(glog - gmax), axis=1, keepdims=True)
    g_sel = jnp.min(jnp.where(glog == gmax, lane_f, big), axis=1, keepdims=True) - GROUP_LANE0
    lo = g_sel * EXPERTS_PER_GROUP
    in_group = (lane_f >= lo) & (lane_f < lo + EXPERTS_PER_GROUP)
    f1 = jnp.where(in_group, logits, neg)
    v1 = jnp.max(f1, axis=1, keepdims=True)
    i1 = jnp.min(jnp.where(f1 == v1, lane_f, big), axis=1, keepdims=True)
    f2 = jnp.where(lane_f == i1, neg, f1)
    v2 = jnp.max(f2, axis=1, keepdims=True)
    i2 = jnp.min(jnp.where(f2 == v2, lane_f, big), axis=1, keepdims=True)
    a = jnp.exp(v2 - v1)
    w1 = 1.0 / (1.0 + a)
    gate0 = p_top * w1
    gate1 = p_top * (a * w1)

    oh0 = lane_f == i1
    oh1 = lane_f == i2
    ind0 = jnp.where(oh0, 1.0, 0.0)
    ind1 = jnp.where(oh1, 1.0, 0.0)
    pre0 = jnp.dot(tri_ref[...], ind0.astype(BF16), preferred_element_type=F32)
    pre1 = jnp.dot(tri_ref[...], ind1.astype(BF16), preferred_element_type=F32)
    tot0 = jnp.sum(ind0, axis=0, keepdims=True)
    tot1 = jnp.sum(ind1, axis=0, keepdims=True)
    seg_len = jnp.floor((tot0 + tot1 + (SEG_ALIGN - 1)) * (1.0 / SEG_ALIGN)) * SEG_ALIGN
    seg_off = jnp.dot(jnp.broadcast_to(seg_len, (8, LANES)), upper_ref[...], precision=HIGHEST,
                      preferred_element_type=F32)[0:1]
    pos0 = jnp.sum(jnp.where(oh0, seg_off + pre0, 0.0), axis=1, keepdims=True)
    pos1 = jnp.sum(jnp.where(oh1, seg_off + tot0 + pre1, 0.0), axis=1, keepdims=True)
    base = base_scr[...]
    new_base = base + seg_len
    base_scr[...] = new_base
    cnt_ref[...] = new_base

    meta = jnp.zeros((tr, LANES), F32)
    for col, val in ((META_GATE0, gate0), (META_GATE1, gate1), (META_POS0, pos0), (META_POS1, pos1)):
        meta = jnp.where(lane == col, val, meta)
    meta_ref[...] = meta
    post_ref[0] = meta.T[0:8].astype(I32)
    srow = lax.broadcasted_iota(I32, (8, LANES), 0)
    seg = jnp.where(srow == SEG_LEN, seg_len, jnp.where(srow == SEG_BASE, base, jnp.where(srow == SEG_OFF, seg_off, 0.0)))
    seg_ref[0] = seg.astype(I32)


def _moe_router(x2, g, sh, sc, wr_g, br_g, wr_e, br_e, seq):
    n_tok, d = x2.shape
    bsz = n_tok // seq
    tr = min(ROUTER_ROWS, seq)
    steps_per_batch = seq // tr
    wr = jnp.zeros((d, LANES), F32).at[:, :N_EXPERTS].set(wr_e).at[:, GROUP_LANE0:GROUP_LANE0 + N_GROUPS].set(wr_g)
    br = jnp.zeros((1, LANES), F32).at[0, :N_EXPERTS].set(br_e).at[0, GROUP_LANE0:GROUP_LANE0 + N_GROUPS].set(br_g)
    tri = jnp.tril(jnp.ones((tr, tr), BF16), -1)
    upper = jnp.triu(jnp.ones((LANES, LANES), F32), 1)
    nb = n_tok // tr
    full = lambda shape: pl.BlockSpec(shape, lambda i: (0,) * len(shape))
    return pl.pallas_call(
        _router_body,
        grid=(nb,),
        in_specs=[pl.BlockSpec((tr, d), lambda i: (i, 0)),
                  full((1, d)),
                  pl.BlockSpec((1, 1, d), lambda i: (i // steps_per_batch, 0, 0)),
                  pl.BlockSpec((1, 1, d), lambda i: (i // steps_per_batch, 0, 0)),
                  full((d, LANES)), full((1, LANES)), full((tr, tr)), full((LANES, LANES))],
        out_specs=[pl.BlockSpec((tr, d), lambda i: (i, 0)),
                   pl.BlockSpec((tr, LANES), lambda i: (i, 0)),
                   pl.BlockSpec((1, 8, tr), lambda i: (i, 0, 0)),
                   pl.BlockSpec((1, 8, LANES), lambda i: (i, 0, 0)),
                   full((1, LANES))],
        out_shape=[jax.ShapeDtypeStruct((n_tok, d), BF16),
                   jax.ShapeDtypeStruct((n_tok, LANES), F32),
                   jax.ShapeDtypeStruct((nb, 8, tr), I32),
                   jax.ShapeDtypeStruct((nb, 8, LANES), I32),
                   jax.ShapeDtypeStruct((1, LANES), F32)],
        scratch_shapes=[pltpu.VMEM((1, LANES), F32)],
        compiler_params=_cparams(("arbitrary",)),
        name="moe_router",
    )(x2, g.reshape(1, d), sh.reshape(bsz, 1, d), sc.reshape(bsz, 1, d), wr, br, tri, upper)


def _segment_copies(seg_ref, starts_ref, make_copy, action):
    def per_expert(e, carry):
        length = seg_ref[0, SEG_LEN, e]
        local = seg_ref[0, SEG_OFF, e]
        glob = starts_ref[e] + seg_ref[0, SEG_BASE, e]
        for size in SEG_SIZES:
            hit = (length & size) != 0

            @pl.when(hit)
            def _(local=local, glob=glob, size=size):
                cp = make_copy(pl.multiple_of(local, SEG_ALIGN), pl.multiple_of(glob, SEG_ALIGN), size)
                cp.start() if action == "start" else cp.wait()

            step = jnp.where(hit, size, 0)
            local = local + step
            glob = glob + step
        return carry

    lax.fori_loop(0, N_EXPERTS, per_expert, 0)


def _local_rows(tr):
    return 2 * tr + N_EXPERTS * SEG_ALIGN


def _dispatch_body(starts_ref, seg_ref, h_ref, post_ref, xr_ref, xs_scr, zero_scr, sem, *, tr):
    i = pl.program_id(0)
    lb = xs_scr.shape[0]
    rows = lax.broadcasted_iota(I32, (lb, tr), 0)
    place = (rows == post_ref[0, META_POS0:META_POS0 + 1, :]) | (rows == post_ref[0, META_POS1:META_POS1 + 1, :])
    xs_scr[...] = jnp.dot(jnp.where(place, 1.0, 0.0).astype(BF16), h_ref[...], preferred_element_type=F32)

    def make_copy(local, glob, size):
        return pltpu.make_async_copy(xs_scr.at[pl.ds(local, size)], xr_ref.at[pl.ds(glob, size)], sem)

    _segment_copies(seg_ref, starts_ref, make_copy, "start")
    _segment_copies(seg_ref, starts_ref, make_copy, "wait")

    @pl.when(i == pl.num_programs(0) - 1)
    def _():
        zero_scr[...] = jnp.zeros_like(zero_scr)
        total = starts_ref[N_EXPERTS]
        tail = (-total) & (MOE_ROWS - 1)
        for action in ("start", "wait"):
            row = total
            for size in SEG_SIZES:
                if size >= MOE_ROWS:
                    continue
                hit = (tail & size) != 0

                @pl.when(hit)
                def _(row=row, size=size, action=action):
                    cp = pltpu.make_async_copy(zero_scr.at[pl.ds(0, size)],
                                               xr_ref.at[pl.ds(pl.multiple_of(row, SEG_ALIGN), size)], sem)
                    cp.start() if action == "start" else cp.wait()

                row = row + jnp.where(hit, size, 0)

        def free_block(b):
            return pltpu.make_async_copy(
                zero_scr, xr_ref.at[pl.ds(pl.multiple_of(b * MOE_ROWS, MOE_ROWS), MOE_ROWS)], sem)

        first_free = (total + MOE_ROWS - 1) // MOE_ROWS
        n_blocks = xr_ref.shape[0] // MOE_ROWS
        lax.fori_loop(first_free, n_blocks, lambda b, c: (free_block(b).start(), c)[1], 0)
        lax.fori_loop(first_free, n_blocks, lambda b, c: (free_block(b).wait(), c)[1], 0)


def _moe_dispatch(h2, post, seg, starts, n_rows, tr):
    n_tok, d = h2.shape
    grid_spec = pltpu.PrefetchScalarGridSpec(
        num_scalar_prefetch=1,
        grid=(n_tok // tr,),
        in_specs=[pl.BlockSpec((1, 8, LANES), lambda i, s: (i, 0, 0), memory_space=pltpu.SMEM),
                  pl.BlockSpec((tr, d), lambda i, s: (i, 0)),
                  pl.BlockSpec((1, 8, tr), lambda i, s: (i, 0, 0))],
        out_specs=pl.BlockSpec(memory_space=pl.ANY),
        scratch_shapes=[pltpu.VMEM((_local_rows(tr), d), F32), pltpu.VMEM((MOE_ROWS, d), F32),
                        pltpu.SemaphoreType.DMA(())],
    )
    return pl.pallas_call(
        functools.partial(_dispatch_body, tr=tr),
        grid_spec=grid_spec,
        out_shape=jax.ShapeDtypeStruct((n_rows, d), F32),
        compiler_params=_cparams(("arbitrary",)),
        name="moe_dispatch",
    )(starts, seg, h2, post)


def _expert_body(pb_ref, pe_ref, plo_ref, phi_ref, x_ref, wg_ref, wu_ref, wd_ref, o_ref, wg_s, wu_s, wd_s):
    p = pl.program_id(0)
    prev = jnp.maximum(p - 1, 0)
    new_expert = (p == 0) | (pe_ref[p] != pe_ref[prev])
    first = (p == 0) | (pb_ref[p] != pb_ref[prev])

    @pl.when(new_expert)
    def _():
        wg_s[...] = wg_ref[0].astype(BF16)
        wu_s[...] = wu_ref[0].astype(BF16)
        wd_s[...] = wd_ref[0].astype(BF16)

    def rows_of_expert():
        x = x_ref[...].astype(BF16)
        gt = jnp.dot(x, wg_s[...], preferred_element_type=F32)
        up = jnp.dot(x, wu_s[...], preferred_element_type=F32)
        act = (gt * _sigmoid(gt)) * up
        y = jnp.dot(act.astype(BF16), wd_s[...], preferred_element_type=F32)
        rows = lax.broadcasted_iota(I32, (y.shape[0], 1), 0)
        return jnp.where((rows >= plo_ref[p]) & (rows < phi_ref[p]), y, 0.0)

    nonempty = phi_ref[p] > plo_ref[p]

    @pl.when(first & nonempty)
    def _():
        o_ref[...] = rows_of_expert()

    @pl.when(first & jnp.logical_not(nonempty))
    def _():
        o_ref[...] = jnp.zeros_like(o_ref)

    @pl.when(jnp.logical_not(first) & nonempty)
    def _():
        o_ref[...] += rows_of_expert()


def _moe_experts(x_rows, pairs, w_gate, w_up, w_down):
    n_rows, d = x_rows.shape
    de = w_gate.shape[-1]
    n_pairs = pairs[0].shape[0]
    grid_spec = pltpu.PrefetchScalarGridSpec(
        num_scalar_prefetch=4,
        grid=(n_pairs,),
        in_specs=[pl.BlockSpec((MOE_ROWS, d), lambda p, pb, pe, lo, hi: (pb[p], 0)),
                  pl.BlockSpec((1, d, de), lambda p, pb, pe, lo, hi: (pe[p], 0, 0)),
                  pl.BlockSpec((1, d, de), lambda p, pb, pe, lo, hi: (pe[p], 0, 0)),
                  pl.BlockSpec((1, de, d), lambda p, pb, pe, lo, hi: (pe[p], 0, 0))],
        out_specs=pl.BlockSpec((MOE_ROWS, d), lambda p, pb, pe, lo, hi: (pb[p], 0)),
        scratch_shapes=[pltpu.VMEM((d, de), BF16), pltpu.VMEM((d, de), BF16), pltpu.VMEM((de, d), BF16)],
    )
    return pl.pallas_call(
        _expert_body,
        grid_spec=grid_spec,
        out_shape=jax.ShapeDtypeStruct((n_rows, d), F32),
        compiler_params=_cparams(("arbitrary",)),
        name="moe_experts",
    )(*pairs, x_rows, w_gate, w_up, w_down)


def _combine_body(starts_ref, seg_ref, y_ref, x_ref, meta_ref, g2_ref, fg_ref, o_ref, ys_scr, sem, *, final_norm):
    @pl.when(pl.program_id(0) == 0)
    def _():
        ys_scr[...] = jnp.zeros_like(ys_scr)

    def make_copy(local, glob, size):
        return pltpu.make_async_copy(y_ref.at[pl.ds(glob, size)], ys_scr.at[pl.ds(local, size)], sem)

    _segment_copies(seg_ref, starts_ref, make_copy, "start")
    _segment_copies(seg_ref, starts_ref, make_copy, "wait")

    meta = meta_ref[...]
    tr = meta.shape[0]
    y = ys_scr[...].astype(BF16)
    cols = lax.broadcasted_iota(I32, (tr, ys_scr.shape[0]), 1)

    def picked(col):
        pos = meta[:, col:col + 1].astype(I32)
        return jnp.dot(jnp.where(cols == pos, 1.0, 0.0).astype(BF16), y, preferred_element_type=F32)

    moe = meta[:, META_GATE0:META_GATE0 + 1] * picked(META_POS0) + meta[:, META_GATE1:META_GATE1 + 1] * picked(META_POS1)
    xn = x_ref[...] + g2_ref[0] * moe
    if final_norm:
        xn = _rms(xn) * fg_ref[...]
    o_ref[...] = xn


def _moe_combine(y_rows, seg, starts, x2, meta, g2, final_g, seq, tr, final_norm):
    n_tok, d = x2.shape
    bsz = n_tok // seq
    steps_per_batch = seq // tr
    grid_spec = pltpu.PrefetchScalarGridSpec(
        num_scalar_prefetch=1,
        grid=(n_tok // tr,),
        in_specs=[pl.BlockSpec((1, 8, LANES), lambda i, s: (i, 0, 0), memory_space=pltpu.SMEM),
                  pl.BlockSpec(memory_space=pl.ANY),
                  pl.BlockSpec((tr, d), lambda i, s: (i, 0)),
                  pl.BlockSpec((tr, LANES), lambda i, s: (i, 0)),
                  pl.BlockSpec((1, 1, d), lambda i, s: (i // steps_per_batch, 0, 0)),
                  pl.BlockSpec((1, d), lambda i, s: (0, 0))],
        out_specs=pl.BlockSpec((tr, d), lambda i, s: (i, 0)),
        scratch_shapes=[pltpu.VMEM((_local_rows(tr), d), F32), pltpu.SemaphoreType.DMA(())],
    )
    return pl.pallas_call(
        functools.partial(_combine_body, final_norm=final_norm),
        grid_spec=grid_spec,
        out_shape=jax.ShapeDtypeStruct((n_tok, d), F32),
        compiler_params=_cparams(("arbitrary",)),
        name="moe_combine",
    )(starts, seg, y_rows, x2, meta, g2.reshape(bsz, 1, d), final_g.reshape(1, d))


def _expert_pairs(counts, n_rows):
    n_blocks = n_rows // MOE_ROWS
    n_pairs = n_blocks + N_EXPERTS
    ends = jnp.cumsum(counts)
    starts = ends - counts
    first_blk = starts // MOE_ROWS
    last_blk = (ends - 1) // MOE_ROWS
    npairs = jnp.where(counts > 0, last_blk - first_blk + 1, 0)
    pend = jnp.cumsum(npairs)
    poff = pend - npairs
    total = pend[-1]
    used_blocks = (ends[-1] + MOE_ROWS - 1) // MOE_ROWS
    p = jnp.arange(n_pairs, dtype=I32)
    p_eff = jnp.minimum(p, total - 1)
    e = jnp.minimum(jnp.sum(pend[None, :] <= p_eff[:, None], axis=1), N_EXPERTS - 1).astype(I32)
    valid = p < total
    blk = jnp.where(valid, first_blk[e] + p_eff - poff[e], jnp.minimum(used_blocks + p - total, n_blocks - 1)).astype(I32)
    lo = jnp.where(valid, jnp.clip(starts[e] - blk * MOE_ROWS, 0, MOE_ROWS), 0).astype(I32)
    hi = jnp.where(valid, jnp.clip(ends[e] - blk * MOE_ROWS, 0, MOE_ROWS), 0).astype(I32)
    return blk, e, lo, hi


def _hier_moe(x, g, sh, sc, g2, wr_g, br_g, wr_e, br_e, w_gate, w_up, w_down, final_g, final_norm):
    bsz, seq, d = x.shape
    n_tok = bsz * seq
    tr = min(ROUTER_ROWS, seq)
    assert 2 * tr <= SEG_SIZES[0]
    x2 = x.reshape(n_tok, d)
    h2, meta, post, seg, cnt = _moe_router(x2, g, sh, sc, wr_g, br_g, wr_e, br_e, seq)
    counts = cnt[0, :N_EXPERTS].astype(I32)
    ends = jnp.cumsum(counts)
    starts = jnp.concatenate([ends - counts, ends[-1:]])
    n_rows = -(-(2 * n_tok + (n_tok // tr) * N_EXPERTS * (SEG_ALIGN - 1)) // MOE_ROWS) * MOE_ROWS
    x_rows = _moe_dispatch(h2, post, seg, starts, n_rows, tr)
    y_rows = _moe_experts(x_rows, _expert_pairs(counts, n_rows), w_gate, w_up, w_down)
    out = _moe_combine(y_rows, seg, starts, x2, meta, g2, final_g, seq, tr, final_norm)
    return out.reshape(bsz, seq, d)


SEL_COLS = 256
SEL_CHUNK = 512
SEL_FOLD = 32
ATT_TILE = 512


def _sortable(score):
    bits = lax.bitcast_convert_type(score, I32)
    return bits ^ ((bits >> 31) & jnp.int32(0x7FFFFFFF))


def _idx_queries(zq):
    zf = zq.astype(F32)
    return [zf[:, h * IDX_DIM:(h + 1) * IDX_DIM].astype(BF16) for h in range(IDX_HEADS)]


def _idx_score_t(k, qs, wt):
    acc = None
    for h, q in enumerate(qs):
        term = jnp.maximum(_dot_nt(k, q), 0.0) * wt[h:h + 1, :]
        acc = term if acc is None else acc + term
    return acc


def _select_body(zq_ref, zk_ref, wt_ref, o_ref, keys_scr, cap_scr, *, topk):
    i = pl.program_id(1)
    tq = zq_ref.shape[1]
    qs = _idx_queries(zq_ref[0])
    wt = wt_ref[0].astype(F32)
    n_chunks = ((i + 1) * tq + SEL_CHUNK - 1) // SEL_CHUNK
    krow0 = lax.broadcasted_iota(I32, (SEL_CHUNK, tq), 0)
    qcol = i * tq + lax.broadcasted_iota(I32, (SEL_CHUNK, tq), 1)
    int_min = jnp.int32(INT_MIN)

    def fill(c, carry):
        off = pl.multiple_of(c * SEL_CHUNK, SEL_CHUNK)
        k = zk_ref[0, pl.ds(off, SEL_CHUNK), :][:, 0:IDX_DIM]
        key = _sortable(_idx_score_t(k, qs, wt))
        keys_scr[pl.ds(off, SEL_CHUNK), :] = jnp.where(krow0 + off <= qcol, key, int_min)
        return carry

    lax.fori_loop(0, n_chunks, fill, 0)

    def count(pred):
        def body(c, acc):
            off = pl.multiple_of(c * SEL_CHUNK, SEL_CHUNK)
            ind = jnp.where(pred(keys_scr[pl.ds(off, SEL_CHUNK), :], krow0 + off), 1, 0)
            return acc + jnp.sum(ind.reshape(SEL_CHUNK // SEL_FOLD, SEL_FOLD, tq), axis=0)
        acc = lax.fori_loop(0, n_chunks, body, jnp.zeros((SEL_FOLD, tq), I32))
        return jnp.sum(acc.astype(F32), axis=0, keepdims=True)

    k_f = jnp.float32(topk)

    def bit_step(ib, u):
        cand_u = u | jnp.left_shift(jnp.int32(1), 31 - ib)
        cand = cand_u ^ int_min
        cnt = count(lambda keys, kidx: keys >= cand)
        return jnp.where(cnt >= k_f, cand_u, u)

    u = lax.fori_loop(0, 32, bit_step, jnp.zeros((1, tq), I32))
    thr = jnp.maximum(u ^ int_min, int_min + 1)

    cap_scr[...] = jnp.full((1, tq), 2 ** 30, I32)
    n_ge = count(lambda keys, kidx: keys >= thr)

    @pl.when(jnp.max(n_ge) > k_f)
    def _():
        n_gt = count(lambda keys, kidx: keys > thr)
        need = k_f - n_gt

        def cap_step(ib, v):
            cand = v | jnp.left_shift(jnp.int32(1), 14 - ib)
            cnt = count(lambda keys, kidx: (keys == thr) & (kidx < cand))
            return jnp.where(cnt <= need, cand, v)

        cap_scr[...] = lax.fori_loop(0, 15, cap_step, jnp.zeros((1, tq), I32))

    cap = cap_scr[...]

    def emit(c, carry):
        off = pl.multiple_of(c * SEL_CHUNK, SEL_CHUNK)
        keys = keys_scr[pl.ds(off, SEL_CHUNK), :]
        keep = (keys > thr) | ((keys == thr) & (krow0 + off < cap))
        o_ref[0, pl.ds(off, SEL_CHUNK), :] = jnp.where(keep, 0.0, -jnp.inf).astype(o_ref.dtype)
        return carry

    lax.fori_loop(0, n_chunks, emit, 0)

    def blank(c, carry):
        off = pl.multiple_of(c * SEL_CHUNK, SEL_CHUNK)
        o_ref[0, pl.ds(off, SEL_CHUNK), :] = jnp.full((SEL_CHUNK, tq), -jnp.inf, o_ref.dtype)
        return carry

    lax.fori_loop(n_chunks, o_ref.shape[1] // SEL_CHUNK, blank, 0)


def _select(z, zt, topk):
    bsz, seq, _ = z.shape
    tq = min(SEL_COLS, seq)
    assert seq % SEL_CHUNK == 0 and seq % tq == 0
    return pl.pallas_call(
        functools.partial(_select_body, topk=topk),
        grid=(bsz, seq // tq),
        in_specs=[pl.BlockSpec((1, tq, Z_QIDX_W), lambda b, i: (b, i, Z_QIDX_OFF // Z_QIDX_W)),
                  pl.BlockSpec((1, seq, LANES), lambda b, i: (b, 0, Z_KIDX_OFF // LANES)),
                  pl.BlockSpec((1, BF16_ROWS, tq), lambda b, i: (b, ZT_W_OFF // BF16_ROWS, i))],
        out_specs=pl.BlockSpec((1, seq, tq), lambda b, i: (b, 0, i)),
        out_shape=jax.ShapeDtypeStruct((bsz, seq, seq), BF16),
        scratch_shapes=[pltpu.VMEM((seq, tq), I32), pltpu.VMEM((1, tq), I32)],
        compiler_params=_cparams(("arbitrary", "arbitrary")),
        name="dsa_select",
    )(z, z, zt)


BIAS_INIT_ROWS = 8
ONES_ROWS = BF16_ROWS


def _init_bias_tiles(btile, bias_ref, head0, n_heads, ta):
    col = lax.broadcasted_iota(I32, (BIAS_INIT_ROWS, ta), 1)
    row0 = lax.broadcasted_iota(I32, (BIAS_INIT_ROWS, ta), 0)

    def body(r, carry):
        off = pl.multiple_of(r * BIAS_INIT_ROWS, BIAS_INIT_ROWS)
        for kind in range(2):
            dist = col - (row0 + off) + kind * ta
            for h in range(n_heads):
                far = bias_ref[NUM_BUCKETS - 1, head0 + h]
                val = jnp.full((BIAS_INIT_ROWS, ta), (bias_ref[0, head0 + h] - far) * LOG2E, F32)
                for b in range(1, NUM_BUCKETS - 1):
                    val = jnp.where(dist >= BUCKET_START[b], (bias_ref[b, head0 + h] - far) * LOG2E, val)
                val = jnp.where(dist >= BUCKET_START[NUM_BUCKETS - 1], 0.0, val)
                btile[h, kind, pl.ds(off, BIAS_INIT_ROWS), :] = val
        return carry

    lax.fori_loop(0, ta // BIAS_INIT_ROWS, body, 0)


def _with_ones(vt):
    return jnp.concatenate([vt, jnp.ones((ONES_ROWS, vt.shape[1]), vt.dtype)], axis=0)


def _softmax_step_t(logits, v_aug, m_ref, acc_ref, idx):
    m_old = m_ref[idx]
    m_new = jnp.maximum(m_old, jnp.max(logits, axis=0, keepdims=True))
    m_safe = jnp.where(m_new == -jnp.inf, 0.0, m_new)
    p = jnp.exp2(logits - m_safe)
    alpha = jnp.exp2(m_old - m_safe)
    acc_ref[idx] = alpha * acc_ref[idx] + jnp.dot(v_aug, p.astype(BF16), preferred_element_type=F32)
    m_ref[idx] = m_new


def _reset_softmax(m_s, acc_s):
    m_s[...] = jnp.full(m_s.shape, -jnp.inf, F32)
    acc_s[...] = jnp.zeros(acc_s.shape, F32)


def _cattn_body(qi_ref, ki_ref, bias_ref, zq_ref, zk_ref, vt_ref, mask_ref, o_ref, m_s, acc_s, btile):
    b = pl.program_id(0)
    p = pl.program_id(1)
    qi = qi_ref[p]
    ki = ki_ref[p]
    ta = zq_ref.shape[1]
    hd = C_HEAD_DIM

    @pl.when((b == 0) & (p == 0))
    def _():
        _init_bias_tiles(btile, bias_ref, 0, C_HEADS, ta)

    @pl.when(ki == 0)
    def _():
        _reset_softmax(m_s, acc_s)

    def heads(extra_of_head):
        for h in range(C_HEADS):
            lo = h * hd
            logits = _dot_nt(zk_ref[0, :, lo:lo + hd], zq_ref[0, :, lo:lo + hd]) + extra_of_head(h)
            _softmax_step_t(logits, _with_ones(vt_ref[0, lo:lo + hd, :]), m_s, acc_s, h)

    near = ki >= qi - 1

    @pl.when(near)
    def _():
        heads(lambda h: mask_ref[0].astype(F32) + btile[h, qi - ki])

    @pl.when(jnp.logical_not(near))
    def _():
        heads(lambda h: mask_ref[0].astype(F32))

    @pl.when(ki == qi)
    def _():
        outs = []
        for h in range(C_HEADS):
            a = acc_s[h]
            outs.append((a[0:hd] / a[hd:hd + 1]).T)
        o_ref[0] = jnp.concatenate(outs, axis=1).astype(o_ref.dtype)


def _dattn_body(qi_ref, ki_ref, bias_ref, zq_ref, zk_ref, vt_ref, lam_ref, ng_ref, o_ref,
                m_s, acc_s, btile, *, lambda_init):
    b = pl.program_id(0)
    p = pl.program_id(1)
    qi = qi_ref[p]
    ki = ki_ref[p]
    ta = zq_ref.shape[1]
    hd = DIFF_HEAD_DIM
    dv = 2 * hd

    @pl.when((b == 0) & (p == 0))
    def _():
        _init_bias_tiles(btile, bias_ref, C_HEADS, DIFF_HEADS, ta)

    @pl.when(ki == 0)
    def _():
        _reset_softmax(m_s, acc_s)

    def heads(extra_of_head):
        for h in range(DIFF_HEADS):
            v_aug = _with_ones(vt_ref[0, dv * h:dv * (h + 1), :])
            extra = extra_of_head(h)
            for j in range(2):
                lo = (2 * h + j) * hd
                logits = _dot_nt(zk_ref[0, :, lo:lo + hd], zq_ref[0, :, lo:lo + hd])
                if extra is not None:
                    logits = logits + extra
                _softmax_step_t(logits, v_aug, m_s, acc_s, 2 * h + j)

    near = ki >= qi - 1

    @pl.when(near)
    def _():
        krow = ki * ta + lax.broadcasted_iota(I32, (ta, ta), 0)
        qcol = qi * ta + lax.broadcasted_iota(I32, (ta, ta), 1)
        heads(lambda h: jnp.where(krow <= qcol, btile[h, qi - ki], -jnp.inf))

    @pl.when(jnp.logical_not(near))
    def _():
        heads(lambda h: None)

    @pl.when(ki == qi)
    def _():
        lam_p = lam_ref[...]
        lam = (jnp.exp(jnp.sum(lam_p[0:1] * lam_p[1:2], axis=1, keepdims=True))
               - jnp.exp(jnp.sum(lam_p[2:3] * lam_p[3:4], axis=1, keepdims=True)) + lambda_init)
        outs = []
        for h in range(DIFF_HEADS):
            a1 = acc_s[2 * h]
            a2 = acc_s[2 * h + 1]
            o = a1[0:dv] / a1[dv:dv + 1] - lam * (a2[0:dv] / a2[dv:dv + 1])
            o = o * lax.rsqrt(jnp.mean(o * o, axis=0, keepdims=True) + EPS) * ng_ref[...] * (1.0 - lambda_init)
            outs.append(o.T)
        o_ref[0] = jnp.concatenate(outs, axis=1).astype(o_ref.dtype)


def _causal_pairs(nq):
    qi = [q for q in range(nq) for _ in range(q + 1)]
    ki = [k for q in range(nq) for k in range(q + 1)]
    return jnp.asarray(qi, I32), jnp.asarray(ki, I32)


def _cattn(z, zt, mask, rel_bias, ta):
    bsz, seq, _ = z.shape
    qi, ki = _causal_pairs(seq // ta)
    qmap = lambda col: (lambda b, p, qi, ki: (b, qi[p], col))
    kmap = lambda col: (lambda b, p, qi, ki: (b, ki[p], col))
    dv_aug = C_HEAD_DIM + ONES_ROWS
    grid_spec = pltpu.PrefetchScalarGridSpec(
        num_scalar_prefetch=2,
        grid=(bsz, qi.shape[0]),
        in_specs=[pl.BlockSpec(memory_space=pltpu.SMEM),
                  pl.BlockSpec((1, ta, C_WIDTH), qmap(Z_QC)),
                  pl.BlockSpec((1, ta, C_WIDTH), kmap(Z_KC)),
                  pl.BlockSpec((1, C_WIDTH, ta), lambda b, p, qi, ki: (b, ZT_VC, ki[p])),
                  pl.BlockSpec((1, ta, ta), lambda b, p, qi, ki: (b, ki[p], qi[p]))],
        out_specs=pl.BlockSpec((1, ta, C_WIDTH), qmap(0)),
        scratch_shapes=[pltpu.VMEM((C_HEADS, 1, ta), F32),
                        pltpu.VMEM((C_HEADS, dv_aug, ta), F32),
                        pltpu.VMEM((C_HEADS, 2, ta, ta), F32)],
    )
    return pl.pallas_call(
        _cattn_body,
        grid_spec=grid_spec,
        out_shape=jax.ShapeDtypeStruct((bsz, seq, C_WIDTH), BF16),
        compiler_params=_cparams(("arbitrary", "arbitrary")),
        name="dsa_attn",
    )(qi, ki, rel_bias, z, z, zt, mask)


def _dattn(z, zt, rel_bias, diff_lam, diff_norm_g, lambda_init, ta):
    bsz, seq, _ = z.shape
    qi, ki = _causal_pairs(seq // ta)
    n_maps = 2 * DIFF_HEADS
    dv = 2 * DIFF_HEAD_DIM
    qmap = lambda col: (lambda b, p, qi, ki: (b, qi[p], col))
    kmap = lambda col: (lambda b, p, qi, ki: (b, ki[p], col))
    grid_spec = pltpu.PrefetchScalarGridSpec(
        num_scalar_prefetch=2,
        grid=(bsz, qi.shape[0]),
        in_specs=[pl.BlockSpec(memory_space=pltpu.SMEM),
                  pl.BlockSpec((1, ta, DIFF_W), qmap(Z_QD)),
                  pl.BlockSpec((1, ta, DIFF_W), kmap(Z_KD)),
                  pl.BlockSpec((1, DIFF_W, ta), lambda b, p, qi, ki: (b, ZT_VD, ki[p])),
                  pl.BlockSpec(diff_lam.shape, lambda b, p, qi, ki: (0, 0)),
                  pl.BlockSpec((dv, 1), lambda b, p, qi, ki: (0, 0))],
        out_specs=pl.BlockSpec((1, ta, DIFF_W), qmap(0)),
        scratch_shapes=[pltpu.VMEM((n_maps, 1, ta), F32),
                        pltpu.VMEM((n_maps, dv + ONES_ROWS, ta), F32),
                        pltpu.VMEM((DIFF_HEADS, 2, ta, ta), F32)],
    )
    return pl.pallas_call(
        functools.partial(_dattn_body, lambda_init=lambda_init),
        grid_spec=grid_spec,
        out_shape=jax.ShapeDtypeStruct((bsz, seq, DIFF_W), BF16),
        compiler_params=_cparams(("arbitrary", "arbitrary")),
        name="diff_attn",
    )(qi, ki, rel_bias, z, z, zt, diff_lam, diff_norm_g.reshape(dv, 1))


def _out_proj_body(oc_ref, od_ref, x_ref, g1_ref, w_ref, o_ref):
    y = (jnp.dot(oc_ref[0], w_ref[0:C_WIDTH, :], preferred_element_type=F32)
         + jnp.dot(od_ref[0], w_ref[C_WIDTH:C_WIDTH + DIFF_W, :], preferred_element_type=F32))
    o_ref[0] = x_ref[0] + g1_ref[0] * y


def _out_proj(out_c, out_d, x, g1, w_out_bf16, tm):
    bsz, seq, d = x.shape
    return pl.pallas_call(
        _out_proj_body,
        grid=(bsz, seq // tm),
        in_specs=[pl.BlockSpec((1, tm, C_WIDTH), lambda b, i: (b, i, 0)),
                  pl.BlockSpec((1, tm, DIFF_W), lambda b, i: (b, i, 0)),
                  pl.BlockSpec((1, tm, d), lambda b, i: (b, i, 0)),
                  pl.BlockSpec((1, 1, d), lambda b, i: (b, 0, 0)),
                  pl.BlockSpec((C_WIDTH + DIFF_W, d), lambda b, i: (0, 0))],
        out_specs=pl.BlockSpec((1, tm, d), lambda b, i: (b, i, 0)),
        out_shape=jax.ShapeDtypeStruct((bsz, seq, d), F32),
        compiler_params=_cparams(("arbitrary", "arbitrary")),
        name="attn_out_proj",
    )(out_c, out_d, x, g1.reshape(bsz, 1, d), w_out_bf16)


def _attn_in_weights(cd_w_in):
    sizes = (C_WIDTH, C_WIDTH, C_WIDTH, IDX_HEADS * IDX_DIM, IDX_DIM, IDX_HEADS, DIFF_W, DIFF_W, DIFF_W)
    cuts = np.cumsum(sizes)[:-1]
    q_c, k_c, v_c, q_i, k_i, w_i, q_d, k_d, v_d = jnp.split(cd_w_in, cuts, axis=1)
    d = cd_w_in.shape[0]
    w = jnp.concatenate([q_c * (C_HEAD_DIM ** -0.5 * LOG2E), k_c, q_d * (DIFF_HEAD_DIM ** -0.5 * LOG2E), k_d,
                         q_i, k_i, jnp.zeros((d, LANES - IDX_DIM), cd_w_in.dtype)], axis=1)
    wt = jnp.concatenate([v_c, v_d, w_i * (IDX_DIM * IDX_HEADS) ** -0.5,
                          jnp.zeros((d, BF16_ROWS - IDX_HEADS), cd_w_in.dtype)], axis=1).T
    return w.astype(BF16), wt.astype(BF16)


def kernel(x, c, positions, rel_bias, norm_g, final_norm_g, ada_w, ada_b, ab_w_in, ab_conv_a, ab_conv_b,
           ab_conv_b_bias, ab_ln_g, ab_ln_b, ab_w_out, cd_w_in, diff_lam, diff_norm_g, cd_w_out,
           moe_wr_g, moe_br_g, moe_wr_e, moe_br_e, moe_w_gate, moe_w_up, moe_w_down):
    del positions
    bsz, seq, d = x.shape
    depth = ada_w.shape[0]
    tm = min(512, seq)
    ta = min(ATT_TILE, seq)
    assert ta >= MAX_DISTANCE and seq % ta == 0
    topk = min(TOPK_MAX, seq // 4)
    mods = _ada_mod(c, ada_w, ada_b)
    for i in range(depth):
        sh1, sc1, g1, sh2, sc2, g2 = jnp.split(mods[i], 6, axis=-1)
        j = i // 2
        if i % 2 == 0:
            z = _norm_proj(x, norm_g[i, 0], sh1, sc1, ab_w_in[j].astype(BF16), tm)
            x = _conv_mix(z, x, g1, ab_conv_a[j], ab_conv_b[j], ab_conv_b_bias[j], ab_ln_g[j], ab_ln_b[j],
                          ab_w_out[j].astype(BF16), min(256, seq))
        else:
            lambda_init = 0.8 - 0.6 * math.exp(-0.3 * i)
            w, wt = _attn_in_weights(cd_w_in[j])
            z, zt = _norm_proj(x, norm_g[i, 0], sh1, sc1, w, tm, wt)
            mask = _select(z, zt, topk)
            out_c = _cattn(z, zt, mask, rel_bias, ta)
            out_d = _dattn(z, zt, rel_bias, diff_lam[j], diff_norm_g[j], lambda_init, ta)
            x = _out_proj(out_c, out_d, x, g1, cd_w_out[j].astype(BF16), tm)
        x = _hier_moe(x, norm_g[i, 1], sh2, sc2, g2, moe_wr_g[i], moe_br_g[i], moe_wr_e[i], moe_br_e[i],
                      moe_w_gate[i], moe_w_up[i], moe_w_down[i], final_norm_g, final_norm=(i == depth - 1))
    return x
```

```python
import functools
import math

import numpy as np
import jax
import jax.numpy as jnp
from jax import lax
from jax.experimental import pallas as pl
from jax.experimental.pallas import tpu as pltpu

F32 = jnp.float32
BF16 = jnp.bfloat16
I32 = jnp.int32
HIGHEST = lax.Precision.HIGHEST

EPS = 1e-6
A_WIDTH = 512
A_CONV = 3
B_WIDTH = 512
B_CONV = 31
C_HEADS = 8
C_HEAD_DIM = 64
IDX_HEADS = 8
IDX_DIM = 32
TOPK_MAX = 256
DIFF_HEADS = 4
DIFF_HEAD_DIM = 64
NUM_BUCKETS = 32
MAX_DISTANCE = 128
N_GROUPS = 4
EXPERTS_PER_GROUP = 8
N_EXPERTS = N_GROUPS * EXPERTS_PER_GROUP
C_WIDTH = C_HEADS * C_HEAD_DIM
DIFF_W = DIFF_HEADS * 2 * DIFF_HEAD_DIM
LANES = 128
BF16_ROWS = 16
INT_MIN = -(2 ** 31)
LOG2E = math.log2(math.e)
VMEM_LIMIT = 56 * 1024 * 1024

Z_QC, Z_KC, Z_QD, Z_KD = 0, 1, 2, 3
Z_QIDX_OFF = 4 * C_WIDTH
Z_QIDX_W = IDX_HEADS * IDX_DIM
Z_KIDX_OFF = Z_QIDX_OFF + Z_QIDX_W
Z_COLS = Z_KIDX_OFF + LANES
ZT_VC, ZT_VD = 0, 1
ZT_W_OFF = C_WIDTH + DIFF_W
ZT_ROWS = ZT_W_OFF + BF16_ROWS


def _bucket_starts():
    n = np.arange(0, 2 * MAX_DISTANCE)
    me = NUM_BUCKETS // 2
    lr = np.log(np.maximum(n, 1) / me) / math.log(MAX_DISTANCE / me)
    large = me + (lr * (NUM_BUCKETS - me)).astype(np.int64)
    b = np.where(n < me, n, np.minimum(large, NUM_BUCKETS - 1))
    return [int(n[b >= k].min()) for k in range(NUM_BUCKETS)]


BUCKET_START = _bucket_starts()


def _cparams(sem):
    return pltpu.CompilerParams(dimension_semantics=sem, vmem_limit_bytes=VMEM_LIMIT)


def _rms(x):
    return x * lax.rsqrt(jnp.mean(x * x, axis=-1, keepdims=True) + EPS)


def _sigmoid(x):
    return 1.0 / (1.0 + jnp.exp(-x))


def _dot_nt(a, b):
    return lax.dot_general(a, b, (((1,), (1,)), ((), ())), preferred_element_type=F32)


def _ada_body(c_ref, w_ref, b_ref, o_ref):
    c = c_ref[...]
    cond = c * _sigmoid(c)
    o_ref[0] = jnp.dot(cond, w_ref[0], precision=HIGHEST, preferred_element_type=F32) + b_ref[0]


def _ada_mod(c, ada_w, ada_b):
    depth, d, n6 = ada_w.shape
    bsz = c.shape[0]
    rows = 8
    c_pad = jnp.zeros((rows, d), F32).at[:bsz].set(c)
    tn = 1536
    out = pl.pallas_call(
        _ada_body,
        grid=(depth, n6 // tn),
        in_specs=[pl.BlockSpec((rows, d), lambda i, j: (0, 0)),
                  pl.BlockSpec((1, d, tn), lambda i, j: (i, 0, j)),
                  pl.BlockSpec((1, 1, tn), lambda i, j: (i, 0, j))],
        out_specs=pl.BlockSpec((1, rows, tn), lambda i, j: (i, 0, j)),
        out_shape=jax.ShapeDtypeStruct((depth, rows, n6), F32),
        compiler_params=_cparams(("arbitrary", "arbitrary")),
        name="ada_mod",
    )(c_pad, ada_w, ada_b.reshape(depth, 1, n6))
    return out[:, :bsz]


def _norm_proj_body(x_ref, g_ref, sh_ref, sc_ref, w_ref, *rest):
    y = _rms(x_ref[0]) * g_ref[...]
    h = (y * (1.0 + sc_ref[0]) + sh_ref[0]).astype(BF16)
    if len(rest) == 1:
        (o_ref,) = rest
    else:
        wt_ref, o_ref, ot_ref = rest
        ot_ref[0] = _dot_nt(wt_ref[...], h).astype(ot_ref.dtype)
    o_ref[0] = jnp.dot(h, w_ref[...], preferred_element_type=F32).astype(o_ref.dtype)


def _norm_proj(x, g, sh, sc, w_bf16, tm, wt_bf16=None):
    bsz, seq, d = x.shape
    n = w_bf16.shape[1]
    in_specs = [pl.BlockSpec((1, tm, d), lambda b, i: (b, i, 0)),
                pl.BlockSpec((1, d), lambda b, i: (0, 0)),
                pl.BlockSpec((1, 1, d), lambda b, i: (b, 0, 0)),
                pl.BlockSpec((1, 1, d), lambda b, i: (b, 0, 0)),
                pl.BlockSpec((d, n), lambda b, i: (0, 0))]
    out_specs = pl.BlockSpec((1, tm, n), lambda b, i: (b, i, 0))
    out_shape = jax.ShapeDtypeStruct((bsz, seq, n), BF16)
    args = [x, g.reshape(1, d), sh.reshape(bsz, 1, d), sc.reshape(bsz, 1, d), w_bf16]
    if wt_bf16 is not None:
        nt = wt_bf16.shape[0]
        in_specs.append(pl.BlockSpec((nt, d), lambda b, i: (0, 0)))
        out_specs = [out_specs, pl.BlockSpec((1, nt, tm), lambda b, i: (b, 0, i))]
        out_shape = [out_shape, jax.ShapeDtypeStruct((bsz, nt, seq), BF16)]
        args.append(wt_bf16)
    return pl.pallas_call(
        _norm_proj_body,
        grid=(bsz, seq // tm),
        in_specs=in_specs,
        out_specs=out_specs,
        out_shape=out_shape,
        compiler_params=_cparams(("arbitrary", "arbitrary")),
        name="norm_proj",
    )(*args)


CONV_HALO = 32
CONV_ROWS = 64


def _conv_body(z_ref, x_ref, g1_ref, ca_ref, cb_ref, cbb_ref, lng_ref, lnb_ref, wo_ref, o_ref,
               ua_scr, ub_scr, y_scr, *, tl):
    l = pl.program_id(1)

    @pl.when(l == 0)
    def _():
        ua_scr[0:CONV_HALO, :] = jnp.zeros((CONV_HALO, A_WIDTH), F32)
        ub_scr[0:CONV_HALO, :] = jnp.zeros((CONV_HALO, B_WIDTH), F32)

    a = A_WIDTH
    gate_c = z_ref[0, :, a:2 * a].astype(F32)
    x_a = z_ref[0, :, 2 * a:3 * a].astype(F32)
    ua_scr[CONV_HALO:CONV_HALO + tl, :] = gate_c * x_a
    val_b = z_ref[0, :, 3 * a:3 * a + B_WIDTH].astype(F32)
    glu = z_ref[0, :, 3 * a + B_WIDTH:3 * a + 2 * B_WIDTH].astype(F32)
    ub_scr[CONV_HALO:CONV_HALO + tl, :] = val_b * _sigmoid(glu)

    for r in range(0, tl, CONV_ROWS):
        acc_a = None
        for k in range(A_CONV):
            tap = ua_scr[CONV_HALO + r - (A_CONV - 1) + k:CONV_HALO + r - (A_CONV - 1) + k + CONV_ROWS, :]
            term = tap * ca_ref[k:k + 1, :]
            acc_a = term if acc_a is None else acc_a + term
        gate_b = z_ref[0, r:r + CONV_ROWS, 0:a].astype(F32)
        y_scr[r:r + CONV_ROWS, 0:a] = (gate_b * acc_a).astype(BF16)

        acc_b = None
        for k in range(B_CONV):
            tap = ub_scr[CONV_HALO + r - (B_CONV - 1) + k:CONV_HALO + r - (B_CONV - 1) + k + CONV_ROWS, :]
            term = tap * cb_ref[k:k + 1, :]
            acc_b = term if acc_b is None else acc_b + term
        u = acc_b + cbb_ref[...]
        mu = jnp.mean(u, axis=-1, keepdims=True)
        uc = u - mu
        var = jnp.mean(uc * uc, axis=-1, keepdims=True)
        v = uc * lax.rsqrt(var + EPS) * lng_ref[...] + lnb_ref[...]
        y_scr[r:r + CONV_ROWS, a:a + B_WIDTH] = (v * _sigmoid(v)).astype(BF16)

    ua_scr[0:CONV_HALO, :] = ua_scr[tl:tl + CONV_HALO, :]
    ub_scr[0:CONV_HALO, :] = ub_scr[tl:tl + CONV_HALO, :]
    y = jnp.dot(y_scr[...], wo_ref[...], preferred_element_type=F32)
    o_ref[0] = x_ref[0] + g1_ref[0] * y


def _conv_mix(z, x, g1, conv_a, conv_b, conv_b_bias, ln_g, ln_b, w_out_bf16, tl):
    bsz, seq, d = x.shape
    nz = z.shape[-1]
    wide = A_WIDTH + B_WIDTH
    full = lambda shape: pl.BlockSpec(shape, lambda b, l: (0,) * len(shape))
    return pl.pallas_call(
        functools.partial(_conv_body, tl=tl),
        grid=(bsz, seq // tl),
        in_specs=[pl.BlockSpec((1, tl, nz), lambda b, l: (b, l, 0)),
                  pl.BlockSpec((1, tl, d), lambda b, l: (b, l, 0)),
                  pl.BlockSpec((1, 1, d), lambda b, l: (b, 0, 0)),
                  full((A_CONV, A_WIDTH)), full((B_CONV, B_WIDTH)), full((1, B_WIDTH)),
                  full((1, B_WIDTH)), full((1, B_WIDTH)), full((wide, d))],
        out_specs=pl.BlockSpec((1, tl, d), lambda b, l: (b, l, 0)),
        out_shape=jax.ShapeDtypeStruct((bsz, seq, d), F32),
        scratch_shapes=[pltpu.VMEM((CONV_HALO + tl, A_WIDTH), F32),
                        pltpu.VMEM((CONV_HALO + tl, B_WIDTH), F32),
                        pltpu.VMEM((tl, wide), BF16)],
        compiler_params=_cparams(("arbitrary", "arbitrary")),
        name="conv_mix",
    )(z, x, g1.reshape(bsz, 1, d), conv_a, conv_b, conv_b_bias.reshape(1, -1), ln_g.reshape(1, -1),
      ln_b.reshape(1, -1), w_out_bf16)


MOE_ROWS = 256
ROUTER_ROWS = 512
META_GATE0, META_GATE1, META_POS0, META_POS1 = range(4)
SEG_LEN, SEG_BASE, SEG_OFF = range(3)
SEG_ALIGN = 8
SEG_SIZES = tuple(2 ** b for b in range(10, 2, -1))
GROUP_LANE0 = N_EXPERTS


def _router_body(x_ref, g_ref, sh_ref, sc_ref, wr_ref, br_ref, tri_ref, upper_ref, h_ref, meta_ref, post_ref, seg_ref,
                 cnt_ref, base_scr):
    @pl.when(pl.program_id(0) == 0)
    def _():
        base_scr[...] = jnp.zeros_like(base_scr)

    h = _rms(x_ref[...]) * g_ref[...]
    h = h * (1.0 + sc_ref[0]) + sh_ref[0]
    h_ref[...] = h.astype(h_ref.dtype)
    logits = jnp.dot(h, wr_ref[...], precision=HIGHEST, preferred_element_type=F32) + br_ref[...]
    tr = logits.shape[0]
    lane = lax.broadcasted_iota(I32, (tr, LANES), 1)
    lane_f = lane.astype(F32)
    neg = jnp.float32(-jnp.inf)
    big = jnp.float32(1e9)

    is_group = (lane >= GROUP_LANE0) & (lane < GROUP_LANE0 + N_GROUPS)
    glog = jnp.where(is_group, logits, neg)
    gmax = jnp.max(glog, axis=1, keepdims=True)
    p_top = 1.0 / jnp.sum(jnp.exp(glog - gmax), axis=1, keepdims=True)
    g_sel = jnp.min(jnp.where(glog == gmax, lane_f, big), axis=1, keepdims=True) - GROUP_LANE0
    lo = g_sel * EXPERTS_PER_GROUP
    in_group = (lane_f >= lo) & (lane_f < lo + EXPERTS_PER_GROUP)
    f1 = jnp.where(in_group, logits, neg)
    v1 = jnp.max(f1, axis=1, keepdims=True)
    i1 = jnp.min(jnp.where(f1 == v1, lane_f, big), axis=1, keepdims=True)
    f2 = jnp.where(lane_f == i1, neg, f1)
    v2 = jnp.max(f2, axis=1, keepdims=True)
    i2 = jnp.min(jnp.where(f2 == v2, lane_f, big), axis=1, keepdims=True)
    a = jnp.exp(v2 - v1)
    w1 = 1.0 / (1.0 + a)
    gate0 = p_top * w1
    gate1 = p_top * (a * w1)

    oh0 = lane_f == i1
    oh1 = lane_f == i2
    ind0 = jnp.where(oh0, 1.0, 0.0)
    ind1 = jnp.where(oh1, 1.0, 0.0)
    pre0 = jnp.dot(tri_ref[...], ind0.astype(BF16), preferred_element_type=F32)
    pre1 = jnp.dot(tri_ref[...], ind1.astype(BF16), preferred_element_type=F32)
    tot0 = jnp.sum(ind0, axis=0, keepdims=True)
    tot1 = jnp.sum(ind1, axis=0, keepdims=True)
    seg_len = jnp.floor((tot0 + tot1 + (SEG_ALIGN - 1)) * (1.0 / SEG_ALIGN)) * SEG_ALIGN
    seg_off = jnp.dot(jnp.broadcast_to(seg_len, (8, LANES)), upper_ref[...], precision=HIGHEST,
                      preferred_element_type=F32)[0:1]
    pos0 = jnp.sum(jnp.where(oh0, seg_off + pre0, 0.0), axis=1, keepdims=True)
    pos1 = jnp.sum(jnp.where(oh1, seg_off + tot0 + pre1, 0.0), axis=1, keepdims=True)
    base = base_scr[...]
    new_base = base + seg_len
    base_scr[...] = new_base
    cnt_ref[...] = new_base

    meta = jnp.zeros((tr, LANES), F32)
    for col, val in ((META_GATE0, gate0), (META_GATE1, gate1), (META_POS0, pos0), (META_POS1, pos1)):
        meta = jnp.where(lane == col, val, meta)
    meta_ref[...] = meta
    post_ref[0] = meta.T[0:8].astype(I32)
    srow = lax.broadcasted_iota(I32, (8, LANES), 0)
    seg = jnp.where(srow == SEG_LEN, seg_len, jnp.where(srow == SEG_BASE, base, jnp.where(srow == SEG_OFF, seg_off, 0.0)))
    seg_ref[0] = seg.astype(I32)


def _moe_router(x2, g, sh, sc, wr_g, br_g, wr_e, br_e, seq):
    n_tok, d = x2.shape
    bsz = n_tok // seq
    tr = min(ROUTER_ROWS, seq)
    steps_per_batch = seq // tr
    wr = jnp.zeros((d, LANES), F32).at[:, :N_EXPERTS].set(wr_e).at[:, GROUP_LANE0:GROUP_LANE0 + N_GROUPS].set(wr_g)
    br = jnp.zeros((1, LANES), F32).at[0, :N_EXPERTS].set(br_e).at[0, GROUP_LANE0:GROUP_LANE0 + N_GROUPS].set(br_g)
    tri = jnp.tril(jnp.ones((tr, tr), BF16), -1)
    upper = jnp.triu(jnp.ones((LANES, LANES), F32), 1)
    nb = n_tok // tr
    full = lambda shape: pl.BlockSpec(shape, lambda i: (0,) * len(shape))
    return pl.pallas_call(
        _router_body,
        grid=(nb,),
        in_specs=[pl.BlockSpec((tr, d), lambda i: (i, 0)),
                  full((1, d)),
                  pl.BlockSpec((1, 1, d), lambda i: (i // steps_per_batch, 0, 0)),
                  pl.BlockSpec((1, 1, d), lambda i: (i // steps_per_batch, 0, 0)),
                  full((d, LANES)), full((1, LANES)), full((tr, tr)), full((LANES, LANES))],
        out_specs=[pl.BlockSpec((tr, d), lambda i: (i, 0)),
                   pl.BlockSpec((tr, LANES), lambda i: (i, 0)),
                   pl.BlockSpec((1, 8, tr), lambda i: (i, 0, 0)),
                   pl.BlockSpec((1, 8, LANES), lambda i: (i, 0, 0)),
                   full((1, LANES))],
        out_shape=[jax.ShapeDtypeStruct((n_tok, d), BF16),
                   jax.ShapeDtypeStruct((n_tok, LANES), F32),
                   jax.ShapeDtypeStruct((nb, 8, tr), I32),
                   jax.ShapeDtypeStruct((nb, 8, LANES), I32),
                   jax.ShapeDtypeStruct((1, LANES), F32)],
        scratch_shapes=[pltpu.VMEM((1, LANES), F32)],
        compiler_params=_cparams(("arbitrary",)),
        name="moe_router",
    )(x2, g.reshape(1, d), sh.reshape(bsz, 1, d), sc.reshape(bsz, 1, d), wr, br, tri, upper)


def _segment_copies(seg_ref, starts_ref, make_copy, action):
    def per_expert(e, carry):
        length = seg_ref[0, SEG_LEN, e]
        local = seg_ref[0, SEG_OFF, e]
        glob = starts_ref[e] + seg_ref[0, SEG_BASE, e]
        for size in SEG_SIZES:
            hit = (length & size) != 0

            @pl.when(hit)
            def _(local=local, glob=glob, size=size):
                cp = make_copy(pl.multiple_of(local, SEG_ALIGN), pl.multiple_of(glob, SEG_ALIGN), size)
                cp.start() if action == "start" else cp.wait()

            step = jnp.where(hit, size, 0)
            local = local + step
            glob = glob + step
        return carry

    lax.fori_loop(0, N_EXPERTS, per_expert, 0)


def _local_rows(tr):
    return 2 * tr + N_EXPERTS * SEG_ALIGN


def _dispatch_body(starts_ref, seg_ref, h_ref, post_ref, xr_ref, xs_scr, zero_scr, sem, *, tr):
    i = pl.program_id(0)
    lb = xs_scr.shape[0]
    rows = lax.broadcasted_iota(I32, (lb, tr), 0)
    place = (rows == post_ref[0, META_POS0:META_POS0 + 1, :]) | (rows == post_ref[0, META_POS1:META_POS1 + 1, :])
    xs_scr[...] = jnp.dot(jnp.where(place, 1.0, 0.0).astype(BF16), h_ref[...], preferred_element_type=F32)

    def make_copy(local, glob, size):
        return pltpu.make_async_copy(xs_scr.at[pl.ds(local, size)], xr_ref.at[pl.ds(glob, size)], sem)

    _segment_copies(seg_ref, starts_ref, make_copy, "start")
    _segment_copies(seg_ref, starts_ref, make_copy, "wait")

    @pl.when(i == pl.num_programs(0) - 1)
    def _():
        zero_scr[...] = jnp.zeros_like(zero_scr)
        total = starts_ref[N_EXPERTS]
        tail = (-total) & (MOE_ROWS - 1)
        for action in ("start", "wait"):
            row = total
            for size in SEG_SIZES:
                if size >= MOE_ROWS:
                    continue
                hit = (tail & size) != 0

                @pl.when(hit)
                def _(row=row, size=size, action=action):
                    cp = pltpu.make_async_copy(zero_scr.at[pl.ds(0, size)],
                                               xr_ref.at[pl.ds(pl.multiple_of(row, SEG_ALIGN), size)], sem)
                    cp.start() if action == "start" else cp.wait()

                row = row + jnp.where(hit, size, 0)

        def free_block(b):
            return pltpu.make_async_copy(
                zero_scr, xr_ref.at[pl.ds(pl.multiple_of(b * MOE_ROWS, MOE_ROWS), MOE_ROWS)], sem)

        first_free = (total + MOE_ROWS - 1) // MOE_ROWS
        n_blocks = xr_ref.shape[0] // MOE_ROWS
        lax.fori_loop(first_free, n_blocks, lambda b, c: (free_block(b).start(), c)[1], 0)
        lax.fori_loop(first_free, n_blocks, lambda b, c: (free_block(b).wait(), c)[1], 0)


def _moe_dispatch(h2, post, seg, starts, n_rows, tr):
    n_tok, d = h2.shape
    grid_spec = pltpu.PrefetchScalarGridSpec(
        num_scalar_prefetch=1,
        grid=(n_tok // tr,),
        in_specs=[pl.BlockSpec((1, 8, LANES), lambda i, s: (i, 0, 0), memory_space=pltpu.SMEM),
                  pl.BlockSpec((tr, d), lambda i, s: (i, 0)),
                  pl.BlockSpec((1, 8, tr), lambda i, s: (i, 0, 0))],
        out_specs=pl.BlockSpec(memory_space=pl.ANY),
        scratch_shapes=[pltpu.VMEM((_local_rows(tr), d), F32), pltpu.VMEM((MOE_ROWS, d), F32),
                        pltpu.SemaphoreType.DMA(())],
    )
    return pl.pallas_call(
        functools.partial(_dispatch_body, tr=tr),
        grid_spec=grid_spec,
        out_shape=jax.ShapeDtypeStruct((n_rows, d), F32),
        compiler_params=_cparams(("arbitrary",)),
        name="moe_dispatch",
    )(starts, seg, h2, post)


def _expert_body(pb_ref, pe_ref, plo_ref, phi_ref, x_ref, wg_ref, wu_ref, wd_ref, o_ref, wg_s, wu_s, wd_s):
    p = pl.program_id(0)
    prev = jnp.maximum(p - 1, 0)
    new_expert = (p == 0) | (pe_ref[p] != pe_ref[prev])
    first = (p == 0) | (pb_ref[p] != pb_ref[prev])

    @pl.when(new_expert)
    def _():
        wg_s[...] = wg_ref[0, 0].astype(BF16)
        wu_s[...] = wu_ref[0, 0].astype(BF16)
        wd_s[...] = wd_ref[0, 0].astype(BF16)

    def rows_of_expert():
        x = x_ref[...].astype(BF16)
        gt = jnp.dot(x, wg_s[...], preferred_element_type=F32)
        up = jnp.dot(x, wu_s[...], preferred_element_type=F32)
        act = (gt * _sigmoid(gt)) * up
        y = jnp.dot(act.astype(BF16), wd_s[...], preferred_element_type=F32)
        rows = lax.broadcasted_iota(I32, (y.shape[0], 1), 0)
        return jnp.where((rows >= plo_ref[p]) & (rows < phi_ref[p]), y, 0.0)

    nonempty = phi_ref[p] > plo_ref[p]

    @pl.when(first & nonempty)
    def _():
        o_ref[...] = rows_of_expert()

    @pl.when(first & jnp.logical_not(nonempty))
    def _():
        o_ref[...] = jnp.zeros_like(o_ref)

    @pl.when(jnp.logical_not(first) & nonempty)
    def _():
        o_ref[...] += rows_of_expert()


def _moe_experts(x_rows, pairs, w_gate, w_up, w_down, layer):
    n_rows, d = x_rows.shape
    de = w_gate.shape[-1]
    n_pairs = pairs[0].shape[0]
    grid_spec = pltpu.PrefetchScalarGridSpec(
        num_scalar_prefetch=4,
        grid=(n_pairs,),
        in_specs=[pl.BlockSpec((MOE_ROWS, d), lambda p, pb, pe, lo, hi: (pb[p], 0)),
                  pl.BlockSpec((1, 1, d, de), lambda p, pb, pe, lo, hi: (layer, pe[p], 0, 0)),
                  pl.BlockSpec((1, 1, d, de), lambda p, pb, pe, lo, hi: (layer, pe[p], 0, 0)),
                  pl.BlockSpec((1, 1, de, d), lambda p, pb, pe, lo, hi: (layer, pe[p], 0, 0))],
        out_specs=pl.BlockSpec((MOE_ROWS, d), lambda p, pb, pe, lo, hi: (pb[p], 0)),
        scratch_shapes=[pltpu.VMEM((d, de), BF16), pltpu.VMEM((d, de), BF16), pltpu.VMEM((de, d), BF16)],
    )
    return pl.pallas_call(
        _expert_body,
        grid_spec=grid_spec,
        out_shape=jax.ShapeDtypeStruct((n_rows, d), F32),
        compiler_params=_cparams(("arbitrary",)),
        name="moe_experts",
    )(*pairs, x_rows, w_gate, w_up, w_down)


def _combine_body(starts_ref, seg_ref, y_ref, x_ref, meta_ref, g2_ref, fg_ref, o_ref, ys_scr, sem, *, final_norm):
    @pl.when(pl.program_id(0) == 0)
    def _():
        ys_scr[...] = jnp.zeros_like(ys_scr)

    def make_copy(local, glob, size):
        return pltpu.make_async_copy(y_ref.at[pl.ds(glob, size)], ys_scr.at[pl.ds(local, size)], sem)

    _segment_copies(seg_ref, starts_ref, make_copy, "start")
    _segment_copies(seg_ref, starts_ref, make_copy, "wait")

    meta = meta_ref[...]
    tr = meta.shape[0]
    y = ys_scr[...].astype(BF16)
    cols = lax.broadcasted_iota(I32, (tr, ys_scr.shape[0]), 1)

    def picked(col):
        pos = meta[:, col:col + 1].astype(I32)
        return jnp.dot(jnp.where(cols == pos, 1.0, 0.0).astype(BF16), y, preferred_element_type=F32)

    moe = meta[:, META_GATE0:META_GATE0 + 1] * picked(META_POS0) + meta[:, META_GATE1:META_GATE1 + 1] * picked(META_POS1)
    xn = x_ref[...] + g2_ref[0] * moe
    if final_norm:
        xn = _rms(xn) * fg_ref[...]
    o_ref[...] = xn


def _moe_combine(y_rows, seg, starts, x2, meta, g2, final_g, seq, tr, final_norm):
    n_tok, d = x2.shape
    bsz = n_tok // seq
    steps_per_batch = seq // tr
    grid_spec = pltpu.PrefetchScalarGridSpec(
        num_scalar_prefetch=1,
        grid=(n_tok // tr,),
        in_specs=[pl.BlockSpec((1, 8, LANES), lambda i, s: (i, 0, 0), memory_space=pltpu.SMEM),
                  pl.BlockSpec(memory_space=pl.ANY),
                  pl.BlockSpec((tr, d), lambda i, s: (i, 0)),
                  pl.BlockSpec((tr, LANES), lambda i, s: (i, 0)),
                  pl.BlockSpec((1, 1, d), lambda i, s: (i // steps_per_batch, 0, 0)),
                  pl.BlockSpec((1, d), lambda i, s: (0, 0))],
        out_specs=pl.BlockSpec((tr, d), lambda i, s: (i, 0)),
        scratch_shapes=[pltpu.VMEM((_local_rows(tr), d), F32), pltpu.SemaphoreType.DMA(())],
    )
    return pl.pallas_call(
        functools.partial(_combine_body, final_norm=final_norm),
        grid_spec=grid_spec,
        out_shape=jax.ShapeDtypeStruct((n_tok, d), F32),
        compiler_params=_cparams(("arbitrary",)),
        name="moe_combine",
    )(starts, seg, y_rows, x2, meta, g2.reshape(bsz, 1, d), final_g.reshape(1, d))


def _expert_pairs(counts, n_rows):
    n_blocks = n_rows // MOE_ROWS
    n_pairs = n_blocks + N_EXPERTS
    ends = jnp.cumsum(counts)
    starts = ends - counts
    first_blk = starts // MOE_ROWS
    last_blk = (ends - 1) // MOE_ROWS
    npairs = jnp.where(counts > 0, last_blk - first_blk + 1, 0)
    pend = jnp.cumsum(npairs)
    poff = pend - npairs
    total = pend[-1]
    used_blocks = (ends[-1] + MOE_ROWS - 1) // MOE_ROWS
    p = jnp.arange(n_pairs, dtype=I32)
    p_eff = jnp.minimum(p, total - 1)
    e = jnp.minimum(jnp.sum(pend[None, :] <= p_eff[:, None], axis=1), N_EXPERTS - 1).astype(I32)
    valid = p < total
    blk = jnp.where(valid, first_blk[e] + p_eff - poff[e], jnp.minimum(used_blocks + p - total, n_blocks - 1)).astype(I32)
    lo = jnp.where(valid, jnp.clip(starts[e] - blk * MOE_ROWS, 0, MOE_ROWS), 0).astype(I32)
    hi = jnp.where(valid, jnp.clip(ends[e] - blk * MOE_ROWS, 0, MOE_ROWS), 0).astype(I32)
    return blk, e, lo, hi


def _hier_moe(x, g, sh, sc, g2, wr_g, br_g, wr_e, br_e, w_gate, w_up, w_down, layer, final_g, final_norm):
    bsz, seq, d = x.shape
    n_tok = bsz * seq
    tr = min(ROUTER_ROWS, seq)
    assert 2 * tr <= SEG_SIZES[0]
    x2 = x.reshape(n_tok, d)
    h2, meta, post, seg, cnt = _moe_router(x2, g, sh, sc, wr_g, br_g, wr_e, br_e, seq)
    counts = cnt[0, :N_EXPERTS].astype(I32)
    ends = jnp.cumsum(counts)
    starts = jnp.concatenate([ends - counts, ends[-1:]])
    n_rows = -(-(2 * n_tok + (n_tok // tr) * N_EXPERTS * (SEG_ALIGN - 1)) // MOE_ROWS) * MOE_ROWS
    x_rows = _moe_dispatch(h2, post, seg, starts, n_rows, tr)
    y_rows = _moe_experts(x_rows, _expert_pairs(counts, n_rows), w_gate, w_up, w_down, layer)
    out = _moe_combine(y_rows, seg, starts, x2, meta, g2, final_g, seq, tr, final_norm)
    return out.reshape(bsz, seq, d)


SEL_COLS = 256
SEL_CHUNK = 512
SEL_FOLD = 32
BIT_GROUP = 256
SEL_SWEEP = 64
ATT_TILE = 512


def _sortable(score):
    bits = lax.bitcast_convert_type(score, I32)
    return bits ^ ((bits >> 31) & jnp.int32(0x7FFFFFFF))


def _idx_queries(zq):
    zf = zq.astype(F32)
    return [zf[:, h * IDX_DIM:(h + 1) * IDX_DIM].astype(BF16) for h in range(IDX_HEADS)]


def _idx_score_t(k, qs, wt):
    acc = None
    for h, q in enumerate(qs):
        term = jnp.maximum(_dot_nt(k, q), 0.0) * wt[h:h + 1, :]
        acc = term if acc is None else acc + term
    return acc


def _bit_transpose32(a):
    a = list(a)
    m, j = 0x0000FFFF, 16
    while j:
        k = 0
        while k < 32:
            t = (a[k] ^ lax.shift_right_logical(a[k + j], jnp.int32(j))) & jnp.int32(m)
            a[k] = a[k] ^ t
            a[k + j] = a[k + j] ^ (t << j)
            k = (k + j + 1) & ~j
        j >>= 1
        m = (m ^ (m << j)) & 0xFFFFFFFF
    return a


def _select_body(zq_ref, zk_ref, wt_ref, o_ref, keys_scr, planes_scr, eq_scr, gt_scr, *, topk):
    i = pl.program_id(1)
    tq = zq_ref.shape[1]
    qs = _idx_queries(zq_ref[0])
    wt = wt_ref[0].astype(F32)
    n_chunks = ((i + 1) * tq + SEL_CHUNK - 1) // SEL_CHUNK
    krow0 = lax.broadcasted_iota(I32, (SEL_CHUNK, tq), 0)
    qcol = i * tq + lax.broadcasted_iota(I32, (SEL_CHUNK, tq), 1)
    int_min = jnp.int32(INT_MIN)

    groups = SEL_CHUNK // BIT_GROUP
    words = SEL_CHUNK // 32
    sweep_rows = min(SEL_SWEEP, eq_scr.shape[0])

    def fill(c, carry):
        off = pl.multiple_of(c * SEL_CHUNK, SEL_CHUNK)
        k = zk_ref[0, pl.ds(off, SEL_CHUNK), :][:, 0:IDX_DIM]
        key = jnp.where(krow0 + off <= qcol, _sortable(_idx_score_t(k, qs, wt)), int_min)
        keys_scr[pl.ds(off, SEL_CHUNK), :] = key
        for g in range(groups):
            for lt in range(0, tq, LANES):
                ku = key[g * BIT_GROUP:(g + 1) * BIT_GROUP, lt:lt + LANES] ^ int_min
                planes = _bit_transpose32([ku[8 * r:8 * r + 8] for r in range(32)])
                wrow = pl.multiple_of(c * words + g * 8, 8)
                for b in range(32):
                    planes_scr[b, pl.ds(wrow, 8), lt:lt + LANES] = planes[b]
        return carry

    lax.fori_loop(0, n_chunks, fill, 0)

    n_sweep = (n_chunks * words + sweep_rows - 1) // sweep_rows

    def pad(c, carry):
        wrow = pl.multiple_of(c * words, words)
        for b in range(32):
            planes_scr[b, pl.ds(wrow, words), :] = jnp.zeros((words, tq), I32)
        return carry

    lax.fori_loop(n_chunks, n_sweep * (sweep_rows // words), pad, 0)

    def sweep(upd, cnt_plane):
        def body(sb, acc):
            r0 = pl.multiple_of(sb * sweep_rows, sweep_rows)
            eq = eq_scr[pl.ds(r0, sweep_rows), :]
            gt = gt_scr[pl.ds(r0, sweep_rows), :]
            if upd is not None:
                plane, accept = upd
                hit = eq & planes_scr[plane, pl.ds(r0, sweep_rows), :]
                gt = jnp.where(accept, gt, gt | hit)
                eq = jnp.where(accept, hit, eq ^ hit)
                eq_scr[pl.ds(r0, sweep_rows), :] = eq
                gt_scr[pl.ds(r0, sweep_rows), :] = gt
            if cnt_plane is None:
                return acc
            ones = lax.population_count(gt | (eq & planes_scr[cnt_plane, pl.ds(r0, sweep_rows), :]))
            return acc + jnp.sum(ones.reshape(sweep_rows // 8, 8, tq), axis=0)
        acc = lax.fori_loop(0, n_sweep, body, jnp.zeros((8, tq), I32))
        return jnp.sum(acc.astype(F32), axis=0, keepdims=True)

    k_f = jnp.float32(topk)
    eq_scr[...] = jnp.full(eq_scr.shape, -1, I32)
    gt_scr[...] = jnp.zeros(gt_scr.shape, I32)

    def bit_step(ib, carry):
        u, cnt = carry
        accept = cnt >= k_f
        u = jnp.where(accept, u | jnp.left_shift(jnp.int32(1), 32 - ib), u)
        return u, sweep((ib - 1, accept), ib)

    u, cnt = lax.fori_loop(1, 32, bit_step, (jnp.zeros((1, tq), I32), sweep(None, 0)))
    accept = cnt >= k_f
    u = jnp.where(accept, u | 1, u)
    sweep((31, accept), None)
    some = jnp.where(u != 0, -1, 0)

    def emit(c, carry):
        off = pl.multiple_of(c * SEL_CHUNK, SEL_CHUNK)
        for g in range(groups):
            wrow = pl.multiple_of(c * words + g * 8, 8)
            keep = gt_scr[pl.ds(wrow, 8), :] | (eq_scr[pl.ds(wrow, 8), :] & some)
            rows = [jnp.where((lax.shift_right_logical(keep, 31 - r) & 1) != 0, 0.0, -jnp.inf) for r in range(32)]
            o_ref[0, pl.ds(pl.multiple_of(off + g * BIT_GROUP, BIT_GROUP), BIT_GROUP), :] = (
                jnp.concatenate(rows, axis=0).astype(o_ref.dtype))
        return carry

    lax.fori_loop(0, n_chunks, emit, 0)

    def popcount_rows(ref):
        def body(sb, acc):
            r0 = pl.multiple_of(sb * sweep_rows, sweep_rows)
            ones = lax.population_count(ref[pl.ds(r0, sweep_rows), :])
            return acc + jnp.sum(ones.reshape(sweep_rows // 8, 8, tq), axis=0)
        acc = lax.fori_loop(0, n_sweep, body, jnp.zeros((8, tq), I32))
        return jnp.sum(acc.astype(F32), axis=0, keepdims=True)

    n_gt = popcount_rows(gt_scr)
    n_eq = jnp.where(u != 0, popcount_rows(eq_scr), 0.0)

    @pl.when(jnp.max(n_gt + n_eq) > k_f)
    def _():
        thr = u ^ int_min

        def count(pred):
            def body(c, acc):
                off = pl.multiple_of(c * SEL_CHUNK, SEL_CHUNK)
                ind = jnp.where(pred(keys_scr[pl.ds(off, SEL_CHUNK), :], krow0 + off), 1, 0)
                return acc + jnp.sum(ind.reshape(SEL_CHUNK // SEL_FOLD, SEL_FOLD, tq), axis=0)
            acc = lax.fori_loop(0, n_chunks, body, jnp.zeros((SEL_FOLD, tq), I32))
            return jnp.sum(acc.astype(F32), axis=0, keepdims=True)

        need = k_f - n_gt

        def cap_step(ib, v):
            cand = v | jnp.left_shift(jnp.int32(1), 14 - ib)
            cnt = count(lambda keys, kidx: (keys == thr) & (kidx < cand))
            return jnp.where(cnt <= need, cand, v)

        cap = lax.fori_loop(0, 15, cap_step, jnp.zeros((1, tq), I32))
        cap = jnp.where((u != 0) & (n_gt + n_eq > k_f), cap, 2 ** 30)

        def drop(c, carry):
            off = pl.multiple_of(c * SEL_CHUNK, SEL_CHUNK)
            past = (keys_scr[pl.ds(off, SEL_CHUNK), :] == thr) & (krow0 + off >= cap)
            cur = o_ref[0, pl.ds(off, SEL_CHUNK), :].astype(F32)
            o_ref[0, pl.ds(off, SEL_CHUNK), :] = jnp.where(past, -jnp.inf, cur).astype(o_ref.dtype)
            return carry

        lax.fori_loop(0, n_chunks, drop, 0)

    def blank(c, carry):
        off = pl.multiple_of(c * SEL_CHUNK, SEL_CHUNK)
        o_ref[0, pl.ds(off, SEL_CHUNK), :] = jnp.full((SEL_CHUNK, tq), -jnp.inf, o_ref.dtype)
        return carry

    lax.fori_loop(n_chunks, o_ref.shape[1] // SEL_CHUNK, blank, 0)


def _select(z, zt, topk):
    bsz, seq, _ = z.shape
    tq = min(SEL_COLS, seq)
    assert seq % SEL_CHUNK == 0 and seq % tq == 0
    return pl.pallas_call(
        functools.partial(_select_body, topk=topk),
        grid=(bsz, seq // tq),
        in_specs=[pl.BlockSpec((1, tq, Z_QIDX_W), lambda b, i: (b, i, Z_QIDX_OFF // Z_QIDX_W)),
                  pl.BlockSpec((1, seq, LANES), lambda b, i: (b, 0, Z_KIDX_OFF // LANES)),
                  pl.BlockSpec((1, BF16_ROWS, tq), lambda b, i: (b, ZT_W_OFF // BF16_ROWS, i))],
        out_specs=pl.BlockSpec((1, seq, tq), lambda b, i: (b, 0, i)),
        out_shape=jax.ShapeDtypeStruct((bsz, seq, seq), BF16),
        scratch_shapes=[pltpu.VMEM((seq, tq), I32), pltpu.VMEM((32, seq // 32, tq), I32),
                        pltpu.VMEM((seq // 32, tq), I32), pltpu.VMEM((seq // 32, tq), I32)],
        compiler_params=_cparams(("arbitrary", "arbitrary")),
        name="dsa_select",
    )(z, z, zt)


BIAS_INIT_ROWS = 8
ONES_ROWS = BF16_ROWS


def _init_bias_tiles(btile, bias_ref, head0, n_heads, ta):
    col = lax.broadcasted_iota(I32, (BIAS_INIT_ROWS, ta), 1)
    row0 = lax.broadcasted_iota(I32, (BIAS_INIT_ROWS, ta), 0)

    def body(r, carry):
        off = pl.multiple_of(r * BIAS_INIT_ROWS, BIAS_INIT_ROWS)
        for kind in range(2):
            dist = col - (row0 + off) + kind * ta
            for h in range(n_heads):
                far = bias_ref[NUM_BUCKETS - 1, head0 + h]
                val = jnp.full((BIAS_INIT_ROWS, ta), (bias_ref[0, head0 + h] - far) * LOG2E, F32)
                for b in range(1, NUM_BUCKETS - 1):
                    val = jnp.where(dist >= BUCKET_START[b], (bias_ref[b, head0 + h] - far) * LOG2E, val)
                val = jnp.where(dist >= BUCKET_START[NUM_BUCKETS - 1], 0.0, val)
                btile[h, kind, pl.ds(off, BIAS_INIT_ROWS), :] = val
        return carry

    lax.fori_loop(0, ta // BIAS_INIT_ROWS, body, 0)


def _with_ones(vt):
    return jnp.concatenate([vt, jnp.ones((ONES_ROWS, vt.shape[1]), vt.dtype)], axis=0)


def _softmax_step_t(logits, v_aug, m_ref, acc_ref, idx):
    m_old = m_ref[idx]
    m_new = jnp.maximum(m_old, jnp.max(logits, axis=0, keepdims=True))
    m_safe = jnp.where(m_new == -jnp.inf, 0.0, m_new)
    p = jnp.exp2(logits - m_safe)
    alpha = jnp.exp2(m_old - m_safe)
    acc_ref[idx] = alpha * acc_ref[idx] + jnp.dot(v_aug, p.astype(BF16), preferred_element_type=F32)
    m_ref[idx] = m_new


def _reset_softmax(m_s, acc_s):
    m_s[...] = jnp.full(m_s.shape, -jnp.inf, F32)
    acc_s[...] = jnp.zeros(acc_s.shape, F32)


def _cattn_body(qi_ref, ki_ref, bias_ref, zq_ref, zk_ref, vt_ref, mask_ref, o_ref, m_s, acc_s, btile):
    b = pl.program_id(0)
    p = pl.program_id(1)
    qi = qi_ref[p]
    ki = ki_ref[p]
    ta = zq_ref.shape[1]
    hd = C_HEAD_DIM

    @pl.when((b == 0) & (p == 0))
    def _():
        _init_bias_tiles(btile, bias_ref, 0, C_HEADS, ta)

    @pl.when(ki == 0)
    def _():
        _reset_softmax(m_s, acc_s)

    def heads(extra_of_head):
        for h in range(C_HEADS):
            lo = h * hd
            logits = _dot_nt(zk_ref[0, :, lo:lo + hd], zq_ref[0, :, lo:lo + hd]) + extra_of_head(h)
            _softmax_step_t(logits, _with_ones(vt_ref[0, lo:lo + hd, :]), m_s, acc_s, h)

    near = ki >= qi - 1

    @pl.when(near)
    def _():
        heads(lambda h: mask_ref[0].astype(F32) + btile[h, qi - ki])

    @pl.when(jnp.logical_not(near))
    def _():
        heads(lambda h: mask_ref[0].astype(F32))

    @pl.when(ki == qi)
    def _():
        outs = []
        for h in range(C_HEADS):
            a = acc_s[h]
            outs.append((a[0:hd] / a[hd:hd + 1]).T)
        o_ref[0] = jnp.concatenate(outs, axis=1).astype(o_ref.dtype)


def _dattn_body(qi_ref, ki_ref, bias_ref, zq_ref, zk_ref, vt_ref, lam_ref, ng_ref, o_ref,
                m_s, acc_s, btile, *, lambda_init):
    b = pl.program_id(0)
    p = pl.program_id(1)
    qi = qi_ref[p]
    ki = ki_ref[p]
    ta = zq_ref.shape[1]
    hd = DIFF_HEAD_DIM
    dv = 2 * hd

    @pl.when((b == 0) & (p == 0))
    def _():
        _init_bias_tiles(btile, bias_ref, C_HEADS, DIFF_HEADS, ta)

    @pl.when(ki == 0)
    def _():
        _reset_softmax(m_s, acc_s)

    def heads(extra_of_head):
        for h in range(DIFF_HEADS):
            v_aug = _with_ones(vt_ref[0, dv * h:dv * (h + 1), :])
            extra = extra_of_head(h)
            for j in range(2):
                lo = (2 * h + j) * hd
                logits = _dot_nt(zk_ref[0, :, lo:lo + hd], zq_ref[0, :, lo:lo + hd])
                if extra is not None:
                    logits = logits + extra
                _softmax_step_t(logits, v_aug, m_s, acc_s, 2 * h + j)

    near = ki >= qi - 1

    @pl.when(near)
    def _():
        krow = ki * ta + lax.broadcasted_iota(I32, (ta, ta), 0)
        qcol = qi * ta + lax.broadcasted_iota(I32, (ta, ta), 1)
        heads(lambda h: jnp.where(krow <= qcol, btile[h, qi - ki], -jnp.inf))

    @pl.when(jnp.logical_not(near))
    def _():
        heads(lambda h: None)

    @pl.when(ki == qi)
    def _():
        lam_p = lam_ref[...]
        lam = (jnp.exp(jnp.sum(lam_p[0:1] * lam_p[1:2], axis=1, keepdims=True))
               - jnp.exp(jnp.sum(lam_p[2:3] * lam_p[3:4], axis=1, keepdims=True)) + lambda_init)
        outs = []
        for h in range(DIFF_HEADS):
            a1 = acc_s[2 * h]
            a2 = acc_s[2 * h + 1]
            o = a1[0:dv] / a1[dv:dv + 1] - lam * (a2[0:dv] / a2[dv:dv + 1])
            o = o * lax.rsqrt(jnp.mean(o * o, axis=0, keepdims=True) + EPS) * ng_ref[...] * (1.0 - lambda_init)
            outs.append(o.T)
        o_ref[0] = jnp.concatenate(outs, axis=1).astype(o_ref.dtype)


def _causal_pairs(nq):
    qi = [q for q in range(nq) for _ in range(q + 1)]
    ki = [k for q in range(nq) for k in range(q + 1)]
    return jnp.asarray(qi, I32), jnp.asarray(ki, I32)


def _cattn(z, zt, mask, rel_bias, ta):
    bsz, seq, _ = z.shape
    qi, ki = _causal_pairs(seq // ta)
    qmap = lambda col: (lambda b, p, qi, ki: (b, qi[p], col))
    kmap = lambda col: (lambda b, p, qi, ki: (b, ki[p], col))
    dv_aug = C_HEAD_DIM + ONES_ROWS
    grid_spec = pltpu.PrefetchScalarGridSpec(
        num_scalar_prefetch=2,
        grid=(bsz, qi.shape[0]),
        in_specs=[pl.BlockSpec(memory_space=pltpu.SMEM),
                  pl.BlockSpec((1, ta, C_WIDTH), qmap(Z_QC)),
                  pl.BlockSpec((1, ta, C_WIDTH), kmap(Z_KC)),
                  pl.BlockSpec((1, C_WIDTH, ta), lambda b, p, qi, ki: (b, ZT_VC, ki[p])),
                  pl.BlockSpec((1, ta, ta), lambda b, p, qi, ki: (b, ki[p], qi[p]))],
        out_specs=pl.BlockSpec((1, ta, C_WIDTH), qmap(0)),
        scratch_shapes=[pltpu.VMEM((C_HEADS, 1, ta), F32),
                        pltpu.VMEM((C_HEADS, dv_aug, ta), F32),
                        pltpu.VMEM((C_HEADS, 2, ta, ta), F32)],
    )
    return pl.pallas_call(
        _cattn_body,
        grid_spec=grid_spec,
        out_shape=jax.ShapeDtypeStruct((bsz, seq, C_WIDTH), BF16),
        compiler_params=_cparams(("arbitrary", "arbitrary")),
        name="dsa_attn",
    )(qi, ki, rel_bias, z, z, zt, mask)


def _dattn(z, zt, rel_bias, diff_lam, diff_norm_g, lambda_init, ta):
    bsz, seq, _ = z.shape
    qi, ki = _causal_pairs(seq // ta)
    n_maps = 2 * DIFF_HEADS
    dv = 2 * DIFF_HEAD_DIM
    qmap = lambda col: (lambda b, p, qi, ki: (b, qi[p], col))
    kmap = lambda col: (lambda b, p, qi, ki: (b, ki[p], col))
    grid_spec = pltpu.PrefetchScalarGridSpec(
        num_scalar_prefetch=2,
        grid=(bsz, qi.shape[0]),
        in_specs=[pl.BlockSpec(memory_space=pltpu.SMEM),
                  pl.BlockSpec((1, ta, DIFF_W), qmap(Z_QD)),
                  pl.BlockSpec((1, ta, DIFF_W), kmap(Z_KD)),
                  pl.BlockSpec((1, DIFF_W, ta), lambda b, p, qi, ki: (b, ZT_VD, ki[p])),
                  pl.BlockSpec(diff_lam.shape, lambda b, p, qi, ki: (0, 0)),
                  pl.BlockSpec((dv, 1), lambda b, p, qi, ki: (0, 0))],
        out_specs=pl.BlockSpec((1, ta, DIFF_W), qmap(0)),
        scratch_shapes=[pltpu.VMEM((n_maps, 1, ta), F32),
                        pltpu.VMEM((n_maps, dv + ONES_ROWS, ta), F32),
                        pltpu.VMEM((DIFF_HEADS, 2, ta, ta), F32)],
    )
    return pl.pallas_call(
        functools.partial(_dattn_body, lambda_init=lambda_init),
        grid_spec=grid_spec,
        out_shape=jax.ShapeDtypeStruct((bsz, seq, DIFF_W), BF16),
        compiler_params=_cparams(("arbitrary", "arbitrary")),
        name="diff_attn",
    )(qi, ki, rel_bias, z, z, zt, diff_lam, diff_norm_g.reshape(dv, 1))


def _out_proj_body(oc_ref, od_ref, x_ref, g1_ref, w_ref, o_ref):
    y = (jnp.dot(oc_ref[0], w_ref[0:C_WIDTH, :], preferred_element_type=F32)
         + jnp.dot(od_ref[0], w_ref[C_WIDTH:C_WIDTH + DIFF_W, :], preferred_element_type=F32))
    o_ref[0] = x_ref[0] + g1_ref[0] * y


def _out_proj(out_c, out_d, x, g1, w_out_bf16, tm):
    bsz, seq, d = x.shape
    return pl.pallas_call(
        _out_proj_body,
        grid=(bsz, seq // tm),
        in_specs=[pl.BlockSpec((1, tm, C_WIDTH), lambda b, i: (b, i, 0)),
                  pl.BlockSpec((1, tm, DIFF_W), lambda b, i: (b, i, 0)),
                  pl.BlockSpec((1, tm, d), lambda b, i: (b, i, 0)),
                  pl.BlockSpec((1, 1, d), lambda b, i: (b, 0, 0)),
                  pl.BlockSpec((C_WIDTH + DIFF_W, d), lambda b, i: (0, 0))],
        out_specs=pl.BlockSpec((1, tm, d), lambda b, i: (b, i, 0)),
        out_shape=jax.ShapeDtypeStruct((bsz, seq, d), F32),
        compiler_params=_cparams(("arbitrary", "arbitrary")),
        name="attn_out_proj",
    )(out_c, out_d, x, g1.reshape(bsz, 1, d), w_out_bf16)


def _attn_in_weights(cd_w_in):
    sizes = (C_WIDTH, C_WIDTH, C_WIDTH, IDX_HEADS * IDX_DIM, IDX_DIM, IDX_HEADS, DIFF_W, DIFF_W, DIFF_W)
    cuts = np.cumsum(sizes)[:-1]
    q_c, k_c, v_c, q_i, k_i, w_i, q_d, k_d, v_d = jnp.split(cd_w_in, cuts, axis=1)
    d = cd_w_in.shape[0]
    w = jnp.concatenate([q_c * (C_HEAD_DIM ** -0.5 * LOG2E), k_c, q_d * (DIFF_HEAD_DIM ** -0.5 * LOG2E), k_d,
                         q_i, k_i, jnp.zeros((d, LANES - IDX_DIM), cd_w_in.dtype)], axis=1)
    wt = jnp.concatenate([v_c, v_d, w_i * (IDX_DIM * IDX_HEADS) ** -0.5,
                          jnp.zeros((d, BF16_ROWS - IDX_HEADS), cd_w_in.dtype)], axis=1).T
    return w.astype(BF16), wt.astype(BF16)


def kernel(x, c, positions, rel_bias, norm_g, final_norm_g, ada_w, ada_b, ab_w_in, ab_conv_a, ab_conv_b,
           ab_conv_b_bias, ab_ln_g, ab_ln_b, ab_w_out, cd_w_in, diff_lam, diff_norm_g, cd_w_out,
           moe_wr_g, moe_br_g, moe_wr_e, moe_br_e, moe_w_gate, moe_w_up, moe_w_down):
    del positions
    bsz, seq, d = x.shape
    depth = ada_w.shape[0]
    tm = min(512, seq)
    ta = min(ATT_TILE, seq)
    assert ta >= MAX_DISTANCE and seq % ta == 0
    topk = min(TOPK_MAX, seq // 4)
    mods = _ada_mod(c, ada_w, ada_b)
    for i in range(depth):
        sh1, sc1, g1, sh2, sc2, g2 = jnp.split(mods[i], 6, axis=-1)
        j = i // 2
        if i % 2 == 0:
            z = _norm_proj(x, norm_g[i, 0], sh1, sc1, ab_w_in[j].astype(BF16), tm)
            x = _conv_mix(z, x, g1, ab_conv_a[j], ab_conv_b[j], ab_conv_b_bias[j], ab_ln_g[j], ab_ln_b[j],
                          ab_w_out[j].astype(BF16), min(256, seq))
        else:
            lambda_init = 0.8 - 0.6 * math.exp(-0.3 * i)
            w, wt = _attn_in_weights(cd_w_in[j])
            z, zt = _norm_proj(x, norm_g[i, 0], sh1, sc1, w, tm, wt)
            mask = _select(z, zt, topk)
            out_c = _cattn(z, zt, mask, rel_bias, ta)
            out_d = _dattn(z, zt, rel_bias, diff_lam[j], diff_norm_g[j], lambda_init, ta)
            x = _out_proj(out_c, out_d, x, g1, cd_w_out[j].astype(BF16), tm)
        x = _hier_moe(x, norm_g[i, 1], sh2, sc2, g2, moe_wr_g[i], moe_br_g[i], moe_wr_e[i], moe_br_e[i],
                      moe_w_gate, moe_w_up, moe_w_down, i, final_norm_g, final_norm=(i == depth - 1))
    return x
```

```python
import functools
import math

import numpy as np
import jax
import jax.numpy as jnp
from jax import lax
from jax.experimental import pallas as pl
from jax.experimental.pallas import tpu as pltpu

F32 = jnp.float32
BF16 = jnp.bfloat16
I32 = jnp.int32
HIGHEST = lax.Precision.HIGHEST

EPS = 1e-6
A_WIDTH = 512
A_CONV = 3
B_WIDTH = 512
B_CONV = 31
C_HEADS = 8
C_HEAD_DIM = 64
IDX_HEADS = 8
IDX_DIM = 32
TOPK_MAX = 256
DIFF_HEADS = 4
DIFF_HEAD_DIM = 64
NUM_BUCKETS = 32
MAX_DISTANCE = 128
N_GROUPS = 4
EXPERTS_PER_GROUP = 8
N_EXPERTS = N_GROUPS * EXPERTS_PER_GROUP
C_WIDTH = C_HEADS * C_HEAD_DIM
DIFF_W = DIFF_HEADS * 2 * DIFF_HEAD_DIM
LANES = 128
BF16_ROWS = 16
INT_MIN = -(2 ** 31)
LOG2E = math.log2(math.e)
VMEM_LIMIT = 56 * 1024 * 1024

Z_QC, Z_KC, Z_QD, Z_KD = 0, 1, 2, 3
Z_QIDX_OFF = 4 * C_WIDTH
Z_QIDX_W = IDX_HEADS * IDX_DIM
Z_KIDX_OFF = Z_QIDX_OFF + Z_QIDX_W
Z_COLS = Z_KIDX_OFF + LANES
ZT_VC, ZT_VD = 0, 1
ZT_W_OFF = C_WIDTH + DIFF_W
ZT_ROWS = ZT_W_OFF + BF16_ROWS


def _bucket_starts():
    n = np.arange(0, 2 * MAX_DISTANCE)
    me = NUM_BUCKETS // 2
    lr = np.log(np.maximum(n, 1) / me) / math.log(MAX_DISTANCE / me)
    large = me + (lr * (NUM_BUCKETS - me)).astype(np.int64)
    b = np.where(n < me, n, np.minimum(large, NUM_BUCKETS - 1))
    return [int(n[b >= k].min()) for k in range(NUM_BUCKETS)]


BUCKET_START = _bucket_starts()


def _cparams(sem):
    return pltpu.CompilerParams(dimension_semantics=sem, vmem_limit_bytes=VMEM_LIMIT)


def _rms(x):
    return x * lax.rsqrt(jnp.mean(x * x, axis=-1, keepdims=True) + EPS)


def _sigmoid(x):
    return 1.0 / (1.0 + jnp.exp(-x))


def _dot_nt(a, b):
    return lax.dot_general(a, b, (((1,), (1,)), ((), ())), preferred_element_type=F32)


def _ada_body(c_ref, w_ref, b_ref, o_ref):
    c = c_ref[...]
    cond = c * _sigmoid(c)
    o_ref[0] = jnp.dot(cond, w_ref[0], precision=HIGHEST, preferred_element_type=F32) + b_ref[0]


def _ada_mod(c, ada_w, ada_b):
    depth, d, n6 = ada_w.shape
    bsz = c.shape[0]
    rows = 8
    c_pad = jnp.zeros((rows, d), F32).at[:bsz].set(c)
    tn = 1536
    out = pl.pallas_call(
        _ada_body,
        grid=(depth, n6 // tn),
        in_specs=[pl.BlockSpec((rows, d), lambda i, j: (0, 0)),
                  pl.BlockSpec((1, d, tn), lambda i, j: (i, 0, j)),
                  pl.BlockSpec((1, 1, tn), lambda i, j: (i, 0, j))],
        out_specs=pl.BlockSpec((1, rows, tn), lambda i, j: (i, 0, j)),
        out_shape=jax.ShapeDtypeStruct((depth, rows, n6), F32),
        compiler_params=_cparams(("arbitrary", "arbitrary")),
        name="ada_mod",
    )(c_pad, ada_w, ada_b.reshape(depth, 1, n6))
    return out[:, :bsz]


def _norm_proj_body(x_ref, g_ref, sh_ref, sc_ref, w_ref, *rest):
    y = _rms(x_ref[0]) * g_ref[...]
    h = (y * (1.0 + sc_ref[0]) + sh_ref[0]).astype(BF16)
    if len(rest) == 1:
        (o_ref,) = rest
    else:
        wt_ref, o_ref, ot_ref = rest
        ot_ref[0] = _dot_nt(wt_ref[...], h).astype(ot_ref.dtype)
    o_ref[0] = jnp.dot(h, w_ref[...], preferred_element_type=F32).astype(o_ref.dtype)


def _norm_proj(x, g, sh, sc, w_bf16, tm, wt_bf16=None):
    bsz, seq, d = x.shape
    n = w_bf16.shape[1]
    in_specs = [pl.BlockSpec((1, tm, d), lambda b, i: (b, i, 0)),
                pl.BlockSpec((1, d), lambda b, i: (0, 0)),
                pl.BlockSpec((1, 1, d), lambda b, i: (b, 0, 0)),
                pl.BlockSpec((1, 1, d), lambda b, i: (b, 0, 0)),
                pl.BlockSpec((d, n), lambda b, i: (0, 0))]
    out_specs = pl.BlockSpec((1, tm, n), lambda b, i: (b, i, 0))
    out_shape = jax.ShapeDtypeStruct((bsz, seq, n), BF16)
    args = [x, g.reshape(1, d), sh.reshape(bsz, 1, d), sc.reshape(bsz, 1, d), w_bf16]
    if wt_bf16 is not None:
        nt = wt_bf16.shape[0]
        in_specs.append(pl.BlockSpec((nt, d), lambda b, i: (0, 0)))
        out_specs = [out_specs, pl.BlockSpec((1, nt, tm), lambda b, i: (b, 0, i))]
        out_shape = [out_shape, jax.ShapeDtypeStruct((bsz, nt, seq), BF16)]
        args.append(wt_bf16)
    return pl.pallas_call(
        _norm_proj_body,
        grid=(bsz, seq // tm),
        in_specs=in_specs,
        out_specs=out_specs,
        out_shape=out_shape,
        compiler_params=_cparams(("arbitrary", "arbitrary")),
        name="norm_proj",
    )(*args)


CONV_HALO = 32
CONV_ROWS = 64


def _conv_body(z_ref, x_ref, g1_ref, ca_ref, cb_ref, cbb_ref, lng_ref, lnb_ref, wo_ref, o_ref,
               ua_scr, ub_scr, y_scr, *, tl):
    l = pl.program_id(1)

    @pl.when(l == 0)
    def _():
        ua_scr[0:CONV_HALO, :] = jnp.zeros((CONV_HALO, A_WIDTH), F32)
        ub_scr[0:CONV_HALO, :] = jnp.zeros((CONV_HALO, B_WIDTH), F32)

    a = A_WIDTH
    gate_c = z_ref[0, :, a:2 * a].astype(F32)
    x_a = z_ref[0, :, 2 * a:3 * a].astype(F32)
    ua_scr[CONV_HALO:CONV_HALO + tl, :] = gate_c * x_a
    val_b = z_ref[0, :, 3 * a:3 * a + B_WIDTH].astype(F32)
    glu = z_ref[0, :, 3 * a + B_WIDTH:3 * a + 2 * B_WIDTH].astype(F32)
    ub_scr[CONV_HALO:CONV_HALO + tl, :] = val_b * _sigmoid(glu)

    for r in range(0, tl, CONV_ROWS):
        acc_a = None
        for k in range(A_CONV):
            tap = ua_scr[CONV_HALO + r - (A_CONV - 1) + k:CONV_HALO + r - (A_CONV - 1) + k + CONV_ROWS, :]
            term = tap * ca_ref[k:k + 1, :]
            acc_a = term if acc_a is None else acc_a + term
        gate_b = z_ref[0, r:r + CONV_ROWS, 0:a].astype(F32)
        y_scr[r:r + CONV_ROWS, 0:a] = (gate_b * acc_a).astype(BF16)

        acc_b = None
        for k in range(B_CONV):
            tap = ub_scr[CONV_HALO + r - (B_CONV - 1) + k:CONV_HALO + r - (B_CONV - 1) + k + CONV_ROWS, :]
            term = tap * cb_ref[k:k + 1, :]
            acc_b = term if acc_b is None else acc_b + term
        u = acc_b + cbb_ref[...]
        mu = jnp.mean(u, axis=-1, keepdims=True)
        uc = u - mu
        var = jnp.mean(uc * uc, axis=-1, keepdims=True)
        v = uc * lax.rsqrt(var + EPS) * lng_ref[...] + lnb_ref[...]
        y_scr[r:r + CONV_ROWS, a:a + B_WIDTH] = (v * _sigmoid(v)).astype(BF16)

    ua_scr[0:CONV_HALO, :] = ua_scr[tl:tl + CONV_HALO, :]
    ub_scr[0:CONV_HALO, :] = ub_scr[tl:tl + CONV_HALO, :]
    y = jnp.dot(y_scr[...], wo_ref[...], preferred_element_type=F32)
    o_ref[0] = x_ref[0] + g1_ref[0] * y


def _conv_mix(z, x, g1, conv_a, conv_b, conv_b_bias, ln_g, ln_b, w_out_bf16, tl):
    bsz, seq, d = x.shape
    nz = z.shape[-1]
    wide = A_WIDTH + B_WIDTH
    full = lambda shape: pl.BlockSpec(shape, lambda b, l: (0,) * len(shape))
    return pl.pallas_call(
        functools.partial(_conv_body, tl=tl),
        grid=(bsz, seq // tl),
        in_specs=[pl.BlockSpec((1, tl, nz), lambda b, l: (b, l, 0)),
                  pl.BlockSpec((1, tl, d), lambda b, l: (b, l, 0)),
                  pl.BlockSpec((1, 1, d), lambda b, l: (b, 0, 0)),
                  full((A_CONV, A_WIDTH)), full((B_CONV, B_WIDTH)), full((1, B_WIDTH)),
                  full((1, B_WIDTH)), full((1, B_WIDTH)), full((wide, d))],
        out_specs=pl.BlockSpec((1, tl, d), lambda b, l: (b, l, 0)),
        out_shape=jax.ShapeDtypeStruct((bsz, seq, d), F32),
        scratch_shapes=[pltpu.VMEM((CONV_HALO + tl, A_WIDTH), F32),
                        pltpu.VMEM((CONV_HALO + tl, B_WIDTH), F32),
                        pltpu.VMEM((tl, wide), BF16)],
        compiler_params=_cparams(("arbitrary", "arbitrary")),
        name="conv_mix",
    )(z, x, g1.reshape(bsz, 1, d), conv_a, conv_b, conv_b_bias.reshape(1, -1), ln_g.reshape(1, -1),
      ln_b.reshape(1, -1), w_out_bf16)


MOE_ROWS = 512
ROUTER_ROWS = 512
META_GATE0, META_GATE1, META_POS0, META_POS1 = range(4)
SEG_LEN, SEG_BASE, SEG_OFF = range(3)
SEG_ALIGN = 8
SEG_SIZES = tuple(2 ** b for b in range(10, 2, -1))
GROUP_LANE0 = N_EXPERTS


def _router_body(x_ref, g_ref, sh_ref, sc_ref, wr_ref, br_ref, tri_ref, upper_ref, h_ref, meta_ref, post_ref, seg_ref,
                 cnt_ref, base_scr):
    @pl.when(pl.program_id(0) == 0)
    def _():
        base_scr[...] = jnp.zeros_like(base_scr)

    h = _rms(x_ref[...]) * g_ref[...]
    h = h * (1.0 + sc_ref[0]) + sh_ref[0]
    h_ref[...] = h.astype(h_ref.dtype)
    logits = jnp.dot(h, wr_ref[...], precision=HIGHEST, preferred_element_type=F32) + br_ref[...]
    tr = logits.shape[0]
    lane = lax.broadcasted_iota(I32, (tr, LANES), 1)
    lane_f = lane.astype(F32)
    neg = jnp.float32(-jnp.inf)
    big = jnp.float32(1e9)

    is_group = (lane >= GROUP_LANE0) & (lane < GROUP_LANE0 + N_GROUPS)
    glog = jnp.where(is_group, logits, neg)
    gmax = jnp.max(glog, axis=1, keepdims=True)
    p_top = 1.0 / jnp.sum(jnp.exp(glog - gmax), axis=1, keepdims=True)
    g_sel = jnp.min(jnp.where(glog == gmax, lane_f, big), axis=1, keepdims=True) - GROUP_LANE0
    lo = g_sel * EXPERTS_PER_GROUP
    in_group = (lane_f >= lo) & (lane_f < lo + EXPERTS_PER_GROUP)
    f1 = jnp.where(in_group, logits, neg)
    v1 = jnp.max(f1, axis=1, keepdims=True)
    i1 = jnp.min(jnp.where(f1 == v1, lane_f, big), axis=1, keepdims=True)
    f2 = jnp.where(lane_f == i1, neg, f1)
    v2 = jnp.max(f2, axis=1, keepdims=True)
    i2 = jnp.min(jnp.where(f2 == v2, lane_f, big), axis=1, keepdims=True)
    a = jnp.exp(v2 - v1)
    w1 = 1.0 / (1.0 + a)
    gate0 = p_top * w1
    gate1 = p_top * (a * w1)

    oh0 = lane_f == i1
    oh1 = lane_f == i2
    ind0 = jnp.where(oh0, 1.0, 0.0)
    ind1 = jnp.where(oh1, 1.0, 0.0)
    pre0 = jnp.dot(tri_ref[...], ind0.astype(BF16), preferred_element_type=F32)
    pre1 = jnp.dot(tri_ref[...], ind1.astype(BF16), preferred_element_type=F32)
    tot0 = jnp.sum(ind0, axis=0, keepdims=True)
    tot1 = jnp.sum(ind1, axis=0, keepdims=True)
    seg_len = jnp.floor((tot0 + tot1 + (SEG_ALIGN - 1)) * (1.0 / SEG_ALIGN)) * SEG_ALIGN
    seg_off = jnp.dot(jnp.broadcast_to(seg_len, (8, LANES)), upper_ref[...], precision=HIGHEST,
                      preferred_element_type=F32)[0:1]
    pos0 = jnp.sum(jnp.where(oh0, seg_off + pre0, 0.0), axis=1, keepdims=True)
    pos1 = jnp.sum(jnp.where(oh1, seg_off + tot0 + pre1, 0.0), axis=1, keepdims=True)
    base = base_scr[...]
    new_base = base + seg_len
    base_scr[...] = new_base
    cnt_ref[...] = new_base

    meta = jnp.zeros((tr, LANES), F32)
    for col, val in ((META_GATE0, gate0), (META_GATE1, gate1), (META_POS0, pos0), (META_POS1, pos1)):
        meta = jnp.where(lane == col, val, meta)
    meta_ref[...] = meta
    post_ref[0] = meta.T[0:8].astype(I32)
    srow = lax.broadcasted_iota(I32, (8, LANES), 0)
    seg = jnp.where(srow == SEG_LEN, seg_len, jnp.where(srow == SEG_BASE, base, jnp.where(srow == SEG_OFF, seg_off, 0.0)))
    seg_ref[0] = seg.astype(I32)


def _moe_router(x2, g, sh, sc, wr_g, br_g, wr_e, br_e, seq):
    n_tok, d = x2.shape
    bsz = n_tok // seq
    tr = min(ROUTER_ROWS, seq)
    steps_per_batch = seq // tr
    wr = jnp.zeros((d, LANES), F32).at[:, :N_EXPERTS].set(wr_e).at[:, GROUP_LANE0:GROUP_LANE0 + N_GROUPS].set(wr_g)
    br = jnp.zeros((1, LANES), F32).at[0, :N_EXPERTS].set(br_e).at[0, GROUP_LANE0:GROUP_LANE0 + N_GROUPS].set(br_g)
    tri = jnp.tril(jnp.ones((tr, tr), BF16), -1)
    upper = jnp.triu(jnp.ones((LANES, LANES), F32), 1)
    nb = n_tok // tr
    full = lambda shape: pl.BlockSpec(shape, lambda i: (0,) * len(shape))
    return pl.pallas_call(
        _router_body,
        grid=(nb,),
        in_specs=[pl.BlockSpec((tr, d), lambda i: (i, 0)),
                  full((1, d)),
                  pl.BlockSpec((1, 1, d), lambda i: (i // steps_per_batch, 0, 0)),
                  pl.BlockSpec((1, 1, d), lambda i: (i // steps_per_batch, 0, 0)),
                  full((d, LANES)), full((1, LANES)), full((tr, tr)), full((LANES, LANES))],
        out_specs=[pl.BlockSpec((tr, d), lambda i: (i, 0)),
                   pl.BlockSpec((tr, LANES), lambda i: (i, 0)),
                   pl.BlockSpec((1, 8, tr), lambda i: (i, 0, 0)),
                   pl.BlockSpec((1, 8, LANES), lambda i: (i, 0, 0)),
                   full((1, LANES))],
        out_shape=[jax.ShapeDtypeStruct((n_tok, d), BF16),
                   jax.ShapeDtypeStruct((n_tok, LANES), F32),
                   jax.ShapeDtypeStruct((nb, 8, tr), I32),
                   jax.ShapeDtypeStruct((nb, 8, LANES), I32),
                   jax.ShapeDtypeStruct((1, LANES), F32)],
        scratch_shapes=[pltpu.VMEM((1, LANES), F32)],
        compiler_params=_cparams(("arbitrary",)),
        name="moe_router",
    )(x2, g.reshape(1, d), sh.reshape(bsz, 1, d), sc.reshape(bsz, 1, d), wr, br, tri, upper)


def _segment_copies(seg_ref, starts_ref, make_copy, action):
    def per_expert(e, carry):
        length = seg_ref[0, SEG_LEN, e]
        local = seg_ref[0, SEG_OFF, e]
        glob = starts_ref[e] + seg_ref[0, SEG_BASE, e]
        for size in SEG_SIZES:
            hit = (length & size) != 0

            @pl.when(hit)
            def _(local=local, glob=glob, size=size):
                cp = make_copy(pl.multiple_of(local, SEG_ALIGN), pl.multiple_of(glob, SEG_ALIGN), size)
                cp.start() if action == "start" else cp.wait()

            step = jnp.where(hit, size, 0)
            local = local + step
            glob = glob + step
        return carry

    lax.fori_loop(0, N_EXPERTS, per_expert, 0)


def _local_rows(tr):
    return 2 * tr + N_EXPERTS * SEG_ALIGN


def _dispatch_body(starts_ref, seg_ref, prev_seg_ref, h_ref, post_ref, xr_ref, xs_scr, zero_scr, sems, *, tr):
    i = pl.program_id(0)
    slot = i % 2
    lb = xs_scr.shape[1]
    sem = sems.at[0]

    def make_copy_of(s):
        def make_copy(local, glob, size):
            return pltpu.make_async_copy(xs_scr.at[s, pl.ds(local, size)], xr_ref.at[pl.ds(glob, size)], sems.at[s])
        return make_copy

    rows = lax.broadcasted_iota(I32, (lb, tr), 0)
    place = (rows == post_ref[0, META_POS0:META_POS0 + 1, :]) | (rows == post_ref[0, META_POS1:META_POS1 + 1, :])
    xs_scr[slot] = jnp.dot(jnp.where(place, 1.0, 0.0).astype(BF16), h_ref[...], preferred_element_type=F32)
    _segment_copies(seg_ref, starts_ref, make_copy_of(slot), "start")

    @pl.when(i > 0)
    def _():
        _segment_copies(prev_seg_ref, starts_ref, make_copy_of(1 - slot), "wait")

    @pl.when(i == pl.num_programs(0) - 1)
    def _():
        _segment_copies(seg_ref, starts_ref, make_copy_of(slot), "wait")
        zero_scr[...] = jnp.zeros_like(zero_scr)
        total = starts_ref[N_EXPERTS]
        tail = (-total) & (MOE_ROWS - 1)
        for action in ("start", "wait"):
            row = total
            for size in SEG_SIZES:
                if size >= MOE_ROWS:
                    continue
                hit = (tail & size) != 0

                @pl.when(hit)
                def _(row=row, size=size, action=action):
                    cp = pltpu.make_async_copy(zero_scr.at[pl.ds(0, size)],
                                               xr_ref.at[pl.ds(pl.multiple_of(row, SEG_ALIGN), size)], sem)
                    cp.start() if action == "start" else cp.wait()

                row = row + jnp.where(hit, size, 0)

        def free_block(b):
            return pltpu.make_async_copy(
                zero_scr, xr_ref.at[pl.ds(pl.multiple_of(b * MOE_ROWS, MOE_ROWS), MOE_ROWS)], sem)

        first_free = (total + MOE_ROWS - 1) // MOE_ROWS
        n_blocks = xr_ref.shape[0] // MOE_ROWS
        lax.fori_loop(first_free, n_blocks, lambda b, c: (free_block(b).start(), c)[1], 0)
        lax.fori_loop(first_free, n_blocks, lambda b, c: (free_block(b).wait(), c)[1], 0)


def _moe_dispatch(h2, post, seg, starts, n_rows, tr):
    n_tok, d = h2.shape
    grid_spec = pltpu.PrefetchScalarGridSpec(
        num_scalar_prefetch=1,
        grid=(n_tok // tr,),
        in_specs=[pl.BlockSpec((1, 8, LANES), lambda i, s: (i, 0, 0), memory_space=pltpu.SMEM),
                  pl.BlockSpec((1, 8, LANES), lambda i, s: (jnp.maximum(i - 1, 0), 0, 0), memory_space=pltpu.SMEM),
                  pl.BlockSpec((tr, d), lambda i, s: (i, 0)),
                  pl.BlockSpec((1, 8, tr), lambda i, s: (i, 0, 0))],
        out_specs=pl.BlockSpec(memory_space=pl.ANY),
        scratch_shapes=[pltpu.VMEM((2, _local_rows(tr), d), F32), pltpu.VMEM((MOE_ROWS, d), F32),
                        pltpu.SemaphoreType.DMA((2,))],
    )
    return pl.pallas_call(
        functools.partial(_dispatch_body, tr=tr),
        grid_spec=grid_spec,
        out_shape=jax.ShapeDtypeStruct((n_rows, d), F32),
        compiler_params=_cparams(("arbitrary",)),
        name="moe_dispatch",
    )(starts, seg, seg, h2, post)


def _expert_body(pb_ref, pe_ref, plo_ref, phi_ref, x_ref, wg_ref, wu_ref, wd_ref, o_ref, wg_s, wu_s, wd_s):
    p = pl.program_id(0)
    prev = jnp.maximum(p - 1, 0)
    new_expert = (p == 0) | (pe_ref[p] != pe_ref[prev])
    first = (p == 0) | (pb_ref[p] != pb_ref[prev])

    @pl.when(new_expert)
    def _():
        wg_s[...] = wg_ref[0, 0].astype(BF16)
        wu_s[...] = wu_ref[0, 0].astype(BF16)
        wd_s[...] = wd_ref[0, 0].astype(BF16)

    def rows_of_expert():
        x = x_ref[...].astype(BF16)
        gt = jnp.dot(x, wg_s[...], preferred_element_type=F32)
        up = jnp.dot(x, wu_s[...], preferred_element_type=F32)
        act = (gt * _sigmoid(gt)) * up
        y = jnp.dot(act.astype(BF16), wd_s[...], preferred_element_type=F32)
        rows = lax.broadcasted_iota(I32, (y.shape[0], 1), 0)
        return jnp.where((rows >= plo_ref[p]) & (rows < phi_ref[p]), y, 0.0)

    nonempty = phi_ref[p] > plo_ref[p]

    @pl.when(first & nonempty)
    def _():
        o_ref[...] = rows_of_expert()

    @pl.when(first & jnp.logical_not(nonempty))
    def _():
        o_ref[...] = jnp.zeros_like(o_ref)

    @pl.when(jnp.logical_not(first) & nonempty)
    def _():
        o_ref[...] += rows_of_expert()


def _moe_experts(x_rows, pairs, w_gate, w_up, w_down, layer):
    n_rows, d = x_rows.shape
    de = w_gate.shape[-1]
    n_pairs = pairs[0].shape[0]
    grid_spec = pltpu.PrefetchScalarGridSpec(
        num_scalar_prefetch=4,
        grid=(n_pairs,),
        in_specs=[pl.BlockSpec((MOE_ROWS, d), lambda p, pb, pe, lo, hi: (pb[p], 0)),
                  pl.BlockSpec((1, 1, d, de), lambda p, pb, pe, lo, hi: (layer, pe[p], 0, 0)),
                  pl.BlockSpec((1, 1, d, de), lambda p, pb, pe, lo, hi: (layer, pe[p], 0, 0)),
                  pl.BlockSpec((1, 1, de, d), lambda p, pb, pe, lo, hi: (layer, pe[p], 0, 0))],
        out_specs=pl.BlockSpec((MOE_ROWS, d), lambda p, pb, pe, lo, hi: (pb[p], 0)),
        scratch_shapes=[pltpu.VMEM((d, de), BF16), pltpu.VMEM((d, de), BF16), pltpu.VMEM((de, d), BF16)],
    )
    return pl.pallas_call(
        _expert_body,
        grid_spec=grid_spec,
        out_shape=jax.ShapeDtypeStruct((n_rows, d), F32),
        compiler_params=_cparams(("arbitrary",)),
        name="moe_experts",
    )(*pairs, x_rows, w_gate, w_up, w_down)


def _combine_body(starts_ref, seg_ref, next_seg_ref, y_ref, x_ref, meta_ref, g2_ref, fg_ref, o_ref, ys_scr, sems, *,
                  final_norm):
    i = pl.program_id(0)
    slot = i % 2

    def make_copy_of(s):
        def make_copy(local, glob, size):
            return pltpu.make_async_copy(y_ref.at[pl.ds(glob, size)], ys_scr.at[s, pl.ds(local, size)], sems.at[s])
        return make_copy

    @pl.when(i == 0)
    def _():
        ys_scr[...] = jnp.zeros_like(ys_scr)
        _segment_copies(seg_ref, starts_ref, make_copy_of(slot), "start")

    @pl.when(i + 1 < pl.num_programs(0))
    def _():
        _segment_copies(next_seg_ref, starts_ref, make_copy_of(1 - slot), "start")

    _segment_copies(seg_ref, starts_ref, make_copy_of(slot), "wait")

    meta = meta_ref[...]
    tr = meta.shape[0]
    y = ys_scr[slot].astype(BF16)
    cols = lax.broadcasted_iota(I32, (tr, ys_scr.shape[1]), 1)

    def picked(col):
        pos = meta[:, col:col + 1].astype(I32)
        return jnp.dot(jnp.where(cols == pos, 1.0, 0.0).astype(BF16), y, preferred_element_type=F32)

    moe = meta[:, META_GATE0:META_GATE0 + 1] * picked(META_POS0) + meta[:, META_GATE1:META_GATE1 + 1] * picked(META_POS1)
    xn = x_ref[...] + g2_ref[0] * moe
    if final_norm:
        xn = _rms(xn) * fg_ref[...]
    o_ref[...] = xn


def _moe_combine(y_rows, seg, starts, x2, meta, g2, final_g, seq, tr, final_norm):
    n_tok, d = x2.shape
    bsz = n_tok // seq
    steps_per_batch = seq // tr
    nb = n_tok // tr
    grid_spec = pltpu.PrefetchScalarGridSpec(
        num_scalar_prefetch=1,
        grid=(nb,),
        in_specs=[pl.BlockSpec((1, 8, LANES), lambda i, s: (i, 0, 0), memory_space=pltpu.SMEM),
                  pl.BlockSpec((1, 8, LANES), lambda i, s: (jnp.minimum(i + 1, nb - 1), 0, 0), memory_space=pltpu.SMEM),
                  pl.BlockSpec(memory_space=pl.ANY),
                  pl.BlockSpec((tr, d), lambda i, s: (i, 0)),
                  pl.BlockSpec((tr, LANES), lambda i, s: (i, 0)),
                  pl.BlockSpec((1, 1, d), lambda i, s: (i // steps_per_batch, 0, 0)),
                  pl.BlockSpec((1, d), lambda i, s: (0, 0))],
        out_specs=pl.BlockSpec((tr, d), lambda i, s: (i, 0)),
        scratch_shapes=[pltpu.VMEM((2, _local_rows(tr), d), F32), pltpu.SemaphoreType.DMA((2,))],
    )
    return pl.pallas_call(
        functools.partial(_combine_body, final_norm=final_norm),
        grid_spec=grid_spec,
        out_shape=jax.ShapeDtypeStruct((n_tok, d), F32),
        compiler_params=_cparams(("arbitrary",)),
        name="moe_combine",
    )(starts, seg, seg, y_rows, x2, meta, g2.reshape(bsz, 1, d), final_g.reshape(1, d))


def _expert_pairs(counts, n_rows):
    n_blocks = n_rows // MOE_ROWS
    n_pairs = n_blocks + N_EXPERTS
    ends = jnp.cumsum(counts)
    starts = ends - counts
    first_blk = starts // MOE_ROWS
    last_blk = (ends - 1) // MOE_ROWS
    npairs = jnp.where(counts > 0, last_blk - first_blk + 1, 0)
    pend = jnp.cumsum(npairs)
    poff = pend - npairs
    total = pend[-1]
    used_blocks = (ends[-1] + MOE_ROWS - 1) // MOE_ROWS
    p = jnp.arange(n_pairs, dtype=I32)
    p_eff = jnp.minimum(p, total - 1)
    e = jnp.minimum(jnp.sum(pend[None, :] <= p_eff[:, None], axis=1), N_EXPERTS - 1).astype(I32)
    valid = p < total
    blk = jnp.where(valid, first_blk[e] + p_eff - poff[e], jnp.minimum(used_blocks + p - total, n_blocks - 1)).astype(I32)
    lo = jnp.where(valid, jnp.clip(starts[e] - blk * MOE_ROWS, 0, MOE_ROWS), 0).astype(I32)
    hi = jnp.where(valid, jnp.clip(ends[e] - blk * MOE_ROWS, 0, MOE_ROWS), 0).astype(I32)
    return blk, e, lo, hi


def _hier_moe(x, g, sh, sc, g2, wr_g, br_g, wr_e, br_e, w_gate, w_up, w_down, layer, final_g, final_norm):
    bsz, seq, d = x.shape
    n_tok = bsz * seq
    tr = min(ROUTER_ROWS, seq)
    assert 2 * tr <= SEG_SIZES[0]
    x2 = x.reshape(n_tok, d)
    h2, meta, post, seg, cnt = _moe_router(x2, g, sh, sc, wr_g, br_g, wr_e, br_e, seq)
    counts = cnt[0, :N_EXPERTS].astype(I32)
    ends = jnp.cumsum(counts)
    starts = jnp.concatenate([ends - counts, ends[-1:]])
    n_rows = -(-(2 * n_tok + (n_tok // tr) * N_EXPERTS * (SEG_ALIGN - 1)) // MOE_ROWS) * MOE_ROWS
    x_rows = _moe_dispatch(h2, post, seg, starts, n_rows, tr)
    y_rows = _moe_experts(x_rows, _expert_pairs(counts, n_rows), w_gate, w_up, w_down, layer)
    out = _moe_combine(y_rows, seg, starts, x2, meta, g2, final_g, seq, tr, final_norm)
    return out.reshape(bsz, seq, d)


SEL_COLS = 256
SEL_CHUNK = 512
SEL_FOLD = 32
BIT_GROUP = 256
SEL_SWEEP = 64
ATT_TILE = 512


def _idx_queries(zq):
    zf = zq.astype(F32)
    return [zf[:, h * IDX_DIM:(h + 1) * IDX_DIM].astype(BF16) for h in range(IDX_HEADS)]


def _idx_score_t(k, qs, wt):
    acc = None
    for h, q in enumerate(qs):
        term = jnp.maximum(_dot_nt(k, q), 0.0) * wt[h:h + 1, :]
        acc = term if acc is None else acc + term
    return acc


def _bit_transpose32(a):
    a = list(a)
    m, j = 0x0000FFFF, 16
    while j:
        k = 0
        while k < 32:
            t = (a[k] ^ lax.shift_right_logical(a[k + j], jnp.int32(j))) & jnp.int32(m)
            a[k] = a[k] ^ t
            a[k + j] = a[k + j] ^ (t << j)
            k = (k + j + 1) & ~j
        j >>= 1
        m = (m ^ (m << j)) & 0xFFFFFFFF
    return a


def _select_body(zq_ref, zk_ref, wt_ref, o_ref, keys_scr, planes_scr, eq_scr, gt_scr, *, topk):
    i = pl.program_id(1)
    tq = zq_ref.shape[1]
    qs = _idx_queries(zq_ref[0])
    wt = wt_ref[0].astype(F32)
    n_chunks = ((i + 1) * tq + SEL_CHUNK - 1) // SEL_CHUNK
    krow0 = lax.broadcasted_iota(I32, (SEL_CHUNK, tq), 0)
    qcol = i * tq + lax.broadcasted_iota(I32, (SEL_CHUNK, tq), 1)
    int_min = jnp.int32(INT_MIN)

    groups = SEL_CHUNK // BIT_GROUP
    words = SEL_CHUNK // 32
    sweep_rows = min(SEL_SWEEP, eq_scr.shape[0])

    def fill(masked, c, carry):
        off = pl.multiple_of(c * SEL_CHUNK, SEL_CHUNK)
        k = zk_ref[0, pl.ds(off, SEL_CHUNK), :][:, 0:IDX_DIM]
        bits = lax.bitcast_convert_type(_idx_score_t(k, qs, wt), I32)
        key = bits ^ ((bits >> 31) | int_min)
        if masked:
            key = jnp.where(krow0 + off <= qcol, key, 0)
        keys_scr[pl.ds(off, SEL_CHUNK), :] = key
        for g in range(groups):
            for lt in range(0, tq, LANES):
                ku = key[g * BIT_GROUP:(g + 1) * BIT_GROUP, lt:lt + LANES]
                planes = _bit_transpose32([ku[8 * r:8 * r + 8] for r in range(32)])
                wrow = pl.multiple_of(c * words + g * 8, 8)
                for b in range(32):
                    planes_scr[b, pl.ds(wrow, 8), lt:lt + LANES] = planes[b]
        return carry

    n_plain = (i * tq + 1) // SEL_CHUNK
    lax.fori_loop(0, n_plain, functools.partial(fill, False), 0)
    lax.fori_loop(n_plain, n_chunks, functools.partial(fill, True), 0)

    n_sweep = (n_chunks * words + sweep_rows - 1) // sweep_rows

    def pad(c, carry):
        wrow = pl.multiple_of(c * words, words)
        for b in range(32):
            planes_scr[b, pl.ds(wrow, words), :] = jnp.zeros((words, tq), I32)
        return carry

    lax.fori_loop(n_chunks, n_sweep * (sweep_rows // words), pad, 0)

    def sweep(upd, cnt_plane):
        def body(sb, acc):
            r0 = pl.multiple_of(sb * sweep_rows, sweep_rows)
            eq = eq_scr[pl.ds(r0, sweep_rows), :]
            gt = gt_scr[pl.ds(r0, sweep_rows), :]
            if upd is not None:
                plane, accept = upd
                hit = eq & planes_scr[plane, pl.ds(r0, sweep_rows), :]
                gt = jnp.where(accept, gt, gt | hit)
                eq = jnp.where(accept, hit, eq ^ hit)
                eq_scr[pl.ds(r0, sweep_rows), :] = eq
                gt_scr[pl.ds(r0, sweep_rows), :] = gt
            if cnt_plane is None:
                return acc
            ones = lax.population_count(gt | (eq & planes_scr[cnt_plane, pl.ds(r0, sweep_rows), :]))
            return acc + jnp.sum(ones.reshape(sweep_rows // 8, 8, tq), axis=0)
        acc = lax.fori_loop(0, n_sweep, body, jnp.zeros((8, tq), I32))
        return jnp.sum(acc.astype(F32), axis=0, keepdims=True)

    k_f = jnp.float32(topk)
    eq_scr[...] = jnp.full(eq_scr.shape, -1, I32)
    gt_scr[...] = jnp.zeros(gt_scr.shape, I32)

    def bit_step(ib, carry):
        u, cnt = carry
        accept = cnt >= k_f
        u = jnp.where(accept, u | jnp.left_shift(jnp.int32(1), 32 - ib), u)
        return u, sweep((ib - 1, accept), ib)

    u, cnt = lax.fori_loop(1, 32, bit_step, (jnp.zeros((1, tq), I32), sweep(None, 0)))
    accept = cnt >= k_f
    u = jnp.where(accept, u | 1, u)
    sweep((31, accept), None)
    some = jnp.where(u != 0, -1, 0)

    def emit(c, carry):
        off = pl.multiple_of(c * SEL_CHUNK, SEL_CHUNK)
        for g in range(groups):
            wrow = pl.multiple_of(c * words + g * 8, 8)
            keep = gt_scr[pl.ds(wrow, 8), :] | (eq_scr[pl.ds(wrow, 8), :] & some)
            rows = [jnp.where((lax.shift_right_logical(keep, 31 - r) & 1) != 0, 0.0, -jnp.inf) for r in range(32)]
            o_ref[0, pl.ds(pl.multiple_of(off + g * BIT_GROUP, BIT_GROUP), BIT_GROUP), :] = (
                jnp.concatenate(rows, axis=0).astype(o_ref.dtype))
        return carry

    lax.fori_loop(0, n_chunks, emit, 0)

    def popcount_rows(ref):
        def body(sb, acc):
            r0 = pl.multiple_of(sb * sweep_rows, sweep_rows)
            ones = lax.population_count(ref[pl.ds(r0, sweep_rows), :])
            return acc + jnp.sum(ones.reshape(sweep_rows // 8, 8, tq), axis=0)
        acc = lax.fori_loop(0, n_sweep, body, jnp.zeros((8, tq), I32))
        return jnp.sum(acc.astype(F32), axis=0, keepdims=True)

    n_gt = popcount_rows(gt_scr)
    n_eq = jnp.where(u != 0, popcount_rows(eq_scr), 0.0)

    @pl.when(jnp.max(n_gt + n_eq) > k_f)
    def _():
        thr = u

        def count(pred):
            def body(c, acc):
                off = pl.multiple_of(c * SEL_CHUNK, SEL_CHUNK)
                ind = jnp.where(pred(keys_scr[pl.ds(off, SEL_CHUNK), :], krow0 + off), 1, 0)
                return acc + jnp.sum(ind.reshape(SEL_CHUNK // SEL_FOLD, SEL_FOLD, tq), axis=0)
            acc = lax.fori_loop(0, n_chunks, body, jnp.zeros((SEL_FOLD, tq), I32))
            return jnp.sum(acc.astype(F32), axis=0, keepdims=True)

        need = k_f - n_gt

        def cap_step(ib, v):
            cand = v | jnp.left_shift(jnp.int32(1), 14 - ib)
            cnt = count(lambda keys, kidx: (keys == thr) & (kidx < cand))
            return jnp.where(cnt <= need, cand, v)

        cap = lax.fori_loop(0, 15, cap_step, jnp.zeros((1, tq), I32))
        cap = jnp.where((u != 0) & (n_gt + n_eq > k_f), cap, 2 ** 30)

        def drop(c, carry):
            off = pl.multiple_of(c * SEL_CHUNK, SEL_CHUNK)
            past = (keys_scr[pl.ds(off, SEL_CHUNK), :] == thr) & (krow0 + off >= cap)
            cur = o_ref[0, pl.ds(off, SEL_CHUNK), :].astype(F32)
            o_ref[0, pl.ds(off, SEL_CHUNK), :] = jnp.where(past, -jnp.inf, cur).astype(o_ref.dtype)
            return carry

        lax.fori_loop(0, n_chunks, drop, 0)

    def blank(c, carry):
        off = pl.multiple_of(c * SEL_CHUNK, SEL_CHUNK)
        o_ref[0, pl.ds(off, SEL_CHUNK), :] = jnp.full((SEL_CHUNK, tq), -jnp.inf, o_ref.dtype)
        return carry

    lax.fori_loop(n_chunks, o_ref.shape[1] // SEL_CHUNK, blank, 0)


def _select(z, zt, topk):
    bsz, seq, _ = z.shape
    tq = min(SEL_COLS, seq)
    assert seq % SEL_CHUNK == 0 and seq % tq == 0
    return pl.pallas_call(
        functools.partial(_select_body, topk=topk),
        grid=(bsz, seq // tq),
        in_specs=[pl.BlockSpec((1, tq, Z_QIDX_W), lambda b, i: (b, i, Z_QIDX_OFF // Z_QIDX_W)),
                  pl.BlockSpec((1, seq, LANES), lambda b, i: (b, 0, Z_KIDX_OFF // LANES)),
                  pl.BlockSpec((1, BF16_ROWS, tq), lambda b, i: (b, ZT_W_OFF // BF16_ROWS, i))],
        out_specs=pl.BlockSpec((1, seq, tq), lambda b, i: (b, 0, i)),
        out_shape=jax.ShapeDtypeStruct((bsz, seq, seq), BF16),
        scratch_shapes=[pltpu.VMEM((seq, tq), I32), pltpu.VMEM((32, seq // 32, tq), I32),
                        pltpu.VMEM((seq // 32, tq), I32), pltpu.VMEM((seq // 32, tq), I32)],
        compiler_params=_cparams(("arbitrary", "arbitrary")),
        name="dsa_select",
    )(z, z, zt)


BIAS_INIT_ROWS = 8
ONES_ROWS = BF16_ROWS


def _init_bias_tiles(btile, bias_ref, head0, n_heads, ta):
    col = lax.broadcasted_iota(I32, (BIAS_INIT_ROWS, ta), 1)
    row0 = lax.broadcasted_iota(I32, (BIAS_INIT_ROWS, ta), 0)

    def body(r, carry):
        off = pl.multiple_of(r * BIAS_INIT_ROWS, BIAS_INIT_ROWS)
        for kind in range(2):
            dist = col - (row0 + off) + kind * ta
            for h in range(n_heads):
                far = bias_ref[NUM_BUCKETS - 1, head0 + h]
                val = jnp.full((BIAS_INIT_ROWS, ta), (bias_ref[0, head0 + h] - far) * LOG2E, F32)
                for b in range(1, NUM_BUCKETS - 1):
                    val = jnp.where(dist >= BUCKET_START[b], (bias_ref[b, head0 + h] - far) * LOG2E, val)
                val = jnp.where(dist >= BUCKET_START[NUM_BUCKETS - 1], 0.0, val)
                btile[h, kind, pl.ds(off, BIAS_INIT_ROWS), :] = val
        return carry

    lax.fori_loop(0, ta // BIAS_INIT_ROWS, body, 0)


def _with_ones(vt):
    return jnp.concatenate([vt, jnp.ones((ONES_ROWS, vt.shape[1]), vt.dtype)], axis=0)


def _softmax_step_t(logits, v_aug, m_ref, acc_ref, idx):
    m_old = m_ref[idx]
    m_new = jnp.maximum(m_old, jnp.max(logits, axis=0, keepdims=True))
    m_safe = jnp.where(m_new == -jnp.inf, 0.0, m_new)
    p = jnp.exp2(logits - m_safe)
    alpha = jnp.exp2(m_old - m_safe)
    acc_ref[idx] = alpha * acc_ref[idx] + jnp.dot(v_aug, p.astype(BF16), preferred_element_type=F32)
    m_ref[idx] = m_new


def _reset_softmax(m_s, acc_s):
    m_s[...] = jnp.full(m_s.shape, -jnp.inf, F32)
    acc_s[...] = jnp.zeros(acc_s.shape, F32)


def _cattn_body(qi_ref, ki_ref, bias_ref, zq_ref, zk_ref, vt_ref, mask_ref, o_ref, m_s, acc_s, btile):
    b = pl.program_id(0)
    p = pl.program_id(1)
    qi = qi_ref[p]
    ki = ki_ref[p]
    ta = zq_ref.shape[1]
    hd = C_HEAD_DIM

    @pl.when((b == 0) & (p == 0))
    def _():
        _init_bias_tiles(btile, bias_ref, 0, C_HEADS, ta)

    @pl.when(ki == 0)
    def _():
        _reset_softmax(m_s, acc_s)

    def heads(extra_of_head):
        for h in range(C_HEADS):
            lo = h * hd
            logits = _dot_nt(zk_ref[0, :, lo:lo + hd], zq_ref[0, :, lo:lo + hd]) + extra_of_head(h)
            _softmax_step_t(logits, _with_ones(vt_ref[0, lo:lo + hd, :]), m_s, acc_s, h)

    near = ki >= qi - 1

    @pl.when(near)
    def _():
        heads(lambda h: mask_ref[0].astype(F32) + btile[h, qi - ki])

    @pl.when(jnp.logical_not(near))
    def _():
        heads(lambda h: mask_ref[0].astype(F32))

    @pl.when(ki == qi)
    def _():
        outs = []
        for h in range(C_HEADS):
            a = acc_s[h]
            outs.append((a[0:hd] / a[hd:hd + 1]).T)
        o_ref[0] = jnp.concatenate(outs, axis=1).astype(o_ref.dtype)


def _dattn_body(qi_ref, ki_ref, bias_ref, zq_ref, zk_ref, vt_ref, lam_ref, ng_ref, o_ref,
                m_s, acc_s, btile, *, lambda_init):
    b = pl.program_id(0)
    p = pl.program_id(1)
    qi = qi_ref[p]
    ki = ki_ref[p]
    ta = zq_ref.shape[1]
    hd = DIFF_HEAD_DIM
    dv = 2 * hd

    @pl.when((b == 0) & (p == 0))
    def _():
        _init_bias_tiles(btile, bias_ref, C_HEADS, DIFF_HEADS, ta)

    @pl.when(ki == 0)
    def _():
        _reset_softmax(m_s, acc_s)

    def heads(extra_of_head):
        for h in range(DIFF_HEADS):
            v_aug = _with_ones(vt_ref[0, dv * h:dv * (h + 1), :])
            extra = extra_of_head(h)
            for j in range(2):
                lo = (2 * h + j) * hd
                logits = _dot_nt(zk_ref[0, :, lo:lo + hd], zq_ref[0, :, lo:lo + hd])
                if extra is not None:
                    logits = logits + extra
                _softmax_step_t(logits, v_aug, m_s, acc_s, 2 * h + j)

    near = ki >= qi - 1

    @pl.when(near)
    def _():
        krow = ki * ta + lax.broadcasted_iota(I32, (ta, ta), 0)
        qcol = qi * ta + lax.broadcasted_iota(I32, (ta, ta), 1)
        heads(lambda h: jnp.where(krow <= qcol, btile[h, qi - ki], -jnp.inf))

    @pl.when(jnp.logical_not(near))
    def _():
        heads(lambda h: None)

    @pl.when(ki == qi)
    def _():
        lam_p = lam_ref[...]
        lam = (jnp.exp(jnp.sum(lam_p[0:1] * lam_p[1:2], axis=1, keepdims=True))
               - jnp.exp(jnp.sum(lam_p[2:3] * lam_p[3:4], axis=1, keepdims=True)) + lambda_init)
        outs = []
        for h in range(DIFF_HEADS):
            a1 = acc_s[2 * h]
            a2 = acc_s[2 * h + 1]
            o = a1[0:dv] / a1[dv:dv + 1] - lam * (a2[0:dv] / a2[dv:dv + 1])
            o = o * lax.rsqrt(jnp.mean(o * o, axis=0, keepdims=True) + EPS) * ng_ref[...] * (1.0 - lambda_init)
            outs.append(o.T)
        o_ref[0] = jnp.concatenate(outs, axis=1).astype(o_ref.dtype)


def _causal_pairs(nq):
    qi = [q for q in range(nq) for _ in range(q + 1)]
    ki = [k for q in range(nq) for k in range(q + 1)]
    return jnp.asarray(qi, I32), jnp.asarray(ki, I32)


def _cattn(z, zt, mask, rel_bias, ta):
    bsz, seq, _ = z.shape
    qi, ki = _causal_pairs(seq // ta)
    qmap = lambda col: (lambda b, p, qi, ki: (b, qi[p], col))
    kmap = lambda col: (lambda b, p, qi, ki: (b, ki[p], col))
    dv_aug = C_HEAD_DIM + ONES_ROWS
    grid_spec = pltpu.PrefetchScalarGridSpec(
        num_scalar_prefetch=2,
        grid=(bsz, qi.shape[0]),
        in_specs=[pl.BlockSpec(memory_space=pltpu.SMEM),
                  pl.BlockSpec((1, ta, C_WIDTH), qmap(Z_QC)),
                  pl.BlockSpec((1, ta, C_WIDTH), kmap(Z_KC)),
                  pl.BlockSpec((1, C_WIDTH, ta), lambda b, p, qi, ki: (b, ZT_VC, ki[p])),
                  pl.BlockSpec((1, ta, ta), lambda b, p, qi, ki: (b, ki[p], qi[p]))],
        out_specs=pl.BlockSpec((1, ta, C_WIDTH), qmap(0)),
        scratch_shapes=[pltpu.VMEM((C_HEADS, 1, ta), F32),
                        pltpu.VMEM((C_HEADS, dv_aug, ta), F32),
                        pltpu.VMEM((C_HEADS, 2, ta, ta), F32)],
    )
    return pl.pallas_call(
        _cattn_body,
        grid_spec=grid_spec,
        out_shape=jax.ShapeDtypeStruct((bsz, seq, C_WIDTH), BF16),
        compiler_params=_cparams(("arbitrary", "arbitrary")),
        name="dsa_attn",
    )(qi, ki, rel_bias, z, z, zt, mask)


def _dattn(z, zt, rel_bias, diff_lam, diff_norm_g, lambda_init, ta):
    bsz, seq, _ = z.shape
    qi, ki = _causal_pairs(seq // ta)
    n_maps = 2 * DIFF_HEADS
    dv = 2 * DIFF_HEAD_DIM
    qmap = lambda col: (lambda b, p, qi, ki: (b, qi[p], col))
    kmap = lambda col: (lambda b, p, qi, ki: (b, ki[p], col))
    grid_spec = pltpu.PrefetchScalarGridSpec(
        num_scalar_prefetch=2,
        grid=(bsz, qi.shape[0]),
        in_specs=[pl.BlockSpec(memory_space=pltpu.SMEM),
                  pl.BlockSpec((1, ta, DIFF_W), qmap(Z_QD)),
                  pl.BlockSpec((1, ta, DIFF_W), kmap(Z_KD)),
                  pl.BlockSpec((1, DIFF_W, ta), lambda b, p, qi, ki: (b, ZT_VD, ki[p])),
                  pl.BlockSpec(diff_lam.shape, lambda b, p, qi, ki: (0, 0)),
                  pl.BlockSpec((dv, 1), lambda b, p, qi, ki: (0, 0))],
        out_specs=pl.BlockSpec((1, ta, DIFF_W), qmap(0)),
        scratch_shapes=[pltpu.VMEM((n_maps, 1, ta), F32),
                        pltpu.VMEM((n_maps, dv + ONES_ROWS, ta), F32),
                        pltpu.VMEM((DIFF_HEADS, 2, ta, ta), F32)],
    )
    return pl.pallas_call(
        functools.partial(_dattn_body, lambda_init=lambda_init),
        grid_spec=grid_spec,
        out_shape=jax.ShapeDtypeStruct((bsz, seq, DIFF_W), BF16),
        compiler_params=_cparams(("arbitrary", "arbitrary")),
        name="diff_attn",
    )(qi, ki, rel_bias, z, z, zt, diff_lam, diff_norm_g.reshape(dv, 1))


def _out_proj_body(oc_ref, od_ref, x_ref, g1_ref, w_ref, o_ref):
    y = (jnp.dot(oc_ref[0], w_ref[0:C_WIDTH, :], preferred_element_type=F32)
         + jnp.dot(od_ref[0], w_ref[C_WIDTH:C_WIDTH + DIFF_W, :], preferred_element_type=F32))
    o_ref[0] = x_ref[0] + g1_ref[0] * y


def _out_proj(out_c, out_d, x, g1, w_out_bf16, tm):
    bsz, seq, d = x.shape
    return pl.pallas_call(
        _out_proj_body,
        grid=(bsz, seq // tm),
        in_specs=[pl.BlockSpec((1, tm, C_WIDTH), lambda b, i: (b, i, 0)),
                  pl.BlockSpec((1, tm, DIFF_W), lambda b, i: (b, i, 0)),
                  pl.BlockSpec((1, tm, d), lambda b, i: (b, i, 0)),
                  pl.BlockSpec((1, 1, d), lambda b, i: (b, 0, 0)),
                  pl.BlockSpec((C_WIDTH + DIFF_W, d), lambda b, i: (0, 0))],
        out_specs=pl.BlockSpec((1, tm, d), lambda b, i: (b, i, 0)),
        out_shape=jax.ShapeDtypeStruct((bsz, seq, d), F32),
        compiler_params=_cparams(("arbitrary", "arbitrary")),
        name="attn_out_proj",
    )(out_c, out_d, x, g1.reshape(bsz, 1, d), w_out_bf16)


def _attn_in_weights(cd_w_in):
    sizes = (C_WIDTH, C_WIDTH, C_WIDTH, IDX_HEADS * IDX_DIM, IDX_DIM, IDX_HEADS, DIFF_W, DIFF_W, DIFF_W)
    cuts = np.cumsum(sizes)[:-1]
    q_c, k_c, v_c, q_i, k_i, w_i, q_d, k_d, v_d = jnp.split(cd_w_in, cuts, axis=1)
    d = cd_w_in.shape[0]
    w = jnp.concatenate([q_c * (C_HEAD_DIM ** -0.5 * LOG2E), k_c, q_d * (DIFF_HEAD_DIM ** -0.5 * LOG2E), k_d,
                         q_i, k_i, jnp.zeros((d, LANES - IDX_DIM), cd_w_in.dtype)], axis=1)
    wt = jnp.concatenate([v_c, v_d, w_i * (IDX_DIM * IDX_HEADS) ** -0.5,
                          jnp.zeros((d, BF16_ROWS - IDX_HEADS), cd_w_in.dtype)], axis=1).T
    return w.astype(BF16), wt.astype(BF16)


def kernel(x, c, positions, rel_bias, norm_g, final_norm_g, ada_w, ada_b, ab_w_in, ab_conv_a, ab_conv_b,
           ab_conv_b_bias, ab_ln_g, ab_ln_b, ab_w_out, cd_w_in, diff_lam, diff_norm_g, cd_w_out,
           moe_wr_g, moe_br_g, moe_wr_e, moe_br_e, moe_w_gate, moe_w_up, moe_w_down):
    del positions
    bsz, seq, d = x.shape
    depth = ada_w.shape[0]
    tm = min(512, seq)
    ta = min(ATT_TILE, seq)
    assert ta >= MAX_DISTANCE and seq % ta == 0
    topk = min(TOPK_MAX, seq // 4)
    mods = _ada_mod(c, ada_w, ada_b)
    for i in range(depth):
        sh1, sc1, g1, sh2, sc2, g2 = jnp.split(mods[i], 6, axis=-1)
        j = i // 2
        if i % 2 == 0:
            z = _norm_proj(x, norm_g[i, 0], sh1, sc1, ab_w_in[j].astype(BF16), tm)
            x = _conv_mix(z, x, g1, ab_conv_a[j], ab_conv_b[j], ab_conv_b_bias[j], ab_ln_g[j], ab_ln_b[j],
                          ab_w_out[j].astype(BF16), min(256, seq))
        else:
            lambda_init = 0.8 - 0.6 * math.exp(-0.3 * i)
            w, wt = _attn_in_weights(cd_w_in[j])
            z, zt = _norm_proj(x, norm_g[i, 0], sh1, sc1, w, tm, wt)
            mask = _select(z, zt, topk)
            out_c = _cattn(z, zt, mask, rel_bias, ta)
            out_d = _dattn(z, zt, rel_bias, diff_lam[j], diff_norm_g[j], lambda_init, ta)
            x = _out_proj(out_c, out_d, x, g1, cd_w_out[j].astype(BF16), tm)
        x = _hier_moe(x, norm_g[i, 1], sh2, sc2, g2, moe_wr_g[i], moe_br_g[i], moe_wr_e[i], moe_br_e[i],
                      moe_w_gate, moe_w_up, moe_w_down, i, final_norm_g, final_norm=(i == depth - 1))
    return x
```

```python
import functools
import math

import numpy as np
import jax
import jax.numpy as jnp
from jax import lax
from jax.experimental import pallas as pl
from jax.experimental.pallas import tpu as pltpu

F32 = jnp.float32
BF16 = jnp.bfloat16
I32 = jnp.int32
HIGHEST = lax.Precision.HIGHEST

EPS = 1e-6
A_WIDTH = 512
A_CONV = 3
B_WIDTH = 512
B_CONV = 31
C_HEADS = 8
C_HEAD_DIM = 64
IDX_HEADS = 8
IDX_DIM = 32
TOPK_MAX = 256
DIFF_HEADS = 4
DIFF_HEAD_DIM = 64
NUM_BUCKETS = 32
MAX_DISTANCE = 128
N_GROUPS = 4
EXPERTS_PER_GROUP = 8
N_EXPERTS = N_GROUPS * EXPERTS_PER_GROUP
C_WIDTH = C_HEADS * C_HEAD_DIM
DIFF_W = DIFF_HEADS * 2 * DIFF_HEAD_DIM
LANES = 128
BF16_ROWS = 16
INT_MIN = -(2 ** 31)
LOG2E = math.log2(math.e)
VMEM_LIMIT = 56 * 1024 * 1024

Z_KC, Z_KD = 0, 1
Z_KIDX_OFF = C_WIDTH + DIFF_W
Z_COLS = Z_KIDX_OFF + LANES
ZT_VC, ZT_VD, ZT_QC, ZT_QD = 0, 1, 2, 3
ZT_QIDX_OFF = 2 * C_WIDTH + 2 * DIFF_W
ZT_QIDX_ROWS = IDX_HEADS * IDX_DIM
ZT_W_OFF = ZT_QIDX_OFF + ZT_QIDX_ROWS
ZT_ROWS = ZT_W_OFF + BF16_ROWS


def _bucket_starts():
    n = np.arange(0, 2 * MAX_DISTANCE)
    me = NUM_BUCKETS // 2
    lr = np.log(np.maximum(n, 1) / me) / math.log(MAX_DISTANCE / me)
    large = me + (lr * (NUM_BUCKETS - me)).astype(np.int64)
    b = np.where(n < me, n, np.minimum(large, NUM_BUCKETS - 1))
    return [int(n[b >= k].min()) for k in range(NUM_BUCKETS)]


BUCKET_START = _bucket_starts()


def _cparams(sem):
    return pltpu.CompilerParams(dimension_semantics=sem, vmem_limit_bytes=VMEM_LIMIT)


def _rms(x):
    return x * lax.rsqrt(jnp.mean(x * x, axis=-1, keepdims=True) + EPS)


def _sigmoid(x):
    return 1.0 / (1.0 + jnp.exp(-x))


def _dot_nt(a, b):
    return lax.dot_general(a, b, (((1,), (1,)), ((), ())), preferred_element_type=F32)


def _ada_body(c_ref, w_ref, b_ref, o_ref):
    c = c_ref[...]
    cond = c * _sigmoid(c)
    o_ref[0] = jnp.dot(cond, w_ref[0], precision=HIGHEST, preferred_element_type=F32) + b_ref[0]


def _ada_mod(c, ada_w, ada_b):
    depth, d, n6 = ada_w.shape
    bsz = c.shape[0]
    rows = 8
    c_pad = jnp.zeros((rows, d), F32).at[:bsz].set(c)
    tn = 1536
    out = pl.pallas_call(
        _ada_body,
        grid=(depth, n6 // tn),
        in_specs=[pl.BlockSpec((rows, d), lambda i, j: (0, 0)),
                  pl.BlockSpec((1, d, tn), lambda i, j: (i, 0, j)),
                  pl.BlockSpec((1, 1, tn), lambda i, j: (i, 0, j))],
        out_specs=pl.BlockSpec((1, rows, tn), lambda i, j: (i, 0, j)),
        out_shape=jax.ShapeDtypeStruct((depth, rows, n6), F32),
        compiler_params=_cparams(("arbitrary", "arbitrary")),
        name="ada_mod",
    )(c_pad, ada_w, ada_b.reshape(depth, 1, n6))
    return out[:, :bsz]


def _norm_proj_body(x_ref, g_ref, sh_ref, sc_ref, w_ref, *rest):
    y = _rms(x_ref[0]) * g_ref[...]
    h = (y * (1.0 + sc_ref[0]) + sh_ref[0]).astype(BF16)
    if len(rest) == 1:
        (o_ref,) = rest
    else:
        wt_ref, o_ref, ot_ref = rest
        ot_ref[0] = _dot_nt(wt_ref[...], h).astype(ot_ref.dtype)
    o_ref[0] = jnp.dot(h, w_ref[...], preferred_element_type=F32).astype(o_ref.dtype)


def _norm_proj(x, g, sh, sc, w_bf16, tm, wt_bf16=None):
    bsz, seq, d = x.shape
    n = w_bf16.shape[1]
    in_specs = [pl.BlockSpec((1, tm, d), lambda b, i: (b, i, 0)),
                pl.BlockSpec((1, d), lambda b, i: (0, 0)),
                pl.BlockSpec((1, 1, d), lambda b, i: (b, 0, 0)),
                pl.BlockSpec((1, 1, d), lambda b, i: (b, 0, 0)),
                pl.BlockSpec((d, n), lambda b, i: (0, 0))]
    out_specs = pl.BlockSpec((1, tm, n), lambda b, i: (b, i, 0))
    out_shape = jax.ShapeDtypeStruct((bsz, seq, n), BF16)
    args = [x, g.reshape(1, d), sh.reshape(bsz, 1, d), sc.reshape(bsz, 1, d), w_bf16]
    if wt_bf16 is not None:
        nt = wt_bf16.shape[0]
        in_specs.append(pl.BlockSpec((nt, d), lambda b, i: (0, 0)))
        out_specs = [out_specs, pl.BlockSpec((1, nt, tm), lambda b, i: (b, 0, i))]
        out_shape = [out_shape, jax.ShapeDtypeStruct((bsz, nt, seq), BF16)]
        args.append(wt_bf16)
    return pl.pallas_call(
        _norm_proj_body,
        grid=(bsz, seq // tm),
        in_specs=in_specs,
        out_specs=out_specs,
        out_shape=out_shape,
        compiler_params=_cparams(("arbitrary", "arbitrary")),
        name="norm_proj",
    )(*args)


CONV_HALO = 32
CONV_ROWS = 64


def _conv_body(z_ref, x_ref, g1_ref, ca_ref, cb_ref, cbb_ref, lng_ref, lnb_ref, wo_ref, o_ref,
               ua_scr, ub_scr, y_scr, *, tl):
    l = pl.program_id(1)

    @pl.when(l == 0)
    def _():
        ua_scr[0:CONV_HALO, :] = jnp.zeros((CONV_HALO, A_WIDTH), F32)
        ub_scr[0:CONV_HALO, :] = jnp.zeros((CONV_HALO, B_WIDTH), F32)

    a = A_WIDTH
    gate_c = z_ref[0, :, a:2 * a].astype(F32)
    x_a = z_ref[0, :, 2 * a:3 * a].astype(F32)
    ua_scr[CONV_HALO:CONV_HALO + tl, :] = gate_c * x_a
    val_b = z_ref[0, :, 3 * a:3 * a + B_WIDTH].astype(F32)
    glu = z_ref[0, :, 3 * a + B_WIDTH:3 * a + 2 * B_WIDTH].astype(F32)
    ub_scr[CONV_HALO:CONV_HALO + tl, :] = val_b * _sigmoid(glu)

    for r in range(0, tl, CONV_ROWS):
        acc_a = None
        for k in range(A_CONV):
            tap = ua_scr[CONV_HALO + r - (A_CONV - 1) + k:CONV_HALO + r - (A_CONV - 1) + k + CONV_ROWS, :]
            term = tap * ca_ref[k:k + 1, :]
            acc_a = term if acc_a is None else acc_a + term
        gate_b = z_ref[0, r:r + CONV_ROWS, 0:a].astype(F32)
        y_scr[r:r + CONV_ROWS, 0:a] = (gate_b * acc_a).astype(BF16)

        win = CONV_HALO + r - 8
        acc_b = None
        for b in range(8):
            phase = None
            for back in range(b, B_CONV, 8):
                tap = ub_scr[win - (back - b):win - (back - b) + CONV_ROWS + 8, :]
                term = tap * cb_ref[B_CONV - 1 - back:B_CONV - back, :]
                phase = term if phase is None else phase + term
            piece = phase[8 - b:8 - b + CONV_ROWS]
            acc_b = piece if acc_b is None else acc_b + piece
        u = acc_b + cbb_ref[...]
        mu = jnp.mean(u, axis=-1, keepdims=True)
        uc = u - mu
        var = jnp.mean(uc * uc, axis=-1, keepdims=True)
        v = uc * lax.rsqrt(var + EPS) * lng_ref[...] + lnb_ref[...]
        y_scr[r:r + CONV_ROWS, a:a + B_WIDTH] = (v * _sigmoid(v)).astype(BF16)

    ua_scr[0:CONV_HALO, :] = ua_scr[tl:tl + CONV_HALO, :]
    ub_scr[0:CONV_HALO, :] = ub_scr[tl:tl + CONV_HALO, :]
    y = jnp.dot(y_scr[...], wo_ref[...], preferred_element_type=F32)
    o_ref[0] = x_ref[0] + g1_ref[0] * y


def _conv_mix(z, x, g1, conv_a, conv_b, conv_b_bias, ln_g, ln_b, w_out_bf16, tl):
    bsz, seq, d = x.shape
    nz = z.shape[-1]
    wide = A_WIDTH + B_WIDTH
    full = lambda shape: pl.BlockSpec(shape, lambda b, l: (0,) * len(shape))
    return pl.pallas_call(
        functools.partial(_conv_body, tl=tl),
        grid=(bsz, seq // tl),
        in_specs=[pl.BlockSpec((1, tl, nz), lambda b, l: (b, l, 0)),
                  pl.BlockSpec((1, tl, d), lambda b, l: (b, l, 0)),
                  pl.BlockSpec((1, 1, d), lambda b, l: (b, 0, 0)),
                  full((A_CONV, A_WIDTH)), full((B_CONV, B_WIDTH)), full((1, B_WIDTH)),
                  full((1, B_WIDTH)), full((1, B_WIDTH)), full((wide, d))],
        out_specs=pl.BlockSpec((1, tl, d), lambda b, l: (b, l, 0)),
        out_shape=jax.ShapeDtypeStruct((bsz, seq, d), F32),
        scratch_shapes=[pltpu.VMEM((CONV_HALO + tl, A_WIDTH), F32),
                        pltpu.VMEM((CONV_HALO + tl, B_WIDTH), F32),
                        pltpu.VMEM((tl, wide), BF16)],
        compiler_params=_cparams(("arbitrary", "arbitrary")),
        name="conv_mix",
    )(z, x, g1.reshape(bsz, 1, d), conv_a, conv_b, conv_b_bias.reshape(1, -1), ln_g.reshape(1, -1),
      ln_b.reshape(1, -1), w_out_bf16)


MOE_ROWS = 512
ROUTER_ROWS = 512
META_GATE0, META_GATE1, META_POS0, META_POS1 = range(4)
SEG_LEN, SEG_BASE, SEG_OFF = range(3)
SEG_ALIGN = 8
SEG_SIZES = tuple(2 ** b for b in range(10, 2, -1))
GROUP_LANE0 = N_EXPERTS


def _router_body(x_ref, g_ref, sh_ref, sc_ref, wr_ref, br_ref, tri_ref, upper_ref, h_ref, meta_ref, post_ref, seg_ref,
                 cnt_ref, base_scr):
    @pl.when(pl.program_id(0) == 0)
    def _():
        base_scr[...] = jnp.zeros_like(base_scr)

    h = _rms(x_ref[...]) * g_ref[...]
    h = h * (1.0 + sc_ref[0]) + sh_ref[0]
    h_ref[...] = h.astype(h_ref.dtype)
    h_hi = h.astype(BF16)
    h_lo = (h - h_hi.astype(F32)).astype(BF16)
    logits = (jnp.dot(h_hi, wr_ref[0], preferred_element_type=F32) + jnp.dot(h_hi, wr_ref[1], preferred_element_type=F32)
              + jnp.dot(h_lo, wr_ref[0], preferred_element_type=F32) + br_ref[...])
    tr = logits.shape[0]
    lane = lax.broadcasted_iota(I32, (tr, LANES), 1)
    lane_f = lane.astype(F32)
    neg = jnp.float32(-jnp.inf)
    big = jnp.float32(1e9)

    is_group = (lane >= GROUP_LANE0) & (lane < GROUP_LANE0 + N_GROUPS)
    glog = jnp.where(is_group, logits, neg)
    gmax = jnp.max(glog, axis=1, keepdims=True)
    p_top = 1.0 / jnp.sum(jnp.exp(glog - gmax), axis=1, keepdims=True)
    g_sel = jnp.min(jnp.where(glog == gmax, lane_f, big), axis=1, keepdims=True) - GROUP_LANE0
    lo = g_sel * EXPERTS_PER_GROUP
    in_group = (lane_f >= lo) & (lane_f < lo + EXPERTS_PER_GROUP)
    f1 = jnp.where(in_group, logits, neg)
    v1 = jnp.max(f1, axis=1, keepdims=True)
    i1 = jnp.min(jnp.where(f1 == v1, lane_f, big), axis=1, keepdims=True)
    f2 = jnp.where(lane_f == i1, neg, f1)
    v2 = jnp.max(f2, axis=1, keepdims=True)
    i2 = jnp.min(jnp.where(f2 == v2, lane_f, big), axis=1, keepdims=True)
    a = jnp.exp(v2 - v1)
    w1 = 1.0 / (1.0 + a)
    gate0 = p_top * w1
    gate1 = p_top * (a * w1)

    oh0 = lane_f == i1
    oh1 = lane_f == i2
    ind0 = jnp.where(oh0, 1.0, 0.0)
    ind1 = jnp.where(oh1, 1.0, 0.0)
    pre0 = jnp.dot(tri_ref[...], ind0.astype(BF16), preferred_element_type=F32)
    pre1 = jnp.dot(tri_ref[...], ind1.astype(BF16), preferred_element_type=F32)
    tot0 = jnp.sum(ind0, axis=0, keepdims=True)
    tot1 = jnp.sum(ind1, axis=0, keepdims=True)
    seg_len = jnp.floor((tot0 + tot1 + (SEG_ALIGN - 1)) * (1.0 / SEG_ALIGN)) * SEG_ALIGN
    seg_off = jnp.dot(jnp.broadcast_to(seg_len, (8, LANES)), upper_ref[...], precision=HIGHEST,
                      preferred_element_type=F32)[0:1]
    pos0 = jnp.sum(jnp.where(oh0, seg_off + pre0, 0.0), axis=1, keepdims=True)
    pos1 = jnp.sum(jnp.where(oh1, seg_off + tot0 + pre1, 0.0), axis=1, keepdims=True)
    base = base_scr[...]
    new_base = base + seg_len
    base_scr[...] = new_base
    cnt_ref[...] = new_base

    meta = jnp.zeros((tr, LANES), F32)
    for col, val in ((META_GATE0, gate0), (META_GATE1, gate1), (META_POS0, pos0), (META_POS1, pos1)):
        meta = jnp.where(lane == col, val, meta)
    meta_ref[...] = meta
    post_ref[0] = meta.T[0:8].astype(I32)
    srow = lax.broadcasted_iota(I32, (8, LANES), 0)
    seg = jnp.where(srow == SEG_LEN, seg_len, jnp.where(srow == SEG_BASE, base, jnp.where(srow == SEG_OFF, seg_off, 0.0)))
    seg_ref[0] = seg.astype(I32)


def _moe_router(x2, g, sh, sc, wr_g, br_g, wr_e, br_e, seq):
    n_tok, d = x2.shape
    bsz = n_tok // seq
    tr = min(ROUTER_ROWS, seq)
    steps_per_batch = seq // tr
    wr = jnp.zeros((d, LANES), F32).at[:, :N_EXPERTS].set(wr_e).at[:, GROUP_LANE0:GROUP_LANE0 + N_GROUPS].set(wr_g)
    wr_hi = wr.astype(BF16)
    wr = jnp.stack([wr_hi, (wr - wr_hi.astype(F32)).astype(BF16)])
    br = jnp.zeros((1, LANES), F32).at[0, :N_EXPERTS].set(br_e).at[0, GROUP_LANE0:GROUP_LANE0 + N_GROUPS].set(br_g)
    tri = jnp.tril(jnp.ones((tr, tr), BF16), -1)
    upper = jnp.triu(jnp.ones((LANES, LANES), F32), 1)
    nb = n_tok // tr
    full = lambda shape: pl.BlockSpec(shape, lambda i: (0,) * len(shape))
    return pl.pallas_call(
        _router_body,
        grid=(nb,),
        in_specs=[pl.BlockSpec((tr, d), lambda i: (i, 0)),
                  full((1, d)),
                  pl.BlockSpec((1, 1, d), lambda i: (i // steps_per_batch, 0, 0)),
                  pl.BlockSpec((1, 1, d), lambda i: (i // steps_per_batch, 0, 0)),
                  full((2, d, LANES)), full((1, LANES)), full((tr, tr)), full((LANES, LANES))],
        out_specs=[pl.BlockSpec((tr, d), lambda i: (i, 0)),
                   pl.BlockSpec((tr, LANES), lambda i: (i, 0)),
                   pl.BlockSpec((1, 8, tr), lambda i: (i, 0, 0)),
                   pl.BlockSpec((1, 8, LANES), lambda i: (i, 0, 0)),
                   full((1, LANES))],
        out_shape=[jax.ShapeDtypeStruct((n_tok, d), BF16),
                   jax.ShapeDtypeStruct((n_tok, LANES), F32),
                   jax.ShapeDtypeStruct((nb, 8, tr), I32),
                   jax.ShapeDtypeStruct((nb, 8, LANES), I32),
                   jax.ShapeDtypeStruct((1, LANES), F32)],
        scratch_shapes=[pltpu.VMEM((1, LANES), F32)],
        compiler_params=_cparams(("arbitrary",)),
        name="moe_router",
    )(x2, g.reshape(1, d), sh.reshape(bsz, 1, d), sc.reshape(bsz, 1, d), wr, br, tri, upper)


def _segment_copies(seg_ref, starts_ref, make_copy, action):
    def per_expert(e, carry):
        length = seg_ref[0, SEG_LEN, e]
        local = seg_ref[0, SEG_OFF, e]
        glob = starts_ref[e] + seg_ref[0, SEG_BASE, e]
        for size in SEG_SIZES:
            hit = (length & size) != 0

            @pl.when(hit)
            def _(local=local, glob=glob, size=size):
                cp = make_copy(pl.multiple_of(local, SEG_ALIGN), pl.multiple_of(glob, SEG_ALIGN), size)
                cp.start() if action == "start" else cp.wait()

            step = jnp.where(hit, size, 0)
            local = local + step
            glob = glob + step
        return carry

    lax.fori_loop(0, N_EXPERTS, per_expert, 0)


def _local_rows(tr):
    return 2 * tr + N_EXPERTS * SEG_ALIGN


def _dispatch_body(starts_ref, seg_ref, prev_seg_ref, h_ref, post_ref, xr_ref, xs_scr, zero_scr, sems, *, tr):
    i = pl.program_id(0)
    slot = i % 2
    lb = xs_scr.shape[1]
    sem = sems.at[0]

    def make_copy_of(s):
        def make_copy(local, glob, size):
            return pltpu.make_async_copy(xs_scr.at[s, pl.ds(local, size)], xr_ref.at[pl.ds(glob, size)], sems.at[s])
        return make_copy

    rows = lax.broadcasted_iota(I32, (lb, tr), 0)
    place = (rows == post_ref[0, META_POS0:META_POS0 + 1, :]) | (rows == post_ref[0, META_POS1:META_POS1 + 1, :])
    xs_scr[slot] = jnp.dot(jnp.where(place, 1.0, 0.0).astype(BF16), h_ref[...], preferred_element_type=F32)
    _segment_copies(seg_ref, starts_ref, make_copy_of(slot), "start")

    @pl.when(i > 0)
    def _():
        _segment_copies(prev_seg_ref, starts_ref, make_copy_of(1 - slot), "wait")

    @pl.when(i == pl.num_programs(0) - 1)
    def _():
        _segment_copies(seg_ref, starts_ref, make_copy_of(slot), "wait")
        zero_scr[...] = jnp.zeros_like(zero_scr)
        total = starts_ref[N_EXPERTS]
        tail = (-total) & (MOE_ROWS - 1)
        for action in ("start", "wait"):
            row = total
            for size in SEG_SIZES:
                if size >= MOE_ROWS:
                    continue
                hit = (tail & size) != 0

                @pl.when(hit)
                def _(row=row, size=size, action=action):
                    cp = pltpu.make_async_copy(zero_scr.at[pl.ds(0, size)],
                                               xr_ref.at[pl.ds(pl.multiple_of(row, SEG_ALIGN), size)], sem)
                    cp.start() if action == "start" else cp.wait()

                row = row + jnp.where(hit, size, 0)

        def free_block(b):
            return pltpu.make_async_copy(
                zero_scr, xr_ref.at[pl.ds(pl.multiple_of(b * MOE_ROWS, MOE_ROWS), MOE_ROWS)], sem)

        first_free = (total + MOE_ROWS - 1) // MOE_ROWS
        n_blocks = xr_ref.shape[0] // MOE_ROWS
        lax.fori_loop(first_free, n_blocks, lambda b, c: (free_block(b).start(), c)[1], 0)
        lax.fori_loop(first_free, n_blocks, lambda b, c: (free_block(b).wait(), c)[1], 0)


def _moe_dispatch(h2, post, seg, starts, n_rows, tr):
    n_tok, d = h2.shape
    grid_spec = pltpu.PrefetchScalarGridSpec(
        num_scalar_prefetch=1,
        grid=(n_tok // tr,),
        in_specs=[pl.BlockSpec((1, 8, LANES), lambda i, s: (i, 0, 0), memory_space=pltpu.SMEM),
                  pl.BlockSpec((1, 8, LANES), lambda i, s: (jnp.maximum(i - 1, 0), 0, 0), memory_space=pltpu.SMEM),
                  pl.BlockSpec((tr, d), lambda i, s: (i, 0)),
                  pl.BlockSpec((1, 8, tr), lambda i, s: (i, 0, 0))],
        out_specs=pl.BlockSpec(memory_space=pl.ANY),
        scratch_shapes=[pltpu.VMEM((2, _local_rows(tr), d), F32), pltpu.VMEM((MOE_ROWS, d), F32),
                        pltpu.SemaphoreType.DMA((2,))],
    )
    return pl.pallas_call(
        functools.partial(_dispatch_body, tr=tr),
        grid_spec=grid_spec,
        out_shape=jax.ShapeDtypeStruct((n_rows, d), F32),
        compiler_params=_cparams(("arbitrary",)),
        name="moe_dispatch",
    )(starts, seg, seg, h2, post)


def _expert_body(pb_ref, pe_ref, plo_ref, phi_ref, x_ref, wg_ref, wu_ref, wd_ref, o_ref, wg_s, wu_s, wd_s):
    p = pl.program_id(0)
    prev = jnp.maximum(p - 1, 0)
    new_expert = (p == 0) | (pe_ref[p] != pe_ref[prev])
    first = (p == 0) | (pb_ref[p] != pb_ref[prev])

    @pl.when(new_expert)
    def _():
        wg_s[...] = wg_ref[0, 0].astype(BF16)
        wu_s[...] = wu_ref[0, 0].astype(BF16)
        wd_s[...] = wd_ref[0, 0].astype(BF16)

    def rows_of_expert():
        x = x_ref[...].astype(BF16)
        gt = jnp.dot(x, wg_s[...], preferred_element_type=F32)
        up = jnp.dot(x, wu_s[...], preferred_element_type=F32)
        act = (gt * _sigmoid(gt)) * up
        y = jnp.dot(act.astype(BF16), wd_s[...], preferred_element_type=F32)
        rows = lax.broadcasted_iota(I32, (y.shape[0], 1), 0)
        return jnp.where((rows >= plo_ref[p]) & (rows < phi_ref[p]), y, 0.0)

    nonempty = phi_ref[p] > plo_ref[p]

    @pl.when(first & nonempty)
    def _():
        o_ref[...] = rows_of_expert()

    @pl.when(first & jnp.logical_not(nonempty))
    def _():
        o_ref[...] = jnp.zeros_like(o_ref)

    @pl.when(jnp.logical_not(first) & nonempty)
    def _():
        o_ref[...] += rows_of_expert()


def _moe_experts(x_rows, pairs, w_gate, w_up, w_down, layer):
    n_rows, d = x_rows.shape
    de = w_gate.shape[-1]
    n_pairs = pairs[0].shape[0]
    grid_spec = pltpu.PrefetchScalarGridSpec(
        num_scalar_prefetch=4,
        grid=(n_pairs,),
        in_specs=[pl.BlockSpec((MOE_ROWS, d), lambda p, pb, pe, lo, hi: (pb[p], 0)),
                  pl.BlockSpec((1, 1, d, de), lambda p, pb, pe, lo, hi: (layer, pe[p], 0, 0)),
                  pl.BlockSpec((1, 1, d, de), lambda p, pb, pe, lo, hi: (layer, pe[p], 0, 0)),
                  pl.BlockSpec((1, 1, de, d), lambda p, pb, pe, lo, hi: (layer, pe[p], 0, 0))],
        out_specs=pl.BlockSpec((MOE_ROWS, d), lambda p, pb, pe, lo, hi: (pb[p], 0)),
        scratch_shapes=[pltpu.VMEM((d, de), BF16), pltpu.VMEM((d, de), BF16), pltpu.VMEM((de, d), BF16)],
    )
    return pl.pallas_call(
        _expert_body,
        grid_spec=grid_spec,
        out_shape=jax.ShapeDtypeStruct((n_rows, d), F32),
        compiler_params=_cparams(("arbitrary",)),
        name="moe_experts",
    )(*pairs, x_rows, w_gate, w_up, w_down)


def _combine_body(starts_ref, seg_ref, next_seg_ref, y_ref, x_ref, meta_ref, g2_ref, fg_ref, o_ref, ys_scr, sems, *,
                  final_norm):
    i = pl.program_id(0)
    slot = i % 2

    def make_copy_of(s):
        def make_copy(local, glob, size):
            return pltpu.make_async_copy(y_ref.at[pl.ds(glob, size)], ys_scr.at[s, pl.ds(local, size)], sems.at[s])
        return make_copy

    @pl.when(i == 0)
    def _():
        ys_scr[...] = jnp.zeros_like(ys_scr)
        _segment_copies(seg_ref, starts_ref, make_copy_of(slot), "start")

    @pl.when(i + 1 < pl.num_programs(0))
    def _():
        _segment_copies(next_seg_ref, starts_ref, make_copy_of(1 - slot), "start")

    _segment_copies(seg_ref, starts_ref, make_copy_of(slot), "wait")

    meta = meta_ref[...]
    tr = meta.shape[0]
    y = ys_scr[slot].astype(BF16)
    cols = lax.broadcasted_iota(I32, (tr, ys_scr.shape[1]), 1)

    def picked(col):
        pos = meta[:, col:col + 1].astype(I32)
        return jnp.dot(jnp.where(cols == pos, 1.0, 0.0).astype(BF16), y, preferred_element_type=F32)

    moe = meta[:, META_GATE0:META_GATE0 + 1] * picked(META_POS0) + meta[:, META_GATE1:META_GATE1 + 1] * picked(META_POS1)
    xn = x_ref[...] + g2_ref[0] * moe
    if final_norm:
        xn = _rms(xn) * fg_ref[...]
    o_ref[...] = xn


def _moe_combine(y_rows, seg, starts, x2, meta, g2, final_g, seq, tr, final_norm):
    n_tok, d = x2.shape
    bsz = n_tok // seq
    steps_per_batch = seq // tr
    nb = n_tok // tr
    grid_spec = pltpu.PrefetchScalarGridSpec(
        num_scalar_prefetch=1,
        grid=(nb,),
        in_specs=[pl.BlockSpec((1, 8, LANES), lambda i, s: (i, 0, 0), memory_space=pltpu.SMEM),
                  pl.BlockSpec((1, 8, LANES), lambda i, s: (jnp.minimum(i + 1, nb - 1), 0, 0), memory_space=pltpu.SMEM),
                  pl.BlockSpec(memory_space=pl.ANY),
                  pl.BlockSpec((tr, d), lambda i, s: (i, 0)),
                  pl.BlockSpec((tr, LANES), lambda i, s: (i, 0)),
                  pl.BlockSpec((1, 1, d), lambda i, s: (i // steps_per_batch, 0, 0)),
                  pl.BlockSpec((1, d), lambda i, s: (0, 0))],
        out_specs=pl.BlockSpec((tr, d), lambda i, s: (i, 0)),
        scratch_shapes=[pltpu.VMEM((2, _local_rows(tr), d), F32), pltpu.SemaphoreType.DMA((2,))],
    )
    return pl.pallas_call(
        functools.partial(_combine_body, final_norm=final_norm),
        grid_spec=grid_spec,
        out_shape=jax.ShapeDtypeStruct((n_tok, d), F32),
        compiler_params=_cparams(("arbitrary",)),
        name="moe_combine",
    )(starts, seg, seg, y_rows, x2, meta, g2.reshape(bsz, 1, d), final_g.reshape(1, d))


def _expert_pairs(counts, n_rows):
    n_blocks = n_rows // MOE_ROWS
    n_pairs = n_blocks + N_EXPERTS
    ends = jnp.cumsum(counts)
    starts = ends - counts
    first_blk = starts // MOE_ROWS
    last_blk = (ends - 1) // MOE_ROWS
    npairs = jnp.where(counts > 0, last_blk - first_blk + 1, 0)
    pend = jnp.cumsum(npairs)
    poff = pend - npairs
    total = pend[-1]
    used_blocks = (ends[-1] + MOE_ROWS - 1) // MOE_ROWS
    p = jnp.arange(n_pairs, dtype=I32)
    p_eff = jnp.minimum(p, total - 1)
    e = jnp.minimum(jnp.sum(pend[None, :] <= p_eff[:, None], axis=1), N_EXPERTS - 1).astype(I32)
    table = jnp.stack([first_blk, poff, starts, ends], axis=1).astype(F32)
    onehot = (e[:, None] == jnp.arange(N_EXPERTS, dtype=I32)[None, :]).astype(F32)
    first_e, poff_e, start_e, end_e = jnp.dot(onehot, table, precision=HIGHEST).astype(I32).T
    valid = p < total
    blk = jnp.where(valid, first_e + p_eff - poff_e, jnp.minimum(used_blocks + p - total, n_blocks - 1)).astype(I32)
    lo = jnp.where(valid, jnp.clip(start_e - blk * MOE_ROWS, 0, MOE_ROWS), 0).astype(I32)
    hi = jnp.where(valid, jnp.clip(end_e - blk * MOE_ROWS, 0, MOE_ROWS), 0).astype(I32)
    return blk, e, lo, hi


def _hier_moe(x, g, sh, sc, g2, wr_g, br_g, wr_e, br_e, w_gate, w_up, w_down, layer, final_g, final_norm):
    bsz, seq, d = x.shape
    n_tok = bsz * seq
    tr = min(ROUTER_ROWS, seq)
    assert 2 * tr <= SEG_SIZES[0]
    x2 = x.reshape(n_tok, d)
    h2, meta, post, seg, cnt = _moe_router(x2, g, sh, sc, wr_g, br_g, wr_e, br_e, seq)
    counts = cnt[0, :N_EXPERTS].astype(I32)
    ends = jnp.cumsum(counts)
    starts = jnp.concatenate([ends - counts, ends[-1:]])
    n_rows = -(-(2 * n_tok + (n_tok // tr) * N_EXPERTS * (SEG_ALIGN - 1)) // MOE_ROWS) * MOE_ROWS
    x_rows = _moe_dispatch(h2, post, seg, starts, n_rows, tr)
    y_rows = _moe_experts(x_rows, _expert_pairs(counts, n_rows), w_gate, w_up, w_down, layer)
    out = _moe_combine(y_rows, seg, starts, x2, meta, g2, final_g, seq, tr, final_norm)
    return out.reshape(bsz, seq, d)


SEL_COLS = 256
SEL_CHUNK = 512
SEL_FOLD = 32
BIT_GROUP = 256
SEL_SWEEP = 64
ATT_TILE = 512


def _idx_score_t(k, qt, wt):
    acc = None
    for h in range(IDX_HEADS):
        rel = jnp.dot(k, qt[h * IDX_DIM:(h + 1) * IDX_DIM], preferred_element_type=F32)
        term = jnp.maximum(rel, 0.0) * wt[h:h + 1, :]
        acc = term if acc is None else acc + term
    return acc


def _bit_transpose32(a):
    a = list(a)
    m, j = 0x0000FFFF, 16
    while j:
        k = 0
        while k < 32:
            t = (a[k] ^ lax.shift_right_logical(a[k + j], jnp.int32(j))) & jnp.int32(m)
            a[k] = a[k] ^ t
            a[k + j] = a[k + j] ^ (t << j)
            k = (k + j + 1) & ~j
        j >>= 1
        m = (m ^ (m << j)) & 0xFFFFFFFF
    return a


def _select_body(qt_ref, zk_ref, wt_ref, o_ref, keys_scr, planes_scr, eq_scr, gt_scr, *, topk):
    i = pl.program_id(1)
    tq = qt_ref.shape[2]
    qt = qt_ref[0]
    wt = wt_ref[0].astype(F32)
    n_chunks = ((i + 1) * tq + SEL_CHUNK - 1) // SEL_CHUNK
    krow0 = lax.broadcasted_iota(I32, (SEL_CHUNK, tq), 0)
    qcol = i * tq + lax.broadcasted_iota(I32, (SEL_CHUNK, tq), 1)
    int_min = jnp.int32(INT_MIN)

    groups = SEL_CHUNK // BIT_GROUP
    words = SEL_CHUNK // 32
    sweep_rows = min(SEL_SWEEP, eq_scr.shape[0])

    def fill(masked, c, carry):
        off = pl.multiple_of(c * SEL_CHUNK, SEL_CHUNK)
        k = zk_ref[0, pl.ds(off, SEL_CHUNK), :][:, 0:IDX_DIM]
        bits = lax.bitcast_convert_type(_idx_score_t(k, qt, wt), I32)
        key = bits ^ ((bits >> 31) | int_min)
        if masked:
            key = jnp.where(krow0 + off <= qcol, key, 0)
        keys_scr[pl.ds(off, SEL_CHUNK), :] = key
        for g in range(groups):
            for lt in range(0, tq, LANES):
                ku = key[g * BIT_GROUP:(g + 1) * BIT_GROUP, lt:lt + LANES]
                planes = _bit_transpose32([ku[8 * r:8 * r + 8] for r in range(32)])
                wrow = pl.multiple_of(c * words + g * 8, 8)
                for b in range(32):
                    planes_scr[b, pl.ds(wrow, 8), lt:lt + LANES] = planes[b]
        return carry

    n_plain = (i * tq + 1) // SEL_CHUNK
    lax.fori_loop(0, n_plain, functools.partial(fill, False), 0)
    lax.fori_loop(n_plain, n_chunks, functools.partial(fill, True), 0)

    n_sweep = (n_chunks * words + sweep_rows - 1) // sweep_rows

    def pad(c, carry):
        wrow = pl.multiple_of(c * words, words)
        for b in range(32):
            planes_scr[b, pl.ds(wrow, words), :] = jnp.zeros((words, tq), I32)
        return carry

    lax.fori_loop(n_chunks, n_sweep * (sweep_rows // words), pad, 0)

    def sweep(upd, cnt_plane):
        def body(sb, acc):
            r0 = pl.multiple_of(sb * sweep_rows, sweep_rows)
            eq = eq_scr[pl.ds(r0, sweep_rows), :]
            gt = gt_scr[pl.ds(r0, sweep_rows), :]
            if upd is not None:
                plane, accept = upd
                hit = eq & planes_scr[plane, pl.ds(r0, sweep_rows), :]
                gt = jnp.where(accept, gt, gt | hit)
                eq = jnp.where(accept, hit, eq ^ hit)
                eq_scr[pl.ds(r0, sweep_rows), :] = eq
                gt_scr[pl.ds(r0, sweep_rows), :] = gt
            if cnt_plane is None:
                return acc
            ones = lax.population_count(gt | (eq & planes_scr[cnt_plane, pl.ds(r0, sweep_rows), :]))
            return acc + jnp.sum(ones.reshape(sweep_rows // 8, 8, tq), axis=0)
        acc = lax.fori_loop(0, n_sweep, body, jnp.zeros((8, tq), I32))
        return jnp.sum(acc.astype(F32), axis=0, keepdims=True)

    k_f = jnp.float32(topk)
    eq_scr[...] = jnp.full(eq_scr.shape, -1, I32)
    gt_scr[...] = jnp.zeros(gt_scr.shape, I32)

    def bit_step(ib, carry):
        u, cnt = carry
        accept = cnt >= k_f
        u = jnp.where(accept, u | jnp.left_shift(jnp.int32(1), 32 - ib), u)
        return u, sweep((ib - 1, accept), ib)

    u, cnt = lax.fori_loop(1, 32, bit_step, (jnp.zeros((1, tq), I32), sweep(None, 0)))
    accept = cnt >= k_f
    u = jnp.where(accept, u | 1, u)
    sweep((31, accept), None)
    some = jnp.where(u != 0, -1, 0)

    def emit(c, carry):
        off = pl.multiple_of(c * SEL_CHUNK, SEL_CHUNK)
        for g in range(groups):
            wrow = pl.multiple_of(c * words + g * 8, 8)
            keep = gt_scr[pl.ds(wrow, 8), :] | (eq_scr[pl.ds(wrow, 8), :] & some)
            rows = [jnp.where((lax.shift_right_logical(keep, 31 - r) & 1) != 0, 0.0, -jnp.inf) for r in range(32)]
            o_ref[0, pl.ds(pl.multiple_of(off + g * BIT_GROUP, BIT_GROUP), BIT_GROUP), :] = (
                jnp.concatenate(rows, axis=0).astype(o_ref.dtype))
        return carry

    lax.fori_loop(0, n_chunks, emit, 0)

    def popcount_rows(ref):
        def body(sb, acc):
            r0 = pl.multiple_of(sb * sweep_rows, sweep_rows)
            ones = lax.population_count(ref[pl.ds(r0, sweep_rows), :])
            return acc + jnp.sum(ones.reshape(sweep_rows // 8, 8, tq), axis=0)
        acc = lax.fori_loop(0, n_sweep, body, jnp.zeros((8, tq), I32))
        return jnp.sum(acc.astype(F32), axis=0, keepdims=True)

    n_gt = popcount_rows(gt_scr)
    n_eq = jnp.where(u != 0, popcount_rows(eq_scr), 0.0)

    @pl.when(jnp.max(n_gt + n_eq) > k_f)
    def _():
        thr = u

        def count(pred):
            def body(c, acc):
                off = pl.multiple_of(c * SEL_CHUNK, SEL_CHUNK)
                ind = jnp.where(pred(keys_scr[pl.ds(off, SEL_CHUNK), :], krow0 + off), 1, 0)
                return acc + jnp.sum(ind.reshape(SEL_CHUNK // SEL_FOLD, SEL_FOLD, tq), axis=0)
            acc = lax.fori_loop(0, n_chunks, body, jnp.zeros((SEL_FOLD, tq), I32))
            return jnp.sum(acc.astype(F32), axis=0, keepdims=True)

        need = k_f - n_gt

        def cap_step(ib, v):
            cand = v | jnp.left_shift(jnp.int32(1), 14 - ib)
            cnt = count(lambda keys, kidx: (keys == thr) & (kidx < cand))
            return jnp.where(cnt <= need, cand, v)

        cap = lax.fori_loop(0, 15, cap_step, jnp.zeros((1, tq), I32))
        cap = jnp.where((u != 0) & (n_gt + n_eq > k_f), cap, 2 ** 30)

        def drop(c, carry):
            off = pl.multiple_of(c * SEL_CHUNK, SEL_CHUNK)
            past = (keys_scr[pl.ds(off, SEL_CHUNK), :] == thr) & (krow0 + off >= cap)
            cur = o_ref[0, pl.ds(off, SEL_CHUNK), :].astype(F32)
            o_ref[0, pl.ds(off, SEL_CHUNK), :] = jnp.where(past, -jnp.inf, cur).astype(o_ref.dtype)
            return carry

        lax.fori_loop(0, n_chunks, drop, 0)

    def blank(c, carry):
        off = pl.multiple_of(c * SEL_CHUNK, SEL_CHUNK)
        o_ref[0, pl.ds(off, SEL_CHUNK), :] = jnp.full((SEL_CHUNK, tq), -jnp.inf, o_ref.dtype)
        return carry

    lax.fori_loop(n_chunks, o_ref.shape[1] // SEL_CHUNK, blank, 0)


def _select(z, zt, topk):
    bsz, seq, _ = z.shape
    tq = min(SEL_COLS, seq)
    assert seq % SEL_CHUNK == 0 and seq % tq == 0
    return pl.pallas_call(
        functools.partial(_select_body, topk=topk),
        grid=(bsz, seq // tq),
        in_specs=[pl.BlockSpec((1, ZT_QIDX_ROWS, tq), lambda b, i: (b, ZT_QIDX_OFF // ZT_QIDX_ROWS, i)),
                  pl.BlockSpec((1, seq, LANES), lambda b, i: (b, 0, Z_KIDX_OFF // LANES)),
                  pl.BlockSpec((1, BF16_ROWS, tq), lambda b, i: (b, ZT_W_OFF // BF16_ROWS, i))],
        out_specs=pl.BlockSpec((1, seq, tq), lambda b, i: (b, 0, i)),
        out_shape=jax.ShapeDtypeStruct((bsz, seq, seq), BF16),
        scratch_shapes=[pltpu.VMEM((seq, tq), I32), pltpu.VMEM((32, seq // 32, tq), I32),
                        pltpu.VMEM((seq // 32, tq), I32), pltpu.VMEM((seq // 32, tq), I32)],
        compiler_params=_cparams(("arbitrary", "arbitrary")),
        name="dsa_select",
    )(zt, z, zt)


BIAS_INIT_ROWS = 8
ONES_ROWS = BF16_ROWS


def _init_bias_tiles(btile, bias_ref, head0, n_heads, ta):
    col = lax.broadcasted_iota(I32, (BIAS_INIT_ROWS, ta), 1)
    row0 = lax.broadcasted_iota(I32, (BIAS_INIT_ROWS, ta), 0)

    def body(r, carry):
        off = pl.multiple_of(r * BIAS_INIT_ROWS, BIAS_INIT_ROWS)
        for kind in range(2):
            dist = col - (row0 + off) + kind * ta
            for h in range(n_heads):
                far = bias_ref[NUM_BUCKETS - 1, head0 + h]
                val = jnp.full((BIAS_INIT_ROWS, ta), (bias_ref[0, head0 + h] - far) * LOG2E, F32)
                for b in range(1, NUM_BUCKETS - 1):
                    val = jnp.where(dist >= BUCKET_START[b], (bias_ref[b, head0 + h] - far) * LOG2E, val)
                val = jnp.where(dist >= BUCKET_START[NUM_BUCKETS - 1], 0.0, val)
                btile[h, kind, pl.ds(off, BIAS_INIT_ROWS), :] = val
        return carry

    lax.fori_loop(0, ta // BIAS_INIT_ROWS, body, 0)


def _with_ones(vt):
    return jnp.concatenate([vt, jnp.ones((ONES_ROWS, vt.shape[1]), vt.dtype)], axis=0)


def _softmax_step_t(logits, v_aug, m_ref, acc_ref, idx):
    m_old = m_ref[idx]
    m_new = jnp.maximum(m_old, jnp.max(logits, axis=0, keepdims=True))
    m_safe = jnp.where(m_new == -jnp.inf, 0.0, m_new)
    p = jnp.exp2(logits - m_safe)
    alpha = jnp.exp2(m_old - m_safe)
    acc_ref[idx] = alpha * acc_ref[idx] + jnp.dot(v_aug, p.astype(BF16), preferred_element_type=F32)
    m_ref[idx] = m_new


def _reset_softmax(m_s, acc_s):
    m_s[...] = jnp.full(m_s.shape, -jnp.inf, F32)
    acc_s[...] = jnp.zeros(acc_s.shape, F32)


def _cattn_body(qi_ref, ki_ref, bias_ref, qt_ref, zk_ref, vt_ref, mask_ref, o_ref, m_s, acc_s, btile):
    b = pl.program_id(0)
    p = pl.program_id(1)
    qi = qi_ref[p]
    ki = ki_ref[p]
    ta = zk_ref.shape[1]
    hd = C_HEAD_DIM

    @pl.when((b == 0) & (p == 0))
    def _():
        _init_bias_tiles(btile, bias_ref, 0, C_HEADS, ta)

    @pl.when(ki == 0)
    def _():
        _reset_softmax(m_s, acc_s)

    def heads(extra_of_head):
        for h in range(C_HEADS):
            lo = h * hd
            logits = (jnp.dot(zk_ref[0, :, lo:lo + hd], qt_ref[0, lo:lo + hd, :], preferred_element_type=F32)
                      + extra_of_head(h))
            _softmax_step_t(logits, _with_ones(vt_ref[0, lo:lo + hd, :]), m_s, acc_s, h)

    near = ki >= qi - 1

    @pl.when(near)
    def _():
        heads(lambda h: mask_ref[0].astype(F32) + btile[h, qi - ki])

    @pl.when(jnp.logical_not(near))
    def _():
        heads(lambda h: mask_ref[0].astype(F32))

    @pl.when(ki == qi)
    def _():
        outs = []
        for h in range(C_HEADS):
            a = acc_s[h]
            outs.append((a[0:hd] / a[hd:hd + 1]).T)
        o_ref[0] = jnp.concatenate(outs, axis=1).astype(o_ref.dtype)


def _dattn_body(qi_ref, ki_ref, bias_ref, qt_ref, zk_ref, vt_ref, lam_ref, ng_ref, o_ref,
                m_s, acc_s, btile, *, lambda_init):
    b = pl.program_id(0)
    p = pl.program_id(1)
    qi = qi_ref[p]
    ki = ki_ref[p]
    ta = zk_ref.shape[1]
    hd = DIFF_HEAD_DIM
    dv = 2 * hd

    @pl.when((b == 0) & (p == 0))
    def _():
        _init_bias_tiles(btile, bias_ref, C_HEADS, DIFF_HEADS, ta)

    @pl.when(ki == 0)
    def _():
        _reset_softmax(m_s, acc_s)

    def heads(extra_of_head):
        for h in range(DIFF_HEADS):
            v_aug = _with_ones(vt_ref[0, dv * h:dv * (h + 1), :])
            extra = extra_of_head(h)
            for j in range(2):
                lo = (2 * h + j) * hd
                logits = jnp.dot(zk_ref[0, :, lo:lo + hd], qt_ref[0, lo:lo + hd, :], preferred_element_type=F32)
                if extra is not None:
                    logits = logits + extra
                _softmax_step_t(logits, v_aug, m_s, acc_s, 2 * h + j)

    near = ki >= qi - 1

    @pl.when(near)
    def _():
        krow = ki * ta + lax.broadcasted_iota(I32, (ta, ta), 0)
        qcol = qi * ta + lax.broadcasted_iota(I32, (ta, ta), 1)
        heads(lambda h: jnp.where(krow <= qcol, btile[h, qi - ki], -jnp.inf))

    @pl.when(jnp.logical_not(near))
    def _():
        heads(lambda h: None)

    @pl.when(ki == qi)
    def _():
        lam_p = lam_ref[...]
        lam = (jnp.exp(jnp.sum(lam_p[0:1] * lam_p[1:2], axis=1, keepdims=True))
               - jnp.exp(jnp.sum(lam_p[2:3] * lam_p[3:4], axis=1, keepdims=True)) + lambda_init)
        outs = []
        for h in range(DIFF_HEADS):
            a1 = acc_s[2 * h]
            a2 = acc_s[2 * h + 1]
            o = a1[0:dv] / a1[dv:dv + 1] - lam * (a2[0:dv] / a2[dv:dv + 1])
            o = o * lax.rsqrt(jnp.mean(o * o, axis=0, keepdims=True) + EPS) * ng_ref[...] * (1.0 - lambda_init)
            outs.append(o.T)
        o_ref[0] = jnp.concatenate(outs, axis=1).astype(o_ref.dtype)


def _causal_pairs(nq):
    qi = [q for q in range(nq) for _ in range(q + 1)]
    ki = [k for q in range(nq) for k in range(q + 1)]
    return jnp.asarray(qi, I32), jnp.asarray(ki, I32)


def _cattn(z, zt, mask, rel_bias, ta):
    bsz, seq, _ = z.shape
    qi, ki = _causal_pairs(seq // ta)
    qmap = lambda col: (lambda b, p, qi, ki: (b, qi[p], col))
    kmap = lambda col: (lambda b, p, qi, ki: (b, ki[p], col))
    dv_aug = C_HEAD_DIM + ONES_ROWS
    grid_spec = pltpu.PrefetchScalarGridSpec(
        num_scalar_prefetch=2,
        grid=(bsz, qi.shape[0]),
        in_specs=[pl.BlockSpec(memory_space=pltpu.SMEM),
                  pl.BlockSpec((1, C_WIDTH, ta), lambda b, p, qi, ki: (b, ZT_QC, qi[p])),
                  pl.BlockSpec((1, ta, C_WIDTH), kmap(Z_KC)),
                  pl.BlockSpec((1, C_WIDTH, ta), lambda b, p, qi, ki: (b, ZT_VC, ki[p])),
                  pl.BlockSpec((1, ta, ta), lambda b, p, qi, ki: (b, ki[p], qi[p]))],
        out_specs=pl.BlockSpec((1, ta, C_WIDTH), qmap(0)),
        scratch_shapes=[pltpu.VMEM((C_HEADS, 1, ta), F32),
                        pltpu.VMEM((C_HEADS, dv_aug, ta), F32),
                        pltpu.VMEM((C_HEADS, 2, ta, ta), F32)],
    )
    return pl.pallas_call(
        _cattn_body,
        grid_spec=grid_spec,
        out_shape=jax.ShapeDtypeStruct((bsz, seq, C_WIDTH), BF16),
        compiler_params=_cparams(("arbitrary", "arbitrary")),
        name="dsa_attn",
    )(qi, ki, rel_bias, zt, z, zt, mask)


def _dattn(z, zt, rel_bias, diff_lam, diff_norm_g, lambda_init, ta):
    bsz, seq, _ = z.shape
    qi, ki = _causal_pairs(seq // ta)
    n_maps = 2 * DIFF_HEADS
    dv = 2 * DIFF_HEAD_DIM
    qmap = lambda col: (lambda b, p, qi, ki: (b, qi[p], col))
    kmap = lambda col: (lambda b, p, qi, ki: (b, ki[p], col))
    grid_spec = pltpu.PrefetchScalarGridSpec(
        num_scalar_prefetch=2,
        grid=(bsz, qi.shape[0]),
        in_specs=[pl.BlockSpec(memory_space=pltpu.SMEM),
                  pl.BlockSpec((1, DIFF_W, ta), lambda b, p, qi, ki: (b, ZT_QD, qi[p])),
                  pl.BlockSpec((1, ta, DIFF_W), kmap(Z_KD)),
                  pl.BlockSpec((1, DIFF_W, ta), lambda b, p, qi, ki: (b, ZT_VD, ki[p])),
                  pl.BlockSpec(diff_lam.shape, lambda b, p, qi, ki: (0, 0)),
                  pl.BlockSpec((dv, 1), lambda b, p, qi, ki: (0, 0))],
        out_specs=pl.BlockSpec((1, ta, DIFF_W), qmap(0)),
        scratch_shapes=[pltpu.VMEM((n_maps, 1, ta), F32),
                        pltpu.VMEM((n_maps, dv + ONES_ROWS, ta), F32),
                        pltpu.VMEM((DIFF_HEADS, 2, ta, ta), F32)],
    )
    return pl.pallas_call(
        functools.partial(_dattn_body, lambda_init=lambda_init),
        grid_spec=grid_spec,
        out_shape=jax.ShapeDtypeStruct((bsz, seq, DIFF_W), BF16),
        compiler_params=_cparams(("arbitrary", "arbitrary")),
        name="diff_attn",
    )(qi, ki, rel_bias, zt, z, zt, diff_lam, diff_norm_g.reshape(dv, 1))


def _out_proj_body(oc_ref, od_ref, x_ref, g1_ref, w_ref, o_ref):
    y = (jnp.dot(oc_ref[0], w_ref[0:C_WIDTH, :], preferred_element_type=F32)
         + jnp.dot(od_ref[0], w_ref[C_WIDTH:C_WIDTH + DIFF_W, :], preferred_element_type=F32))
    o_ref[0] = x_ref[0] + g1_ref[0] * y


def _out_proj(out_c, out_d, x, g1, w_out_bf16, tm):
    bsz, seq, d = x.shape
    return pl.pallas_call(
        _out_proj_body,
        grid=(bsz, seq // tm),
        in_specs=[pl.BlockSpec((1, tm, C_WIDTH), lambda b, i: (b, i, 0)),
                  pl.BlockSpec((1, tm, DIFF_W), lambda b, i: (b, i, 0)),
                  pl.BlockSpec((1, tm, d), lambda b, i: (b, i, 0)),
                  pl.BlockSpec((1, 1, d), lambda b, i: (b, 0, 0)),
                  pl.BlockSpec((C_WIDTH + DIFF_W, d), lambda b, i: (0, 0))],
        out_specs=pl.BlockSpec((1, tm, d), lambda b, i: (b, i, 0)),
        out_shape=jax.ShapeDtypeStruct((bsz, seq, d), F32),
        compiler_params=_cparams(("arbitrary", "arbitrary")),
        name="attn_out_proj",
    )(out_c, out_d, x, g1.reshape(bsz, 1, d), w_out_bf16)


def _attn_in_weights(cd_w_in):
    sizes = (C_WIDTH, C_WIDTH, C_WIDTH, IDX_HEADS * IDX_DIM, IDX_DIM, IDX_HEADS, DIFF_W, DIFF_W, DIFF_W)
    cuts = np.cumsum(sizes)[:-1]
    q_c, k_c, v_c, q_i, k_i, w_i, q_d, k_d, v_d = jnp.split(cd_w_in, cuts, axis=1)
    d = cd_w_in.shape[0]
    w = jnp.concatenate([k_c, k_d, k_i, jnp.zeros((d, LANES - IDX_DIM), cd_w_in.dtype)], axis=1)
    wt = jnp.concatenate([v_c, v_d, q_c * (C_HEAD_DIM ** -0.5 * LOG2E), q_d * (DIFF_HEAD_DIM ** -0.5 * LOG2E), q_i,
                          w_i * (IDX_DIM * IDX_HEADS) ** -0.5,
                          jnp.zeros((d, BF16_ROWS - IDX_HEADS), cd_w_in.dtype)], axis=1).T
    return w.astype(BF16), wt.astype(BF16)


def kernel(x, c, positions, rel_bias, norm_g, final_norm_g, ada_w, ada_b, ab_w_in, ab_conv_a, ab_conv_b,
           ab_conv_b_bias, ab_ln_g, ab_ln_b, ab_w_out, cd_w_in, diff_lam, diff_norm_g, cd_w_out,
           moe_wr_g, moe_br_g, moe_wr_e, moe_br_e, moe_w_gate, moe_w_up, moe_w_down):
    del positions
    bsz, seq, d = x.shape
    depth = ada_w.shape[0]
    tm = min(512, seq)
    ta = min(ATT_TILE, seq)
    assert ta >= MAX_DISTANCE and seq % ta == 0
    topk = min(TOPK_MAX, seq // 4)
    mods = _ada_mod(c, ada_w, ada_b)
    for i in range(depth):
        sh1, sc1, g1, sh2, sc2, g2 = jnp.split(mods[i], 6, axis=-1)
        j = i // 2
        if i % 2 == 0:
            z = _norm_proj(x, norm_g[i, 0], sh1, sc1, ab_w_in[j].astype(BF16), tm)
            x = _conv_mix(z, x, g1, ab_conv_a[j], ab_conv_b[j], ab_conv_b_bias[j], ab_ln_g[j], ab_ln_b[j],
                          ab_w_out[j].astype(BF16), min(256, seq))
        else:
            lambda_init = 0.8 - 0.6 * math.exp(-0.3 * i)
            w, wt = _attn_in_weights(cd_w_in[j])
            z, zt = _norm_proj(x, norm_g[i, 0], sh1, sc1, w, tm, wt)
            mask = _select(z, zt, topk)
            out_c = _cattn(z, zt, mask, rel_bias, ta)
            out_d = _dattn(z, zt, rel_bias, diff_lam[j], diff_norm_g[j], lambda_init, ta)
            x = _out_proj(out_c, out_d, x, g1, cd_w_out[j].astype(BF16), tm)
        x = _hier_moe(x, norm_g[i, 1], sh2, sc2, g2, moe_wr_g[i], moe_br_g[i], moe_wr_e[i], moe_br_e[i],
                      moe_w_gate, moe_w_up, moe_w_down, i, final_norm_g, final_norm=(i == depth - 1))
    return x
```

```python
import functools
import math

import numpy as np
import jax
import jax.numpy as jnp
from jax import lax
from jax.experimental import pallas as pl
from jax.experimental.pallas import tpu as pltpu

F32 = jnp.float32
BF16 = jnp.bfloat16
I32 = jnp.int32
HIGHEST = lax.Precision.HIGHEST

EPS = 1e-6
A_WIDTH = 512
A_CONV = 3
B_WIDTH = 512
B_CONV = 31
C_HEADS = 8
C_HEAD_DIM = 64
IDX_HEADS = 8
IDX_DIM = 32
TOPK_MAX = 256
DIFF_HEADS = 4
DIFF_HEAD_DIM = 64
NUM_BUCKETS = 32
MAX_DISTANCE = 128
N_GROUPS = 4
EXPERTS_PER_GROUP = 8
N_EXPERTS = N_GROUPS * EXPERTS_PER_GROUP
C_WIDTH = C_HEADS * C_HEAD_DIM
DIFF_W = DIFF_HEADS * 2 * DIFF_HEAD_DIM
LANES = 128
BF16_ROWS = 16
INT_MIN = -(2 ** 31)
LOG2E = math.log2(math.e)
VMEM_LIMIT = 56 * 1024 * 1024

Z_KC, Z_KD = 0, 1
Z_KIDX_OFF = C_WIDTH + DIFF_W
Z_COLS = Z_KIDX_OFF + LANES
ZT_VC, ZT_VD, ZT_QC, ZT_QD = 0, 1, 2, 3
ZT_QIDX_OFF = 2 * C_WIDTH + 2 * DIFF_W
ZT_QIDX_ROWS = IDX_HEADS * IDX_DIM
ZT_W_OFF = ZT_QIDX_OFF + ZT_QIDX_ROWS
ZT_ROWS = ZT_W_OFF + BF16_ROWS


def _bucket_starts():
    n = np.arange(0, 2 * MAX_DISTANCE)
    me = NUM_BUCKETS // 2
    lr = np.log(np.maximum(n, 1) / me) / math.log(MAX_DISTANCE / me)
    large = me + (lr * (NUM_BUCKETS - me)).astype(np.int64)
    b = np.where(n < me, n, np.minimum(large, NUM_BUCKETS - 1))
    return [int(n[b >= k].min()) for k in range(NUM_BUCKETS)]


BUCKET_START = _bucket_starts()


def _cparams(sem):
    return pltpu.CompilerParams(dimension_semantics=sem, vmem_limit_bytes=VMEM_LIMIT)


def _rms(x):
    return x * lax.rsqrt(jnp.mean(x * x, axis=-1, keepdims=True) + EPS)


def _sigmoid(x):
    return 1.0 / (1.0 + jnp.exp(-x))


def _dot_nt(a, b):
    return lax.dot_general(a, b, (((1,), (1,)), ((), ())), preferred_element_type=F32)


def _ada_body(c_ref, w_ref, b_ref, o_ref):
    c = c_ref[...]
    cond = c * _sigmoid(c)
    o_ref[0] = jnp.dot(cond, w_ref[0], precision=HIGHEST, preferred_element_type=F32) + b_ref[0]


def _ada_mod(c, ada_w, ada_b):
    depth, d, n6 = ada_w.shape
    bsz = c.shape[0]
    rows = 8
    c_pad = jnp.zeros((rows, d), F32).at[:bsz].set(c)
    tn = 1536
    out = pl.pallas_call(
        _ada_body,
        grid=(depth, n6 // tn),
        in_specs=[pl.BlockSpec((rows, d), lambda i, j: (0, 0)),
                  pl.BlockSpec((1, d, tn), lambda i, j: (i, 0, j)),
                  pl.BlockSpec((1, 1, tn), lambda i, j: (i, 0, j))],
        out_specs=pl.BlockSpec((1, rows, tn), lambda i, j: (i, 0, j)),
        out_shape=jax.ShapeDtypeStruct((depth, rows, n6), F32),
        compiler_params=_cparams(("arbitrary", "arbitrary")),
        name="ada_mod",
    )(c_pad, ada_w, ada_b.reshape(depth, 1, n6))
    return out[:, :bsz]


def _norm_proj_body(x_ref, g_ref, sh_ref, sc_ref, w_ref, *rest):
    y = _rms(x_ref[0]) * g_ref[...]
    h = (y * (1.0 + sc_ref[0]) + sh_ref[0]).astype(BF16)
    if len(rest) == 1:
        (o_ref,) = rest
    else:
        wt_ref, o_ref, ot_ref = rest
        ot_ref[0] = _dot_nt(wt_ref[...], h).astype(ot_ref.dtype)
    o_ref[0] = jnp.dot(h, w_ref[...], preferred_element_type=F32).astype(o_ref.dtype)


def _norm_proj(x, g, sh, sc, w_bf16, tm, wt_bf16=None):
    bsz, seq, d = x.shape
    n = w_bf16.shape[1]
    in_specs = [pl.BlockSpec((1, tm, d), lambda b, i: (b, i, 0)),
                pl.BlockSpec((1, d), lambda b, i: (0, 0)),
                pl.BlockSpec((1, 1, d), lambda b, i: (b, 0, 0)),
                pl.BlockSpec((1, 1, d), lambda b, i: (b, 0, 0)),
                pl.BlockSpec((d, n), lambda b, i: (0, 0))]
    out_specs = pl.BlockSpec((1, tm, n), lambda b, i: (b, i, 0))
    out_shape = jax.ShapeDtypeStruct((bsz, seq, n), BF16)
    args = [x, g.reshape(1, d), sh.reshape(bsz, 1, d), sc.reshape(bsz, 1, d), w_bf16]
    if wt_bf16 is not None:
        nt = wt_bf16.shape[0]
        in_specs.append(pl.BlockSpec((nt, d), lambda b, i: (0, 0)))
        out_specs = [out_specs, pl.BlockSpec((1, nt, tm), lambda b, i: (b, 0, i))]
        out_shape = [out_shape, jax.ShapeDtypeStruct((bsz, nt, seq), BF16)]
        args.append(wt_bf16)
    return pl.pallas_call(
        _norm_proj_body,
        grid=(bsz, seq // tm),
        in_specs=in_specs,
        out_specs=out_specs,
        out_shape=out_shape,
        compiler_params=_cparams(("arbitrary", "arbitrary")),
        name="norm_proj",
    )(*args)


CONV_HALO = 32
CONV_ROWS = 64


def _conv_body(z_ref, x_ref, g1_ref, ca_ref, cb_ref, cbb_ref, lng_ref, lnb_ref, wo_ref, o_ref,
               ua_scr, ub_scr, y_scr, *, tl):
    l = pl.program_id(1)

    @pl.when(l == 0)
    def _():
        ua_scr[0:CONV_HALO, :] = jnp.zeros((CONV_HALO, A_WIDTH), F32)
        ub_scr[0:CONV_HALO, :] = jnp.zeros((CONV_HALO, B_WIDTH), F32)

    a = A_WIDTH
    gate_c = z_ref[0, :, a:2 * a].astype(F32)
    x_a = z_ref[0, :, 2 * a:3 * a].astype(F32)
    ua_scr[CONV_HALO:CONV_HALO + tl, :] = gate_c * x_a
    val_b = z_ref[0, :, 3 * a:3 * a + B_WIDTH].astype(F32)
    glu = z_ref[0, :, 3 * a + B_WIDTH:3 * a + 2 * B_WIDTH].astype(F32)
    ub_scr[CONV_HALO:CONV_HALO + tl, :] = val_b * _sigmoid(glu)

    for r in range(0, tl, CONV_ROWS):
        acc_a = None
        for k in range(A_CONV):
            tap = ua_scr[CONV_HALO + r - (A_CONV - 1) + k:CONV_HALO + r - (A_CONV - 1) + k + CONV_ROWS, :]
            term = tap * ca_ref[k:k + 1, :]
            acc_a = term if acc_a is None else acc_a + term
        gate_b = z_ref[0, r:r + CONV_ROWS, 0:a].astype(F32)
        y_scr[r:r + CONV_ROWS, 0:a] = (gate_b * acc_a).astype(BF16)

        win = CONV_HALO + r - 8
        acc_b = None
        for b in range(8):
            phase = None
            for back in range(b, B_CONV, 8):
                tap = ub_scr[win - (back - b):win - (back - b) + CONV_ROWS + 8, :]
                term = tap * cb_ref[B_CONV - 1 - back:B_CONV - back, :]
                phase = term if phase is None else phase + term
            piece = phase[8 - b:8 - b + CONV_ROWS]
            acc_b = piece if acc_b is None else acc_b + piece
        u = acc_b + cbb_ref[...]
        mu = jnp.mean(u, axis=-1, keepdims=True)
        uc = u - mu
        var = jnp.mean(uc * uc, axis=-1, keepdims=True)
        v = uc * lax.rsqrt(var + EPS) * lng_ref[...] + lnb_ref[...]
        y_scr[r:r + CONV_ROWS, a:a + B_WIDTH] = (v * _sigmoid(v)).astype(BF16)

    ua_scr[0:CONV_HALO, :] = ua_scr[tl:tl + CONV_HALO, :]
    ub_scr[0:CONV_HALO, :] = ub_scr[tl:tl + CONV_HALO, :]
    y = jnp.dot(y_scr[...], wo_ref[...], preferred_element_type=F32)
    o_ref[0] = x_ref[0] + g1_ref[0] * y


def _conv_mix(z, x, g1, conv_a, conv_b, conv_b_bias, ln_g, ln_b, w_out_bf16, tl):
    bsz, seq, d = x.shape
    nz = z.shape[-1]
    wide = A_WIDTH + B_WIDTH
    full = lambda shape: pl.BlockSpec(shape, lambda b, l: (0,) * len(shape))
    return pl.pallas_call(
        functools.partial(_conv_body, tl=tl),
        grid=(bsz, seq // tl),
        in_specs=[pl.BlockSpec((1, tl, nz), lambda b, l: (b, l, 0)),
                  pl.BlockSpec((1, tl, d), lambda b, l: (b, l, 0)),
                  pl.BlockSpec((1, 1, d), lambda b, l: (b, 0, 0)),
                  full((A_CONV, A_WIDTH)), full((B_CONV, B_WIDTH)), full((1, B_WIDTH)),
                  full((1, B_WIDTH)), full((1, B_WIDTH)), full((wide, d))],
        out_specs=pl.BlockSpec((1, tl, d), lambda b, l: (b, l, 0)),
        out_shape=jax.ShapeDtypeStruct((bsz, seq, d), F32),
        scratch_shapes=[pltpu.VMEM((CONV_HALO + tl, A_WIDTH), F32),
                        pltpu.VMEM((CONV_HALO + tl, B_WIDTH), F32),
                        pltpu.VMEM((tl, wide), BF16)],
        compiler_params=_cparams(("arbitrary", "arbitrary")),
        name="conv_mix",
    )(z, x, g1.reshape(bsz, 1, d), conv_a, conv_b, conv_b_bias.reshape(1, -1), ln_g.reshape(1, -1),
      ln_b.reshape(1, -1), w_out_bf16)


MOE_ROWS = 512
ROUTER_ROWS = 512
META_GATE0, META_GATE1, META_POS0, META_POS1 = range(4)
SEG_LEN, SEG_BASE, SEG_OFF = range(3)
SEG_ALIGN = 8
SEG_SIZES = tuple(2 ** b for b in range(10, 2, -1))
GROUP_LANE0 = N_EXPERTS


def _router_body(x_ref, g_ref, sh_ref, sc_ref, wr_ref, br_ref, tri_ref, upper_ref, h_ref, meta_ref, post_ref, seg_ref,
                 cnt_ref, base_scr):
    @pl.when(pl.program_id(0) == 0)
    def _():
        base_scr[...] = jnp.zeros_like(base_scr)

    h = _rms(x_ref[...]) * g_ref[...]
    h = h * (1.0 + sc_ref[0]) + sh_ref[0]
    h_ref[...] = h.astype(h_ref.dtype)
    h_hi = h.astype(BF16)
    h_lo = (h - h_hi.astype(F32)).astype(BF16)
    logits = (jnp.dot(h_hi, wr_ref[0], preferred_element_type=F32) + jnp.dot(h_hi, wr_ref[1], preferred_element_type=F32)
              + jnp.dot(h_lo, wr_ref[0], preferred_element_type=F32) + br_ref[...])
    tr = logits.shape[0]
    lane = lax.broadcasted_iota(I32, (tr, LANES), 1)
    lane_f = lane.astype(F32)
    neg = jnp.float32(-jnp.inf)
    big = jnp.float32(1e9)

    is_group = (lane >= GROUP_LANE0) & (lane < GROUP_LANE0 + N_GROUPS)
    glog = jnp.where(is_group, logits, neg)
    gmax = jnp.max(glog, axis=1, keepdims=True)
    p_top = 1.0 / jnp.sum(jnp.exp(glog - gmax), axis=1, keepdims=True)
    g_sel = jnp.min(jnp.where(glog == gmax, lane_f, big), axis=1, keepdims=True) - GROUP_LANE0
    lo = g_sel * EXPERTS_PER_GROUP
    in_group = (lane_f >= lo) & (lane_f < lo + EXPERTS_PER_GROUP)
    f1 = jnp.where(in_group, logits, neg)
    v1 = jnp.max(f1, axis=1, keepdims=True)
    i1 = jnp.min(jnp.where(f1 == v1, lane_f, big), axis=1, keepdims=True)
    f2 = jnp.where(lane_f == i1, neg, f1)
    v2 = jnp.max(f2, axis=1, keepdims=True)
    i2 = jnp.min(jnp.where(f2 == v2, lane_f, big), axis=1, keepdims=True)
    a = jnp.exp(v2 - v1)
    w1 = 1.0 / (1.0 + a)
    gate0 = p_top * w1
    gate1 = p_top * (a * w1)

    oh0 = lane_f == i1
    oh1 = lane_f == i2
    ind0 = jnp.where(oh0, 1.0, 0.0)
    ind1 = jnp.where(oh1, 1.0, 0.0)
    pre0 = jnp.dot(tri_ref[...], ind0.astype(BF16), preferred_element_type=F32)
    pre1 = jnp.dot(tri_ref[...], ind1.astype(BF16), preferred_element_type=F32)
    tot0 = jnp.sum(ind0, axis=0, keepdims=True)
    tot1 = jnp.sum(ind1, axis=0, keepdims=True)
    seg_len = jnp.floor((tot0 + tot1 + (SEG_ALIGN - 1)) * (1.0 / SEG_ALIGN)) * SEG_ALIGN
    seg_off = jnp.dot(jnp.broadcast_to(seg_len, (8, LANES)), upper_ref[...], precision=HIGHEST,
                      preferred_element_type=F32)[0:1]
    pos0 = jnp.sum(jnp.where(oh0, seg_off + pre0, 0.0), axis=1, keepdims=True)
    pos1 = jnp.sum(jnp.where(oh1, seg_off + tot0 + pre1, 0.0), axis=1, keepdims=True)
    base = base_scr[...]
    new_base = base + seg_len
    base_scr[...] = new_base
    cnt_ref[...] = new_base

    meta = jnp.zeros((tr, LANES), F32)
    for col, val in ((META_GATE0, gate0), (META_GATE1, gate1), (META_POS0, pos0), (META_POS1, pos1)):
        meta = jnp.where(lane == col, val, meta)
    meta_ref[...] = meta
    post_ref[0] = meta.T[0:8].astype(I32)
    srow = lax.broadcasted_iota(I32, (8, LANES), 0)
    seg = jnp.where(srow == SEG_LEN, seg_len, jnp.where(srow == SEG_BASE, base, jnp.where(srow == SEG_OFF, seg_off, 0.0)))
    seg_ref[0] = seg.astype(I32)


def _moe_router(x2, g, sh, sc, wr_g, br_g, wr_e, br_e, seq):
    n_tok, d = x2.shape
    bsz = n_tok // seq
    tr = min(ROUTER_ROWS, seq)
    steps_per_batch = seq // tr
    wr = jnp.zeros((d, LANES), F32).at[:, :N_EXPERTS].set(wr_e).at[:, GROUP_LANE0:GROUP_LANE0 + N_GROUPS].set(wr_g)
    wr_hi = wr.astype(BF16)
    wr = jnp.stack([wr_hi, (wr - wr_hi.astype(F32)).astype(BF16)])
    br = jnp.zeros((1, LANES), F32).at[0, :N_EXPERTS].set(br_e).at[0, GROUP_LANE0:GROUP_LANE0 + N_GROUPS].set(br_g)
    tri = jnp.tril(jnp.ones((tr, tr), BF16), -1)
    upper = jnp.triu(jnp.ones((LANES, LANES), F32), 1)
    nb = n_tok // tr
    full = lambda shape: pl.BlockSpec(shape, lambda i: (0,) * len(shape))
    return pl.pallas_call(
        _router_body,
        grid=(nb,),
        in_specs=[pl.BlockSpec((tr, d), lambda i: (i, 0)),
                  full((1, d)),
                  pl.BlockSpec((1, 1, d), lambda i: (i // steps_per_batch, 0, 0)),
                  pl.BlockSpec((1, 1, d), lambda i: (i // steps_per_batch, 0, 0)),
                  full((2, d, LANES)), full((1, LANES)), full((tr, tr)), full((LANES, LANES))],
        out_specs=[pl.BlockSpec((tr, d), lambda i: (i, 0)),
                   pl.BlockSpec((tr, LANES), lambda i: (i, 0)),
                   pl.BlockSpec((1, 8, tr), lambda i: (i, 0, 0)),
                   pl.BlockSpec((1, 8, LANES), lambda i: (i, 0, 0)),
                   full((1, LANES))],
        out_shape=[jax.ShapeDtypeStruct((n_tok, d), BF16),
                   jax.ShapeDtypeStruct((n_tok, LANES), F32),
                   jax.ShapeDtypeStruct((nb, 8, tr), I32),
                   jax.ShapeDtypeStruct((nb, 8, LANES), I32),
                   jax.ShapeDtypeStruct((1, LANES), F32)],
        scratch_shapes=[pltpu.VMEM((1, LANES), F32)],
        compiler_params=_cparams(("arbitrary",)),
        name="moe_router",
    )(x2, g.reshape(1, d), sh.reshape(bsz, 1, d), sc.reshape(bsz, 1, d), wr, br, tri, upper)


def _segment_copies(seg_ref, starts_ref, make_copy, action):
    def per_expert(e, carry):
        length = seg_ref[0, SEG_LEN, e]
        local = seg_ref[0, SEG_OFF, e]
        glob = starts_ref[e] + seg_ref[0, SEG_BASE, e]
        for size in SEG_SIZES:
            hit = (length & size) != 0

            @pl.when(hit)
            def _(local=local, glob=glob, size=size):
                cp = make_copy(pl.multiple_of(local, SEG_ALIGN), pl.multiple_of(glob, SEG_ALIGN), size)
                cp.start() if action == "start" else cp.wait()

            step = jnp.where(hit, size, 0)
            local = local + step
            glob = glob + step
        return carry

    lax.fori_loop(0, N_EXPERTS, per_expert, 0)


def _local_rows(tr):
    return 2 * tr + N_EXPERTS * SEG_ALIGN


def _dispatch_body(starts_ref, seg_ref, prev_seg_ref, h_ref, post_ref, xr_ref, xs_scr, zero_scr, sems, *, tr):
    i = pl.program_id(0)
    slot = i % 2
    lb = xs_scr.shape[1]
    sem = sems.at[0]

    def make_copy_of(s):
        def make_copy(local, glob, size):
            return pltpu.make_async_copy(xs_scr.at[s, pl.ds(local, size)], xr_ref.at[pl.ds(glob, size)], sems.at[s])
        return make_copy

    rows = lax.broadcasted_iota(I32, (lb, tr), 0)
    place = (rows == post_ref[0, META_POS0:META_POS0 + 1, :]) | (rows == post_ref[0, META_POS1:META_POS1 + 1, :])
    xs_scr[slot] = jnp.dot(jnp.where(place, 1.0, 0.0).astype(BF16), h_ref[...], preferred_element_type=F32)
    _segment_copies(seg_ref, starts_ref, make_copy_of(slot), "start")

    @pl.when(i > 0)
    def _():
        _segment_copies(prev_seg_ref, starts_ref, make_copy_of(1 - slot), "wait")

    @pl.when(i == pl.num_programs(0) - 1)
    def _():
        _segment_copies(seg_ref, starts_ref, make_copy_of(slot), "wait")
        zero_scr[...] = jnp.zeros_like(zero_scr)
        total = starts_ref[N_EXPERTS]
        tail = (-total) & (MOE_ROWS - 1)
        for action in ("start", "wait"):
            row = total
            for size in SEG_SIZES:
                if size >= MOE_ROWS:
                    continue
                hit = (tail & size) != 0

                @pl.when(hit)
                def _(row=row, size=size, action=action):
                    cp = pltpu.make_async_copy(zero_scr.at[pl.ds(0, size)],
                                               xr_ref.at[pl.ds(pl.multiple_of(row, SEG_ALIGN), size)], sem)
                    cp.start() if action == "start" else cp.wait()

                row = row + jnp.where(hit, size, 0)

        def free_block(b):
            return pltpu.make_async_copy(
                zero_scr, xr_ref.at[pl.ds(pl.multiple_of(b * MOE_ROWS, MOE_ROWS), MOE_ROWS)], sem)

        first_free = (total + MOE_ROWS - 1) // MOE_ROWS
        n_blocks = xr_ref.shape[0] // MOE_ROWS
        lax.fori_loop(first_free, n_blocks, lambda b, c: (free_block(b).start(), c)[1], 0)
        lax.fori_loop(first_free, n_blocks, lambda b, c: (free_block(b).wait(), c)[1], 0)


def _moe_dispatch(h2, post, seg, starts, n_rows, tr):
    n_tok, d = h2.shape
    grid_spec = pltpu.PrefetchScalarGridSpec(
        num_scalar_prefetch=1,
        grid=(n_tok // tr,),
        in_specs=[pl.BlockSpec((1, 8, LANES), lambda i, s: (i, 0, 0), memory_space=pltpu.SMEM),
                  pl.BlockSpec((1, 8, LANES), lambda i, s: (jnp.maximum(i - 1, 0), 0, 0), memory_space=pltpu.SMEM),
                  pl.BlockSpec((tr, d), lambda i, s: (i, 0)),
                  pl.BlockSpec((1, 8, tr), lambda i, s: (i, 0, 0))],
        out_specs=pl.BlockSpec(memory_space=pl.ANY),
        scratch_shapes=[pltpu.VMEM((2, _local_rows(tr), d), F32), pltpu.VMEM((MOE_ROWS, d), F32),
                        pltpu.SemaphoreType.DMA((2,))],
    )
    return pl.pallas_call(
        functools.partial(_dispatch_body, tr=tr),
        grid_spec=grid_spec,
        out_shape=jax.ShapeDtypeStruct((n_rows, d), F32),
        compiler_params=_cparams(("arbitrary",)),
        name="moe_dispatch",
    )(starts, seg, seg, h2, post)


def _expert_body(pb_ref, pe_ref, plo_ref, phi_ref, x_ref, wg_ref, wu_ref, wd_ref, o_ref, wg_s, wu_s, wd_s):
    p = pl.program_id(0)
    prev = jnp.maximum(p - 1, 0)
    new_expert = (p == 0) | (pe_ref[p] != pe_ref[prev])
    first = (p == 0) | (pb_ref[p] != pb_ref[prev])

    @pl.when(new_expert)
    def _():
        wg_s[...] = wg_ref[0, 0].astype(BF16)
        wu_s[...] = wu_ref[0, 0].astype(BF16)
        wd_s[...] = wd_ref[0, 0].astype(BF16)

    def rows_of_expert():
        x = x_ref[...].astype(BF16)
        gt = jnp.dot(x, wg_s[...], preferred_element_type=F32)
        up = jnp.dot(x, wu_s[...], preferred_element_type=F32)
        act = (gt * _sigmoid(gt)) * up
        y = jnp.dot(act.astype(BF16), wd_s[...], preferred_element_type=F32)
        rows = lax.broadcasted_iota(I32, (y.shape[0], 1), 0)
        return jnp.where((rows >= plo_ref[p]) & (rows < phi_ref[p]), y, 0.0)

    nonempty = phi_ref[p] > plo_ref[p]

    @pl.when(first & nonempty)
    def _():
        o_ref[...] = rows_of_expert()

    @pl.when(first & jnp.logical_not(nonempty))
    def _():
        o_ref[...] = jnp.zeros_like(o_ref)

    @pl.when(jnp.logical_not(first) & nonempty)
    def _():
        o_ref[...] += rows_of_expert()


def _moe_experts(x_rows, pairs, w_gate, w_up, w_down, layer):
    n_rows, d = x_rows.shape
    de = w_gate.shape[-1]
    n_pairs = pairs[0].shape[0]
    grid_spec = pltpu.PrefetchScalarGridSpec(
        num_scalar_prefetch=4,
        grid=(n_pairs,),
        in_specs=[pl.BlockSpec((MOE_ROWS, d), lambda p, pb, pe, lo, hi: (pb[p], 0)),
                  pl.BlockSpec((1, 1, d, de), lambda p, pb, pe, lo, hi: (layer, pe[p], 0, 0)),
                  pl.BlockSpec((1, 1, d, de), lambda p, pb, pe, lo, hi: (layer, pe[p], 0, 0)),
                  pl.BlockSpec((1, 1, de, d), lambda p, pb, pe, lo, hi: (layer, pe[p], 0, 0))],
        out_specs=pl.BlockSpec((MOE_ROWS, d), lambda p, pb, pe, lo, hi: (pb[p], 0)),
        scratch_shapes=[pltpu.VMEM((d, de), BF16), pltpu.VMEM((d, de), BF16), pltpu.VMEM((de, d), BF16)],
    )
    return pl.pallas_call(
        _expert_body,
        grid_spec=grid_spec,
        out_shape=jax.ShapeDtypeStruct((n_rows, d), F32),
        compiler_params=_cparams(("arbitrary",)),
        name="moe_experts",
    )(*pairs, x_rows, w_gate, w_up, w_down)


def _combine_body(starts_ref, seg_ref, next_seg_ref, y_ref, x_ref, meta_ref, g2_ref, fg_ref, o_ref, ys_scr, sems, *,
                  final_norm):
    i = pl.program_id(0)
    slot = i % 2

    def make_copy_of(s):
        def make_copy(local, glob, size):
            return pltpu.make_async_copy(y_ref.at[pl.ds(glob, size)], ys_scr.at[s, pl.ds(local, size)], sems.at[s])
        return make_copy

    @pl.when(i == 0)
    def _():
        ys_scr[...] = jnp.zeros_like(ys_scr)
        _segment_copies(seg_ref, starts_ref, make_copy_of(slot), "start")

    @pl.when(i + 1 < pl.num_programs(0))
    def _():
        _segment_copies(next_seg_ref, starts_ref, make_copy_of(1 - slot), "start")

    _segment_copies(seg_ref, starts_ref, make_copy_of(slot), "wait")

    meta = meta_ref[...]
    tr = meta.shape[0]
    y = ys_scr[slot].astype(BF16)
    cols = lax.broadcasted_iota(I32, (tr, ys_scr.shape[1]), 1)

    def picked(col):
        pos = meta[:, col:col + 1].astype(I32)
        return jnp.dot(jnp.where(cols == pos, 1.0, 0.0).astype(BF16), y, preferred_element_type=F32)

    moe = meta[:, META_GATE0:META_GATE0 + 1] * picked(META_POS0) + meta[:, META_GATE1:META_GATE1 + 1] * picked(META_POS1)
    xn = x_ref[...] + g2_ref[0] * moe
    if final_norm:
        xn = _rms(xn) * fg_ref[...]
    o_ref[...] = xn


def _moe_combine(y_rows, seg, starts, x2, meta, g2, final_g, seq, tr, final_norm):
    n_tok, d = x2.shape
    bsz = n_tok // seq
    steps_per_batch = seq // tr
    nb = n_tok // tr
    grid_spec = pltpu.PrefetchScalarGridSpec(
        num_scalar_prefetch=1,
        grid=(nb,),
        in_specs=[pl.BlockSpec((1, 8, LANES), lambda i, s: (i, 0, 0), memory_space=pltpu.SMEM),
                  pl.BlockSpec((1, 8, LANES), lambda i, s: (jnp.minimum(i + 1, nb - 1), 0, 0), memory_space=pltpu.SMEM),
                  pl.BlockSpec(memory_space=pl.ANY),
                  pl.BlockSpec((tr, d), lambda i, s: (i, 0)),
                  pl.BlockSpec((tr, LANES), lambda i, s: (i, 0)),
                  pl.BlockSpec((1, 1, d), lambda i, s: (i // steps_per_batch, 0, 0)),
                  pl.BlockSpec((1, d), lambda i, s: (0, 0))],
        out_specs=pl.BlockSpec((tr, d), lambda i, s: (i, 0)),
        scratch_shapes=[pltpu.VMEM((2, _local_rows(tr), d), F32), pltpu.SemaphoreType.DMA((2,))],
    )
    return pl.pallas_call(
        functools.partial(_combine_body, final_norm=final_norm),
        grid_spec=grid_spec,
        out_shape=jax.ShapeDtypeStruct((n_tok, d), F32),
        compiler_params=_cparams(("arbitrary",)),
        name="moe_combine",
    )(starts, seg, seg, y_rows, x2, meta, g2.reshape(bsz, 1, d), final_g.reshape(1, d))


def _expert_pairs(counts, n_rows):
    n_blocks = n_rows // MOE_ROWS
    n_pairs = n_blocks + N_EXPERTS
    ends = jnp.cumsum(counts)
    starts = ends - counts
    first_blk = starts // MOE_ROWS
    last_blk = (ends - 1) // MOE_ROWS
    npairs = jnp.where(counts > 0, last_blk - first_blk + 1, 0)
    pend = jnp.cumsum(npairs)
    poff = pend - npairs
    total = pend[-1]
    used_blocks = (ends[-1] + MOE_ROWS - 1) // MOE_ROWS
    p = jnp.arange(n_pairs, dtype=I32)
    p_eff = jnp.minimum(p, total - 1)
    e = jnp.minimum(jnp.sum(pend[None, :] <= p_eff[:, None], axis=1), N_EXPERTS - 1).astype(I32)
    table = jnp.stack([first_blk, poff, starts, ends], axis=1).astype(F32)
    onehot = (e[:, None] == jnp.arange(N_EXPERTS, dtype=I32)[None, :]).astype(F32)
    first_e, poff_e, start_e, end_e = jnp.dot(onehot, table, precision=HIGHEST).astype(I32).T
    valid = p < total
    blk = jnp.where(valid, first_e + p_eff - poff_e, jnp.minimum(used_blocks + p - total, n_blocks - 1)).astype(I32)
    lo = jnp.where(valid, jnp.clip(start_e - blk * MOE_ROWS, 0, MOE_ROWS), 0).astype(I32)
    hi = jnp.where(valid, jnp.clip(end_e - blk * MOE_ROWS, 0, MOE_ROWS), 0).astype(I32)
    return blk, e, lo, hi


def _hier_moe(x, g, sh, sc, g2, wr_g, br_g, wr_e, br_e, w_gate, w_up, w_down, layer, final_g, final_norm):
    bsz, seq, d = x.shape
    n_tok = bsz * seq
    tr = min(ROUTER_ROWS, seq)
    assert 2 * tr <= SEG_SIZES[0]
    x2 = x.reshape(n_tok, d)
    h2, meta, post, seg, cnt = _moe_router(x2, g, sh, sc, wr_g, br_g, wr_e, br_e, seq)
    counts = cnt[0, :N_EXPERTS].astype(I32)
    ends = jnp.cumsum(counts)
    starts = jnp.concatenate([ends - counts, ends[-1:]])
    n_rows = -(-(2 * n_tok + (n_tok // tr) * N_EXPERTS * (SEG_ALIGN - 1)) // MOE_ROWS) * MOE_ROWS
    x_rows = _moe_dispatch(h2, post, seg, starts, n_rows, tr)
    y_rows = _moe_experts(x_rows, _expert_pairs(counts, n_rows), w_gate, w_up, w_down, layer)
    out = _moe_combine(y_rows, seg, starts, x2, meta, g2, final_g, seq, tr, final_norm)
    return out.reshape(bsz, seq, d)


SEL_COLS = 256
SEL_CHUNK = 512
BIT_GROUP = 256
SEL_SWEEP = 64
ATT_TILE = 512


def _idx_score_t(k, qt, wt):
    acc = None
    for h in range(IDX_HEADS):
        rel = jnp.dot(k, qt[h * IDX_DIM:(h + 1) * IDX_DIM], preferred_element_type=F32)
        term = jnp.maximum(rel, 0.0) * wt[h:h + 1, :]
        acc = term if acc is None else acc + term
    return acc


def _bit_transpose32(a):
    a = list(a)
    m, j = 0x0000FFFF, 16
    while j:
        k = 0
        while k < 32:
            t = (a[k] ^ lax.shift_right_logical(a[k + j], jnp.int32(j))) & jnp.int32(m)
            a[k] = a[k] ^ t
            a[k + j] = a[k + j] ^ (t << j)
            k = (k + j + 1) & ~j
        j >>= 1
        m = (m ^ (m << j)) & 0xFFFFFFFF
    return a


def _select_body(qt_ref, zk_ref, wt_ref, o_ref, planes_scr, eq_scr, gt_scr, *, topk):
    i = pl.program_id(1)
    tq = qt_ref.shape[2]
    qt = qt_ref[0]
    wt = wt_ref[0].astype(F32)
    n_chunks = ((i + 1) * tq + SEL_CHUNK - 1) // SEL_CHUNK
    krow0 = lax.broadcasted_iota(I32, (SEL_CHUNK, tq), 0)
    qcol = i * tq + lax.broadcasted_iota(I32, (SEL_CHUNK, tq), 1)
    int_min = jnp.int32(INT_MIN)

    groups = SEL_CHUNK // BIT_GROUP
    words = SEL_CHUNK // 32
    sweep_rows = min(SEL_SWEEP, eq_scr.shape[0])

    def fill(masked, c, carry):
        off = pl.multiple_of(c * SEL_CHUNK, SEL_CHUNK)
        k = zk_ref[0, pl.ds(off, SEL_CHUNK), :][:, 0:IDX_DIM]
        bits = lax.bitcast_convert_type(_idx_score_t(k, qt, wt), I32)
        key = bits ^ ((bits >> 31) | int_min)
        if masked:
            key = jnp.where(krow0 + off <= qcol, key, 0)
        for g in range(groups):
            for lt in range(0, tq, LANES):
                ku = key[g * BIT_GROUP:(g + 1) * BIT_GROUP, lt:lt + LANES]
                planes = _bit_transpose32([ku[8 * r:8 * r + 8] for r in range(32)])
                wrow = pl.multiple_of(c * words + g * 8, 8)
                for b in range(32):
                    planes_scr[b, pl.ds(wrow, 8), lt:lt + LANES] = planes[b]
        return carry

    n_plain = (i * tq + 1) // SEL_CHUNK
    lax.fori_loop(0, n_plain, functools.partial(fill, False), 0)
    lax.fori_loop(n_plain, n_chunks, functools.partial(fill, True), 0)

    n_sweep = (n_chunks * words + sweep_rows - 1) // sweep_rows

    def pad(c, carry):
        wrow = pl.multiple_of(c * words, words)
        for b in range(32):
            planes_scr[b, pl.ds(wrow, words), :] = jnp.zeros((words, tq), I32)
        return carry

    lax.fori_loop(n_chunks, n_sweep * (sweep_rows // words), pad, 0)

    def sweep(upd, cnt_plane):
        def body(sb, acc):
            r0 = pl.multiple_of(sb * sweep_rows, sweep_rows)
            eq = eq_scr[pl.ds(r0, sweep_rows), :]
            gt = gt_scr[pl.ds(r0, sweep_rows), :]
            if upd is not None:
                plane, accept = upd
                hit = eq & planes_scr[plane, pl.ds(r0, sweep_rows), :]
                gt = jnp.where(accept, gt, gt | hit)
                eq = jnp.where(accept, hit, eq ^ hit)
                eq_scr[pl.ds(r0, sweep_rows), :] = eq
                gt_scr[pl.ds(r0, sweep_rows), :] = gt
            if cnt_plane is None:
                return acc
            ones = lax.population_count(gt | (eq & planes_scr[cnt_plane, pl.ds(r0, sweep_rows), :]))
            return acc + jnp.sum(ones.reshape(sweep_rows // 8, 8, tq), axis=0)
        acc = lax.fori_loop(0, n_sweep, body, jnp.zeros((8, tq), I32))
        return jnp.sum(acc.astype(F32), axis=0, keepdims=True)

    k_f = jnp.float32(topk)
    eq_scr[...] = jnp.full(eq_scr.shape, -1, I32)
    gt_scr[...] = jnp.zeros(gt_scr.shape, I32)

    def bit_step(ib, carry):
        u, cnt = carry
        accept = cnt >= k_f
        u = jnp.where(accept, u | jnp.left_shift(jnp.int32(1), 32 - ib), u)
        return u, sweep((ib - 1, accept), ib)

    u, cnt = lax.fori_loop(1, 32, bit_step, (jnp.zeros((1, tq), I32), sweep(None, 0)))
    accept = cnt >= k_f
    u = jnp.where(accept, u | 1, u)
    sweep((31, accept), None)
    some = jnp.where(u != 0, -1, 0)

    def popcount_rows(word_of):
        def body(sb, acc):
            r0 = pl.multiple_of(sb * sweep_rows, sweep_rows)
            ones = lax.population_count(word_of(r0))
            return acc + jnp.sum(ones.reshape(sweep_rows // 8, 8, tq), axis=0)
        acc = lax.fori_loop(0, n_sweep, body, jnp.zeros((8, tq), I32))
        return jnp.sum(acc.astype(F32), axis=0, keepdims=True)

    n_gt = popcount_rows(lambda r0: gt_scr[pl.ds(r0, sweep_rows), :])
    n_eq = jnp.where(u != 0, popcount_rows(lambda r0: eq_scr[pl.ds(r0, sweep_rows), :]), 0.0)

    @pl.when(jnp.max(n_gt + n_eq) > k_f)
    def _():
        wrow = lax.broadcasted_iota(I32, (sweep_rows, tq), 0)

        def below(cap, r0):
            w = wrow + r0
            last = lax.shift_right_arithmetic(cap - (w >> 3) * BIT_GROUP - (w & 7) - 1, 3)
            clear = jnp.clip(31 - last, 0, 32)
            return jnp.where(clear >= 32, 0, jnp.left_shift(jnp.int32(-1), jnp.minimum(clear, 31)))

        need = k_f - n_gt

        def cap_step(ib, v):
            cand = v | jnp.left_shift(jnp.int32(1), 14 - ib)
            cnt = popcount_rows(lambda r0: eq_scr[pl.ds(r0, sweep_rows), :] & below(cand, r0))
            return jnp.where(cnt <= need, cand, v)

        cap = lax.fori_loop(0, 15, cap_step, jnp.zeros((1, tq), I32))
        cap = jnp.where((u != 0) & (n_gt + n_eq > k_f), cap, 2 ** 30)

        def trim(sb, carry):
            r0 = pl.multiple_of(sb * sweep_rows, sweep_rows)
            eq_scr[pl.ds(r0, sweep_rows), :] = eq_scr[pl.ds(r0, sweep_rows), :] & below(cap, r0)
            return carry

        lax.fori_loop(0, n_sweep, trim, 0)

    def emit(c, carry):
        off = pl.multiple_of(c * SEL_CHUNK, SEL_CHUNK)
        for g in range(groups):
            wrow = pl.multiple_of(c * words + g * 8, 8)
            keep = gt_scr[pl.ds(wrow, 8), :] | (eq_scr[pl.ds(wrow, 8), :] & some)
            rows = [jnp.where((lax.shift_right_logical(keep, 31 - r) & 1) != 0, 0.0, -jnp.inf) for r in range(32)]
            o_ref[0, pl.ds(pl.multiple_of(off + g * BIT_GROUP, BIT_GROUP), BIT_GROUP), :] = (
                jnp.concatenate(rows, axis=0).astype(o_ref.dtype))
        return carry

    lax.fori_loop(0, n_chunks, emit, 0)

    def blank(c, carry):
        off = pl.multiple_of(c * SEL_CHUNK, SEL_CHUNK)
        o_ref[0, pl.ds(off, SEL_CHUNK), :] = jnp.full((SEL_CHUNK, tq), -jnp.inf, o_ref.dtype)
        return carry

    lax.fori_loop(n_chunks, o_ref.shape[1] // SEL_CHUNK, blank, 0)


def _select(z, zt, topk):
    bsz, seq, _ = z.shape
    tq = min(SEL_COLS, seq)
    assert seq % SEL_CHUNK == 0 and seq % tq == 0
    return pl.pallas_call(
        functools.partial(_select_body, topk=topk),
        grid=(bsz, seq // tq),
        in_specs=[pl.BlockSpec((1, ZT_QIDX_ROWS, tq), lambda b, i: (b, ZT_QIDX_OFF // ZT_QIDX_ROWS, i)),
                  pl.BlockSpec((1, seq, LANES), lambda b, i: (b, 0, Z_KIDX_OFF // LANES)),
                  pl.BlockSpec((1, BF16_ROWS, tq), lambda b, i: (b, ZT_W_OFF // BF16_ROWS, i))],
        out_specs=pl.BlockSpec((1, seq, tq), lambda b, i: (b, 0, i)),
        out_shape=jax.ShapeDtypeStruct((bsz, seq, seq), BF16),
        scratch_shapes=[pltpu.VMEM((32, seq // 32, tq), I32),
                        pltpu.VMEM((seq // 32, tq), I32), pltpu.VMEM((seq // 32, tq), I32)],
        compiler_params=_cparams(("arbitrary", "arbitrary")),
        name="dsa_select",
    )(zt, z, zt)


BIAS_INIT_ROWS = 8
ONES_ROWS = BF16_ROWS


def _init_bias_tiles(btile, bias_ref, head0, n_heads, ta):
    col = lax.broadcasted_iota(I32, (BIAS_INIT_ROWS, ta), 1)
    row0 = lax.broadcasted_iota(I32, (BIAS_INIT_ROWS, ta), 0)

    def body(r, carry):
        off = pl.multiple_of(r * BIAS_INIT_ROWS, BIAS_INIT_ROWS)
        for kind in range(2):
            dist = col - (row0 + off) + kind * ta
            for h in range(n_heads):
                far = bias_ref[NUM_BUCKETS - 1, head0 + h]
                val = jnp.full((BIAS_INIT_ROWS, ta), (bias_ref[0, head0 + h] - far) * LOG2E, F32)
                for b in range(1, NUM_BUCKETS - 1):
                    val = jnp.where(dist >= BUCKET_START[b], (bias_ref[b, head0 + h] - far) * LOG2E, val)
                val = jnp.where(dist >= BUCKET_START[NUM_BUCKETS - 1], 0.0, val)
                btile[h, kind, pl.ds(off, BIAS_INIT_ROWS), :] = val
        return carry

    lax.fori_loop(0, ta // BIAS_INIT_ROWS, body, 0)


def _with_ones(vt):
    return jnp.concatenate([vt, jnp.ones((ONES_ROWS, vt.shape[1]), vt.dtype)], axis=0)


def _softmax_step_t(logits, v_aug, m_ref, acc_ref, idx):
    m_old = m_ref[idx]
    m_new = jnp.maximum(m_old, jnp.max(logits, axis=0, keepdims=True))
    m_safe = jnp.where(m_new == -jnp.inf, 0.0, m_new)
    p = jnp.exp2(logits - m_safe)
    alpha = jnp.exp2(m_old - m_safe)
    acc_ref[idx] = alpha * acc_ref[idx] + jnp.dot(v_aug, p.astype(BF16), preferred_element_type=F32)
    m_ref[idx] = m_new


def _reset_softmax(m_s, acc_s):
    m_s[...] = jnp.full(m_s.shape, -jnp.inf, F32)
    acc_s[...] = jnp.zeros(acc_s.shape, F32)


def _cattn_body(qi_ref, ki_ref, bias_ref, qt_ref, zk_ref, vt_ref, mask_ref, o_ref, m_s, acc_s, btile):
    b = pl.program_id(0)
    p = pl.program_id(1)
    qi = qi_ref[p]
    ki = ki_ref[p]
    ta = zk_ref.shape[1]
    hd = C_HEAD_DIM

    @pl.when((b == 0) & (p == 0))
    def _():
        _init_bias_tiles(btile, bias_ref, 0, C_HEADS, ta)

    @pl.when(ki == 0)
    def _():
        _reset_softmax(m_s, acc_s)

    def heads(extra_of_head):
        for h in range(C_HEADS):
            lo = h * hd
            logits = (jnp.dot(zk_ref[0, :, lo:lo + hd], qt_ref[0, lo:lo + hd, :], preferred_element_type=F32)
                      + extra_of_head(h))
            _softmax_step_t(logits, _with_ones(vt_ref[0, lo:lo + hd, :]), m_s, acc_s, h)

    near = ki >= qi - 1

    @pl.when(near)
    def _():
        heads(lambda h: mask_ref[0].astype(F32) + btile[h, qi - ki])

    @pl.when(jnp.logical_not(near))
    def _():
        heads(lambda h: mask_ref[0].astype(F32))

    @pl.when(ki == qi)
    def _():
        outs = []
        for h in range(C_HEADS):
            a = acc_s[h]
            outs.append((a[0:hd] / a[hd:hd + 1]).T)
        o_ref[0] = jnp.concatenate(outs, axis=1).astype(o_ref.dtype)


def _dattn_body(qi_ref, ki_ref, bias_ref, qt_ref, zk_ref, vt_ref, lam_ref, ng_ref, o_ref,
                m_s, acc_s, btile, *, lambda_init):
    b = pl.program_id(0)
    p = pl.program_id(1)
    qi = qi_ref[p]
    ki = ki_ref[p]
    ta = zk_ref.shape[1]
    hd = DIFF_HEAD_DIM
    dv = 2 * hd

    @pl.when((b == 0) & (p == 0))
    def _():
        _init_bias_tiles(btile, bias_ref, C_HEADS, DIFF_HEADS, ta)

    @pl.when(ki == 0)
    def _():
        _reset_softmax(m_s, acc_s)

    def heads(extra_of_head):
        for h in range(DIFF_HEADS):
            v_aug = _with_ones(vt_ref[0, dv * h:dv * (h + 1), :])
            extra = extra_of_head(h)
            for j in range(2):
                lo = (2 * h + j) * hd
                logits = jnp.dot(zk_ref[0, :, lo:lo + hd], qt_ref[0, lo:lo + hd, :], preferred_element_type=F32)
                if extra is not None:
                    logits = logits + extra
                _softmax_step_t(logits, v_aug, m_s, acc_s, 2 * h + j)

    near = ki >= qi - 1

    @pl.when(near)
    def _():
        krow = ki * ta + lax.broadcasted_iota(I32, (ta, ta), 0)
        qcol = qi * ta + lax.broadcasted_iota(I32, (ta, ta), 1)
        heads(lambda h: jnp.where(krow <= qcol, btile[h, qi - ki], -jnp.inf))

    @pl.when(jnp.logical_not(near))
    def _():
        heads(lambda h: None)

    @pl.when(ki == qi)
    def _():
        lam_p = lam_ref[...]
        lam = (jnp.exp(jnp.sum(lam_p[0:1] * lam_p[1:2], axis=1, keepdims=True))
               - jnp.exp(jnp.sum(lam_p[2:3] * lam_p[3:4], axis=1, keepdims=True)) + lambda_init)
        outs = []
        for h in range(DIFF_HEADS):
            a1 = acc_s[2 * h]
            a2 = acc_s[2 * h + 1]
            o = a1[0:dv] / a1[dv:dv + 1] - lam * (a2[0:dv] / a2[dv:dv + 1])
            o = o * lax.rsqrt(jnp.mean(o * o, axis=0, keepdims=True) + EPS) * ng_ref[...] * (1.0 - lambda_init)
            outs.append(o.T)
        o_ref[0] = jnp.concatenate(outs, axis=1).astype(o_ref.dtype)


def _causal_pairs(nq):
    qi = [q for q in range(nq) for _ in range(q + 1)]
    ki = [k for q in range(nq) for k in range(q + 1)]
    return jnp.asarray(qi, I32), jnp.asarray(ki, I32)


def _cattn(z, zt, mask, rel_bias, ta):
    bsz, seq, _ = z.shape
    qi, ki = _causal_pairs(seq // ta)
    qmap = lambda col: (lambda b, p, qi, ki: (b, qi[p], col))
    kmap = lambda col: (lambda b, p, qi, ki: (b, ki[p], col))
    dv_aug = C_HEAD_DIM + ONES_ROWS
    grid_spec = pltpu.PrefetchScalarGridSpec(
        num_scalar_prefetch=2,
        grid=(bsz, qi.shape[0]),
        in_specs=[pl.BlockSpec(memory_space=pltpu.SMEM),
                  pl.BlockSpec((1, C_WIDTH, ta), lambda b, p, qi, ki: (b, ZT_QC, qi[p])),
                  pl.BlockSpec((1, ta, C_WIDTH), kmap(Z_KC)),
                  pl.BlockSpec((1, C_WIDTH, ta), lambda b, p, qi, ki: (b, ZT_VC, ki[p])),
                  pl.BlockSpec((1, ta, ta), lambda b, p, qi, ki: (b, ki[p], qi[p]))],
        out_specs=pl.BlockSpec((1, ta, C_WIDTH), qmap(0)),
        scratch_shapes=[pltpu.VMEM((C_HEADS, 1, ta), F32),
                        pltpu.VMEM((C_HEADS, dv_aug, ta), F32),
                        pltpu.VMEM((C_HEADS, 2, ta, ta), F32)],
    )
    return pl.pallas_call(
        _cattn_body,
        grid_spec=grid_spec,
        out_shape=jax.ShapeDtypeStruct((bsz, seq, C_WIDTH), BF16),
        compiler_params=_cparams(("arbitrary", "arbitrary")),
        name="dsa_attn",
    )(qi, ki, rel_bias, zt, z, zt, mask)


def _dattn(z, zt, rel_bias, diff_lam, diff_norm_g, lambda_init, ta):
    bsz, seq, _ = z.shape
    qi, ki = _causal_pairs(seq // ta)
    n_maps = 2 * DIFF_HEADS
    dv = 2 * DIFF_HEAD_DIM
    qmap = lambda col: (lambda b, p, qi, ki: (b, qi[p], col))
    kmap = lambda col: (lambda b, p, qi, ki: (b, ki[p], col))
    grid_spec = pltpu.PrefetchScalarGridSpec(
        num_scalar_prefetch=2,
        grid=(bsz, qi.shape[0]),
        in_specs=[pl.BlockSpec(memory_space=pltpu.SMEM),
                  pl.BlockSpec((1, DIFF_W, ta), lambda b, p, qi, ki: (b, ZT_QD, qi[p])),
                  pl.BlockSpec((1, ta, DIFF_W), kmap(Z_KD)),
                  pl.BlockSpec((1, DIFF_W, ta), lambda b, p, qi, ki: (b, ZT_VD, ki[p])),
                  pl.BlockSpec(diff_lam.shape, lambda b, p, qi, ki: (0, 0)),
                  pl.BlockSpec((dv, 1), lambda b, p, qi, ki: (0, 0))],
        out_specs=pl.BlockSpec((1, ta, DIFF_W), qmap(0)),
        scratch_shapes=[pltpu.VMEM((n_maps, 1, ta), F32),
                        pltpu.VMEM((n_maps, dv + ONES_ROWS, ta), F32),
                        pltpu.VMEM((DIFF_HEADS, 2, ta, ta), F32)],
    )
    return pl.pallas_call(
        functools.partial(_dattn_body, lambda_init=lambda_init),
        grid_spec=grid_spec,
        out_shape=jax.ShapeDtypeStruct((bsz, seq, DIFF_W), BF16),
        compiler_params=_cparams(("arbitrary", "arbitrary")),
        name="diff_attn",
    )(qi, ki, rel_bias, zt, z, zt, diff_lam, diff_norm_g.reshape(dv, 1))


def _out_proj_body(oc_ref, od_ref, x_ref, g1_ref, w_ref, o_ref):
    y = (jnp.dot(oc_ref[0], w_ref[0:C_WIDTH, :], preferred_element_type=F32)
         + jnp.dot(od_ref[0], w_ref[C_WIDTH:C_WIDTH + DIFF_W, :], preferred_element_type=F32))
    o_ref[0] = x_ref[0] + g1_ref[0] * y


def _out_proj(out_c, out_d, x, g1, w_out_bf16, tm):
    bsz, seq, d = x.shape
    return pl.pallas_call(
        _out_proj_body,
        grid=(bsz, seq // tm),
        in_specs=[pl.BlockSpec((1, tm, C_WIDTH), lambda b, i: (b, i, 0)),
                  pl.BlockSpec((1, tm, DIFF_W), lambda b, i: (b, i, 0)),
                  pl.BlockSpec((1, tm, d), lambda b, i: (b, i, 0)),
                  pl.BlockSpec((1, 1, d), lambda b, i: (b, 0, 0)),
                  pl.BlockSpec((C_WIDTH + DIFF_W, d), lambda b, i: (0, 0))],
        out_specs=pl.BlockSpec((1, tm, d), lambda b, i: (b, i, 0)),
        out_shape=jax.ShapeDtypeStruct((bsz, seq, d), F32),
        compiler_params=_cparams(("arbitrary", "arbitrary")),
        name="attn_out_proj",
    )(out_c, out_d, x, g1.reshape(bsz, 1, d), w_out_bf16)


def _attn_in_weights(cd_w_in):
    sizes = (C_WIDTH, C_WIDTH, C_WIDTH, IDX_HEADS * IDX_DIM, IDX_DIM, IDX_HEADS, DIFF_W, DIFF_W, DIFF_W)
    cuts = np.cumsum(sizes)[:-1]
    q_c, k_c, v_c, q_i, k_i, w_i, q_d, k_d, v_d = jnp.split(cd_w_in, cuts, axis=1)
    d = cd_w_in.shape[0]
    w = jnp.concatenate([k_c, k_d, k_i, jnp.zeros((d, LANES - IDX_DIM), cd_w_in.dtype)], axis=1)
    wt = jnp.concatenate([v_c, v_d, q_c * (C_HEAD_DIM ** -0.5 * LOG2E), q_d * (DIFF_HEAD_DIM ** -0.5 * LOG2E), q_i,
                          w_i * (IDX_DIM * IDX_HEADS) ** -0.5,
                          jnp.zeros((d, BF16_ROWS - IDX_HEADS), cd_w_in.dtype)], axis=1).T
    return w.astype(BF16), wt.astype(BF16)


def kernel(x, c, positions, rel_bias, norm_g, final_norm_g, ada_w, ada_b, ab_w_in, ab_conv_a, ab_conv_b,
           ab_conv_b_bias, ab_ln_g, ab_ln_b, ab_w_out, cd_w_in, diff_lam, diff_norm_g, cd_w_out,
           moe_wr_g, moe_br_g, moe_wr_e, moe_br_e, moe_w_gate, moe_w_up, moe_w_down):
    del positions
    bsz, seq, d = x.shape
    depth = ada_w.shape[0]
    tm = min(512, seq)
    ta = min(ATT_TILE, seq)
    assert ta >= MAX_DISTANCE and seq % ta == 0
    topk = min(TOPK_MAX, seq // 4)
    mods = _ada_mod(c, ada_w, ada_b)
    for i in range(depth):
        sh1, sc1, g1, sh2, sc2, g2 = jnp.split(mods[i], 6, axis=-1)
        j = i // 2
        if i % 2 == 0:
            z = _norm_proj(x, norm_g[i, 0], sh1, sc1, ab_w_in[j].astype(BF16), tm)
            x = _conv_mix(z, x, g1, ab_conv_a[j], ab_conv_b[j], ab_conv_b_bias[j], ab_ln_g[j], ab_ln_b[j],
                          ab_w_out[j].astype(BF16), min(256, seq))
        else:
            lambda_init = 0.8 - 0.6 * math.exp(-0.3 * i)
            w, wt = _attn_in_weights(cd_w_in[j])
            z, zt = _norm_proj(x, norm_g[i, 0], sh1, sc1, w, tm, wt)
            mask = _select(z, zt, topk)
            out_c = _cattn(z, zt, mask, rel_bias, ta)
            out_d = _dattn(z, zt, rel_bias, diff_lam[j], diff_norm_g[j], lambda_init, ta)
            x = _out_proj(out_c, out_d, x, g1, cd_w_out[j].astype(BF16), tm)
        x = _hier_moe(x, norm_g[i, 1], sh2, sc2, g2, moe_wr_g[i], moe_br_g[i], moe_wr_e[i], moe_br_e[i],
                      moe_w_gate, moe_w_up, moe_w_down, i, final_norm_g, final_norm=(i == depth - 1))
    return x
```

```python
import functools
import math

import numpy as np
import jax
import jax.numpy as jnp
from jax import lax
from jax.experimental import pallas as pl
from jax.experimental.pallas import tpu as pltpu

F32 = jnp.float32
BF16 = jnp.bfloat16
I32 = jnp.int32
HIGHEST = lax.Precision.HIGHEST

EPS = 1e-6
A_WIDTH = 512
A_CONV = 3
B_WIDTH = 512
B_CONV = 31
C_HEADS = 8
C_HEAD_DIM = 64
IDX_HEADS = 8
IDX_DIM = 32
TOPK_MAX = 256
DIFF_HEADS = 4
DIFF_HEAD_DIM = 64
NUM_BUCKETS = 32
MAX_DISTANCE = 128
N_GROUPS = 4
EXPERTS_PER_GROUP = 8
N_EXPERTS = N_GROUPS * EXPERTS_PER_GROUP
C_WIDTH = C_HEADS * C_HEAD_DIM
DIFF_W = DIFF_HEADS * 2 * DIFF_HEAD_DIM
LANES = 128
BF16_ROWS = 16
INT_MIN = -(2 ** 31)
LOG2E = math.log2(math.e)
VMEM_LIMIT = 56 * 1024 * 1024

Z_KC, Z_KD = 0, 1
Z_KIDX_OFF = C_WIDTH + DIFF_W
Z_COLS = Z_KIDX_OFF + LANES
ZT_VC, ZT_VD, ZT_QC, ZT_QD = 0, 1, 2, 3
ZT_QIDX_OFF = 2 * C_WIDTH + 2 * DIFF_W
ZT_QIDX_ROWS = IDX_HEADS * IDX_DIM
ZT_W_OFF = ZT_QIDX_OFF + ZT_QIDX_ROWS
ZT_ROWS = ZT_W_OFF + BF16_ROWS


def _bucket_starts():
    n = np.arange(0, 2 * MAX_DISTANCE)
    me = NUM_BUCKETS // 2
    lr = np.log(np.maximum(n, 1) / me) / math.log(MAX_DISTANCE / me)
    large = me + (lr * (NUM_BUCKETS - me)).astype(np.int64)
    b = np.where(n < me, n, np.minimum(large, NUM_BUCKETS - 1))
    return [int(n[b >= k].min()) for k in range(NUM_BUCKETS)]


BUCKET_START = _bucket_starts()


def _cparams(sem):
    return pltpu.CompilerParams(dimension_semantics=sem, vmem_limit_bytes=VMEM_LIMIT)


def _rms(x):
    return x * lax.rsqrt(jnp.mean(x * x, axis=-1, keepdims=True) + EPS)


def _sigmoid(x):
    return 1.0 / (1.0 + jnp.exp(-x))


def _dot_nt(a, b):
    return lax.dot_general(a, b, (((1,), (1,)), ((), ())), preferred_element_type=F32)


def _ada_body(c_ref, w_ref, b_ref, o_ref):
    c = c_ref[...]
    cond = c * _sigmoid(c)
    o_ref[0] = jnp.dot(cond, w_ref[0], precision=HIGHEST, preferred_element_type=F32) + b_ref[0]


def _ada_mod(c, ada_w, ada_b):
    depth, d, n6 = ada_w.shape
    bsz = c.shape[0]
    rows = 8
    c_pad = jnp.zeros((rows, d), F32).at[:bsz].set(c)
    tn = 1536
    out = pl.pallas_call(
        _ada_body,
        grid=(depth, n6 // tn),
        in_specs=[pl.BlockSpec((rows, d), lambda i, j: (0, 0)),
                  pl.BlockSpec((1, d, tn), lambda i, j: (i, 0, j)),
                  pl.BlockSpec((1, 1, tn), lambda i, j: (i, 0, j))],
        out_specs=pl.BlockSpec((1, rows, tn), lambda i, j: (i, 0, j)),
        out_shape=jax.ShapeDtypeStruct((depth, rows, n6), F32),
        compiler_params=_cparams(("arbitrary", "arbitrary")),
        name="ada_mod",
    )(c_pad, ada_w, ada_b.reshape(depth, 1, n6))
    return out[:, :bsz]


def _norm_proj_body(x_ref, g_ref, sh_ref, sc_ref, w_ref, *rest):
    y = _rms(x_ref[0]) * g_ref[...]
    h = (y * (1.0 + sc_ref[0]) + sh_ref[0]).astype(BF16)
    if len(rest) == 1:
        (o_ref,) = rest
    else:
        wt_ref, o_ref, ot_ref = rest
        ot_ref[0] = _dot_nt(wt_ref[...], h).astype(ot_ref.dtype)
    o_ref[0] = jnp.dot(h, w_ref[...], preferred_element_type=F32).astype(o_ref.dtype)


def _norm_proj(x, g, sh, sc, w_bf16, tm, wt_bf16=None):
    bsz, seq, d = x.shape
    n = w_bf16.shape[1]
    in_specs = [pl.BlockSpec((1, tm, d), lambda b, i: (b, i, 0)),
                pl.BlockSpec((1, d), lambda b, i: (0, 0)),
                pl.BlockSpec((1, 1, d), lambda b, i: (b, 0, 0)),
                pl.BlockSpec((1, 1, d), lambda b, i: (b, 0, 0)),
                pl.BlockSpec((d, n), lambda b, i: (0, 0))]
    out_specs = pl.BlockSpec((1, tm, n), lambda b, i: (b, i, 0))
    out_shape = jax.ShapeDtypeStruct((bsz, seq, n), BF16)
    args = [x, g.reshape(1, d), sh.reshape(bsz, 1, d), sc.reshape(bsz, 1, d), w_bf16]
    if wt_bf16 is not None:
        nt = wt_bf16.shape[0]
        in_specs.append(pl.BlockSpec((nt, d), lambda b, i: (0, 0)))
        out_specs = [out_specs, pl.BlockSpec((1, nt, tm), lambda b, i: (b, 0, i))]
        out_shape = [out_shape, jax.ShapeDtypeStruct((bsz, nt, seq), BF16)]
        args.append(wt_bf16)
    return pl.pallas_call(
        _norm_proj_body,
        grid=(bsz, seq // tm),
        in_specs=in_specs,
        out_specs=out_specs,
        out_shape=out_shape,
        compiler_params=_cparams(("arbitrary", "arbitrary")),
        name="norm_proj",
    )(*args)


CONV_HALO = 32
CONV_ROWS = 64


def _conv_body(z_ref, x_ref, g1_ref, ca_ref, cb_ref, cbb_ref, lng_ref, lnb_ref, wo_ref, o_ref,
               ua_scr, ub_scr, y_scr, *, tl):
    l = pl.program_id(1)

    @pl.when(l == 0)
    def _():
        ua_scr[0:CONV_HALO, :] = jnp.zeros((CONV_HALO, A_WIDTH), F32)
        ub_scr[0:CONV_HALO, :] = jnp.zeros((CONV_HALO, B_WIDTH), F32)

    a = A_WIDTH
    gate_c = z_ref[0, :, a:2 * a].astype(F32)
    x_a = z_ref[0, :, 2 * a:3 * a].astype(F32)
    ua_scr[CONV_HALO:CONV_HALO + tl, :] = gate_c * x_a
    val_b = z_ref[0, :, 3 * a:3 * a + B_WIDTH].astype(F32)
    glu = z_ref[0, :, 3 * a + B_WIDTH:3 * a + 2 * B_WIDTH].astype(F32)
    ub_scr[CONV_HALO:CONV_HALO + tl, :] = val_b * _sigmoid(glu)

    for r in range(0, tl, CONV_ROWS):
        acc_a = None
        for k in range(A_CONV):
            tap = ua_scr[CONV_HALO + r - (A_CONV - 1) + k:CONV_HALO + r - (A_CONV - 1) + k + CONV_ROWS, :]
            term = tap * ca_ref[k:k + 1, :]
            acc_a = term if acc_a is None else acc_a + term
        gate_b = z_ref[0, r:r + CONV_ROWS, 0:a].astype(F32)
        y_scr[r:r + CONV_ROWS, 0:a] = (gate_b * acc_a).astype(BF16)

        win = CONV_HALO + r - 8
        acc_b = None
        for b in range(8):
            phase = None
            for back in range(b, B_CONV, 8):
                tap = ub_scr[win - (back - b):win - (back - b) + CONV_ROWS + 8, :]
                term = tap * cb_ref[B_CONV - 1 - back:B_CONV - back, :]
                phase = term if phase is None else phase + term
            piece = phase[8 - b:8 - b + CONV_ROWS]
            acc_b = piece if acc_b is None else acc_b + piece
        u = acc_b + cbb_ref[...]
        mu = jnp.mean(u, axis=-1, keepdims=True)
        uc = u - mu
        var = jnp.mean(uc * uc, axis=-1, keepdims=True)
        v = uc * lax.rsqrt(var + EPS) * lng_ref[...] + lnb_ref[...]
        y_scr[r:r + CONV_ROWS, a:a + B_WIDTH] = (v * _sigmoid(v)).astype(BF16)

    ua_scr[0:CONV_HALO, :] = ua_scr[tl:tl + CONV_HALO, :]
    ub_scr[0:CONV_HALO, :] = ub_scr[tl:tl + CONV_HALO, :]
    y = jnp.dot(y_scr[...], wo_ref[...], preferred_element_type=F32)
    o_ref[0] = x_ref[0] + g1_ref[0] * y


def _conv_mix(z, x, g1, conv_a, conv_b, conv_b_bias, ln_g, ln_b, w_out_bf16, tl):
    bsz, seq, d = x.shape
    nz = z.shape[-1]
    wide = A_WIDTH + B_WIDTH
    full = lambda shape: pl.BlockSpec(shape, lambda b, l: (0,) * len(shape))
    return pl.pallas_call(
        functools.partial(_conv_body, tl=tl),
        grid=(bsz, seq // tl),
        in_specs=[pl.BlockSpec((1, tl, nz), lambda b, l: (b, l, 0)),
                  pl.BlockSpec((1, tl, d), lambda b, l: (b, l, 0)),
                  pl.BlockSpec((1, 1, d), lambda b, l: (b, 0, 0)),
                  full((A_CONV, A_WIDTH)), full((B_CONV, B_WIDTH)), full((1, B_WIDTH)),
                  full((1, B_WIDTH)), full((1, B_WIDTH)), full((wide, d))],
        out_specs=pl.BlockSpec((1, tl, d), lambda b, l: (b, l, 0)),
        out_shape=jax.ShapeDtypeStruct((bsz, seq, d), F32),
        scratch_shapes=[pltpu.VMEM((CONV_HALO + tl, A_WIDTH), F32),
                        pltpu.VMEM((CONV_HALO + tl, B_WIDTH), F32),
                        pltpu.VMEM((tl, wide), BF16)],
        compiler_params=_cparams(("arbitrary", "arbitrary")),
        name="conv_mix",
    )(z, x, g1.reshape(bsz, 1, d), conv_a, conv_b, conv_b_bias.reshape(1, -1), ln_g.reshape(1, -1),
      ln_b.reshape(1, -1), w_out_bf16)


MOE_ROWS = 512
ROUTER_ROWS = 512
META_GATE0, META_GATE1, META_POS0, META_POS1 = range(4)
SEG_LEN, SEG_BASE, SEG_OFF = range(3)
SEG_ALIGN = 8
SEG_SIZES = tuple(2 ** b for b in range(10, 2, -1))
GROUP_LANE0 = N_EXPERTS


def _router_body(x_ref, g_ref, sh_ref, sc_ref, wr_ref, br_ref, tri_ref, upper_ref, h_ref, meta_ref, post_ref, seg_ref,
                 cnt_ref, base_scr):
    @pl.when(pl.program_id(0) == 0)
    def _():
        base_scr[...] = jnp.zeros_like(base_scr)

    h = _rms(x_ref[...]) * g_ref[...]
    h = h * (1.0 + sc_ref[0]) + sh_ref[0]
    h_ref[...] = h.astype(h_ref.dtype)
    h_hi = h.astype(BF16)
    h_lo = (h - h_hi.astype(F32)).astype(BF16)
    logits = (jnp.dot(h_hi, wr_ref[0], preferred_element_type=F32) + jnp.dot(h_hi, wr_ref[1], preferred_element_type=F32)
              + jnp.dot(h_lo, wr_ref[0], preferred_element_type=F32) + br_ref[...])
    tr = logits.shape[0]
    lane = lax.broadcasted_iota(I32, (tr, LANES), 1)
    lane_f = lane.astype(F32)
    neg = jnp.float32(-jnp.inf)
    big = jnp.float32(1e9)

    is_group = (lane >= GROUP_LANE0) & (lane < GROUP_LANE0 + N_GROUPS)
    glog = jnp.where(is_group, logits, neg)
    gmax = jnp.max(glog, axis=1, keepdims=True)
    p_top = 1.0 / jnp.sum(jnp.exp(glog - gmax), axis=1, keepdims=True)
    g_sel = jnp.min(jnp.where(glog == gmax, lane_f, big), axis=1, keepdims=True) - GROUP_LANE0
    lo = g_sel * EXPERTS_PER_GROUP
    in_group = (lane_f >= lo) & (lane_f < lo + EXPERTS_PER_GROUP)
    f1 = jnp.where(in_group, logits, neg)
    v1 = jnp.max(f1, axis=1, keepdims=True)
    i1 = jnp.min(jnp.where(f1 == v1, lane_f, big), axis=1, keepdims=True)
    f2 = jnp.where(lane_f == i1, neg, f1)
    v2 = jnp.max(f2, axis=1, keepdims=True)
    i2 = jnp.min(jnp.where(f2 == v2, lane_f, big), axis=1, keepdims=True)
    a = jnp.exp(v2 - v1)
    w1 = 1.0 / (1.0 + a)
    gate0 = p_top * w1
    gate1 = p_top * (a * w1)

    oh0 = lane_f == i1
    oh1 = lane_f == i2
    ind0 = jnp.where(oh0, 1.0, 0.0)
    ind1 = jnp.where(oh1, 1.0, 0.0)
    pre0 = jnp.dot(tri_ref[...], ind0.astype(BF16), preferred_element_type=F32)
    pre1 = jnp.dot(tri_ref[...], ind1.astype(BF16), preferred_element_type=F32)
    tot0 = jnp.sum(ind0, axis=0, keepdims=True)
    tot1 = jnp.sum(ind1, axis=0, keepdims=True)
    seg_len = jnp.floor((tot0 + tot1 + (SEG_ALIGN - 1)) * (1.0 / SEG_ALIGN)) * SEG_ALIGN
    seg_off = jnp.dot(jnp.broadcast_to(seg_len, (8, LANES)), upper_ref[...], precision=HIGHEST,
                      preferred_element_type=F32)[0:1]
    pos0 = jnp.sum(jnp.where(oh0, seg_off + pre0, 0.0), axis=1, keepdims=True)
    pos1 = jnp.sum(jnp.where(oh1, seg_off + tot0 + pre1, 0.0), axis=1, keepdims=True)
    base = base_scr[...]
    new_base = base + seg_len
    base_scr[...] = new_base
    cnt_ref[...] = new_base

    meta = jnp.zeros((tr, LANES), F32)
    for col, val in ((META_GATE0, gate0), (META_GATE1, gate1), (META_POS0, pos0), (META_POS1, pos1)):
        meta = jnp.where(lane == col, val, meta)
    meta_ref[...] = meta
    post_ref[0] = meta.T[0:8].astype(I32)
    srow = lax.broadcasted_iota(I32, (8, LANES), 0)
    seg = jnp.where(srow == SEG_LEN, seg_len, jnp.where(srow == SEG_BASE, base, jnp.where(srow == SEG_OFF, seg_off, 0.0)))
    seg_ref[0] = seg.astype(I32)


def _moe_router(x2, g, sh, sc, wr_g, br_g, wr_e, br_e, seq):
    n_tok, d = x2.shape
    bsz = n_tok // seq
    tr = min(ROUTER_ROWS, seq)
    steps_per_batch = seq // tr
    wr = jnp.zeros((d, LANES), F32).at[:, :N_EXPERTS].set(wr_e).at[:, GROUP_LANE0:GROUP_LANE0 + N_GROUPS].set(wr_g)
    wr_hi = wr.astype(BF16)
    wr = jnp.stack([wr_hi, (wr - wr_hi.astype(F32)).astype(BF16)])
    br = jnp.zeros((1, LANES), F32).at[0, :N_EXPERTS].set(br_e).at[0, GROUP_LANE0:GROUP_LANE0 + N_GROUPS].set(br_g)
    tri = jnp.tril(jnp.ones((tr, tr), BF16), -1)
    upper = jnp.triu(jnp.ones((LANES, LANES), F32), 1)
    nb = n_tok // tr
    full = lambda shape: pl.BlockSpec(shape, lambda i: (0,) * len(shape))
    return pl.pallas_call(
        _router_body,
        grid=(nb,),
        in_specs=[pl.BlockSpec((tr, d), lambda i: (i, 0)),
                  full((1, d)),
                  pl.BlockSpec((1, 1, d), lambda i: (i // steps_per_batch, 0, 0)),
                  pl.BlockSpec((1, 1, d), lambda i: (i // steps_per_batch, 0, 0)),
                  full((2, d, LANES)), full((1, LANES)), full((tr, tr)), full((LANES, LANES))],
        out_specs=[pl.BlockSpec((tr, d), lambda i: (i, 0)),
                   pl.BlockSpec((tr, LANES), lambda i: (i, 0)),
                   pl.BlockSpec((1, 8, tr), lambda i: (i, 0, 0)),
                   pl.BlockSpec((1, 8, LANES), lambda i: (i, 0, 0)),
                   full((1, LANES))],
        out_shape=[jax.ShapeDtypeStruct((n_tok, d), BF16),
                   jax.ShapeDtypeStruct((n_tok, LANES), F32),
                   jax.ShapeDtypeStruct((nb, 8, tr), I32),
                   jax.ShapeDtypeStruct((nb, 8, LANES), I32),
                   jax.ShapeDtypeStruct((1, LANES), F32)],
        scratch_shapes=[pltpu.VMEM((1, LANES), F32)],
        compiler_params=_cparams(("arbitrary",)),
        name="moe_router",
    )(x2, g.reshape(1, d), sh.reshape(bsz, 1, d), sc.reshape(bsz, 1, d), wr, br, tri, upper)


def _segment_copies(seg_ref, starts_ref, make_copy, action):
    def per_expert(e, carry):
        length = seg_ref[0, SEG_LEN, e]
        local = seg_ref[0, SEG_OFF, e]
        glob = starts_ref[e] + seg_ref[0, SEG_BASE, e]
        for size in SEG_SIZES:
            hit = (length & size) != 0

            @pl.when(hit)
            def _(local=local, glob=glob, size=size):
                cp = make_copy(pl.multiple_of(local, SEG_ALIGN), pl.multiple_of(glob, SEG_ALIGN), size)
                cp.start() if action == "start" else cp.wait()

            step = jnp.where(hit, size, 0)
            local = local + step
            glob = glob + step
        return carry

    lax.fori_loop(0, N_EXPERTS, per_expert, 0)


def _local_rows(tr):
    return 2 * tr + N_EXPERTS * SEG_ALIGN


def _dispatch_body(starts_ref, seg_ref, prev_seg_ref, h_ref, post_ref, xr_ref, xs_scr, zero_scr, sems, *, tr):
    i = pl.program_id(0)
    slot = i % 2
    lb = xs_scr.shape[1]
    sem = sems.at[0]

    def make_copy_of(s):
        def make_copy(local, glob, size):
            return pltpu.make_async_copy(xs_scr.at[s, pl.ds(local, size)], xr_ref.at[pl.ds(glob, size)], sems.at[s])
        return make_copy

    rows = lax.broadcasted_iota(I32, (lb, tr), 0)
    place = (rows == post_ref[0, META_POS0:META_POS0 + 1, :]) | (rows == post_ref[0, META_POS1:META_POS1 + 1, :])
    xs_scr[slot] = jnp.dot(jnp.where(place, 1.0, 0.0).astype(BF16), h_ref[...], preferred_element_type=F32)
    _segment_copies(seg_ref, starts_ref, make_copy_of(slot), "start")

    @pl.when(i > 0)
    def _():
        _segment_copies(prev_seg_ref, starts_ref, make_copy_of(1 - slot), "wait")

    @pl.when(i == pl.num_programs(0) - 1)
    def _():
        _segment_copies(seg_ref, starts_ref, make_copy_of(slot), "wait")
        zero_scr[...] = jnp.zeros_like(zero_scr)
        total = starts_ref[N_EXPERTS]
        tail = (-total) & (MOE_ROWS - 1)
        for action in ("start", "wait"):
            row = total
            for size in SEG_SIZES:
                if size >= MOE_ROWS:
                    continue
                hit = (tail & size) != 0

                @pl.when(hit)
                def _(row=row, size=size, action=action):
                    cp = pltpu.make_async_copy(zero_scr.at[pl.ds(0, size)],
                                               xr_ref.at[pl.ds(pl.multiple_of(row, SEG_ALIGN), size)], sem)
                    cp.start() if action == "start" else cp.wait()

                row = row + jnp.where(hit, size, 0)

        def free_block(b):
            return pltpu.make_async_copy(
                zero_scr, xr_ref.at[pl.ds(pl.multiple_of(b * MOE_ROWS, MOE_ROWS), MOE_ROWS)], sem)

        first_free = (total + MOE_ROWS - 1) // MOE_ROWS
        n_blocks = xr_ref.shape[0] // MOE_ROWS
        lax.fori_loop(first_free, n_blocks, lambda b, c: (free_block(b).start(), c)[1], 0)
        lax.fori_loop(first_free, n_blocks, lambda b, c: (free_block(b).wait(), c)[1], 0)


def _moe_dispatch(h2, post, seg, starts, n_rows, tr):
    n_tok, d = h2.shape
    grid_spec = pltpu.PrefetchScalarGridSpec(
        num_scalar_prefetch=1,
        grid=(n_tok // tr,),
        in_specs=[pl.BlockSpec((1, 8, LANES), lambda i, s: (i, 0, 0), memory_space=pltpu.SMEM),
                  pl.BlockSpec((1, 8, LANES), lambda i, s: (jnp.maximum(i - 1, 0), 0, 0), memory_space=pltpu.SMEM),
                  pl.BlockSpec((tr, d), lambda i, s: (i, 0)),
                  pl.BlockSpec((1, 8, tr), lambda i, s: (i, 0, 0))],
        out_specs=pl.BlockSpec(memory_space=pl.ANY),
        scratch_shapes=[pltpu.VMEM((2, _local_rows(tr), d), F32), pltpu.VMEM((MOE_ROWS, d), F32),
                        pltpu.SemaphoreType.DMA((2,))],
    )
    return pl.pallas_call(
        functools.partial(_dispatch_body, tr=tr),
        grid_spec=grid_spec,
        out_shape=jax.ShapeDtypeStruct((n_rows, d), F32),
        compiler_params=_cparams(("arbitrary",)),
        name="moe_dispatch",
    )(starts, seg, seg, h2, post)


def _expert_body(pb_ref, pe_ref, plo_ref, phi_ref, x_ref, wg_ref, wu_ref, wd_ref, o_ref, wg_s, wu_s, wd_s):
    p = pl.program_id(0)
    prev = jnp.maximum(p - 1, 0)
    new_expert = (p == 0) | (pe_ref[p] != pe_ref[prev])
    first = (p == 0) | (pb_ref[p] != pb_ref[prev])

    @pl.when(new_expert)
    def _():
        wg_s[...] = wg_ref[0, 0].astype(BF16)
        wu_s[...] = wu_ref[0, 0].astype(BF16)
        wd_s[...] = wd_ref[0, 0].astype(BF16)

    def rows_of_expert():
        x = x_ref[...].astype(BF16)
        gt = jnp.dot(x, wg_s[...], preferred_element_type=F32)
        up = jnp.dot(x, wu_s[...], preferred_element_type=F32)
        act = (gt * _sigmoid(gt)) * up
        y = jnp.dot(act.astype(BF16), wd_s[...], preferred_element_type=F32)
        rows = lax.broadcasted_iota(I32, (y.shape[0], 1), 0)
        return jnp.where((rows >= plo_ref[p]) & (rows < phi_ref[p]), y, 0.0)

    nonempty = phi_ref[p] > plo_ref[p]

    @pl.when(first & nonempty)
    def _():
        o_ref[...] = rows_of_expert()

    @pl.when(first & jnp.logical_not(nonempty))
    def _():
        o_ref[...] = jnp.zeros_like(o_ref)

    @pl.when(jnp.logical_not(first) & nonempty)
    def _():
        o_ref[...] += rows_of_expert()


def _moe_experts(x_rows, pairs, w_gate, w_up, w_down, layer):
    n_rows, d = x_rows.shape
    de = w_gate.shape[-1]
    n_pairs = pairs[0].shape[0]
    grid_spec = pltpu.PrefetchScalarGridSpec(
        num_scalar_prefetch=4,
        grid=(n_pairs,),
        in_specs=[pl.BlockSpec((MOE_ROWS, d), lambda p, pb, pe, lo, hi: (pb[p], 0)),
                  pl.BlockSpec((1, 1, d, de), lambda p, pb, pe, lo, hi: (layer, pe[p], 0, 0)),
                  pl.BlockSpec((1, 1, d, de), lambda p, pb, pe, lo, hi: (layer, pe[p], 0, 0)),
                  pl.BlockSpec((1, 1, de, d), lambda p, pb, pe, lo, hi: (layer, pe[p], 0, 0))],
        out_specs=pl.BlockSpec((MOE_ROWS, d), lambda p, pb, pe, lo, hi: (pb[p], 0)),
        scratch_shapes=[pltpu.VMEM((d, de), BF16), pltpu.VMEM((d, de), BF16), pltpu.VMEM((de, d), BF16)],
    )
    return pl.pallas_call(
        _expert_body,
        grid_spec=grid_spec,
        out_shape=jax.ShapeDtypeStruct((n_rows, d), F32),
        compiler_params=_cparams(("arbitrary",)),
        name="moe_experts",
    )(*pairs, x_rows, w_gate, w_up, w_down)


def _combine_body(starts_ref, seg_ref, next_seg_ref, y_ref, x_ref, meta_ref, g2_ref, fg_ref, o_ref, ys_scr, sems, *,
                  final_norm):
    i = pl.program_id(0)
    slot = i % 2

    def make_copy_of(s):
        def make_copy(local, glob, size):
            return pltpu.make_async_copy(y_ref.at[pl.ds(glob, size)], ys_scr.at[s, pl.ds(local, size)], sems.at[s])
        return make_copy

    @pl.when(i == 0)
    def _():
        ys_scr[...] = jnp.zeros_like(ys_scr)
        _segment_copies(seg_ref, starts_ref, make_copy_of(slot), "start")

    @pl.when(i + 1 < pl.num_programs(0))
    def _():
        _segment_copies(next_seg_ref, starts_ref, make_copy_of(1 - slot), "start")

    _segment_copies(seg_ref, starts_ref, make_copy_of(slot), "wait")

    meta = meta_ref[...]
    tr = meta.shape[0]
    y = ys_scr[slot].astype(BF16)
    cols = lax.broadcasted_iota(I32, (tr, ys_scr.shape[1]), 1)

    def picked(col):
        pos = meta[:, col:col + 1].astype(I32)
        return jnp.dot(jnp.where(cols == pos, 1.0, 0.0).astype(BF16), y, preferred_element_type=F32)

    moe = meta[:, META_GATE0:META_GATE0 + 1] * picked(META_POS0) + meta[:, META_GATE1:META_GATE1 + 1] * picked(META_POS1)
    xn = x_ref[...] + g2_ref[0] * moe
    if final_norm:
        xn = _rms(xn) * fg_ref[...]
    o_ref[...] = xn


def _moe_combine(y_rows, seg, starts, x2, meta, g2, final_g, seq, tr, final_norm):
    n_tok, d = x2.shape
    bsz = n_tok // seq
    steps_per_batch = seq // tr
    nb = n_tok // tr
    grid_spec = pltpu.PrefetchScalarGridSpec(
        num_scalar_prefetch=1,
        grid=(nb,),
        in_specs=[pl.BlockSpec((1, 8, LANES), lambda i, s: (i, 0, 0), memory_space=pltpu.SMEM),
                  pl.BlockSpec((1, 8, LANES), lambda i, s: (jnp.minimum(i + 1, nb - 1), 0, 0), memory_space=pltpu.SMEM),
                  pl.BlockSpec(memory_space=pl.ANY),
                  pl.BlockSpec((tr, d), lambda i, s: (i, 0)),
                  pl.BlockSpec((tr, LANES), lambda i, s: (i, 0)),
                  pl.BlockSpec((1, 1, d), lambda i, s: (i // steps_per_batch, 0, 0)),
                  pl.BlockSpec((1, d), lambda i, s: (0, 0))],
        out_specs=pl.BlockSpec((tr, d), lambda i, s: (i, 0)),
        scratch_shapes=[pltpu.VMEM((2, _local_rows(tr), d), F32), pltpu.SemaphoreType.DMA((2,))],
    )
    return pl.pallas_call(
        functools.partial(_combine_body, final_norm=final_norm),
        grid_spec=grid_spec,
        out_shape=jax.ShapeDtypeStruct((n_tok, d), F32),
        compiler_params=_cparams(("arbitrary",)),
        name="moe_combine",
    )(starts, seg, seg, y_rows, x2, meta, g2.reshape(bsz, 1, d), final_g.reshape(1, d))


def _expert_pairs(counts, n_rows):
    n_blocks = n_rows // MOE_ROWS
    n_pairs = n_blocks + N_EXPERTS
    ends = jnp.cumsum(counts)
    starts = ends - counts
    first_blk = starts // MOE_ROWS
    last_blk = (ends - 1) // MOE_ROWS
    npairs = jnp.where(counts > 0, last_blk - first_blk + 1, 0)
    pend = jnp.cumsum(npairs)
    poff = pend - npairs
    total = pend[-1]
    used_blocks = (ends[-1] + MOE_ROWS - 1) // MOE_ROWS
    p = jnp.arange(n_pairs, dtype=I32)
    p_eff = jnp.minimum(p, total - 1)
    e = jnp.minimum(jnp.sum(pend[None, :] <= p_eff[:, None], axis=1), N_EXPERTS - 1).astype(I32)
    table = jnp.stack([first_blk, poff, starts, ends], axis=1).astype(F32)
    onehot = (e[:, None] == jnp.arange(N_EXPERTS, dtype=I32)[None, :]).astype(F32)
    first_e, poff_e, start_e, end_e = jnp.dot(onehot, table, precision=HIGHEST).astype(I32).T
    valid = p < total
    blk = jnp.where(valid, first_e + p_eff - poff_e, jnp.minimum(used_blocks + p - total, n_blocks - 1)).astype(I32)
    lo = jnp.where(valid, jnp.clip(start_e - blk * MOE_ROWS, 0, MOE_ROWS), 0).astype(I32)
    hi = jnp.where(valid, jnp.clip(end_e - blk * MOE_ROWS, 0, MOE_ROWS), 0).astype(I32)
    return blk, e, lo, hi


def _hier_moe(x, g, sh, sc, g2, wr_g, br_g, wr_e, br_e, w_gate, w_up, w_down, layer, final_g, final_norm):
    bsz, seq, d = x.shape
    n_tok = bsz * seq
    tr = min(ROUTER_ROWS, seq)
    assert 2 * tr <= SEG_SIZES[0]
    x2 = x.reshape(n_tok, d)
    h2, meta, post, seg, cnt = _moe_router(x2, g, sh, sc, wr_g, br_g, wr_e, br_e, seq)
    counts = cnt[0, :N_EXPERTS].astype(I32)
    ends = jnp.cumsum(counts)
    starts = jnp.concatenate([ends - counts, ends[-1:]])
    n_rows = -(-(2 * n_tok + (n_tok // tr) * N_EXPERTS * (SEG_ALIGN - 1)) // MOE_ROWS) * MOE_ROWS
    x_rows = _moe_dispatch(h2, post, seg, starts, n_rows, tr)
    y_rows = _moe_experts(x_rows, _expert_pairs(counts, n_rows), w_gate, w_up, w_down, layer)
    out = _moe_combine(y_rows, seg, starts, x2, meta, g2, final_g, seq, tr, final_norm)
    return out.reshape(bsz, seq, d)


SEL_COLS = 256
SEL_CHUNK = 512
BIT_GROUP = 256
SEL_SWEEP = 64
ATT_TILE = 512
MASK_NEG = -1e30


def _idx_score_t(k, qt, wt):
    acc = None
    for h in range(IDX_HEADS):
        rel = jnp.dot(k, qt[h * IDX_DIM:(h + 1) * IDX_DIM], preferred_element_type=F32)
        term = jnp.maximum(rel, 0.0) * wt[h:h + 1, :]
        acc = term if acc is None else acc + term
    return acc


def _bit_transpose32(a):
    a = list(a)
    m, j = 0x0000FFFF, 16
    while j:
        k = 0
        while k < 32:
            t = (a[k] ^ lax.shift_right_logical(a[k + j], jnp.int32(j))) & jnp.int32(m)
            a[k] = a[k] ^ t
            a[k + j] = a[k + j] ^ (t << j)
            k = (k + j + 1) & ~j
        j >>= 1
        m = (m ^ (m << j)) & 0xFFFFFFFF
    return a


def _select_body(qt_ref, zk_ref, wt_ref, o_ref, planes_scr, eq_scr, gt_scr, *, topk):
    i = pl.program_id(1)
    tq = qt_ref.shape[2]
    qt = qt_ref[0]
    wt = wt_ref[0].astype(F32)
    n_chunks = ((i + 1) * tq + SEL_CHUNK - 1) // SEL_CHUNK
    krow0 = lax.broadcasted_iota(I32, (SEL_CHUNK, tq), 0)
    qcol = i * tq + lax.broadcasted_iota(I32, (SEL_CHUNK, tq), 1)
    int_min = jnp.int32(INT_MIN)

    groups = SEL_CHUNK // BIT_GROUP
    words = SEL_CHUNK // 32
    sweep_rows = min(SEL_SWEEP, eq_scr.shape[0])

    def fill(masked, c, carry):
        off = pl.multiple_of(c * SEL_CHUNK, SEL_CHUNK)
        k = zk_ref[0, pl.ds(off, SEL_CHUNK), :][:, 0:IDX_DIM]
        bits = lax.bitcast_convert_type(_idx_score_t(k, qt, wt), I32)
        key = bits ^ ((bits >> 31) | int_min)
        if masked:
            key = jnp.where(krow0 + off <= qcol, key, 0)
        for g in range(groups):
            for lt in range(0, tq, LANES):
                ku = key[g * BIT_GROUP:(g + 1) * BIT_GROUP, lt:lt + LANES]
                planes = _bit_transpose32([ku[8 * r:8 * r + 8] for r in range(32)])
                wrow = pl.multiple_of(c * words + g * 8, 8)
                for b in range(32):
                    planes_scr[b, pl.ds(wrow, 8), lt:lt + LANES] = planes[b]
        return carry

    n_plain = (i * tq + 1) // SEL_CHUNK
    lax.fori_loop(0, n_plain, functools.partial(fill, False), 0)
    lax.fori_loop(n_plain, n_chunks, functools.partial(fill, True), 0)

    n_sweep = (n_chunks * words + sweep_rows - 1) // sweep_rows

    def pad(c, carry):
        wrow = pl.multiple_of(c * words, words)
        for b in range(32):
            planes_scr[b, pl.ds(wrow, words), :] = jnp.zeros((words, tq), I32)
        return carry

    lax.fori_loop(n_chunks, n_sweep * (sweep_rows // words), pad, 0)

    def sweep(upd, cnt_plane):
        def body(sb, acc):
            r0 = pl.multiple_of(sb * sweep_rows, sweep_rows)
            eq = eq_scr[pl.ds(r0, sweep_rows), :]
            gt = gt_scr[pl.ds(r0, sweep_rows), :]
            if upd is not None:
                plane, accept = upd
                hit = eq & planes_scr[plane, pl.ds(r0, sweep_rows), :]
                gt = jnp.where(accept, gt, gt | hit)
                eq = jnp.where(accept, hit, eq ^ hit)
                eq_scr[pl.ds(r0, sweep_rows), :] = eq
                gt_scr[pl.ds(r0, sweep_rows), :] = gt
            if cnt_plane is None:
                return acc
            ones = lax.population_count(gt | (eq & planes_scr[cnt_plane, pl.ds(r0, sweep_rows), :]))
            return acc + jnp.sum(ones.reshape(sweep_rows // 8, 8, tq), axis=0)
        acc = lax.fori_loop(0, n_sweep, body, jnp.zeros((8, tq), I32))
        return jnp.sum(acc.astype(F32), axis=0, keepdims=True)

    k_f = jnp.float32(topk)
    eq_scr[...] = jnp.full(eq_scr.shape, -1, I32)
    gt_scr[...] = jnp.zeros(gt_scr.shape, I32)

    def bit_step(ib, carry):
        u, cnt = carry
        accept = cnt >= k_f
        u = jnp.where(accept, u | jnp.left_shift(jnp.int32(1), 32 - ib), u)
        return u, sweep((ib - 1, accept), ib)

    u, cnt = lax.fori_loop(1, 32, bit_step, (jnp.zeros((1, tq), I32), sweep(None, 0)))
    accept = cnt >= k_f
    u = jnp.where(accept, u | 1, u)
    sweep((31, accept), None)
    some = jnp.where(u != 0, -1, 0)

    def popcount_rows(word_of):
        def body(sb, acc):
            r0 = pl.multiple_of(sb * sweep_rows, sweep_rows)
            ones = lax.population_count(word_of(r0))
            return acc + jnp.sum(ones.reshape(sweep_rows // 8, 8, tq), axis=0)
        acc = lax.fori_loop(0, n_sweep, body, jnp.zeros((8, tq), I32))
        return jnp.sum(acc.astype(F32), axis=0, keepdims=True)

    n_gt = popcount_rows(lambda r0: gt_scr[pl.ds(r0, sweep_rows), :])
    n_eq = jnp.where(u != 0, popcount_rows(lambda r0: eq_scr[pl.ds(r0, sweep_rows), :]), 0.0)

    @pl.when(jnp.max(n_gt + n_eq) > k_f)
    def _():
        wrow = lax.broadcasted_iota(I32, (sweep_rows, tq), 0)

        def below(cap, r0):
            w = wrow + r0
            last = lax.shift_right_arithmetic(cap - (w >> 3) * BIT_GROUP - (w & 7) - 1, 3)
            clear = jnp.clip(31 - last, 0, 32)
            return jnp.where(clear >= 32, 0, jnp.left_shift(jnp.int32(-1), jnp.minimum(clear, 31)))

        need = k_f - n_gt

        def cap_step(ib, v):
            cand = v | jnp.left_shift(jnp.int32(1), 14 - ib)
            cnt = popcount_rows(lambda r0: eq_scr[pl.ds(r0, sweep_rows), :] & below(cand, r0))
            return jnp.where(cnt <= need, cand, v)

        cap = lax.fori_loop(0, 15, cap_step, jnp.zeros((1, tq), I32))
        cap = jnp.where((u != 0) & (n_gt + n_eq > k_f), cap, 2 ** 30)

        def trim(sb, carry):
            r0 = pl.multiple_of(sb * sweep_rows, sweep_rows)
            eq_scr[pl.ds(r0, sweep_rows), :] = eq_scr[pl.ds(r0, sweep_rows), :] & below(cap, r0)
            return carry

        lax.fori_loop(0, n_sweep, trim, 0)

    def emit(c, carry):
        off = pl.multiple_of(c * SEL_CHUNK, SEL_CHUNK)
        for g in range(groups):
            wrow = pl.multiple_of(c * words + g * 8, 8)
            keep = gt_scr[pl.ds(wrow, 8), :] | (eq_scr[pl.ds(wrow, 8), :] & some)
            rows = [jnp.where((lax.shift_right_logical(keep, 31 - r) & 1) != 0, 0.0, MASK_NEG) for r in range(32)]
            o_ref[0, pl.ds(pl.multiple_of(off + g * BIT_GROUP, BIT_GROUP), BIT_GROUP), :] = (
                jnp.concatenate(rows, axis=0).astype(o_ref.dtype))
        return carry

    lax.fori_loop(0, n_chunks, emit, 0)

    def blank(c, carry):
        off = pl.multiple_of(c * SEL_CHUNK, SEL_CHUNK)
        o_ref[0, pl.ds(off, SEL_CHUNK), :] = jnp.full((SEL_CHUNK, tq), MASK_NEG, o_ref.dtype)
        return carry

    lax.fori_loop(n_chunks, o_ref.shape[1] // SEL_CHUNK, blank, 0)


def _select(z, zt, topk):
    bsz, seq, _ = z.shape
    tq = min(SEL_COLS, seq)
    assert seq % SEL_CHUNK == 0 and seq % tq == 0
    return pl.pallas_call(
        functools.partial(_select_body, topk=topk),
        grid=(bsz, seq // tq),
        in_specs=[pl.BlockSpec((1, ZT_QIDX_ROWS, tq), lambda b, i: (b, ZT_QIDX_OFF // ZT_QIDX_ROWS, i)),
                  pl.BlockSpec((1, seq, LANES), lambda b, i: (b, 0, Z_KIDX_OFF // LANES)),
                  pl.BlockSpec((1, BF16_ROWS, tq), lambda b, i: (b, ZT_W_OFF // BF16_ROWS, i))],
        out_specs=pl.BlockSpec((1, seq, tq), lambda b, i: (b, 0, i)),
        out_shape=jax.ShapeDtypeStruct((bsz, seq, seq), BF16),
        scratch_shapes=[pltpu.VMEM((32, seq // 32, tq), I32),
                        pltpu.VMEM((seq // 32, tq), I32), pltpu.VMEM((seq // 32, tq), I32)],
        compiler_params=_cparams(("arbitrary", "arbitrary")),
        name="dsa_select",
    )(zt, z, zt)


BIAS_INIT_ROWS = 8
ONES_ROWS = BF16_ROWS


def _init_bias_tiles(btile, bias_ref, head0, n_heads, ta):
    col = lax.broadcasted_iota(I32, (BIAS_INIT_ROWS, ta), 1)
    row0 = lax.broadcasted_iota(I32, (BIAS_INIT_ROWS, ta), 0)

    def body(r, carry):
        off = pl.multiple_of(r * BIAS_INIT_ROWS, BIAS_INIT_ROWS)
        for kind in range(2):
            dist = col - (row0 + off) + kind * ta
            for h in range(n_heads):
                far = bias_ref[NUM_BUCKETS - 1, head0 + h]
                val = jnp.full((BIAS_INIT_ROWS, ta), (bias_ref[0, head0 + h] - far) * LOG2E, F32)
                for b in range(1, NUM_BUCKETS - 1):
                    val = jnp.where(dist >= BUCKET_START[b], (bias_ref[b, head0 + h] - far) * LOG2E, val)
                val = jnp.where(dist >= BUCKET_START[NUM_BUCKETS - 1], 0.0, val)
                btile[h, kind, pl.ds(off, BIAS_INIT_ROWS), :] = val
        return carry

    lax.fori_loop(0, ta // BIAS_INIT_ROWS, body, 0)


def _with_ones(vt):
    return jnp.concatenate([vt, jnp.ones((ONES_ROWS, vt.shape[1]), vt.dtype)], axis=0)


def _softmax_step_t(logits, v_aug, m_ref, acc_ref, idx):
    m_old = m_ref[idx]
    m_new = jnp.maximum(m_old, jnp.max(logits, axis=0, keepdims=True))
    m_safe = jnp.where(m_new == -jnp.inf, 0.0, m_new)
    p = jnp.exp2(logits - m_safe)
    alpha = jnp.exp2(m_old - m_safe)
    acc_ref[idx] = alpha * acc_ref[idx] + jnp.dot(v_aug, p.astype(BF16), preferred_element_type=F32)
    m_ref[idx] = m_new


def _reset_softmax(m_s, acc_s):
    m_s[...] = jnp.full(m_s.shape, -jnp.inf, F32)
    acc_s[...] = jnp.zeros(acc_s.shape, F32)


def _cattn_body(qi_ref, ki_ref, bias_ref, qt_ref, zk_ref, vt_ref, mask_ref, o_ref, m_s, acc_s, btile):
    b = pl.program_id(0)
    p = pl.program_id(1)
    qi = qi_ref[p]
    ki = ki_ref[p]
    ta = zk_ref.shape[1]
    hd = C_HEAD_DIM

    @pl.when((b == 0) & (p == 0))
    def _():
        _init_bias_tiles(btile, bias_ref, 0, C_HEADS, ta)

    @pl.when(ki == 0)
    def _():
        _reset_softmax(m_s, acc_s)

    def heads(extra_of_head):
        for h in range(C_HEADS):
            lo = h * hd
            logits = (jnp.dot(zk_ref[0, :, lo:lo + hd], qt_ref[0, lo:lo + hd, :], preferred_element_type=F32)
                      + extra_of_head(h))
            _softmax_step_t(logits, _with_ones(vt_ref[0, lo:lo + hd, :]), m_s, acc_s, h)

    near = ki >= qi - 1

    @pl.when(near)
    def _():
        heads(lambda h: mask_ref[0].astype(F32) + btile[h, qi - ki])

    @pl.when(jnp.logical_not(near))
    def _():
        heads(lambda h: mask_ref[0].astype(F32))

    @pl.when(ki == qi)
    def _():
        outs = []
        for h in range(C_HEADS):
            a = acc_s[h]
            outs.append((a[0:hd] / a[hd:hd + 1]).T)
        o_ref[0] = jnp.concatenate(outs, axis=1).astype(o_ref.dtype)


def _dattn_body(qi_ref, ki_ref, bias_ref, qt_ref, zk_ref, vt_ref, lam_ref, ng_ref, o_ref,
                m_s, acc_s, btile, *, lambda_init):
    b = pl.program_id(0)
    p = pl.program_id(1)
    qi = qi_ref[p]
    ki = ki_ref[p]
    ta = zk_ref.shape[1]
    hd = DIFF_HEAD_DIM
    dv = 2 * hd

    @pl.when((b == 0) & (p == 0))
    def _():
        _init_bias_tiles(btile, bias_ref, C_HEADS, DIFF_HEADS, ta)

    @pl.when(ki == 0)
    def _():
        _reset_softmax(m_s, acc_s)

    def heads(extra_of_head):
        for h in range(DIFF_HEADS):
            v_aug = _with_ones(vt_ref[0, dv * h:dv * (h + 1), :])
            extra = extra_of_head(h)
            for j in range(2):
                lo = (2 * h + j) * hd
                logits = jnp.dot(zk_ref[0, :, lo:lo + hd], qt_ref[0, lo:lo + hd, :], preferred_element_type=F32)
                if extra is not None:
                    logits = logits + extra
                _softmax_step_t(logits, v_aug, m_s, acc_s, 2 * h + j)

    near = ki >= qi - 1

    @pl.when(near)
    def _():
        krow = ki * ta + lax.broadcasted_iota(I32, (ta, ta), 0)
        qcol = qi * ta + lax.broadcasted_iota(I32, (ta, ta), 1)
        heads(lambda h: jnp.where(krow <= qcol, btile[h, qi - ki], MASK_NEG))

    @pl.when(jnp.logical_not(near))
    def _():
        heads(lambda h: None)

    @pl.when(ki == qi)
    def _():
        lam_p = lam_ref[...]
        lam = (jnp.exp(jnp.sum(lam_p[0:1] * lam_p[1:2], axis=1, keepdims=True))
               - jnp.exp(jnp.sum(lam_p[2:3] * lam_p[3:4], axis=1, keepdims=True)) + lambda_init)
        outs = []
        for h in range(DIFF_HEADS):
            a1 = acc_s[2 * h]
            a2 = acc_s[2 * h + 1]
            o = a1[0:dv] / a1[dv:dv + 1] - lam * (a2[0:dv] / a2[dv:dv + 1])
            o = o * lax.rsqrt(jnp.mean(o * o, axis=0, keepdims=True) + EPS) * ng_ref[...] * (1.0 - lambda_init)
            outs.append(o.T)
        o_ref[0] = jnp.concatenate(outs, axis=1).astype(o_ref.dtype)


def _causal_pairs(nq):
    qi = [q for q in range(nq) for _ in range(q + 1)]
    ki = [k for q in range(nq) for k in range(q + 1)]
    return jnp.asarray(qi, I32), jnp.asarray(ki, I32)


def _cattn(z, zt, mask, rel_bias, ta):
    bsz, seq, _ = z.shape
    qi, ki = _causal_pairs(seq // ta)
    qmap = lambda col: (lambda b, p, qi, ki: (b, qi[p], col))
    kmap = lambda col: (lambda b, p, qi, ki: (b, ki[p], col))
    dv_aug = C_HEAD_DIM + ONES_ROWS
    grid_spec = pltpu.PrefetchScalarGridSpec(
        num_scalar_prefetch=2,
        grid=(bsz, qi.shape[0]),
        in_specs=[pl.BlockSpec(memory_space=pltpu.SMEM),
                  pl.BlockSpec((1, C_WIDTH, ta), lambda b, p, qi, ki: (b, ZT_QC, qi[p])),
                  pl.BlockSpec((1, ta, C_WIDTH), kmap(Z_KC)),
                  pl.BlockSpec((1, C_WIDTH, ta), lambda b, p, qi, ki: (b, ZT_VC, ki[p])),
                  pl.BlockSpec((1, ta, ta), lambda b, p, qi, ki: (b, ki[p], qi[p]))],
        out_specs=pl.BlockSpec((1, ta, C_WIDTH), qmap(0)),
        scratch_shapes=[pltpu.VMEM((C_HEADS, 1, ta), F32),
                        pltpu.VMEM((C_HEADS, dv_aug, ta), F32),
                        pltpu.VMEM((C_HEADS, 2, ta, ta), F32)],
    )
    return pl.pallas_call(
        _cattn_body,
        grid_spec=grid_spec,
        out_shape=jax.ShapeDtypeStruct((bsz, seq, C_WIDTH), BF16),
        compiler_params=_cparams(("arbitrary", "arbitrary")),
        name="dsa_attn",
    )(qi, ki, rel_bias, zt, z, zt, mask)


def _dattn(z, zt, rel_bias, diff_lam, diff_norm_g, lambda_init, ta):
    bsz, seq, _ = z.shape
    qi, ki = _causal_pairs(seq // ta)
    n_maps = 2 * DIFF_HEADS
    dv = 2 * DIFF_HEAD_DIM
    qmap = lambda col: (lambda b, p, qi, ki: (b, qi[p], col))
    kmap = lambda col: (lambda b, p, qi, ki: (b, ki[p], col))
    grid_spec = pltpu.PrefetchScalarGridSpec(
        num_scalar_prefetch=2,
        grid=(bsz, qi.shape[0]),
        in_specs=[pl.BlockSpec(memory_space=pltpu.SMEM),
                  pl.BlockSpec((1, DIFF_W, ta), lambda b, p, qi, ki: (b, ZT_QD, qi[p])),
                  pl.BlockSpec((1, ta, DIFF_W), kmap(Z_KD)),
                  pl.BlockSpec((1, DIFF_W, ta), lambda b, p, qi, ki: (b, ZT_VD, ki[p])),
                  pl.BlockSpec(diff_lam.shape, lambda b, p, qi, ki: (0, 0)),
                  pl.BlockSpec((dv, 1), lambda b, p, qi, ki: (0, 0))],
        out_specs=pl.BlockSpec((1, ta, DIFF_W), qmap(0)),
        scratch_shapes=[pltpu.VMEM((n_maps, 1, ta), F32),
                        pltpu.VMEM((n_maps, dv + ONES_ROWS, ta), F32),
                        pltpu.VMEM((DIFF_HEADS, 2, ta, ta), F32)],
    )
    return pl.pallas_call(
        functools.partial(_dattn_body, lambda_init=lambda_init),
        grid_spec=grid_spec,
        out_shape=jax.ShapeDtypeStruct((bsz, seq, DIFF_W), BF16),
        compiler_params=_cparams(("arbitrary", "arbitrary")),
        name="diff_attn",
    )(qi, ki, rel_bias, zt, z, zt, diff_lam, diff_norm_g.reshape(dv, 1))


def _out_proj_body(oc_ref, od_ref, x_ref, g1_ref, w_ref, o_ref):
    y = (jnp.dot(oc_ref[0], w_ref[0:C_WIDTH, :], preferred_element_type=F32)
         + jnp.dot(od_ref[0], w_ref[C_WIDTH:C_WIDTH + DIFF_W, :], preferred_element_type=F32))
    o_ref[0] = x_ref[0] + g1_ref[0] * y


def _out_proj(out_c, out_d, x, g1, w_out_bf16, tm):
    bsz, seq, d = x.shape
    return pl.pallas_call(
        _out_proj_body,
        grid=(bsz, seq // tm),
        in_specs=[pl.BlockSpec((1, tm, C_WIDTH), lambda b, i: (b, i, 0)),
                  pl.BlockSpec((1, tm, DIFF_W), lambda b, i: (b, i, 0)),
                  pl.BlockSpec((1, tm, d), lambda b, i: (b, i, 0)),
                  pl.BlockSpec((1, 1, d), lambda b, i: (b, 0, 0)),
                  pl.BlockSpec((C_WIDTH + DIFF_W, d), lambda b, i: (0, 0))],
        out_specs=pl.BlockSpec((1, tm, d), lambda b, i: (b, i, 0)),
        out_shape=jax.ShapeDtypeStruct((bsz, seq, d), F32),
        compiler_params=_cparams(("arbitrary", "arbitrary")),
        name="attn_out_proj",
    )(out_c, out_d, x, g1.reshape(bsz, 1, d), w_out_bf16)


def _attn_in_weights(cd_w_in):
    sizes = (C_WIDTH, C_WIDTH, C_WIDTH, IDX_HEADS * IDX_DIM, IDX_DIM, IDX_HEADS, DIFF_W, DIFF_W, DIFF_W)
    cuts = np.cumsum(sizes)[:-1]
    q_c, k_c, v_c, q_i, k_i, w_i, q_d, k_d, v_d = jnp.split(cd_w_in, cuts, axis=1)
    d = cd_w_in.shape[0]
    w = jnp.concatenate([k_c, k_d, k_i, jnp.zeros((d, LANES - IDX_DIM), cd_w_in.dtype)], axis=1)
    wt = jnp.concatenate([v_c, v_d, q_c * (C_HEAD_DIM ** -0.5 * LOG2E), q_d * (DIFF_HEAD_DIM ** -0.5 * LOG2E), q_i,
                          w_i * (IDX_DIM * IDX_HEADS) ** -0.5,
                          jnp.zeros((d, BF16_ROWS - IDX_HEADS), cd_w_in.dtype)], axis=1).T
    return w.astype(BF16), wt.astype(BF16)


def kernel(x, c, positions, rel_bias, norm_g, final_norm_g, ada_w, ada_b, ab_w_in, ab_conv_a, ab_conv_b,
           ab_conv_b_bias, ab_ln_g, ab_ln_b, ab_w_out, cd_w_in, diff_lam, diff_norm_g, cd_w_out,
           moe_wr_g, moe_br_g, moe_wr_e, moe_br_e, moe_w_gate, moe_w_up, moe_w_down):
    del positions
    bsz, seq, d = x.shape
    depth = ada_w.shape[0]
    tm = min(512, seq)
    ta = min(ATT_TILE, seq)
    assert ta >= MAX_DISTANCE and seq % ta == 0
    topk = min(TOPK_MAX, seq // 4)
    mods = _ada_mod(c, ada_w, ada_b)
    for i in range(depth):
        sh1, sc1, g1, sh2, sc2, g2 = jnp.split(mods[i], 6, axis=-1)
        j = i // 2
        if i % 2 == 0:
            z = _norm_proj(x, norm_g[i, 0], sh1, sc1, ab_w_in[j].astype(BF16), tm)
            x = _conv_mix(z, x, g1, ab_conv_a[j], ab_conv_b[j], ab_conv_b_bias[j], ab_ln_g[j], ab_ln_b[j],
                          ab_w_out[j].astype(BF16), min(256, seq))
        else:
            lambda_init = 0.8 - 0.6 * math.exp(-0.3 * i)
            w, wt = _attn_in_weights(cd_w_in[j])
            z, zt = _norm_proj(x, norm_g[i, 0], sh1, sc1, w, tm, wt)
            mask = _select(z, zt, topk)
            out_c = _cattn(z, zt, mask, rel_bias, ta)
            out_d = _dattn(z, zt, rel_bias, diff_lam[j], diff_norm_g[j], lambda_init, ta)
            x = _out_proj(out_c, out_d, x, g1, cd_w_out[j].astype(BF16), tm)
        x = _hier_moe(x, norm_g[i, 1], sh2, sc2, g2, moe_wr_g[i], moe_br_g[i], moe_wr_e[i], moe_br_e[i],
                      moe_w_gate, moe_w_up, moe_w_down, i, final_norm_g, final_norm=(i == depth - 1))
    return x
```

```python
import functools
import math

import numpy as np
import jax
import jax.numpy as jnp
from jax import lax
from jax.experimental import pallas as pl
from jax.experimental.pallas import tpu as pltpu

F32 = jnp.float32
BF16 = jnp.bfloat16
I32 = jnp.int32
HIGHEST = lax.Precision.HIGHEST

EPS = 1e-6
A_WIDTH = 512
A_CONV = 3
B_WIDTH = 512
B_CONV = 31
C_HEADS = 8
C_HEAD_DIM = 64
IDX_HEADS = 8
IDX_DIM = 32
TOPK_MAX = 256
DIFF_HEADS = 4
DIFF_HEAD_DIM = 64
NUM_BUCKETS = 32
MAX_DISTANCE = 128
N_GROUPS = 4
EXPERTS_PER_GROUP = 8
N_EXPERTS = N_GROUPS * EXPERTS_PER_GROUP
C_WIDTH = C_HEADS * C_HEAD_DIM
DIFF_W = DIFF_HEADS * 2 * DIFF_HEAD_DIM
LANES = 128
BF16_ROWS = 16
INT_MIN = -(2 ** 31)
LOG2E = math.log2(math.e)
VMEM_LIMIT = 56 * 1024 * 1024

Z_KC, Z_KD = 0, 1
Z_KIDX_OFF = C_WIDTH + DIFF_W
Z_COLS = Z_KIDX_OFF + LANES
ZT_VC, ZT_VD, ZT_QC, ZT_QD = 0, 1, 2, 3
ZT_QIDX_OFF = 2 * C_WIDTH + 2 * DIFF_W
ZT_QIDX_ROWS = IDX_HEADS * IDX_DIM
ZT_W_OFF = ZT_QIDX_OFF + ZT_QIDX_ROWS
ZT_ROWS = ZT_W_OFF + BF16_ROWS


def _bucket_starts():
    n = np.arange(0, 2 * MAX_DISTANCE)
    me = NUM_BUCKETS // 2
    lr = np.log(np.maximum(n, 1) / me) / math.log(MAX_DISTANCE / me)
    large = me + (lr * (NUM_BUCKETS - me)).astype(np.int64)
    b = np.where(n < me, n, np.minimum(large, NUM_BUCKETS - 1))
    return [int(n[b >= k].min()) for k in range(NUM_BUCKETS)]


BUCKET_START = _bucket_starts()


def _cparams(sem):
    return pltpu.CompilerParams(dimension_semantics=sem, vmem_limit_bytes=VMEM_LIMIT)


def _rms(x):
    return x * lax.rsqrt(jnp.mean(x * x, axis=-1, keepdims=True) + EPS)


def _sigmoid(x):
    return 1.0 / (1.0 + jnp.exp(-x))


def _dot_nt(a, b):
    return lax.dot_general(a, b, (((1,), (1,)), ((), ())), preferred_element_type=F32)


def _ada_body(c_ref, w_ref, b_ref, o_ref):
    c = c_ref[...]
    cond = c * _sigmoid(c)
    o_ref[0] = jnp.dot(cond, w_ref[0], precision=HIGHEST, preferred_element_type=F32) + b_ref[0]


def _ada_mod(c, ada_w, ada_b):
    depth, d, n6 = ada_w.shape
    bsz = c.shape[0]
    rows = 8
    c_pad = jnp.zeros((rows, d), F32).at[:bsz].set(c)
    tn = 1536
    out = pl.pallas_call(
        _ada_body,
        grid=(depth, n6 // tn),
        in_specs=[pl.BlockSpec((rows, d), lambda i, j: (0, 0)),
                  pl.BlockSpec((1, d, tn), lambda i, j: (i, 0, j)),
                  pl.BlockSpec((1, 1, tn), lambda i, j: (i, 0, j))],
        out_specs=pl.BlockSpec((1, rows, tn), lambda i, j: (i, 0, j)),
        out_shape=jax.ShapeDtypeStruct((depth, rows, n6), F32),
        compiler_params=_cparams(("arbitrary", "arbitrary")),
        name="ada_mod",
    )(c_pad, ada_w, ada_b.reshape(depth, 1, n6))
    return out[:, :bsz]


def _norm_proj_body(x_ref, g_ref, sh_ref, sc_ref, w_ref, *rest):
    y = _rms(x_ref[0]) * g_ref[...]
    h = (y * (1.0 + sc_ref[0]) + sh_ref[0]).astype(BF16)
    if len(rest) == 1:
        (o_ref,) = rest
    else:
        wt_ref, o_ref, ot_ref = rest
        ot_ref[0] = _dot_nt(wt_ref[...], h).astype(ot_ref.dtype)
    o_ref[0] = jnp.dot(h, w_ref[...], preferred_element_type=F32).astype(o_ref.dtype)


def _norm_proj(x, g, sh, sc, w_bf16, tm, wt_bf16=None):
    bsz, seq, d = x.shape
    n = w_bf16.shape[1]
    in_specs = [pl.BlockSpec((1, tm, d), lambda b, i: (b, i, 0)),
                pl.BlockSpec((1, d), lambda b, i: (0, 0)),
                pl.BlockSpec((1, 1, d), lambda b, i: (b, 0, 0)),
                pl.BlockSpec((1, 1, d), lambda b, i: (b, 0, 0)),
                pl.BlockSpec((d, n), lambda b, i: (0, 0))]
    out_specs = pl.BlockSpec((1, tm, n), lambda b, i: (b, i, 0))
    out_shape = jax.ShapeDtypeStruct((bsz, seq, n), BF16)
    args = [x, g.reshape(1, d), sh.reshape(bsz, 1, d), sc.reshape(bsz, 1, d), w_bf16]
    if wt_bf16 is not None:
        nt = wt_bf16.shape[0]
        in_specs.append(pl.BlockSpec((nt, d), lambda b, i: (0, 0)))
        out_specs = [out_specs, pl.BlockSpec((1, nt, tm), lambda b, i: (b, 0, i))]
        out_shape = [out_shape, jax.ShapeDtypeStruct((bsz, nt, seq), BF16)]
        args.append(wt_bf16)
    return pl.pallas_call(
        _norm_proj_body,
        grid=(bsz, seq // tm),
        in_specs=in_specs,
        out_specs=out_specs,
        out_shape=out_shape,
        compiler_params=_cparams(("arbitrary", "arbitrary")),
        name="norm_proj",
    )(*args)


CONV_HALO = 32
CONV_ROWS = 64


def _conv_body(z_ref, x_ref, g1_ref, ca_ref, cb_ref, cbb_ref, lng_ref, lnb_ref, wo_ref, o_ref,
               ua_scr, ub_scr, y_scr, *, tl):
    l = pl.program_id(1)

    @pl.when(l == 0)
    def _():
        ua_scr[0:CONV_HALO, :] = jnp.zeros((CONV_HALO, A_WIDTH), F32)
        ub_scr[0:CONV_HALO, :] = jnp.zeros((CONV_HALO, B_WIDTH), F32)

    a = A_WIDTH
    gate_c = z_ref[0, :, a:2 * a].astype(F32)
    x_a = z_ref[0, :, 2 * a:3 * a].astype(F32)
    ua_scr[CONV_HALO:CONV_HALO + tl, :] = gate_c * x_a
    val_b = z_ref[0, :, 3 * a:3 * a + B_WIDTH].astype(F32)
    glu = z_ref[0, :, 3 * a + B_WIDTH:3 * a + 2 * B_WIDTH].astype(F32)
    ub_scr[CONV_HALO:CONV_HALO + tl, :] = val_b * _sigmoid(glu)

    for r in range(0, tl, CONV_ROWS):
        acc_a = None
        for k in range(A_CONV):
            tap = ua_scr[CONV_HALO + r - (A_CONV - 1) + k:CONV_HALO + r - (A_CONV - 1) + k + CONV_ROWS, :]
            term = tap * ca_ref[k:k + 1, :]
            acc_a = term if acc_a is None else acc_a + term
        gate_b = z_ref[0, r:r + CONV_ROWS, 0:a].astype(F32)
        y_scr[r:r + CONV_ROWS, 0:a] = (gate_b * acc_a).astype(BF16)

        win = CONV_HALO + r - 8
        acc_b = None
        for b in range(8):
            phase = None
            for back in range(b, B_CONV, 8):
                tap = ub_scr[win - (back - b):win - (back - b) + CONV_ROWS + 8, :]
                term = tap * cb_ref[B_CONV - 1 - back:B_CONV - back, :]
                phase = term if phase is None else phase + term
            piece = phase[8 - b:8 - b + CONV_ROWS]
            acc_b = piece if acc_b is None else acc_b + piece
        u = acc_b + cbb_ref[...]
        mu = jnp.mean(u, axis=-1, keepdims=True)
        uc = u - mu
        var = jnp.mean(uc * uc, axis=-1, keepdims=True)
        v = uc * lax.rsqrt(var + EPS) * lng_ref[...] + lnb_ref[...]
        y_scr[r:r + CONV_ROWS, a:a + B_WIDTH] = (v * _sigmoid(v)).astype(BF16)

    ua_scr[0:CONV_HALO, :] = ua_scr[tl:tl + CONV_HALO, :]
    ub_scr[0:CONV_HALO, :] = ub_scr[tl:tl + CONV_HALO, :]
    y = jnp.dot(y_scr[...], wo_ref[...], preferred_element_type=F32)
    o_ref[0] = x_ref[0] + g1_ref[0] * y


def _conv_mix(z, x, g1, conv_a, conv_b, conv_b_bias, ln_g, ln_b, w_out_bf16, tl):
    bsz, seq, d = x.shape
    nz = z.shape[-1]
    wide = A_WIDTH + B_WIDTH
    full = lambda shape: pl.BlockSpec(shape, lambda b, l: (0,) * len(shape))
    return pl.pallas_call(
        functools.partial(_conv_body, tl=tl),
        grid=(bsz, seq // tl),
        in_specs=[pl.BlockSpec((1, tl, nz), lambda b, l: (b, l, 0)),
                  pl.BlockSpec((1, tl, d), lambda b, l: (b, l, 0)),
                  pl.BlockSpec((1, 1, d), lambda b, l: (b, 0, 0)),
                  full((A_CONV, A_WIDTH)), full((B_CONV, B_WIDTH)), full((1, B_WIDTH)),
                  full((1, B_WIDTH)), full((1, B_WIDTH)), full((wide, d))],
        out_specs=pl.BlockSpec((1, tl, d), lambda b, l: (b, l, 0)),
        out_shape=jax.ShapeDtypeStruct((bsz, seq, d), F32),
        scratch_shapes=[pltpu.VMEM((CONV_HALO + tl, A_WIDTH), F32),
                        pltpu.VMEM((CONV_HALO + tl, B_WIDTH), F32),
                        pltpu.VMEM((tl, wide), BF16)],
        compiler_params=_cparams(("arbitrary", "arbitrary")),
        name="conv_mix",
    )(z, x, g1.reshape(bsz, 1, d), conv_a, conv_b, conv_b_bias.reshape(1, -1), ln_g.reshape(1, -1),
      ln_b.reshape(1, -1), w_out_bf16)


MOE_ROWS = 512
ROUTER_ROWS = 512
META_GATE0, META_GATE1, META_POS0, META_POS1 = range(4)
SEG_LEN, SEG_BASE, SEG_OFF = range(3)
SEG_ALIGN = 8
SEG_SIZES = tuple(2 ** b for b in range(10, 2, -1))
GROUP_LANE0 = N_EXPERTS


def _router_body(x_ref, g_ref, sh_ref, sc_ref, wr_ref, br_ref, tri_ref, upper_ref, h_ref, meta_ref, post_ref, seg_ref,
                 cnt_ref, base_scr):
    @pl.when(pl.program_id(0) == 0)
    def _():
        base_scr[...] = jnp.zeros_like(base_scr)

    h = _rms(x_ref[...]) * g_ref[...]
    h = h * (1.0 + sc_ref[0]) + sh_ref[0]
    h_ref[...] = h.astype(h_ref.dtype)
    h_hi = h.astype(BF16)
    h_lo = (h - h_hi.astype(F32)).astype(BF16)
    logits = (jnp.dot(h_hi, wr_ref[0], preferred_element_type=F32) + jnp.dot(h_hi, wr_ref[1], preferred_element_type=F32)
              + jnp.dot(h_lo, wr_ref[0], preferred_element_type=F32) + br_ref[...])
    tr = logits.shape[0]
    lane = lax.broadcasted_iota(I32, (tr, LANES), 1)
    lane_f = lane.astype(F32)
    neg = jnp.float32(-jnp.inf)
    big = jnp.float32(1e9)

    is_group = (lane >= GROUP_LANE0) & (lane < GROUP_LANE0 + N_GROUPS)
    glog = jnp.where(is_group, logits, neg)
    gmax = jnp.max(glog, axis=1, keepdims=True)
    p_top = 1.0 / jnp.sum(jnp.exp(glog - gmax), axis=1, keepdims=True)
    g_sel = jnp.min(jnp.where(glog == gmax, lane_f, big), axis=1, keepdims=True) - GROUP_LANE0
    lo = g_sel * EXPERTS_PER_GROUP
    in_group = (lane_f >= lo) & (lane_f < lo + EXPERTS_PER_GROUP)
    f1 = jnp.where(in_group, logits, neg)
    v1 = jnp.max(f1, axis=1, keepdims=True)
    i1 = jnp.min(jnp.where(f1 == v1, lane_f, big), axis=1, keepdims=True)
    f2 = jnp.where(lane_f == i1, neg, f1)
    v2 = jnp.max(f2, axis=1, keepdims=True)
    i2 = jnp.min(jnp.where(f2 == v2, lane_f, big), axis=1, keepdims=True)
    a = jnp.exp(v2 - v1)
    w1 = 1.0 / (1.0 + a)
    gate0 = p_top * w1
    gate1 = p_top * (a * w1)

    oh0 = lane_f == i1
    oh1 = lane_f == i2
    ind0 = jnp.where(oh0, 1.0, 0.0)
    ind1 = jnp.where(oh1, 1.0, 0.0)
    pre0 = jnp.dot(tri_ref[...], ind0.astype(BF16), preferred_element_type=F32)
    pre1 = jnp.dot(tri_ref[...], ind1.astype(BF16), preferred_element_type=F32)
    tot0 = jnp.sum(ind0, axis=0, keepdims=True)
    tot1 = jnp.sum(ind1, axis=0, keepdims=True)
    seg_len = jnp.floor((tot0 + tot1 + (SEG_ALIGN - 1)) * (1.0 / SEG_ALIGN)) * SEG_ALIGN
    seg_off = jnp.dot(jnp.broadcast_to(seg_len, (8, LANES)), upper_ref[...], precision=HIGHEST,
                      preferred_element_type=F32)[0:1]
    pos0 = jnp.sum(jnp.where(oh0, seg_off + pre0, 0.0), axis=1, keepdims=True)
    pos1 = jnp.sum(jnp.where(oh1, seg_off + tot0 + pre1, 0.0), axis=1, keepdims=True)
    base = base_scr[...]
    new_base = base + seg_len
    base_scr[...] = new_base
    cnt_ref[...] = new_base

    meta = jnp.zeros((tr, LANES), F32)
    for col, val in ((META_GATE0, gate0), (META_GATE1, gate1), (META_POS0, pos0), (META_POS1, pos1)):
        meta = jnp.where(lane == col, val, meta)
    meta_ref[...] = meta
    post_ref[0] = meta.T[0:8]
    srow = lax.broadcasted_iota(I32, (8, LANES), 0)
    seg = jnp.where(srow == SEG_LEN, seg_len, jnp.where(srow == SEG_BASE, base, jnp.where(srow == SEG_OFF, seg_off, 0.0)))
    seg_ref[0] = seg.astype(I32)


def _moe_router(x2, g, sh, sc, wr_g, br_g, wr_e, br_e, seq):
    n_tok, d = x2.shape
    bsz = n_tok // seq
    tr = min(ROUTER_ROWS, seq)
    steps_per_batch = seq // tr
    wr = jnp.zeros((d, LANES), F32).at[:, :N_EXPERTS].set(wr_e).at[:, GROUP_LANE0:GROUP_LANE0 + N_GROUPS].set(wr_g)
    wr_hi = wr.astype(BF16)
    wr = jnp.stack([wr_hi, (wr - wr_hi.astype(F32)).astype(BF16)])
    br = jnp.zeros((1, LANES), F32).at[0, :N_EXPERTS].set(br_e).at[0, GROUP_LANE0:GROUP_LANE0 + N_GROUPS].set(br_g)
    tri = jnp.tril(jnp.ones((tr, tr), BF16), -1)
    upper = jnp.triu(jnp.ones((LANES, LANES), F32), 1)
    nb = n_tok // tr
    full = lambda shape: pl.BlockSpec(shape, lambda i: (0,) * len(shape))
    return pl.pallas_call(
        _router_body,
        grid=(nb,),
        in_specs=[pl.BlockSpec((tr, d), lambda i: (i, 0)),
                  full((1, d)),
                  pl.BlockSpec((1, 1, d), lambda i: (i // steps_per_batch, 0, 0)),
                  pl.BlockSpec((1, 1, d), lambda i: (i // steps_per_batch, 0, 0)),
                  full((2, d, LANES)), full((1, LANES)), full((tr, tr)), full((LANES, LANES))],
        out_specs=[pl.BlockSpec((tr, d), lambda i: (i, 0)),
                   pl.BlockSpec((tr, LANES), lambda i: (i, 0)),
                   pl.BlockSpec((1, 8, tr), lambda i: (i, 0, 0)),
                   pl.BlockSpec((1, 8, LANES), lambda i: (i, 0, 0)),
                   full((1, LANES))],
        out_shape=[jax.ShapeDtypeStruct((n_tok, d), BF16),
                   jax.ShapeDtypeStruct((n_tok, LANES), F32),
                   jax.ShapeDtypeStruct((nb, 8, tr), F32),
                   jax.ShapeDtypeStruct((nb, 8, LANES), I32),
                   jax.ShapeDtypeStruct((1, LANES), F32)],
        scratch_shapes=[pltpu.VMEM((1, LANES), F32)],
        compiler_params=_cparams(("arbitrary",)),
        name="moe_router",
    )(x2, g.reshape(1, d), sh.reshape(bsz, 1, d), sc.reshape(bsz, 1, d), wr, br, tri, upper)


def _segment_copies(seg_ref, starts_ref, make_copy, action):
    def per_expert(e, carry):
        length = seg_ref[0, SEG_LEN, e]
        local = seg_ref[0, SEG_OFF, e]
        glob = starts_ref[e] + seg_ref[0, SEG_BASE, e]
        for size in SEG_SIZES:
            hit = (length & size) != 0

            @pl.when(hit)
            def _(local=local, glob=glob, size=size):
                cp = make_copy(pl.multiple_of(local, SEG_ALIGN), pl.multiple_of(glob, SEG_ALIGN), size)
                cp.start() if action == "start" else cp.wait()

            step = jnp.where(hit, size, 0)
            local = local + step
            glob = glob + step
        return carry

    lax.fori_loop(0, N_EXPERTS, per_expert, 0)


def _local_rows(tr):
    return 2 * tr + N_EXPERTS * SEG_ALIGN


def _row_words(d):
    return d // 2 + LANES


def _dispatch_body(starts_ref, seg_ref, prev_seg_ref, h_ref, post_ref, xr_ref, xs_scr, zero_scr, sems, *, tr):
    i = pl.program_id(0)
    slot = i % 2
    lb = xs_scr.shape[1]
    sem = sems.at[0]

    def make_copy_of(s):
        def make_copy(local, glob, size):
            return pltpu.make_async_copy(xs_scr.at[s, pl.ds(local, size)], xr_ref.at[pl.ds(glob, size)], sems.at[s])
        return make_copy

    rows = lax.broadcasted_iota(I32, (lb, tr), 0)
    first = rows == post_ref[0, META_POS0:META_POS0 + 1, :].astype(I32)
    second = rows == post_ref[0, META_POS1:META_POS1 + 1, :].astype(I32)
    x = jnp.dot(jnp.where(first | second, 1.0, 0.0).astype(BF16), h_ref[...], preferred_element_type=F32)
    bits = lax.bitcast_convert_type(x, I32)
    half = x.shape[1] // 2
    xs_scr[slot, :, 0:half] = lax.shift_right_logical(bits[:, 0:half], 16) | (bits[:, half:] & jnp.int32(-65536))
    gate = jnp.sum(jnp.where(first, post_ref[0, META_GATE0:META_GATE0 + 1, :], 0.0)
                   + jnp.where(second, post_ref[0, META_GATE1:META_GATE1 + 1, :], 0.0), axis=1, keepdims=True)
    xs_scr[slot, :, half:half + LANES] = jnp.broadcast_to(lax.bitcast_convert_type(gate, I32), (lb, LANES))
    _segment_copies(seg_ref, starts_ref, make_copy_of(slot), "start")

    @pl.when(i > 0)
    def _():
        _segment_copies(prev_seg_ref, starts_ref, make_copy_of(1 - slot), "wait")

    @pl.when(i == pl.num_programs(0) - 1)
    def _():
        _segment_copies(seg_ref, starts_ref, make_copy_of(slot), "wait")
        zero_scr[...] = jnp.zeros_like(zero_scr)
        total = starts_ref[N_EXPERTS]
        tail = (-total) & (MOE_ROWS - 1)
        for action in ("start", "wait"):
            row = total
            for size in SEG_SIZES:
                if size >= MOE_ROWS:
                    continue
                hit = (tail & size) != 0

                @pl.when(hit)
                def _(row=row, size=size, action=action):
                    cp = pltpu.make_async_copy(zero_scr.at[pl.ds(0, size)],
                                               xr_ref.at[pl.ds(pl.multiple_of(row, SEG_ALIGN), size)], sem)
                    cp.start() if action == "start" else cp.wait()

                row = row + jnp.where(hit, size, 0)

        def free_block(b):
            return pltpu.make_async_copy(
                zero_scr, xr_ref.at[pl.ds(pl.multiple_of(b * MOE_ROWS, MOE_ROWS), MOE_ROWS)], sem)

        first_free = (total + MOE_ROWS - 1) // MOE_ROWS
        n_blocks = xr_ref.shape[0] // MOE_ROWS
        lax.fori_loop(first_free, n_blocks, lambda b, c: (free_block(b).start(), c)[1], 0)
        lax.fori_loop(first_free, n_blocks, lambda b, c: (free_block(b).wait(), c)[1], 0)


def _moe_dispatch(h2, post, seg, starts, n_rows, tr):
    n_tok, d = h2.shape
    grid_spec = pltpu.PrefetchScalarGridSpec(
        num_scalar_prefetch=1,
        grid=(n_tok // tr,),
        in_specs=[pl.BlockSpec((1, 8, LANES), lambda i, s: (i, 0, 0), memory_space=pltpu.SMEM),
                  pl.BlockSpec((1, 8, LANES), lambda i, s: (jnp.maximum(i - 1, 0), 0, 0), memory_space=pltpu.SMEM),
                  pl.BlockSpec((tr, d), lambda i, s: (i, 0)),
                  pl.BlockSpec((1, 8, tr), lambda i, s: (i, 0, 0))],
        out_specs=pl.BlockSpec(memory_space=pl.ANY),
        scratch_shapes=[pltpu.VMEM((2, _local_rows(tr), _row_words(d)), I32), pltpu.VMEM((MOE_ROWS, _row_words(d)), I32),
                        pltpu.SemaphoreType.DMA((2,))],
    )
    return pl.pallas_call(
        functools.partial(_dispatch_body, tr=tr),
        grid_spec=grid_spec,
        out_shape=jax.ShapeDtypeStruct((n_rows, _row_words(d)), I32),
        compiler_params=_cparams(("arbitrary",)),
        name="moe_dispatch",
    )(starts, seg, seg, h2, post)


def _expert_body(pb_ref, pe_ref, plo_ref, phi_ref, x_ref, wg_ref, wu_ref, wd_ref, o_ref, wg_s, wu_s, wd_s):
    p = pl.program_id(0)
    prev = jnp.maximum(p - 1, 0)
    new_expert = (p == 0) | (pe_ref[p] != pe_ref[prev])
    first = (p == 0) | (pb_ref[p] != pb_ref[prev])

    @pl.when(new_expert)
    def _():
        wg_s[...] = wg_ref[0, 0].astype(BF16)
        wu_s[...] = wu_ref[0, 0].astype(BF16)
        wd_s[...] = wd_ref[0, 0].astype(BF16)

    def rows_of_expert():
        half = wg_s.shape[0] // 2
        words = x_ref[:, 0:half]
        x_lo = lax.bitcast_convert_type(words << 16, F32).astype(BF16)
        x_hi = lax.bitcast_convert_type(words & jnp.int32(-65536), F32).astype(BF16)
        gate = lax.bitcast_convert_type(x_ref[:, half:half + 1], F32)
        gt = (jnp.dot(x_lo, wg_s[0:half], preferred_element_type=F32)
              + jnp.dot(x_hi, wg_s[half:], preferred_element_type=F32))
        up = (jnp.dot(x_lo, wu_s[0:half], preferred_element_type=F32)
              + jnp.dot(x_hi, wu_s[half:], preferred_element_type=F32))
        act = (gt * _sigmoid(gt)) * up
        y = jnp.dot(act.astype(BF16), wd_s[...], preferred_element_type=F32) * gate
        rows = lax.broadcasted_iota(I32, (y.shape[0], 1), 0)
        return jnp.where((rows >= plo_ref[p]) & (rows < phi_ref[p]), y, 0.0)

    nonempty = phi_ref[p] > plo_ref[p]

    @pl.when(first & nonempty)
    def _():
        o_ref[...] = rows_of_expert()

    @pl.when(first & jnp.logical_not(nonempty))
    def _():
        o_ref[...] = jnp.zeros_like(o_ref)

    @pl.when(jnp.logical_not(first) & nonempty)
    def _():
        o_ref[...] += rows_of_expert()


def _moe_experts(x_rows, pairs, w_gate, w_up, w_down, layer):
    n_rows = x_rows.shape[0]
    d, de = w_gate.shape[-2:]
    n_pairs = pairs[0].shape[0]
    grid_spec = pltpu.PrefetchScalarGridSpec(
        num_scalar_prefetch=4,
        grid=(n_pairs,),
        in_specs=[pl.BlockSpec((MOE_ROWS, _row_words(d)), lambda p, pb, pe, lo, hi: (pb[p], 0)),
                  pl.BlockSpec((1, 1, d, de), lambda p, pb, pe, lo, hi: (layer, pe[p], 0, 0)),
                  pl.BlockSpec((1, 1, d, de), lambda p, pb, pe, lo, hi: (layer, pe[p], 0, 0)),
                  pl.BlockSpec((1, 1, de, d), lambda p, pb, pe, lo, hi: (layer, pe[p], 0, 0))],
        out_specs=pl.BlockSpec((MOE_ROWS, d), lambda p, pb, pe, lo, hi: (pb[p], 0)),
        scratch_shapes=[pltpu.VMEM((d, de), BF16), pltpu.VMEM((d, de), BF16), pltpu.VMEM((de, d), BF16)],
    )
    return pl.pallas_call(
        _expert_body,
        grid_spec=grid_spec,
        out_shape=jax.ShapeDtypeStruct((n_rows, d), F32),
        compiler_params=_cparams(("arbitrary",)),
        name="moe_experts",
    )(*pairs, x_rows, w_gate, w_up, w_down)


def _combine_body(starts_ref, seg_ref, next_seg_ref, y_ref, x_ref, meta_ref, g2_ref, fg_ref, o_ref, ys_scr, sems, *,
                  final_norm):
    i = pl.program_id(0)
    slot = i % 2

    def make_copy_of(s):
        def make_copy(local, glob, size):
            return pltpu.make_async_copy(y_ref.at[pl.ds(glob, size)], ys_scr.at[s, pl.ds(local, size)], sems.at[s])
        return make_copy

    @pl.when(i == 0)
    def _():
        ys_scr[...] = jnp.zeros_like(ys_scr)
        _segment_copies(seg_ref, starts_ref, make_copy_of(slot), "start")

    @pl.when(i + 1 < pl.num_programs(0))
    def _():
        _segment_copies(next_seg_ref, starts_ref, make_copy_of(1 - slot), "start")

    _segment_copies(seg_ref, starts_ref, make_copy_of(slot), "wait")

    meta = meta_ref[...]
    tr = meta.shape[0]
    y = ys_scr[slot].astype(BF16)
    cols = lax.broadcasted_iota(I32, (tr, ys_scr.shape[1]), 1)

    pick = ((cols == meta[:, META_POS0:META_POS0 + 1].astype(I32))
            | (cols == meta[:, META_POS1:META_POS1 + 1].astype(I32)))
    moe = jnp.dot(jnp.where(pick, 1.0, 0.0).astype(BF16), y, preferred_element_type=F32)
    xn = x_ref[...] + g2_ref[0] * moe
    if final_norm:
        xn = _rms(xn) * fg_ref[...]
    o_ref[...] = xn


def _moe_combine(y_rows, seg, starts, x2, meta, g2, final_g, seq, tr, final_norm):
    n_tok, d = x2.shape
    bsz = n_tok // seq
    steps_per_batch = seq // tr
    nb = n_tok // tr
    grid_spec = pltpu.PrefetchScalarGridSpec(
        num_scalar_prefetch=1,
        grid=(nb,),
        in_specs=[pl.BlockSpec((1, 8, LANES), lambda i, s: (i, 0, 0), memory_space=pltpu.SMEM),
                  pl.BlockSpec((1, 8, LANES), lambda i, s: (jnp.minimum(i + 1, nb - 1), 0, 0), memory_space=pltpu.SMEM),
                  pl.BlockSpec(memory_space=pl.ANY),
                  pl.BlockSpec((tr, d), lambda i, s: (i, 0)),
                  pl.BlockSpec((tr, LANES), lambda i, s: (i, 0)),
                  pl.BlockSpec((1, 1, d), lambda i, s: (i // steps_per_batch, 0, 0)),
                  pl.BlockSpec((1, d), lambda i, s: (0, 0))],
        out_specs=pl.BlockSpec((tr, d), lambda i, s: (i, 0)),
        scratch_shapes=[pltpu.VMEM((2, _local_rows(tr), d), F32), pltpu.SemaphoreType.DMA((2,))],
    )
    return pl.pallas_call(
        functools.partial(_combine_body, final_norm=final_norm),
        grid_spec=grid_spec,
        out_shape=jax.ShapeDtypeStruct((n_tok, d), F32),
        compiler_params=_cparams(("arbitrary",)),
        name="moe_combine",
    )(starts, seg, seg, y_rows, x2, meta, g2.reshape(bsz, 1, d), final_g.reshape(1, d))


def _expert_pairs(counts, n_rows):
    n_blocks = n_rows // MOE_ROWS
    n_pairs = n_blocks + N_EXPERTS
    ends = jnp.cumsum(counts)
    starts = ends - counts
    first_blk = starts // MOE_ROWS
    last_blk = (ends - 1) // MOE_ROWS
    npairs = jnp.where(counts > 0, last_blk - first_blk + 1, 0)
    pend = jnp.cumsum(npairs)
    poff = pend - npairs
    total = pend[-1]
    used_blocks = (ends[-1] + MOE_ROWS - 1) // MOE_ROWS
    p = jnp.arange(n_pairs, dtype=I32)
    p_eff = jnp.minimum(p, total - 1)
    e = jnp.minimum(jnp.sum(pend[None, :] <= p_eff[:, None], axis=1), N_EXPERTS - 1).astype(I32)
    table = jnp.stack([first_blk, poff, starts, ends], axis=1).astype(F32)
    onehot = (e[:, None] == jnp.arange(N_EXPERTS, dtype=I32)[None, :]).astype(F32)
    first_e, poff_e, start_e, end_e = jnp.dot(onehot, table, precision=HIGHEST).astype(I32).T
    valid = p < total
    blk = jnp.where(valid, first_e + p_eff - poff_e, jnp.minimum(used_blocks + p - total, n_blocks - 1)).astype(I32)
    lo = jnp.where(valid, jnp.clip(start_e - blk * MOE_ROWS, 0, MOE_ROWS), 0).astype(I32)
    hi = jnp.where(valid, jnp.clip(end_e - blk * MOE_ROWS, 0, MOE_ROWS), 0).astype(I32)
    return blk, e, lo, hi


def _hier_moe(x, g, sh, sc, g2, wr_g, br_g, wr_e, br_e, w_gate, w_up, w_down, layer, final_g, final_norm):
    bsz, seq, d = x.shape
    n_tok = bsz * seq
    tr = min(ROUTER_ROWS, seq)
    assert 2 * tr <= SEG_SIZES[0]
    x2 = x.reshape(n_tok, d)
    h2, meta, post, seg, cnt = _moe_router(x2, g, sh, sc, wr_g, br_g, wr_e, br_e, seq)
    counts = cnt[0, :N_EXPERTS].astype(I32)
    ends = jnp.cumsum(counts)
    starts = jnp.concatenate([ends - counts, ends[-1:]])
    n_rows = -(-(2 * n_tok + (n_tok // tr) * N_EXPERTS * (SEG_ALIGN - 1)) // MOE_ROWS) * MOE_ROWS
    x_rows = _moe_dispatch(h2, post, seg, starts, n_rows, tr)
    y_rows = _moe_experts(x_rows, _expert_pairs(counts, n_rows), w_gate, w_up, w_down, layer)
    out = _moe_combine(y_rows, seg, starts, x2, meta, g2, final_g, seq, tr, final_norm)
    return out.reshape(bsz, seq, d)


SEL_COLS = 256
SEL_CHUNK = 512
BIT_GROUP = 256
SEL_SWEEP = 64
ATT_TILE = 512


def _idx_score_t(k, qt, wt):
    acc = None
    for h in range(IDX_HEADS):
        rel = jnp.dot(k, qt[h * IDX_DIM:(h + 1) * IDX_DIM], preferred_element_type=F32)
        term = jnp.maximum(rel, 0.0) * wt[h:h + 1, :]
        acc = term if acc is None else acc + term
    return acc


def _bit_transpose32(a):
    a = list(a)
    m, j = 0x0000FFFF, 16
    while j:
        k = 0
        while k < 32:
            t = (a[k] ^ lax.shift_right_logical(a[k + j], jnp.int32(j))) & jnp.int32(m)
            a[k] = a[k] ^ t
            a[k + j] = a[k + j] ^ (t << j)
            k = (k + j + 1) & ~j
        j >>= 1
        m = (m ^ (m << j)) & 0xFFFFFFFF
    return a


def _select_body(qt_ref, zk_ref, wt_ref, o_ref, planes_scr, eq_scr, gt_scr, *, topk):
    i = pl.program_id(1)
    tq = qt_ref.shape[2]
    qt = qt_ref[0]
    wt = wt_ref[0].astype(F32)
    n_chunks = ((i + 1) * tq + SEL_CHUNK - 1) // SEL_CHUNK
    krow0 = lax.broadcasted_iota(I32, (SEL_CHUNK, tq), 0)
    qcol = i * tq + lax.broadcasted_iota(I32, (SEL_CHUNK, tq), 1)
    int_min = jnp.int32(INT_MIN)

    groups = SEL_CHUNK // BIT_GROUP
    words = SEL_CHUNK // 32
    sweep_rows = min(SEL_SWEEP, eq_scr.shape[0])

    def fill(masked, c, carry):
        off = pl.multiple_of(c * SEL_CHUNK, SEL_CHUNK)
        k = zk_ref[0, pl.ds(off, SEL_CHUNK), :][:, 0:IDX_DIM]
        bits = lax.bitcast_convert_type(_idx_score_t(k, qt, wt), I32)
        key = bits ^ ((bits >> 31) | int_min)
        if masked:
            key = jnp.where(krow0 + off <= qcol, key, 0)
        for g in range(groups):
            for lt in range(0, tq, LANES):
                ku = key[g * BIT_GROUP:(g + 1) * BIT_GROUP, lt:lt + LANES]
                planes = _bit_transpose32([ku[8 * r:8 * r + 8] for r in range(32)])
                wrow = pl.multiple_of(c * words + g * 8, 8)
                for b in range(32):
                    planes_scr[b, pl.ds(wrow, 8), lt:lt + LANES] = planes[b]
        return carry

    n_plain = (i * tq + 1) // SEL_CHUNK
    lax.fori_loop(0, n_plain, functools.partial(fill, False), 0)
    lax.fori_loop(n_plain, n_chunks, functools.partial(fill, True), 0)

    n_sweep = (n_chunks * words + sweep_rows - 1) // sweep_rows

    def pad(c, carry):
        wrow = pl.multiple_of(c * words, words)
        for b in range(32):
            planes_scr[b, pl.ds(wrow, words), :] = jnp.zeros((words, tq), I32)
        return carry

    lax.fori_loop(n_chunks, n_sweep * (sweep_rows // words), pad, 0)

    def sweep(upd, cnt_plane):
        def body(sb, acc):
            r0 = pl.multiple_of(sb * sweep_rows, sweep_rows)
            eq = eq_scr[pl.ds(r0, sweep_rows), :]
            gt = gt_scr[pl.ds(r0, sweep_rows), :]
            if upd is not None:
                plane, accept = upd
                hit = eq & planes_scr[plane, pl.ds(r0, sweep_rows), :]
                gt = jnp.where(accept, gt, gt | hit)
                eq = jnp.where(accept, hit, eq ^ hit)
                eq_scr[pl.ds(r0, sweep_rows), :] = eq
                gt_scr[pl.ds(r0, sweep_rows), :] = gt
            if cnt_plane is None:
                return acc
            ones = lax.population_count(gt | (eq & planes_scr[cnt_plane, pl.ds(r0, sweep_rows), :]))
            return acc + jnp.sum(ones.reshape(sweep_rows // 8, 8, tq), axis=0)
        acc = lax.fori_loop(0, n_sweep, body, jnp.zeros((8, tq), I32))
        return jnp.sum(acc.astype(F32), axis=0, keepdims=True)

    k_f = jnp.float32(topk)
    eq_scr[...] = jnp.full(eq_scr.shape, -1, I32)
    gt_scr[...] = jnp.zeros(gt_scr.shape, I32)

    def bit_step(ib, carry):
        u, cnt = carry
        accept = cnt >= k_f
        u = jnp.where(accept, u | jnp.left_shift(jnp.int32(1), 32 - ib), u)
        return u, sweep((ib - 1, accept), ib)

    u, cnt = lax.fori_loop(1, 32, bit_step, (jnp.zeros((1, tq), I32), sweep(None, 0)))
    accept = cnt >= k_f
    u = jnp.where(accept, u | 1, u)
    sweep((31, accept), None)
    some = jnp.where(u != 0, -1, 0)

    def popcount_rows(word_of):
        def body(sb, acc):
            r0 = pl.multiple_of(sb * sweep_rows, sweep_rows)
            ones = lax.population_count(word_of(r0))
            return acc + jnp.sum(ones.reshape(sweep_rows // 8, 8, tq), axis=0)
        acc = lax.fori_loop(0, n_sweep, body, jnp.zeros((8, tq), I32))
        return jnp.sum(acc.astype(F32), axis=0, keepdims=True)

    n_gt = popcount_rows(lambda r0: gt_scr[pl.ds(r0, sweep_rows), :])
    n_eq = jnp.where(u != 0, popcount_rows(lambda r0: eq_scr[pl.ds(r0, sweep_rows), :]), 0.0)

    @pl.when(jnp.max(n_gt + n_eq) > k_f)
    def _():
        wrow = lax.broadcasted_iota(I32, (sweep_rows, tq), 0)

        def below(cap, r0):
            w = wrow + r0
            last = lax.shift_right_arithmetic(cap - (w >> 3) * BIT_GROUP - (w & 7) - 1, 3)
            clear = jnp.clip(31 - last, 0, 32)
            return jnp.where(clear >= 32, 0, jnp.left_shift(jnp.int32(-1), jnp.minimum(clear, 31)))

        need = k_f - n_gt

        def cap_step(ib, v):
            cand = v | jnp.left_shift(jnp.int32(1), 14 - ib)
            cnt = popcount_rows(lambda r0: eq_scr[pl.ds(r0, sweep_rows), :] & below(cand, r0))
            return jnp.where(cnt <= need, cand, v)

        cap = lax.fori_loop(0, 15, cap_step, jnp.zeros((1, tq), I32))
        cap = jnp.where((u != 0) & (n_gt + n_eq > k_f), cap, 2 ** 30)

        def trim(sb, carry):
            r0 = pl.multiple_of(sb * sweep_rows, sweep_rows)
            eq_scr[pl.ds(r0, sweep_rows), :] = eq_scr[pl.ds(r0, sweep_rows), :] & below(cap, r0)
            return carry

        lax.fori_loop(0, n_sweep, trim, 0)

    def emit(c, carry):
        off = pl.multiple_of(c * SEL_CHUNK, SEL_CHUNK)
        for g in range(groups):
            wrow = pl.multiple_of(c * words + g * 8, 8)
            keep = gt_scr[pl.ds(wrow, 8), :] | (eq_scr[pl.ds(wrow, 8), :] & some)
            rows = [jnp.where((lax.shift_right_logical(keep, 31 - r) & 1) != 0, 0.0, -jnp.inf) for r in range(32)]
            o_ref[0, pl.ds(pl.multiple_of(off + g * BIT_GROUP, BIT_GROUP), BIT_GROUP), :] = (
                jnp.concatenate(rows, axis=0).astype(o_ref.dtype))
        return carry

    lax.fori_loop(0, n_chunks, emit, 0)

    def blank(c, carry):
        off = pl.multiple_of(c * SEL_CHUNK, SEL_CHUNK)
        o_ref[0, pl.ds(off, SEL_CHUNK), :] = jnp.full((SEL_CHUNK, tq), -jnp.inf, o_ref.dtype)
        return carry

    lax.fori_loop(n_chunks, o_ref.shape[1] // SEL_CHUNK, blank, 0)


def _select(z, zt, topk):
    bsz, seq, _ = z.shape
    tq = min(SEL_COLS, seq)
    assert seq % SEL_CHUNK == 0 and seq % tq == 0
    return pl.pallas_call(
        functools.partial(_select_body, topk=topk),
        grid=(bsz, seq // tq),
        in_specs=[pl.BlockSpec((1, ZT_QIDX_ROWS, tq), lambda b, i: (b, ZT_QIDX_OFF // ZT_QIDX_ROWS, i)),
                  pl.BlockSpec((1, seq, LANES), lambda b, i: (b, 0, Z_KIDX_OFF // LANES)),
                  pl.BlockSpec((1, BF16_ROWS, tq), lambda b, i: (b, ZT_W_OFF // BF16_ROWS, i))],
        out_specs=pl.BlockSpec((1, seq, tq), lambda b, i: (b, 0, i)),
        out_shape=jax.ShapeDtypeStruct((bsz, seq, seq), BF16),
        scratch_shapes=[pltpu.VMEM((32, seq // 32, tq), I32),
                        pltpu.VMEM((seq // 32, tq), I32), pltpu.VMEM((seq // 32, tq), I32)],
        compiler_params=_cparams(("arbitrary", "arbitrary")),
        name="dsa_select",
    )(zt, z, zt)


BIAS_INIT_ROWS = 8
ONES_ROWS = BF16_ROWS


def _init_bias_tiles(btile, bias_ref, head0, n_heads, ta):
    col = lax.broadcasted_iota(I32, (BIAS_INIT_ROWS, ta), 1)
    row0 = lax.broadcasted_iota(I32, (BIAS_INIT_ROWS, ta), 0)

    def body(r, carry):
        off = pl.multiple_of(r * BIAS_INIT_ROWS, BIAS_INIT_ROWS)
        for kind in range(2):
            dist = col - (row0 + off) + kind * ta
            for h in range(n_heads):
                far = bias_ref[NUM_BUCKETS - 1, head0 + h]
                val = jnp.full((BIAS_INIT_ROWS, ta), (bias_ref[0, head0 + h] - far) * LOG2E, F32)
                for b in range(1, NUM_BUCKETS - 1):
                    val = jnp.where(dist >= BUCKET_START[b], (bias_ref[b, head0 + h] - far) * LOG2E, val)
                val = jnp.where(dist >= BUCKET_START[NUM_BUCKETS - 1], 0.0, val)
                btile[h, kind, pl.ds(off, BIAS_INIT_ROWS), :] = val
        return carry

    lax.fori_loop(0, ta // BIAS_INIT_ROWS, body, 0)


def _with_ones(vt):
    return jnp.concatenate([vt, jnp.ones((ONES_ROWS, vt.shape[1]), vt.dtype)], axis=0)


def _softmax_step_t(logits, v_aug, m_ref, acc_ref, idx):
    m_old = m_ref[idx]
    m_new = jnp.maximum(m_old, jnp.max(logits, axis=0, keepdims=True))
    m_safe = jnp.where(m_new == -jnp.inf, 0.0, m_new)
    p = jnp.exp2(logits - m_safe)
    alpha = jnp.exp2(m_old - m_safe)
    acc_ref[idx] = alpha * acc_ref[idx] + jnp.dot(v_aug, p.astype(BF16), preferred_element_type=F32)
    m_ref[idx] = m_new


def _reset_softmax(m_s, acc_s):
    m_s[...] = jnp.full(m_s.shape, -jnp.inf, F32)
    acc_s[...] = jnp.zeros(acc_s.shape, F32)


def _cattn_body(qi_ref, ki_ref, bias_ref, qt_ref, zk_ref, vt_ref, mask_ref, o_ref, m_s, acc_s, btile):
    b = pl.program_id(0)
    p = pl.program_id(1)
    qi = qi_ref[p]
    ki = ki_ref[p]
    ta = zk_ref.shape[1]
    hd = C_HEAD_DIM

    @pl.when((b == 0) & (p == 0))
    def _():
        _init_bias_tiles(btile, bias_ref, 0, C_HEADS, ta)

    @pl.when(ki == 0)
    def _():
        _reset_softmax(m_s, acc_s)

    def heads(extra_of_head):
        for h in range(C_HEADS):
            lo = h * hd
            logits = (jnp.dot(zk_ref[0, :, lo:lo + hd], qt_ref[0, lo:lo + hd, :], preferred_element_type=F32)
                      + extra_of_head(h))
            _softmax_step_t(logits, _with_ones(vt_ref[0, lo:lo + hd, :]), m_s, acc_s, h)

    near = ki >= qi - 1

    @pl.when(near)
    def _():
        heads(lambda h: mask_ref[0].astype(F32) + btile[h, qi - ki])

    @pl.when(jnp.logical_not(near))
    def _():
        heads(lambda h: mask_ref[0].astype(F32))

    @pl.when(ki == qi)
    def _():
        outs = []
        for h in range(C_HEADS):
            a = acc_s[h]
            outs.append((a[0:hd] / a[hd:hd + 1]).T)
        o_ref[0] = jnp.concatenate(outs, axis=1).astype(o_ref.dtype)


def _dattn_body(qi_ref, ki_ref, bias_ref, qt_ref, zk_ref, vt_ref, lam_ref, ng_ref, o_ref,
                m_s, acc_s, btile, *, lambda_init):
    b = pl.program_id(0)
    p = pl.program_id(1)
    qi = qi_ref[p]
    ki = ki_ref[p]
    ta = zk_ref.shape[1]
    hd = DIFF_HEAD_DIM
    dv = 2 * hd

    @pl.when((b == 0) & (p == 0))
    def _():
        _init_bias_tiles(btile, bias_ref, C_HEADS, DIFF_HEADS, ta)

    @pl.when(ki == 0)
    def _():
        _reset_softmax(m_s, acc_s)

    def heads(extra_of_head):
        for h in range(DIFF_HEADS):
            v_aug = _with_ones(vt_ref[0, dv * h:dv * (h + 1), :])
            extra = extra_of_head(h)
            for j in range(2):
                lo = (2 * h + j) * hd
                logits = jnp.dot(zk_ref[0, :, lo:lo + hd], qt_ref[0, lo:lo + hd, :], preferred_element_type=F32)
                if extra is not None:
                    logits = logits + extra
                _softmax_step_t(logits, v_aug, m_s, acc_s, 2 * h + j)

    near = ki >= qi - 1

    @pl.when(near)
    def _():
        krow = ki * ta + lax.broadcasted_iota(I32, (ta, ta), 0)
        qcol = qi * ta + lax.broadcasted_iota(I32, (ta, ta), 1)
        heads(lambda h: jnp.where(krow <= qcol, btile[h, qi - ki], -jnp.inf))

    @pl.when(jnp.logical_not(near))
    def _():
        heads(lambda h: None)

    @pl.when(ki == qi)
    def _():
        lam_p = lam_ref[...]
        lam = (jnp.exp(jnp.sum(lam_p[0:1] * lam_p[1:2], axis=1, keepdims=True))
               - jnp.exp(jnp.sum(lam_p[2:3] * lam_p[3:4], axis=1, keepdims=True)) + lambda_init)
        outs = []
        for h in range(DIFF_HEADS):
            a1 = acc_s[2 * h]
            a2 = acc_s[2 * h + 1]
            o = a1[0:dv] / a1[dv:dv + 1] - lam * (a2[0:dv] / a2[dv:dv + 1])
            o = o * lax.rsqrt(jnp.mean(o * o, axis=0, keepdims=True) + EPS) * ng_ref[...] * (1.0 - lambda_init)
            outs.append(o.T)
        o_ref[0] = jnp.concatenate(outs, axis=1).astype(o_ref.dtype)


def _causal_pairs(nq):
    qi = [q for q in range(nq) for _ in range(q + 1)]
    ki = [k for q in range(nq) for k in range(q + 1)]
    return jnp.asarray(qi, I32), jnp.asarray(ki, I32)


def _cattn(z, zt, mask, rel_bias, ta):
    bsz, seq, _ = z.shape
    qi, ki = _causal_pairs(seq // ta)
    qmap = lambda col: (lambda b, p, qi, ki: (b, qi[p], col))
    kmap = lambda col: (lambda b, p, qi, ki: (b, ki[p], col))
    dv_aug = C_HEAD_DIM + ONES_ROWS
    grid_spec = pltpu.PrefetchScalarGridSpec(
        num_scalar_prefetch=2,
        grid=(bsz, qi.shape[0]),
        in_specs=[pl.BlockSpec(memory_space=pltpu.SMEM),
                  pl.BlockSpec((1, C_WIDTH, ta), lambda b, p, qi, ki: (b, ZT_QC, qi[p])),
                  pl.BlockSpec((1, ta, C_WIDTH), kmap(Z_KC)),
                  pl.BlockSpec((1, C_WIDTH, ta), lambda b, p, qi, ki: (b, ZT_VC, ki[p])),
                  pl.BlockSpec((1, ta, ta), lambda b, p, qi, ki: (b, ki[p], qi[p]))],
        out_specs=pl.BlockSpec((1, ta, C_WIDTH), qmap(0)),
        scratch_shapes=[pltpu.VMEM((C_HEADS, 1, ta), F32),
                        pltpu.VMEM((C_HEADS, dv_aug, ta), F32),
                        pltpu.VMEM((C_HEADS, 2, ta, ta), F32)],
    )
    return pl.pallas_call(
        _cattn_body,
        grid_spec=grid_spec,
        out_shape=jax.ShapeDtypeStruct((bsz, seq, C_WIDTH), BF16),
        compiler_params=_cparams(("arbitrary", "arbitrary")),
        name="dsa_attn",
    )(qi, ki, rel_bias, zt, z, zt, mask)


def _dattn(z, zt, rel_bias, diff_lam, diff_norm_g, lambda_init, ta):
    bsz, seq, _ = z.shape
    qi, ki = _causal_pairs(seq // ta)
    n_maps = 2 * DIFF_HEADS
    dv = 2 * DIFF_HEAD_DIM
    qmap = lambda col: (lambda b, p, qi, ki: (b, qi[p], col))
    kmap = lambda col: (lambda b, p, qi, ki: (b, ki[p], col))
    grid_spec = pltpu.PrefetchScalarGridSpec(
        num_scalar_prefetch=2,
        grid=(bsz, qi.shape[0]),
        in_specs=[pl.BlockSpec(memory_space=pltpu.SMEM),
                  pl.BlockSpec((1, DIFF_W, ta), lambda b, p, qi, ki: (b, ZT_QD, qi[p])),
                  pl.BlockSpec((1, ta, DIFF_W), kmap(Z_KD)),
                  pl.BlockSpec((1, DIFF_W, ta), lambda b, p, qi, ki: (b, ZT_VD, ki[p])),
                  pl.BlockSpec(diff_lam.shape, lambda b, p, qi, ki: (0, 0)),
                  pl.BlockSpec((dv, 1), lambda b, p, qi, ki: (0, 0))],
        out_specs=pl.BlockSpec((1, ta, DIFF_W), qmap(0)),
        scratch_shapes=[pltpu.VMEM((n_maps, 1, ta), F32),
                        pltpu.VMEM((n_maps, dv + ONES_ROWS, ta), F32),
                        pltpu.VMEM((DIFF_HEADS, 2, ta, ta), F32)],
    )
    return pl.pallas_call(
        functools.partial(_dattn_body, lambda_init=lambda_init),
        grid_spec=grid_spec,
        out_shape=jax.ShapeDtypeStruct((bsz, seq, DIFF_W), BF16),
        compiler_params=_cparams(("arbitrary", "arbitrary")),
        name="diff_attn",
    )(qi, ki, rel_bias, zt, z, zt, diff_lam, diff_norm_g.reshape(dv, 1))


def _out_proj_body(oc_ref, od_ref, x_ref, g1_ref, w_ref, o_ref):
    y = (jnp.dot(oc_ref[0], w_ref[0:C_WIDTH, :], preferred_element_type=F32)
         + jnp.dot(od_ref[0], w_ref[C_WIDTH:C_WIDTH + DIFF_W, :], preferred_element_type=F32))
    o_ref[0] = x_ref[0] + g1_ref[0] * y


def _out_proj(out_c, out_d, x, g1, w_out_bf16, tm):
    bsz, seq, d = x.shape
    return pl.pallas_call(
        _out_proj_body,
        grid=(bsz, seq // tm),
        in_specs=[pl.BlockSpec((1, tm, C_WIDTH), lambda b, i: (b, i, 0)),
                  pl.BlockSpec((1, tm, DIFF_W), lambda b, i: (b, i, 0)),
                  pl.BlockSpec((1, tm, d), lambda b, i: (b, i, 0)),
                  pl.BlockSpec((1, 1, d), lambda b, i: (b, 0, 0)),
                  pl.BlockSpec((C_WIDTH + DIFF_W, d), lambda b, i: (0, 0))],
        out_specs=pl.BlockSpec((1, tm, d), lambda b, i: (b, i, 0)),
        out_shape=jax.ShapeDtypeStruct((bsz, seq, d), F32),
        compiler_params=_cparams(("arbitrary", "arbitrary")),
        name="attn_out_proj",
    )(out_c, out_d, x, g1.reshape(bsz, 1, d), w_out_bf16)


def _attn_in_weights(cd_w_in):
    sizes = (C_WIDTH, C_WIDTH, C_WIDTH, IDX_HEADS * IDX_DIM, IDX_DIM, IDX_HEADS, DIFF_W, DIFF_W, DIFF_W)
    cuts = np.cumsum(sizes)[:-1]
    q_c, k_c, v_c, q_i, k_i, w_i, q_d, k_d, v_d = jnp.split(cd_w_in, cuts, axis=1)
    d = cd_w_in.shape[0]
    w = jnp.concatenate([k_c, k_d, k_i, jnp.zeros((d, LANES - IDX_DIM), cd_w_in.dtype)], axis=1)
    wt = jnp.concatenate([v_c, v_d, q_c * (C_HEAD_DIM ** -0.5 * LOG2E), q_d * (DIFF_HEAD_DIM ** -0.5 * LOG2E), q_i,
                          w_i * (IDX_DIM * IDX_HEADS) ** -0.5,
                          jnp.zeros((d, BF16_ROWS - IDX_HEADS), cd_w_in.dtype)], axis=1).T
    return w.astype(BF16), wt.astype(BF16)


def kernel(x, c, positions, rel_bias, norm_g, final_norm_g, ada_w, ada_b, ab_w_in, ab_conv_a, ab_conv_b,
           ab_conv_b_bias, ab_ln_g, ab_ln_b, ab_w_out, cd_w_in, diff_lam, diff_norm_g, cd_w_out,
           moe_wr_g, moe_br_g, moe_wr_e, moe_br_e, moe_w_gate, moe_w_up, moe_w_down):
    del positions
    bsz, seq, d = x.shape
    depth = ada_w.shape[0]
    tm = min(512, seq)
    ta = min(ATT_TILE, seq)
    assert ta >= MAX_DISTANCE and seq % ta == 0
    topk = min(TOPK_MAX, seq // 4)
    mods = _ada_mod(c, ada_w, ada_b)
    for i in range(depth):
        sh1, sc1, g1, sh2, sc2, g2 = jnp.split(mods[i], 6, axis=-1)
        j = i // 2
        if i % 2 == 0:
            z = _norm_proj(x, norm_g[i, 0], sh1, sc1, ab_w_in[j].astype(BF16), tm)
            x = _conv_mix(z, x, g1, ab_conv_a[j], ab_conv_b[j], ab_conv_b_bias[j], ab_ln_g[j], ab_ln_b[j],
                          ab_w_out[j].astype(BF16), min(256, seq))
        else:
            lambda_init = 0.8 - 0.6 * math.exp(-0.3 * i)
            w, wt = _attn_in_weights(cd_w_in[j])
            z, zt = _norm_proj(x, norm_g[i, 0], sh1, sc1, w, tm, wt)
            mask = _select(z, zt, topk)
            out_c = _cattn(z, zt, mask, rel_bias, ta)
            out_d = _dattn(z, zt, rel_bias, diff_lam[j], diff_norm_g[j], lambda_init, ta)
            x = _out_proj(out_c, out_d, x, g1, cd_w_out[j].astype(BF16), tm)
        x = _hier_moe(x, norm_g[i, 1], sh2, sc2, g2, moe_wr_g[i], moe_br_g[i], moe_wr_e[i], moe_br_e[i],
                      moe_w_gate, moe_w_up, moe_w_down, i, final_norm_g, final_norm=(i == depth - 1))
    return x
```

```python
import functools
import math

import numpy as np
import jax
import jax.numpy as jnp
from jax import lax
from jax.experimental import pallas as pl
from jax.experimental.pallas import tpu as pltpu

F32 = jnp.float32
BF16 = jnp.bfloat16
I32 = jnp.int32
HIGHEST = lax.Precision.HIGHEST

EPS = 1e-6
A_WIDTH = 512
A_CONV = 3
B_WIDTH = 512
B_CONV = 31
C_HEADS = 8
C_HEAD_DIM = 64
IDX_HEADS = 8
IDX_DIM = 32
TOPK_MAX = 256
DIFF_HEADS = 4
DIFF_HEAD_DIM = 64
NUM_BUCKETS = 32
MAX_DISTANCE = 128
N_GROUPS = 4
EXPERTS_PER_GROUP = 8
N_EXPERTS = N_GROUPS * EXPERTS_PER_GROUP
C_WIDTH = C_HEADS * C_HEAD_DIM
DIFF_W = DIFF_HEADS * 2 * DIFF_HEAD_DIM
LANES = 128
BF16_ROWS = 16
INT_MIN = -(2 ** 31)
LOG2E = math.log2(math.e)
VMEM_LIMIT = 56 * 1024 * 1024

Z_KC, Z_KD = 0, 1
Z_KIDX_OFF = C_WIDTH + DIFF_W
Z_COLS = Z_KIDX_OFF + LANES
ZT_VC, ZT_VD, ZT_QC, ZT_QD = 0, 1, 2, 3
ZT_QIDX_OFF = 2 * C_WIDTH + 2 * DIFF_W
ZT_QIDX_ROWS = IDX_HEADS * IDX_DIM
ZT_W_OFF = ZT_QIDX_OFF + ZT_QIDX_ROWS
ZT_ROWS = ZT_W_OFF + BF16_ROWS


def _bucket_starts():
    n = np.arange(0, 2 * MAX_DISTANCE)
    me = NUM_BUCKETS // 2
    lr = np.log(np.maximum(n, 1) / me) / math.log(MAX_DISTANCE / me)
    large = me + (lr * (NUM_BUCKETS - me)).astype(np.int64)
    b = np.where(n < me, n, np.minimum(large, NUM_BUCKETS - 1))
    return [int(n[b >= k].min()) for k in range(NUM_BUCKETS)]


BUCKET_START = _bucket_starts()


def _cparams(sem):
    return pltpu.CompilerParams(dimension_semantics=sem, vmem_limit_bytes=VMEM_LIMIT)


def _rms(x):
    return x * lax.rsqrt(jnp.mean(x * x, axis=-1, keepdims=True) + EPS)


def _sigmoid(x):
    return 1.0 / (1.0 + jnp.exp(-x))


def _dot_nt(a, b):
    return lax.dot_general(a, b, (((1,), (1,)), ((), ())), preferred_element_type=F32)


def _ada_body(c_ref, w_ref, b_ref, o_ref):
    c = c_ref[...]
    cond = c * _sigmoid(c)
    o_ref[0] = jnp.dot(cond, w_ref[0], precision=HIGHEST, preferred_element_type=F32) + b_ref[0]


def _ada_mod(c, ada_w, ada_b):
    depth, d, n6 = ada_w.shape
    bsz = c.shape[0]
    rows = 8
    c_pad = jnp.zeros((rows, d), F32).at[:bsz].set(c)
    tn = 1536
    out = pl.pallas_call(
        _ada_body,
        grid=(depth, n6 // tn),
        in_specs=[pl.BlockSpec((rows, d), lambda i, j: (0, 0)),
                  pl.BlockSpec((1, d, tn), lambda i, j: (i, 0, j)),
                  pl.BlockSpec((1, 1, tn), lambda i, j: (i, 0, j))],
        out_specs=pl.BlockSpec((1, rows, tn), lambda i, j: (i, 0, j)),
        out_shape=jax.ShapeDtypeStruct((depth, rows, n6), F32),
        compiler_params=_cparams(("arbitrary", "arbitrary")),
        name="ada_mod",
    )(c_pad, ada_w, ada_b.reshape(depth, 1, n6))
    return out[:, :bsz]


def _norm_proj_body(x_ref, g_ref, sh_ref, sc_ref, w_ref, *rest):
    y = _rms(x_ref[0]) * g_ref[...]
    h = (y * (1.0 + sc_ref[0]) + sh_ref[0]).astype(BF16)
    if len(rest) == 1:
        (o_ref,) = rest
    else:
        wt_ref, o_ref, ot_ref = rest
        ot_ref[0] = _dot_nt(wt_ref[...], h).astype(ot_ref.dtype)
    o_ref[0] = jnp.dot(h, w_ref[...], preferred_element_type=F32).astype(o_ref.dtype)


def _norm_proj(x, g, sh, sc, w_bf16, tm, wt_bf16=None):
    bsz, seq, d = x.shape
    n = w_bf16.shape[1]
    in_specs = [pl.BlockSpec((1, tm, d), lambda b, i: (b, i, 0)),
                pl.BlockSpec((1, d), lambda b, i: (0, 0)),
                pl.BlockSpec((1, 1, d), lambda b, i: (b, 0, 0)),
                pl.BlockSpec((1, 1, d), lambda b, i: (b, 0, 0)),
                pl.BlockSpec((d, n), lambda b, i: (0, 0))]
    out_specs = pl.BlockSpec((1, tm, n), lambda b, i: (b, i, 0))
    out_shape = jax.ShapeDtypeStruct((bsz, seq, n), BF16)
    args = [x, g.reshape(1, d), sh.reshape(bsz, 1, d), sc.reshape(bsz, 1, d), w_bf16]
    if wt_bf16 is not None:
        nt = wt_bf16.shape[0]
        in_specs.append(pl.BlockSpec((nt, d), lambda b, i: (0, 0)))
        out_specs = [out_specs, pl.BlockSpec((1, nt, tm), lambda b, i: (b, 0, i))]
        out_shape = [out_shape, jax.ShapeDtypeStruct((bsz, nt, seq), BF16)]
        args.append(wt_bf16)
    return pl.pallas_call(
        _norm_proj_body,
        grid=(bsz, seq // tm),
        in_specs=in_specs,
        out_specs=out_specs,
        out_shape=out_shape,
        compiler_params=_cparams(("arbitrary", "arbitrary")),
        name="norm_proj",
    )(*args)


CONV_HALO = 32
CONV_ROWS = 64


def _conv_body(z_ref, x_ref, g1_ref, ca_ref, cb_ref, cbb_ref, lng_ref, lnb_ref, wo_ref, o_ref,
               ua_scr, ub_scr, y_scr, *, tl):
    l = pl.program_id(1)

    @pl.when(l == 0)
    def _():
        ua_scr[0:CONV_HALO, :] = jnp.zeros((CONV_HALO, A_WIDTH), F32)
        ub_scr[0:CONV_HALO, :] = jnp.zeros((CONV_HALO, B_WIDTH), F32)

    a = A_WIDTH
    gate_c = z_ref[0, :, a:2 * a].astype(F32)
    x_a = z_ref[0, :, 2 * a:3 * a].astype(F32)
    ua_scr[CONV_HALO:CONV_HALO + tl, :] = gate_c * x_a
    val_b = z_ref[0, :, 3 * a:3 * a + B_WIDTH].astype(F32)
    glu = z_ref[0, :, 3 * a + B_WIDTH:3 * a + 2 * B_WIDTH].astype(F32)
    ub_scr[CONV_HALO:CONV_HALO + tl, :] = val_b * _sigmoid(glu)

    for r in range(0, tl, CONV_ROWS):
        acc_a = None
        for k in range(A_CONV):
            tap = ua_scr[CONV_HALO + r - (A_CONV - 1) + k:CONV_HALO + r - (A_CONV - 1) + k + CONV_ROWS, :]
            term = tap * ca_ref[k:k + 1, :]
            acc_a = term if acc_a is None else acc_a + term
        gate_b = z_ref[0, r:r + CONV_ROWS, 0:a].astype(F32)
        y_scr[r:r + CONV_ROWS, 0:a] = (gate_b * acc_a).astype(BF16)

        win = CONV_HALO + r - 8
        acc_b = None
        for b in range(8):
            phase = None
            for back in range(b, B_CONV, 8):
                tap = ub_scr[win - (back - b):win - (back - b) + CONV_ROWS + 8, :]
                term = tap * cb_ref[B_CONV - 1 - back:B_CONV - back, :]
                phase = term if phase is None else phase + term
            piece = phase[8 - b:8 - b + CONV_ROWS]
            acc_b = piece if acc_b is None else acc_b + piece
        u = acc_b + cbb_ref[...]
        mu = jnp.mean(u, axis=-1, keepdims=True)
        uc = u - mu
        var = jnp.mean(uc * uc, axis=-1, keepdims=True)
        v = uc * lax.rsqrt(var + EPS) * lng_ref[...] + lnb_ref[...]
        y_scr[r:r + CONV_ROWS, a:a + B_WIDTH] = (v * _sigmoid(v)).astype(BF16)

    ua_scr[0:CONV_HALO, :] = ua_scr[tl:tl + CONV_HALO, :]
    ub_scr[0:CONV_HALO, :] = ub_scr[tl:tl + CONV_HALO, :]
    y = jnp.dot(y_scr[...], wo_ref[...], preferred_element_type=F32)
    o_ref[0] = x_ref[0] + g1_ref[0] * y


def _conv_mix(z, x, g1, conv_a, conv_b, conv_b_bias, ln_g, ln_b, w_out_bf16, tl):
    bsz, seq, d = x.shape
    nz = z.shape[-1]
    wide = A_WIDTH + B_WIDTH
    full = lambda shape: pl.BlockSpec(shape, lambda b, l: (0,) * len(shape))
    return pl.pallas_call(
        functools.partial(_conv_body, tl=tl),
        grid=(bsz, seq // tl),
        in_specs=[pl.BlockSpec((1, tl, nz), lambda b, l: (b, l, 0)),
                  pl.BlockSpec((1, tl, d), lambda b, l: (b, l, 0)),
                  pl.BlockSpec((1, 1, d), lambda b, l: (b, 0, 0)),
                  full((A_CONV, A_WIDTH)), full((B_CONV, B_WIDTH)), full((1, B_WIDTH)),
                  full((1, B_WIDTH)), full((1, B_WIDTH)), full((wide, d))],
        out_specs=pl.BlockSpec((1, tl, d), lambda b, l: (b, l, 0)),
        out_shape=jax.ShapeDtypeStruct((bsz, seq, d), F32),
        scratch_shapes=[pltpu.VMEM((CONV_HALO + tl, A_WIDTH), F32),
                        pltpu.VMEM((CONV_HALO + tl, B_WIDTH), F32),
                        pltpu.VMEM((tl, wide), BF16)],
        compiler_params=_cparams(("arbitrary", "arbitrary")),
        name="conv_mix",
    )(z, x, g1.reshape(bsz, 1, d), conv_a, conv_b, conv_b_bias.reshape(1, -1), ln_g.reshape(1, -1),
      ln_b.reshape(1, -1), w_out_bf16)


MOE_ROWS = 512
ROUTER_ROWS = 512
META_GATE0, META_GATE1, META_POS0, META_POS1 = range(4)
SEG_LEN, SEG_BASE, SEG_OFF = range(3)
SEG_ALIGN = 8
SEG_SIZES = tuple(2 ** b for b in range(10, 2, -1))
GROUP_LANE0 = N_EXPERTS


def _router_body(x_ref, g_ref, sh_ref, sc_ref, wr_ref, br_ref, tri_ref, upper_ref, h_ref, meta_ref, post_ref, seg_ref,
                 cnt_ref, base_scr):
    @pl.when(pl.program_id(0) == 0)
    def _():
        base_scr[...] = jnp.zeros_like(base_scr)

    h = _rms(x_ref[...]) * g_ref[...]
    h = h * (1.0 + sc_ref[0]) + sh_ref[0]
    h_ref[...] = h.astype(h_ref.dtype)
    h_hi = h.astype(BF16)
    h_lo = (h - h_hi.astype(F32)).astype(BF16)
    logits = (jnp.dot(h_hi, wr_ref[0], preferred_element_type=F32) + jnp.dot(h_hi, wr_ref[1], preferred_element_type=F32)
              + jnp.dot(h_lo, wr_ref[0], preferred_element_type=F32) + br_ref[...])
    tr = logits.shape[0]
    lane = lax.broadcasted_iota(I32, (tr, LANES), 1)
    lane_f = lane.astype(F32)
    neg = jnp.float32(-jnp.inf)
    big = jnp.float32(1e9)

    is_group = (lane >= GROUP_LANE0) & (lane < GROUP_LANE0 + N_GROUPS)
    glog = jnp.where(is_group, logits, neg)
    gmax = jnp.max(glog, axis=1, keepdims=True)
    p_top = 1.0 / jnp.sum(jnp.exp(glog - gmax), axis=1, keepdims=True)
    g_sel = jnp.min(jnp.where(glog == gmax, lane_f, big), axis=1, keepdims=True) - GROUP_LANE0
    lo = g_sel * EXPERTS_PER_GROUP
    in_group = (lane_f >= lo) & (lane_f < lo + EXPERTS_PER_GROUP)
    f1 = jnp.where(in_group, logits, neg)
    v1 = jnp.max(f1, axis=1, keepdims=True)
    i1 = jnp.min(jnp.where(f1 == v1, lane_f, big), axis=1, keepdims=True)
    f2 = jnp.where(lane_f == i1, neg, f1)
    v2 = jnp.max(f2, axis=1, keepdims=True)
    i2 = jnp.min(jnp.where(f2 == v2, lane_f, big), axis=1, keepdims=True)
    a = jnp.exp(v2 - v1)
    w1 = 1.0 / (1.0 + a)
    gate0 = p_top * w1
    gate1 = p_top * (a * w1)

    oh0 = lane_f == i1
    oh1 = lane_f == i2
    ind0 = jnp.where(oh0, 1.0, 0.0)
    ind1 = jnp.where(oh1, 1.0, 0.0)
    pre0 = jnp.dot(tri_ref[...], ind0.astype(BF16), preferred_element_type=F32)
    pre1 = jnp.dot(tri_ref[...], ind1.astype(BF16), preferred_element_type=F32)
    tot0 = jnp.sum(ind0, axis=0, keepdims=True)
    tot1 = jnp.sum(ind1, axis=0, keepdims=True)
    seg_len = jnp.floor((tot0 + tot1 + (SEG_ALIGN - 1)) * (1.0 / SEG_ALIGN)) * SEG_ALIGN
    seg_off = jnp.dot(jnp.broadcast_to(seg_len, (8, LANES)), upper_ref[...], precision=HIGHEST,
                      preferred_element_type=F32)[0:1]
    pos0 = jnp.sum(jnp.where(oh0, seg_off + pre0, 0.0), axis=1, keepdims=True)
    pos1 = jnp.sum(jnp.where(oh1, seg_off + tot0 + pre1, 0.0), axis=1, keepdims=True)
    base = base_scr[...]
    new_base = base + seg_len
    base_scr[...] = new_base
    cnt_ref[...] = new_base

    meta = jnp.zeros((tr, LANES), F32)
    for col, val in ((META_GATE0, gate0), (META_GATE1, gate1), (META_POS0, pos0), (META_POS1, pos1)):
        meta = jnp.where(lane == col, val, meta)
    meta_ref[...] = meta
    post_ref[0] = meta.T[0:8]
    srow = lax.broadcasted_iota(I32, (8, LANES), 0)
    seg = jnp.where(srow == SEG_LEN, seg_len, jnp.where(srow == SEG_BASE, base, jnp.where(srow == SEG_OFF, seg_off, 0.0)))
    seg_ref[0] = seg.astype(I32)


def _moe_router(x2, g, sh, sc, wr_g, br_g, wr_e, br_e, seq):
    n_tok, d = x2.shape
    bsz = n_tok // seq
    tr = min(ROUTER_ROWS, seq)
    steps_per_batch = seq // tr
    wr = jnp.zeros((d, LANES), F32).at[:, :N_EXPERTS].set(wr_e).at[:, GROUP_LANE0:GROUP_LANE0 + N_GROUPS].set(wr_g)
    wr_hi = wr.astype(BF16)
    wr = jnp.stack([wr_hi, (wr - wr_hi.astype(F32)).astype(BF16)])
    br = jnp.zeros((1, LANES), F32).at[0, :N_EXPERTS].set(br_e).at[0, GROUP_LANE0:GROUP_LANE0 + N_GROUPS].set(br_g)
    tri = jnp.tril(jnp.ones((tr, tr), BF16), -1)
    upper = jnp.triu(jnp.ones((LANES, LANES), F32), 1)
    nb = n_tok // tr
    full = lambda shape: pl.BlockSpec(shape, lambda i: (0,) * len(shape))
    return pl.pallas_call(
        _router_body,
        grid=(nb,),
        in_specs=[pl.BlockSpec((tr, d), lambda i: (i, 0)),
                  full((1, d)),
                  pl.BlockSpec((1, 1, d), lambda i: (i // steps_per_batch, 0, 0)),
                  pl.BlockSpec((1, 1, d), lambda i: (i // steps_per_batch, 0, 0)),
                  full((2, d, LANES)), full((1, LANES)), full((tr, tr)), full((LANES, LANES))],
        out_specs=[pl.BlockSpec((tr, d), lambda i: (i, 0)),
                   pl.BlockSpec((tr, LANES), lambda i: (i, 0)),
                   pl.BlockSpec((1, 8, tr), lambda i: (i, 0, 0)),
                   pl.BlockSpec((1, 8, LANES), lambda i: (i, 0, 0)),
                   full((1, LANES))],
        out_shape=[jax.ShapeDtypeStruct((n_tok, d), BF16),
                   jax.ShapeDtypeStruct((n_tok, LANES), F32),
                   jax.ShapeDtypeStruct((nb, 8, tr), F32),
                   jax.ShapeDtypeStruct((nb, 8, LANES), I32),
                   jax.ShapeDtypeStruct((1, LANES), F32)],
        scratch_shapes=[pltpu.VMEM((1, LANES), F32)],
        compiler_params=_cparams(("arbitrary",)),
        name="moe_router",
    )(x2, g.reshape(1, d), sh.reshape(bsz, 1, d), sc.reshape(bsz, 1, d), wr, br, tri, upper)


def _segment_copies(seg_ref, starts_ref, make_copy, action):
    def per_expert(e, carry):
        length = seg_ref[0, SEG_LEN, e]
        local = seg_ref[0, SEG_OFF, e]
        glob = starts_ref[e] + seg_ref[0, SEG_BASE, e]
        for size in SEG_SIZES:
            hit = (length & size) != 0

            @pl.when(hit)
            def _(local=local, glob=glob, size=size):
                cp = make_copy(pl.multiple_of(local, SEG_ALIGN), pl.multiple_of(glob, SEG_ALIGN), size)
                cp.start() if action == "start" else cp.wait()

            step = jnp.where(hit, size, 0)
            local = local + step
            glob = glob + step
        return carry

    lax.fori_loop(0, N_EXPERTS, per_expert, 0)


def _local_rows(tr):
    return 2 * tr + N_EXPERTS * SEG_ALIGN


def _row_words(d):
    return d // 2 + LANES


def _dispatch_body(starts_ref, seg_ref, prev_seg_ref, h_ref, post_ref, xr_ref, xs_scr, zero_scr, sems, *, tr):
    i = pl.program_id(0)
    slot = i % 2
    lb = xs_scr.shape[1]
    sem = sems.at[0]

    def make_copy_of(s):
        def make_copy(local, glob, size):
            return pltpu.make_async_copy(xs_scr.at[s, pl.ds(local, size)], xr_ref.at[pl.ds(glob, size)], sems.at[s])
        return make_copy

    rows = lax.broadcasted_iota(I32, (lb, tr), 0)
    first = rows == post_ref[0, META_POS0:META_POS0 + 1, :].astype(I32)
    second = rows == post_ref[0, META_POS1:META_POS1 + 1, :].astype(I32)
    x = jnp.dot(jnp.where(first | second, 1.0, 0.0).astype(BF16), h_ref[...], preferred_element_type=F32)
    bits = lax.bitcast_convert_type(x, I32)
    half = x.shape[1] // 2
    xs_scr[slot, :, 0:half] = lax.shift_right_logical(bits[:, 0:half], 16) | (bits[:, half:] & jnp.int32(-65536))
    gate = jnp.sum(jnp.where(first, post_ref[0, META_GATE0:META_GATE0 + 1, :], 0.0)
                   + jnp.where(second, post_ref[0, META_GATE1:META_GATE1 + 1, :], 0.0), axis=1, keepdims=True)
    xs_scr[slot, :, half:half + LANES] = jnp.broadcast_to(lax.bitcast_convert_type(gate, I32), (lb, LANES))
    _segment_copies(seg_ref, starts_ref, make_copy_of(slot), "start")

    @pl.when(i > 0)
    def _():
        _segment_copies(prev_seg_ref, starts_ref, make_copy_of(1 - slot), "wait")

    @pl.when(i == pl.num_programs(0) - 1)
    def _():
        _segment_copies(seg_ref, starts_ref, make_copy_of(slot), "wait")
        zero_scr[...] = jnp.zeros_like(zero_scr)
        total = starts_ref[N_EXPERTS]
        tail = (-total) & (MOE_ROWS - 1)
        for action in ("start", "wait"):
            row = total
            for size in SEG_SIZES:
                if size >= MOE_ROWS:
                    continue
                hit = (tail & size) != 0

                @pl.when(hit)
                def _(row=row, size=size, action=action):
                    cp = pltpu.make_async_copy(zero_scr.at[pl.ds(0, size)],
                                               xr_ref.at[pl.ds(pl.multiple_of(row, SEG_ALIGN), size)], sem)
                    cp.start() if action == "start" else cp.wait()

                row = row + jnp.where(hit, size, 0)

        def free_block(b):
            return pltpu.make_async_copy(
                zero_scr, xr_ref.at[pl.ds(pl.multiple_of(b * MOE_ROWS, MOE_ROWS), MOE_ROWS)], sem)

        first_free = (total + MOE_ROWS - 1) // MOE_ROWS
        n_blocks = xr_ref.shape[0] // MOE_ROWS
        lax.fori_loop(first_free, n_blocks, lambda b, c: (free_block(b).start(), c)[1], 0)
        lax.fori_loop(first_free, n_blocks, lambda b, c: (free_block(b).wait(), c)[1], 0)


def _moe_dispatch(h2, post, seg, starts, n_rows, tr):
    n_tok, d = h2.shape
    grid_spec = pltpu.PrefetchScalarGridSpec(
        num_scalar_prefetch=1,
        grid=(n_tok // tr,),
        in_specs=[pl.BlockSpec((1, 8, LANES), lambda i, s: (i, 0, 0), memory_space=pltpu.SMEM),
                  pl.BlockSpec((1, 8, LANES), lambda i, s: (jnp.maximum(i - 1, 0), 0, 0), memory_space=pltpu.SMEM),
                  pl.BlockSpec((tr, d), lambda i, s: (i, 0)),
                  pl.BlockSpec((1, 8, tr), lambda i, s: (i, 0, 0))],
        out_specs=pl.BlockSpec(memory_space=pl.ANY),
        scratch_shapes=[pltpu.VMEM((2, _local_rows(tr), _row_words(d)), I32), pltpu.VMEM((MOE_ROWS, _row_words(d)), I32),
                        pltpu.SemaphoreType.DMA((2,))],
    )
    return pl.pallas_call(
        functools.partial(_dispatch_body, tr=tr),
        grid_spec=grid_spec,
        out_shape=jax.ShapeDtypeStruct((n_rows, _row_words(d)), I32),
        compiler_params=_cparams(("arbitrary",)),
        name="moe_dispatch",
    )(starts, seg, seg, h2, post)


def _expert_body(pb_ref, pe_ref, plo_ref, phi_ref, x_ref, wg_ref, wu_ref, wd_ref, o_ref, wg_s, wu_s, wd_s):
    p = pl.program_id(0)
    prev = jnp.maximum(p - 1, 0)
    new_expert = (p == 0) | (pe_ref[p] != pe_ref[prev])
    first = (p == 0) | (pb_ref[p] != pb_ref[prev])

    @pl.when(new_expert)
    def _():
        wg_s[...] = wg_ref[0, 0].astype(BF16)
        wu_s[...] = wu_ref[0, 0].astype(BF16)
        wd_s[...] = wd_ref[0, 0].astype(BF16)

    def rows_of_expert():
        half = wg_s.shape[0] // 2
        words = x_ref[:, 0:half]
        x_lo = lax.bitcast_convert_type(words << 16, F32).astype(BF16)
        x_hi = lax.bitcast_convert_type(words & jnp.int32(-65536), F32).astype(BF16)
        gate = lax.bitcast_convert_type(x_ref[:, half:half + 1], F32)
        gt = (jnp.dot(x_lo, wg_s[0:half], preferred_element_type=F32)
              + jnp.dot(x_hi, wg_s[half:], preferred_element_type=F32))
        up = (jnp.dot(x_lo, wu_s[0:half], preferred_element_type=F32)
              + jnp.dot(x_hi, wu_s[half:], preferred_element_type=F32))
        act = (gt * _sigmoid(gt)) * up
        y = jnp.dot(act.astype(BF16), wd_s[...], preferred_element_type=F32) * gate
        rows = lax.broadcasted_iota(I32, (y.shape[0], 1), 0)
        return jnp.where((rows >= plo_ref[p]) & (rows < phi_ref[p]), y, 0.0)

    nonempty = phi_ref[p] > plo_ref[p]

    @pl.when(first & nonempty)
    def _():
        o_ref[...] = rows_of_expert()

    @pl.when(first & jnp.logical_not(nonempty))
    def _():
        o_ref[...] = jnp.zeros_like(o_ref)

    @pl.when(jnp.logical_not(first) & nonempty)
    def _():
        o_ref[...] += rows_of_expert()


def _moe_experts(x_rows, pairs, w_gate, w_up, w_down, layer):
    n_rows = x_rows.shape[0]
    d, de = w_gate.shape[-2:]
    n_pairs = pairs[0].shape[0]
    grid_spec = pltpu.PrefetchScalarGridSpec(
        num_scalar_prefetch=4,
        grid=(n_pairs,),
        in_specs=[pl.BlockSpec((MOE_ROWS, _row_words(d)), lambda p, pb, pe, lo, hi: (pb[p], 0)),
                  pl.BlockSpec((1, 1, d, de), lambda p, pb, pe, lo, hi: (layer, pe[p], 0, 0)),
                  pl.BlockSpec((1, 1, d, de), lambda p, pb, pe, lo, hi: (layer, pe[p], 0, 0)),
                  pl.BlockSpec((1, 1, de, d), lambda p, pb, pe, lo, hi: (layer, pe[p], 0, 0))],
        out_specs=pl.BlockSpec((MOE_ROWS, d), lambda p, pb, pe, lo, hi: (pb[p], 0)),
        scratch_shapes=[pltpu.VMEM((d, de), BF16), pltpu.VMEM((d, de), BF16), pltpu.VMEM((de, d), BF16)],
    )
    return pl.pallas_call(
        _expert_body,
        grid_spec=grid_spec,
        out_shape=jax.ShapeDtypeStruct((n_rows, d), F32),
        compiler_params=_cparams(("arbitrary",)),
        name="moe_experts",
    )(*pairs, x_rows, w_gate, w_up, w_down)


def _combine_body(starts_ref, seg_ref, next_seg_ref, y_ref, x_ref, meta_ref, g2_ref, fg_ref, o_ref, ys_scr, sems, *,
                  final_norm):
    i = pl.program_id(0)
    slot = i % 2

    def make_copy_of(s):
        def make_copy(local, glob, size):
            return pltpu.make_async_copy(y_ref.at[pl.ds(glob, size)], ys_scr.at[s, pl.ds(local, size)], sems.at[s])
        return make_copy

    @pl.when(i == 0)
    def _():
        ys_scr[...] = jnp.zeros_like(ys_scr)
        _segment_copies(seg_ref, starts_ref, make_copy_of(slot), "start")

    @pl.when(i + 1 < pl.num_programs(0))
    def _():
        _segment_copies(next_seg_ref, starts_ref, make_copy_of(1 - slot), "start")

    _segment_copies(seg_ref, starts_ref, make_copy_of(slot), "wait")

    meta = meta_ref[...]
    tr = meta.shape[0]
    y = ys_scr[slot].astype(BF16)
    cols = lax.broadcasted_iota(I32, (tr, ys_scr.shape[1]), 1)

    pick = ((cols == meta[:, META_POS0:META_POS0 + 1].astype(I32))
            | (cols == meta[:, META_POS1:META_POS1 + 1].astype(I32)))
    moe = jnp.dot(jnp.where(pick, 1.0, 0.0).astype(BF16), y, preferred_element_type=F32)
    xn = x_ref[...] + g2_ref[0] * moe
    if final_norm:
        xn = _rms(xn) * fg_ref[...]
    o_ref[...] = xn


def _moe_combine(y_rows, seg, starts, x2, meta, g2, final_g, seq, tr, final_norm):
    n_tok, d = x2.shape
    bsz = n_tok // seq
    steps_per_batch = seq // tr
    nb = n_tok // tr
    grid_spec = pltpu.PrefetchScalarGridSpec(
        num_scalar_prefetch=1,
        grid=(nb,),
        in_specs=[pl.BlockSpec((1, 8, LANES), lambda i, s: (i, 0, 0), memory_space=pltpu.SMEM),
                  pl.BlockSpec((1, 8, LANES), lambda i, s: (jnp.minimum(i + 1, nb - 1), 0, 0), memory_space=pltpu.SMEM),
                  pl.BlockSpec(memory_space=pl.ANY),
                  pl.BlockSpec((tr, d), lambda i, s: (i, 0)),
                  pl.BlockSpec((tr, LANES), lambda i, s: (i, 0)),
                  pl.BlockSpec((1, 1, d), lambda i, s: (i // steps_per_batch, 0, 0)),
                  pl.BlockSpec((1, d), lambda i, s: (0, 0))],
        out_specs=pl.BlockSpec((tr, d), lambda i, s: (i, 0)),
        scratch_shapes=[pltpu.VMEM((2, _local_rows(tr), d), F32), pltpu.SemaphoreType.DMA((2,))],
    )
    return pl.pallas_call(
        functools.partial(_combine_body, final_norm=final_norm),
        grid_spec=grid_spec,
        out_shape=jax.ShapeDtypeStruct((n_tok, d), F32),
        compiler_params=_cparams(("arbitrary",)),
        name="moe_combine",
    )(starts, seg, seg, y_rows, x2, meta, g2.reshape(bsz, 1, d), final_g.reshape(1, d))


def _expert_pairs(counts, n_rows):
    n_blocks = n_rows // MOE_ROWS
    n_pairs = n_blocks + N_EXPERTS
    ends = jnp.cumsum(counts)
    starts = ends - counts
    first_blk = starts // MOE_ROWS
    last_blk = (ends - 1) // MOE_ROWS
    npairs = jnp.where(counts > 0, last_blk - first_blk + 1, 0)
    pend = jnp.cumsum(npairs)
    poff = pend - npairs
    total = pend[-1]
    used_blocks = (ends[-1] + MOE_ROWS - 1) // MOE_ROWS
    p = jnp.arange(n_pairs, dtype=I32)
    p_eff = jnp.minimum(p, total - 1)
    e = jnp.minimum(jnp.sum(pend[None, :] <= p_eff[:, None], axis=1), N_EXPERTS - 1).astype(I32)
    table = jnp.stack([first_blk, poff, starts, ends], axis=1).astype(F32)
    onehot = (e[:, None] == jnp.arange(N_EXPERTS, dtype=I32)[None, :]).astype(F32)
    first_e, poff_e, start_e, end_e = jnp.dot(onehot, table, precision=HIGHEST).astype(I32).T
    valid = p < total
    blk = jnp.where(valid, first_e + p_eff - poff_e, jnp.minimum(used_blocks + p - total, n_blocks - 1)).astype(I32)
    lo = jnp.where(valid, jnp.clip(start_e - blk * MOE_ROWS, 0, MOE_ROWS), 0).astype(I32)
    hi = jnp.where(valid, jnp.clip(end_e - blk * MOE_ROWS, 0, MOE_ROWS), 0).astype(I32)
    return blk, e, lo, hi


def _hier_moe(x, g, sh, sc, g2, wr_g, br_g, wr_e, br_e, w_gate, w_up, w_down, layer, final_g, final_norm):
    bsz, seq, d = x.shape
    n_tok = bsz * seq
    tr = min(ROUTER_ROWS, seq)
    assert 2 * tr <= SEG_SIZES[0]
    x2 = x.reshape(n_tok, d)
    h2, meta, post, seg, cnt = _moe_router(x2, g, sh, sc, wr_g, br_g, wr_e, br_e, seq)
    counts = cnt[0, :N_EXPERTS].astype(I32)
    ends = jnp.cumsum(counts)
    starts = jnp.concatenate([ends - counts, ends[-1:]])
    n_rows = -(-(2 * n_tok + (n_tok // tr) * N_EXPERTS * (SEG_ALIGN - 1)) // MOE_ROWS) * MOE_ROWS
    x_rows = _moe_dispatch(h2, post, seg, starts, n_rows, tr)
    y_rows = _moe_experts(x_rows, _expert_pairs(counts, n_rows), w_gate, w_up, w_down, layer)
    out = _moe_combine(y_rows, seg, starts, x2, meta, g2, final_g, seq, tr, final_norm)
    return out.reshape(bsz, seq, d)


SEL_COLS = 512
SEL_CHUNK = 512
BIT_GROUP = 256
SEL_SWEEP = 64
ATT_TILE = 1024


def _idx_score_t(k, qt, wt):
    acc = None
    for h in range(IDX_HEADS):
        rel = jnp.dot(k, qt[h * IDX_DIM:(h + 1) * IDX_DIM], preferred_element_type=F32)
        term = jnp.maximum(rel, 0.0) * wt[h:h + 1, :]
        acc = term if acc is None else acc + term
    return acc


def _bit_transpose32(a):
    a = list(a)
    m, j = 0x0000FFFF, 16
    while j:
        k = 0
        while k < 32:
            t = (a[k] ^ lax.shift_right_logical(a[k + j], jnp.int32(j))) & jnp.int32(m)
            a[k] = a[k] ^ t
            a[k + j] = a[k + j] ^ (t << j)
            k = (k + j + 1) & ~j
        j >>= 1
        m = (m ^ (m << j)) & 0xFFFFFFFF
    return a


def _select_body(qt_ref, zk_ref, wt_ref, o_ref, planes_scr, eq_scr, gt_scr, *, topk):
    i = pl.program_id(1)
    tq = qt_ref.shape[2]
    qt = qt_ref[0]
    wt = wt_ref[0].astype(F32)
    n_chunks = ((i + 1) * tq + SEL_CHUNK - 1) // SEL_CHUNK
    krow0 = lax.broadcasted_iota(I32, (SEL_CHUNK, tq), 0)
    qcol = i * tq + lax.broadcasted_iota(I32, (SEL_CHUNK, tq), 1)
    int_min = jnp.int32(INT_MIN)

    groups = SEL_CHUNK // BIT_GROUP
    words = SEL_CHUNK // 32
    sweep_rows = min(SEL_SWEEP, eq_scr.shape[0])

    def fill(masked, c, carry):
        off = pl.multiple_of(c * SEL_CHUNK, SEL_CHUNK)
        k = zk_ref[0, pl.ds(off, SEL_CHUNK), :][:, 0:IDX_DIM]
        bits = lax.bitcast_convert_type(_idx_score_t(k, qt, wt), I32)
        key = bits ^ ((bits >> 31) | int_min)
        if masked:
            key = jnp.where(krow0 + off <= qcol, key, 0)
        for g in range(groups):
            for lt in range(0, tq, LANES):
                ku = key[g * BIT_GROUP:(g + 1) * BIT_GROUP, lt:lt + LANES]
                planes = _bit_transpose32([ku[8 * r:8 * r + 8] for r in range(32)])
                wrow = pl.multiple_of(c * words + g * 8, 8)
                for b in range(32):
                    planes_scr[b, pl.ds(wrow, 8), lt:lt + LANES] = planes[b]
        return carry

    n_plain = (i * tq + 1) // SEL_CHUNK
    lax.fori_loop(0, n_plain, functools.partial(fill, False), 0)
    lax.fori_loop(n_plain, n_chunks, functools.partial(fill, True), 0)

    n_sweep = (n_chunks * words + sweep_rows - 1) // sweep_rows

    def pad(c, carry):
        wrow = pl.multiple_of(c * words, words)
        for b in range(32):
            planes_scr[b, pl.ds(wrow, words), :] = jnp.zeros((words, tq), I32)
        return carry

    lax.fori_loop(n_chunks, n_sweep * (sweep_rows // words), pad, 0)

    def sweep(upd, cnt_plane):
        def body(sb, acc):
            r0 = pl.multiple_of(sb * sweep_rows, sweep_rows)
            eq = eq_scr[pl.ds(r0, sweep_rows), :]
            gt = gt_scr[pl.ds(r0, sweep_rows), :]
            if upd is not None:
                plane, accept = upd
                hit = eq & planes_scr[plane, pl.ds(r0, sweep_rows), :]
                gt = jnp.where(accept, gt, gt | hit)
                eq = jnp.where(accept, hit, eq ^ hit)
                eq_scr[pl.ds(r0, sweep_rows), :] = eq
                gt_scr[pl.ds(r0, sweep_rows), :] = gt
            if cnt_plane is None:
                return acc
            ones = lax.population_count(gt | (eq & planes_scr[cnt_plane, pl.ds(r0, sweep_rows), :]))
            return acc + jnp.sum(ones.reshape(sweep_rows // 8, 8, tq), axis=0)
        acc = lax.fori_loop(0, n_sweep, body, jnp.zeros((8, tq), I32))
        return jnp.sum(acc.astype(F32), axis=0, keepdims=True)

    k_f = jnp.float32(topk)
    eq_scr[...] = jnp.full(eq_scr.shape, -1, I32)
    gt_scr[...] = jnp.zeros(gt_scr.shape, I32)

    def bit_step(ib, carry):
        u, cnt = carry
        accept = cnt >= k_f
        u = jnp.where(accept, u | jnp.left_shift(jnp.int32(1), 32 - ib), u)
        return u, sweep((ib - 1, accept), ib)

    u, cnt = lax.fori_loop(1, 32, bit_step, (jnp.zeros((1, tq), I32), sweep(None, 0)))
    accept = cnt >= k_f
    u = jnp.where(accept, u | 1, u)
    sweep((31, accept), None)
    some = jnp.where(u != 0, -1, 0)

    def popcount_rows(word_of):
        def body(sb, acc):
            r0 = pl.multiple_of(sb * sweep_rows, sweep_rows)
            ones = lax.population_count(word_of(r0))
            return acc + jnp.sum(ones.reshape(sweep_rows // 8, 8, tq), axis=0)
        acc = lax.fori_loop(0, n_sweep, body, jnp.zeros((8, tq), I32))
        return jnp.sum(acc.astype(F32), axis=0, keepdims=True)

    n_gt = popcount_rows(lambda r0: gt_scr[pl.ds(r0, sweep_rows), :])
    n_eq = jnp.where(u != 0, popcount_rows(lambda r0: eq_scr[pl.ds(r0, sweep_rows), :]), 0.0)

    @pl.when(jnp.max(n_gt + n_eq) > k_f)
    def _():
        wrow = lax.broadcasted_iota(I32, (sweep_rows, tq), 0)

        def below(cap, r0):
            w = wrow + r0
            last = lax.shift_right_arithmetic(cap - (w >> 3) * BIT_GROUP - (w & 7) - 1, 3)
            clear = jnp.clip(31 - last, 0, 32)
            return jnp.where(clear >= 32, 0, jnp.left_shift(jnp.int32(-1), jnp.minimum(clear, 31)))

        need = k_f - n_gt

        def cap_step(ib, v):
            cand = v | jnp.left_shift(jnp.int32(1), 14 - ib)
            cnt = popcount_rows(lambda r0: eq_scr[pl.ds(r0, sweep_rows), :] & below(cand, r0))
            return jnp.where(cnt <= need, cand, v)

        cap = lax.fori_loop(0, 15, cap_step, jnp.zeros((1, tq), I32))
        cap = jnp.where((u != 0) & (n_gt + n_eq > k_f), cap, 2 ** 30)

        def trim(sb, carry):
            r0 = pl.multiple_of(sb * sweep_rows, sweep_rows)
            eq_scr[pl.ds(r0, sweep_rows), :] = eq_scr[pl.ds(r0, sweep_rows), :] & below(cap, r0)
            return carry

        lax.fori_loop(0, n_sweep, trim, 0)

    def emit(c, carry):
        off = pl.multiple_of(c * SEL_CHUNK, SEL_CHUNK)
        for g in range(groups):
            wrow = pl.multiple_of(c * words + g * 8, 8)
            keep = gt_scr[pl.ds(wrow, 8), :] | (eq_scr[pl.ds(wrow, 8), :] & some)
            rows = [jnp.where((lax.shift_right_logical(keep, 31 - r) & 1) != 0, 0.0, -jnp.inf) for r in range(32)]
            o_ref[0, pl.ds(pl.multiple_of(off + g * BIT_GROUP, BIT_GROUP), BIT_GROUP), :] = (
                jnp.concatenate(rows, axis=0).astype(o_ref.dtype))
        return carry

    lax.fori_loop(0, n_chunks, emit, 0)

    def blank(c, carry):
        off = pl.multiple_of(c * SEL_CHUNK, SEL_CHUNK)
        o_ref[0, pl.ds(off, SEL_CHUNK), :] = jnp.full((SEL_CHUNK, tq), -jnp.inf, o_ref.dtype)
        return carry

    lax.fori_loop(n_chunks, o_ref.shape[1] // SEL_CHUNK, blank, 0)


def _select(z, zt, topk):
    bsz, seq, _ = z.shape
    tq = min(SEL_COLS, seq)
    assert seq % SEL_CHUNK == 0 and seq % tq == 0
    return pl.pallas_call(
        functools.partial(_select_body, topk=topk),
        grid=(bsz, seq // tq),
        in_specs=[pl.BlockSpec((1, ZT_QIDX_ROWS, tq), lambda b, i: (b, ZT_QIDX_OFF // ZT_QIDX_ROWS, i)),
                  pl.BlockSpec((1, seq, LANES), lambda b, i: (b, 0, Z_KIDX_OFF // LANES)),
                  pl.BlockSpec((1, BF16_ROWS, tq), lambda b, i: (b, ZT_W_OFF // BF16_ROWS, i))],
        out_specs=pl.BlockSpec((1, seq, tq), lambda b, i: (b, 0, i)),
        out_shape=jax.ShapeDtypeStruct((bsz, seq, seq), BF16),
        scratch_shapes=[pltpu.VMEM((32, seq // 32, tq), I32),
                        pltpu.VMEM((seq // 32, tq), I32), pltpu.VMEM((seq // 32, tq), I32)],
        compiler_params=_cparams(("arbitrary", "arbitrary")),
        name="dsa_select",
    )(zt, z, zt)


BIAS_INIT_ROWS = 8
ONES_ROWS = BF16_ROWS


BAND = 128


def _init_bias_blocks(btile, bias_ref, head0, n_heads):
    ta = BAND
    col = lax.broadcasted_iota(I32, (BIAS_INIT_ROWS, ta), 1)
    row0 = lax.broadcasted_iota(I32, (BIAS_INIT_ROWS, ta), 0)

    def body(r, carry):
        off = pl.multiple_of(r * BIAS_INIT_ROWS, BIAS_INIT_ROWS)
        for kind in range(2):
            dist = col - (row0 + off) + kind * ta
            for h in range(n_heads):
                far = bias_ref[NUM_BUCKETS - 1, head0 + h]
                val = jnp.full((BIAS_INIT_ROWS, ta), (bias_ref[0, head0 + h] - far) * LOG2E, F32)
                for b in range(1, NUM_BUCKETS - 1):
                    val = jnp.where(dist >= BUCKET_START[b], (bias_ref[b, head0 + h] - far) * LOG2E, val)
                val = jnp.where(dist >= BUCKET_START[NUM_BUCKETS - 1], 0.0, val)
                btile[h, kind, pl.ds(off, BIAS_INIT_ROWS), :] = val
        return carry

    lax.fori_loop(0, ta // BIAS_INIT_ROWS, body, 0)


def _add_bias_band(s_scr, btile, h, kind, ta):
    nb = ta // BAND
    blk = lambda i: slice(i * BAND, (i + 1) * BAND)
    if kind == 0:
        for kb in range(nb):
            s_scr[blk(kb), blk(kb)] += btile[h, 0]
            if kb + 1 < nb:
                s_scr[blk(kb), blk(kb + 1)] += btile[h, 1]
    else:
        s_scr[blk(nb - 1), blk(0)] += btile[h, 1]


def _with_ones(vt):
    return jnp.concatenate([vt, jnp.ones((ONES_ROWS, vt.shape[1]), vt.dtype)], axis=0)


def _softmax_step_t(logits, v_aug, m_ref, acc_ref, idx):
    m_old = m_ref[idx]
    m_new = jnp.maximum(m_old, jnp.max(logits, axis=0, keepdims=True))
    m_safe = jnp.where(m_new == -jnp.inf, 0.0, m_new)
    p = jnp.exp2(logits - m_safe)
    alpha = jnp.exp2(m_old - m_safe)
    acc_ref[idx] = alpha * acc_ref[idx] + jnp.dot(v_aug, p.astype(BF16), preferred_element_type=F32)
    m_ref[idx] = m_new


def _reset_softmax(m_s, acc_s):
    m_s[...] = jnp.full(m_s.shape, -jnp.inf, F32)
    acc_s[...] = jnp.zeros(acc_s.shape, F32)


def _cattn_body(qi_ref, ki_ref, bias_ref, qt_ref, zk_ref, vt_ref, mask_ref, o_ref, m_s, acc_s, btile, s_scr):
    b = pl.program_id(0)
    p = pl.program_id(1)
    qi = qi_ref[p]
    ki = ki_ref[p]
    ta = zk_ref.shape[1]
    hd = C_HEAD_DIM

    @pl.when((b == 0) & (p == 0))
    def _():
        _init_bias_blocks(btile, bias_ref, 0, C_HEADS)

    @pl.when(ki == 0)
    def _():
        _reset_softmax(m_s, acc_s)

    def heads(kind):
        for h in range(C_HEADS):
            lo = h * hd
            logits = (jnp.dot(zk_ref[0, :, lo:lo + hd], qt_ref[0, lo:lo + hd, :], preferred_element_type=F32)
                      + mask_ref[0].astype(F32))
            if kind is not None:
                s_scr[...] = logits
                _add_bias_band(s_scr, btile, h, kind, ta)
                logits = s_scr[...]
            _softmax_step_t(logits, _with_ones(vt_ref[0, lo:lo + hd, :]), m_s, acc_s, h)

    for kind in (0, 1):
        pl.when(ki == qi - kind)(functools.partial(heads, kind))
    pl.when(ki < qi - 1)(functools.partial(heads, None))

    @pl.when(ki == qi)
    def _():
        outs = []
        for h in range(C_HEADS):
            a = acc_s[h]
            outs.append((a[0:hd] / a[hd:hd + 1]).T)
        o_ref[0] = jnp.concatenate(outs, axis=1).astype(o_ref.dtype)


def _dattn_body(qi_ref, ki_ref, bias_ref, qt_ref, zk_ref, vt_ref, lam_ref, ng_ref, o_ref,
                m_s, acc_s, btile, s_scr, causal_s, *, lambda_init):
    b = pl.program_id(0)
    p = pl.program_id(1)
    qi = qi_ref[p]
    ki = ki_ref[p]
    ta = zk_ref.shape[1]
    hd = DIFF_HEAD_DIM
    dv = 2 * hd

    @pl.when((b == 0) & (p == 0))
    def _():
        _init_bias_blocks(btile, bias_ref, C_HEADS, DIFF_HEADS)
        krow = lax.broadcasted_iota(I32, (ta, ta), 0)
        qcol = lax.broadcasted_iota(I32, (ta, ta), 1)
        causal_s[...] = jnp.where(krow <= qcol, 0.0, -jnp.inf)

    @pl.when(ki == 0)
    def _():
        _reset_softmax(m_s, acc_s)

    def heads(kind):
        for h in range(DIFF_HEADS):
            v_aug = _with_ones(vt_ref[0, dv * h:dv * (h + 1), :])
            for j in range(2):
                lo = (2 * h + j) * hd
                logits = jnp.dot(zk_ref[0, :, lo:lo + hd], qt_ref[0, lo:lo + hd, :], preferred_element_type=F32)
                if kind is not None:
                    s_scr[...] = logits + causal_s[...] if kind == 0 else logits
                    _add_bias_band(s_scr, btile, h, kind, ta)
                    logits = s_scr[...]
                _softmax_step_t(logits, v_aug, m_s, acc_s, 2 * h + j)

    for kind in (0, 1):
        pl.when(ki == qi - kind)(functools.partial(heads, kind))
    pl.when(ki < qi - 1)(functools.partial(heads, None))

    @pl.when(ki == qi)
    def _():
        lam_p = lam_ref[...]
        lam = (jnp.exp(jnp.sum(lam_p[0:1] * lam_p[1:2], axis=1, keepdims=True))
               - jnp.exp(jnp.sum(lam_p[2:3] * lam_p[3:4], axis=1, keepdims=True)) + lambda_init)
        outs = []
        for h in range(DIFF_HEADS):
            a1 = acc_s[2 * h]
            a2 = acc_s[2 * h + 1]
            o = a1[0:dv] / a1[dv:dv + 1] - lam * (a2[0:dv] / a2[dv:dv + 1])
            o = o * lax.rsqrt(jnp.mean(o * o, axis=0, keepdims=True) + EPS) * ng_ref[...] * (1.0 - lambda_init)
            outs.append(o.T)
        o_ref[0] = jnp.concatenate(outs, axis=1).astype(o_ref.dtype)


def _causal_pairs(nq):
    qi = [q for q in range(nq) for _ in range(q + 1)]
    ki = [k for q in range(nq) for k in range(q + 1)]
    return jnp.asarray(qi, I32), jnp.asarray(ki, I32)


def _cattn(z, zt, mask, rel_bias, ta):
    bsz, seq, _ = z.shape
    qi, ki = _causal_pairs(seq // ta)
    qmap = lambda col: (lambda b, p, qi, ki: (b, qi[p], col))
    kmap = lambda col: (lambda b, p, qi, ki: (b, ki[p], col))
    dv_aug = C_HEAD_DIM + ONES_ROWS
    grid_spec = pltpu.PrefetchScalarGridSpec(
        num_scalar_prefetch=2,
        grid=(bsz, qi.shape[0]),
        in_specs=[pl.BlockSpec(memory_space=pltpu.SMEM),
                  pl.BlockSpec((1, C_WIDTH, ta), lambda b, p, qi, ki: (b, ZT_QC, qi[p])),
                  pl.BlockSpec((1, ta, C_WIDTH), kmap(Z_KC)),
                  pl.BlockSpec((1, C_WIDTH, ta), lambda b, p, qi, ki: (b, ZT_VC, ki[p])),
                  pl.BlockSpec((1, ta, ta), lambda b, p, qi, ki: (b, ki[p], qi[p]))],
        out_specs=pl.BlockSpec((1, ta, C_WIDTH), qmap(0)),
        scratch_shapes=[pltpu.VMEM((C_HEADS, 1, ta), F32),
                        pltpu.VMEM((C_HEADS, dv_aug, ta), F32),
                        pltpu.VMEM((C_HEADS, 2, BAND, BAND), F32),
                        pltpu.VMEM((ta, ta), F32)],
    )
    return pl.pallas_call(
        _cattn_body,
        grid_spec=grid_spec,
        out_shape=jax.ShapeDtypeStruct((bsz, seq, C_WIDTH), BF16),
        compiler_params=_cparams(("arbitrary", "arbitrary")),
        name="dsa_attn",
    )(qi, ki, rel_bias, zt, z, zt, mask)


def _dattn(z, zt, rel_bias, diff_lam, diff_norm_g, lambda_init, ta):
    bsz, seq, _ = z.shape
    qi, ki = _causal_pairs(seq // ta)
    n_maps = 2 * DIFF_HEADS
    dv = 2 * DIFF_HEAD_DIM
    qmap = lambda col: (lambda b, p, qi, ki: (b, qi[p], col))
    kmap = lambda col: (lambda b, p, qi, ki: (b, ki[p], col))
    grid_spec = pltpu.PrefetchScalarGridSpec(
        num_scalar_prefetch=2,
        grid=(bsz, qi.shape[0]),
        in_specs=[pl.BlockSpec(memory_space=pltpu.SMEM),
                  pl.BlockSpec((1, DIFF_W, ta), lambda b, p, qi, ki: (b, ZT_QD, qi[p])),
                  pl.BlockSpec((1, ta, DIFF_W), kmap(Z_KD)),
                  pl.BlockSpec((1, DIFF_W, ta), lambda b, p, qi, ki: (b, ZT_VD, ki[p])),
                  pl.BlockSpec(diff_lam.shape, lambda b, p, qi, ki: (0, 0)),
                  pl.BlockSpec((dv, 1), lambda b, p, qi, ki: (0, 0))],
        out_specs=pl.BlockSpec((1, ta, DIFF_W), qmap(0)),
        scratch_shapes=[pltpu.VMEM((n_maps, 1, ta), F32),
                        pltpu.VMEM((n_maps, dv + ONES_ROWS, ta), F32),
                        pltpu.VMEM((DIFF_HEADS, 2, BAND, BAND), F32),
                        pltpu.VMEM((ta, ta), F32), pltpu.VMEM((ta, ta), F32)],
    )
    return pl.pallas_call(
        functools.partial(_dattn_body, lambda_init=lambda_init),
        grid_spec=grid_spec,
        out_shape=jax.ShapeDtypeStruct((bsz, seq, DIFF_W), BF16),
        compiler_params=_cparams(("arbitrary", "arbitrary")),
        name="diff_attn",
    )(qi, ki, rel_bias, zt, z, zt, diff_lam, diff_norm_g.reshape(dv, 1))


def _out_proj_body(oc_ref, od_ref, x_ref, g1_ref, w_ref, o_ref):
    y = (jnp.dot(oc_ref[0], w_ref[0:C_WIDTH, :], preferred_element_type=F32)
         + jnp.dot(od_ref[0], w_ref[C_WIDTH:C_WIDTH + DIFF_W, :], preferred_element_type=F32))
    o_ref[0] = x_ref[0] + g1_ref[0] * y


def _out_proj(out_c, out_d, x, g1, w_out_bf16, tm):
    bsz, seq, d = x.shape
    return pl.pallas_call(
        _out_proj_body,
        grid=(bsz, seq // tm),
        in_specs=[pl.BlockSpec((1, tm, C_WIDTH), lambda b, i: (b, i, 0)),
                  pl.BlockSpec((1, tm, DIFF_W), lambda b, i: (b, i, 0)),
                  pl.BlockSpec((1, tm, d), lambda b, i: (b, i, 0)),
                  pl.BlockSpec((1, 1, d), lambda b, i: (b, 0, 0)),
                  pl.BlockSpec((C_WIDTH + DIFF_W, d), lambda b, i: (0, 0))],
        out_specs=pl.BlockSpec((1, tm, d), lambda b, i: (b, i, 0)),
        out_shape=jax.ShapeDtypeStruct((bsz, seq, d), F32),
        compiler_params=_cparams(("arbitrary", "arbitrary")),
        name="attn_out_proj",
    )(out_c, out_d, x, g1.reshape(bsz, 1, d), w_out_bf16)


def _attn_in_weights(cd_w_in):
    sizes = (C_WIDTH, C_WIDTH, C_WIDTH, IDX_HEADS * IDX_DIM, IDX_DIM, IDX_HEADS, DIFF_W, DIFF_W, DIFF_W)
    cuts = np.cumsum(sizes)[:-1]
    q_c, k_c, v_c, q_i, k_i, w_i, q_d, k_d, v_d = jnp.split(cd_w_in, cuts, axis=1)
    d = cd_w_in.shape[0]
    w = jnp.concatenate([k_c, k_d, k_i, jnp.zeros((d, LANES - IDX_DIM), cd_w_in.dtype)], axis=1)
    wt = jnp.concatenate([v_c, v_d, q_c * (C_HEAD_DIM ** -0.5 * LOG2E), q_d * (DIFF_HEAD_DIM ** -0.5 * LOG2E), q_i,
                          w_i * (IDX_DIM * IDX_HEADS) ** -0.5,
                          jnp.zeros((d, BF16_ROWS - IDX_HEADS), cd_w_in.dtype)], axis=1).T
    return w.astype(BF16), wt.astype(BF16)


def kernel(x, c, positions, rel_bias, norm_g, final_norm_g, ada_w, ada_b, ab_w_in, ab_conv_a, ab_conv_b,
           ab_conv_b_bias, ab_ln_g, ab_ln_b, ab_w_out, cd_w_in, diff_lam, diff_norm_g, cd_w_out,
           moe_wr_g, moe_br_g, moe_wr_e, moe_br_e, moe_w_gate, moe_w_up, moe_w_down):
    del positions
    bsz, seq, d = x.shape
    depth = ada_w.shape[0]
    tm = min(512, seq)
    ta = min(ATT_TILE, seq)
    assert ta % BAND == 0 and BUCKET_START[-1] < BAND and seq % ta == 0
    topk = min(TOPK_MAX, seq // 4)
    mods = _ada_mod(c, ada_w, ada_b)
    for i in range(depth):
        sh1, sc1, g1, sh2, sc2, g2 = jnp.split(mods[i], 6, axis=-1)
        j = i // 2
        if i % 2 == 0:
            z = _norm_proj(x, norm_g[i, 0], sh1, sc1, ab_w_in[j].astype(BF16), tm)
            x = _conv_mix(z, x, g1, ab_conv_a[j], ab_conv_b[j], ab_conv_b_bias[j], ab_ln_g[j], ab_ln_b[j],
                          ab_w_out[j].astype(BF16), min(256, seq))
        else:
            lambda_init = 0.8 - 0.6 * math.exp(-0.3 * i)
            w, wt = _attn_in_weights(cd_w_in[j])
            z, zt = _norm_proj(x, norm_g[i, 0], sh1, sc1, w, tm, wt)
            mask = _select(z, zt, topk)
            out_c = _cattn(z, zt, mask, rel_bias, ta)
            out_d = _dattn(z, zt, rel_bias, diff_lam[j], diff_norm_g[j], lambda_init, ta)
            x = _out_proj(out_c, out_d, x, g1, cd_w_out[j].astype(BF16), tm)
        x = _hier_moe(x, norm_g[i, 1], sh2, sc2, g2, moe_wr_g[i], moe_br_g[i], moe_wr_e[i], moe_br_e[i],
                      moe_w_gate, moe_w_up, moe_w_down, i, final_norm_g, final_norm=(i == depth - 1))
    return x
```

```python
import functools
import math

import numpy as np
import jax
import jax.numpy as jnp
from jax import lax
from jax.experimental import pallas as pl
from jax.experimental.pallas import tpu as pltpu

F32 = jnp.float32
BF16 = jnp.bfloat16
I32 = jnp.int32
HIGHEST = lax.Precision.HIGHEST

EPS = 1e-6
A_WIDTH = 512
A_CONV = 3
B_WIDTH = 512
B_CONV = 31
C_HEADS = 8
C_HEAD_DIM = 64
IDX_HEADS = 8
IDX_DIM = 32
TOPK_MAX = 256
DIFF_HEADS = 4
DIFF_HEAD_DIM = 64
NUM_BUCKETS = 32
MAX_DISTANCE = 128
N_GROUPS = 4
EXPERTS_PER_GROUP = 8
N_EXPERTS = N_GROUPS * EXPERTS_PER_GROUP
C_WIDTH = C_HEADS * C_HEAD_DIM
DIFF_W = DIFF_HEADS * 2 * DIFF_HEAD_DIM
LANES = 128
BF16_ROWS = 16
INT_MIN = -(2 ** 31)
LOG2E = math.log2(math.e)
VMEM_LIMIT = 56 * 1024 * 1024

Z_KC, Z_KD = 0, 1
Z_KIDX_OFF = C_WIDTH + DIFF_W
Z_COLS = Z_KIDX_OFF + LANES
ZT_VC, ZT_VD, ZT_QC, ZT_QD = 0, 1, 2, 3
ZT_QIDX_OFF = 2 * C_WIDTH + 2 * DIFF_W
ZT_QIDX_ROWS = IDX_HEADS * IDX_DIM
ZT_W_OFF = ZT_QIDX_OFF + ZT_QIDX_ROWS
ZT_ROWS = ZT_W_OFF + BF16_ROWS


def _bucket_starts():
    n = np.arange(0, 2 * MAX_DISTANCE)
    me = NUM_BUCKETS // 2
    lr = np.log(np.maximum(n, 1) / me) / math.log(MAX_DISTANCE / me)
    large = me + (lr * (NUM_BUCKETS - me)).astype(np.int64)
    b = np.where(n < me, n, np.minimum(large, NUM_BUCKETS - 1))
    return [int(n[b >= k].min()) for k in range(NUM_BUCKETS)]


BUCKET_START = _bucket_starts()


def _cparams(sem):
    return pltpu.CompilerParams(dimension_semantics=sem, vmem_limit_bytes=VMEM_LIMIT)


def _rms(x):
    return x * lax.rsqrt(jnp.mean(x * x, axis=-1, keepdims=True) + EPS)


def _sigmoid(x):
    return 1.0 / (1.0 + jnp.exp(-x))


def _dot_nt(a, b):
    return lax.dot_general(a, b, (((1,), (1,)), ((), ())), preferred_element_type=F32)


def _ada_body(c_ref, w_ref, b_ref, o_ref):
    c = c_ref[...]
    cond = c * _sigmoid(c)
    o_ref[0] = jnp.dot(cond, w_ref[0], precision=HIGHEST, preferred_element_type=F32) + b_ref[0]


def _ada_mod(c, ada_w, ada_b):
    depth, d, n6 = ada_w.shape
    bsz = c.shape[0]
    rows = 8
    c_pad = jnp.zeros((rows, d), F32).at[:bsz].set(c)
    tn = 1536
    out = pl.pallas_call(
        _ada_body,
        grid=(depth, n6 // tn),
        in_specs=[pl.BlockSpec((rows, d), lambda i, j: (0, 0)),
                  pl.BlockSpec((1, d, tn), lambda i, j: (i, 0, j)),
                  pl.BlockSpec((1, 1, tn), lambda i, j: (i, 0, j))],
        out_specs=pl.BlockSpec((1, rows, tn), lambda i, j: (i, 0, j)),
        out_shape=jax.ShapeDtypeStruct((depth, rows, n6), F32),
        compiler_params=_cparams(("arbitrary", "arbitrary")),
        name="ada_mod",
    )(c_pad, ada_w, ada_b.reshape(depth, 1, n6))
    return out[:, :bsz]


def _norm_proj_body(x_ref, g_ref, sh_ref, sc_ref, w_ref, *rest):
    y = _rms(x_ref[0]) * g_ref[...]
    h = (y * (1.0 + sc_ref[0]) + sh_ref[0]).astype(BF16)
    if len(rest) == 1:
        (o_ref,) = rest
    else:
        wt_ref, o_ref, ot_ref = rest
        ot_ref[0] = _dot_nt(wt_ref[...], h).astype(ot_ref.dtype)
    o_ref[0] = jnp.dot(h, w_ref[...], preferred_element_type=F32).astype(o_ref.dtype)


def _norm_proj(x, g, sh, sc, w_bf16, tm, wt_bf16=None):
    bsz, seq, d = x.shape
    n = w_bf16.shape[1]
    in_specs = [pl.BlockSpec((1, tm, d), lambda b, i: (b, i, 0)),
                pl.BlockSpec((1, d), lambda b, i: (0, 0)),
                pl.BlockSpec((1, 1, d), lambda b, i: (b, 0, 0)),
                pl.BlockSpec((1, 1, d), lambda b, i: (b, 0, 0)),
                pl.BlockSpec((d, n), lambda b, i: (0, 0))]
    out_specs = pl.BlockSpec((1, tm, n), lambda b, i: (b, i, 0))
    out_shape = jax.ShapeDtypeStruct((bsz, seq, n), BF16)
    args = [x, g.reshape(1, d), sh.reshape(bsz, 1, d), sc.reshape(bsz, 1, d), w_bf16]
    if wt_bf16 is not None:
        nt = wt_bf16.shape[0]
        in_specs.append(pl.BlockSpec((nt, d), lambda b, i: (0, 0)))
        out_specs = [out_specs, pl.BlockSpec((1, nt, tm), lambda b, i: (b, 0, i))]
        out_shape = [out_shape, jax.ShapeDtypeStruct((bsz, nt, seq), BF16)]
        args.append(wt_bf16)
    return pl.pallas_call(
        _norm_proj_body,
        grid=(bsz, seq // tm),
        in_specs=in_specs,
        out_specs=out_specs,
        out_shape=out_shape,
        compiler_params=_cparams(("arbitrary", "arbitrary")),
        name="norm_proj",
    )(*args)


CONV_HALO = 32
CONV_ROWS = 64


def _conv_body(z_ref, x_ref, g1_ref, ca_ref, cb_ref, cbb_ref, lng_ref, lnb_ref, wo_ref, o_ref,
               ua_scr, ub_scr, y_scr, *, tl):
    l = pl.program_id(1)

    @pl.when(l == 0)
    def _():
        ua_scr[0:CONV_HALO, :] = jnp.zeros((CONV_HALO, A_WIDTH), F32)
        ub_scr[0:CONV_HALO, :] = jnp.zeros((CONV_HALO, B_WIDTH), F32)

    a = A_WIDTH
    gate_c = z_ref[0, :, a:2 * a].astype(F32)
    x_a = z_ref[0, :, 2 * a:3 * a].astype(F32)
    ua_scr[CONV_HALO:CONV_HALO + tl, :] = gate_c * x_a
    val_b = z_ref[0, :, 3 * a:3 * a + B_WIDTH].astype(F32)
    glu = z_ref[0, :, 3 * a + B_WIDTH:3 * a + 2 * B_WIDTH].astype(F32)
    ub_scr[CONV_HALO:CONV_HALO + tl, :] = val_b * _sigmoid(glu)

    for r in range(0, tl, CONV_ROWS):
        acc_a = None
        for k in range(A_CONV):
            tap = ua_scr[CONV_HALO + r - (A_CONV - 1) + k:CONV_HALO + r - (A_CONV - 1) + k + CONV_ROWS, :]
            term = tap * ca_ref[k:k + 1, :]
            acc_a = term if acc_a is None else acc_a + term
        gate_b = z_ref[0, r:r + CONV_ROWS, 0:a].astype(F32)
        y_scr[r:r + CONV_ROWS, 0:a] = (gate_b * acc_a).astype(BF16)

        win = CONV_HALO + r - 8
        acc_b = None
        for b in range(8):
            phase = None
            for back in range(b, B_CONV, 8):
                tap = ub_scr[win - (back - b):win - (back - b) + CONV_ROWS + 8, :]
                term = tap * cb_ref[B_CONV - 1 - back:B_CONV - back, :]
                phase = term if phase is None else phase + term
            piece = phase[8 - b:8 - b + CONV_ROWS]
            acc_b = piece if acc_b is None else acc_b + piece
        u = acc_b + cbb_ref[...]
        mu = jnp.mean(u, axis=-1, keepdims=True)
        uc = u - mu
        var = jnp.mean(uc * uc, axis=-1, keepdims=True)
        v = uc * lax.rsqrt(var + EPS) * lng_ref[...] + lnb_ref[...]
        y_scr[r:r + CONV_ROWS, a:a + B_WIDTH] = (v * _sigmoid(v)).astype(BF16)

    ua_scr[0:CONV_HALO, :] = ua_scr[tl:tl + CONV_HALO, :]
    ub_scr[0:CONV_HALO, :] = ub_scr[tl:tl + CONV_HALO, :]
    y = jnp.dot(y_scr[...], wo_ref[...], preferred_element_type=F32)
    o_ref[0] = x_ref[0] + g1_ref[0] * y


def _conv_mix(z, x, g1, conv_a, conv_b, conv_b_bias, ln_g, ln_b, w_out_bf16, tl):
    bsz, seq, d = x.shape
    nz = z.shape[-1]
    wide = A_WIDTH + B_WIDTH
    full = lambda shape: pl.BlockSpec(shape, lambda b, l: (0,) * len(shape))
    return pl.pallas_call(
        functools.partial(_conv_body, tl=tl),
        grid=(bsz, seq // tl),
        in_specs=[pl.BlockSpec((1, tl, nz), lambda b, l: (b, l, 0)),
                  pl.BlockSpec((1, tl, d), lambda b, l: (b, l, 0)),
                  pl.BlockSpec((1, 1, d), lambda b, l: (b, 0, 0)),
                  full((A_CONV, A_WIDTH)), full((B_CONV, B_WIDTH)), full((1, B_WIDTH)),
                  full((1, B_WIDTH)), full((1, B_WIDTH)), full((wide, d))],
        out_specs=pl.BlockSpec((1, tl, d), lambda b, l: (b, l, 0)),
        out_shape=jax.ShapeDtypeStruct((bsz, seq, d), F32),
        scratch_shapes=[pltpu.VMEM((CONV_HALO + tl, A_WIDTH), F32),
                        pltpu.VMEM((CONV_HALO + tl, B_WIDTH), F32),
                        pltpu.VMEM((tl, wide), BF16)],
        compiler_params=_cparams(("arbitrary", "arbitrary")),
        name="conv_mix",
    )(z, x, g1.reshape(bsz, 1, d), conv_a, conv_b, conv_b_bias.reshape(1, -1), ln_g.reshape(1, -1),
      ln_b.reshape(1, -1), w_out_bf16)


MOE_ROWS = 512
ROUTER_ROWS = 512
META_GATE0, META_GATE1, META_POS0, META_POS1 = range(4)
SEG_LEN, SEG_BASE, SEG_OFF = range(3)
SEG_ALIGN = 8
SEG_SIZES = tuple(2 ** b for b in range(10, 2, -1))
GROUP_LANE0 = N_EXPERTS


def _router_body(x_ref, g_ref, sh_ref, sc_ref, wr_ref, br_ref, tri_ref, upper_ref, h_ref, meta_ref, post_ref, seg_ref,
                 cnt_ref, base_scr):
    @pl.when(pl.program_id(0) == 0)
    def _():
        base_scr[...] = jnp.zeros_like(base_scr)

    h = _rms(x_ref[...]) * g_ref[...]
    h = h * (1.0 + sc_ref[0]) + sh_ref[0]
    h_ref[...] = h.astype(h_ref.dtype)
    h_hi = h.astype(BF16)
    h_lo = (h - h_hi.astype(F32)).astype(BF16)
    logits = (jnp.dot(h_hi, wr_ref[0], preferred_element_type=F32) + jnp.dot(h_hi, wr_ref[1], preferred_element_type=F32)
              + jnp.dot(h_lo, wr_ref[0], preferred_element_type=F32) + br_ref[...])
    tr = logits.shape[0]
    lane = lax.broadcasted_iota(I32, (tr, LANES), 1)
    lane_f = lane.astype(F32)
    neg = jnp.float32(-jnp.inf)
    big = jnp.float32(1e9)

    is_group = (lane >= GROUP_LANE0) & (lane < GROUP_LANE0 + N_GROUPS)
    glog = jnp.where(is_group, logits, neg)
    gmax = jnp.max(glog, axis=1, keepdims=True)
    p_top = 1.0 / jnp.sum(jnp.exp(glog - gmax), axis=1, keepdims=True)
    g_sel = jnp.min(jnp.where(glog == gmax, lane_f, big), axis=1, keepdims=True) - GROUP_LANE0
    lo = g_sel * EXPERTS_PER_GROUP
    in_group = (lane_f >= lo) & (lane_f < lo + EXPERTS_PER_GROUP)
    f1 = jnp.where(in_group, logits, neg)
    v1 = jnp.max(f1, axis=1, keepdims=True)
    i1 = jnp.min(jnp.where(f1 == v1, lane_f, big), axis=1, keepdims=True)
    f2 = jnp.where(lane_f == i1, neg, f1)
    v2 = jnp.max(f2, axis=1, keepdims=True)
    i2 = jnp.min(jnp.where(f2 == v2, lane_f, big), axis=1, keepdims=True)
    a = jnp.exp(v2 - v1)
    w1 = 1.0 / (1.0 + a)
    gate0 = p_top * w1
    gate1 = p_top * (a * w1)

    oh0 = lane_f == i1
    oh1 = lane_f == i2
    ind0 = jnp.where(oh0, 1.0, 0.0)
    ind1 = jnp.where(oh1, 1.0, 0.0)
    pre0 = jnp.dot(tri_ref[...], ind0.astype(BF16), preferred_element_type=F32)
    pre1 = jnp.dot(tri_ref[...], ind1.astype(BF16), preferred_element_type=F32)
    tot0 = jnp.sum(ind0, axis=0, keepdims=True)
    tot1 = jnp.sum(ind1, axis=0, keepdims=True)
    seg_len = jnp.floor((tot0 + tot1 + (SEG_ALIGN - 1)) * (1.0 / SEG_ALIGN)) * SEG_ALIGN
    seg_off = jnp.dot(jnp.broadcast_to(seg_len, (8, LANES)), upper_ref[...], precision=HIGHEST,
                      preferred_element_type=F32)[0:1]
    pos0 = jnp.sum(jnp.where(oh0, seg_off + pre0, 0.0), axis=1, keepdims=True)
    pos1 = jnp.sum(jnp.where(oh1, seg_off + tot0 + pre1, 0.0), axis=1, keepdims=True)
    base = base_scr[...]
    new_base = base + seg_len
    base_scr[...] = new_base
    cnt_ref[...] = new_base

    meta = jnp.zeros((tr, LANES), F32)
    for col, val in ((META_GATE0, gate0), (META_GATE1, gate1), (META_POS0, pos0), (META_POS1, pos1)):
        meta = jnp.where(lane == col, val, meta)
    meta_ref[...] = meta
    post_ref[0] = meta.T[0:8]
    srow = lax.broadcasted_iota(I32, (8, LANES), 0)
    seg = jnp.where(srow == SEG_LEN, seg_len, jnp.where(srow == SEG_BASE, base, jnp.where(srow == SEG_OFF, seg_off, 0.0)))
    seg_ref[0] = seg.astype(I32)


def _moe_router(x2, g, sh, sc, wr_g, br_g, wr_e, br_e, seq):
    n_tok, d = x2.shape
    bsz = n_tok // seq
    tr = min(ROUTER_ROWS, seq)
    steps_per_batch = seq // tr
    wr = jnp.zeros((d, LANES), F32).at[:, :N_EXPERTS].set(wr_e).at[:, GROUP_LANE0:GROUP_LANE0 + N_GROUPS].set(wr_g)
    wr_hi = wr.astype(BF16)
    wr = jnp.stack([wr_hi, (wr - wr_hi.astype(F32)).astype(BF16)])
    br = jnp.zeros((1, LANES), F32).at[0, :N_EXPERTS].set(br_e).at[0, GROUP_LANE0:GROUP_LANE0 + N_GROUPS].set(br_g)
    tri = jnp.tril(jnp.ones((tr, tr), BF16), -1)
    upper = jnp.triu(jnp.ones((LANES, LANES), F32), 1)
    nb = n_tok // tr
    full = lambda shape: pl.BlockSpec(shape, lambda i: (0,) * len(shape))
    return pl.pallas_call(
        _router_body,
        grid=(nb,),
        in_specs=[pl.BlockSpec((tr, d), lambda i: (i, 0)),
                  full((1, d)),
                  pl.BlockSpec((1, 1, d), lambda i: (i // steps_per_batch, 0, 0)),
                  pl.BlockSpec((1, 1, d), lambda i: (i // steps_per_batch, 0, 0)),
                  full((2, d, LANES)), full((1, LANES)), full((tr, tr)), full((LANES, LANES))],
        out_specs=[pl.BlockSpec((tr, d), lambda i: (i, 0)),
                   pl.BlockSpec((tr, LANES), lambda i: (i, 0)),
                   pl.BlockSpec((1, 8, tr), lambda i: (i, 0, 0)),
                   pl.BlockSpec((1, 8, LANES), lambda i: (i, 0, 0)),
                   full((1, LANES))],
        out_shape=[jax.ShapeDtypeStruct((n_tok, d), BF16),
                   jax.ShapeDtypeStruct((n_tok, LANES), F32),
                   jax.ShapeDtypeStruct((nb, 8, tr), F32),
                   jax.ShapeDtypeStruct((nb, 8, LANES), I32),
                   jax.ShapeDtypeStruct((1, LANES), F32)],
        scratch_shapes=[pltpu.VMEM((1, LANES), F32)],
        compiler_params=_cparams(("arbitrary",)),
        name="moe_router",
    )(x2, g.reshape(1, d), sh.reshape(bsz, 1, d), sc.reshape(bsz, 1, d), wr, br, tri, upper)


def _segment_copies(seg_ref, starts_ref, make_copy, action):
    def per_expert(e, carry):
        length = seg_ref[0, SEG_LEN, e]
        local = seg_ref[0, SEG_OFF, e]
        glob = starts_ref[e] + seg_ref[0, SEG_BASE, e]
        for size in SEG_SIZES:
            hit = (length & size) != 0

            @pl.when(hit)
            def _(local=local, glob=glob, size=size):
                cp = make_copy(pl.multiple_of(local, SEG_ALIGN), pl.multiple_of(glob, SEG_ALIGN), size)
                cp.start() if action == "start" else cp.wait()

            step = jnp.where(hit, size, 0)
            local = local + step
            glob = glob + step
        return carry

    lax.fori_loop(0, N_EXPERTS, per_expert, 0)


def _local_rows(tr):
    return 2 * tr + N_EXPERTS * SEG_ALIGN


def _row_words(d):
    return d // 2 + LANES


def _dispatch_body(starts_ref, seg_ref, prev_seg_ref, h_ref, post_ref, xr_ref, xs_scr, zero_scr, sems, *, tr):
    i = pl.program_id(0)
    slot = i % 2
    lb = xs_scr.shape[1]
    sem = sems.at[0]

    def make_copy_of(s):
        def make_copy(local, glob, size):
            return pltpu.make_async_copy(xs_scr.at[s, pl.ds(local, size)], xr_ref.at[pl.ds(glob, size)], sems.at[s])
        return make_copy

    rows = lax.broadcasted_iota(I32, (lb, tr), 0)
    first = rows == post_ref[0, META_POS0:META_POS0 + 1, :].astype(I32)
    second = rows == post_ref[0, META_POS1:META_POS1 + 1, :].astype(I32)
    x = jnp.dot(jnp.where(first | second, 1.0, 0.0).astype(BF16), h_ref[...], preferred_element_type=F32)
    bits = lax.bitcast_convert_type(x, I32)
    half = x.shape[1] // 2
    xs_scr[slot, :, 0:half] = lax.shift_right_logical(bits[:, 0:half], 16) | (bits[:, half:] & jnp.int32(-65536))
    gate = jnp.sum(jnp.where(first, post_ref[0, META_GATE0:META_GATE0 + 1, :], 0.0)
                   + jnp.where(second, post_ref[0, META_GATE1:META_GATE1 + 1, :], 0.0), axis=1, keepdims=True)
    xs_scr[slot, :, half:half + LANES] = jnp.broadcast_to(lax.bitcast_convert_type(gate, I32), (lb, LANES))
    _segment_copies(seg_ref, starts_ref, make_copy_of(slot), "start")

    @pl.when(i > 0)
    def _():
        _segment_copies(prev_seg_ref, starts_ref, make_copy_of(1 - slot), "wait")

    @pl.when(i == pl.num_programs(0) - 1)
    def _():
        _segment_copies(seg_ref, starts_ref, make_copy_of(slot), "wait")
        zero_scr[...] = jnp.zeros_like(zero_scr)
        total = starts_ref[N_EXPERTS]
        tail = (-total) & (MOE_ROWS - 1)
        for action in ("start", "wait"):
            row = total
            for size in SEG_SIZES:
                if size >= MOE_ROWS:
                    continue
                hit = (tail & size) != 0

                @pl.when(hit)
                def _(row=row, size=size, action=action):
                    cp = pltpu.make_async_copy(zero_scr.at[pl.ds(0, size)],
                                               xr_ref.at[pl.ds(pl.multiple_of(row, SEG_ALIGN), size)], sem)
                    cp.start() if action == "start" else cp.wait()

                row = row + jnp.where(hit, size, 0)

        def free_block(b):
            return pltpu.make_async_copy(
                zero_scr, xr_ref.at[pl.ds(pl.multiple_of(b * MOE_ROWS, MOE_ROWS), MOE_ROWS)], sem)

        first_free = (total + MOE_ROWS - 1) // MOE_ROWS
        n_blocks = xr_ref.shape[0] // MOE_ROWS
        lax.fori_loop(first_free, n_blocks, lambda b, c: (free_block(b).start(), c)[1], 0)
        lax.fori_loop(first_free, n_blocks, lambda b, c: (free_block(b).wait(), c)[1], 0)


def _moe_dispatch(h2, post, seg, starts, n_rows, tr):
    n_tok, d = h2.shape
    grid_spec = pltpu.PrefetchScalarGridSpec(
        num_scalar_prefetch=1,
        grid=(n_tok // tr,),
        in_specs=[pl.BlockSpec((1, 8, LANES), lambda i, s: (i, 0, 0), memory_space=pltpu.SMEM),
                  pl.BlockSpec((1, 8, LANES), lambda i, s: (jnp.maximum(i - 1, 0), 0, 0), memory_space=pltpu.SMEM),
                  pl.BlockSpec((tr, d), lambda i, s: (i, 0)),
                  pl.BlockSpec((1, 8, tr), lambda i, s: (i, 0, 0))],
        out_specs=pl.BlockSpec(memory_space=pl.ANY),
        scratch_shapes=[pltpu.VMEM((2, _local_rows(tr), _row_words(d)), I32), pltpu.VMEM((MOE_ROWS, _row_words(d)), I32),
                        pltpu.SemaphoreType.DMA((2,))],
    )
    return pl.pallas_call(
        functools.partial(_dispatch_body, tr=tr),
        grid_spec=grid_spec,
        out_shape=jax.ShapeDtypeStruct((n_rows, _row_words(d)), I32),
        compiler_params=_cparams(("arbitrary",)),
        name="moe_dispatch",
    )(starts, seg, seg, h2, post)


def _expert_body(pb_ref, pe_ref, plo_ref, phi_ref, x_ref, wg_ref, wu_ref, wd_ref, o_ref, wg_s, wu_s, wd_s):
    p = pl.program_id(0)
    prev = jnp.maximum(p - 1, 0)
    new_expert = (p == 0) | (pe_ref[p] != pe_ref[prev])
    first = (p == 0) | (pb_ref[p] != pb_ref[prev])

    @pl.when(new_expert)
    def _():
        wg_s[...] = wg_ref[0, 0].astype(BF16)
        wu_s[...] = wu_ref[0, 0].astype(BF16)
        wd_s[...] = wd_ref[0, 0].astype(BF16)

    def rows_of_expert():
        half = wg_s.shape[0] // 2
        words = x_ref[:, 0:half]
        x_lo = lax.bitcast_convert_type(words << 16, F32).astype(BF16)
        x_hi = lax.bitcast_convert_type(words & jnp.int32(-65536), F32).astype(BF16)
        gate = lax.bitcast_convert_type(x_ref[:, half:half + 1], F32)
        gt = (jnp.dot(x_lo, wg_s[0:half], preferred_element_type=F32)
              + jnp.dot(x_hi, wg_s[half:], preferred_element_type=F32))
        up = (jnp.dot(x_lo, wu_s[0:half], preferred_element_type=F32)
              + jnp.dot(x_hi, wu_s[half:], preferred_element_type=F32))
        act = (gt * _sigmoid(gt)) * up
        y = jnp.dot(act.astype(BF16), wd_s[...], preferred_element_type=F32) * gate
        rows = lax.broadcasted_iota(I32, (y.shape[0], 1), 0)
        return jnp.where((rows >= plo_ref[p]) & (rows < phi_ref[p]), y, 0.0)

    nonempty = phi_ref[p] > plo_ref[p]

    @pl.when(first & nonempty)
    def _():
        o_ref[...] = rows_of_expert()

    @pl.when(first & jnp.logical_not(nonempty))
    def _():
        o_ref[...] = jnp.zeros_like(o_ref)

    @pl.when(jnp.logical_not(first) & nonempty)
    def _():
        o_ref[...] += rows_of_expert()


def _moe_experts(x_rows, pairs, w_gate, w_up, w_down, layer):
    n_rows = x_rows.shape[0]
    d, de = w_gate.shape[-2:]
    n_pairs = pairs[0].shape[0]
    grid_spec = pltpu.PrefetchScalarGridSpec(
        num_scalar_prefetch=4,
        grid=(n_pairs,),
        in_specs=[pl.BlockSpec((MOE_ROWS, _row_words(d)), lambda p, pb, pe, lo, hi: (pb[p], 0)),
                  pl.BlockSpec((1, 1, d, de), lambda p, pb, pe, lo, hi: (layer, pe[p], 0, 0)),
                  pl.BlockSpec((1, 1, d, de), lambda p, pb, pe, lo, hi: (layer, pe[p], 0, 0)),
                  pl.BlockSpec((1, 1, de, d), lambda p, pb, pe, lo, hi: (layer, pe[p], 0, 0))],
        out_specs=pl.BlockSpec((MOE_ROWS, d), lambda p, pb, pe, lo, hi: (pb[p], 0)),
        scratch_shapes=[pltpu.VMEM((d, de), BF16), pltpu.VMEM((d, de), BF16), pltpu.VMEM((de, d), BF16)],
    )
    return pl.pallas_call(
        _expert_body,
        grid_spec=grid_spec,
        out_shape=jax.ShapeDtypeStruct((n_rows, d), F32),
        compiler_params=_cparams(("arbitrary",)),
        name="moe_experts",
    )(*pairs, x_rows, w_gate, w_up, w_down)


def _combine_body(starts_ref, seg_ref, next_seg_ref, y_ref, x_ref, meta_ref, g2_ref, fg_ref, o_ref, ys_scr, sems, *,
                  final_norm):
    i = pl.program_id(0)
    slot = i % 2

    def make_copy_of(s):
        def make_copy(local, glob, size):
            return pltpu.make_async_copy(y_ref.at[pl.ds(glob, size)], ys_scr.at[s, pl.ds(local, size)], sems.at[s])
        return make_copy

    @pl.when(i == 0)
    def _():
        ys_scr[...] = jnp.zeros_like(ys_scr)
        _segment_copies(seg_ref, starts_ref, make_copy_of(slot), "start")

    @pl.when(i + 1 < pl.num_programs(0))
    def _():
        _segment_copies(next_seg_ref, starts_ref, make_copy_of(1 - slot), "start")

    _segment_copies(seg_ref, starts_ref, make_copy_of(slot), "wait")

    meta = meta_ref[...]
    tr = meta.shape[0]
    y = ys_scr[slot].astype(BF16)
    cols = lax.broadcasted_iota(I32, (tr, ys_scr.shape[1]), 1)

    pick = ((cols == meta[:, META_POS0:META_POS0 + 1].astype(I32))
            | (cols == meta[:, META_POS1:META_POS1 + 1].astype(I32)))
    moe = jnp.dot(jnp.where(pick, 1.0, 0.0).astype(BF16), y, preferred_element_type=F32)
    xn = x_ref[...] + g2_ref[0] * moe
    if final_norm:
        xn = _rms(xn) * fg_ref[...]
    o_ref[...] = xn


def _moe_combine(y_rows, seg, starts, x2, meta, g2, final_g, seq, tr, final_norm):
    n_tok, d = x2.shape
    bsz = n_tok // seq
    steps_per_batch = seq // tr
    nb = n_tok // tr
    grid_spec = pltpu.PrefetchScalarGridSpec(
        num_scalar_prefetch=1,
        grid=(nb,),
        in_specs=[pl.BlockSpec((1, 8, LANES), lambda i, s: (i, 0, 0), memory_space=pltpu.SMEM),
                  pl.BlockSpec((1, 8, LANES), lambda i, s: (jnp.minimum(i + 1, nb - 1), 0, 0), memory_space=pltpu.SMEM),
                  pl.BlockSpec(memory_space=pl.ANY),
                  pl.BlockSpec((tr, d), lambda i, s: (i, 0)),
                  pl.BlockSpec((tr, LANES), lambda i, s: (i, 0)),
                  pl.BlockSpec((1, 1, d), lambda i, s: (i // steps_per_batch, 0, 0)),
                  pl.BlockSpec((1, d), lambda i, s: (0, 0))],
        out_specs=pl.BlockSpec((tr, d), lambda i, s: (i, 0)),
        scratch_shapes=[pltpu.VMEM((2, _local_rows(tr), d), F32), pltpu.SemaphoreType.DMA((2,))],
    )
    return pl.pallas_call(
        functools.partial(_combine_body, final_norm=final_norm),
        grid_spec=grid_spec,
        out_shape=jax.ShapeDtypeStruct((n_tok, d), F32),
        compiler_params=_cparams(("arbitrary",)),
        name="moe_combine",
    )(starts, seg, seg, y_rows, x2, meta, g2.reshape(bsz, 1, d), final_g.reshape(1, d))


def _expert_pairs(counts, n_rows):
    n_blocks = n_rows // MOE_ROWS
    n_pairs = n_blocks + N_EXPERTS
    ends = jnp.cumsum(counts)
    starts = ends - counts
    first_blk = starts // MOE_ROWS
    last_blk = (ends - 1) // MOE_ROWS
    npairs = jnp.where(counts > 0, last_blk - first_blk + 1, 0)
    pend = jnp.cumsum(npairs)
    poff = pend - npairs
    total = pend[-1]
    used_blocks = (ends[-1] + MOE_ROWS - 1) // MOE_ROWS
    p = jnp.arange(n_pairs, dtype=I32)
    p_eff = jnp.minimum(p, total - 1)
    e = jnp.minimum(jnp.sum(pend[None, :] <= p_eff[:, None], axis=1), N_EXPERTS - 1).astype(I32)
    table = jnp.stack([first_blk, poff, starts, ends], axis=1).astype(F32)
    onehot = (e[:, None] == jnp.arange(N_EXPERTS, dtype=I32)[None, :]).astype(F32)
    first_e, poff_e, start_e, end_e = jnp.dot(onehot, table, precision=HIGHEST).astype(I32).T
    valid = p < total
    blk = jnp.where(valid, first_e + p_eff - poff_e, jnp.minimum(used_blocks + p - total, n_blocks - 1)).astype(I32)
    lo = jnp.where(valid, jnp.clip(start_e - blk * MOE_ROWS, 0, MOE_ROWS), 0).astype(I32)
    hi = jnp.where(valid, jnp.clip(end_e - blk * MOE_ROWS, 0, MOE_ROWS), 0).astype(I32)
    return blk, e, lo, hi


def _hier_moe(x, g, sh, sc, g2, wr_g, br_g, wr_e, br_e, w_gate, w_up, w_down, layer, final_g, final_norm):
    bsz, seq, d = x.shape
    n_tok = bsz * seq
    tr = min(ROUTER_ROWS, seq)
    assert 2 * tr <= SEG_SIZES[0]
    x2 = x.reshape(n_tok, d)
    h2, meta, post, seg, cnt = _moe_router(x2, g, sh, sc, wr_g, br_g, wr_e, br_e, seq)
    counts = cnt[0, :N_EXPERTS].astype(I32)
    ends = jnp.cumsum(counts)
    starts = jnp.concatenate([ends - counts, ends[-1:]])
    n_rows = -(-(2 * n_tok + (n_tok // tr) * N_EXPERTS * (SEG_ALIGN - 1)) // MOE_ROWS) * MOE_ROWS
    x_rows = _moe_dispatch(h2, post, seg, starts, n_rows, tr)
    y_rows = _moe_experts(x_rows, _expert_pairs(counts, n_rows), w_gate, w_up, w_down, layer)
    out = _moe_combine(y_rows, seg, starts, x2, meta, g2, final_g, seq, tr, final_norm)
    return out.reshape(bsz, seq, d)


SEL_COLS = 512
SEL_CHUNK = 512
BIT_GROUP = 256
SEL_SWEEP = 64
ATT_TILE = 1024


def _idx_score_t(k, qt, wt):
    acc = None
    for h in range(IDX_HEADS):
        rel = jnp.dot(k, qt[h * IDX_DIM:(h + 1) * IDX_DIM], preferred_element_type=F32)
        term = jnp.maximum(rel, 0.0) * wt[h:h + 1, :]
        acc = term if acc is None else acc + term
    return acc


def _bit_transpose32(a):
    a = list(a)
    m, j = 0x0000FFFF, 16
    while j:
        k = 0
        while k < 32:
            t = (a[k] ^ lax.shift_right_logical(a[k + j], jnp.int32(j))) & jnp.int32(m)
            a[k] = a[k] ^ t
            a[k + j] = a[k + j] ^ (t << j)
            k = (k + j + 1) & ~j
        j >>= 1
        m = (m ^ (m << j)) & 0xFFFFFFFF
    return a


def _select_body(qt_ref, zk_ref, wt_ref, o_ref, planes_scr, eq_scr, gt_scr, *, topk):
    i = pl.program_id(1)
    tq = qt_ref.shape[2]
    qt = qt_ref[0]
    wt = wt_ref[0].astype(F32)
    n_chunks = ((i + 1) * tq + SEL_CHUNK - 1) // SEL_CHUNK
    krow0 = lax.broadcasted_iota(I32, (SEL_CHUNK, tq), 0)
    qcol = i * tq + lax.broadcasted_iota(I32, (SEL_CHUNK, tq), 1)
    int_min = jnp.int32(INT_MIN)

    groups = SEL_CHUNK // BIT_GROUP
    words = SEL_CHUNK // 32
    sweep_rows = min(SEL_SWEEP, eq_scr.shape[0])

    def fill(masked, c, carry):
        off = pl.multiple_of(c * SEL_CHUNK, SEL_CHUNK)
        k = zk_ref[0, pl.ds(off, SEL_CHUNK), :][:, 0:IDX_DIM]
        bits = lax.bitcast_convert_type(_idx_score_t(k, qt, wt), I32)
        key = bits ^ ((bits >> 31) | int_min)
        if masked:
            key = jnp.where(krow0 + off <= qcol, key, 0)
        for g in range(groups):
            for lt in range(0, tq, LANES):
                ku = key[g * BIT_GROUP:(g + 1) * BIT_GROUP, lt:lt + LANES]
                planes = _bit_transpose32([ku[8 * r:8 * r + 8] for r in range(32)])
                wrow = pl.multiple_of(c * words + g * 8, 8)
                for b in range(32):
                    planes_scr[b, pl.ds(wrow, 8), lt:lt + LANES] = planes[b]
        return carry

    n_plain = (i * tq + 1) // SEL_CHUNK
    lax.fori_loop(0, n_plain, functools.partial(fill, False), 0)
    lax.fori_loop(n_plain, n_chunks, functools.partial(fill, True), 0)

    n_sweep = (n_chunks * words + sweep_rows - 1) // sweep_rows

    def pad(c, carry):
        wrow = pl.multiple_of(c * words, words)
        for b in range(32):
            planes_scr[b, pl.ds(wrow, words), :] = jnp.zeros((words, tq), I32)
        return carry

    lax.fori_loop(n_chunks, n_sweep * (sweep_rows // words), pad, 0)

    def sweep(upd, cnt_plane):
        def body(sb, acc):
            r0 = pl.multiple_of(sb * sweep_rows, sweep_rows)
            eq = eq_scr[pl.ds(r0, sweep_rows), :]
            gt = gt_scr[pl.ds(r0, sweep_rows), :]
            if upd is not None:
                plane, accept = upd
                hit = eq & planes_scr[plane, pl.ds(r0, sweep_rows), :]
                gt = jnp.where(accept, gt, gt | hit)
                eq = jnp.where(accept, hit, eq ^ hit)
                eq_scr[pl.ds(r0, sweep_rows), :] = eq
                gt_scr[pl.ds(r0, sweep_rows), :] = gt
            if cnt_plane is None:
                return acc
            ones = lax.population_count(gt | (eq & planes_scr[cnt_plane, pl.ds(r0, sweep_rows), :]))
            return acc + jnp.sum(ones.reshape(sweep_rows // 8, 8, tq), axis=0)
        acc = lax.fori_loop(0, n_sweep, body, jnp.zeros((8, tq), I32))
        return jnp.sum(acc.astype(F32), axis=0, keepdims=True)

    k_f = jnp.float32(topk)
    eq_scr[...] = jnp.full(eq_scr.shape, -1, I32)
    gt_scr[...] = jnp.zeros(gt_scr.shape, I32)

    def bit_step(ib, carry):
        u, cnt = carry
        accept = cnt >= k_f
        u = jnp.where(accept, u | jnp.left_shift(jnp.int32(1), 32 - ib), u)
        return u, sweep((ib - 1, accept), ib)

    u, cnt = lax.fori_loop(1, 32, bit_step, (jnp.zeros((1, tq), I32), sweep(None, 0)))
    accept = cnt >= k_f
    u = jnp.where(accept, u | 1, u)
    sweep((31, accept), None)
    some = jnp.where(u != 0, -1, 0)

    def popcount_rows(word_of):
        def body(sb, acc):
            r0 = pl.multiple_of(sb * sweep_rows, sweep_rows)
            ones = lax.population_count(word_of(r0))
            return acc + jnp.sum(ones.reshape(sweep_rows // 8, 8, tq), axis=0)
        acc = lax.fori_loop(0, n_sweep, body, jnp.zeros((8, tq), I32))
        return jnp.sum(acc.astype(F32), axis=0, keepdims=True)

    n_gt = popcount_rows(lambda r0: gt_scr[pl.ds(r0, sweep_rows), :])
    n_eq = jnp.where(u != 0, popcount_rows(lambda r0: eq_scr[pl.ds(r0, sweep_rows), :]), 0.0)

    @pl.when(jnp.max(n_gt + n_eq) > k_f)
    def _():
        wrow = lax.broadcasted_iota(I32, (sweep_rows, tq), 0)

        def below(cap, r0):
            w = wrow + r0
            last = lax.shift_right_arithmetic(cap - (w >> 3) * BIT_GROUP - (w & 7) - 1, 3)
            clear = jnp.clip(31 - last, 0, 32)
            return jnp.where(clear >= 32, 0, jnp.left_shift(jnp.int32(-1), jnp.minimum(clear, 31)))

        need = k_f - n_gt

        def cap_step(ib, v):
            cand = v | jnp.left_shift(jnp.int32(1), 14 - ib)
            cnt = popcount_rows(lambda r0: eq_scr[pl.ds(r0, sweep_rows), :] & below(cand, r0))
            return jnp.where(cnt <= need, cand, v)

        cap = lax.fori_loop(0, 15, cap_step, jnp.zeros((1, tq), I32))
        cap = jnp.where((u != 0) & (n_gt + n_eq > k_f), cap, 2 ** 30)

        def trim(sb, carry):
            r0 = pl.multiple_of(sb * sweep_rows, sweep_rows)
            eq_scr[pl.ds(r0, sweep_rows), :] = eq_scr[pl.ds(r0, sweep_rows), :] & below(cap, r0)
            return carry

        lax.fori_loop(0, n_sweep, trim, 0)

    def emit(c, carry):
        off = pl.multiple_of(c * SEL_CHUNK, SEL_CHUNK)
        for g in range(groups):
            wrow = pl.multiple_of(c * words + g * 8, 8)
            keep = gt_scr[pl.ds(wrow, 8), :] | (eq_scr[pl.ds(wrow, 8), :] & some)
            rows = [jnp.where((lax.shift_right_logical(keep, 31 - r) & 1) != 0, 0.0, -jnp.inf) for r in range(32)]
            o_ref[0, pl.ds(pl.multiple_of(off + g * BIT_GROUP, BIT_GROUP), BIT_GROUP), :] = (
                jnp.concatenate(rows, axis=0).astype(o_ref.dtype))
        return carry

    lax.fori_loop(0, n_chunks, emit, 0)

    def blank(c, carry):
        off = pl.multiple_of(c * SEL_CHUNK, SEL_CHUNK)
        o_ref[0, pl.ds(off, SEL_CHUNK), :] = jnp.full((SEL_CHUNK, tq), -jnp.inf, o_ref.dtype)
        return carry

    lax.fori_loop(n_chunks, o_ref.shape[1] // SEL_CHUNK, blank, 0)


def _select(z, zt, topk):
    bsz, seq, _ = z.shape
    tq = min(SEL_COLS, seq)
    assert seq % SEL_CHUNK == 0 and seq % tq == 0
    return pl.pallas_call(
        functools.partial(_select_body, topk=topk),
        grid=(bsz, seq // tq),
        in_specs=[pl.BlockSpec((1, ZT_QIDX_ROWS, tq), lambda b, i: (b, ZT_QIDX_OFF // ZT_QIDX_ROWS, i)),
                  pl.BlockSpec((1, seq, LANES), lambda b, i: (b, 0, Z_KIDX_OFF // LANES)),
                  pl.BlockSpec((1, BF16_ROWS, tq), lambda b, i: (b, ZT_W_OFF // BF16_ROWS, i))],
        out_specs=pl.BlockSpec((1, seq, tq), lambda b, i: (b, 0, i)),
        out_shape=jax.ShapeDtypeStruct((bsz, seq, seq), BF16),
        scratch_shapes=[pltpu.VMEM((32, seq // 32, tq), I32),
                        pltpu.VMEM((seq // 32, tq), I32), pltpu.VMEM((seq // 32, tq), I32)],
        compiler_params=_cparams(("arbitrary", "arbitrary")),
        name="dsa_select",
    )(zt, z, zt)


BIAS_INIT_ROWS = 8
ONES_ROWS = BF16_ROWS


BAND = 128


def _init_bias_blocks(btile, bias_ref, head0, n_heads):
    ta = BAND
    col = lax.broadcasted_iota(I32, (BIAS_INIT_ROWS, ta), 1)
    row0 = lax.broadcasted_iota(I32, (BIAS_INIT_ROWS, ta), 0)

    def body(r, carry):
        off = pl.multiple_of(r * BIAS_INIT_ROWS, BIAS_INIT_ROWS)
        for kind in range(2):
            dist = col - (row0 + off) + kind * ta
            for h in range(n_heads):
                far = bias_ref[NUM_BUCKETS - 1, head0 + h]
                val = jnp.full((BIAS_INIT_ROWS, ta), (bias_ref[0, head0 + h] - far) * LOG2E, F32)
                for b in range(1, NUM_BUCKETS - 1):
                    val = jnp.where(dist >= BUCKET_START[b], (bias_ref[b, head0 + h] - far) * LOG2E, val)
                val = jnp.where(dist >= BUCKET_START[NUM_BUCKETS - 1], 0.0, val)
                btile[h, kind, pl.ds(off, BIAS_INIT_ROWS), :] = val
        return carry

    lax.fori_loop(0, ta // BIAS_INIT_ROWS, body, 0)


def _add_bias_band(s_scr, btile, h, kind, ta):
    nb = ta // BAND
    blk = lambda i: slice(i * BAND, (i + 1) * BAND)
    if kind == 0:
        for kb in range(nb):
            s_scr[blk(kb), blk(kb)] += btile[h, 0]
            if kb + 1 < nb:
                s_scr[blk(kb), blk(kb + 1)] += btile[h, 1]
    else:
        s_scr[blk(nb - 1), blk(0)] += btile[h, 1]


def _with_ones(vt):
    return jnp.concatenate([vt, jnp.ones((ONES_ROWS, vt.shape[1]), vt.dtype)], axis=0)


def _softmax_step_t(logits, v_aug, m_ref, acc_ref, idx):
    m_old = m_ref[idx]
    m_new = jnp.maximum(m_old, jnp.max(logits, axis=0, keepdims=True))
    m_safe = jnp.where(m_new == -jnp.inf, 0.0, m_new)
    p = jnp.exp2(logits - m_safe)
    alpha = jnp.exp2(m_old - m_safe)
    acc_ref[idx] = alpha * acc_ref[idx] + jnp.dot(v_aug, p.astype(BF16), preferred_element_type=F32)
    m_ref[idx] = m_new


def _reset_softmax(m_s, acc_s):
    m_s[...] = jnp.full(m_s.shape, -jnp.inf, F32)
    acc_s[...] = jnp.zeros(acc_s.shape, F32)


def _cattn_body(qi_ref, ki_ref, bias_ref, qt_ref, zk_ref, vt_ref, mask_ref, o_ref, m_s, acc_s, btile, s_scr):
    b = pl.program_id(0)
    p = pl.program_id(1)
    qi = qi_ref[p]
    ki = ki_ref[p]
    ta = zk_ref.shape[1]
    hd = C_HEAD_DIM

    @pl.when((b == 0) & (p == 0))
    def _():
        _init_bias_blocks(btile, bias_ref, 0, C_HEADS)

    @pl.when(ki == 0)
    def _():
        _reset_softmax(m_s, acc_s)

    def heads(kind):
        for h in range(C_HEADS):
            lo = h * hd
            logits = (jnp.dot(zk_ref[0, :, lo:lo + hd], qt_ref[0, lo:lo + hd, :], preferred_element_type=F32)
                      + mask_ref[0].astype(F32))
            if kind is not None:
                s_scr[...] = logits
                _add_bias_band(s_scr, btile, h, kind, ta)
                logits = s_scr[...]
            _softmax_step_t(logits, _with_ones(vt_ref[0, lo:lo + hd, :]), m_s, acc_s, h)

    for kind in (0, 1):
        pl.when(ki == qi - kind)(functools.partial(heads, kind))
    pl.when(ki < qi - 1)(functools.partial(heads, None))

    @pl.when(ki == qi)
    def _():
        outs = []
        for h in range(C_HEADS):
            a = acc_s[h]
            outs.append((a[0:hd] / a[hd:hd + 1]).T)
        o_ref[0] = jnp.concatenate(outs, axis=1).astype(o_ref.dtype)


def _dattn_body(qi_ref, ki_ref, bias_ref, qt_ref, zk_ref, vt_ref, lam_ref, ng_ref, o_ref,
                m_s, acc_s, btile, s_scr, causal_s, *, lambda_init):
    b = pl.program_id(0)
    p = pl.program_id(1)
    qi = qi_ref[p]
    ki = ki_ref[p]
    ta = zk_ref.shape[1]
    hd = DIFF_HEAD_DIM
    dv = 2 * hd

    @pl.when((b == 0) & (p == 0))
    def _():
        _init_bias_blocks(btile, bias_ref, C_HEADS, DIFF_HEADS)
        krow = lax.broadcasted_iota(I32, (ta, ta), 0)
        qcol = lax.broadcasted_iota(I32, (ta, ta), 1)
        causal_s[...] = jnp.where(krow <= qcol, 0.0, -jnp.inf)

    @pl.when(ki == 0)
    def _():
        _reset_softmax(m_s, acc_s)

    def heads(kind):
        for h in range(DIFF_HEADS):
            v_aug = _with_ones(vt_ref[0, dv * h:dv * (h + 1), :])
            for j in range(2):
                lo = (2 * h + j) * hd
                logits = jnp.dot(zk_ref[0, :, lo:lo + hd], qt_ref[0, lo:lo + hd, :], preferred_element_type=F32)
                if kind is not None:
                    s_scr[...] = logits + causal_s[...] if kind == 0 else logits
                    _add_bias_band(s_scr, btile, h, kind, ta)
                    logits = s_scr[...]
                _softmax_step_t(logits, v_aug, m_s, acc_s, 2 * h + j)

    for kind in (0, 1):
        pl.when(ki == qi - kind)(functools.partial(heads, kind))
    pl.when(ki < qi - 1)(functools.partial(heads, None))

    @pl.when(ki == qi)
    def _():
        lam_p = lam_ref[...]
        lam = (jnp.exp(jnp.sum(lam_p[0:1] * lam_p[1:2], axis=1, keepdims=True))
               - jnp.exp(jnp.sum(lam_p[2:3] * lam_p[3:4], axis=1, keepdims=True)) + lambda_init)
        outs = []
        for h in range(DIFF_HEADS):
            a1 = acc_s[2 * h]
            a2 = acc_s[2 * h + 1]
            o = a1[0:dv] / a1[dv:dv + 1] - lam * (a2[0:dv] / a2[dv:dv + 1])
            o = o * lax.rsqrt(jnp.mean(o * o, axis=0, keepdims=True) + EPS) * ng_ref[...] * (1.0 - lambda_init)
            outs.append(o.T)
        o_ref[0] = jnp.concatenate(outs, axis=1).astype(o_ref.dtype)


def _causal_pairs(nq):
    qi = [q for q in range(nq) for _ in range(q + 1)]
    ki = [k for q in range(nq) for k in range(q + 1)]
    return jnp.asarray(qi, I32), jnp.asarray(ki, I32)


def _cattn(z, zt, mask, rel_bias, ta):
    bsz, seq, _ = z.shape
    qi, ki = _causal_pairs(seq // ta)
    qmap = lambda col: (lambda b, p, qi, ki: (b, qi[p], col))
    kmap = lambda col: (lambda b, p, qi, ki: (b, ki[p], col))
    dv_aug = C_HEAD_DIM + ONES_ROWS
    grid_spec = pltpu.PrefetchScalarGridSpec(
        num_scalar_prefetch=2,
        grid=(bsz, qi.shape[0]),
        in_specs=[pl.BlockSpec(memory_space=pltpu.SMEM),
                  pl.BlockSpec((1, C_WIDTH, ta), lambda b, p, qi, ki: (b, ZT_QC, qi[p])),
                  pl.BlockSpec((1, ta, C_WIDTH), kmap(Z_KC)),
                  pl.BlockSpec((1, C_WIDTH, ta), lambda b, p, qi, ki: (b, ZT_VC, ki[p])),
                  pl.BlockSpec((1, ta, ta), lambda b, p, qi, ki: (b, ki[p], qi[p]))],
        out_specs=pl.BlockSpec((1, ta, C_WIDTH), qmap(0)),
        scratch_shapes=[pltpu.VMEM((C_HEADS, 1, ta), F32),
                        pltpu.VMEM((C_HEADS, dv_aug, ta), F32),
                        pltpu.VMEM((C_HEADS, 2, BAND, BAND), F32),
                        pltpu.VMEM((ta, ta), F32)],
    )
    return pl.pallas_call(
        _cattn_body,
        grid_spec=grid_spec,
        out_shape=jax.ShapeDtypeStruct((bsz, seq, C_WIDTH), BF16),
        compiler_params=_cparams(("arbitrary", "arbitrary")),
        name="dsa_attn",
    )(qi, ki, rel_bias, zt, z, zt, mask)


def _dattn(z, zt, rel_bias, diff_lam, diff_norm_g, lambda_init, ta):
    bsz, seq, _ = z.shape
    qi, ki = _causal_pairs(seq // ta)
    n_maps = 2 * DIFF_HEADS
    dv = 2 * DIFF_HEAD_DIM
    qmap = lambda col: (lambda b, p, qi, ki: (b, qi[p], col))
    kmap = lambda col: (lambda b, p, qi, ki: (b, ki[p], col))
    grid_spec = pltpu.PrefetchScalarGridSpec(
        num_scalar_prefetch=2,
        grid=(bsz, qi.shape[0]),
        in_specs=[pl.BlockSpec(memory_space=pltpu.SMEM),
                  pl.BlockSpec((1, DIFF_W, ta), lambda b, p, qi, ki: (b, ZT_QD, qi[p])),
                  pl.BlockSpec((1, ta, DIFF_W), kmap(Z_KD)),
                  pl.BlockSpec((1, DIFF_W, ta), lambda b, p, qi, ki: (b, ZT_VD, ki[p])),
                  pl.BlockSpec(diff_lam.shape, lambda b, p, qi, ki: (0, 0)),
                  pl.BlockSpec((dv, 1), lambda b, p, qi, ki: (0, 0))],
        out_specs=pl.BlockSpec((1, ta, DIFF_W), qmap(0)),
        scratch_shapes=[pltpu.VMEM((n_maps, 1, ta), F32),
                        pltpu.VMEM((n_maps, dv + ONES_ROWS, ta), F32),
                        pltpu.VMEM((DIFF_HEADS, 2, BAND, BAND), F32),
                        pltpu.VMEM((ta, ta), F32), pltpu.VMEM((ta, ta), F32)],
    )
    return pl.pallas_call(
        functools.partial(_dattn_body, lambda_init=lambda_init),
        grid_spec=grid_spec,
        out_shape=jax.ShapeDtypeStruct((bsz, seq, DIFF_W), BF16),
        compiler_params=_cparams(("arbitrary", "arbitrary")),
        name="diff_attn",
    )(qi, ki, rel_bias, zt, z, zt, diff_lam, diff_norm_g.reshape(dv, 1))


def _out_proj_body(oc_ref, od_ref, x_ref, g1_ref, w_ref, o_ref):
    y = (jnp.dot(oc_ref[0], w_ref[0:C_WIDTH, :], preferred_element_type=F32)
         + jnp.dot(od_ref[0], w_ref[C_WIDTH:C_WIDTH + DIFF_W, :], preferred_element_type=F32))
    o_ref[0] = x_ref[0] + g1_ref[0] * y


def _out_proj(out_c, out_d, x, g1, w_out_bf16, tm):
    bsz, seq, d = x.shape
    return pl.pallas_call(
        _out_proj_body,
        grid=(bsz, seq // tm),
        in_specs=[pl.BlockSpec((1, tm, C_WIDTH), lambda b, i: (b, i, 0)),
                  pl.BlockSpec((1, tm, DIFF_W), lambda b, i: (b, i, 0)),
                  pl.BlockSpec((1, tm, d), lambda b, i: (b, i, 0)),
                  pl.BlockSpec((1, 1, d), lambda b, i: (b, 0, 0)),
                  pl.BlockSpec((C_WIDTH + DIFF_W, d), lambda b, i: (0, 0))],
        out_specs=pl.BlockSpec((1, tm, d), lambda b, i: (b, i, 0)),
        out_shape=jax.ShapeDtypeStruct((bsz, seq, d), F32),
        compiler_params=_cparams(("arbitrary", "arbitrary")),
        name="attn_out_proj",
    )(out_c, out_d, x, g1.reshape(bsz, 1, d), w_out_bf16)


def _attn_in_weights(cd_w_in):
    sizes = (C_WIDTH, C_WIDTH, C_WIDTH, IDX_HEADS * IDX_DIM, IDX_DIM, IDX_HEADS, DIFF_W, DIFF_W, DIFF_W)
    cuts = np.cumsum(sizes)[:-1]
    q_c, k_c, v_c, q_i, k_i, w_i, q_d, k_d, v_d = jnp.split(cd_w_in, cuts, axis=1)
    d = cd_w_in.shape[0]
    w = jnp.concatenate([k_c, k_d, k_i, jnp.zeros((d, LANES - IDX_DIM), cd_w_in.dtype)], axis=1)
    wt = jnp.concatenate([v_c, v_d, q_c * (C_HEAD_DIM ** -0.5 * LOG2E), q_d * (DIFF_HEAD_DIM ** -0.5 * LOG2E), q_i,
                          w_i * (IDX_DIM * IDX_HEADS) ** -0.5,
                          jnp.zeros((d, BF16_ROWS - IDX_HEADS), cd_w_in.dtype)], axis=1).T
    return w.astype(BF16), wt.astype(BF16)


def kernel(x, c, positions, rel_bias, norm_g, final_norm_g, ada_w, ada_b, ab_w_in, ab_conv_a, ab_conv_b,
           ab_conv_b_bias, ab_ln_g, ab_ln_b, ab_w_out, cd_w_in, diff_lam, diff_norm_g, cd_w_out,
           moe_wr_g, moe_br_g, moe_wr_e, moe_br_e, moe_w_gate, moe_w_up, moe_w_down):
    del positions
    bsz, seq, d = x.shape
    depth = ada_w.shape[0]
    tm = min(512, seq)
    ta = min(ATT_TILE, seq)
    assert ta % BAND == 0 and BUCKET_START[-1] < BAND and seq % ta == 0
    topk = min(TOPK_MAX, seq // 4)
    mods = _ada_mod(c, ada_w, ada_b)
    for i in range(depth):
        sh1, sc1, g1, sh2, sc2, g2 = jnp.split(mods[i], 6, axis=-1)
        j = i // 2
        if i % 2 == 0:
            z = _norm_proj(x, norm_g[i, 0], sh1, sc1, ab_w_in[j].astype(BF16), tm)
            x = _conv_mix(z, x, g1, ab_conv_a[j], ab_conv_b[j], ab_conv_b_bias[j], ab_ln_g[j], ab_ln_b[j],
                          ab_w_out[j].astype(BF16), min(512, seq))
        else:
            lambda_init = 0.8 - 0.6 * math.exp(-0.3 * i)
            w, wt = _attn_in_weights(cd_w_in[j])
            z, zt = _norm_proj(x, norm_g[i, 0], sh1, sc1, w, tm, wt)
            mask = _select(z, zt, topk)
            out_c = _cattn(z, zt, mask, rel_bias, ta)
            out_d = _dattn(z, zt, rel_bias, diff_lam[j], diff_norm_g[j], lambda_init, ta)
            x = _out_proj(out_c, out_d, x, g1, cd_w_out[j].astype(BF16), tm)
        x = _hier_moe(x, norm_g[i, 1], sh2, sc2, g2, moe_wr_g[i], moe_br_g[i], moe_wr_e[i], moe_br_e[i],
                      moe_w_gate, moe_w_up, moe_w_down, i, final_norm_g, final_norm=(i == depth - 1))
    return x
```

```python
import functools
import math

import numpy as np
import jax
import jax.numpy as jnp
from jax import lax
from jax.experimental import pallas as pl
from jax.experimental.pallas import tpu as pltpu

F32 = jnp.float32
BF16 = jnp.bfloat16
I32 = jnp.int32
HIGHEST = lax.Precision.HIGHEST

EPS = 1e-6
A_WIDTH = 512
A_CONV = 3
B_WIDTH = 512
B_CONV = 31
C_HEADS = 8
C_HEAD_DIM = 64
IDX_HEADS = 8
IDX_DIM = 32
TOPK_MAX = 256
DIFF_HEADS = 4
DIFF_HEAD_DIM = 64
NUM_BUCKETS = 32
MAX_DISTANCE = 128
N_GROUPS = 4
EXPERTS_PER_GROUP = 8
N_EXPERTS = N_GROUPS * EXPERTS_PER_GROUP
C_WIDTH = C_HEADS * C_HEAD_DIM
DIFF_W = DIFF_HEADS * 2 * DIFF_HEAD_DIM
LANES = 128
BF16_ROWS = 16
INT_MIN = -(2 ** 31)
LOG2E = math.log2(math.e)
VMEM_LIMIT = 56 * 1024 * 1024

Z_KC, Z_KD = 0, 1
Z_KIDX_OFF = C_WIDTH + DIFF_W
Z_COLS = Z_KIDX_OFF + LANES
ZT_VC, ZT_VD, ZT_QC, ZT_QD = 0, 1, 2, 3
ZT_QIDX_OFF = 2 * C_WIDTH + 2 * DIFF_W
ZT_QIDX_ROWS = IDX_HEADS * IDX_DIM
ZT_W_OFF = ZT_QIDX_OFF + ZT_QIDX_ROWS
ZT_ROWS = ZT_W_OFF + BF16_ROWS


def _bucket_starts():
    n = np.arange(0, 2 * MAX_DISTANCE)
    me = NUM_BUCKETS // 2
    lr = np.log(np.maximum(n, 1) / me) / math.log(MAX_DISTANCE / me)
    large = me + (lr * (NUM_BUCKETS - me)).astype(np.int64)
    b = np.where(n < me, n, np.minimum(large, NUM_BUCKETS - 1))
    return [int(n[b >= k].min()) for k in range(NUM_BUCKETS)]


BUCKET_START = _bucket_starts()


def _cparams(sem):
    return pltpu.CompilerParams(dimension_semantics=sem, vmem_limit_bytes=VMEM_LIMIT)


def _rms(x):
    return x * lax.rsqrt(jnp.mean(x * x, axis=-1, keepdims=True) + EPS)


def _sigmoid(x):
    return 1.0 / (1.0 + jnp.exp(-x))


def _dot_nt(a, b):
    return lax.dot_general(a, b, (((1,), (1,)), ((), ())), preferred_element_type=F32)


def _ada_body(c_ref, w_ref, b_ref, o_ref):
    c = c_ref[...]
    cond = c * _sigmoid(c)
    o_ref[0] = jnp.dot(cond, w_ref[0], precision=HIGHEST, preferred_element_type=F32) + b_ref[0]


def _ada_mod(c, ada_w, ada_b):
    depth, d, n6 = ada_w.shape
    bsz = c.shape[0]
    rows = 8
    c_pad = jnp.zeros((rows, d), F32).at[:bsz].set(c)
    tn = 1536
    out = pl.pallas_call(
        _ada_body,
        grid=(depth, n6 // tn),
        in_specs=[pl.BlockSpec((rows, d), lambda i, j: (0, 0)),
                  pl.BlockSpec((1, d, tn), lambda i, j: (i, 0, j)),
                  pl.BlockSpec((1, 1, tn), lambda i, j: (i, 0, j))],
        out_specs=pl.BlockSpec((1, rows, tn), lambda i, j: (i, 0, j)),
        out_shape=jax.ShapeDtypeStruct((depth, rows, n6), F32),
        compiler_params=_cparams(("arbitrary", "arbitrary")),
        name="ada_mod",
    )(c_pad, ada_w, ada_b.reshape(depth, 1, n6))
    return out[:, :bsz]


def _norm_proj_body(x_ref, g_ref, sh_ref, sc_ref, w_ref, *rest):
    y = _rms(x_ref[0]) * g_ref[...]
    h = (y * (1.0 + sc_ref[0]) + sh_ref[0]).astype(BF16)
    if len(rest) == 1:
        (o_ref,) = rest
    else:
        wt_ref, o_ref, ot_ref = rest
        ot_ref[0] = _dot_nt(wt_ref[...], h).astype(ot_ref.dtype)
    o_ref[0] = jnp.dot(h, w_ref[...], preferred_element_type=F32).astype(o_ref.dtype)


def _norm_proj(x, g, sh, sc, w_bf16, tm, wt_bf16=None):
    bsz, seq, d = x.shape
    n = w_bf16.shape[1]
    in_specs = [pl.BlockSpec((1, tm, d), lambda b, i: (b, i, 0)),
                pl.BlockSpec((1, d), lambda b, i: (0, 0)),
                pl.BlockSpec((1, 1, d), lambda b, i: (b, 0, 0)),
                pl.BlockSpec((1, 1, d), lambda b, i: (b, 0, 0)),
                pl.BlockSpec((d, n), lambda b, i: (0, 0))]
    out_specs = pl.BlockSpec((1, tm, n), lambda b, i: (b, i, 0))
    out_shape = jax.ShapeDtypeStruct((bsz, seq, n), BF16)
    args = [x, g.reshape(1, d), sh.reshape(bsz, 1, d), sc.reshape(bsz, 1, d), w_bf16]
    if wt_bf16 is not None:
        nt = wt_bf16.shape[0]
        in_specs.append(pl.BlockSpec((nt, d), lambda b, i: (0, 0)))
        out_specs = [out_specs, pl.BlockSpec((1, nt, tm), lambda b, i: (b, 0, i))]
        out_shape = [out_shape, jax.ShapeDtypeStruct((bsz, nt, seq), BF16)]
        args.append(wt_bf16)
    return pl.pallas_call(
        _norm_proj_body,
        grid=(bsz, seq // tm),
        in_specs=in_specs,
        out_specs=out_specs,
        out_shape=out_shape,
        compiler_params=_cparams(("arbitrary", "arbitrary")),
        name="norm_proj",
    )(*args)


CONV_HALO = 32
CONV_ROWS = 64


def _conv_body(z_ref, x_ref, g1_ref, ca_ref, cb_ref, cbb_ref, lng_ref, lnb_ref, wo_ref, o_ref,
               ua_scr, ub_scr, y_scr, *, tl):
    l = pl.program_id(1)

    @pl.when(l == 0)
    def _():
        ua_scr[0:CONV_HALO, :] = jnp.zeros((CONV_HALO, A_WIDTH), F32)
        ub_scr[0:CONV_HALO, :] = jnp.zeros((CONV_HALO, B_WIDTH), F32)

    a = A_WIDTH
    gate_c = z_ref[0, :, a:2 * a].astype(F32)
    x_a = z_ref[0, :, 2 * a:3 * a].astype(F32)
    ua_scr[CONV_HALO:CONV_HALO + tl, :] = gate_c * x_a
    val_b = z_ref[0, :, 3 * a:3 * a + B_WIDTH].astype(F32)
    glu = z_ref[0, :, 3 * a + B_WIDTH:3 * a + 2 * B_WIDTH].astype(F32)
    ub_scr[CONV_HALO:CONV_HALO + tl, :] = val_b * _sigmoid(glu)

    for r in range(0, tl, CONV_ROWS):
        acc_a = None
        for k in range(A_CONV):
            tap = ua_scr[CONV_HALO + r - (A_CONV - 1) + k:CONV_HALO + r - (A_CONV - 1) + k + CONV_ROWS, :]
            term = tap * ca_ref[k:k + 1, :]
            acc_a = term if acc_a is None else acc_a + term
        gate_b = z_ref[0, r:r + CONV_ROWS, 0:a].astype(F32)
        y_scr[r:r + CONV_ROWS, 0:a] = (gate_b * acc_a).astype(BF16)

        win = CONV_HALO + r - 8
        acc_b = None
        for b in range(8):
            phase = None
            for back in range(b, B_CONV, 8):
                tap = ub_scr[win - (back - b):win - (back - b) + CONV_ROWS + 8, :]
                term = tap * cb_ref[B_CONV - 1 - back:B_CONV - back, :]
                phase = term if phase is None else phase + term
            piece = phase[8 - b:8 - b + CONV_ROWS]
            acc_b = piece if acc_b is None else acc_b + piece
        u = acc_b + cbb_ref[...]
        mu = jnp.mean(u, axis=-1, keepdims=True)
        uc = u - mu
        var = jnp.mean(uc * uc, axis=-1, keepdims=True)
        v = uc * lax.rsqrt(var + EPS) * lng_ref[...] + lnb_ref[...]
        y_scr[r:r + CONV_ROWS, a:a + B_WIDTH] = (v * _sigmoid(v)).astype(BF16)

    ua_scr[0:CONV_HALO, :] = ua_scr[tl:tl + CONV_HALO, :]
    ub_scr[0:CONV_HALO, :] = ub_scr[tl:tl + CONV_HALO, :]
    y = jnp.dot(y_scr[...], wo_ref[...], preferred_element_type=F32)
    o_ref[0] = x_ref[0] + g1_ref[0] * y


def _conv_mix(z, x, g1, conv_a, conv_b, conv_b_bias, ln_g, ln_b, w_out_bf16, tl):
    bsz, seq, d = x.shape
    nz = z.shape[-1]
    wide = A_WIDTH + B_WIDTH
    full = lambda shape: pl.BlockSpec(shape, lambda b, l: (0,) * len(shape))
    return pl.pallas_call(
        functools.partial(_conv_body, tl=tl),
        grid=(bsz, seq // tl),
        in_specs=[pl.BlockSpec((1, tl, nz), lambda b, l: (b, l, 0)),
                  pl.BlockSpec((1, tl, d), lambda b, l: (b, l, 0)),
                  pl.BlockSpec((1, 1, d), lambda b, l: (b, 0, 0)),
                  full((A_CONV, A_WIDTH)), full((B_CONV, B_WIDTH)), full((1, B_WIDTH)),
                  full((1, B_WIDTH)), full((1, B_WIDTH)), full((wide, d))],
        out_specs=pl.BlockSpec((1, tl, d), lambda b, l: (b, l, 0)),
        out_shape=jax.ShapeDtypeStruct((bsz, seq, d), F32),
        scratch_shapes=[pltpu.VMEM((CONV_HALO + tl, A_WIDTH), F32),
                        pltpu.VMEM((CONV_HALO + tl, B_WIDTH), F32),
                        pltpu.VMEM((tl, wide), BF16)],
        compiler_params=_cparams(("arbitrary", "arbitrary")),
        name="conv_mix",
    )(z, x, g1.reshape(bsz, 1, d), conv_a, conv_b, conv_b_bias.reshape(1, -1), ln_g.reshape(1, -1),
      ln_b.reshape(1, -1), w_out_bf16)


MOE_ROWS = 512
ROUTER_ROWS = 512
META_GATE0, META_GATE1, META_POS0, META_POS1 = range(4)
SEG_LEN, SEG_BASE, SEG_OFF = range(3)
SEG_ALIGN = 8
SEG_SIZES = tuple(2 ** b for b in range(10, 2, -1))
ROWS_START, ROWS_DATA_END, ROWS_REGION_END, ROWS_TOTAL = 0, N_EXPERTS, 2 * N_EXPERTS, 3 * N_EXPERTS
GROUP_LANE0 = N_EXPERTS


def _router_body(x_ref, g_ref, sh_ref, sc_ref, wr_ref, br_ref, tri_ref, upper_ref, h_ref, meta_ref, post_ref, seg_ref,
                 cnt_ref, base_scr):
    @pl.when(pl.program_id(0) == 0)
    def _():
        base_scr[...] = jnp.zeros_like(base_scr)

    h = _rms(x_ref[...]) * g_ref[...]
    h = h * (1.0 + sc_ref[0]) + sh_ref[0]
    h_ref[...] = h.astype(h_ref.dtype)
    h_hi = h.astype(BF16)
    h_lo = (h - h_hi.astype(F32)).astype(BF16)
    logits = (jnp.dot(h_hi, wr_ref[0], preferred_element_type=F32) + jnp.dot(h_hi, wr_ref[1], preferred_element_type=F32)
              + jnp.dot(h_lo, wr_ref[0], preferred_element_type=F32) + br_ref[...])
    tr = logits.shape[0]
    lane = lax.broadcasted_iota(I32, (tr, LANES), 1)
    lane_f = lane.astype(F32)
    neg = jnp.float32(-jnp.inf)
    big = jnp.float32(1e9)

    is_group = (lane >= GROUP_LANE0) & (lane < GROUP_LANE0 + N_GROUPS)
    glog = jnp.where(is_group, logits, neg)
    gmax = jnp.max(glog, axis=1, keepdims=True)
    p_top = 1.0 / jnp.sum(jnp.exp(glog - gmax), axis=1, keepdims=True)
    g_sel = jnp.min(jnp.where(glog == gmax, lane_f, big), axis=1, keepdims=True) - GROUP_LANE0
    lo = g_sel * EXPERTS_PER_GROUP
    in_group = (lane_f >= lo) & (lane_f < lo + EXPERTS_PER_GROUP)
    f1 = jnp.where(in_group, logits, neg)
    v1 = jnp.max(f1, axis=1, keepdims=True)
    i1 = jnp.min(jnp.where(f1 == v1, lane_f, big), axis=1, keepdims=True)
    f2 = jnp.where(lane_f == i1, neg, f1)
    v2 = jnp.max(f2, axis=1, keepdims=True)
    i2 = jnp.min(jnp.where(f2 == v2, lane_f, big), axis=1, keepdims=True)
    a = jnp.exp(v2 - v1)
    w1 = 1.0 / (1.0 + a)
    gate0 = p_top * w1
    gate1 = p_top * (a * w1)

    oh0 = lane_f == i1
    oh1 = lane_f == i2
    ind0 = jnp.where(oh0, 1.0, 0.0)
    ind1 = jnp.where(oh1, 1.0, 0.0)
    pre0 = jnp.dot(tri_ref[...], ind0.astype(BF16), preferred_element_type=F32)
    pre1 = jnp.dot(tri_ref[...], ind1.astype(BF16), preferred_element_type=F32)
    tot0 = jnp.sum(ind0, axis=0, keepdims=True)
    tot1 = jnp.sum(ind1, axis=0, keepdims=True)
    seg_len = jnp.floor((tot0 + tot1 + (SEG_ALIGN - 1)) * (1.0 / SEG_ALIGN)) * SEG_ALIGN
    seg_off = jnp.dot(jnp.broadcast_to(seg_len, (8, LANES)), upper_ref[...], precision=HIGHEST,
                      preferred_element_type=F32)[0:1]
    pos0 = jnp.sum(jnp.where(oh0, seg_off + pre0, 0.0), axis=1, keepdims=True)
    pos1 = jnp.sum(jnp.where(oh1, seg_off + tot0 + pre1, 0.0), axis=1, keepdims=True)
    base = base_scr[...]
    new_base = base + seg_len
    base_scr[...] = new_base
    cnt_ref[...] = new_base

    meta = jnp.zeros((tr, LANES), F32)
    for col, val in ((META_GATE0, gate0), (META_GATE1, gate1), (META_POS0, pos0), (META_POS1, pos1)):
        meta = jnp.where(lane == col, val, meta)
    meta_ref[...] = meta
    post_ref[0] = meta.T[0:8]
    srow = lax.broadcasted_iota(I32, (8, LANES), 0)
    seg = jnp.where(srow == SEG_LEN, seg_len, jnp.where(srow == SEG_BASE, base, jnp.where(srow == SEG_OFF, seg_off, 0.0)))
    seg_ref[0] = seg.astype(I32)


def _moe_router(x2, g, sh, sc, wr_g, br_g, wr_e, br_e, seq):
    n_tok, d = x2.shape
    bsz = n_tok // seq
    tr = min(ROUTER_ROWS, seq)
    steps_per_batch = seq // tr
    wr = jnp.zeros((d, LANES), F32).at[:, :N_EXPERTS].set(wr_e).at[:, GROUP_LANE0:GROUP_LANE0 + N_GROUPS].set(wr_g)
    wr_hi = wr.astype(BF16)
    wr = jnp.stack([wr_hi, (wr - wr_hi.astype(F32)).astype(BF16)])
    br = jnp.zeros((1, LANES), F32).at[0, :N_EXPERTS].set(br_e).at[0, GROUP_LANE0:GROUP_LANE0 + N_GROUPS].set(br_g)
    tri = jnp.tril(jnp.ones((tr, tr), BF16), -1)
    upper = jnp.triu(jnp.ones((LANES, LANES), F32), 1)
    nb = n_tok // tr
    full = lambda shape: pl.BlockSpec(shape, lambda i: (0,) * len(shape))
    return pl.pallas_call(
        _router_body,
        grid=(nb,),
        in_specs=[pl.BlockSpec((tr, d), lambda i: (i, 0)),
                  full((1, d)),
                  pl.BlockSpec((1, 1, d), lambda i: (i // steps_per_batch, 0, 0)),
                  pl.BlockSpec((1, 1, d), lambda i: (i // steps_per_batch, 0, 0)),
                  full((2, d, LANES)), full((1, LANES)), full((tr, tr)), full((LANES, LANES))],
        out_specs=[pl.BlockSpec((tr, d), lambda i: (i, 0)),
                   pl.BlockSpec((tr, LANES), lambda i: (i, 0)),
                   pl.BlockSpec((1, 8, tr), lambda i: (i, 0, 0)),
                   pl.BlockSpec((1, 8, LANES), lambda i: (i, 0, 0)),
                   full((1, LANES))],
        out_shape=[jax.ShapeDtypeStruct((n_tok, d), BF16),
                   jax.ShapeDtypeStruct((n_tok, LANES), F32),
                   jax.ShapeDtypeStruct((nb, 8, tr), F32),
                   jax.ShapeDtypeStruct((nb, 8, LANES), I32),
                   jax.ShapeDtypeStruct((1, LANES), F32)],
        scratch_shapes=[pltpu.VMEM((1, LANES), F32)],
        compiler_params=_cparams(("arbitrary",)),
        name="moe_router",
    )(x2, g.reshape(1, d), sh.reshape(bsz, 1, d), sc.reshape(bsz, 1, d), wr, br, tri, upper)


def _segment_copies(seg_ref, starts_ref, make_copy, action):
    def per_expert(e, carry):
        length = seg_ref[0, SEG_LEN, e]
        local = seg_ref[0, SEG_OFF, e]
        glob = starts_ref[e] + seg_ref[0, SEG_BASE, e]
        for size in SEG_SIZES:
            hit = (length & size) != 0

            @pl.when(hit)
            def _(local=local, glob=glob, size=size):
                cp = make_copy(pl.multiple_of(local, SEG_ALIGN), pl.multiple_of(glob, SEG_ALIGN), size)
                cp.start() if action == "start" else cp.wait()

            step = jnp.where(hit, size, 0)
            local = local + step
            glob = glob + step
        return carry

    lax.fori_loop(0, N_EXPERTS, per_expert, 0)


def _local_rows(tr):
    return 2 * tr + N_EXPERTS * SEG_ALIGN


def _row_words(d):
    return d // 2 + LANES


def _dispatch_body(starts_ref, seg_ref, prev_seg_ref, h_ref, post_ref, xr_ref, xs_scr, zero_scr, sems, *, tr):
    i = pl.program_id(0)
    slot = i % 2
    lb = xs_scr.shape[1]
    sem = sems.at[0]

    def make_copy_of(s):
        def make_copy(local, glob, size):
            return pltpu.make_async_copy(xs_scr.at[s, pl.ds(local, size)], xr_ref.at[pl.ds(glob, size)], sems.at[s])
        return make_copy

    rows = lax.broadcasted_iota(I32, (lb, tr), 0)
    first = rows == post_ref[0, META_POS0:META_POS0 + 1, :].astype(I32)
    second = rows == post_ref[0, META_POS1:META_POS1 + 1, :].astype(I32)
    x = jnp.dot(jnp.where(first | second, 1.0, 0.0).astype(BF16), h_ref[...], preferred_element_type=F32)
    bits = lax.bitcast_convert_type(x, I32)
    half = x.shape[1] // 2
    xs_scr[slot, :, 0:half] = lax.shift_right_logical(bits[:, 0:half], 16) | (bits[:, half:] & jnp.int32(-65536))
    gate = jnp.sum(jnp.where(first, post_ref[0, META_GATE0:META_GATE0 + 1, :], 0.0)
                   + jnp.where(second, post_ref[0, META_GATE1:META_GATE1 + 1, :], 0.0), axis=1, keepdims=True)
    xs_scr[slot, :, half:half + LANES] = jnp.broadcast_to(lax.bitcast_convert_type(gate, I32), (lb, LANES))
    _segment_copies(seg_ref, starts_ref, make_copy_of(slot), "start")

    @pl.when(i > 0)
    def _():
        _segment_copies(prev_seg_ref, starts_ref, make_copy_of(1 - slot), "wait")

    @pl.when(i == pl.num_programs(0) - 1)
    def _():
        _segment_copies(seg_ref, starts_ref, make_copy_of(slot), "wait")
        zero_scr[...] = jnp.zeros_like(zero_scr)
        total = starts_ref[ROWS_TOTAL]
        for action in ("start", "wait"):
            def per_expert(e, carry, action=action):
                row = starts_ref[ROWS_DATA_END + e]
                gap = starts_ref[ROWS_REGION_END + e] - row
                for size in SEG_SIZES:
                    if size >= MOE_ROWS:
                        continue
                    hit = (gap & size) != 0

                    @pl.when(hit)
                    def _(row=row, size=size):
                        cp = pltpu.make_async_copy(zero_scr.at[pl.ds(0, size)],
                                                   xr_ref.at[pl.ds(pl.multiple_of(row, SEG_ALIGN), size)], sem)
                        cp.start() if action == "start" else cp.wait()

                    row = row + jnp.where(hit, size, 0)
                return carry

            lax.fori_loop(0, N_EXPERTS, per_expert, 0)

        def free_block(b):
            return pltpu.make_async_copy(
                zero_scr, xr_ref.at[pl.ds(pl.multiple_of(b * MOE_ROWS, MOE_ROWS), MOE_ROWS)], sem)

        first_free = (total + MOE_ROWS - 1) // MOE_ROWS
        n_blocks = xr_ref.shape[0] // MOE_ROWS
        lax.fori_loop(first_free, n_blocks, lambda b, c: (free_block(b).start(), c)[1], 0)
        lax.fori_loop(first_free, n_blocks, lambda b, c: (free_block(b).wait(), c)[1], 0)


def _moe_dispatch(h2, post, seg, starts, n_rows, tr):
    n_tok, d = h2.shape
    grid_spec = pltpu.PrefetchScalarGridSpec(
        num_scalar_prefetch=1,
        grid=(n_tok // tr,),
        in_specs=[pl.BlockSpec((1, 8, LANES), lambda i, s: (i, 0, 0), memory_space=pltpu.SMEM),
                  pl.BlockSpec((1, 8, LANES), lambda i, s: (jnp.maximum(i - 1, 0), 0, 0), memory_space=pltpu.SMEM),
                  pl.BlockSpec((tr, d), lambda i, s: (i, 0)),
                  pl.BlockSpec((1, 8, tr), lambda i, s: (i, 0, 0))],
        out_specs=pl.BlockSpec(memory_space=pl.ANY),
        scratch_shapes=[pltpu.VMEM((2, _local_rows(tr), _row_words(d)), I32), pltpu.VMEM((MOE_ROWS, _row_words(d)), I32),
                        pltpu.SemaphoreType.DMA((2,))],
    )
    return pl.pallas_call(
        functools.partial(_dispatch_body, tr=tr),
        grid_spec=grid_spec,
        out_shape=jax.ShapeDtypeStruct((n_rows, _row_words(d)), I32),
        compiler_params=_cparams(("arbitrary",)),
        name="moe_dispatch",
    )(starts, seg, seg, h2, post)


def _expert_body(pb_ref, pe_ref, plo_ref, phi_ref, x_ref, wg_ref, wu_ref, wd_ref, o_ref, wg_s, wu_s, wd_s):
    p = pl.program_id(0)
    prev = jnp.maximum(p - 1, 0)
    new_expert = (p == 0) | (pe_ref[p] != pe_ref[prev])
    first = (p == 0) | (pb_ref[p] != pb_ref[prev])

    @pl.when(new_expert)
    def _():
        wg_s[...] = wg_ref[0, 0].astype(BF16)
        wu_s[...] = wu_ref[0, 0].astype(BF16)
        wd_s[...] = wd_ref[0, 0].astype(BF16)

    def rows_of_expert():
        half = wg_s.shape[0] // 2
        words = x_ref[:, 0:half]
        x_lo = lax.bitcast_convert_type(words << 16, F32).astype(BF16)
        x_hi = lax.bitcast_convert_type(words & jnp.int32(-65536), F32).astype(BF16)
        gate = lax.bitcast_convert_type(x_ref[:, half:half + 1], F32)
        gt = (jnp.dot(x_lo, wg_s[0:half], preferred_element_type=F32)
              + jnp.dot(x_hi, wg_s[half:], preferred_element_type=F32))
        up = (jnp.dot(x_lo, wu_s[0:half], preferred_element_type=F32)
              + jnp.dot(x_hi, wu_s[half:], preferred_element_type=F32))
        act = (gt * _sigmoid(gt)) * up
        y = jnp.dot(act.astype(BF16), wd_s[...], preferred_element_type=F32) * gate
        rows = lax.broadcasted_iota(I32, (y.shape[0], 1), 0)
        return jnp.where((rows >= plo_ref[p]) & (rows < phi_ref[p]), y, 0.0)

    nonempty = phi_ref[p] > plo_ref[p]

    @pl.when(first & nonempty)
    def _():
        o_ref[...] = rows_of_expert()

    @pl.when(first & jnp.logical_not(nonempty))
    def _():
        o_ref[...] = jnp.zeros_like(o_ref)

    @pl.when(jnp.logical_not(first) & nonempty)
    def _():
        o_ref[...] += rows_of_expert()


def _moe_experts(x_rows, pairs, w_gate, w_up, w_down, layer):
    n_rows = x_rows.shape[0]
    d, de = w_gate.shape[-2:]
    n_pairs = pairs[0].shape[0]
    grid_spec = pltpu.PrefetchScalarGridSpec(
        num_scalar_prefetch=4,
        grid=(n_pairs,),
        in_specs=[pl.BlockSpec((MOE_ROWS, _row_words(d)), lambda p, pb, pe, lo, hi: (pb[p], 0)),
                  pl.BlockSpec((1, 1, d, de), lambda p, pb, pe, lo, hi: (layer, pe[p], 0, 0)),
                  pl.BlockSpec((1, 1, d, de), lambda p, pb, pe, lo, hi: (layer, pe[p], 0, 0)),
                  pl.BlockSpec((1, 1, de, d), lambda p, pb, pe, lo, hi: (layer, pe[p], 0, 0))],
        out_specs=pl.BlockSpec((MOE_ROWS, d), lambda p, pb, pe, lo, hi: (pb[p], 0)),
        scratch_shapes=[pltpu.VMEM((d, de), BF16), pltpu.VMEM((d, de), BF16), pltpu.VMEM((de, d), BF16)],
    )
    return pl.pallas_call(
        _expert_body,
        grid_spec=grid_spec,
        out_shape=jax.ShapeDtypeStruct((n_rows, d), F32),
        compiler_params=_cparams(("arbitrary",)),
        name="moe_experts",
    )(*pairs, x_rows, w_gate, w_up, w_down)


def _combine_body(starts_ref, seg_ref, next_seg_ref, y_ref, x_ref, meta_ref, g2_ref, fg_ref, o_ref, ys_scr, sems, *,
                  final_norm):
    i = pl.program_id(0)
    slot = i % 2

    def make_copy_of(s):
        def make_copy(local, glob, size):
            return pltpu.make_async_copy(y_ref.at[pl.ds(glob, size)], ys_scr.at[s, pl.ds(local, size)], sems.at[s])
        return make_copy

    @pl.when(i == 0)
    def _():
        ys_scr[...] = jnp.zeros_like(ys_scr)
        _segment_copies(seg_ref, starts_ref, make_copy_of(slot), "start")

    @pl.when(i + 1 < pl.num_programs(0))
    def _():
        _segment_copies(next_seg_ref, starts_ref, make_copy_of(1 - slot), "start")

    _segment_copies(seg_ref, starts_ref, make_copy_of(slot), "wait")

    meta = meta_ref[...]
    tr = meta.shape[0]
    y = ys_scr[slot].astype(BF16)
    cols = lax.broadcasted_iota(I32, (tr, ys_scr.shape[1]), 1)

    pick = ((cols == meta[:, META_POS0:META_POS0 + 1].astype(I32))
            | (cols == meta[:, META_POS1:META_POS1 + 1].astype(I32)))
    moe = jnp.dot(jnp.where(pick, 1.0, 0.0).astype(BF16), y, preferred_element_type=F32)
    xn = x_ref[...] + g2_ref[0] * moe
    if final_norm:
        xn = _rms(xn) * fg_ref[...]
    o_ref[...] = xn


def _moe_combine(y_rows, seg, starts, x2, meta, g2, final_g, seq, tr, final_norm):
    n_tok, d = x2.shape
    bsz = n_tok // seq
    steps_per_batch = seq // tr
    nb = n_tok // tr
    grid_spec = pltpu.PrefetchScalarGridSpec(
        num_scalar_prefetch=1,
        grid=(nb,),
        in_specs=[pl.BlockSpec((1, 8, LANES), lambda i, s: (i, 0, 0), memory_space=pltpu.SMEM),
                  pl.BlockSpec((1, 8, LANES), lambda i, s: (jnp.minimum(i + 1, nb - 1), 0, 0), memory_space=pltpu.SMEM),
                  pl.BlockSpec(memory_space=pl.ANY),
                  pl.BlockSpec((tr, d), lambda i, s: (i, 0)),
                  pl.BlockSpec((tr, LANES), lambda i, s: (i, 0)),
                  pl.BlockSpec((1, 1, d), lambda i, s: (i // steps_per_batch, 0, 0)),
                  pl.BlockSpec((1, d), lambda i, s: (0, 0))],
        out_specs=pl.BlockSpec((tr, d), lambda i, s: (i, 0)),
        scratch_shapes=[pltpu.VMEM((2, _local_rows(tr), d), F32), pltpu.SemaphoreType.DMA((2,))],
    )
    return pl.pallas_call(
        functools.partial(_combine_body, final_norm=final_norm),
        grid_spec=grid_spec,
        out_shape=jax.ShapeDtypeStruct((n_tok, d), F32),
        compiler_params=_cparams(("arbitrary",)),
        name="moe_combine",
    )(starts, seg, seg, y_rows, x2, meta, g2.reshape(bsz, 1, d), final_g.reshape(1, d))


def _expert_pairs(starts, ends, n_rows):
    n_blocks = n_rows // MOE_ROWS
    n_pairs = n_blocks + N_EXPERTS
    first_blk = starts // MOE_ROWS
    last_blk = (ends - 1) // MOE_ROWS
    npairs = jnp.where(ends > starts, last_blk - first_blk + 1, 0)
    pend = jnp.cumsum(npairs)
    poff = pend - npairs
    total = pend[-1]
    used_blocks = jnp.max(jnp.where(ends > starts, last_blk + 1, 0))
    p = jnp.arange(n_pairs, dtype=I32)
    p_eff = jnp.minimum(p, total - 1)
    e = jnp.minimum(jnp.sum(pend[None, :] <= p_eff[:, None], axis=1), N_EXPERTS - 1).astype(I32)
    table = jnp.stack([first_blk, poff, starts, ends], axis=1).astype(F32)
    onehot = (e[:, None] == jnp.arange(N_EXPERTS, dtype=I32)[None, :]).astype(F32)
    first_e, poff_e, start_e, end_e = jnp.dot(onehot, table, precision=HIGHEST).astype(I32).T
    valid = p < total
    blk = jnp.where(valid, first_e + p_eff - poff_e, jnp.minimum(used_blocks + p - total, n_blocks - 1)).astype(I32)
    lo = jnp.where(valid, jnp.clip(start_e - blk * MOE_ROWS, 0, MOE_ROWS), 0).astype(I32)
    hi = jnp.where(valid, jnp.clip(end_e - blk * MOE_ROWS, 0, MOE_ROWS), 0).astype(I32)
    return blk, e, lo, hi


def _hier_moe(x, g, sh, sc, g2, wr_g, br_g, wr_e, br_e, w_gate, w_up, w_down, layer, final_g, final_norm):
    bsz, seq, d = x.shape
    n_tok = bsz * seq
    tr = min(ROUTER_ROWS, seq)
    assert 2 * tr <= SEG_SIZES[0]
    x2 = x.reshape(n_tok, d)
    h2, meta, post, seg, cnt = _moe_router(x2, g, sh, sc, wr_g, br_g, wr_e, br_e, seq)
    counts = cnt[0, :N_EXPERTS].astype(I32)
    region_end = jnp.cumsum(-(-counts // MOE_ROWS) * MOE_ROWS)
    starts = jnp.concatenate([region_end[:1] * 0, region_end[:-1]])
    data_end = starts + counts
    layout = jnp.concatenate([starts, data_end, region_end, region_end[-1:]])
    n_rows = -(-(2 * n_tok + (n_tok // tr) * N_EXPERTS * (SEG_ALIGN - 1) + N_EXPERTS * (MOE_ROWS - SEG_ALIGN))
               // MOE_ROWS) * MOE_ROWS
    x_rows = _moe_dispatch(h2, post, seg, layout, n_rows, tr)
    y_rows = _moe_experts(x_rows, _expert_pairs(starts, data_end, n_rows), w_gate, w_up, w_down, layer)
    out = _moe_combine(y_rows, seg, layout, x2, meta, g2, final_g, seq, tr, final_norm)
    return out.reshape(bsz, seq, d)


SEL_COLS = 512
SEL_CHUNK = 512
BIT_GROUP = 256
SEL_SWEEP = 64
ATT_TILE = 1024


def _idx_score_t(k, qt, wt):
    acc = None
    for h in range(IDX_HEADS):
        rel = jnp.dot(k, qt[h * IDX_DIM:(h + 1) * IDX_DIM], preferred_element_type=F32)
        term = jnp.maximum(rel, 0.0) * wt[h:h + 1, :]
        acc = term if acc is None else acc + term
    return acc


def _bit_transpose32(a):
    a = list(a)
    m, j = 0x0000FFFF, 16
    while j:
        k = 0
        while k < 32:
            t = (a[k] ^ lax.shift_right_logical(a[k + j], jnp.int32(j))) & jnp.int32(m)
            a[k] = a[k] ^ t
            a[k + j] = a[k + j] ^ (t << j)
            k = (k + j + 1) & ~j
        j >>= 1
        m = (m ^ (m << j)) & 0xFFFFFFFF
    return a


def _select_body(qt_ref, zk_ref, wt_ref, o_ref, planes_scr, eq_scr, gt_scr, *, topk):
    i = pl.program_id(1)
    tq = qt_ref.shape[2]
    qt = qt_ref[0]
    wt = wt_ref[0].astype(F32)
    n_chunks = ((i + 1) * tq + SEL_CHUNK - 1) // SEL_CHUNK
    krow0 = lax.broadcasted_iota(I32, (SEL_CHUNK, tq), 0)
    qcol = i * tq + lax.broadcasted_iota(I32, (SEL_CHUNK, tq), 1)
    int_min = jnp.int32(INT_MIN)

    groups = SEL_CHUNK // BIT_GROUP
    words = SEL_CHUNK // 32
    sweep_rows = min(SEL_SWEEP, eq_scr.shape[0])

    def fill(masked, c, carry):
        off = pl.multiple_of(c * SEL_CHUNK, SEL_CHUNK)
        k = zk_ref[0, pl.ds(off, SEL_CHUNK), :][:, 0:IDX_DIM]
        bits = lax.bitcast_convert_type(_idx_score_t(k, qt, wt), I32)
        key = bits ^ ((bits >> 31) | int_min)
        if masked:
            key = jnp.where(krow0 + off <= qcol, key, 0)
        for g in range(groups):
            for lt in range(0, tq, LANES):
                ku = key[g * BIT_GROUP:(g + 1) * BIT_GROUP, lt:lt + LANES]
                planes = _bit_transpose32([ku[8 * r:8 * r + 8] for r in range(32)])
                wrow = pl.multiple_of(c * words + g * 8, 8)
                for b in range(32):
                    planes_scr[b, pl.ds(wrow, 8), lt:lt + LANES] = planes[b]
        return carry

    n_plain = (i * tq + 1) // SEL_CHUNK
    lax.fori_loop(0, n_plain, functools.partial(fill, False), 0)
    lax.fori_loop(n_plain, n_chunks, functools.partial(fill, True), 0)

    n_sweep = (n_chunks * words + sweep_rows - 1) // sweep_rows

    def pad(c, carry):
        wrow = pl.multiple_of(c * words, words)
        for b in range(32):
            planes_scr[b, pl.ds(wrow, words), :] = jnp.zeros((words, tq), I32)
        return carry

    lax.fori_loop(n_chunks, n_sweep * (sweep_rows // words), pad, 0)

    def sweep(upd, cnt_plane):
        def body(sb, acc):
            r0 = pl.multiple_of(sb * sweep_rows, sweep_rows)
            eq = eq_scr[pl.ds(r0, sweep_rows), :]
            gt = gt_scr[pl.ds(r0, sweep_rows), :]
            if upd is not None:
                plane, accept = upd
                hit = eq & planes_scr[plane, pl.ds(r0, sweep_rows), :]
                gt = jnp.where(accept, gt, gt | hit)
                eq = jnp.where(accept, hit, eq ^ hit)
                eq_scr[pl.ds(r0, sweep_rows), :] = eq
                gt_scr[pl.ds(r0, sweep_rows), :] = gt
            if cnt_plane is None:
                return acc
            ones = lax.population_count(gt | (eq & planes_scr[cnt_plane, pl.ds(r0, sweep_rows), :]))
            return acc + jnp.sum(ones.reshape(sweep_rows // 8, 8, tq), axis=0)
        acc = lax.fori_loop(0, n_sweep, body, jnp.zeros((8, tq), I32))
        return jnp.sum(acc.astype(F32), axis=0, keepdims=True)

    k_f = jnp.float32(topk)
    eq_scr[...] = jnp.full(eq_scr.shape, -1, I32)
    gt_scr[...] = jnp.zeros(gt_scr.shape, I32)

    def bit_step(ib, carry):
        u, cnt = carry
        accept = cnt >= k_f
        u = jnp.where(accept, u | jnp.left_shift(jnp.int32(1), 32 - ib), u)
        return u, sweep((ib - 1, accept), ib)

    u, cnt = lax.fori_loop(1, 32, bit_step, (jnp.zeros((1, tq), I32), sweep(None, 0)))
    accept = cnt >= k_f
    u = jnp.where(accept, u | 1, u)
    sweep((31, accept), None)
    some = jnp.where(u != 0, -1, 0)

    def popcount_rows(word_of):
        def body(sb, acc):
            r0 = pl.multiple_of(sb * sweep_rows, sweep_rows)
            ones = lax.population_count(word_of(r0))
            return acc + jnp.sum(ones.reshape(sweep_rows // 8, 8, tq), axis=0)
        acc = lax.fori_loop(0, n_sweep, body, jnp.zeros((8, tq), I32))
        return jnp.sum(acc.astype(F32), axis=0, keepdims=True)

    n_gt = popcount_rows(lambda r0: gt_scr[pl.ds(r0, sweep_rows), :])
    n_eq = jnp.where(u != 0, popcount_rows(lambda r0: eq_scr[pl.ds(r0, sweep_rows), :]), 0.0)

    @pl.when(jnp.max(n_gt + n_eq) > k_f)
    def _():
        wrow = lax.broadcasted_iota(I32, (sweep_rows, tq), 0)

        def below(cap, r0):
            w = wrow + r0
            last = lax.shift_right_arithmetic(cap - (w >> 3) * BIT_GROUP - (w & 7) - 1, 3)
            clear = jnp.clip(31 - last, 0, 32)
            return jnp.where(clear >= 32, 0, jnp.left_shift(jnp.int32(-1), jnp.minimum(clear, 31)))

        need = k_f - n_gt

        def cap_step(ib, v):
            cand = v | jnp.left_shift(jnp.int32(1), 14 - ib)
            cnt = popcount_rows(lambda r0: eq_scr[pl.ds(r0, sweep_rows), :] & below(cand, r0))
            return jnp.where(cnt <= need, cand, v)

        cap = lax.fori_loop(0, 15, cap_step, jnp.zeros((1, tq), I32))
        cap = jnp.where((u != 0) & (n_gt + n_eq > k_f), cap, 2 ** 30)

        def trim(sb, carry):
            r0 = pl.multiple_of(sb * sweep_rows, sweep_rows)
            eq_scr[pl.ds(r0, sweep_rows), :] = eq_scr[pl.ds(r0, sweep_rows), :] & below(cap, r0)
            return carry

        lax.fori_loop(0, n_sweep, trim, 0)

    def emit(c, carry):
        off = pl.multiple_of(c * SEL_CHUNK, SEL_CHUNK)
        for g in range(groups):
            wrow = pl.multiple_of(c * words + g * 8, 8)
            keep = gt_scr[pl.ds(wrow, 8), :] | (eq_scr[pl.ds(wrow, 8), :] & some)
            rows = [jnp.where((lax.shift_right_logical(keep, 31 - r) & 1) != 0, 0.0, -jnp.inf) for r in range(32)]
            o_ref[0, pl.ds(pl.multiple_of(off + g * BIT_GROUP, BIT_GROUP), BIT_GROUP), :] = (
                jnp.concatenate(rows, axis=0).astype(o_ref.dtype))
        return carry

    lax.fori_loop(0, n_chunks, emit, 0)

    def blank(c, carry):
        off = pl.multiple_of(c * SEL_CHUNK, SEL_CHUNK)
        o_ref[0, pl.ds(off, SEL_CHUNK), :] = jnp.full((SEL_CHUNK, tq), -jnp.inf, o_ref.dtype)
        return carry

    lax.fori_loop(n_chunks, o_ref.shape[1] // SEL_CHUNK, blank, 0)


def _select(z, zt, topk):
    bsz, seq, _ = z.shape
    tq = min(SEL_COLS, seq)
    assert seq % SEL_CHUNK == 0 and seq % tq == 0
    return pl.pallas_call(
        functools.partial(_select_body, topk=topk),
        grid=(bsz, seq // tq),
        in_specs=[pl.BlockSpec((1, ZT_QIDX_ROWS, tq), lambda b, i: (b, ZT_QIDX_OFF // ZT_QIDX_ROWS, i)),
                  pl.BlockSpec((1, seq, LANES), lambda b, i: (b, 0, Z_KIDX_OFF // LANES)),
                  pl.BlockSpec((1, BF16_ROWS, tq), lambda b, i: (b, ZT_W_OFF // BF16_ROWS, i))],
        out_specs=pl.BlockSpec((1, seq, tq), lambda b, i: (b, 0, i)),
        out_shape=jax.ShapeDtypeStruct((bsz, seq, seq), BF16),
        scratch_shapes=[pltpu.VMEM((32, seq // 32, tq), I32),
                        pltpu.VMEM((seq // 32, tq), I32), pltpu.VMEM((seq // 32, tq), I32)],
        compiler_params=_cparams(("arbitrary", "arbitrary")),
        name="dsa_select",
    )(zt, z, zt)


BIAS_INIT_ROWS = 8
ONES_ROWS = BF16_ROWS


BAND = 128


def _init_bias_blocks(btile, bias_ref, head0, n_heads):
    ta = BAND
    col = lax.broadcasted_iota(I32, (BIAS_INIT_ROWS, ta), 1)
    row0 = lax.broadcasted_iota(I32, (BIAS_INIT_ROWS, ta), 0)

    def body(r, carry):
        off = pl.multiple_of(r * BIAS_INIT_ROWS, BIAS_INIT_ROWS)
        for kind in range(2):
            dist = col - (row0 + off) + kind * ta
            for h in range(n_heads):
                far = bias_ref[NUM_BUCKETS - 1, head0 + h]
                val = jnp.full((BIAS_INIT_ROWS, ta), (bias_ref[0, head0 + h] - far) * LOG2E, F32)
                for b in range(1, NUM_BUCKETS - 1):
                    val = jnp.where(dist >= BUCKET_START[b], (bias_ref[b, head0 + h] - far) * LOG2E, val)
                val = jnp.where(dist >= BUCKET_START[NUM_BUCKETS - 1], 0.0, val)
                btile[h, kind, pl.ds(off, BIAS_INIT_ROWS), :] = val
        return carry

    lax.fori_loop(0, ta // BIAS_INIT_ROWS, body, 0)


def _add_bias_band(s_scr, btile, h, kind, ta):
    nb = ta // BAND
    blk = lambda i: slice(i * BAND, (i + 1) * BAND)
    if kind == 0:
        for kb in range(nb):
            s_scr[blk(kb), blk(kb)] += btile[h, 0]
            if kb + 1 < nb:
                s_scr[blk(kb), blk(kb + 1)] += btile[h, 1]
    else:
        s_scr[blk(nb - 1), blk(0)] += btile[h, 1]


def _with_ones(vt):
    return jnp.concatenate([vt, jnp.ones((ONES_ROWS, vt.shape[1]), vt.dtype)], axis=0)


def _softmax_step_t(logits, v_aug, m_ref, acc_ref, idx):
    m_old = m_ref[idx]
    m_new = jnp.maximum(m_old, jnp.max(logits, axis=0, keepdims=True))
    m_safe = jnp.where(m_new == -jnp.inf, 0.0, m_new)
    p = jnp.exp2(logits - m_safe)
    alpha = jnp.exp2(m_old - m_safe)
    acc_ref[idx] = alpha * acc_ref[idx] + jnp.dot(v_aug, p.astype(BF16), preferred_element_type=F32)
    m_ref[idx] = m_new


def _reset_softmax(m_s, acc_s):
    m_s[...] = jnp.full(m_s.shape, -jnp.inf, F32)
    acc_s[...] = jnp.zeros(acc_s.shape, F32)


def _cattn_body(qi_ref, ki_ref, bias_ref, qt_ref, zk_ref, vt_ref, mask_ref, o_ref, m_s, acc_s, btile, s_scr):
    b = pl.program_id(0)
    p = pl.program_id(1)
    qi = qi_ref[p]
    ki = ki_ref[p]
    ta = zk_ref.shape[1]
    hd = C_HEAD_DIM

    @pl.when((b == 0) & (p == 0))
    def _():
        _init_bias_blocks(btile, bias_ref, 0, C_HEADS)

    @pl.when(ki == 0)
    def _():
        _reset_softmax(m_s, acc_s)

    def heads(kind):
        for h in range(C_HEADS):
            lo = h * hd
            logits = (jnp.dot(zk_ref[0, :, lo:lo + hd], qt_ref[0, lo:lo + hd, :], preferred_element_type=F32)
                      + mask_ref[0].astype(F32))
            if kind is not None:
                s_scr[...] = logits
                _add_bias_band(s_scr, btile, h, kind, ta)
                logits = s_scr[...]
            _softmax_step_t(logits, _with_ones(vt_ref[0, lo:lo + hd, :]), m_s, acc_s, h)

    for kind in (0, 1):
        pl.when(ki == qi - kind)(functools.partial(heads, kind))
    pl.when(ki < qi - 1)(functools.partial(heads, None))

    @pl.when(ki == qi)
    def _():
        outs = []
        for h in range(C_HEADS):
            a = acc_s[h]
            outs.append((a[0:hd] / a[hd:hd + 1]).T)
        o_ref[0] = jnp.concatenate(outs, axis=1).astype(o_ref.dtype)


def _dattn_body(qi_ref, ki_ref, bias_ref, qt_ref, zk_ref, vt_ref, lam_ref, ng_ref, o_ref,
                m_s, acc_s, btile, s_scr, causal_s, *, lambda_init):
    b = pl.program_id(0)
    p = pl.program_id(1)
    qi = qi_ref[p]
    ki = ki_ref[p]
    ta = zk_ref.shape[1]
    hd = DIFF_HEAD_DIM
    dv = 2 * hd

    @pl.when((b == 0) & (p == 0))
    def _():
        _init_bias_blocks(btile, bias_ref, C_HEADS, DIFF_HEADS)
        krow = lax.broadcasted_iota(I32, (ta, ta), 0)
        qcol = lax.broadcasted_iota(I32, (ta, ta), 1)
        causal_s[...] = jnp.where(krow <= qcol, 0.0, -jnp.inf)

    @pl.when(ki == 0)
    def _():
        _reset_softmax(m_s, acc_s)

    def heads(kind):
        for h in range(DIFF_HEADS):
            v_aug = _with_ones(vt_ref[0, dv * h:dv * (h + 1), :])
            for j in range(2):
                lo = (2 * h + j) * hd
                logits = jnp.dot(zk_ref[0, :, lo:lo + hd], qt_ref[0, lo:lo + hd, :], preferred_element_type=F32)
                if kind is not None:
                    s_scr[...] = logits + causal_s[...] if kind == 0 else logits
                    _add_bias_band(s_scr, btile, h, kind, ta)
                    logits = s_scr[...]
                _softmax_step_t(logits, v_aug, m_s, acc_s, 2 * h + j)

    for kind in (0, 1):
        pl.when(ki == qi - kind)(functools.partial(heads, kind))
    pl.when(ki < qi - 1)(functools.partial(heads, None))

    @pl.when(ki == qi)
    def _():
        lam_p = lam_ref[...]
        lam = (jnp.exp(jnp.sum(lam_p[0:1] * lam_p[1:2], axis=1, keepdims=True))
               - jnp.exp(jnp.sum(lam_p[2:3] * lam_p[3:4], axis=1, keepdims=True)) + lambda_init)
        outs = []
        for h in range(DIFF_HEADS):
            a1 = acc_s[2 * h]
            a2 = acc_s[2 * h + 1]
            o = a1[0:dv] / a1[dv:dv + 1] - lam * (a2[0:dv] / a2[dv:dv + 1])
            o = o * lax.rsqrt(jnp.mean(o * o, axis=0, keepdims=True) + EPS) * ng_ref[...] * (1.0 - lambda_init)
            outs.append(o.T)
        o_ref[0] = jnp.concatenate(outs, axis=1).astype(o_ref.dtype)


def _causal_pairs(nq):
    qi = [q for q in range(nq) for _ in range(q + 1)]
    ki = [k for q in range(nq) for k in range(q + 1)]
    return jnp.asarray(qi, I32), jnp.asarray(ki, I32)


def _cattn(z, zt, mask, rel_bias, ta):
    bsz, seq, _ = z.shape
    qi, ki = _causal_pairs(seq // ta)
    qmap = lambda col: (lambda b, p, qi, ki: (b, qi[p], col))
    kmap = lambda col: (lambda b, p, qi, ki: (b, ki[p], col))
    dv_aug = C_HEAD_DIM + ONES_ROWS
    grid_spec = pltpu.PrefetchScalarGridSpec(
        num_scalar_prefetch=2,
        grid=(bsz, qi.shape[0]),
        in_specs=[pl.BlockSpec(memory_space=pltpu.SMEM),
                  pl.BlockSpec((1, C_WIDTH, ta), lambda b, p, qi, ki: (b, ZT_QC, qi[p])),
                  pl.BlockSpec((1, ta, C_WIDTH), kmap(Z_KC)),
                  pl.BlockSpec((1, C_WIDTH, ta), lambda b, p, qi, ki: (b, ZT_VC, ki[p])),
                  pl.BlockSpec((1, ta, ta), lambda b, p, qi, ki: (b, ki[p], qi[p]))],
        out_specs=pl.BlockSpec((1, ta, C_WIDTH), qmap(0)),
        scratch_shapes=[pltpu.VMEM((C_HEADS, 1, ta), F32),
                        pltpu.VMEM((C_HEADS, dv_aug, ta), F32),
                        pltpu.VMEM((C_HEADS, 2, BAND, BAND), F32),
                        pltpu.VMEM((ta, ta), F32)],
    )
    return pl.pallas_call(
        _cattn_body,
        grid_spec=grid_spec,
        out_shape=jax.ShapeDtypeStruct((bsz, seq, C_WIDTH), BF16),
        compiler_params=_cparams(("arbitrary", "arbitrary")),
        name="dsa_attn",
    )(qi, ki, rel_bias, zt, z, zt, mask)


def _dattn(z, zt, rel_bias, diff_lam, diff_norm_g, lambda_init, ta):
    bsz, seq, _ = z.shape
    qi, ki = _causal_pairs(seq // ta)
    n_maps = 2 * DIFF_HEADS
    dv = 2 * DIFF_HEAD_DIM
    qmap = lambda col: (lambda b, p, qi, ki: (b, qi[p], col))
    kmap = lambda col: (lambda b, p, qi, ki: (b, ki[p], col))
    grid_spec = pltpu.PrefetchScalarGridSpec(
        num_scalar_prefetch=2,
        grid=(bsz, qi.shape[0]),
        in_specs=[pl.BlockSpec(memory_space=pltpu.SMEM),
                  pl.BlockSpec((1, DIFF_W, ta), lambda b, p, qi, ki: (b, ZT_QD, qi[p])),
                  pl.BlockSpec((1, ta, DIFF_W), kmap(Z_KD)),
                  pl.BlockSpec((1, DIFF_W, ta), lambda b, p, qi, ki: (b, ZT_VD, ki[p])),
                  pl.BlockSpec(diff_lam.shape, lambda b, p, qi, ki: (0, 0)),
                  pl.BlockSpec((dv, 1), lambda b, p, qi, ki: (0, 0))],
        out_specs=pl.BlockSpec((1, ta, DIFF_W), qmap(0)),
        scratch_shapes=[pltpu.VMEM((n_maps, 1, ta), F32),
                        pltpu.VMEM((n_maps, dv + ONES_ROWS, ta), F32),
                        pltpu.VMEM((DIFF_HEADS, 2, BAND, BAND), F32),
                        pltpu.VMEM((ta, ta), F32), pltpu.VMEM((ta, ta), F32)],
    )
    return pl.pallas_call(
        functools.partial(_dattn_body, lambda_init=lambda_init),
        grid_spec=grid_spec,
        out_shape=jax.ShapeDtypeStruct((bsz, seq, DIFF_W), BF16),
        compiler_params=_cparams(("arbitrary", "arbitrary")),
        name="diff_attn",
    )(qi, ki, rel_bias, zt, z, zt, diff_lam, diff_norm_g.reshape(dv, 1))


def _out_proj_body(oc_ref, od_ref, x_ref, g1_ref, w_ref, o_ref):
    y = (jnp.dot(oc_ref[0], w_ref[0:C_WIDTH, :], preferred_element_type=F32)
         + jnp.dot(od_ref[0], w_ref[C_WIDTH:C_WIDTH + DIFF_W, :], preferred_element_type=F32))
    o_ref[0] = x_ref[0] + g1_ref[0] * y


def _out_proj(out_c, out_d, x, g1, w_out_bf16, tm):
    bsz, seq, d = x.shape
    return pl.pallas_call(
        _out_proj_body,
        grid=(bsz, seq // tm),
        in_specs=[pl.BlockSpec((1, tm, C_WIDTH), lambda b, i: (b, i, 0)),
                  pl.BlockSpec((1, tm, DIFF_W), lambda b, i: (b, i, 0)),
                  pl.BlockSpec((1, tm, d), lambda b, i: (b, i, 0)),
                  pl.BlockSpec((1, 1, d), lambda b, i: (b, 0, 0)),
                  pl.BlockSpec((C_WIDTH + DIFF_W, d), lambda b, i: (0, 0))],
        out_specs=pl.BlockSpec((1, tm, d), lambda b, i: (b, i, 0)),
        out_shape=jax.ShapeDtypeStruct((bsz, seq, d), F32),
        compiler_params=_cparams(("arbitrary", "arbitrary")),
        name="attn_out_proj",
    )(out_c, out_d, x, g1.reshape(bsz, 1, d), w_out_bf16)


def _attn_in_weights(cd_w_in):
    sizes = (C_WIDTH, C_WIDTH, C_WIDTH, IDX_HEADS * IDX_DIM, IDX_DIM, IDX_HEADS, DIFF_W, DIFF_W, DIFF_W)
    cuts = np.cumsum(sizes)[:-1]
    q_c, k_c, v_c, q_i, k_i, w_i, q_d, k_d, v_d = jnp.split(cd_w_in, cuts, axis=1)
    d = cd_w_in.shape[0]
    w = jnp.concatenate([k_c, k_d, k_i, jnp.zeros((d, LANES - IDX_DIM), cd_w_in.dtype)], axis=1)
    wt = jnp.concatenate([v_c, v_d, q_c * (C_HEAD_DIM ** -0.5 * LOG2E), q_d * (DIFF_HEAD_DIM ** -0.5 * LOG2E), q_i,
                          w_i * (IDX_DIM * IDX_HEADS) ** -0.5,
                          jnp.zeros((d, BF16_ROWS - IDX_HEADS), cd_w_in.dtype)], axis=1).T
    return w.astype(BF16), wt.astype(BF16)


def kernel(x, c, positions, rel_bias, norm_g, final_norm_g, ada_w, ada_b, ab_w_in, ab_conv_a, ab_conv_b,
           ab_conv_b_bias, ab_ln_g, ab_ln_b, ab_w_out, cd_w_in, diff_lam, diff_norm_g, cd_w_out,
           moe_wr_g, moe_br_g, moe_wr_e, moe_br_e, moe_w_gate, moe_w_up, moe_w_down):
    del positions
    bsz, seq, d = x.shape
    depth = ada_w.shape[0]
    tm = min(512, seq)
    ta = min(ATT_TILE, seq)
    assert ta % BAND == 0 and BUCKET_START[-1] < BAND and seq % ta == 0
    topk = min(TOPK_MAX, seq // 4)
    mods = _ada_mod(c, ada_w, ada_b)
    for i in range(depth):
        sh1, sc1, g1, sh2, sc2, g2 = jnp.split(mods[i], 6, axis=-1)
        j = i // 2
        if i % 2 == 0:
            z = _norm_proj(x, norm_g[i, 0], sh1, sc1, ab_w_in[j].astype(BF16), tm)
            x = _conv_mix(z, x, g1, ab_conv_a[j], ab_conv_b[j], ab_conv_b_bias[j], ab_ln_g[j], ab_ln_b[j],
                          ab_w_out[j].astype(BF16), min(512, seq))
        else:
            lambda_init = 0.8 - 0.6 * math.exp(-0.3 * i)
            w, wt = _attn_in_weights(cd_w_in[j])
            z, zt = _norm_proj(x, norm_g[i, 0], sh1, sc1, w, tm, wt)
            mask = _select(z, zt, topk)
            out_c = _cattn(z, zt, mask, rel_bias, ta)
            out_d = _dattn(z, zt, rel_bias, diff_lam[j], diff_norm_g[j], lambda_init, ta)
            x = _out_proj(out_c, out_d, x, g1, cd_w_out[j].astype(BF16), tm)
        x = _hier_moe(x, norm_g[i, 1], sh2, sc2, g2, moe_wr_g[i], moe_br_g[i], moe_wr_e[i], moe_br_e[i],
                      moe_w_gate, moe_w_up, moe_w_down, i, final_norm_g, final_norm=(i == depth - 1))
    return x
```

```python
import functools
import math

import numpy as np
import jax
import jax.numpy as jnp
from jax import lax
from jax.experimental import pallas as pl
from jax.experimental.pallas import tpu as pltpu

F32 = jnp.float32
BF16 = jnp.bfloat16
I32 = jnp.int32
HIGHEST = lax.Precision.HIGHEST

EPS = 1e-6
A_WIDTH = 512
A_CONV = 3
B_WIDTH = 512
B_CONV = 31
C_HEADS = 8
C_HEAD_DIM = 64
IDX_HEADS = 8
IDX_DIM = 32
TOPK_MAX = 256
DIFF_HEADS = 4
DIFF_HEAD_DIM = 64
NUM_BUCKETS = 32
MAX_DISTANCE = 128
N_GROUPS = 4
EXPERTS_PER_GROUP = 8
N_EXPERTS = N_GROUPS * EXPERTS_PER_GROUP
C_WIDTH = C_HEADS * C_HEAD_DIM
DIFF_W = DIFF_HEADS * 2 * DIFF_HEAD_DIM
LANES = 128
BF16_ROWS = 16
INT_MIN = -(2 ** 31)
LOG2E = math.log2(math.e)
VMEM_LIMIT = 56 * 1024 * 1024

Z_KC, Z_KD = 0, 1
Z_KIDX_OFF = C_WIDTH + DIFF_W
Z_COLS = Z_KIDX_OFF + LANES
ZT_VC, ZT_VD, ZT_QC, ZT_QD = 0, 1, 2, 3
ZT_QIDX_OFF = 2 * C_WIDTH + 2 * DIFF_W
ZT_QIDX_ROWS = IDX_HEADS * IDX_DIM
ZT_W_OFF = ZT_QIDX_OFF + ZT_QIDX_ROWS
ZT_ROWS = ZT_W_OFF + BF16_ROWS


def _bucket_starts():
    n = np.arange(0, 2 * MAX_DISTANCE)
    me = NUM_BUCKETS // 2
    lr = np.log(np.maximum(n, 1) / me) / math.log(MAX_DISTANCE / me)
    large = me + (lr * (NUM_BUCKETS - me)).astype(np.int64)
    b = np.where(n < me, n, np.minimum(large, NUM_BUCKETS - 1))
    return [int(n[b >= k].min()) for k in range(NUM_BUCKETS)]


BUCKET_START = _bucket_starts()


def _cparams(sem):
    return pltpu.CompilerParams(dimension_semantics=sem, vmem_limit_bytes=VMEM_LIMIT)


def _rms(x):
    return x * lax.rsqrt(jnp.mean(x * x, axis=-1, keepdims=True) + EPS)


def _sigmoid(x):
    return 1.0 / (1.0 + jnp.exp(-x))


def _dot_nt(a, b):
    return lax.dot_general(a, b, (((1,), (1,)), ((), ())), preferred_element_type=F32)


def _ada_body(c_ref, w_ref, b_ref, o_ref):
    c = c_ref[...]
    cond = c * _sigmoid(c)
    o_ref[0] = jnp.dot(cond, w_ref[0], precision=HIGHEST, preferred_element_type=F32) + b_ref[0]


def _ada_mod(c, ada_w, ada_b):
    depth, d, n6 = ada_w.shape
    bsz = c.shape[0]
    rows = 8
    c_pad = jnp.zeros((rows, d), F32).at[:bsz].set(c)
    tn = 1536
    out = pl.pallas_call(
        _ada_body,
        grid=(depth, n6 // tn),
        in_specs=[pl.BlockSpec((rows, d), lambda i, j: (0, 0)),
                  pl.BlockSpec((1, d, tn), lambda i, j: (i, 0, j)),
                  pl.BlockSpec((1, 1, tn), lambda i, j: (i, 0, j))],
        out_specs=pl.BlockSpec((1, rows, tn), lambda i, j: (i, 0, j)),
        out_shape=jax.ShapeDtypeStruct((depth, rows, n6), F32),
        compiler_params=_cparams(("arbitrary", "arbitrary")),
        name="ada_mod",
    )(c_pad, ada_w, ada_b.reshape(depth, 1, n6))
    return out[:, :bsz]


def _norm_proj_body(x_ref, g_ref, sh_ref, sc_ref, w_ref, *rest):
    y = _rms(x_ref[0]) * g_ref[...]
    h = (y * (1.0 + sc_ref[0]) + sh_ref[0]).astype(BF16)
    if len(rest) == 1:
        (o_ref,) = rest
    else:
        wt_ref, o_ref, ot_ref = rest
        ot_ref[0] = _dot_nt(wt_ref[...], h).astype(ot_ref.dtype)
    o_ref[0] = jnp.dot(h, w_ref[...], preferred_element_type=F32).astype(o_ref.dtype)


def _norm_proj(x, g, sh, sc, w_bf16, tm, wt_bf16=None):
    bsz, seq, d = x.shape
    n = w_bf16.shape[1]
    in_specs = [pl.BlockSpec((1, tm, d), lambda b, i: (b, i, 0)),
                pl.BlockSpec((1, d), lambda b, i: (0, 0)),
                pl.BlockSpec((1, 1, d), lambda b, i: (b, 0, 0)),
                pl.BlockSpec((1, 1, d), lambda b, i: (b, 0, 0)),
                pl.BlockSpec((d, n), lambda b, i: (0, 0))]
    out_specs = pl.BlockSpec((1, tm, n), lambda b, i: (b, i, 0))
    out_shape = jax.ShapeDtypeStruct((bsz, seq, n), BF16)
    args = [x, g.reshape(1, d), sh.reshape(bsz, 1, d), sc.reshape(bsz, 1, d), w_bf16]
    if wt_bf16 is not None:
        nt = wt_bf16.shape[0]
        in_specs.append(pl.BlockSpec((nt, d), lambda b, i: (0, 0)))
        out_specs = [out_specs, pl.BlockSpec((1, nt, tm), lambda b, i: (b, 0, i))]
        out_shape = [out_shape, jax.ShapeDtypeStruct((bsz, nt, seq), BF16)]
        args.append(wt_bf16)
    return pl.pallas_call(
        _norm_proj_body,
        grid=(bsz, seq // tm),
        in_specs=in_specs,
        out_specs=out_specs,
        out_shape=out_shape,
        compiler_params=_cparams(("arbitrary", "arbitrary")),
        name="norm_proj",
    )(*args)


CONV_HALO = 32
CONV_ROWS = 64


def _conv_body(z_ref, x_ref, g1_ref, ca_ref, cb_ref, cbb_ref, lng_ref, lnb_ref, wo_ref, o_ref,
               ua_scr, ub_scr, y_scr, *, tl):
    l = pl.program_id(1)

    @pl.when(l == 0)
    def _():
        ua_scr[0:CONV_HALO, :] = jnp.zeros((CONV_HALO, A_WIDTH), F32)
        ub_scr[0:CONV_HALO, :] = jnp.zeros((CONV_HALO, B_WIDTH), F32)

    a = A_WIDTH
    gate_c = z_ref[0, :, a:2 * a].astype(F32)
    x_a = z_ref[0, :, 2 * a:3 * a].astype(F32)
    ua_scr[CONV_HALO:CONV_HALO + tl, :] = gate_c * x_a
    val_b = z_ref[0, :, 3 * a:3 * a + B_WIDTH].astype(F32)
    glu = z_ref[0, :, 3 * a + B_WIDTH:3 * a + 2 * B_WIDTH].astype(F32)
    ub_scr[CONV_HALO:CONV_HALO + tl, :] = val_b * _sigmoid(glu)

    for r in range(0, tl, CONV_ROWS):
        acc_a = None
        for k in range(A_CONV):
            tap = ua_scr[CONV_HALO + r - (A_CONV - 1) + k:CONV_HALO + r - (A_CONV - 1) + k + CONV_ROWS, :]
            term = tap * ca_ref[k:k + 1, :]
            acc_a = term if acc_a is None else acc_a + term
        gate_b = z_ref[0, r:r + CONV_ROWS, 0:a].astype(F32)
        y_scr[r:r + CONV_ROWS, 0:a] = (gate_b * acc_a).astype(BF16)

        win = CONV_HALO + r - 8
        acc_b = None
        for b in range(8):
            phase = None
            for back in range(b, B_CONV, 8):
                tap = ub_scr[win - (back - b):win - (back - b) + CONV_ROWS + 8, :]
                term = tap * cb_ref[B_CONV - 1 - back:B_CONV - back, :]
                phase = term if phase is None else phase + term
            piece = phase[8 - b:8 - b + CONV_ROWS]
            acc_b = piece if acc_b is None else acc_b + piece
        u = acc_b + cbb_ref[...]
        mu = jnp.mean(u, axis=-1, keepdims=True)
        uc = u - mu
        var = jnp.mean(uc * uc, axis=-1, keepdims=True)
        v = uc * lax.rsqrt(var + EPS) * lng_ref[...] + lnb_ref[...]
        y_scr[r:r + CONV_ROWS, a:a + B_WIDTH] = (v * _sigmoid(v)).astype(BF16)

    ua_scr[0:CONV_HALO, :] = ua_scr[tl:tl + CONV_HALO, :]
    ub_scr[0:CONV_HALO, :] = ub_scr[tl:tl + CONV_HALO, :]
    y = jnp.dot(y_scr[...], wo_ref[...], preferred_element_type=F32)
    o_ref[0] = x_ref[0] + g1_ref[0] * y


def _conv_mix(z, x, g1, conv_a, conv_b, conv_b_bias, ln_g, ln_b, w_out_bf16, tl):
    bsz, seq, d = x.shape
    nz = z.shape[-1]
    wide = A_WIDTH + B_WIDTH
    full = lambda shape: pl.BlockSpec(shape, lambda b, l: (0,) * len(shape))
    return pl.pallas_call(
        functools.partial(_conv_body, tl=tl),
        grid=(bsz, seq // tl),
        in_specs=[pl.BlockSpec((1, tl, nz), lambda b, l: (b, l, 0)),
                  pl.BlockSpec((1, tl, d), lambda b, l: (b, l, 0)),
                  pl.BlockSpec((1, 1, d), lambda b, l: (b, 0, 0)),
                  full((A_CONV, A_WIDTH)), full((B_CONV, B_WIDTH)), full((1, B_WIDTH)),
                  full((1, B_WIDTH)), full((1, B_WIDTH)), full((wide, d))],
        out_specs=pl.BlockSpec((1, tl, d), lambda b, l: (b, l, 0)),
        out_shape=jax.ShapeDtypeStruct((bsz, seq, d), F32),
        scratch_shapes=[pltpu.VMEM((CONV_HALO + tl, A_WIDTH), F32),
                        pltpu.VMEM((CONV_HALO + tl, B_WIDTH), F32),
                        pltpu.VMEM((tl, wide), BF16)],
        compiler_params=_cparams(("arbitrary", "arbitrary")),
        name="conv_mix",
    )(z, x, g1.reshape(bsz, 1, d), conv_a, conv_b, conv_b_bias.reshape(1, -1), ln_g.reshape(1, -1),
      ln_b.reshape(1, -1), w_out_bf16)


MOE_ROWS = 512
ROUTER_ROWS = 512
META_GATE0, META_GATE1, META_POS0, META_POS1 = range(4)
SEG_LEN, SEG_BASE, SEG_OFF = range(3)
SEG_ALIGN = 8
SEG_SIZES = tuple(2 ** b for b in range(10, 2, -1))
GROUP_LANE0 = N_EXPERTS


def _router_body(x_ref, g_ref, sh_ref, sc_ref, wr_ref, br_ref, tri_ref, upper_ref, h_ref, meta_ref, post_ref, seg_ref,
                 cnt_ref, base_scr):
    @pl.when(pl.program_id(0) == 0)
    def _():
        base_scr[...] = jnp.zeros_like(base_scr)

    h = _rms(x_ref[...]) * g_ref[...]
    h = h * (1.0 + sc_ref[0]) + sh_ref[0]
    h_ref[...] = h.astype(h_ref.dtype)
    h_hi = h.astype(BF16)
    h_lo = (h - h_hi.astype(F32)).astype(BF16)
    logits = (jnp.dot(h_hi, wr_ref[0], preferred_element_type=F32) + jnp.dot(h_hi, wr_ref[1], preferred_element_type=F32)
              + jnp.dot(h_lo, wr_ref[0], preferred_element_type=F32) + br_ref[...])
    tr = logits.shape[0]
    lane = lax.broadcasted_iota(I32, (tr, LANES), 1)
    lane_f = lane.astype(F32)
    neg = jnp.float32(-jnp.inf)
    big = jnp.float32(1e9)

    is_group = (lane >= GROUP_LANE0) & (lane < GROUP_LANE0 + N_GROUPS)
    glog = jnp.where(is_group, logits, neg)
    gmax = jnp.max(glog, axis=1, keepdims=True)
    p_top = 1.0 / jnp.sum(jnp.exp(glog - gmax), axis=1, keepdims=True)
    g_sel = jnp.min(jnp.where(glog == gmax, lane_f, big), axis=1, keepdims=True) - GROUP_LANE0
    lo = g_sel * EXPERTS_PER_GROUP
    in_group = (lane_f >= lo) & (lane_f < lo + EXPERTS_PER_GROUP)
    f1 = jnp.where(in_group, logits, neg)
    v1 = jnp.max(f1, axis=1, keepdims=True)
    i1 = jnp.min(jnp.where(f1 == v1, lane_f, big), axis=1, keepdims=True)
    f2 = jnp.where(lane_f == i1, neg, f1)
    v2 = jnp.max(f2, axis=1, keepdims=True)
    i2 = jnp.min(jnp.where(f2 == v2, lane_f, big), axis=1, keepdims=True)
    a = jnp.exp(v2 - v1)
    w1 = 1.0 / (1.0 + a)
    gate0 = p_top * w1
    gate1 = p_top * (a * w1)

    oh0 = lane_f == i1
    oh1 = lane_f == i2
    ind0 = jnp.where(oh0, 1.0, 0.0)
    ind1 = jnp.where(oh1, 1.0, 0.0)
    pre0 = jnp.dot(tri_ref[...], ind0.astype(BF16), preferred_element_type=F32)
    pre1 = jnp.dot(tri_ref[...], ind1.astype(BF16), preferred_element_type=F32)
    tot0 = jnp.sum(ind0, axis=0, keepdims=True)
    tot1 = jnp.sum(ind1, axis=0, keepdims=True)
    seg_len = jnp.floor((tot0 + tot1 + (SEG_ALIGN - 1)) * (1.0 / SEG_ALIGN)) * SEG_ALIGN
    seg_off = jnp.dot(jnp.broadcast_to(seg_len, (8, LANES)), upper_ref[...], precision=HIGHEST,
                      preferred_element_type=F32)[0:1]
    pos0 = jnp.sum(jnp.where(oh0, seg_off + pre0, 0.0), axis=1, keepdims=True)
    pos1 = jnp.sum(jnp.where(oh1, seg_off + tot0 + pre1, 0.0), axis=1, keepdims=True)
    base = base_scr[...]
    new_base = base + seg_len
    base_scr[...] = new_base
    cnt_ref[...] = new_base

    meta = jnp.zeros((tr, LANES), F32)
    for col, val in ((META_GATE0, gate0), (META_GATE1, gate1), (META_POS0, pos0), (META_POS1, pos1)):
        meta = jnp.where(lane == col, val, meta)
    meta_ref[...] = meta
    post_ref[0] = meta.T[0:8]
    srow = lax.broadcasted_iota(I32, (8, LANES), 0)
    seg = jnp.where(srow == SEG_LEN, seg_len, jnp.where(srow == SEG_BASE, base, jnp.where(srow == SEG_OFF, seg_off, 0.0)))
    seg_ref[0] = seg.astype(I32)


def _moe_router(x2, g, sh, sc, wr_g, br_g, wr_e, br_e, seq):
    n_tok, d = x2.shape
    bsz = n_tok // seq
    tr = min(ROUTER_ROWS, seq)
    steps_per_batch = seq // tr
    wr = jnp.zeros((d, LANES), F32).at[:, :N_EXPERTS].set(wr_e).at[:, GROUP_LANE0:GROUP_LANE0 + N_GROUPS].set(wr_g)
    wr_hi = wr.astype(BF16)
    wr = jnp.stack([wr_hi, (wr - wr_hi.astype(F32)).astype(BF16)])
    br = jnp.zeros((1, LANES), F32).at[0, :N_EXPERTS].set(br_e).at[0, GROUP_LANE0:GROUP_LANE0 + N_GROUPS].set(br_g)
    tri = jnp.tril(jnp.ones((tr, tr), BF16), -1)
    upper = jnp.triu(jnp.ones((LANES, LANES), F32), 1)
    nb = n_tok // tr
    full = lambda shape: pl.BlockSpec(shape, lambda i: (0,) * len(shape))
    return pl.pallas_call(
        _router_body,
        grid=(nb,),
        in_specs=[pl.BlockSpec((tr, d), lambda i: (i, 0)),
                  full((1, d)),
                  pl.BlockSpec((1, 1, d), lambda i: (i // steps_per_batch, 0, 0)),
                  pl.BlockSpec((1, 1, d), lambda i: (i // steps_per_batch, 0, 0)),
                  full((2, d, LANES)), full((1, LANES)), full((tr, tr)), full((LANES, LANES))],
        out_specs=[pl.BlockSpec((tr, d), lambda i: (i, 0)),
                   pl.BlockSpec((tr, LANES), lambda i: (i, 0)),
                   pl.BlockSpec((1, 8, tr), lambda i: (i, 0, 0)),
                   pl.BlockSpec((1, 8, LANES), lambda i: (i, 0, 0)),
                   full((1, LANES))],
        out_shape=[jax.ShapeDtypeStruct((n_tok, d), BF16),
                   jax.ShapeDtypeStruct((n_tok, LANES), F32),
                   jax.ShapeDtypeStruct((nb, 8, tr), F32),
                   jax.ShapeDtypeStruct((nb, 8, LANES), I32),
                   jax.ShapeDtypeStruct((1, LANES), F32)],
        scratch_shapes=[pltpu.VMEM((1, LANES), F32)],
        compiler_params=_cparams(("arbitrary",)),
        name="moe_router",
    )(x2, g.reshape(1, d), sh.reshape(bsz, 1, d), sc.reshape(bsz, 1, d), wr, br, tri, upper)


def _segment_copies(seg_ref, starts_ref, make_copy, action):
    def per_expert(e, carry):
        length = seg_ref[0, SEG_LEN, e]
        local = seg_ref[0, SEG_OFF, e]
        glob = starts_ref[e] + seg_ref[0, SEG_BASE, e]
        for size in SEG_SIZES:
            hit = (length & size) != 0

            @pl.when(hit)
            def _(local=local, glob=glob, size=size):
                cp = make_copy(pl.multiple_of(local, SEG_ALIGN), pl.multiple_of(glob, SEG_ALIGN), size)
                cp.start() if action == "start" else cp.wait()

            step = jnp.where(hit, size, 0)
            local = local + step
            glob = glob + step
        return carry

    lax.fori_loop(0, N_EXPERTS, per_expert, 0)


def _local_rows(tr):
    return 2 * tr + N_EXPERTS * SEG_ALIGN


def _row_words(d):
    return d // 2 + LANES


def _dispatch_body(starts_ref, seg_ref, prev_seg_ref, h_ref, post_ref, xr_ref, xs_scr, zero_scr, sems, *, tr):
    i = pl.program_id(0)
    slot = i % 2
    lb = xs_scr.shape[1]
    sem = sems.at[0]

    def make_copy_of(s):
        def make_copy(local, glob, size):
            return pltpu.make_async_copy(xs_scr.at[s, pl.ds(local, size)], xr_ref.at[pl.ds(glob, size)], sems.at[s])
        return make_copy

    rows = lax.broadcasted_iota(I32, (lb, tr), 0)
    first = rows == post_ref[0, META_POS0:META_POS0 + 1, :].astype(I32)
    second = rows == post_ref[0, META_POS1:META_POS1 + 1, :].astype(I32)
    x = jnp.dot(jnp.where(first | second, 1.0, 0.0).astype(BF16), h_ref[...], preferred_element_type=F32)
    bits = lax.bitcast_convert_type(x, I32)
    half = x.shape[1] // 2
    xs_scr[slot, :, 0:half] = lax.shift_right_logical(bits[:, 0:half], 16) | (bits[:, half:] & jnp.int32(-65536))
    gate = jnp.sum(jnp.where(first, post_ref[0, META_GATE0:META_GATE0 + 1, :], 0.0)
                   + jnp.where(second, post_ref[0, META_GATE1:META_GATE1 + 1, :], 0.0), axis=1, keepdims=True)
    xs_scr[slot, :, half:half + LANES] = jnp.broadcast_to(lax.bitcast_convert_type(gate, I32), (lb, LANES))
    _segment_copies(seg_ref, starts_ref, make_copy_of(slot), "start")

    @pl.when(i > 0)
    def _():
        _segment_copies(prev_seg_ref, starts_ref, make_copy_of(1 - slot), "wait")

    @pl.when(i == pl.num_programs(0) - 1)
    def _():
        _segment_copies(seg_ref, starts_ref, make_copy_of(slot), "wait")
        zero_scr[...] = jnp.zeros_like(zero_scr)
        total = starts_ref[N_EXPERTS]
        tail = (-total) & (MOE_ROWS - 1)
        for action in ("start", "wait"):
            row = total
            for size in SEG_SIZES:
                if size >= MOE_ROWS:
                    continue
                hit = (tail & size) != 0

                @pl.when(hit)
                def _(row=row, size=size, action=action):
                    cp = pltpu.make_async_copy(zero_scr.at[pl.ds(0, size)],
                                               xr_ref.at[pl.ds(pl.multiple_of(row, SEG_ALIGN), size)], sem)
                    cp.start() if action == "start" else cp.wait()

                row = row + jnp.where(hit, size, 0)

        def free_block(b):
            return pltpu.make_async_copy(
                zero_scr, xr_ref.at[pl.ds(pl.multiple_of(b * MOE_ROWS, MOE_ROWS), MOE_ROWS)], sem)

        first_free = (total + MOE_ROWS - 1) // MOE_ROWS
        n_blocks = xr_ref.shape[0] // MOE_ROWS
        lax.fori_loop(first_free, n_blocks, lambda b, c: (free_block(b).start(), c)[1], 0)
        lax.fori_loop(first_free, n_blocks, lambda b, c: (free_block(b).wait(), c)[1], 0)


def _moe_dispatch(h2, post, seg, starts, n_rows, tr):
    n_tok, d = h2.shape
    grid_spec = pltpu.PrefetchScalarGridSpec(
        num_scalar_prefetch=1,
        grid=(n_tok // tr,),
        in_specs=[pl.BlockSpec((1, 8, LANES), lambda i, s: (i, 0, 0), memory_space=pltpu.SMEM),
                  pl.BlockSpec((1, 8, LANES), lambda i, s: (jnp.maximum(i - 1, 0), 0, 0), memory_space=pltpu.SMEM),
                  pl.BlockSpec((tr, d), lambda i, s: (i, 0)),
                  pl.BlockSpec((1, 8, tr), lambda i, s: (i, 0, 0))],
        out_specs=pl.BlockSpec(memory_space=pl.ANY),
        scratch_shapes=[pltpu.VMEM((2, _local_rows(tr), _row_words(d)), I32), pltpu.VMEM((MOE_ROWS, _row_words(d)), I32),
                        pltpu.SemaphoreType.DMA((2,))],
    )
    return pl.pallas_call(
        functools.partial(_dispatch_body, tr=tr),
        grid_spec=grid_spec,
        out_shape=jax.ShapeDtypeStruct((n_rows, _row_words(d)), I32),
        compiler_params=_cparams(("arbitrary",)),
        name="moe_dispatch",
    )(starts, seg, seg, h2, post)


def _expert_body(pb_ref, pe_ref, plo_ref, phi_ref, x_ref, wg_ref, wu_ref, wd_ref, o_ref, wg_s, wu_s, wd_s):
    p = pl.program_id(0)
    prev = jnp.maximum(p - 1, 0)
    new_expert = (p == 0) | (pe_ref[p] != pe_ref[prev])
    first = (p == 0) | (pb_ref[p] != pb_ref[prev])

    @pl.when(new_expert)
    def _():
        wg_s[...] = wg_ref[0, 0].astype(BF16)
        wu_s[...] = wu_ref[0, 0].astype(BF16)
        wd_s[...] = wd_ref[0, 0].astype(BF16)

    def rows_of_expert():
        half = wg_s.shape[0] // 2
        words = x_ref[:, 0:half]
        x_lo = lax.bitcast_convert_type(words << 16, F32).astype(BF16)
        x_hi = lax.bitcast_convert_type(words & jnp.int32(-65536), F32).astype(BF16)
        gate = lax.bitcast_convert_type(x_ref[:, half:half + 1], F32)
        gt = (jnp.dot(x_lo, wg_s[0:half], preferred_element_type=F32)
              + jnp.dot(x_hi, wg_s[half:], preferred_element_type=F32))
        up = (jnp.dot(x_lo, wu_s[0:half], preferred_element_type=F32)
              + jnp.dot(x_hi, wu_s[half:], preferred_element_type=F32))
        act = (gt * _sigmoid(gt)) * up
        y = jnp.dot(act.astype(BF16), wd_s[...], preferred_element_type=F32) * gate
        rows = lax.broadcasted_iota(I32, (y.shape[0], 1), 0)
        return jnp.where((rows >= plo_ref[p]) & (rows < phi_ref[p]), y, 0.0)

    nonempty = phi_ref[p] > plo_ref[p]

    @pl.when(first & nonempty)
    def _():
        o_ref[...] = rows_of_expert()

    @pl.when(first & jnp.logical_not(nonempty))
    def _():
        o_ref[...] = jnp.zeros_like(o_ref)

    @pl.when(jnp.logical_not(first) & nonempty)
    def _():
        o_ref[...] += rows_of_expert()


def _moe_experts(x_rows, pairs, w_gate, w_up, w_down, layer):
    n_rows = x_rows.shape[0]
    d, de = w_gate.shape[-2:]
    n_pairs = pairs[0].shape[0]
    grid_spec = pltpu.PrefetchScalarGridSpec(
        num_scalar_prefetch=4,
        grid=(n_pairs,),
        in_specs=[pl.BlockSpec((MOE_ROWS, _row_words(d)), lambda p, pb, pe, lo, hi: (pb[p], 0)),
                  pl.BlockSpec((1, 1, d, de), lambda p, pb, pe, lo, hi: (layer, pe[p], 0, 0)),
                  pl.BlockSpec((1, 1, d, de), lambda p, pb, pe, lo, hi: (layer, pe[p], 0, 0)),
                  pl.BlockSpec((1, 1, de, d), lambda p, pb, pe, lo, hi: (layer, pe[p], 0, 0))],
        out_specs=pl.BlockSpec((MOE_ROWS, d), lambda p, pb, pe, lo, hi: (pb[p], 0)),
        scratch_shapes=[pltpu.VMEM((d, de), BF16), pltpu.VMEM((d, de), BF16), pltpu.VMEM((de, d), BF16)],
    )
    return pl.pallas_call(
        _expert_body,
        grid_spec=grid_spec,
        out_shape=jax.ShapeDtypeStruct((n_rows, d), F32),
        compiler_params=_cparams(("arbitrary",)),
        name="moe_experts",
    )(*pairs, x_rows, w_gate, w_up, w_down)


def _combine_body(starts_ref, seg_ref, next_seg_ref, y_ref, x_ref, meta_ref, g2_ref, fg_ref, o_ref, ys_scr, sems, *,
                  final_norm):
    i = pl.program_id(0)
    slot = i % 2

    def make_copy_of(s):
        def make_copy(local, glob, size):
            return pltpu.make_async_copy(y_ref.at[pl.ds(glob, size)], ys_scr.at[s, pl.ds(local, size)], sems.at[s])
        return make_copy

    @pl.when(i == 0)
    def _():
        ys_scr[...] = jnp.zeros_like(ys_scr)
        _segment_copies(seg_ref, starts_ref, make_copy_of(slot), "start")

    @pl.when(i + 1 < pl.num_programs(0))
    def _():
        _segment_copies(next_seg_ref, starts_ref, make_copy_of(1 - slot), "start")

    _segment_copies(seg_ref, starts_ref, make_copy_of(slot), "wait")

    meta = meta_ref[...]
    tr = meta.shape[0]
    y = ys_scr[slot].astype(BF16)
    cols = lax.broadcasted_iota(I32, (tr, ys_scr.shape[1]), 1)

    pick = ((cols == meta[:, META_POS0:META_POS0 + 1].astype(I32))
            | (cols == meta[:, META_POS1:META_POS1 + 1].astype(I32)))
    moe = jnp.dot(jnp.where(pick, 1.0, 0.0).astype(BF16), y, preferred_element_type=F32)
    xn = x_ref[...] + g2_ref[0] * moe
    if final_norm:
        xn = _rms(xn) * fg_ref[...]
    o_ref[...] = xn


def _moe_combine(y_rows, seg, starts, x2, meta, g2, final_g, seq, tr, final_norm):
    n_tok, d = x2.shape
    bsz = n_tok // seq
    steps_per_batch = seq // tr
    nb = n_tok // tr
    grid_spec = pltpu.PrefetchScalarGridSpec(
        num_scalar_prefetch=1,
        grid=(nb,),
        in_specs=[pl.BlockSpec((1, 8, LANES), lambda i, s: (i, 0, 0), memory_space=pltpu.SMEM),
                  pl.BlockSpec((1, 8, LANES), lambda i, s: (jnp.minimum(i + 1, nb - 1), 0, 0), memory_space=pltpu.SMEM),
                  pl.BlockSpec(memory_space=pl.ANY),
                  pl.BlockSpec((tr, d), lambda i, s: (i, 0)),
                  pl.BlockSpec((tr, LANES), lambda i, s: (i, 0)),
                  pl.BlockSpec((1, 1, d), lambda i, s: (i // steps_per_batch, 0, 0)),
                  pl.BlockSpec((1, d), lambda i, s: (0, 0))],
        out_specs=pl.BlockSpec((tr, d), lambda i, s: (i, 0)),
        scratch_shapes=[pltpu.VMEM((2, _local_rows(tr), d), F32), pltpu.SemaphoreType.DMA((2,))],
    )
    return pl.pallas_call(
        functools.partial(_combine_body, final_norm=final_norm),
        grid_spec=grid_spec,
        out_shape=jax.ShapeDtypeStruct((n_tok, d), F32),
        compiler_params=_cparams(("arbitrary",)),
        name="moe_combine",
    )(starts, seg, seg, y_rows, x2, meta, g2.reshape(bsz, 1, d), final_g.reshape(1, d))


def _expert_pairs(counts, n_rows):
    n_blocks = n_rows // MOE_ROWS
    n_pairs = n_blocks + N_EXPERTS
    ends = jnp.cumsum(counts)
    starts = ends - counts
    first_blk = starts // MOE_ROWS
    last_blk = (ends - 1) // MOE_ROWS
    npairs = jnp.where(counts > 0, last_blk - first_blk + 1, 0)
    pend = jnp.cumsum(npairs)
    poff = pend - npairs
    total = pend[-1]
    used_blocks = (ends[-1] + MOE_ROWS - 1) // MOE_ROWS
    p = jnp.arange(n_pairs, dtype=I32)
    p_eff = jnp.minimum(p, total - 1)
    e = jnp.minimum(jnp.sum(pend[None, :] <= p_eff[:, None], axis=1), N_EXPERTS - 1).astype(I32)
    table = jnp.stack([first_blk, poff, starts, ends], axis=1).astype(F32)
    onehot = (e[:, None] == jnp.arange(N_EXPERTS, dtype=I32)[None, :]).astype(F32)
    first_e, poff_e, start_e, end_e = jnp.dot(onehot, table, precision=HIGHEST).astype(I32).T
    valid = p < total
    blk = jnp.where(valid, first_e + p_eff - poff_e, jnp.minimum(used_blocks + p - total, n_blocks - 1)).astype(I32)
    lo = jnp.where(valid, jnp.clip(start_e - blk * MOE_ROWS, 0, MOE_ROWS), 0).astype(I32)
    hi = jnp.where(valid, jnp.clip(end_e - blk * MOE_ROWS, 0, MOE_ROWS), 0).astype(I32)
    return blk, e, lo, hi


def _hier_moe(x, g, sh, sc, g2, wr_g, br_g, wr_e, br_e, w_gate, w_up, w_down, layer, final_g, final_norm):
    bsz, seq, d = x.shape
    n_tok = bsz * seq
    tr = min(ROUTER_ROWS, seq)
    assert 2 * tr <= SEG_SIZES[0]
    x2 = x.reshape(n_tok, d)
    h2, meta, post, seg, cnt = _moe_router(x2, g, sh, sc, wr_g, br_g, wr_e, br_e, seq)
    counts = cnt[0, :N_EXPERTS].astype(I32)
    ends = jnp.cumsum(counts)
    starts = jnp.concatenate([ends - counts, ends[-1:]])
    n_rows = -(-(2 * n_tok + (n_tok // tr) * N_EXPERTS * (SEG_ALIGN - 1)) // MOE_ROWS) * MOE_ROWS
    x_rows = _moe_dispatch(h2, post, seg, starts, n_rows, tr)
    y_rows = _moe_experts(x_rows, _expert_pairs(counts, n_rows), w_gate, w_up, w_down, layer)
    out = _moe_combine(y_rows, seg, starts, x2, meta, g2, final_g, seq, tr, final_norm)
    return out.reshape(bsz, seq, d)


SEL_COLS = 512
SEL_CHUNK = 512
BIT_GROUP = 256
SEL_SWEEP = 32
ATT_TILE = 1024


def _idx_score_t(k, qt, wt):
    acc = None
    for h in range(IDX_HEADS):
        rel = jnp.dot(k, qt[h * IDX_DIM:(h + 1) * IDX_DIM], preferred_element_type=F32)
        term = jnp.maximum(rel, 0.0) * wt[h:h + 1, :]
        acc = term if acc is None else acc + term
    return acc


def _bit_transpose32(a):
    a = list(a)
    m, j = 0x0000FFFF, 16
    while j:
        k = 0
        while k < 32:
            t = (a[k] ^ lax.shift_right_logical(a[k + j], jnp.int32(j))) & jnp.int32(m)
            a[k] = a[k] ^ t
            a[k + j] = a[k + j] ^ (t << j)
            k = (k + j + 1) & ~j
        j >>= 1
        m = (m ^ (m << j)) & 0xFFFFFFFF
    return a


def _select_body(qt_ref, zk_ref, wt_ref, o_ref, planes_scr, eq_scr, gt_scr, *, topk):
    i = pl.program_id(1)
    tq = qt_ref.shape[2]
    qt = qt_ref[0]
    wt = wt_ref[0].astype(F32)
    n_chunks = ((i + 1) * tq + SEL_CHUNK - 1) // SEL_CHUNK
    krow0 = lax.broadcasted_iota(I32, (SEL_CHUNK, tq), 0)
    qcol = i * tq + lax.broadcasted_iota(I32, (SEL_CHUNK, tq), 1)
    int_min = jnp.int32(INT_MIN)

    groups = SEL_CHUNK // BIT_GROUP
    words = SEL_CHUNK // 32
    sweep_rows = min(SEL_SWEEP, eq_scr.shape[0])

    def fill(masked, c, carry):
        off = pl.multiple_of(c * SEL_CHUNK, SEL_CHUNK)
        k = zk_ref[0, pl.ds(off, SEL_CHUNK), :][:, 0:IDX_DIM]
        bits = lax.bitcast_convert_type(_idx_score_t(k, qt, wt), I32)
        key = bits ^ ((bits >> 31) | int_min)
        if masked:
            key = jnp.where(krow0 + off <= qcol, key, 0)
        for g in range(groups):
            for lt in range(0, tq, LANES):
                ku = key[g * BIT_GROUP:(g + 1) * BIT_GROUP, lt:lt + LANES]
                planes = _bit_transpose32([ku[8 * r:8 * r + 8] for r in range(32)])
                wrow = pl.multiple_of(c * words + g * 8, 8)
                for b in range(32):
                    planes_scr[b, pl.ds(wrow, 8), lt:lt + LANES] = planes[b]
        return carry

    n_plain = (i * tq + 1) // SEL_CHUNK
    lax.fori_loop(0, n_plain, functools.partial(fill, False), 0)
    lax.fori_loop(n_plain, n_chunks, functools.partial(fill, True), 0)

    n_sweep = (n_chunks * words + sweep_rows - 1) // sweep_rows

    def pad(c, carry):
        wrow = pl.multiple_of(c * words, words)
        for b in range(32):
            planes_scr[b, pl.ds(wrow, words), :] = jnp.zeros((words, tq), I32)
        return carry

    lax.fori_loop(n_chunks, n_sweep * (sweep_rows // words), pad, 0)

    def sweep(upd, cnt_plane):
        def body(sb, acc):
            r0 = pl.multiple_of(sb * sweep_rows, sweep_rows)
            eq = eq_scr[pl.ds(r0, sweep_rows), :]
            gt = gt_scr[pl.ds(r0, sweep_rows), :]
            if upd is not None:
                plane, accept = upd
                hit = eq & planes_scr[plane, pl.ds(r0, sweep_rows), :]
                gt = jnp.where(accept, gt, gt | hit)
                eq = jnp.where(accept, hit, eq ^ hit)
                eq_scr[pl.ds(r0, sweep_rows), :] = eq
                gt_scr[pl.ds(r0, sweep_rows), :] = gt
            if cnt_plane is None:
                return acc
            ones = lax.population_count(gt | (eq & planes_scr[cnt_plane, pl.ds(r0, sweep_rows), :]))
            return acc + jnp.sum(ones.reshape(sweep_rows // 8, 8, tq), axis=0)
        acc = lax.fori_loop(0, n_sweep, body, jnp.zeros((8, tq), I32))
        return jnp.sum(acc.astype(F32), axis=0, keepdims=True)

    k_f = jnp.float32(topk)
    eq_scr[...] = jnp.full(eq_scr.shape, -1, I32)
    gt_scr[...] = jnp.zeros(gt_scr.shape, I32)

    def bit_step(ib, carry):
        u, cnt = carry
        accept = cnt >= k_f
        u = jnp.where(accept, u | jnp.left_shift(jnp.int32(1), 32 - ib), u)
        return u, sweep((ib - 1, accept), ib)

    u, cnt = lax.fori_loop(1, 32, bit_step, (jnp.zeros((1, tq), I32), sweep(None, 0)))
    accept = cnt >= k_f
    u = jnp.where(accept, u | 1, u)
    sweep((31, accept), None)
    some = jnp.where(u != 0, -1, 0)

    def popcount_rows(word_of):
        def body(sb, acc):
            r0 = pl.multiple_of(sb * sweep_rows, sweep_rows)
            ones = lax.population_count(word_of(r0))
            return acc + jnp.sum(ones.reshape(sweep_rows // 8, 8, tq), axis=0)
        acc = lax.fori_loop(0, n_sweep, body, jnp.zeros((8, tq), I32))
        return jnp.sum(acc.astype(F32), axis=0, keepdims=True)

    n_gt = popcount_rows(lambda r0: gt_scr[pl.ds(r0, sweep_rows), :])
    n_eq = jnp.where(u != 0, popcount_rows(lambda r0: eq_scr[pl.ds(r0, sweep_rows), :]), 0.0)

    @pl.when(jnp.max(n_gt + n_eq) > k_f)
    def _():
        wrow = lax.broadcasted_iota(I32, (sweep_rows, tq), 0)

        def below(cap, r0):
            w = wrow + r0
            last = lax.shift_right_arithmetic(cap - (w >> 3) * BIT_GROUP - (w & 7) - 1, 3)
            clear = jnp.clip(31 - last, 0, 32)
            return jnp.where(clear >= 32, 0, jnp.left_shift(jnp.int32(-1), jnp.minimum(clear, 31)))

        need = k_f - n_gt

        def cap_step(ib, v):
            cand = v | jnp.left_shift(jnp.int32(1), 14 - ib)
            cnt = popcount_rows(lambda r0: eq_scr[pl.ds(r0, sweep_rows), :] & below(cand, r0))
            return jnp.where(cnt <= need, cand, v)

        cap = lax.fori_loop(0, 15, cap_step, jnp.zeros((1, tq), I32))
        cap = jnp.where((u != 0) & (n_gt + n_eq > k_f), cap, 2 ** 30)

        def trim(sb, carry):
            r0 = pl.multiple_of(sb * sweep_rows, sweep_rows)
            eq_scr[pl.ds(r0, sweep_rows), :] = eq_scr[pl.ds(r0, sweep_rows), :] & below(cap, r0)
            return carry

        lax.fori_loop(0, n_sweep, trim, 0)

    def emit(c, carry):
        off = pl.multiple_of(c * SEL_CHUNK, SEL_CHUNK)
        for g in range(groups):
            wrow = pl.multiple_of(c * words + g * 8, 8)
            keep = gt_scr[pl.ds(wrow, 8), :] | (eq_scr[pl.ds(wrow, 8), :] & some)
            rows = [jnp.where((lax.shift_right_logical(keep, 31 - r) & 1) != 0, 0.0, -jnp.inf) for r in range(32)]
            o_ref[0, pl.ds(pl.multiple_of(off + g * BIT_GROUP, BIT_GROUP), BIT_GROUP), :] = (
                jnp.concatenate(rows, axis=0).astype(o_ref.dtype))
        return carry

    lax.fori_loop(0, n_chunks, emit, 0)

    def blank(c, carry):
        off = pl.multiple_of(c * SEL_CHUNK, SEL_CHUNK)
        o_ref[0, pl.ds(off, SEL_CHUNK), :] = jnp.full((SEL_CHUNK, tq), -jnp.inf, o_ref.dtype)
        return carry

    lax.fori_loop(n_chunks, o_ref.shape[1] // SEL_CHUNK, blank, 0)


def _select(z, zt, topk):
    bsz, seq, _ = z.shape
    tq = min(SEL_COLS, seq)
    assert seq % SEL_CHUNK == 0 and seq % tq == 0
    return pl.pallas_call(
        functools.partial(_select_body, topk=topk),
        grid=(bsz, seq // tq),
        in_specs=[pl.BlockSpec((1, ZT_QIDX_ROWS, tq), lambda b, i: (b, ZT_QIDX_OFF // ZT_QIDX_ROWS, i)),
                  pl.BlockSpec((1, seq, LANES), lambda b, i: (b, 0, Z_KIDX_OFF // LANES)),
                  pl.BlockSpec((1, BF16_ROWS, tq), lambda b, i: (b, ZT_W_OFF // BF16_ROWS, i))],
        out_specs=pl.BlockSpec((1, seq, tq), lambda b, i: (b, 0, i)),
        out_shape=jax.ShapeDtypeStruct((bsz, seq, seq), BF16),
        scratch_shapes=[pltpu.VMEM((32, seq // 32, tq), I32),
                        pltpu.VMEM((seq // 32, tq), I32), pltpu.VMEM((seq // 32, tq), I32)],
        compiler_params=_cparams(("arbitrary", "arbitrary")),
        name="dsa_select",
    )(zt, z, zt)


BIAS_INIT_ROWS = 8
ONES_ROWS = BF16_ROWS


BAND = 128


def _init_bias_blocks(btile, bias_ref, head0, n_heads):
    ta = BAND
    col = lax.broadcasted_iota(I32, (BIAS_INIT_ROWS, ta), 1)
    row0 = lax.broadcasted_iota(I32, (BIAS_INIT_ROWS, ta), 0)

    def body(r, carry):
        off = pl.multiple_of(r * BIAS_INIT_ROWS, BIAS_INIT_ROWS)
        for kind in range(2):
            dist = col - (row0 + off) + kind * ta
            for h in range(n_heads):
                far = bias_ref[NUM_BUCKETS - 1, head0 + h]
                val = jnp.full((BIAS_INIT_ROWS, ta), (bias_ref[0, head0 + h] - far) * LOG2E, F32)
                for b in range(1, NUM_BUCKETS - 1):
                    val = jnp.where(dist >= BUCKET_START[b], (bias_ref[b, head0 + h] - far) * LOG2E, val)
                val = jnp.where(dist >= BUCKET_START[NUM_BUCKETS - 1], 0.0, val)
                btile[h, kind, pl.ds(off, BIAS_INIT_ROWS), :] = val
        return carry

    lax.fori_loop(0, ta // BIAS_INIT_ROWS, body, 0)


def _add_bias_band(s_scr, btile, h, kind, ta):
    nb = ta // BAND
    blk = lambda i: slice(i * BAND, (i + 1) * BAND)
    if kind == 0:
        for kb in range(nb):
            s_scr[blk(kb), blk(kb)] += btile[h, 0]
            if kb + 1 < nb:
                s_scr[blk(kb), blk(kb + 1)] += btile[h, 1]
    else:
        s_scr[blk(nb - 1), blk(0)] += btile[h, 1]


def _with_ones(vt):
    return jnp.concatenate([vt, jnp.ones((ONES_ROWS, vt.shape[1]), vt.dtype)], axis=0)


def _softmax_step_t(logits, v_aug, m_ref, acc_ref, idx):
    m_old = m_ref[idx]
    m_new = jnp.maximum(m_old, jnp.max(logits, axis=0, keepdims=True))
    m_safe = jnp.where(m_new == -jnp.inf, 0.0, m_new)
    p = jnp.exp2(logits - m_safe)
    alpha = jnp.exp2(m_old - m_safe)
    acc_ref[idx] = alpha * acc_ref[idx] + jnp.dot(v_aug, p.astype(BF16), preferred_element_type=F32)
    m_ref[idx] = m_new


def _reset_softmax(m_s, acc_s):
    m_s[...] = jnp.full(m_s.shape, -jnp.inf, F32)
    acc_s[...] = jnp.zeros(acc_s.shape, F32)


def _cattn_body(qi_ref, ki_ref, bias_ref, qt_ref, zk_ref, vt_ref, mask_ref, o_ref, m_s, acc_s, btile, s_scr):
    b = pl.program_id(0)
    p = pl.program_id(1)
    qi = qi_ref[p]
    ki = ki_ref[p]
    ta = zk_ref.shape[1]
    hd = C_HEAD_DIM

    @pl.when((b == 0) & (p == 0))
    def _():
        _init_bias_blocks(btile, bias_ref, 0, C_HEADS)

    @pl.when(ki == 0)
    def _():
        _reset_softmax(m_s, acc_s)

    def heads(kind):
        for h in range(C_HEADS):
            lo = h * hd
            logits = (jnp.dot(zk_ref[0, :, lo:lo + hd], qt_ref[0, lo:lo + hd, :], preferred_element_type=F32)
                      + mask_ref[0].astype(F32))
            if kind is not None:
                s_scr[...] = logits
                _add_bias_band(s_scr, btile, h, kind, ta)
                logits = s_scr[...]
            _softmax_step_t(logits, _with_ones(vt_ref[0, lo:lo + hd, :]), m_s, acc_s, h)

    for kind in (0, 1):
        pl.when(ki == qi - kind)(functools.partial(heads, kind))
    pl.when(ki < qi - 1)(functools.partial(heads, None))

    @pl.when(ki == qi)
    def _():
        outs = []
        for h in range(C_HEADS):
            a = acc_s[h]
            outs.append((a[0:hd] / a[hd:hd + 1]).T)
        o_ref[0] = jnp.concatenate(outs, axis=1).astype(o_ref.dtype)


def _dattn_body(qi_ref, ki_ref, bias_ref, qt_ref, zk_ref, vt_ref, lam_ref, ng_ref, o_ref,
                m_s, acc_s, btile, s_scr, causal_s, *, lambda_init):
    b = pl.program_id(0)
    p = pl.program_id(1)
    qi = qi_ref[p]
    ki = ki_ref[p]
    ta = zk_ref.shape[1]
    hd = DIFF_HEAD_DIM
    dv = 2 * hd

    @pl.when((b == 0) & (p == 0))
    def _():
        _init_bias_blocks(btile, bias_ref, C_HEADS, DIFF_HEADS)
        krow = lax.broadcasted_iota(I32, (ta, ta), 0)
        qcol = lax.broadcasted_iota(I32, (ta, ta), 1)
        causal_s[...] = jnp.where(krow <= qcol, 0.0, -jnp.inf)

    @pl.when(ki == 0)
    def _():
        _reset_softmax(m_s, acc_s)

    def heads(kind):
        for h in range(DIFF_HEADS):
            v_aug = _with_ones(vt_ref[0, dv * h:dv * (h + 1), :])
            for j in range(2):
                lo = (2 * h + j) * hd
                logits = jnp.dot(zk_ref[0, :, lo:lo + hd], qt_ref[0, lo:lo + hd, :], preferred_element_type=F32)
                if kind is not None:
                    s_scr[...] = logits + causal_s[...] if kind == 0 else logits
                    _add_bias_band(s_scr, btile, h, kind, ta)
                    logits = s_scr[...]
                _softmax_step_t(logits, v_aug, m_s, acc_s, 2 * h + j)

    for kind in (0, 1):
        pl.when(ki == qi - kind)(functools.partial(heads, kind))
    pl.when(ki < qi - 1)(functools.partial(heads, None))

    @pl.when(ki == qi)
    def _():
        lam_p = lam_ref[...]
        lam = (jnp.exp(jnp.sum(lam_p[0:1] * lam_p[1:2], axis=1, keepdims=True))
               - jnp.exp(jnp.sum(lam_p[2:3] * lam_p[3:4], axis=1, keepdims=True)) + lambda_init)
        outs = []
        for h in range(DIFF_HEADS):
            a1 = acc_s[2 * h]
            a2 = acc_s[2 * h + 1]
            o = a1[0:dv] / a1[dv:dv + 1] - lam * (a2[0:dv] / a2[dv:dv + 1])
            o = o * lax.rsqrt(jnp.mean(o * o, axis=0, keepdims=True) + EPS) * ng_ref[...] * (1.0 - lambda_init)
            outs.append(o.T)
        o_ref[0] = jnp.concatenate(outs, axis=1).astype(o_ref.dtype)


def _causal_pairs(nq):
    qi = [q for q in range(nq) for _ in range(q + 1)]
    ki = [k for q in range(nq) for k in range(q + 1)]
    return jnp.asarray(qi, I32), jnp.asarray(ki, I32)


def _cattn(z, zt, mask, rel_bias, ta):
    bsz, seq, _ = z.shape
    qi, ki = _causal_pairs(seq // ta)
    qmap = lambda col: (lambda b, p, qi, ki: (b, qi[p], col))
    kmap = lambda col: (lambda b, p, qi, ki: (b, ki[p], col))
    dv_aug = C_HEAD_DIM + ONES_ROWS
    grid_spec = pltpu.PrefetchScalarGridSpec(
        num_scalar_prefetch=2,
        grid=(bsz, qi.shape[0]),
        in_specs=[pl.BlockSpec(memory_space=pltpu.SMEM),
                  pl.BlockSpec((1, C_WIDTH, ta), lambda b, p, qi, ki: (b, ZT_QC, qi[p])),
                  pl.BlockSpec((1, ta, C_WIDTH), kmap(Z_KC)),
                  pl.BlockSpec((1, C_WIDTH, ta), lambda b, p, qi, ki: (b, ZT_VC, ki[p])),
                  pl.BlockSpec((1, ta, ta), lambda b, p, qi, ki: (b, ki[p], qi[p]))],
        out_specs=pl.BlockSpec((1, ta, C_WIDTH), qmap(0)),
        scratch_shapes=[pltpu.VMEM((C_HEADS, 1, ta), F32),
                        pltpu.VMEM((C_HEADS, dv_aug, ta), F32),
                        pltpu.VMEM((C_HEADS, 2, BAND, BAND), F32),
                        pltpu.VMEM((ta, ta), F32)],
    )
    return pl.pallas_call(
        _cattn_body,
        grid_spec=grid_spec,
        out_shape=jax.ShapeDtypeStruct((bsz, seq, C_WIDTH), BF16),
        compiler_params=_cparams(("arbitrary", "arbitrary")),
        name="dsa_attn",
    )(qi, ki, rel_bias, zt, z, zt, mask)


def _dattn(z, zt, rel_bias, diff_lam, diff_norm_g, lambda_init, ta):
    bsz, seq, _ = z.shape
    qi, ki = _causal_pairs(seq // ta)
    n_maps = 2 * DIFF_HEADS
    dv = 2 * DIFF_HEAD_DIM
    qmap = lambda col: (lambda b, p, qi, ki: (b, qi[p], col))
    kmap = lambda col: (lambda b, p, qi, ki: (b, ki[p], col))
    grid_spec = pltpu.PrefetchScalarGridSpec(
        num_scalar_prefetch=2,
        grid=(bsz, qi.shape[0]),
        in_specs=[pl.BlockSpec(memory_space=pltpu.SMEM),
                  pl.BlockSpec((1, DIFF_W, ta), lambda b, p, qi, ki: (b, ZT_QD, qi[p])),
                  pl.BlockSpec((1, ta, DIFF_W), kmap(Z_KD)),
                  pl.BlockSpec((1, DIFF_W, ta), lambda b, p, qi, ki: (b, ZT_VD, ki[p])),
                  pl.BlockSpec(diff_lam.shape, lambda b, p, qi, ki: (0, 0)),
                  pl.BlockSpec((dv, 1), lambda b, p, qi, ki: (0, 0))],
        out_specs=pl.BlockSpec((1, ta, DIFF_W), qmap(0)),
        scratch_shapes=[pltpu.VMEM((n_maps, 1, ta), F32),
                        pltpu.VMEM((n_maps, dv + ONES_ROWS, ta), F32),
                        pltpu.VMEM((DIFF_HEADS, 2, BAND, BAND), F32),
                        pltpu.VMEM((ta, ta), F32), pltpu.VMEM((ta, ta), F32)],
    )
    return pl.pallas_call(
        functools.partial(_dattn_body, lambda_init=lambda_init),
        grid_spec=grid_spec,
        out_shape=jax.ShapeDtypeStruct((bsz, seq, DIFF_W), BF16),
        compiler_params=_cparams(("arbitrary", "arbitrary")),
        name="diff_attn",
    )(qi, ki, rel_bias, zt, z, zt, diff_lam, diff_norm_g.reshape(dv, 1))


def _out_proj_body(oc_ref, od_ref, x_ref, g1_ref, w_ref, o_ref):
    y = (jnp.dot(oc_ref[0], w_ref[0:C_WIDTH, :], preferred_element_type=F32)
         + jnp.dot(od_ref[0], w_ref[C_WIDTH:C_WIDTH + DIFF_W, :], preferred_element_type=F32))
    o_ref[0] = x_ref[0] + g1_ref[0] * y


def _out_proj(out_c, out_d, x, g1, w_out_bf16, tm):
    bsz, seq, d = x.shape
    return pl.pallas_call(
        _out_proj_body,
        grid=(bsz, seq // tm),
        in_specs=[pl.BlockSpec((1, tm, C_WIDTH), lambda b, i: (b, i, 0)),
                  pl.BlockSpec((1, tm, DIFF_W), lambda b, i: (b, i, 0)),
                  pl.BlockSpec((1, tm, d), lambda b, i: (b, i, 0)),
                  pl.BlockSpec((1, 1, d), lambda b, i: (b, 0, 0)),
                  pl.BlockSpec((C_WIDTH + DIFF_W, d), lambda b, i: (0, 0))],
        out_specs=pl.BlockSpec((1, tm, d), lambda b, i: (b, i, 0)),
        out_shape=jax.ShapeDtypeStruct((bsz, seq, d), F32),
        compiler_params=_cparams(("arbitrary", "arbitrary")),
        name="attn_out_proj",
    )(out_c, out_d, x, g1.reshape(bsz, 1, d), w_out_bf16)


def _attn_in_weights(cd_w_in):
    sizes = (C_WIDTH, C_WIDTH, C_WIDTH, IDX_HEADS * IDX_DIM, IDX_DIM, IDX_HEADS, DIFF_W, DIFF_W, DIFF_W)
    cuts = np.cumsum(sizes)[:-1]
    q_c, k_c, v_c, q_i, k_i, w_i, q_d, k_d, v_d = jnp.split(cd_w_in, cuts, axis=1)
    d = cd_w_in.shape[0]
    w = jnp.concatenate([k_c, k_d, k_i, jnp.zeros((d, LANES - IDX_DIM), cd_w_in.dtype)], axis=1)
    wt = jnp.concatenate([v_c, v_d, q_c * (C_HEAD_DIM ** -0.5 * LOG2E), q_d * (DIFF_HEAD_DIM ** -0.5 * LOG2E), q_i,
                          w_i * (IDX_DIM * IDX_HEADS) ** -0.5,
                          jnp.zeros((d, BF16_ROWS - IDX_HEADS), cd_w_in.dtype)], axis=1).T
    return w.astype(BF16), wt.astype(BF16)


def kernel(x, c, positions, rel_bias, norm_g, final_norm_g, ada_w, ada_b, ab_w_in, ab_conv_a, ab_conv_b,
           ab_conv_b_bias, ab_ln_g, ab_ln_b, ab_w_out, cd_w_in, diff_lam, diff_norm_g, cd_w_out,
           moe_wr_g, moe_br_g, moe_wr_e, moe_br_e, moe_w_gate, moe_w_up, moe_w_down):
    del positions
    bsz, seq, d = x.shape
    depth = ada_w.shape[0]
    tm = min(512, seq)
    ta = min(ATT_TILE, seq)
    assert ta % BAND == 0 and BUCKET_START[-1] < BAND and seq % ta == 0
    topk = min(TOPK_MAX, seq // 4)
    mods = _ada_mod(c, ada_w, ada_b)
    for i in range(depth):
        sh1, sc1, g1, sh2, sc2, g2 = jnp.split(mods[i], 6, axis=-1)
        j = i // 2
        if i % 2 == 0:
            z = _norm_proj(x, norm_g[i, 0], sh1, sc1, ab_w_in[j].astype(BF16), tm)
            x = _conv_mix(z, x, g1, ab_conv_a[j], ab_conv_b[j], ab_conv_b_bias[j], ab_ln_g[j], ab_ln_b[j],
                          ab_w_out[j].astype(BF16), min(512, seq))
        else:
            lambda_init = 0.8 - 0.6 * math.exp(-0.3 * i)
            w, wt = _attn_in_weights(cd_w_in[j])
            z, zt = _norm_proj(x, norm_g[i, 0], sh1, sc1, w, tm, wt)
            mask = _select(z, zt, topk)
            out_c = _cattn(z, zt, mask, rel_bias, ta)
            out_d = _dattn(z, zt, rel_bias, diff_lam[j], diff_norm_g[j], lambda_init, ta)
            x = _out_proj(out_c, out_d, x, g1, cd_w_out[j].astype(BF16), tm)
        x = _hier_moe(x, norm_g[i, 1], sh2, sc2, g2, moe_wr_g[i], moe_br_g[i], moe_wr_e[i], moe_br_e[i],
                      moe_w_gate, moe_w_up, moe_w_down, i, final_norm_g, final_norm=(i == depth - 1))
    return x
```

```python
import functools
import math

import numpy as np
import jax
import jax.numpy as jnp
from jax import lax
from jax.experimental import pallas as pl
from jax.experimental.pallas import tpu as pltpu

F32 = jnp.float32
BF16 = jnp.bfloat16
I32 = jnp.int32
HIGHEST = lax.Precision.HIGHEST

EPS = 1e-6
A_WIDTH = 512
A_CONV = 3
B_WIDTH = 512
B_CONV = 31
C_HEADS = 8
C_HEAD_DIM = 64
IDX_HEADS = 8
IDX_DIM = 32
TOPK_MAX = 256
DIFF_HEADS = 4
DIFF_HEAD_DIM = 64
NUM_BUCKETS = 32
MAX_DISTANCE = 128
N_GROUPS = 4
EXPERTS_PER_GROUP = 8
N_EXPERTS = N_GROUPS * EXPERTS_PER_GROUP
C_WIDTH = C_HEADS * C_HEAD_DIM
DIFF_W = DIFF_HEADS * 2 * DIFF_HEAD_DIM
LANES = 128
BF16_ROWS = 16
INT_MIN = -(2 ** 31)
LOG2E = math.log2(math.e)
VMEM_LIMIT = 56 * 1024 * 1024

Z_KC, Z_KD = 0, 1
Z_KIDX_OFF = C_WIDTH + DIFF_W
Z_COLS = Z_KIDX_OFF + LANES
ZT_VC, ZT_VD, ZT_QC, ZT_QD = 0, 1, 2, 3
ZT_QIDX_OFF = 2 * C_WIDTH + 2 * DIFF_W
ZT_QIDX_ROWS = IDX_HEADS * IDX_DIM
ZT_W_OFF = ZT_QIDX_OFF + ZT_QIDX_ROWS
ZT_ROWS = ZT_W_OFF + BF16_ROWS


def _bucket_starts():
    n = np.arange(0, 2 * MAX_DISTANCE)
    me = NUM_BUCKETS // 2
    lr = np.log(np.maximum(n, 1) / me) / math.log(MAX_DISTANCE / me)
    large = me + (lr * (NUM_BUCKETS - me)).astype(np.int64)
    b = np.where(n < me, n, np.minimum(large, NUM_BUCKETS - 1))
    return [int(n[b >= k].min()) for k in range(NUM_BUCKETS)]


BUCKET_START = _bucket_starts()


def _cparams(sem):
    return pltpu.CompilerParams(dimension_semantics=sem, vmem_limit_bytes=VMEM_LIMIT)


def _rms(x):
    return x * lax.rsqrt(jnp.mean(x * x, axis=-1, keepdims=True) + EPS)


def _sigmoid(x):
    return 1.0 / (1.0 + jnp.exp(-x))


def _dot_nt(a, b):
    return lax.dot_general(a, b, (((1,), (1,)), ((), ())), preferred_element_type=F32)


def _ada_body(c_ref, w_ref, b_ref, o_ref):
    c = c_ref[...]
    cond = c * _sigmoid(c)
    o_ref[0] = jnp.dot(cond, w_ref[0], precision=HIGHEST, preferred_element_type=F32) + b_ref[0]


def _ada_mod(c, ada_w, ada_b):
    depth, d, n6 = ada_w.shape
    bsz = c.shape[0]
    rows = 8
    c_pad = jnp.zeros((rows, d), F32).at[:bsz].set(c)
    tn = 1536
    out = pl.pallas_call(
        _ada_body,
        grid=(depth, n6 // tn),
        in_specs=[pl.BlockSpec((rows, d), lambda i, j: (0, 0)),
                  pl.BlockSpec((1, d, tn), lambda i, j: (i, 0, j)),
                  pl.BlockSpec((1, 1, tn), lambda i, j: (i, 0, j))],
        out_specs=pl.BlockSpec((1, rows, tn), lambda i, j: (i, 0, j)),
        out_shape=jax.ShapeDtypeStruct((depth, rows, n6), F32),
        compiler_params=_cparams(("arbitrary", "arbitrary")),
        name="ada_mod",
    )(c_pad, ada_w, ada_b.reshape(depth, 1, n6))
    return out[:, :bsz]


def _norm_proj_body(x_ref, g_ref, sh_ref, sc_ref, w_ref, *rest):
    y = _rms(x_ref[0]) * g_ref[...]
    h = (y * (1.0 + sc_ref[0]) + sh_ref[0]).astype(BF16)
    if len(rest) == 1:
        (o_ref,) = rest
    else:
        wt_ref, o_ref, ot_ref = rest
        ot_ref[0] = _dot_nt(wt_ref[...], h).astype(ot_ref.dtype)
    o_ref[0] = jnp.dot(h, w_ref[...], preferred_element_type=F32).astype(o_ref.dtype)


def _norm_proj(x, g, sh, sc, w_bf16, tm, wt_bf16=None):
    bsz, seq, d = x.shape
    n = w_bf16.shape[1]
    in_specs = [pl.BlockSpec((1, tm, d), lambda b, i: (b, i, 0)),
                pl.BlockSpec((1, d), lambda b, i: (0, 0)),
                pl.BlockSpec((1, 1, d), lambda b, i: (b, 0, 0)),
                pl.BlockSpec((1, 1, d), lambda b, i: (b, 0, 0)),
                pl.BlockSpec((d, n), lambda b, i: (0, 0))]
    out_specs = pl.BlockSpec((1, tm, n), lambda b, i: (b, i, 0))
    out_shape = jax.ShapeDtypeStruct((bsz, seq, n), BF16)
    args = [x, g.reshape(1, d), sh.reshape(bsz, 1, d), sc.reshape(bsz, 1, d), w_bf16]
    if wt_bf16 is not None:
        nt = wt_bf16.shape[0]
        in_specs.append(pl.BlockSpec((nt, d), lambda b, i: (0, 0)))
        out_specs = [out_specs, pl.BlockSpec((1, nt, tm), lambda b, i: (b, 0, i))]
        out_shape = [out_shape, jax.ShapeDtypeStruct((bsz, nt, seq), BF16)]
        args.append(wt_bf16)
    return pl.pallas_call(
        _norm_proj_body,
        grid=(bsz, seq // tm),
        in_specs=in_specs,
        out_specs=out_specs,
        out_shape=out_shape,
        compiler_params=_cparams(("arbitrary", "arbitrary")),
        name="norm_proj",
    )(*args)


CONV_HALO = 32
CONV_ROWS = 64


def _conv_body(z_ref, x_ref, g1_ref, ca_ref, cb_ref, cbb_ref, lng_ref, lnb_ref, wo_ref, o_ref,
               ua_scr, ub_scr, y_scr, *, tl):
    l = pl.program_id(1)

    @pl.when(l == 0)
    def _():
        ua_scr[0:CONV_HALO, :] = jnp.zeros((CONV_HALO, A_WIDTH), F32)
        ub_scr[0:CONV_HALO, :] = jnp.zeros((CONV_HALO, B_WIDTH), F32)

    a = A_WIDTH
    gate_c = z_ref[0, :, a:2 * a].astype(F32)
    x_a = z_ref[0, :, 2 * a:3 * a].astype(F32)
    ua_scr[CONV_HALO:CONV_HALO + tl, :] = gate_c * x_a
    val_b = z_ref[0, :, 3 * a:3 * a + B_WIDTH].astype(F32)
    glu = z_ref[0, :, 3 * a + B_WIDTH:3 * a + 2 * B_WIDTH].astype(F32)
    ub_scr[CONV_HALO:CONV_HALO + tl, :] = val_b * _sigmoid(glu)

    for r in range(0, tl, CONV_ROWS):
        acc_a = None
        for k in range(A_CONV):
            tap = ua_scr[CONV_HALO + r - (A_CONV - 1) + k:CONV_HALO + r - (A_CONV - 1) + k + CONV_ROWS, :]
            term = tap * ca_ref[k:k + 1, :]
            acc_a = term if acc_a is None else acc_a + term
        gate_b = z_ref[0, r:r + CONV_ROWS, 0:a].astype(F32)
        y_scr[r:r + CONV_ROWS, 0:a] = (gate_b * acc_a).astype(BF16)

        win = CONV_HALO + r - 8
        acc_b = None
        for b in range(8):
            phase = None
            for back in range(b, B_CONV, 8):
                tap = ub_scr[win - (back - b):win - (back - b) + CONV_ROWS + 8, :]
                term = tap * cb_ref[B_CONV - 1 - back:B_CONV - back, :]
                phase = term if phase is None else phase + term
            piece = phase[8 - b:8 - b + CONV_ROWS]
            acc_b = piece if acc_b is None else acc_b + piece
        u = acc_b + cbb_ref[...]
        mu = jnp.mean(u, axis=-1, keepdims=True)
        uc = u - mu
        var = jnp.mean(uc * uc, axis=-1, keepdims=True)
        v = uc * lax.rsqrt(var + EPS) * lng_ref[...] + lnb_ref[...]
        y_scr[r:r + CONV_ROWS, a:a + B_WIDTH] = (v * _sigmoid(v)).astype(BF16)

    ua_scr[0:CONV_HALO, :] = ua_scr[tl:tl + CONV_HALO, :]
    ub_scr[0:CONV_HALO, :] = ub_scr[tl:tl + CONV_HALO, :]
    y = jnp.dot(y_scr[...], wo_ref[...], preferred_element_type=F32)
    o_ref[0] = x_ref[0] + g1_ref[0] * y


def _conv_mix(z, x, g1, conv_a, conv_b, conv_b_bias, ln_g, ln_b, w_out_bf16, tl):
    bsz, seq, d = x.shape
    nz = z.shape[-1]
    wide = A_WIDTH + B_WIDTH
    full = lambda shape: pl.BlockSpec(shape, lambda b, l: (0,) * len(shape))
    return pl.pallas_call(
        functools.partial(_conv_body, tl=tl),
        grid=(bsz, seq // tl),
        in_specs=[pl.BlockSpec((1, tl, nz), lambda b, l: (b, l, 0)),
                  pl.BlockSpec((1, tl, d), lambda b, l: (b, l, 0)),
                  pl.BlockSpec((1, 1, d), lambda b, l: (b, 0, 0)),
                  full((A_CONV, A_WIDTH)), full((B_CONV, B_WIDTH)), full((1, B_WIDTH)),
                  full((1, B_WIDTH)), full((1, B_WIDTH)), full((wide, d))],
        out_specs=pl.BlockSpec((1, tl, d), lambda b, l: (b, l, 0)),
        out_shape=jax.ShapeDtypeStruct((bsz, seq, d), F32),
        scratch_shapes=[pltpu.VMEM((CONV_HALO + tl, A_WIDTH), F32),
                        pltpu.VMEM((CONV_HALO + tl, B_WIDTH), F32),
                        pltpu.VMEM((tl, wide), BF16)],
        compiler_params=_cparams(("arbitrary", "arbitrary")),
        name="conv_mix",
    )(z, x, g1.reshape(bsz, 1, d), conv_a, conv_b, conv_b_bias.reshape(1, -1), ln_g.reshape(1, -1),
      ln_b.reshape(1, -1), w_out_bf16)


MOE_ROWS = 512
ROUTER_ROWS = 512
META_GATE0, META_GATE1, META_POS0, META_POS1 = range(4)
SEG_LEN, SEG_BASE, SEG_OFF = range(3)
SEG_ALIGN = 8
SEG_SIZES = tuple(2 ** b for b in range(10, 2, -1))
GROUP_LANE0 = N_EXPERTS


def _router_body(x_ref, g_ref, sh_ref, sc_ref, wr_ref, br_ref, tri_ref, upper_ref, h_ref, meta_ref, post_ref, seg_ref,
                 cnt_ref, base_scr):
    @pl.when(pl.program_id(0) == 0)
    def _():
        base_scr[...] = jnp.zeros_like(base_scr)

    h = _rms(x_ref[...]) * g_ref[...]
    h = h * (1.0 + sc_ref[0]) + sh_ref[0]
    h_ref[...] = h.astype(h_ref.dtype)
    h_hi = h.astype(BF16)
    h_lo = (h - h_hi.astype(F32)).astype(BF16)
    logits = (jnp.dot(h_hi, wr_ref[0], preferred_element_type=F32) + jnp.dot(h_hi, wr_ref[1], preferred_element_type=F32)
              + jnp.dot(h_lo, wr_ref[0], preferred_element_type=F32) + br_ref[...])
    tr = logits.shape[0]
    lane = lax.broadcasted_iota(I32, (tr, LANES), 1)
    lane_f = lane.astype(F32)
    neg = jnp.float32(-jnp.inf)
    big = jnp.float32(1e9)

    is_group = (lane >= GROUP_LANE0) & (lane < GROUP_LANE0 + N_GROUPS)
    glog = jnp.where(is_group, logits, neg)
    gmax = jnp.max(glog, axis=1, keepdims=True)
    p_top = 1.0 / jnp.sum(jnp.exp(glog - gmax), axis=1, keepdims=True)
    g_sel = jnp.min(jnp.where(glog == gmax, lane_f, big), axis=1, keepdims=True) - GROUP_LANE0
    lo = g_sel * EXPERTS_PER_GROUP
    in_group = (lane_f >= lo) & (lane_f < lo + EXPERTS_PER_GROUP)
    f1 = jnp.where(in_group, logits, neg)
    v1 = jnp.max(f1, axis=1, keepdims=True)
    i1 = jnp.min(jnp.where(f1 == v1, lane_f, big), axis=1, keepdims=True)
    f2 = jnp.where(lane_f == i1, neg, f1)
    v2 = jnp.max(f2, axis=1, keepdims=True)
    i2 = jnp.min(jnp.where(f2 == v2, lane_f, big), axis=1, keepdims=True)
    a = jnp.exp(v2 - v1)
    w1 = 1.0 / (1.0 + a)
    gate0 = p_top * w1
    gate1 = p_top * (a * w1)

    oh0 = lane_f == i1
    oh1 = lane_f == i2
    ind0 = jnp.where(oh0, 1.0, 0.0)
    ind1 = jnp.where(oh1, 1.0, 0.0)
    pre0 = jnp.dot(tri_ref[...], ind0.astype(BF16), preferred_element_type=F32)
    pre1 = jnp.dot(tri_ref[...], ind1.astype(BF16), preferred_element_type=F32)
    tot0 = jnp.sum(ind0, axis=0, keepdims=True)
    tot1 = jnp.sum(ind1, axis=0, keepdims=True)
    seg_len = jnp.floor((tot0 + tot1 + (SEG_ALIGN - 1)) * (1.0 / SEG_ALIGN)) * SEG_ALIGN
    seg_off = jnp.dot(jnp.broadcast_to(seg_len, (8, LANES)), upper_ref[...], precision=HIGHEST,
                      preferred_element_type=F32)[0:1]
    pos0 = jnp.sum(jnp.where(oh0, seg_off + pre0, 0.0), axis=1, keepdims=True)
    pos1 = jnp.sum(jnp.where(oh1, seg_off + tot0 + pre1, 0.0), axis=1, keepdims=True)
    base = base_scr[...]
    new_base = base + seg_len
    base_scr[...] = new_base
    cnt_ref[...] = new_base

    meta = jnp.zeros((tr, LANES), F32)
    for col, val in ((META_GATE0, gate0), (META_GATE1, gate1), (META_POS0, pos0), (META_POS1, pos1)):
        meta = jnp.where(lane == col, val, meta)
    meta_ref[...] = meta
    post_ref[0] = meta.T[0:8]
    srow = lax.broadcasted_iota(I32, (8, LANES), 0)
    seg = jnp.where(srow == SEG_LEN, seg_len, jnp.where(srow == SEG_BASE, base, jnp.where(srow == SEG_OFF, seg_off, 0.0)))
    seg_ref[0] = seg.astype(I32)


def _moe_router(x2, g, sh, sc, wr_g, br_g, wr_e, br_e, seq):
    n_tok, d = x2.shape
    bsz = n_tok // seq
    tr = min(ROUTER_ROWS, seq)
    steps_per_batch = seq // tr
    wr = jnp.zeros((d, LANES), F32).at[:, :N_EXPERTS].set(wr_e).at[:, GROUP_LANE0:GROUP_LANE0 + N_GROUPS].set(wr_g)
    wr_hi = wr.astype(BF16)
    wr = jnp.stack([wr_hi, (wr - wr_hi.astype(F32)).astype(BF16)])
    br = jnp.zeros((1, LANES), F32).at[0, :N_EXPERTS].set(br_e).at[0, GROUP_LANE0:GROUP_LANE0 + N_GROUPS].set(br_g)
    tri = jnp.tril(jnp.ones((tr, tr), BF16), -1)
    upper = jnp.triu(jnp.ones((LANES, LANES), F32), 1)
    nb = n_tok // tr
    full = lambda shape: pl.BlockSpec(shape, lambda i: (0,) * len(shape))
    return pl.pallas_call(
        _router_body,
        grid=(nb,),
        in_specs=[pl.BlockSpec((tr, d), lambda i: (i, 0)),
                  full((1, d)),
                  pl.BlockSpec((1, 1, d), lambda i: (i // steps_per_batch, 0, 0)),
                  pl.BlockSpec((1, 1, d), lambda i: (i // steps_per_batch, 0, 0)),
                  full((2, d, LANES)), full((1, LANES)), full((tr, tr)), full((LANES, LANES))],
        out_specs=[pl.BlockSpec((tr, d), lambda i: (i, 0)),
                   pl.BlockSpec((tr, LANES), lambda i: (i, 0)),
                   pl.BlockSpec((1, 8, tr), lambda i: (i, 0, 0)),
                   pl.BlockSpec((1, 8, LANES), lambda i: (i, 0, 0)),
                   full((1, LANES))],
        out_shape=[jax.ShapeDtypeStruct((n_tok, d), BF16),
                   jax.ShapeDtypeStruct((n_tok, LANES), F32),
                   jax.ShapeDtypeStruct((nb, 8, tr), F32),
                   jax.ShapeDtypeStruct((nb, 8, LANES), I32),
                   jax.ShapeDtypeStruct((1, LANES), F32)],
        scratch_shapes=[pltpu.VMEM((1, LANES), F32)],
        compiler_params=_cparams(("arbitrary",)),
        name="moe_router",
    )(x2, g.reshape(1, d), sh.reshape(bsz, 1, d), sc.reshape(bsz, 1, d), wr, br, tri, upper)


def _segment_copies(seg_ref, starts_ref, make_copy, action):
    def per_expert(e, carry):
        length = seg_ref[0, SEG_LEN, e]
        local = seg_ref[0, SEG_OFF, e]
        glob = starts_ref[e] + seg_ref[0, SEG_BASE, e]
        for size in SEG_SIZES:
            hit = (length & size) != 0

            @pl.when(hit)
            def _(local=local, glob=glob, size=size):
                cp = make_copy(pl.multiple_of(local, SEG_ALIGN), pl.multiple_of(glob, SEG_ALIGN), size)
                cp.start() if action == "start" else cp.wait()

            step = jnp.where(hit, size, 0)
            local = local + step
            glob = glob + step
        return carry

    lax.fori_loop(0, N_EXPERTS, per_expert, 0)


def _local_rows(tr):
    return 2 * tr + N_EXPERTS * SEG_ALIGN


def _row_words(d):
    return d // 2 + LANES


def _dispatch_body(starts_ref, seg_ref, prev_seg_ref, h_ref, post_ref, xr_ref, xs_scr, zero_scr, sems, *, tr):
    i = pl.program_id(0)
    slot = i % 2
    lb = xs_scr.shape[1]
    sem = sems.at[0]

    def make_copy_of(s):
        def make_copy(local, glob, size):
            return pltpu.make_async_copy(xs_scr.at[s, pl.ds(local, size)], xr_ref.at[pl.ds(glob, size)], sems.at[s])
        return make_copy

    rows = lax.broadcasted_iota(I32, (lb, tr), 0)
    first = rows == post_ref[0, META_POS0:META_POS0 + 1, :].astype(I32)
    second = rows == post_ref[0, META_POS1:META_POS1 + 1, :].astype(I32)
    x = jnp.dot(jnp.where(first | second, 1.0, 0.0).astype(BF16), h_ref[...], preferred_element_type=F32)
    bits = lax.bitcast_convert_type(x, I32)
    half = x.shape[1] // 2
    xs_scr[slot, :, 0:half] = lax.shift_right_logical(bits[:, 0:half], 16) | (bits[:, half:] & jnp.int32(-65536))
    gate = jnp.sum(jnp.where(first, post_ref[0, META_GATE0:META_GATE0 + 1, :], 0.0)
                   + jnp.where(second, post_ref[0, META_GATE1:META_GATE1 + 1, :], 0.0), axis=1, keepdims=True)
    xs_scr[slot, :, half:half + LANES] = jnp.broadcast_to(lax.bitcast_convert_type(gate, I32), (lb, LANES))
    _segment_copies(seg_ref, starts_ref, make_copy_of(slot), "start")

    @pl.when(i > 0)
    def _():
        _segment_copies(prev_seg_ref, starts_ref, make_copy_of(1 - slot), "wait")

    @pl.when(i == pl.num_programs(0) - 1)
    def _():
        _segment_copies(seg_ref, starts_ref, make_copy_of(slot), "wait")
        zero_scr[...] = jnp.zeros_like(zero_scr)
        total = starts_ref[N_EXPERTS]
        tail = (-total) & (MOE_ROWS - 1)
        for action in ("start", "wait"):
            row = total
            for size in SEG_SIZES:
                if size >= MOE_ROWS:
                    continue
                hit = (tail & size) != 0

                @pl.when(hit)
                def _(row=row, size=size, action=action):
                    cp = pltpu.make_async_copy(zero_scr.at[pl.ds(0, size)],
                                               xr_ref.at[pl.ds(pl.multiple_of(row, SEG_ALIGN), size)], sem)
                    cp.start() if action == "start" else cp.wait()

                row = row + jnp.where(hit, size, 0)

        def free_block(b):
            return pltpu.make_async_copy(
                zero_scr, xr_ref.at[pl.ds(pl.multiple_of(b * MOE_ROWS, MOE_ROWS), MOE_ROWS)], sem)

        first_free = (total + MOE_ROWS - 1) // MOE_ROWS
        n_blocks = xr_ref.shape[0] // MOE_ROWS
        lax.fori_loop(first_free, n_blocks, lambda b, c: (free_block(b).start(), c)[1], 0)
        lax.fori_loop(first_free, n_blocks, lambda b, c: (free_block(b).wait(), c)[1], 0)


def _moe_dispatch(h2, post, seg, starts, n_rows, tr):
    n_tok, d = h2.shape
    grid_spec = pltpu.PrefetchScalarGridSpec(
        num_scalar_prefetch=1,
        grid=(n_tok // tr,),
        in_specs=[pl.BlockSpec((1, 8, LANES), lambda i, s: (i, 0, 0), memory_space=pltpu.SMEM),
                  pl.BlockSpec((1, 8, LANES), lambda i, s: (jnp.maximum(i - 1, 0), 0, 0), memory_space=pltpu.SMEM),
                  pl.BlockSpec((tr, d), lambda i, s: (i, 0)),
                  pl.BlockSpec((1, 8, tr), lambda i, s: (i, 0, 0))],
        out_specs=pl.BlockSpec(memory_space=pl.ANY),
        scratch_shapes=[pltpu.VMEM((2, _local_rows(tr), _row_words(d)), I32), pltpu.VMEM((MOE_ROWS, _row_words(d)), I32),
                        pltpu.SemaphoreType.DMA((2,))],
    )
    return pl.pallas_call(
        functools.partial(_dispatch_body, tr=tr),
        grid_spec=grid_spec,
        out_shape=jax.ShapeDtypeStruct((n_rows, _row_words(d)), I32),
        compiler_params=_cparams(("arbitrary",)),
        name="moe_dispatch",
    )(starts, seg, seg, h2, post)


def _expert_body(pb_ref, pe_ref, plo_ref, phi_ref, x_ref, wg_ref, wu_ref, wd_ref, o_ref, wg_s, wu_s, wd_s):
    p = pl.program_id(0)
    prev = jnp.maximum(p - 1, 0)
    new_expert = (p == 0) | (pe_ref[p] != pe_ref[prev])
    first = (p == 0) | (pb_ref[p] != pb_ref[prev])

    @pl.when(new_expert)
    def _():
        wg_s[...] = wg_ref[0, 0].astype(BF16)
        wu_s[...] = wu_ref[0, 0].astype(BF16)
        wd_s[...] = wd_ref[0, 0].astype(BF16)

    def rows_of_expert():
        half = wg_s.shape[0] // 2
        words = x_ref[:, 0:half]
        x_lo = lax.bitcast_convert_type(words << 16, F32).astype(BF16)
        x_hi = lax.bitcast_convert_type(words & jnp.int32(-65536), F32).astype(BF16)
        gate = lax.bitcast_convert_type(x_ref[:, half:half + 1], F32)
        gt = (jnp.dot(x_lo, wg_s[0:half], preferred_element_type=F32)
              + jnp.dot(x_hi, wg_s[half:], preferred_element_type=F32))
        up = (jnp.dot(x_lo, wu_s[0:half], preferred_element_type=F32)
              + jnp.dot(x_hi, wu_s[half:], preferred_element_type=F32))
        act = (gt * _sigmoid(gt)) * up
        y = jnp.dot(act.astype(BF16), wd_s[...], preferred_element_type=F32) * gate
        rows = lax.broadcasted_iota(I32, (y.shape[0], 1), 0)
        return jnp.where((rows >= plo_ref[p]) & (rows < phi_ref[p]), y, 0.0)

    nonempty = phi_ref[p] > plo_ref[p]

    @pl.when(first & nonempty)
    def _():
        o_ref[...] = rows_of_expert()

    @pl.when(first & jnp.logical_not(nonempty))
    def _():
        o_ref[...] = jnp.zeros_like(o_ref)

    @pl.when(jnp.logical_not(first) & nonempty)
    def _():
        o_ref[...] += rows_of_expert()


def _moe_experts(x_rows, pairs, w_gate, w_up, w_down, layer):
    n_rows = x_rows.shape[0]
    d, de = w_gate.shape[-2:]
    n_pairs = pairs[0].shape[0]
    grid_spec = pltpu.PrefetchScalarGridSpec(
        num_scalar_prefetch=4,
        grid=(n_pairs,),
        in_specs=[pl.BlockSpec((MOE_ROWS, _row_words(d)), lambda p, pb, pe, lo, hi: (pb[p], 0)),
                  pl.BlockSpec((1, 1, d, de), lambda p, pb, pe, lo, hi: (layer, pe[p], 0, 0)),
                  pl.BlockSpec((1, 1, d, de), lambda p, pb, pe, lo, hi: (layer, pe[p], 0, 0)),
                  pl.BlockSpec((1, 1, de, d), lambda p, pb, pe, lo, hi: (layer, pe[p], 0, 0))],
        out_specs=pl.BlockSpec((MOE_ROWS, d), lambda p, pb, pe, lo, hi: (pb[p], 0)),
        scratch_shapes=[pltpu.VMEM((d, de), BF16), pltpu.VMEM((d, de), BF16), pltpu.VMEM((de, d), BF16)],
    )
    return pl.pallas_call(
        _expert_body,
        grid_spec=grid_spec,
        out_shape=jax.ShapeDtypeStruct((n_rows, d), F32),
        compiler_params=_cparams(("arbitrary",)),
        name="moe_experts",
    )(*pairs, x_rows, w_gate, w_up, w_down)


def _combine_body(starts_ref, seg_ref, next_seg_ref, y_ref, x_ref, meta_ref, g2_ref, fg_ref, o_ref, ys_scr, sems, *,
                  final_norm):
    i = pl.program_id(0)
    slot = i % 2

    def make_copy_of(s):
        def make_copy(local, glob, size):
            return pltpu.make_async_copy(y_ref.at[pl.ds(glob, size)], ys_scr.at[s, pl.ds(local, size)], sems.at[s])
        return make_copy

    @pl.when(i == 0)
    def _():
        ys_scr[...] = jnp.zeros_like(ys_scr)
        _segment_copies(seg_ref, starts_ref, make_copy_of(slot), "start")

    @pl.when(i + 1 < pl.num_programs(0))
    def _():
        _segment_copies(next_seg_ref, starts_ref, make_copy_of(1 - slot), "start")

    _segment_copies(seg_ref, starts_ref, make_copy_of(slot), "wait")

    meta = meta_ref[...]
    tr = meta.shape[0]
    y = ys_scr[slot].astype(BF16)
    cols = lax.broadcasted_iota(I32, (tr, ys_scr.shape[1]), 1)

    pick = ((cols == meta[:, META_POS0:META_POS0 + 1].astype(I32))
            | (cols == meta[:, META_POS1:META_POS1 + 1].astype(I32)))
    moe = jnp.dot(jnp.where(pick, 1.0, 0.0).astype(BF16), y, preferred_element_type=F32)
    xn = x_ref[...] + g2_ref[0] * moe
    if final_norm:
        xn = _rms(xn) * fg_ref[...]
    o_ref[...] = xn


def _moe_combine(y_rows, seg, starts, x2, meta, g2, final_g, seq, tr, final_norm):
    n_tok, d = x2.shape
    bsz = n_tok // seq
    steps_per_batch = seq // tr
    nb = n_tok // tr
    grid_spec = pltpu.PrefetchScalarGridSpec(
        num_scalar_prefetch=1,
        grid=(nb,),
        in_specs=[pl.BlockSpec((1, 8, LANES), lambda i, s: (i, 0, 0), memory_space=pltpu.SMEM),
                  pl.BlockSpec((1, 8, LANES), lambda i, s: (jnp.minimum(i + 1, nb - 1), 0, 0), memory_space=pltpu.SMEM),
                  pl.BlockSpec(memory_space=pl.ANY),
                  pl.BlockSpec((tr, d), lambda i, s: (i, 0)),
                  pl.BlockSpec((tr, LANES), lambda i, s: (i, 0)),
                  pl.BlockSpec((1, 1, d), lambda i, s: (i // steps_per_batch, 0, 0)),
                  pl.BlockSpec((1, d), lambda i, s: (0, 0))],
        out_specs=pl.BlockSpec((tr, d), lambda i, s: (i, 0)),
        scratch_shapes=[pltpu.VMEM((2, _local_rows(tr), d), F32), pltpu.SemaphoreType.DMA((2,))],
    )
    return pl.pallas_call(
        functools.partial(_combine_body, final_norm=final_norm),
        grid_spec=grid_spec,
        out_shape=jax.ShapeDtypeStruct((n_tok, d), F32),
        compiler_params=_cparams(("arbitrary",)),
        name="moe_combine",
    )(starts, seg, seg, y_rows, x2, meta, g2.reshape(bsz, 1, d), final_g.reshape(1, d))


def _expert_pairs(counts, n_rows):
    n_blocks = n_rows // MOE_ROWS
    n_pairs = n_blocks + N_EXPERTS
    ends = jnp.cumsum(counts)
    starts = ends - counts
    first_blk = starts // MOE_ROWS
    last_blk = (ends - 1) // MOE_ROWS
    npairs = jnp.where(counts > 0, last_blk - first_blk + 1, 0)
    pend = jnp.cumsum(npairs)
    poff = pend - npairs
    total = pend[-1]
    used_blocks = (ends[-1] + MOE_ROWS - 1) // MOE_ROWS
    p = jnp.arange(n_pairs, dtype=I32)
    p_eff = jnp.minimum(p, total - 1)
    e = jnp.minimum(jnp.sum(pend[None, :] <= p_eff[:, None], axis=1), N_EXPERTS - 1).astype(I32)
    table = jnp.stack([first_blk, poff, starts, ends], axis=1).astype(F32)
    onehot = (e[:, None] == jnp.arange(N_EXPERTS, dtype=I32)[None, :]).astype(F32)
    first_e, poff_e, start_e, end_e = jnp.dot(onehot, table, precision=HIGHEST).astype(I32).T
    valid = p < total
    blk = jnp.where(valid, first_e + p_eff - poff_e, jnp.minimum(used_blocks + p - total, n_blocks - 1)).astype(I32)
    lo = jnp.where(valid, jnp.clip(start_e - blk * MOE_ROWS, 0, MOE_ROWS), 0).astype(I32)
    hi = jnp.where(valid, jnp.clip(end_e - blk * MOE_ROWS, 0, MOE_ROWS), 0).astype(I32)
    return blk, e, lo, hi


def _hier_moe(x, g, sh, sc, g2, wr_g, br_g, wr_e, br_e, w_gate, w_up, w_down, layer, final_g, final_norm):
    bsz, seq, d = x.shape
    n_tok = bsz * seq
    tr = min(ROUTER_ROWS, seq)
    assert 2 * tr <= SEG_SIZES[0]
    x2 = x.reshape(n_tok, d)
    h2, meta, post, seg, cnt = _moe_router(x2, g, sh, sc, wr_g, br_g, wr_e, br_e, seq)
    counts = cnt[0, :N_EXPERTS].astype(I32)
    ends = jnp.cumsum(counts)
    starts = jnp.concatenate([ends - counts, ends[-1:]])
    n_rows = -(-(2 * n_tok + (n_tok // tr) * N_EXPERTS * (SEG_ALIGN - 1)) // MOE_ROWS) * MOE_ROWS
    x_rows = _moe_dispatch(h2, post, seg, starts, n_rows, tr)
    y_rows = _moe_experts(x_rows, _expert_pairs(counts, n_rows), w_gate, w_up, w_down, layer)
    out = _moe_combine(y_rows, seg, starts, x2, meta, g2, final_g, seq, tr, final_norm)
    return out.reshape(bsz, seq, d)


SEL_COLS = 512
SEL_CHUNK = 512
BIT_GROUP = 256
SEL_SWEEP = 32
ATT_TILE = 1024


def _idx_score_t(k, qt, wt):
    acc = None
    for h in range(IDX_HEADS):
        rel = jnp.dot(k, qt[h * IDX_DIM:(h + 1) * IDX_DIM], preferred_element_type=F32)
        term = jnp.maximum(rel, 0.0) * wt[h:h + 1, :]
        acc = term if acc is None else acc + term
    return acc


def _bit_transpose32(a):
    a = list(a)
    m, j = 0x0000FFFF, 16
    while j:
        k = 0
        while k < 32:
            t = (a[k] ^ lax.shift_right_logical(a[k + j], jnp.int32(j))) & jnp.int32(m)
            a[k] = a[k] ^ t
            a[k + j] = a[k + j] ^ (t << j)
            k = (k + j + 1) & ~j
        j >>= 1
        m = (m ^ (m << j)) & 0xFFFFFFFF
    return a


def _select_body(qt_ref, zk_ref, wt_ref, o_ref, planes_scr, eq_scr, gt_scr, *, topk):
    i = pl.program_id(1)
    tq = qt_ref.shape[2]
    qt = qt_ref[0]
    wt = wt_ref[0].astype(F32)
    n_chunks = ((i + 1) * tq + SEL_CHUNK - 1) // SEL_CHUNK
    krow0 = lax.broadcasted_iota(I32, (SEL_CHUNK, tq), 0)
    qcol = i * tq + lax.broadcasted_iota(I32, (SEL_CHUNK, tq), 1)
    int_min = jnp.int32(INT_MIN)

    groups = SEL_CHUNK // BIT_GROUP
    words = SEL_CHUNK // 32
    sweep_rows = min(SEL_SWEEP, eq_scr.shape[0])

    def fill(masked, c, carry):
        off = pl.multiple_of(c * SEL_CHUNK, SEL_CHUNK)
        k = zk_ref[0, pl.ds(off, SEL_CHUNK), :][:, 0:IDX_DIM]
        bits = lax.bitcast_convert_type(_idx_score_t(k, qt, wt), I32)
        key = bits ^ ((bits >> 31) | int_min)
        if masked:
            key = jnp.where(krow0 + off <= qcol, key, 0)
        for g in range(groups):
            for lt in range(0, tq, LANES):
                ku = key[g * BIT_GROUP:(g + 1) * BIT_GROUP, lt:lt + LANES]
                planes = _bit_transpose32([ku[8 * r:8 * r + 8] for r in range(32)])
                wrow = pl.multiple_of(c * words + g * 8, 8)
                for b in range(32):
                    planes_scr[b, pl.ds(wrow, 8), lt:lt + LANES] = planes[b]
        return carry

    n_plain = (i * tq + 1) // SEL_CHUNK
    lax.fori_loop(0, n_plain, functools.partial(fill, False), 0)
    lax.fori_loop(n_plain, n_chunks, functools.partial(fill, True), 0)

    n_sweep = (n_chunks * words + sweep_rows - 1) // sweep_rows

    def pad(c, carry):
        wrow = pl.multiple_of(c * words, words)
        for b in range(32):
            planes_scr[b, pl.ds(wrow, words), :] = jnp.zeros((words, tq), I32)
        return carry

    lax.fori_loop(n_chunks, n_sweep * (sweep_rows // words), pad, 0)

    def sweep(upd, cnt_plane):
        def body(sb, acc):
            r0 = pl.multiple_of(sb * sweep_rows, sweep_rows)
            eq = eq_scr[pl.ds(r0, sweep_rows), :]
            gt = gt_scr[pl.ds(r0, sweep_rows), :]
            if upd is not None:
                plane, accept = upd
                hit = eq & planes_scr[plane, pl.ds(r0, sweep_rows), :]
                gt = jnp.where(accept, gt, gt | hit)
                eq = jnp.where(accept, hit, eq ^ hit)
                eq_scr[pl.ds(r0, sweep_rows), :] = eq
                gt_scr[pl.ds(r0, sweep_rows), :] = gt
            if cnt_plane is None:
                return acc
            ones = lax.population_count(gt | (eq & planes_scr[cnt_plane, pl.ds(r0, sweep_rows), :]))
            return acc + jnp.sum(ones.reshape(sweep_rows // 8, 8, tq), axis=0)
        acc = lax.fori_loop(0, n_sweep, body, jnp.zeros((8, tq), I32))
        return jnp.sum(acc.astype(F32), axis=0, keepdims=True)

    k_f = jnp.float32(topk)
    eq_scr[...] = jnp.full(eq_scr.shape, -1, I32)
    gt_scr[...] = jnp.zeros(gt_scr.shape, I32)

    def bit_step(ib, carry):
        u, cnt = carry
        accept = cnt >= k_f
        u = jnp.where(accept, u | jnp.left_shift(jnp.int32(1), 32 - ib), u)
        return u, sweep((ib - 1, accept), ib)

    u, cnt = lax.fori_loop(1, 32, bit_step, (jnp.zeros((1, tq), I32), sweep(None, 0)))
    accept = cnt >= k_f
    u = jnp.where(accept, u | 1, u)
    sweep((31, accept), None)
    some = jnp.where(u != 0, -1, 0)

    def popcount_rows(word_of):
        def body(sb, acc):
            r0 = pl.multiple_of(sb * sweep_rows, sweep_rows)
            ones = lax.population_count(word_of(r0))
            return acc + jnp.sum(ones.reshape(sweep_rows // 8, 8, tq), axis=0)
        acc = lax.fori_loop(0, n_sweep, body, jnp.zeros((8, tq), I32))
        return jnp.sum(acc.astype(F32), axis=0, keepdims=True)

    n_gt = popcount_rows(lambda r0: gt_scr[pl.ds(r0, sweep_rows), :])
    n_eq = jnp.where(u != 0, popcount_rows(lambda r0: eq_scr[pl.ds(r0, sweep_rows), :]), 0.0)

    @pl.when(jnp.max(n_gt + n_eq) > k_f)
    def _():
        wrow = lax.broadcasted_iota(I32, (sweep_rows, tq), 0)

        def below(cap, r0):
            w = wrow + r0
            last = lax.shift_right_arithmetic(cap - (w >> 3) * BIT_GROUP - (w & 7) - 1, 3)
            clear = jnp.clip(31 - last, 0, 32)
            return jnp.where(clear >= 32, 0, jnp.left_shift(jnp.int32(-1), jnp.minimum(clear, 31)))

        need = k_f - n_gt

        def cap_step(ib, v):
            cand = v | jnp.left_shift(jnp.int32(1), 14 - ib)
            cnt = popcount_rows(lambda r0: eq_scr[pl.ds(r0, sweep_rows), :] & below(cand, r0))
            return jnp.where(cnt <= need, cand, v)

        cap = lax.fori_loop(0, 15, cap_step, jnp.zeros((1, tq), I32))
        cap = jnp.where((u != 0) & (n_gt + n_eq > k_f), cap, 2 ** 30)

        def trim(sb, carry):
            r0 = pl.multiple_of(sb * sweep_rows, sweep_rows)
            eq_scr[pl.ds(r0, sweep_rows), :] = eq_scr[pl.ds(r0, sweep_rows), :] & below(cap, r0)
            return carry

        lax.fori_loop(0, n_sweep, trim, 0)

    def emit(c, carry):
        off = pl.multiple_of(c * SEL_CHUNK, SEL_CHUNK)
        for g in range(groups):
            wrow = pl.multiple_of(c * words + g * 8, 8)
            keep = gt_scr[pl.ds(wrow, 8), :] | (eq_scr[pl.ds(wrow, 8), :] & some)
            rows = [jnp.where((lax.shift_right_logical(keep, 31 - r) & 1) != 0, 0.0, -jnp.inf) for r in range(32)]
            o_ref[0, pl.ds(pl.multiple_of(off + g * BIT_GROUP, BIT_GROUP), BIT_GROUP), :] = (
                jnp.concatenate(rows, axis=0).astype(o_ref.dtype))
        return carry

    lax.fori_loop(0, n_chunks, emit, 0)

    def blank(c, carry):
        off = pl.multiple_of(c * SEL_CHUNK, SEL_CHUNK)
        o_ref[0, pl.ds(off, SEL_CHUNK), :] = jnp.full((SEL_CHUNK, tq), -jnp.inf, o_ref.dtype)
        return carry

    lax.fori_loop(n_chunks, o_ref.shape[1] // SEL_CHUNK, blank, 0)


def _select(z, zt, topk):
    bsz, seq, _ = z.shape
    tq = min(SEL_COLS, seq)
    assert seq % SEL_CHUNK == 0 and seq % tq == 0
    return pl.pallas_call(
        functools.partial(_select_body, topk=topk),
        grid=(bsz, seq // tq),
        in_specs=[pl.BlockSpec((1, ZT_QIDX_ROWS, tq), lambda b, i: (b, ZT_QIDX_OFF // ZT_QIDX_ROWS, i)),
                  pl.BlockSpec((1, seq, LANES), lambda b, i: (b, 0, Z_KIDX_OFF // LANES)),
                  pl.BlockSpec((1, BF16_ROWS, tq), lambda b, i: (b, ZT_W_OFF // BF16_ROWS, i))],
        out_specs=pl.BlockSpec((1, seq, tq), lambda b, i: (b, 0, i)),
        out_shape=jax.ShapeDtypeStruct((bsz, seq, seq), BF16),
        scratch_shapes=[pltpu.VMEM((32, seq // 32, tq), I32),
                        pltpu.VMEM((seq // 32, tq), I32), pltpu.VMEM((seq // 32, tq), I32)],
        compiler_params=_cparams(("arbitrary", "arbitrary")),
        name="dsa_select",
    )(zt, z, zt)


BIAS_INIT_ROWS = 8
ONES_ROWS = BF16_ROWS


BAND = 128


def _init_bias_blocks(btile, bias_ref, head0, n_heads):
    ta = BAND
    col = lax.broadcasted_iota(I32, (BIAS_INIT_ROWS, ta), 1)
    row0 = lax.broadcasted_iota(I32, (BIAS_INIT_ROWS, ta), 0)

    def body(r, carry):
        off = pl.multiple_of(r * BIAS_INIT_ROWS, BIAS_INIT_ROWS)
        for kind in range(2):
            dist = col - (row0 + off) + kind * ta
            for h in range(n_heads):
                far = bias_ref[NUM_BUCKETS - 1, head0 + h]
                val = jnp.full((BIAS_INIT_ROWS, ta), (bias_ref[0, head0 + h] - far) * LOG2E, F32)
                for b in range(1, NUM_BUCKETS - 1):
                    val = jnp.where(dist >= BUCKET_START[b], (bias_ref[b, head0 + h] - far) * LOG2E, val)
                val = jnp.where(dist >= BUCKET_START[NUM_BUCKETS - 1], 0.0, val)
                btile[h, kind, pl.ds(off, BIAS_INIT_ROWS), :] = val
        return carry

    lax.fori_loop(0, ta // BIAS_INIT_ROWS, body, 0)


def _add_bias_band(s_scr, btile, h, kind, ta, key_blocks=None):
    nb = ta // BAND
    blk = lambda i: slice(i * BAND, (i + 1) * BAND)
    if kind == 0:
        for kb in range(*(key_blocks or (0, nb))):
            s_scr[blk(kb), blk(kb)] += btile[h, 0]
            if kb + 1 < nb:
                s_scr[blk(kb), blk(kb + 1)] += btile[h, 1]
    else:
        s_scr[blk(nb - 1), blk(0)] += btile[h, 1]


def _with_ones(vt):
    return jnp.concatenate([vt, jnp.ones((ONES_ROWS, vt.shape[1]), vt.dtype)], axis=0)


def _softmax_step_t(logits, v_aug, m_ref, acc_ref, idx, cols=slice(None)):
    m_old = m_ref[idx, :, cols]
    m_new = jnp.maximum(m_old, jnp.max(logits, axis=0, keepdims=True))
    m_safe = jnp.where(m_new == -jnp.inf, 0.0, m_new)
    p = jnp.exp2(logits - m_safe)
    alpha = jnp.exp2(m_old - m_safe)
    acc_ref[idx, :, cols] = alpha * acc_ref[idx, :, cols] + jnp.dot(v_aug, p.astype(BF16), preferred_element_type=F32)
    m_ref[idx, :, cols] = m_new


def _diagonal_parts(ta):
    half = ta // 2
    nb = ta // BAND
    return ((slice(0, half), slice(0, ta), (0, nb // 2)), (slice(half, ta), slice(half, ta), (nb // 2, nb)))


def _reset_softmax(m_s, acc_s):
    m_s[...] = jnp.full(m_s.shape, -jnp.inf, F32)
    acc_s[...] = jnp.zeros(acc_s.shape, F32)


def _cattn_body(qi_ref, ki_ref, bias_ref, qt_ref, zk_ref, vt_ref, mask_ref, o_ref, m_s, acc_s, btile, s_scr):
    b = pl.program_id(0)
    p = pl.program_id(1)
    qi = qi_ref[p]
    ki = ki_ref[p]
    ta = zk_ref.shape[1]
    hd = C_HEAD_DIM

    @pl.when((b == 0) & (p == 0))
    def _():
        _init_bias_blocks(btile, bias_ref, 0, C_HEADS)

    @pl.when(ki == 0)
    def _():
        _reset_softmax(m_s, acc_s)

    def heads(kind):
        for h in range(C_HEADS):
            lo = h * hd
            if kind == 0:
                v_aug = _with_ones(vt_ref[0, lo:lo + hd, :])
                for rows, cols, key_blocks in _diagonal_parts(ta):
                    s_scr[rows, cols] = (jnp.dot(zk_ref[0, rows, lo:lo + hd], qt_ref[0, lo:lo + hd, cols],
                                                 preferred_element_type=F32) + mask_ref[0, rows, cols].astype(F32))
                    _add_bias_band(s_scr, btile, h, kind, ta, key_blocks)
                    _softmax_step_t(s_scr[rows, cols], v_aug[:, rows], m_s, acc_s, h, cols)
                continue
            logits = (jnp.dot(zk_ref[0, :, lo:lo + hd], qt_ref[0, lo:lo + hd, :], preferred_element_type=F32)
                      + mask_ref[0].astype(F32))
            if kind is not None:
                s_scr[...] = logits
                _add_bias_band(s_scr, btile, h, kind, ta)
                logits = s_scr[...]
            _softmax_step_t(logits, _with_ones(vt_ref[0, lo:lo + hd, :]), m_s, acc_s, h)

    for kind in (0, 1):
        pl.when(ki == qi - kind)(functools.partial(heads, kind))
    pl.when(ki < qi - 1)(functools.partial(heads, None))

    @pl.when(ki == qi)
    def _():
        outs = []
        for h in range(C_HEADS):
            a = acc_s[h]
            outs.append((a[0:hd] / a[hd:hd + 1]).T)
        o_ref[0] = jnp.concatenate(outs, axis=1).astype(o_ref.dtype)


def _dattn_body(qi_ref, ki_ref, bias_ref, qt_ref, zk_ref, vt_ref, lam_ref, ng_ref, o_ref,
                m_s, acc_s, btile, s_scr, causal_s, *, lambda_init):
    b = pl.program_id(0)
    p = pl.program_id(1)
    qi = qi_ref[p]
    ki = ki_ref[p]
    ta = zk_ref.shape[1]
    hd = DIFF_HEAD_DIM
    dv = 2 * hd

    @pl.when((b == 0) & (p == 0))
    def _():
        _init_bias_blocks(btile, bias_ref, C_HEADS, DIFF_HEADS)
        krow = lax.broadcasted_iota(I32, (ta, ta), 0)
        qcol = lax.broadcasted_iota(I32, (ta, ta), 1)
        causal_s[...] = jnp.where(krow <= qcol, 0.0, -jnp.inf)

    @pl.when(ki == 0)
    def _():
        _reset_softmax(m_s, acc_s)

    def heads(kind):
        for h in range(DIFF_HEADS):
            v_aug = _with_ones(vt_ref[0, dv * h:dv * (h + 1), :])
            for j in range(2):
                lo = (2 * h + j) * hd
                if kind == 0:
                    for rows, cols, key_blocks in _diagonal_parts(ta):
                        s_scr[rows, cols] = (jnp.dot(zk_ref[0, rows, lo:lo + hd], qt_ref[0, lo:lo + hd, cols],
                                                     preferred_element_type=F32) + causal_s[rows, cols])
                        _add_bias_band(s_scr, btile, h, kind, ta, key_blocks)
                        _softmax_step_t(s_scr[rows, cols], v_aug[:, rows], m_s, acc_s, 2 * h + j, cols)
                    continue
                logits = jnp.dot(zk_ref[0, :, lo:lo + hd], qt_ref[0, lo:lo + hd, :], preferred_element_type=F32)
                if kind is not None:
                    s_scr[...] = logits + causal_s[...] if kind == 0 else logits
                    _add_bias_band(s_scr, btile, h, kind, ta)
                    logits = s_scr[...]
                _softmax_step_t(logits, v_aug, m_s, acc_s, 2 * h + j)

    for kind in (0, 1):
        pl.when(ki == qi - kind)(functools.partial(heads, kind))
    pl.when(ki < qi - 1)(functools.partial(heads, None))

    @pl.when(ki == qi)
    def _():
        lam_p = lam_ref[...]
        lam = (jnp.exp(jnp.sum(lam_p[0:1] * lam_p[1:2], axis=1, keepdims=True))
               - jnp.exp(jnp.sum(lam_p[2:3] * lam_p[3:4], axis=1, keepdims=True)) + lambda_init)
        outs = []
        for h in range(DIFF_HEADS):
            a1 = acc_s[2 * h]
            a2 = acc_s[2 * h + 1]
            o = a1[0:dv] / a1[dv:dv + 1] - lam * (a2[0:dv] / a2[dv:dv + 1])
            o = o * lax.rsqrt(jnp.mean(o * o, axis=0, keepdims=True) + EPS) * ng_ref[...] * (1.0 - lambda_init)
            outs.append(o.T)
        o_ref[0] = jnp.concatenate(outs, axis=1).astype(o_ref.dtype)


def _causal_pairs(nq):
    qi = [q for q in range(nq) for _ in range(q + 1)]
    ki = [k for q in range(nq) for k in range(q + 1)]
    return jnp.asarray(qi, I32), jnp.asarray(ki, I32)


def _cattn(z, zt, mask, rel_bias, ta):
    bsz, seq, _ = z.shape
    qi, ki = _causal_pairs(seq // ta)
    qmap = lambda col: (lambda b, p, qi, ki: (b, qi[p], col))
    kmap = lambda col: (lambda b, p, qi, ki: (b, ki[p], col))
    dv_aug = C_HEAD_DIM + ONES_ROWS
    grid_spec = pltpu.PrefetchScalarGridSpec(
        num_scalar_prefetch=2,
        grid=(bsz, qi.shape[0]),
        in_specs=[pl.BlockSpec(memory_space=pltpu.SMEM),
                  pl.BlockSpec((1, C_WIDTH, ta), lambda b, p, qi, ki: (b, ZT_QC, qi[p])),
                  pl.BlockSpec((1, ta, C_WIDTH), kmap(Z_KC)),
                  pl.BlockSpec((1, C_WIDTH, ta), lambda b, p, qi, ki: (b, ZT_VC, ki[p])),
                  pl.BlockSpec((1, ta, ta), lambda b, p, qi, ki: (b, ki[p], qi[p]))],
        out_specs=pl.BlockSpec((1, ta, C_WIDTH), qmap(0)),
        scratch_shapes=[pltpu.VMEM((C_HEADS, 1, ta), F32),
                        pltpu.VMEM((C_HEADS, dv_aug, ta), F32),
                        pltpu.VMEM((C_HEADS, 2, BAND, BAND), F32),
                        pltpu.VMEM((ta, ta), F32)],
    )
    return pl.pallas_call(
        _cattn_body,
        grid_spec=grid_spec,
        out_shape=jax.ShapeDtypeStruct((bsz, seq, C_WIDTH), BF16),
        compiler_params=_cparams(("arbitrary", "arbitrary")),
        name="dsa_attn",
    )(qi, ki, rel_bias, zt, z, zt, mask)


def _dattn(z, zt, rel_bias, diff_lam, diff_norm_g, lambda_init, ta):
    bsz, seq, _ = z.shape
    qi, ki = _causal_pairs(seq // ta)
    n_maps = 2 * DIFF_HEADS
    dv = 2 * DIFF_HEAD_DIM
    qmap = lambda col: (lambda b, p, qi, ki: (b, qi[p], col))
    kmap = lambda col: (lambda b, p, qi, ki: (b, ki[p], col))
    grid_spec = pltpu.PrefetchScalarGridSpec(
        num_scalar_prefetch=2,
        grid=(bsz, qi.shape[0]),
        in_specs=[pl.BlockSpec(memory_space=pltpu.SMEM),
                  pl.BlockSpec((1, DIFF_W, ta), lambda b, p, qi, ki: (b, ZT_QD, qi[p])),
                  pl.BlockSpec((1, ta, DIFF_W), kmap(Z_KD)),
                  pl.BlockSpec((1, DIFF_W, ta), lambda b, p, qi, ki: (b, ZT_VD, ki[p])),
                  pl.BlockSpec(diff_lam.shape, lambda b, p, qi, ki: (0, 0)),
                  pl.BlockSpec((dv, 1), lambda b, p, qi, ki: (0, 0))],
        out_specs=pl.BlockSpec((1, ta, DIFF_W), qmap(0)),
        scratch_shapes=[pltpu.VMEM((n_maps, 1, ta), F32),
                        pltpu.VMEM((n_maps, dv + ONES_ROWS, ta), F32),
                        pltpu.VMEM((DIFF_HEADS, 2, BAND, BAND), F32),
                        pltpu.VMEM((ta, ta), F32), pltpu.VMEM((ta, ta), F32)],
    )
    return pl.pallas_call(
        functools.partial(_dattn_body, lambda_init=lambda_init),
        grid_spec=grid_spec,
        out_shape=jax.ShapeDtypeStruct((bsz, seq, DIFF_W), BF16),
        compiler_params=_cparams(("arbitrary", "arbitrary")),
        name="diff_attn",
    )(qi, ki, rel_bias, zt, z, zt, diff_lam, diff_norm_g.reshape(dv, 1))


def _out_proj_body(oc_ref, od_ref, x_ref, g1_ref, w_ref, o_ref):
    y = (jnp.dot(oc_ref[0], w_ref[0:C_WIDTH, :], preferred_element_type=F32)
         + jnp.dot(od_ref[0], w_ref[C_WIDTH:C_WIDTH + DIFF_W, :], preferred_element_type=F32))
    o_ref[0] = x_ref[0] + g1_ref[0] * y


def _out_proj(out_c, out_d, x, g1, w_out_bf16, tm):
    bsz, seq, d = x.shape
    return pl.pallas_call(
        _out_proj_body,
        grid=(bsz, seq // tm),
        in_specs=[pl.BlockSpec((1, tm, C_WIDTH), lambda b, i: (b, i, 0)),
                  pl.BlockSpec((1, tm, DIFF_W), lambda b, i: (b, i, 0)),
                  pl.BlockSpec((1, tm, d), lambda b, i: (b, i, 0)),
                  pl.BlockSpec((1, 1, d), lambda b, i: (b, 0, 0)),
                  pl.BlockSpec((C_WIDTH + DIFF_W, d), lambda b, i: (0, 0))],
        out_specs=pl.BlockSpec((1, tm, d), lambda b, i: (b, i, 0)),
        out_shape=jax.ShapeDtypeStruct((bsz, seq, d), F32),
        compiler_params=_cparams(("arbitrary", "arbitrary")),
        name="attn_out_proj",
    )(out_c, out_d, x, g1.reshape(bsz, 1, d), w_out_bf16)


def _attn_in_weights(cd_w_in):
    sizes = (C_WIDTH, C_WIDTH, C_WIDTH, IDX_HEADS * IDX_DIM, IDX_DIM, IDX_HEADS, DIFF_W, DIFF_W, DIFF_W)
    cuts = np.cumsum(sizes)[:-1]
    q_c, k_c, v_c, q_i, k_i, w_i, q_d, k_d, v_d = jnp.split(cd_w_in, cuts, axis=1)
    d = cd_w_in.shape[0]
    w = jnp.concatenate([k_c, k_d, k_i, jnp.zeros((d, LANES - IDX_DIM), cd_w_in.dtype)], axis=1)
    wt = jnp.concatenate([v_c, v_d, q_c * (C_HEAD_DIM ** -0.5 * LOG2E), q_d * (DIFF_HEAD_DIM ** -0.5 * LOG2E), q_i,
                          w_i * (IDX_DIM * IDX_HEADS) ** -0.5,
                          jnp.zeros((d, BF16_ROWS - IDX_HEADS), cd_w_in.dtype)], axis=1).T
    return w.astype(BF16), wt.astype(BF16)


def kernel(x, c, positions, rel_bias, norm_g, final_norm_g, ada_w, ada_b, ab_w_in, ab_conv_a, ab_conv_b,
           ab_conv_b_bias, ab_ln_g, ab_ln_b, ab_w_out, cd_w_in, diff_lam, diff_norm_g, cd_w_out,
           moe_wr_g, moe_br_g, moe_wr_e, moe_br_e, moe_w_gate, moe_w_up, moe_w_down):
    del positions
    bsz, seq, d = x.shape
    depth = ada_w.shape[0]
    tm = min(512, seq)
    ta = min(ATT_TILE, seq)
    assert ta % BAND == 0 and BUCKET_START[-1] < BAND and seq % ta == 0
    topk = min(TOPK_MAX, seq // 4)
    mods = _ada_mod(c, ada_w, ada_b)
    for i in range(depth):
        sh1, sc1, g1, sh2, sc2, g2 = jnp.split(mods[i], 6, axis=-1)
        j = i // 2
        if i % 2 == 0:
            z = _norm_proj(x, norm_g[i, 0], sh1, sc1, ab_w_in[j].astype(BF16), tm)
            x = _conv_mix(z, x, g1, ab_conv_a[j], ab_conv_b[j], ab_conv_b_bias[j], ab_ln_g[j], ab_ln_b[j],
                          ab_w_out[j].astype(BF16), min(512, seq))
        else:
            lambda_init = 0.8 - 0.6 * math.exp(-0.3 * i)
            w, wt = _attn_in_weights(cd_w_in[j])
            z, zt = _norm_proj(x, norm_g[i, 0], sh1, sc1, w, tm, wt)
            mask = _select(z, zt, topk)
            out_c = _cattn(z, zt, mask, rel_bias, ta)
            out_d = _dattn(z, zt, rel_bias, diff_lam[j], diff_norm_g[j], lambda_init, ta)
            x = _out_proj(out_c, out_d, x, g1, cd_w_out[j].astype(BF16), tm)
        x = _hier_moe(x, norm_g[i, 1], sh2, sc2, g2, moe_wr_g[i], moe_br_g[i], moe_wr_e[i], moe_br_e[i],
                      moe_w_gate, moe_w_up, moe_w_down, i, final_norm_g, final_norm=(i == depth - 1))
    return x
```
